```python
import jax, jax.numpy as jnp
from jax import lax
import numpy as np

D_MODEL = 1024
BATCH = 8
SEQ = 8192
DEPTH = 2

CHUNK = 64
POOL_WIDTH = D_MODEL // 2
POOL_WINDOWS = (2, 4, 8, 16)
POOL_GROUPS = len(POOL_WINDOWS)
POOL_GROUP_DIM = POOL_WIDTH // POOL_GROUPS
HGRN_HEAD_DIM = 128
HGRN_HEADS = (D_MODEL // 2) // HGRN_HEAD_DIM
HGRN_WIDTH = HGRN_HEADS * HGRN_HEAD_DIM
N_BRANCHES = 2
IN_COLS = POOL_WIDTH + 4 * HGRN_WIDTH + N_BRANCHES * D_MODEL
D_FF = 128 * ((8 * D_MODEL // 3 + 127) // 128)
CONV_WIDTH = 3
EPS = 1e-6

kernel_name = "hybrid_pool_hgrn2_convglu_trunk"


def rmsnorm(x, g):
    xf = x.astype(jnp.float32)
    y = xf * lax.rsqrt(jnp.mean(xf * xf, axis=-1, keepdims=True) + EPS) * g.astype(jnp.float32)
    return y.astype(x.dtype)


def pool_mixer(u, pool_w, pool_scale):
    B, S, _ = u.shape
    ug = u.reshape(B, S, POOL_GROUPS, POOL_GROUP_DIM).astype(jnp.float32)
    cs = jnp.cumsum(ug, axis=1)
    t = jnp.arange(1, S + 1, dtype=jnp.float32)
    outs = []
    for g, w in enumerate(POOL_WINDOWS):
        c_pad = jnp.pad(cs[:, :, g], ((0, 0), (w, 0), (0, 0)))
        win_sum = c_pad[:, w:] - c_pad[:, :S]
        count = jnp.minimum(t, float(w))[None, :, None]
        outs.append(win_sum / count - ug[:, :, g])
    pooled = jnp.stack(outs, axis=2).astype(u.dtype)
    mixed = jnp.einsum('bsgc,gcd->bsgd', pooled, pool_w).reshape(B, S, POOL_WIDTH)
    return mixed * pool_scale


def hgrn2_mixer(zq, zf, zi, zo, lb, norm_g):
    B, S, _ = zq.shape
    n_chunks = S // CHUNK
    H, Dh = HGRN_HEADS, HGRN_HEAD_DIM

    def heads(a):
        return a.reshape(B, S, H, Dh).astype(jnp.float32)

    q = jax.nn.silu(heads(zq))
    lbh = lb.reshape(H, Dh).astype(jnp.float32)
    f = lbh + (1.0 - lbh) * jax.nn.sigmoid(heads(zf))
    log_f = jnp.log(f)
    k = 1.0 - f
    v = heads(zi)

    def to_chunks(a):
        return a.reshape(B, n_chunks, CHUNK, H, Dh).transpose(1, 0, 3, 2, 4)

    causal = jnp.tril(jnp.ones((CHUNK, CHUNK), dtype=bool))

    def step(state, xs):
        qc, kc, vc, gc = xs
        b = jnp.cumsum(gc, axis=2)
        diff = b[:, :, :, None, :] - b[:, :, None, :, :]
        decay = jnp.exp(jnp.where(causal[:, :, None], diff, -jnp.inf))
        attn = jnp.einsum('bhtk,bhtsk,bhsk->bhts', qc, decay, kc)
        o = (jnp.einsum('bhts,bhsv->bhtv', attn, vc)
             + jnp.einsum('bhtk,bhkv->bhtv', qc * jnp.exp(b), state))
        b_last = b[:, :, -1:, :]
        new_state = (jnp.exp(b_last[:, :, 0, :])[..., None] * state
                     + jnp.einsum('bhsk,bhsv->bhkv', kc * jnp.exp(b_last - b), vc))
        return new_state, o

    s0 = jnp.zeros((B, H, Dh, Dh), jnp.float32)
    _, o = lax.scan(step, s0, (to_chunks(q), to_chunks(k), to_chunks(v), to_chunks(log_f)))
    o = o.transpose(1, 0, 3, 2, 4).reshape(B, S, H, Dh)
    o = o * lax.rsqrt(jnp.mean(o * o, axis=-1, keepdims=True) + EPS) * norm_g.astype(jnp.float32)
    o = o * jax.nn.silu(heads(zo))
    return o.reshape(B, S, HGRN_WIDTH).astype(zq.dtype)


def conv_glu_ffn(x, w_up, conv_w, conv_b, w_down):
    S = x.shape[1]
    h = x @ w_up
    hp = jnp.pad(h, ((0, 0), (CONV_WIDTH - 1, 0), (0, 0)))
    hc = sum(conv_w[j] * hp[:, j:j + S] for j in range(CONV_WIDTH)) + conv_b
    val, gate = jnp.split(hc, 2, axis=-1)
    return (jax.nn.silu(gate) * val) @ w_down


def _fwd_setup_inputs(seed: int = 0) -> dict:
    key = jax.random.key(seed)
    ks = jax.random.split(key, 20)
    f32 = jnp.float32
    nrm = lambda k, shape, scale: (jax.random.normal(k, shape, f32) * scale).astype(f32)
    L = DEPTH
    return {
        "x": nrm(ks[0], (BATCH, SEQ, D_MODEL), 1.0),
        "norm1_g": 1.0 + nrm(ks[1], (L, D_MODEL), 0.02),
        "w_in": nrm(ks[2], (L, D_MODEL, IN_COLS), D_MODEL ** -0.5),
        "b_gate": nrm(ks[3], (L, N_BRANCHES * D_MODEL), 0.01),
        "pool_w": nrm(ks[4], (L, POOL_GROUPS, POOL_GROUP_DIM, POOL_GROUP_DIM), POOL_GROUP_DIM ** -0.5),
        "pool_scale": 1.0 + nrm(ks[5], (L, POOL_WIDTH), 0.02),
        "lb_logits": nrm(ks[6], (L, HGRN_WIDTH), 1.0),
        "hgrn_norm_g": 1.0 + nrm(ks[7], (L, HGRN_HEAD_DIM), 0.02),
        "w_pa": nrm(ks[8], (L, POOL_WIDTH, D_MODEL), POOL_WIDTH ** -0.5),
        "w_pb": nrm(ks[9], (L, HGRN_WIDTH, D_MODEL), HGRN_WIDTH ** -0.5),
        "w_o": nrm(ks[10], (L, D_MODEL, D_MODEL), D_MODEL ** -0.5),
        "norm2_g": 1.0 + nrm(ks[11], (L, D_MODEL), 0.02),
        "w_up": nrm(ks[12], (L, D_MODEL, 2 * D_FF), D_MODEL ** -0.5),
        "conv_w": nrm(ks[13], (L, CONV_WIDTH, 2 * D_FF), CONV_WIDTH ** -0.5),
        "conv_b": nrm(ks[14], (L, 2 * D_FF), 0.01),
        "w_down": nrm(ks[15], (L, D_FF, D_MODEL), D_FF ** -0.5),
        "final_g": 1.0 + nrm(ks[16], (D_MODEL,), 0.02),
    }


def _fwd_reference(x, norm1_g, w_in, b_gate, pool_w, pool_scale, lb_logits, hgrn_norm_g,
              w_pa, w_pb, w_o, norm2_g, w_up, conv_w, conv_b, w_down, final_g):
    B, S, _ = x.shape
    lb_soft = jax.nn.softmax(lb_logits.astype(jnp.float32), axis=0)
    lb_cum = jnp.cumsum(lb_soft, axis=0)
    lower_bounds = lb_cum - lb_cum[0:1]

    splits = np.cumsum([POOL_WIDTH, HGRN_WIDTH, HGRN_WIDTH, HGRN_WIDTH, HGRN_WIDTH]).tolist()
    for l in range(DEPTH):
        xn = rmsnorm(x, norm1_g[l])
        z = xn @ w_in[l]
        u_pool, zq, zf, zi, zo, zg = jnp.split(z, splits, axis=-1)
        gates = jax.nn.sigmoid(zg + b_gate[l]).reshape(B, S, N_BRANCHES, D_MODEL)
        ya = pool_mixer(u_pool, pool_w[l], pool_scale[l]) @ w_pa[l]
        yb = hgrn2_mixer(zq, zf, zi, zo, lower_bounds[l], hgrn_norm_g[l]) @ w_pb[l]
        merged = gates[:, :, 0] * ya + gates[:, :, 1] * yb
        x = x + merged @ w_o[l]
        x = x + conv_glu_ffn(rmsnorm(x, norm2_g[l]), w_up[l], conv_w[l], conv_b[l], w_down[l])
    return rmsnorm(x, final_g)


import jax as _jax
import jax.numpy as _jnp

TWIN_FORMAT = 'train_step'
FWD_PARAMS = ['x', 'norm1_g', 'w_in', 'b_gate', 'pool_w', 'pool_scale', 'lb_logits', 'hgrn_norm_g', 'w_pa', 'w_pb', 'w_o', 'norm2_g', 'w_up', 'conv_w', 'conv_b', 'w_down', 'final_g']
TWIN_WEIGHTS = ['norm1_g', 'w_in', 'b_gate', 'pool_w', 'pool_scale', 'lb_logits', 'hgrn_norm_g', 'w_pa', 'w_pb', 'w_o', 'norm2_g', 'w_up', 'conv_w', 'conv_b', 'w_down', 'final_g']
TWIN_DIFF_INPUT = 'x'
TWIN_INPUTS = ['x', 'norm1_g', 'w_in', 'b_gate', 'pool_w', 'pool_scale', 'lb_logits', 'hgrn_norm_g', 'w_pa', 'w_pb', 'w_o', 'norm2_g', 'w_up', 'conv_w', 'conv_b', 'w_down', 'final_g', 'loss_target', 'm_norm1_g', 'm_w_in', 'm_b_gate', 'm_pool_w', 'm_pool_scale', 'm_lb_logits', 'm_hgrn_norm_g', 'm_w_pa', 'm_w_pb', 'm_w_o', 'm_norm2_g', 'm_w_up', 'm_conv_w', 'm_conv_b', 'm_w_down', 'm_final_g', 'v_norm1_g', 'v_w_in', 'v_b_gate', 'v_pool_w', 'v_pool_scale', 'v_lb_logits', 'v_hgrn_norm_g', 'v_w_pa', 'v_w_pb', 'v_w_o', 'v_norm2_g', 'v_w_up', 'v_conv_w', 'v_conv_b', 'v_w_down', 'v_final_g']
TWIN_OUTPUTS = ['loss', 'grad_x', 'grad_norm1_g', 'grad_w_in', 'grad_b_gate', 'grad_pool_w', 'grad_pool_scale', 'grad_lb_logits', 'grad_hgrn_norm_g', 'grad_w_pa', 'grad_w_pb', 'grad_w_o', 'grad_norm2_g', 'grad_w_up', 'grad_conv_w', 'grad_conv_b', 'grad_w_down', 'grad_final_g', 'delta_norm1_g', 'delta_w_in', 'delta_b_gate', 'delta_pool_w', 'delta_pool_scale', 'delta_lb_logits', 'delta_hgrn_norm_g', 'delta_w_pa', 'delta_w_pb', 'delta_w_o', 'delta_norm2_g', 'delta_w_up', 'delta_conv_w', 'delta_conv_b', 'delta_w_down', 'delta_final_g', 'new_m_norm1_g', 'new_m_w_in', 'new_m_b_gate', 'new_m_pool_w', 'new_m_pool_scale', 'new_m_lb_logits', 'new_m_hgrn_norm_g', 'new_m_w_pa', 'new_m_w_pb', 'new_m_w_o', 'new_m_norm2_g', 'new_m_w_up', 'new_m_conv_w', 'new_m_conv_b', 'new_m_w_down', 'new_m_final_g', 'new_v_norm1_g', 'new_v_w_in', 'new_v_b_gate', 'new_v_pool_w', 'new_v_pool_scale', 'new_v_lb_logits', 'new_v_hgrn_norm_g', 'new_v_w_pa', 'new_v_w_pb', 'new_v_w_o', 'new_v_norm2_g', 'new_v_w_up', 'new_v_conv_w', 'new_v_conv_b', 'new_v_w_down', 'new_v_final_g']
TWIN_LEAF_KINDS = {'loss': 'loss', 'grad_x': 'grad_x', 'grad_norm1_g': 'grad_w', 'grad_w_in': 'grad_w', 'grad_b_gate': 'grad_w', 'grad_pool_w': 'grad_w', 'grad_pool_scale': 'grad_w', 'grad_lb_logits': 'grad_w', 'grad_hgrn_norm_g': 'grad_w', 'grad_w_pa': 'grad_w', 'grad_w_pb': 'grad_w', 'grad_w_o': 'grad_w', 'grad_norm2_g': 'grad_w', 'grad_w_up': 'grad_w', 'grad_conv_w': 'grad_w', 'grad_conv_b': 'grad_w', 'grad_w_down': 'grad_w', 'grad_final_g': 'grad_w', 'delta_norm1_g': 'delta_w', 'delta_w_in': 'delta_w', 'delta_b_gate': 'delta_w', 'delta_pool_w': 'delta_w', 'delta_pool_scale': 'delta_w', 'delta_lb_logits': 'delta_w', 'delta_hgrn_norm_g': 'delta_w', 'delta_w_pa': 'delta_w', 'delta_w_pb': 'delta_w', 'delta_w_o': 'delta_w', 'delta_norm2_g': 'delta_w', 'delta_w_up': 'delta_w', 'delta_conv_w': 'delta_w', 'delta_conv_b': 'delta_w', 'delta_w_down': 'delta_w', 'delta_final_g': 'delta_w', 'new_m_norm1_g': 'new_m', 'new_m_w_in': 'new_m', 'new_m_b_gate': 'new_m', 'new_m_pool_w': 'new_m', 'new_m_pool_scale': 'new_m', 'new_m_lb_logits': 'new_m', 'new_m_hgrn_norm_g': 'new_m', 'new_m_w_pa': 'new_m', 'new_m_w_pb': 'new_m', 'new_m_w_o': 'new_m', 'new_m_norm2_g': 'new_m', 'new_m_w_up': 'new_m', 'new_m_conv_w': 'new_m', 'new_m_conv_b': 'new_m', 'new_m_w_down': 'new_m', 'new_m_final_g': 'new_m', 'new_v_norm1_g': 'new_v', 'new_v_w_in': 'new_v', 'new_v_b_gate': 'new_v', 'new_v_pool_w': 'new_v', 'new_v_pool_scale': 'new_v', 'new_v_lb_logits': 'new_v', 'new_v_hgrn_norm_g': 'new_v', 'new_v_w_pa': 'new_v', 'new_v_w_pb': 'new_v', 'new_v_w_o': 'new_v', 'new_v_norm2_g': 'new_v', 'new_v_w_up': 'new_v', 'new_v_conv_w': 'new_v', 'new_v_conv_b': 'new_v', 'new_v_w_down': 'new_v', 'new_v_final_g': 'new_v'}


def _forward(args):
    return _fwd_reference(*[args[k] for k in FWD_PARAMS])


def _output_shape():
    def fwd():
        inp = _fwd_setup_inputs(0)
        return _fwd_reference(*[inp[k] for k in FWD_PARAMS])
    out = _jax.eval_shape(fwd)
    return out.shape, out.dtype

N_MICROBATCH = 1
ADAM_LR = 0.001
ADAM_B1 = 0.9
ADAM_B2 = 0.999
ADAM_EPS = 1e-08
ADAM_WD = 0.01
ADAM_STEP = 10
PER_EXAMPLE_BATCH_AXIS = {'x': 0, 'loss_target': 0}
SHARED_INPUTS = []
_WEIGHT_DTYPES = {'norm1_g': _jnp.float32, 'w_in': _jnp.float32, 'b_gate': _jnp.float32, 'pool_w': _jnp.float32, 'pool_scale': _jnp.float32, 'lb_logits': _jnp.float32, 'hgrn_norm_g': _jnp.float32, 'w_pa': _jnp.float32, 'w_pb': _jnp.float32, 'w_o': _jnp.float32, 'norm2_g': _jnp.float32, 'w_up': _jnp.float32, 'conv_w': _jnp.float32, 'conv_b': _jnp.float32, 'w_down': _jnp.float32, 'final_g': _jnp.float32}
MOMENT_SCALE = {'norm1_g': 1.621586e-01, 'w_in': 7.813683e-02, 'b_gate': 3.955592e-02, 'pool_w': 1.594041e-01, 'pool_scale': 1.672602e-01, 'lb_logits': 6.879427e-03, 'hgrn_norm_g': 2.182709e-01, 'w_pa': 1.117638e-01, 'w_pb': 7.577506e-02, 'w_o': 1.357938e-01, 'norm2_g': 1.696648e-01, 'w_up': 7.268074e-02, 'conv_w': 7.173869e-02, 'conv_b': 7.290713e-02, 'w_down': 1.185193e-01, 'final_g': 6.397962e+01}


def _to_microbatches(a, axis):
    t = _jnp.moveaxis(a, axis, 0)
    t = t.reshape((N_MICROBATCH, t.shape[0] // N_MICROBATCH) + t.shape[1:])
    return _jnp.moveaxis(t, 1, axis + 1)


def setup_inputs(seed: int = 0) -> dict:
    inp = _fwd_setup_inputs(seed)
    key = _jax.random.fold_in(_jax.random.key(seed), 7919)
    shape, _ = _output_shape()
    out = dict(inp)
    out["loss_target"] = _jax.random.normal(_jax.random.fold_in(key, 0), shape, _jnp.float32)
    for i, name in enumerate(TWIN_WEIGHTS):
        w = inp[name].astype(_jnp.float32)
        if MOMENT_SCALE is None:
            s = _jnp.sqrt(_jnp.mean(_jnp.square(w)) + 1e-30)
        else:
            s = MOMENT_SCALE[name]
        km, kv = _jax.random.split(_jax.random.fold_in(key, i + 1))
        out[name] = w
        out["m_" + name] = s * _jax.random.normal(km, w.shape, _jnp.float32)
        out["v_" + name] = (s * s) * _jax.random.uniform(kv, w.shape, _jnp.float32, 0.5, 1.5)
    if N_MICROBATCH > 1:
        for name, axis in PER_EXAMPLE_BATCH_AXIS.items():
            out[name] = _to_microbatches(out[name], axis)
    return {'x': out['x'], 'norm1_g': out['norm1_g'], 'w_in': out['w_in'], 'b_gate': out['b_gate'], 'pool_w': out['pool_w'], 'pool_scale': out['pool_scale'], 'lb_logits': out['lb_logits'], 'hgrn_norm_g': out['hgrn_norm_g'], 'w_pa': out['w_pa'], 'w_pb': out['w_pb'], 'w_o': out['w_o'], 'norm2_g': out['norm2_g'], 'w_up': out['w_up'], 'conv_w': out['conv_w'], 'conv_b': out['conv_b'], 'w_down': out['w_down'], 'final_g': out['final_g'], 'loss_target': out['loss_target'], 'm_norm1_g': out['m_norm1_g'], 'm_w_in': out['m_w_in'], 'm_b_gate': out['m_b_gate'], 'm_pool_w': out['m_pool_w'], 'm_pool_scale': out['m_pool_scale'], 'm_lb_logits': out['m_lb_logits'], 'm_hgrn_norm_g': out['m_hgrn_norm_g'], 'm_w_pa': out['m_w_pa'], 'm_w_pb': out['m_w_pb'], 'm_w_o': out['m_w_o'], 'm_norm2_g': out['m_norm2_g'], 'm_w_up': out['m_w_up'], 'm_conv_w': out['m_conv_w'], 'm_conv_b': out['m_conv_b'], 'm_w_down': out['m_w_down'], 'm_final_g': out['m_final_g'], 'v_norm1_g': out['v_norm1_g'], 'v_w_in': out['v_w_in'], 'v_b_gate': out['v_b_gate'], 'v_pool_w': out['v_pool_w'], 'v_pool_scale': out['v_pool_scale'], 'v_lb_logits': out['v_lb_logits'], 'v_hgrn_norm_g': out['v_hgrn_norm_g'], 'v_w_pa': out['v_w_pa'], 'v_w_pb': out['v_w_pb'], 'v_w_o': out['v_w_o'], 'v_norm2_g': out['v_norm2_g'], 'v_w_up': out['v_w_up'], 'v_conv_w': out['v_conv_w'], 'v_conv_b': out['v_conv_b'], 'v_w_down': out['v_w_down'], 'v_final_g': out['v_final_g']}


def _loss(weights, diff, rest, loss_target):
    with _jax.named_scope("forward"):
        args = {**rest, TWIN_DIFF_INPUT: diff, **{k: w.astype(_WEIGHT_DTYPES[k]) for k, w in weights.items()}}
        y = _forward(args)
    with _jax.named_scope("loss_head"):
        err = _jnp.square(y.astype(_jnp.float32) - loss_target)
        return 0.5 * _jnp.sum(_jnp.mean(err, axis=-1)) if err.ndim else 0.5 * err


def _adamw(w, g, m, v):
    m = ADAM_B1 * m + (1.0 - ADAM_B1) * g
    v = ADAM_B2 * v + (1.0 - ADAM_B2) * _jnp.square(g)
    m_hat = m / (1.0 - ADAM_B1 ** ADAM_STEP)
    v_hat = v / (1.0 - ADAM_B2 ** ADAM_STEP)
    delta = -ADAM_LR * (m_hat / (_jnp.sqrt(v_hat) + ADAM_EPS) + ADAM_WD * w)
    return delta, m, v


def reference(x, norm1_g, w_in, b_gate, pool_w, pool_scale, lb_logits, hgrn_norm_g, w_pa, w_pb, w_o, norm2_g, w_up, conv_w, conv_b, w_down, final_g, loss_target, m_norm1_g, m_w_in, m_b_gate, m_pool_w, m_pool_scale, m_lb_logits, m_hgrn_norm_g, m_w_pa, m_w_pb, m_w_o, m_norm2_g, m_w_up, m_conv_w, m_conv_b, m_w_down, m_final_g, v_norm1_g, v_w_in, v_b_gate, v_pool_w, v_pool_scale, v_lb_logits, v_hgrn_norm_g, v_w_pa, v_w_pb, v_w_o, v_norm2_g, v_w_up, v_conv_w, v_conv_b, v_w_down, v_final_g):
    given = dict(x=x, norm1_g=norm1_g, w_in=w_in, b_gate=b_gate, pool_w=pool_w, pool_scale=pool_scale, lb_logits=lb_logits, hgrn_norm_g=hgrn_norm_g, w_pa=w_pa, w_pb=w_pb, w_o=w_o, norm2_g=norm2_g, w_up=w_up, conv_w=conv_w, conv_b=conv_b, w_down=w_down, final_g=final_g, loss_target=loss_target, m_norm1_g=m_norm1_g, m_w_in=m_w_in, m_b_gate=m_b_gate, m_pool_w=m_pool_w, m_pool_scale=m_pool_scale, m_lb_logits=m_lb_logits, m_hgrn_norm_g=m_hgrn_norm_g, m_w_pa=m_w_pa, m_w_pb=m_w_pb, m_w_o=m_w_o, m_norm2_g=m_norm2_g, m_w_up=m_w_up, m_conv_w=m_conv_w, m_conv_b=m_conv_b, m_w_down=m_w_down, m_final_g=m_final_g, v_norm1_g=v_norm1_g, v_w_in=v_w_in, v_b_gate=v_b_gate, v_pool_w=v_pool_w, v_pool_scale=v_pool_scale, v_lb_logits=v_lb_logits, v_hgrn_norm_g=v_hgrn_norm_g, v_w_pa=v_w_pa, v_w_pb=v_w_pb, v_w_o=v_w_o, v_norm2_g=v_norm2_g, v_w_up=v_w_up, v_conv_w=v_conv_w, v_conv_b=v_conv_b, v_w_down=v_w_down, v_final_g=v_final_g)
    weights = {n: given[n] for n in TWIN_WEIGHTS}
    shared = {n: given[n] for n in SHARED_INPUTS}
    per_example = {n: given[n] for n in ['x']}
    grad_fn = _jax.value_and_grad(_loss, argnums=(0, 1))

    def one_microbatch(ex, loss_target):
        ex = dict(ex)
        diff = ex.pop(TWIN_DIFF_INPUT)
        return grad_fn(weights, diff, {**shared, **ex}, loss_target)

    if N_MICROBATCH == 1:
        loss, (grad_w, grad_x) = one_microbatch(per_example, given["loss_target"])
    else:
        def body(carry, xs):
            loss_sum, grad_sum = carry
            l_k, (gw_k, gx_k) = one_microbatch(xs[0], xs[1])
            with _jax.named_scope("update"):
                return (loss_sum + l_k, _jax.tree.map(_jnp.add, grad_sum, gw_k)), gx_k

        init = (_jnp.zeros((), _jnp.float32), _jax.tree.map(_jnp.zeros_like, weights))
        (loss, grad_w), grad_x = _jax.lax.scan(body, init, (per_example, given["loss_target"]))
    with _jax.named_scope("update"):
        delta_w, new_m, new_v = {}, {}, {}
        for n in TWIN_WEIGHTS:
            delta_w[n], new_m[n], new_v[n] = _adamw(weights[n], grad_w[n], given["m_" + n], given["v_" + n])
    return (loss, grad_x, *[grad_w[n] for n in TWIN_WEIGHTS], *[delta_w[n] for n in TWIN_WEIGHTS],
            *[new_m[n] for n in TWIN_WEIGHTS], *[new_v[n] for n in TWIN_WEIGHTS])
```

```python
import functools

import jax
import jax.numpy as jnp
from jax import lax
from jax.experimental import pallas as pl
from jax.experimental.pallas import tpu as pltpu

F32 = jnp.float32
BF16 = jnp.bfloat16

EPS = 1e-6
CHUNK = 64
SUB = 16
LANES = 128
POOL_WINDOWS = (2, 4, 8, 16)
HALO_POOL = 16
HALO_CONV = 8
EXP_CLAMP = 80.0

ADAM_LR = 0.001
ADAM_B1 = 0.9
ADAM_B2 = 0.999
ADAM_EPS = 1e-08
ADAM_WD = 0.01
ADAM_STEP = 10

VMEM_LIMIT = 56 * 1024 * 1024
MESH_ID = pl.DeviceIdType.MESH


def _params(*sem):
    return pltpu.CompilerParams(dimension_semantics=sem or None, vmem_limit_bytes=VMEM_LIMIT)


def _dot(a, b):
    return jnp.dot(a, b, preferred_element_type=F32)


def _dot_nt(a, b):
    return lax.dot_general(a, b, (((1,), (1,)), ((), ())), preferred_element_type=F32)


def _dot_tn(a, b):
    return lax.dot_general(a, b, (((0,), (0,)), ((), ())), preferred_element_type=F32)


def _sigmoid(x):
    return jax.nn.sigmoid(x)


def _dsilu(x, s):
    return s * (1.0 + x * (1.0 - s))


def _row_tile(rows, want):
    t = min(rows, want)
    while rows % t:
        t //= 2
    return t


def _norm_matmul(x, g, w, *, name, tn):
    S, D = x.shape
    N = w.shape[1]
    tm = _row_tile(S, 512)

    def body(x_ref, g_ref, w_ref, xn_ref, o_ref):
        @pl.when(pl.program_id(1) == 0)
        def _():
            xf = x_ref[...]
            r = lax.rsqrt(jnp.mean(xf * xf, axis=-1, keepdims=True) + EPS)
            xn_ref[...] = (xf * r * g_ref[...]).astype(BF16)

        o_ref[...] = _dot(xn_ref[...], w_ref[...])

    return pl.pallas_call(
        body, name=name, grid=(S // tm, N // tn),
        in_specs=[pl.BlockSpec((tm, D), lambda i, j: (i, 0)),
                  pl.BlockSpec((1, D), lambda i, j: (0, 0)),
                  pl.BlockSpec((D, tn), lambda i, j: (0, j))],
        out_specs=[pl.BlockSpec((tm, D), lambda i, j: (i, 0)),
                   pl.BlockSpec((tm, tn), lambda i, j: (i, j))],
        out_shape=[jax.ShapeDtypeStruct((S, D), BF16), jax.ShapeDtypeStruct((S, N), F32)],
        compiler_params=_params("parallel", "arbitrary"),
    )(x, g, w)


def _dgrad_norm(dy, w, x, g, dres, *, name, tk):
    S, N = dy.shape
    D = x.shape[1]
    tm = _row_tile(S, 512)
    nk = N // tk

    def body(dy_ref, w_ref, x_ref, g_ref, dres_ref, dx_ref, dg_ref, acc_ref):
        i, k = pl.program_id(0), pl.program_id(1)

        @pl.when(k == 0)
        def _():
            acc_ref[...] = jnp.zeros_like(acc_ref)

        @pl.when((i == 0) & (k == 0))
        def _():
            dg_ref[...] = jnp.zeros_like(dg_ref)

        acc_ref[...] += _dot_nt(dy_ref[...], w_ref[...])

        @pl.when(k == nk - 1)
        def _():
            dxn = acc_ref[...]
            xf = x_ref[...]
            r = lax.rsqrt(jnp.mean(xf * xf, axis=-1, keepdims=True) + EPS)
            xhat = xf * r
            dxhat = dxn * g_ref[...]
            dx_ref[...] = dres_ref[...] + r * (dxhat - xhat * jnp.mean(dxhat * xhat, axis=-1, keepdims=True))
            dg_ref[...] += jnp.sum(dxn * xhat, axis=0, keepdims=True)

    return pl.pallas_call(
        body, name=name, grid=(S // tm, nk),
        in_specs=[pl.BlockSpec((tm, tk), lambda i, k: (i, k)),
                  pl.BlockSpec((D, tk), lambda i, k: (0, k)),
                  pl.BlockSpec((tm, D), lambda i, k: (i, 0)),
                  pl.BlockSpec((1, D), lambda i, k: (0, 0)),
                  pl.BlockSpec((tm, D), lambda i, k: (i, 0))],
        out_specs=[pl.BlockSpec((tm, D), lambda i, k: (i, 0)),
                   pl.BlockSpec((1, D), lambda i, k: (0, 0))],
        out_shape=[jax.ShapeDtypeStruct((S, D), F32), jax.ShapeDtypeStruct((1, D), F32)],
        scratch_shapes=[pltpu.VMEM((tm, D), F32)],
        compiler_params=_params("arbitrary", "arbitrary"),
    )(dy, w, x, g, dres)


def _wgrad(a, dy, *, name, tn):
    S, K = a.shape
    N = dy.shape[1]
    tm = _row_tile(S, 512)

    def body(a_ref, dy_ref, o_ref):
        @pl.when(pl.program_id(1) == 0)
        def _():
            o_ref[...] = jnp.zeros_like(o_ref)

        o_ref[...] += _dot_tn(a_ref[...], dy_ref[...])

    return pl.pallas_call(
        body, name=name, grid=(N // tn, S // tm),
        in_specs=[pl.BlockSpec((tm, K), lambda n, s: (s, 0)),
                  pl.BlockSpec((tm, tn), lambda n, s: (s, n))],
        out_specs=pl.BlockSpec((K, tn), lambda n, s: (0, n)),
        out_shape=jax.ShapeDtypeStruct((K, N), F32),
        compiler_params=_params("parallel", "arbitrary"),
    )(a, dy)


def _pooled(u, halo, first_tile, row0):
    T = u.shape[0]
    halo = jnp.where(first_tile, 0.0, halo)
    up = jnp.concatenate([halo, u], axis=0)
    t1 = (row0 + lax.broadcasted_iota(jnp.int32, (T, 1), 0) + 1).astype(F32)
    outs = []
    for gi, w in enumerate(POOL_WINDOWS):
        s = up[:, gi * LANES:(gi + 1) * LANES]
        k = 1
        while k < w:
            s = s + pltpu.roll(s, k, 0)
            k *= 2
        inv = 1.0 / jnp.minimum(t1, float(w))
        outs.append(s[HALO_POOL:, :] * inv - u[:, gi * LANES:(gi + 1) * LANES])
    return outs


def _pool_fwd(zu, pool_w, pool_scale, *, name):
    S, P = zu.shape
    T = _row_tile(S, 512)
    hb = T // HALO_POOL

    def body(u_ref, halo_ref, pw_ref, ps_ref, o_ref):
        i = pl.program_id(0)
        pooled = _pooled(u_ref[...], halo_ref[...], i == 0, i * T)
        for gi in range(len(POOL_WINDOWS)):
            mixed = _dot(pooled[gi].astype(BF16), pw_ref[gi].astype(BF16))
            cols = slice(gi * LANES, (gi + 1) * LANES)
            o_ref[:, cols] = (mixed * ps_ref[:, cols]).astype(BF16)

    return pl.pallas_call(
        body, name=name, grid=(S // T,),
        in_specs=[pl.BlockSpec((T, P), lambda i: (i, 0)),
                  pl.BlockSpec((HALO_POOL, P), lambda i: (jnp.maximum(i * hb - 1, 0), 0)),
                  pl.BlockSpec(pool_w.shape, lambda i: (0, 0, 0)),
                  pl.BlockSpec((1, P), lambda i: (0, 0))],
        out_specs=pl.BlockSpec((T, P), lambda i: (i, 0)),
        out_shape=jax.ShapeDtypeStruct((S, P), BF16),
        compiler_params=_params("parallel"),
    )(zu, zu, pool_w, pool_scale)


def _pool_bwd(zu, dpm, pool_w, pool_scale, *, name):
    S, P = zu.shape
    T = _row_tile(S, 512)
    hb = T // HALO_POOL
    nt = S // T
    G = len(POOL_WINDOWS)

    def body(u_ref, halo_ref, d_ref, dnext_ref, pw_ref, ps_ref, du_ref, dpw_ref, dps_ref):
        i = pl.program_id(0)

        @pl.when(i == 0)
        def _():
            dpw_ref[...] = jnp.zeros_like(dpw_ref)
            dps_ref[...] = jnp.zeros_like(dps_ref)

        u = u_ref[...]
        pooled = _pooled(u, halo_ref[...], i == 0, i * T)
        dnext = jnp.where(i == nt - 1, 0.0, dnext_ref[...])
        dext = jnp.concatenate([d_ref[...], dnext], axis=0)
        t1 = (i * T + lax.broadcasted_iota(jnp.int32, (T + HALO_POOL, 1), 0) + 1).astype(F32)
        n_ext = T + HALO_POOL
        for gi, w in enumerate(POOL_WINDOWS):
            cols = slice(gi * LANES, (gi + 1) * LANES)
            pw = pw_ref[gi].astype(BF16)
            pg = pooled[gi].astype(BF16)
            mixed = _dot(pg, pw)
            dps_ref[:, cols] += jnp.sum(d_ref[:, cols] * mixed, axis=0, keepdims=True)
            dmixed = (dext[:, cols] * ps_ref[:, cols]).astype(BF16)
            dpw_ref[gi] += _dot_tn(pg, dmixed[:T, :])
            dpooled = _dot_nt(dmixed, pw)
            e = dpooled * (1.0 / jnp.minimum(t1, float(w)))
            k = 1
            while k < w:
                e = e + pltpu.roll(e, n_ext - k, 0)
                k *= 2
            du_ref[:, cols] = (e[:T, :] - dpooled[:T, :]).astype(BF16)

    return pl.pallas_call(
        body, name=name, grid=(nt,),
        in_specs=[pl.BlockSpec((T, P), lambda i: (i, 0)),
                  pl.BlockSpec((HALO_POOL, P), lambda i: (jnp.maximum(i * hb - 1, 0), 0)),
                  pl.BlockSpec((T, P), lambda i: (i, 0)),
                  pl.BlockSpec((HALO_POOL, P), lambda i: (jnp.minimum((i + 1) * hb, S // HALO_POOL - 1), 0)),
                  pl.BlockSpec(pool_w.shape, lambda i: (0, 0, 0)),
                  pl.BlockSpec((1, P), lambda i: (0, 0))],
        out_specs=[pl.BlockSpec((T, P), lambda i: (i, 0)),
                   pl.BlockSpec(pool_w.shape, lambda i: (0, 0, 0)),
                   pl.BlockSpec((1, P), lambda i: (0, 0))],
        out_shape=[jax.ShapeDtypeStruct((S, P), BF16),
                   jax.ShapeDtypeStruct(pool_w.shape, F32),
                   jax.ShapeDtypeStruct((1, P), F32)],
        compiler_params=_params("arbitrary"),
    )(zu, zu, dpm, dpm, pool_w, pool_scale)


def _cumsum_rows(x):
    n = x.shape[0]
    row = lax.broadcasted_iota(jnp.int32, x.shape, 0)
    s = 1
    while s < n:
        x = x + jnp.where(row >= s, pltpu.roll(x, s, 0), 0.0)
        s *= 2
    return x


def _rev_cumsum_rows(x):
    n = x.shape[0]
    row = lax.broadcasted_iota(jnp.int32, x.shape, 0)
    s = 1
    while s < n:
        x = x + jnp.where(row < n - s, pltpu.roll(x, n - s, 0), 0.0)
        s *= 2
    return x


def _chunk_prep(zq, zf, lb, b_ref):
    n_sub = CHUNK // SUB
    sq = _sigmoid(zq)
    q = zq * sq
    sf = _sigmoid(zf)
    f = lb + (1.0 - lb) * sf
    k = 1.0 - f
    b = _cumsum_rows(jnp.log(f))
    b_ref[...] = b
    shape = (SUB, b.shape[1])
    ends = [jnp.broadcast_to(b_ref[pl.ds(SUB * j + SUB - 1, 1), :], shape) for j in range(n_sub)]
    mids = [jnp.broadcast_to(b_ref[pl.ds(SUB * j + SUB // 2 - 1, 1), :], shape) for j in range(n_sub)]
    own = [b[SUB * j:SUB * (j + 1), :] for j in range(n_sub)]
    m0 = jnp.concatenate(mids, axis=0)
    e1 = jnp.concatenate(ends, axis=0)
    eq = [jnp.exp(jnp.minimum(b - m0, EXP_CLAMP))]
    for d in range(1, n_sub):
        rd = jnp.concatenate([own[j] if j < d else ends[j - d] for j in range(n_sub)], axis=0)
        eq.append(jnp.exp(b - rd))
    ek0 = jnp.exp(jnp.minimum(m0 - b, EXP_CLAMP))
    ek1 = jnp.exp(e1 - b)
    b_last = b_ref[pl.ds(CHUNK - 1, 1), :]
    return dict(q=q, k=k, f=f, sq=sq, sf=sf, b=b, eq=eq, ek0=ek0, ek1=ek1,
                eb=jnp.exp(b), ekl=jnp.exp(b_last - b), el=jnp.exp(b_last))


def _chunk_masks():
    ti = lax.broadcasted_iota(jnp.int32, (CHUNK, CHUNK), 0)
    si = lax.broadcasted_iota(jnp.int32, (CHUNK, CHUNK), 1)
    shift = SUB.bit_length() - 1
    dsub = jnp.right_shift(ti, shift) - jnp.right_shift(si, shift)
    masks = [(dsub == 0) & (si <= ti)]
    masks += [dsub == d for d in range(1, CHUNK // SUB)]
    return masks


def _chunk_attn(p, masks):
    qd = [(p["q"] * e).astype(BF16) for e in p["eq"]]
    k0 = (p["k"] * p["ek0"]).astype(BF16)
    k1 = (p["k"] * p["ek1"]).astype(BF16)
    a = jnp.where(masks[0], _dot_nt(qd[0], k0), 0.0)
    for d in range(1, len(masks)):
        a = jnp.where(masks[d], _dot_nt(qd[d], k1), a)
    return a, qd, k0, k1


def _hgrn_fwd(zh, lb, norm_g, *, name):
    S = zh.shape[0]
    HW = zh.shape[1] // 4
    NH = HW // LANES
    T = _row_tile(S, 512)
    nc = T // CHUNK

    def body(zq_ref, zf_ref, zi_ref, zo_ref, lb_ref, ng_ref, o_ref, of_ref, st_ref, s_scr, b_scr):
        @pl.when(pl.program_id(1) == 0)
        def _():
            s_scr[...] = jnp.zeros_like(s_scr)

        lbv = lb_ref[...]
        ng = ng_ref[...]
        masks = _chunk_masks()

        def chunk(c, carry):
            rows = pl.ds(pl.multiple_of(c * CHUNK, CHUNK), CHUNK)
            p = _chunk_prep(zq_ref[rows, :], zf_ref[rows, :], lbv, b_scr)
            v = zi_ref[rows, :].astype(BF16)
            zo = zo_ref[rows, :]
            st = s_scr[...]
            st_ref[c] = st
            a, _, _, _ = _chunk_attn(p, masks)
            o = _dot(a.astype(BF16), v) + _dot_nt((p["q"] * p["eb"]).astype(BF16), st.astype(BF16))
            s_scr[...] = st * p["el"] + _dot_tn(v, (p["k"] * p["ekl"]).astype(BF16))
            o_ref[rows, :] = o
            r = lax.rsqrt(jnp.mean(o * o, axis=-1, keepdims=True) + EPS)
            of_ref[rows, :] = (o * r * ng * (zo * _sigmoid(zo))).astype(BF16)
            return carry

        lax.fori_loop(0, nc, chunk, 0)

    col = lambda part: (lambda h, i: (i, part * NH + h))
    return pl.pallas_call(
        body, name=name, grid=(NH, S // T),
        in_specs=[pl.BlockSpec((T, LANES), col(0)), pl.BlockSpec((T, LANES), col(1)),
                  pl.BlockSpec((T, LANES), col(2)), pl.BlockSpec((T, LANES), col(3)),
                  pl.BlockSpec((1, LANES), lambda h, i: (0, h)),
                  pl.BlockSpec((1, LANES), lambda h, i: (0, 0))],
        out_specs=[pl.BlockSpec((T, LANES), lambda h, i: (i, h)),
                   pl.BlockSpec((T, LANES), lambda h, i: (i, h)),
                   pl.BlockSpec((nc, None, LANES, LANES), lambda h, i: (i, h, 0, 0))],
        out_shape=[jax.ShapeDtypeStruct((S, HW), F32),
                   jax.ShapeDtypeStruct((S, HW), BF16),
                   jax.ShapeDtypeStruct((S // CHUNK, NH, LANES, LANES), F32)],
        scratch_shapes=[pltpu.VMEM((LANES, LANES), F32), pltpu.VMEM((CHUNK, LANES), F32)],
        compiler_params=_params("parallel", "arbitrary"),
    )(zh, zh, zh, zh, lb, norm_g)


def _hgrn_bwd(zh, lb, norm_g, o_raw, states, dof, *, name):
    S = zh.shape[0]
    HW = zh.shape[1] // 4
    NH = HW // LANES
    T = _row_tile(S, 512)
    nc = T // CHUNK
    nt = S // T

    def body(zq_ref, zf_ref, zi_ref, zo_ref, lb_ref, ng_ref, o_ref, st_ref, dof_ref,
             dzq_ref, dzf_ref, dzi_ref, dzo_ref, dlb_ref, dng_ref, ds_scr, b_scr):
        @pl.when(pl.program_id(1) == 0)
        def _():
            ds_scr[...] = jnp.zeros_like(ds_scr)
            dlb_ref[...] = jnp.zeros_like(dlb_ref)
            dng_ref[...] = jnp.zeros_like(dng_ref)

        lbv = lb_ref[...]
        ng = ng_ref[...]
        masks = _chunk_masks()
        last_row = lax.broadcasted_iota(jnp.int32, (CHUNK, 1), 0) == CHUNK - 1

        def chunk(cr, carry):
            dlb_acc, dng_acc = carry
            c = nc - 1 - cr
            rows = pl.ds(pl.multiple_of(c * CHUNK, CHUNK), CHUNK)
            zq, zf, zo = zq_ref[rows, :], zf_ref[rows, :], zo_ref[rows, :]
            o = o_ref[rows, :]
            dof_c = dof_ref[rows, :]
            st = st_ref[c]
            dst = ds_scr[...]

            so = _sigmoid(zo)
            r = lax.rsqrt(jnp.mean(o * o, axis=-1, keepdims=True) + EPS)
            ohat = o * r
            d_on = dof_c * (zo * so)
            dzo_ref[rows, :] = (dof_c * ohat * ng * _dsilu(zo, so)).astype(BF16)
            dng_acc = dng_acc + jnp.sum(d_on * ohat, axis=0, keepdims=True)
            dohat = d_on * ng
            do = (r * (dohat - ohat * jnp.mean(dohat * ohat, axis=-1, keepdims=True))).astype(BF16)

            p = _chunk_prep(zq, zf, lbv, b_scr)
            q, k = p["q"], p["k"]
            v = zi_ref[rows, :].astype(BF16)
            a, qd, k0, k1 = _chunk_attn(p, masks)
            ktl = (k * p["ekl"]).astype(BF16)
            dstb = dst.astype(BF16)

            da = _dot_nt(do, v)
            dzi_ref[rows, :] = (_dot_tn(a.astype(BF16), do) + _dot_nt(ktl, dstb)).astype(BF16)

            da0 = jnp.where(masks[0], da, 0.0).astype(BF16)
            rq = _dot(da0, k0)
            rk0 = _dot_tn(da0, qd[0])
            dq = rq * p["eq"][0]
            db = qd[0].astype(F32) * rq - k0.astype(F32) * rk0
            rk1 = jnp.zeros_like(rk0)
            for d in range(1, len(masks)):
                dad = jnp.where(masks[d], da, 0.0).astype(BF16)
                rq = _dot(dad, k1)
                dq = dq + rq * p["eq"][d]
                db = db + qd[d].astype(F32) * rq
                rk1 = rk1 + _dot_tn(dad, qd[d])
            dk = rk0 * p["ek0"] + rk1 * p["ek1"]
            db = db - k1.astype(F32) * rk1
            qe = (q * p["eb"]).astype(BF16)
            rq = _dot(do, st.astype(BF16))
            dq = dq + rq * p["eb"]
            db = db + qe.astype(F32) * rq
            rk = _dot(v, dstb)
            dk = dk + rk * p["ekl"]
            db = db - ktl.astype(F32) * rk

            st_new = st * p["el"] + _dot_tn(v, ktl)
            db = db + jnp.where(last_row, jnp.sum(dstb.astype(F32) * st_new, axis=0, keepdims=True), 0.0)
            dg = _rev_cumsum_rows(db)
            ds_scr[...] = dst * p["el"] + _dot_tn(do, qe)

            dzq_ref[rows, :] = (dq * _dsilu(zq, p["sq"])).astype(BF16)
            df = dg / p["f"] - dk
            sf = p["sf"]
            dzf_ref[rows, :] = (df * (1.0 - lbv) * sf * (1.0 - sf)).astype(BF16)
            dlb_acc = dlb_acc + jnp.sum(df * (1.0 - sf), axis=0, keepdims=True)
            return dlb_acc, dng_acc

        zero = jnp.zeros((1, LANES), F32)
        dlb_acc, dng_acc = lax.fori_loop(0, nc, chunk, (zero, zero))
        dlb_ref[...] += dlb_acc
        dng_ref[...] += dng_acc

    rev = lambda i: nt - 1 - i
    col = lambda part: (lambda h, i: (rev(i), part * NH + h))
    blk = pl.BlockSpec((T, LANES), lambda h, i: (rev(i), h))
    return pl.pallas_call(
        body, name=name, grid=(NH, nt),
        in_specs=[pl.BlockSpec((T, LANES), col(0)), pl.BlockSpec((T, LANES), col(1)),
                  pl.BlockSpec((T, LANES), col(2)), pl.BlockSpec((T, LANES), col(3)),
                  pl.BlockSpec((1, LANES), lambda h, i: (0, h)),
                  pl.BlockSpec((1, LANES), lambda h, i: (0, 0)),
                  blk,
                  pl.BlockSpec((nc, None, LANES, LANES), lambda h, i: (rev(i), h, 0, 0)),
                  blk],
        out_specs=[blk, blk, blk, blk,
                   pl.BlockSpec((1, LANES), lambda h, i: (0, h)),
                   pl.BlockSpec((None, 1, LANES), lambda h, i: (h, 0, 0))],
        out_shape=[jax.ShapeDtypeStruct((S, HW), BF16)] * 4
        + [jax.ShapeDtypeStruct((1, HW), F32), jax.ShapeDtypeStruct((NH, 1, LANES), F32)],
        scratch_shapes=[pltpu.VMEM((LANES, LANES), F32), pltpu.VMEM((CHUNK, LANES), F32)],
        compiler_params=_params("parallel", "arbitrary"),
    )(zh, zh, zh, zh, lb, norm_g, o_raw, states, dof)


def _mix_fwd(x, pm, of, zg, b_gate, w_pa, w_pb, w_o, *, name):
    S, D = x.shape
    P = pm.shape[1]
    T = _row_tile(S, 256)

    def body(x_ref, pm_ref, of_ref, zg_ref, bg_ref, wpa_ref, wpb_ref, wo_ref, xo_ref, ya_ref, yb_ref):
        ya = _dot(pm_ref[...], wpa_ref[...])
        yb = _dot(of_ref[...], wpb_ref[...])
        ga = _sigmoid(zg_ref[:, :D] + bg_ref[:, :D])
        gb = _sigmoid(zg_ref[:, D:] + bg_ref[:, D:])
        merged = (ga * ya + gb * yb).astype(BF16)
        xo_ref[...] = x_ref[...] + _dot(merged, wo_ref[...])
        ya_ref[...] = ya.astype(BF16)
        yb_ref[...] = yb.astype(BF16)

    row = lambda w: pl.BlockSpec((T, w), lambda i: (i, 0))
    full = lambda a: pl.BlockSpec(a.shape, lambda i: (0, 0))
    return pl.pallas_call(
        body, name=name, grid=(S // T,),
        in_specs=[row(D), row(P), row(P), row(2 * D), full(b_gate), full(w_pa), full(w_pb), full(w_o)],
        out_specs=[row(D), row(D), row(D)],
        out_shape=[jax.ShapeDtypeStruct((S, D), F32), jax.ShapeDtypeStruct((S, D), BF16),
                   jax.ShapeDtypeStruct((S, D), BF16)],
        compiler_params=_params("parallel"),
    )(x, pm, of, zg, b_gate, w_pa, w_pb, w_o)


def _mix_bwd(dxm, ya, yb, zg, b_gate, pm, of, w_pa, w_pb, w_o, *, name):
    S, D = dxm.shape
    P = pm.shape[1]
    T = _row_tile(S, 256)

    def body(dx_ref, ya_ref, yb_ref, zg_ref, bg_ref, pm_ref, of_ref, wpa_ref, wpb_ref, wo_ref,
             dzg_ref, dpm_ref, dof_ref, dwo_ref, dwpa_ref, dwpb_ref, dbg_ref):
        @pl.when(pl.program_id(0) == 0)
        def _():
            dwo_ref[...] = jnp.zeros_like(dwo_ref)
            dwpa_ref[...] = jnp.zeros_like(dwpa_ref)
            dwpb_ref[...] = jnp.zeros_like(dwpb_ref)
            dbg_ref[...] = jnp.zeros_like(dbg_ref)

        dxb = dx_ref[...].astype(BF16)
        ya = ya_ref[...].astype(F32)
        yb = yb_ref[...].astype(F32)
        ga = _sigmoid(zg_ref[:, :D] + bg_ref[:, :D])
        gb = _sigmoid(zg_ref[:, D:] + bg_ref[:, D:])
        merged = (ga * ya + gb * yb).astype(BF16)
        dwo_ref[...] += _dot_tn(merged, dxb)
        dm = _dot_nt(dxb, wo_ref[...])
        dza = dm * ya * ga * (1.0 - ga)
        dzb = dm * yb * gb * (1.0 - gb)
        dzg_ref[:, :D] = dza.astype(BF16)
        dzg_ref[:, D:] = dzb.astype(BF16)
        dbg_ref[:, :D] += jnp.sum(dza, axis=0, keepdims=True)
        dbg_ref[:, D:] += jnp.sum(dzb, axis=0, keepdims=True)
        dya = (dm * ga).astype(BF16)
        dyb = (dm * gb).astype(BF16)
        dwpa_ref[...] += _dot_tn(pm_ref[...], dya)
        dwpb_ref[...] += _dot_tn(of_ref[...], dyb)
        dpm_ref[...] = _dot_nt(dya, wpa_ref[...])
        dof_ref[...] = _dot_nt(dyb, wpb_ref[...])

    row = lambda w: pl.BlockSpec((T, w), lambda i: (i, 0))
    full = lambda a: pl.BlockSpec(a.shape, lambda i: (0, 0))
    return pl.pallas_call(
        body, name=name, grid=(S // T,),
        in_specs=[row(D), row(D), row(D), row(2 * D), full(b_gate), row(P), row(P),
                  full(w_pa), full(w_pb), full(w_o)],
        out_specs=[row(2 * D), row(P), row(P), full(w_o), full(w_pa), full(w_pb), full(b_gate)],
        out_shape=[jax.ShapeDtypeStruct((S, 2 * D), BF16), jax.ShapeDtypeStruct((S, P), F32),
                   jax.ShapeDtypeStruct((S, P), F32), jax.ShapeDtypeStruct(w_o.shape, F32),
                   jax.ShapeDtypeStruct(w_pa.shape, F32), jax.ShapeDtypeStruct(w_pb.shape, F32),
                   jax.ShapeDtypeStruct(b_gate.shape, F32)],
        compiler_params=_params("arbitrary"),
    )(dxm, ya, yb, zg, b_gate, pm, of, w_pa, w_pb, w_o)


def _conv3(h, halo, first_tile, cw, cb):
    halo = jnp.where(first_tile, 0.0, halo)
    hp = jnp.concatenate([halo, h], axis=0)
    h1 = pltpu.roll(hp, 1, 0)[HALO_CONV:, :]
    h2 = pltpu.roll(hp, 2, 0)[HALO_CONV:, :]
    return cw[0:1, :] * h2 + cw[1:2, :] * h1 + cw[2:3, :] * h + cb, h1, h2


def _ffn_tiles(S, F, rows):
    T = _row_tile(S, rows)
    tf = F // 2 if (F // 2) % LANES == 0 else F
    return T, tf, F // tf


def _ffn_down_fwd(h, conv_w, conv_b, w_down, x, *, name):
    S, F2 = h.shape
    F, D = w_down.shape
    T, tf, nf = _ffn_tiles(S, F, 512)
    hb = T // HALO_CONV

    def body(hv_ref, hg_ref, pv_ref, pg_ref, cwv_ref, cwg_ref, cbv_ref, cbg_ref, wd_ref, x_ref, o_ref, acc_ref):
        i, f = pl.program_id(0), pl.program_id(1)

        @pl.when(f == 0)
        def _():
            acc_ref[...] = jnp.zeros_like(acc_ref)

        val, _, _ = _conv3(hv_ref[...], pv_ref[...], i == 0, cwv_ref[...], cbv_ref[...])
        gate, _, _ = _conv3(hg_ref[...], pg_ref[...], i == 0, cwg_ref[...], cbg_ref[...])
        a = (gate * _sigmoid(gate) * val).astype(BF16)
        acc_ref[...] += _dot(a, wd_ref[...])

        @pl.when(f == nf - 1)
        def _():
            o_ref[...] = x_ref[...] + acc_ref[...]

    prev = lambda i: jnp.maximum(i * hb - 1, 0)
    return pl.pallas_call(
        body, name=name, grid=(S // T, nf),
        in_specs=[pl.BlockSpec((T, tf), lambda i, f: (i, f)),
                  pl.BlockSpec((T, tf), lambda i, f: (i, nf + f)),
                  pl.BlockSpec((HALO_CONV, tf), lambda i, f: (prev(i), f)),
                  pl.BlockSpec((HALO_CONV, tf), lambda i, f: (prev(i), nf + f)),
                  pl.BlockSpec((3, tf), lambda i, f: (0, f)),
                  pl.BlockSpec((3, tf), lambda i, f: (0, nf + f)),
                  pl.BlockSpec((1, tf), lambda i, f: (0, f)),
                  pl.BlockSpec((1, tf), lambda i, f: (0, nf + f)),
                  pl.BlockSpec((tf, D), lambda i, f: (f, 0)),
                  pl.BlockSpec((T, D), lambda i, f: (i, 0))],
        out_specs=pl.BlockSpec((T, D), lambda i, f: (i, 0)),
        out_shape=jax.ShapeDtypeStruct((S, D), F32),
        scratch_shapes=[pltpu.VMEM((T, D), F32)],
        compiler_params=_params("parallel", "arbitrary"),
    )(h, h, h, h, conv_w, conv_w, conv_b, conv_b, w_down, x)


def _ffn_down_bwd(dxo, h, conv_w, conv_b, w_down, *, name):
    S, F2 = h.shape
    F, D = w_down.shape
    T, tf, nf = _ffn_tiles(S, F, 256)
    hb = T // HALO_CONV
    nt = S // T
    n_ext = T + HALO_CONV

    def body(dx_ref, hv_ref, hg_ref, pv_ref, pg_ref, cwv_ref, cwg_ref, cbv_ref, cbg_ref, wd_ref,
             dhv_ref, dhg_ref, dwd_ref, dcwv_ref, dcwg_ref, dcbv_ref, dcbg_ref, cv_scr, cg_scr):
        i = pl.program_id(1)
        first_tile = i == nt - 1

        @pl.when(i == 0)
        def _():
            cv_scr[...] = jnp.zeros_like(cv_scr)
            cg_scr[...] = jnp.zeros_like(cg_scr)
            dwd_ref[...] = jnp.zeros_like(dwd_ref)
            dcwv_ref[...] = jnp.zeros_like(dcwv_ref)
            dcwg_ref[...] = jnp.zeros_like(dcwg_ref)
            dcbv_ref[...] = jnp.zeros_like(dcbv_ref)
            dcbg_ref[...] = jnp.zeros_like(dcbg_ref)

        dxb = dx_ref[...].astype(BF16)
        hv, hg = hv_ref[...], hg_ref[...]
        val, hv1, hv2 = _conv3(hv, pv_ref[...], first_tile, cwv_ref[...], cbv_ref[...])
        gate, hg1, hg2 = _conv3(hg, pg_ref[...], first_tile, cwg_ref[...], cbg_ref[...])
        sg = _sigmoid(gate)
        sil = gate * sg
        dwd_ref[...] += _dot_tn((sil * val).astype(BF16), dxb)
        da = _dot_nt(dxb, wd_ref[...])
        dval = da * sil
        dgate = da * val * _dsilu(gate, sg)

        def conv_bwd(dhc, h0, h1, h2, cw, c_scr, dh_ref, dcw_ref, dcb_ref):
            ext = jnp.concatenate([dhc, c_scr[...]], axis=0)
            n1 = pltpu.roll(ext, n_ext - 1, 0)[:T, :]
            n2 = pltpu.roll(ext, n_ext - 2, 0)[:T, :]
            dh_ref[...] = (cw[2:3, :] * dhc + cw[1:2, :] * n1 + cw[0:1, :] * n2).astype(BF16)
            c_scr[...] = dhc[:HALO_CONV, :]
            dcw_ref[0:1, :] += jnp.sum(dhc * h2, axis=0, keepdims=True)
            dcw_ref[1:2, :] += jnp.sum(dhc * h1, axis=0, keepdims=True)
            dcw_ref[2:3, :] += jnp.sum(dhc * h0, axis=0, keepdims=True)
            dcb_ref[...] += jnp.sum(dhc, axis=0, keepdims=True)

        conv_bwd(dval, hv, hv1, hv2, cwv_ref[...], cv_scr, dhv_ref, dcwv_ref, dcbv_ref)
        conv_bwd(dgate, hg, hg1, hg2, cwg_ref[...], cg_scr, dhg_ref, dcwg_ref, dcbg_ref)

    rev = lambda i: nt - 1 - i
    prev = lambda i: jnp.maximum(rev(i) * hb - 1, 0)
    return pl.pallas_call(
        body, name=name, grid=(nf, nt),
        in_specs=[pl.BlockSpec((T, D), lambda f, i: (rev(i), 0)),
                  pl.BlockSpec((T, tf), lambda f, i: (rev(i), f)),
                  pl.BlockSpec((T, tf), lambda f, i: (rev(i), nf + f)),
                  pl.BlockSpec((HALO_CONV, tf), lambda f, i: (prev(i), f)),
                  pl.BlockSpec((HALO_CONV, tf), lambda f, i: (prev(i), nf + f)),
                  pl.BlockSpec((3, tf), lambda f, i: (0, f)),
                  pl.BlockSpec((3, tf), lambda f, i: (0, nf + f)),
                  pl.BlockSpec((1, tf), lambda f, i: (0, f)),
                  pl.BlockSpec((1, tf), lambda f, i: (0, nf + f)),
                  pl.BlockSpec((tf, D), lambda f, i: (f, 0))],
        out_specs=[pl.BlockSpec((T, tf), lambda f, i: (rev(i), f)),
                   pl.BlockSpec((T, tf), lambda f, i: (rev(i), f)),
                   pl.BlockSpec((tf, D), lambda f, i: (f, 0)),
                   pl.BlockSpec((3, tf), lambda f, i: (0, f)),
                   pl.BlockSpec((3, tf), lambda f, i: (0, f)),
                   pl.BlockSpec((1, tf), lambda f, i: (0, f)),
                   pl.BlockSpec((1, tf), lambda f, i: (0, f))],
        out_shape=[jax.ShapeDtypeStruct((S, F), BF16), jax.ShapeDtypeStruct((S, F), BF16),
                   jax.ShapeDtypeStruct((F, D), F32),
                   jax.ShapeDtypeStruct((3, F), F32), jax.ShapeDtypeStruct((3, F), F32),
                   jax.ShapeDtypeStruct((1, F), F32), jax.ShapeDtypeStruct((1, F), F32)],
        scratch_shapes=[pltpu.VMEM((HALO_CONV, tf), F32), pltpu.VMEM((HALO_CONV, tf), F32)],
        compiler_params=_params("parallel", "arbitrary"),
    )(dxo, h, h, h, h, conv_w, conv_w, conv_b, conv_b, w_down)


def _final_loss(x, g, target, *, name):
    S, D = x.shape
    T = _row_tile(S, 512)

    def body(x_ref, g_ref, t_ref, loss_ref, dx_ref, dg_ref):
        @pl.when(pl.program_id(0) == 0)
        def _():
            loss_ref[...] = jnp.zeros_like(loss_ref)
            dg_ref[...] = jnp.zeros_like(dg_ref)

        xf = x_ref[...]
        r = lax.rsqrt(jnp.mean(xf * xf, axis=-1, keepdims=True) + EPS)
        xhat = xf * r
        err = xhat * g_ref[...] - t_ref[...]
        loss_ref[...] += jnp.sum(err * err, axis=0, keepdims=True) * (0.5 / D)
        dy = err * (1.0 / D)
        dxhat = dy * g_ref[...]
        dx_ref[...] = r * (dxhat - xhat * jnp.mean(dxhat * xhat, axis=-1, keepdims=True))
        dg_ref[...] += jnp.sum(dy * xhat, axis=0, keepdims=True)

    return pl.pallas_call(
        body, name=name, grid=(S // T,),
        in_specs=[pl.BlockSpec((T, D), lambda i: (i, 0)), pl.BlockSpec((1, D), lambda i: (0, 0)),
                  pl.BlockSpec((T, D), lambda i: (i, 0))],
        out_specs=[pl.BlockSpec((1, D), lambda i: (0, 0)), pl.BlockSpec((T, D), lambda i: (i, 0)),
                   pl.BlockSpec((1, D), lambda i: (0, 0))],
        out_shape=[jax.ShapeDtypeStruct((1, D), F32), jax.ShapeDtypeStruct((S, D), F32),
                   jax.ShapeDtypeStruct((1, D), F32)],
        compiler_params=_params("arbitrary"),
    )(x, g, target)


def _lower_bounds(lb_logits):
    soft = jax.nn.softmax(lb_logits.astype(F32), axis=0)
    cum = jnp.cumsum(soft, axis=0)
    return cum - cum[0:1]


def _local_step(x, target, sm, big):
    L = sm["norm1_g"].shape[0]
    D = x.shape[1]
    P = sm["pool_scale"].shape[1]
    lbs, lb_vjp = jax.vjp(_lower_bounds, sm["lb_logits"])
    row = lambda a: a.reshape(1, -1)

    saved = []
    for l in range(L):
        w_in = big["w_in"][l]
        g1 = row(sm["norm1_g"][l])
        xn1, zu = _norm_matmul(x, g1, w_in[:, :P], name=f"in_pool_{l}", tn=P)
        _, zh = _norm_matmul(x, g1, w_in[:, P:5 * P], name=f"in_hgrn_{l}", tn=P)
        _, zg = _norm_matmul(x, g1, w_in[:, 5 * P:], name=f"in_gate_{l}", tn=P)
        pm = _pool_fwd(zu, sm["pool_w"][l], row(sm["pool_scale"][l]), name=f"pool_fwd_{l}")
        o_raw, of, states = _hgrn_fwd(zh, row(lbs[l]), row(sm["hgrn_norm_g"][l]), name=f"hgrn_fwd_{l}")
        x_mid, ya, yb = _mix_fwd(x, pm, of, zg, row(sm["b_gate"][l]), big["w_pa"][l], big["w_pb"][l],
                                 big["w_o"][l], name=f"mix_fwd_{l}")
        g2 = row(sm["norm2_g"][l])
        F2 = big["w_up"].shape[2]
        xn2, h = _norm_matmul(x_mid, g2, big["w_up"][l], name=f"up_{l}", tn=F2 // 4)
        x_out = _ffn_down_fwd(h, sm["conv_w"][l], row(sm["conv_b"][l]), big["w_down"][l], x_mid,
                              name=f"down_fwd_{l}")
        saved.append(dict(x=x, xn1=xn1, zu=zu, zh=zh, zg=zg, pm=pm, o_raw=o_raw, of=of, states=states,
                          x_mid=x_mid, ya=ya, yb=yb, xn2=xn2, h=h))
        x = x_out

    loss_cols, dx, d_final_g = _final_loss(x, row(sm["final_g"]), target, name="final_loss")

    grads = {k: [None] * L for k in ("norm1_g", "w_in", "b_gate", "pool_w", "pool_scale", "hgrn_norm_g", "w_pa",
                                     "w_pb", "w_o", "norm2_g", "w_up", "conv_w", "conv_b", "w_down")}
    dlbs = [None] * L
    for l in reversed(range(L)):
        s = saved[l]
        F2 = big["w_up"].shape[2]
        dhv, dhg, d_wd, dcwv, dcwg, dcbv, dcbg = _ffn_down_bwd(
            dx, s["h"], sm["conv_w"][l], row(sm["conv_b"][l]), big["w_down"][l], name=f"down_bwd_{l}")
        dh = jnp.concatenate([dhv, dhg], axis=1)
        grads["w_down"][l] = d_wd
        grads["conv_w"][l] = jnp.concatenate([dcwv, dcwg], axis=1)
        grads["conv_b"][l] = jnp.concatenate([dcbv, dcbg], axis=1)[0]
        grads["w_up"][l] = _wgrad(s["xn2"], dh, name=f"up_wgrad_{l}", tn=F2 // 4)
        dxm, dg2 = _dgrad_norm(dh, big["w_up"][l], s["x_mid"], row(sm["norm2_g"][l]), dx,
                               name=f"up_dgrad_{l}", tk=F2 // 11)
        grads["norm2_g"][l] = dg2[0]

        dzg, dpm, dof, d_wo, d_wpa, d_wpb, dbg = _mix_bwd(
            dxm, s["ya"], s["yb"], s["zg"], row(sm["b_gate"][l]), s["pm"], s["of"],
            big["w_pa"][l], big["w_pb"][l], big["w_o"][l], name=f"mix_bwd_{l}")
        grads["w_o"][l], grads["w_pa"][l], grads["w_pb"][l], grads["b_gate"][l] = d_wo, d_wpa, d_wpb, dbg[0]

        du, dpw, dps = _pool_bwd(s["zu"], dpm, sm["pool_w"][l], row(sm["pool_scale"][l]), name=f"pool_bwd_{l}")
        grads["pool_w"][l], grads["pool_scale"][l] = dpw, dps[0]

        dzq, dzf, dzi, dzo, dlb, dng = _hgrn_bwd(s["zh"], row(lbs[l]), row(sm["hgrn_norm_g"][l]), s["o_raw"],
                                                 s["states"], dof, name=f"hgrn_bwd_{l}")
        dlbs[l] = dlb[0]
        grads["hgrn_norm_g"][l] = jnp.sum(dng, axis=(0, 1))

        dz = jnp.concatenate([du, dzq, dzf, dzi, dzo, dzg], axis=1)
        grads["w_in"][l] = _wgrad(s["xn1"], dz, name=f"in_wgrad_{l}", tn=dz.shape[1] // 4)
        dx, dg1 = _dgrad_norm(dz, big["w_in"][l], s["x"], row(sm["norm1_g"][l]), dxm,
                              name=f"in_dgrad_{l}", tk=P)
        grads["norm1_g"][l] = dg1[0]

    out = {k: jnp.stack(v) for k, v in grads.items()}
    out["lb_logits"] = lb_vjp(jnp.stack(dlbs))[0]
    out["final_g"] = d_final_g[0]
    return loss_cols, dx, out


def _allgather8(xs, *, name):
    m_per, n = xs.shape

    def body(x_ref, out_ref, send_sems, recv_sems, local_sem):
        x, y, c = lax.axis_index("x"), lax.axis_index("y"), lax.axis_index("c")
        me, sibling = (x, y, c), (x, y, 1 - c)
        chips = [(1 - x, y), (x, 1 - y), (1 - x, 1 - y)]

        def rows(px, py, pc):
            return out_ref.at[pl.ds((4 * px + 2 * py + pc) * m_per, m_per), :]

        def copy(k, block, to, src=None):
            return pltpu.make_async_remote_copy(
                src_ref=rows(*block) if src is None else src, dst_ref=rows(*block),
                send_sem=send_sems.at[k], recv_sem=recv_sems.at[k], device_id=to, device_id_type=MESH_ID)

        mine = pltpu.make_async_copy(x_ref, rows(*me), local_sem)
        mine.start()
        first = [copy(0, me, sibling, src=x_ref)]
        first += [copy(1 + j, me, (*chip, c), src=x_ref) for j, chip in enumerate(chips)]
        for cp in first:
            cp.start()
        passed = [copy(4 + j, (*chip, c), sibling) for j, chip in enumerate(chips)]
        for j, chip in enumerate(chips):
            copy(1 + j, (*chip, c), me).wait_recv()
            passed[j].start()
        copy(0, sibling, me).wait_recv()
        for j, chip in enumerate(chips):
            copy(4 + j, (*chip, 1 - c), me).wait_recv()
        for cp in first + passed:
            cp.wait_send()
        mine.wait()

    return pl.pallas_call(
        body, name=name,
        out_shape=jax.ShapeDtypeStruct((8 * m_per, n), xs.dtype),
        in_specs=[pl.BlockSpec(memory_space=pl.ANY)],
        out_specs=pl.BlockSpec(memory_space=pl.ANY),
        scratch_shapes=[pltpu.SemaphoreType.DMA((7,)), pltpu.SemaphoreType.DMA((7,)), pltpu.SemaphoreType.DMA],
    )(xs)


def _shard_exchange(parts, *, name):
    _, R, n = parts.shape

    def body(p_ref, out_ref, send_sems, recv_sems):
        x, y, c = lax.axis_index("x"), lax.axis_index("y"), lax.axis_index("c")
        chips = [(1 - x, y), (x, 1 - y), (1 - x, 1 - y)]
        copies = []
        for j, (px, py) in enumerate(chips):
            cp = pltpu.make_async_remote_copy(
                src_ref=p_ref.at[2 * px + py], dst_ref=out_ref.at[j],
                send_sem=send_sems.at[j], recv_sem=recv_sems.at[j],
                device_id=(px, py, c), device_id_type=MESH_ID)
            cp.start()
            copies.append(cp)
        for cp in copies:
            cp.wait_recv()
        for cp in copies:
            cp.wait_send()

    return pl.pallas_call(
        body, name=name,
        out_shape=jax.ShapeDtypeStruct((3, R, n), parts.dtype),
        in_specs=[pl.BlockSpec(memory_space=pl.ANY)],
        out_specs=pl.BlockSpec(memory_space=pl.ANY),
        scratch_shapes=[pltpu.SemaphoreType.DMA((3,)), pltpu.SemaphoreType.DMA((3,))],
    )(parts)


def _sibling_exchange(a, *, name):
    def body(a_ref, out_ref, send_sem, recv_sem):
        x, y, c = lax.axis_index("x"), lax.axis_index("y"), lax.axis_index("c")
        cp = pltpu.make_async_remote_copy(src_ref=a_ref, dst_ref=out_ref, send_sem=send_sem, recv_sem=recv_sem,
                                          device_id=(x, y, 1 - c), device_id_type=MESH_ID)
        cp.start()
        cp.wait_recv()
        cp.wait_send()

    return pl.pallas_call(
        body, name=name,
        out_shape=jax.ShapeDtypeStruct(a.shape, a.dtype),
        in_specs=[pl.BlockSpec(memory_space=pl.ANY)],
        out_specs=pl.BlockSpec(memory_space=pl.ANY),
        scratch_shapes=[pltpu.SemaphoreType.DMA, pltpu.SemaphoreType.DMA],
    )(a)


def _elementwise_rows(R, n, n_arrays):
    if 2 * n_arrays * R * n * 4 <= VMEM_LIMIT // 4 or R % 8:
        return R
    want = 8
    while want * 2 * n * 4 <= 1024 * 1024:
        want *= 2
    return _row_tile(R, want)


def _sum_stack(first, rest, *, name):
    R, n = first.shape
    K = rest.shape[0]
    T = _elementwise_rows(R, n, K + 2)

    def body(a_ref, r_ref, o_ref):
        acc = a_ref[...]
        for j in range(K):
            acc = acc + r_ref[j].astype(F32)
        o_ref[...] = acc

    return pl.pallas_call(
        body, name=name, grid=(R // T,),
        in_specs=[pl.BlockSpec((T, n), lambda i: (i, 0)), pl.BlockSpec((K, T, n), lambda i: (0, i, 0))],
        out_specs=pl.BlockSpec((T, n), lambda i: (i, 0)),
        out_shape=jax.ShapeDtypeStruct((R, n), F32),
        compiler_params=_params("parallel"),
    )(first, rest)


def _adamw(w, g, m, v, *, name):
    R, n = w.shape
    T = _elementwise_rows(R, n, 7)

    def body(w_ref, g_ref, m_ref, v_ref, d_ref, mo_ref, vo_ref):
        g_ = g_ref[...]
        m_ = ADAM_B1 * m_ref[...] + (1.0 - ADAM_B1) * g_
        v_ = ADAM_B2 * v_ref[...] + (1.0 - ADAM_B2) * (g_ * g_)
        m_hat = m_ / (1.0 - ADAM_B1 ** ADAM_STEP)
        v_hat = v_ / (1.0 - ADAM_B2 ** ADAM_STEP)
        d_ref[...] = -ADAM_LR * (m_hat / (jnp.sqrt(v_hat) + ADAM_EPS) + ADAM_WD * w_ref[...])
        mo_ref[...] = m_
        vo_ref[...] = v_

    blk = pl.BlockSpec((T, n), lambda i: (i, 0))
    return pl.pallas_call(
        body, name=name, grid=(R // T,),
        in_specs=[blk] * 4, out_specs=[blk] * 3,
        out_shape=[jax.ShapeDtypeStruct((R, n), F32)] * 3,
        compiler_params=_params("parallel"),
    )(w, g, m, v)


BIG = ("w_in", "w_pa", "w_pb", "w_o", "w_up", "w_down")
ROW_SHARDED = ("w_o", "w_down")
SMALL = ("norm1_g", "b_gate", "pool_w", "pool_scale", "lb_logits", "hgrn_norm_g", "norm2_g", "conv_b", "final_g")
WEIGHTS = ("norm1_g", "w_in", "b_gate", "pool_w", "pool_scale", "lb_logits", "hgrn_norm_g", "w_pa", "w_pb", "w_o",
           "norm2_g", "w_up", "conv_w", "conv_b", "w_down", "final_g")
N_CHIPS = 4
PACK_ALIGN = 2 * 16 * LANES


def _pack(pieces):
    flat = []
    for a in pieces:
        a = a.reshape(-1)
        pad = (-a.shape[0]) % PACK_ALIGN
        flat.append(jnp.pad(a, (0, pad)) if pad else a)
    return jnp.concatenate(flat).reshape(-1, LANES)


def _unpack(buf, shapes):
    flat = buf.reshape(-1)
    out, off = [], 0
    for shp in shapes:
        size = 1
        for s in shp:
            size *= s
        out.append(flat[off:off + size].reshape(shp))
        off += size + (-size) % PACK_ALIGN
    return out


def _shard_of(name, full, k):
    if name in ROW_SHARDED:
        n = full.shape[1] // N_CHIPS
        return full[:, k * n:(k + 1) * n, :]
    n = full.shape[2] // N_CHIPS
    return full[:, :, k * n:(k + 1) * n]


def _assemble(name, shards):
    return jnp.concatenate(shards, axis=1 if name in ROW_SHARDED else 2)


def kernel(x, norm1_g, w_in, b_gate, pool_w, pool_scale, lb_logits, hgrn_norm_g, w_pa, w_pb, w_o, norm2_g, w_up, conv_w, conv_b, w_down, final_g, loss_target, m_norm1_g, m_w_in, m_b_gate, m_pool_w, m_pool_scale, m_lb_logits, m_hgrn_norm_g, m_w_pa, m_w_pb, m_w_o, m_norm2_g, m_w_up, m_conv_w, m_conv_b, m_w_down, m_final_g, v_norm1_g, v_w_in, v_b_gate, v_pool_w, v_pool_scale, v_lb_logits, v_hgrn_norm_g, v_w_pa, v_w_pb, v_w_o, v_norm2_g, v_w_up, v_conv_w, v_conv_b, v_w_down, v_final_g):
    env = dict(locals())
    w = {n: env[n] for n in WEIGHTS}
    m = {n: env["m_" + n] for n in WEIGHTS}
    v = {n: env["v_" + n] for n in WEIGHTS}
    ix, iy, ic = lax.axis_index("x"), lax.axis_index("y"), lax.axis_index("c")
    my_chip = 2 * ix + iy

    shard_shapes = [w[n].shape for n in BIG]
    mine16 = _pack([w[n].astype(BF16) for n in BIG])
    R = mine16.shape[0]
    half = lax.dynamic_slice_in_dim(mine16, ic * (R // 2), R // 2, axis=0)
    all16 = _allgather8(half, name="gather_weights").reshape(N_CHIPS, R, LANES)
    per_chip = [_unpack(all16[k], shard_shapes) for k in range(N_CHIPS)]
    big = {n: _assemble(n, [per_chip[k][i] for k in range(N_CHIPS)]) for i, n in enumerate(BIG)}

    cw_mine = _pack([w["conv_w"]])
    Rc = cw_mine.shape[0]
    cw_half = lax.dynamic_slice_in_dim(cw_mine, ic * (Rc // 2), Rc // 2, axis=0)
    cw_all = _allgather8(cw_half, name="gather_conv_w").reshape(N_CHIPS, Rc, LANES)
    conv_w_full = jnp.concatenate([_unpack(cw_all[k], [w["conv_w"].shape])[0] for k in range(N_CHIPS)], axis=2)

    sm = {n: w[n] for n in SMALL}
    sm["conv_w"] = conv_w_full
    loss_cols, grad_x, g_loc = _local_step(x[0], loss_target[0], sm, big)

    parts32 = jnp.stack([_pack([_shard_of(n, g_loc[n], k) for n in BIG]) for k in range(N_CHIPS)])
    got = _shard_exchange(parts32.astype(BF16), name="grad_exchange")
    own = lax.dynamic_index_in_dim(parts32, my_chip, axis=0, keepdims=False)
    sum_c = _sum_stack(own, got, name="grad_sum_chips")
    sib = _sibling_exchange(sum_c, name="grad_sibling")
    g_big = _unpack(_sum_stack(sum_c, sib[None], name="grad_sum_cores"), shard_shapes)

    small_names = list(SMALL)
    small_pieces = [g_loc[n] for n in small_names] + [g_loc["conv_w"], loss_cols]
    small_shapes = [a.shape for a in small_pieces]
    packed = _pack(small_pieces)
    Rs = packed.shape[0]
    everyone = _allgather8(packed, name="gather_small").reshape(8, Rs, LANES)
    g_small = _unpack(_sum_stack(everyone[0], everyone[1:], name="small_sum"), small_shapes)
    loss = jnp.sum(g_small[-1])
    cshard = w["conv_w"].shape[2]
    g = dict(zip(small_names, g_small[:len(small_names)]))
    g["conv_w"] = lax.dynamic_slice_in_dim(g_small[-2], my_chip * cshard, cshard, axis=2)
    g.update(dict(zip(BIG, g_big)))

    delta, new_m, new_v = {}, {}, {}
    sm_w = _pack([w[n] for n in small_names])
    sm_d, sm_m, sm_v = _adamw(sm_w, _pack([g[n] for n in small_names]), _pack([m[n] for n in small_names]),
                              _pack([v[n] for n in small_names]), name="adamw_small")
    shapes = [w[n].shape for n in small_names]
    for n, d_, m_, v_ in zip(small_names, _unpack(sm_d, shapes), _unpack(sm_m, shapes), _unpack(sm_v, shapes)):
        delta[n], new_m[n], new_v[n] = d_, m_, v_
    for n in BIG + ("conv_w",):
        shp = w[n].shape
        two_d = lambda a: a.reshape(-1, shp[-1])
        d_, m_, v_ = _adamw(two_d(w[n]), two_d(g[n]), two_d(m[n]), two_d(v[n]), name="adamw_" + n)
        delta[n], new_m[n], new_v[n] = d_.reshape(shp), m_.reshape(shp), v_.reshape(shp)

    return (loss, grad_x[None], *[g[n] for n in WEIGHTS], *[delta[n] for n in WEIGHTS],
            *[new_m[n] for n in WEIGHTS], *[new_v[n] for n in WEIGHTS])
```

```python
import jax
import jax.numpy as jnp
from jax import lax
from jax.experimental import pallas as pl
from jax.experimental.pallas import tpu as pltpu

F32 = jnp.float32
BF16 = jnp.bfloat16

EPS = 1e-6
CHUNK = 64
SUB = 16
LANES = 128
SUBLANES = 8
POOL_WINDOWS = (2, 4, 8, 16)
HALO_POOL = 16
HALO_CONV = 8
EXP_CLAMP = 80.0

ADAM_LR = 0.001
ADAM_B1 = 0.9
ADAM_B2 = 0.999
ADAM_EPS = 1e-08
ADAM_WD = 0.01
ADAM_STEP = 10

VMEM_LIMIT = 56 * 1024 * 1024
MESH_ID = pl.DeviceIdType.MESH
N_CHIPS = 4
ANY = pl.BlockSpec(memory_space=pl.ANY)


def _dot(a, b):
    return jnp.dot(a, b, preferred_element_type=F32)


def _dot_nt(a, b):
    return lax.dot_general(a, b, (((1,), (1,)), ((), ())), preferred_element_type=F32)


def _dot_tn(a, b):
    return lax.dot_general(a, b, (((0,), (0,)), ((), ())), preferred_element_type=F32)


def _sigmoid(x):
    return jax.nn.sigmoid(x)


def _dsilu(x, s):
    return s * (1.0 + x * (1.0 - s))


def _row_tile(rows, want):
    t = min(rows, want)
    while rows % t:
        t //= 2
    return t


def _place():
    x, y, c = lax.axis_index("x"), lax.axis_index("y"), lax.axis_index("c")
    chips = [(1 - x, y), (x, 1 - y), (1 - x, 1 - y)]
    return x, y, c, chips


def _remote(src, dst, sems, k, to):
    return pltpu.make_async_remote_copy(src_ref=src, dst_ref=dst, send_sem=sems[0].at[k], recv_sem=sems[1].at[k],
                                        device_id=to, device_id_type=MESH_ID)


class _GatherPlan:
    def __init__(self, items):
        self.items = items
        self.inputs = [a for a, _, _ in items]
        self.out_shapes = []
        for a, kind, _ in items:
            shp = (N_CHIPS,) + a.shape[1:] if kind == "rows" else (a.shape[0], N_CHIPS) + a.shape[1:]
            self.out_shapes.append(jax.ShapeDtypeStruct(shp, a.dtype))
        n = len(items)
        self.scratch = [pltpu.SemaphoreType.DMA((6 * n,)), pltpu.SemaphoreType.DMA((6 * n,)),
                        pltpu.SemaphoreType.DMA((2 * n,))]

    def _views(self, i, src, dst):
        _, kind, l = self.items[i]
        if kind == "rows":
            half = src.shape[1] // 2
            part = lambda core: src.at[l, pl.ds(core * half, half), :]
            land = lambda chip, core: dst.at[chip, pl.ds(core * half, half), :]
        else:
            part = lambda core: src.at[core]
            land = lambda chip, core: dst.at[core, chip]
        return part, land

    def start(self, srcs, dsts, sems):
        x, y, c, chips = _place()
        me = 2 * x + y
        for i, (src, dst) in enumerate(zip(srcs, dsts)):
            part, land = self._views(i, src, dst)
            for core in range(2):
                pltpu.make_async_copy(part(core), land(me, core), sems[2].at[2 * i + core]).start()
            for j, (px, py) in enumerate(chips):
                _remote(part(c), land(me, c), sems, 6 * i + j, (px, py, c)).start()

    def finish(self, srcs, dsts, sems):
        x, y, c, chips = _place()
        me = 2 * x + y
        sibling = (x, y, 1 - c)
        for i, (src, dst) in enumerate(zip(srcs, dsts)):
            part, land = self._views(i, src, dst)
            for j, (px, py) in enumerate(chips):
                got = land(2 * px + py, c)
                _remote(got, got, sems, 6 * i + j, (px, py, c)).wait_recv()
                _remote(got, got, sems, 6 * i + 3 + j, sibling).start()
        for i, (src, dst) in enumerate(zip(srcs, dsts)):
            part, land = self._views(i, src, dst)
            for j, (px, py) in enumerate(chips):
                got = land(2 * px + py, 1 - c)
                _remote(got, got, sems, 6 * i + 3 + j, sibling).wait_recv()
            for j, (px, py) in enumerate(chips):
                _remote(part(c), land(me, c), sems, 6 * i + j, (px, py, c)).wait_send()
                mine = land(2 * px + py, c)
                _remote(mine, mine, sems, 6 * i + 3 + j, sibling).wait_send()
            for core in range(2):
                pltpu.make_async_copy(part(core), land(me, core), sems[2].at[2 * i + core]).wait()


class _ScatterPlan:
    def __init__(self, items):
        self.inputs = list(items)
        self.out_shapes = [jax.ShapeDtypeStruct((3,) + a.shape[1:], a.dtype) for a in items]
        n = len(items)
        self.scratch = [pltpu.SemaphoreType.DMA((3 * n,)), pltpu.SemaphoreType.DMA((3 * n,))]

    def _copies(self, srcs, dsts, sems):
        x, y, c, chips = _place()
        return [_remote(src.at[2 * px + py], dst.at[j], sems, 3 * i + j, (px, py, c))
                for i, (src, dst) in enumerate(zip(srcs, dsts)) for j, (px, py) in enumerate(chips)]

    def start(self, srcs, dsts, sems):
        for cp in self._copies(srcs, dsts, sems):
            cp.start()

    def finish(self, srcs, dsts, sems):
        copies = self._copies(srcs, dsts, sems)
        for cp in copies:
            cp.wait_recv()
        for cp in copies:
            cp.wait_send()


class _SiblingPlan:
    def __init__(self, items):
        self.inputs = list(items)
        self.out_shapes = [jax.ShapeDtypeStruct(a.shape, a.dtype) for a in items]
        n = len(items)
        self.scratch = [pltpu.SemaphoreType.DMA((n,)), pltpu.SemaphoreType.DMA((n,))]

    def _copies(self, srcs, dsts, sems):
        x, y, c, _ = _place()
        return [_remote(src, dst, sems, i, (x, y, 1 - c)) for i, (src, dst) in enumerate(zip(srcs, dsts))]

    def start(self, srcs, dsts, sems):
        for cp in self._copies(srcs, dsts, sems):
            cp.start()

    def finish(self, srcs, dsts, sems):
        copies = self._copies(srcs, dsts, sems)
        for cp in copies:
            cp.wait_recv()
        for cp in copies:
            cp.wait_send()


class _EveryonePlan:
    def __init__(self, block):
        self.inputs = [block]
        self.m = block.shape[0]
        self.out_shapes = [jax.ShapeDtypeStruct((8 * self.m,) + block.shape[1:], block.dtype)]
        self.scratch = [pltpu.SemaphoreType.DMA((7,)), pltpu.SemaphoreType.DMA((7,)), pltpu.SemaphoreType.DMA((1,))]

    def _rows(self, dst, px, py, pc):
        return dst.at[pl.ds((4 * px + 2 * py + pc) * self.m, self.m), :]

    def start(self, srcs, dsts, sems):
        x, y, c, chips = _place()
        src, dst = srcs[0], dsts[0]
        pltpu.make_async_copy(src, self._rows(dst, x, y, c), sems[2].at[0]).start()
        _remote(src, self._rows(dst, x, y, c), sems, 0, (x, y, 1 - c)).start()
        for j, (px, py) in enumerate(chips):
            _remote(src, self._rows(dst, x, y, c), sems, 1 + j, (px, py, c)).start()

    def finish(self, srcs, dsts, sems):
        x, y, c, chips = _place()
        src, dst = srcs[0], dsts[0]
        sibling = (x, y, 1 - c)
        for j, (px, py) in enumerate(chips):
            got = self._rows(dst, px, py, c)
            _remote(got, got, sems, 1 + j, (px, py, c)).wait_recv()
            _remote(got, got, sems, 4 + j, sibling).start()
        sib = self._rows(dst, x, y, 1 - c)
        _remote(sib, sib, sems, 0, sibling).wait_recv()
        for j, (px, py) in enumerate(chips):
            got = self._rows(dst, px, py, 1 - c)
            _remote(got, got, sems, 4 + j, sibling).wait_recv()
        mine = self._rows(dst, x, y, c)
        _remote(src, mine, sems, 0, sibling).wait_send()
        for j, (px, py) in enumerate(chips):
            _remote(src, mine, sems, 1 + j, (px, py, c)).wait_send()
            got = self._rows(dst, px, py, c)
            _remote(got, got, sems, 4 + j, sibling).wait_send()
        pltpu.make_async_copy(src, mine, sems[2].at[0]).wait()


def _call(body, *, name, grid, in_specs, out_specs, out_shape, args, scratch=(), parallel=(), plan=None):
    n_in, n_out, n_scr = len(in_specs), len(out_shape), len(scratch)
    sem = tuple("parallel" if (a in parallel and plan is None) else "arbitrary" for a in range(len(grid)))
    params = pltpu.CompilerParams(dimension_semantics=sem, vmem_limit_bytes=VMEM_LIMIT)
    if plan is None:
        outs = pl.pallas_call(body, name=name, grid=grid, in_specs=in_specs, out_specs=out_specs,
                              out_shape=out_shape, scratch_shapes=list(scratch), compiler_params=params)(*args)
        return list(outs), []
    p_in, p_out, p_scr = len(plan.inputs), len(plan.out_shapes), len(plan.scratch)

    def wrapped(*refs):
        ins, refs = refs[:n_in], refs[n_in:]
        p_ins, refs = refs[:p_in], refs[p_in:]
        outs, refs = refs[:n_out], refs[n_out:]
        p_outs, refs = refs[:p_out], refs[p_out:]
        scr, p_sems = refs[:n_scr], refs[n_scr:]
        ids = [pl.program_id(a) for a in range(len(grid))]
        first = functools_reduce_and([i == 0 for i in ids])
        last = functools_reduce_and([i == n - 1 for i, n in zip(ids, grid)])

        @pl.when(first)
        def _():
            plan.start(p_ins, p_outs, p_sems)

        body(*ins, *outs, *scr)

        @pl.when(last)
        def _():
            plan.finish(p_ins, p_outs, p_sems)

    outs = pl.pallas_call(
        wrapped, name=name, grid=grid,
        in_specs=list(in_specs) + [ANY] * p_in, out_specs=list(out_specs) + [ANY] * p_out,
        out_shape=list(out_shape) + list(plan.out_shapes),
        scratch_shapes=list(scratch) + list(plan.scratch), compiler_params=params,
    )(*args, *plan.inputs)
    return list(outs[:n_out]), list(outs[n_out:])


def functools_reduce_and(conds):
    out = conds[0]
    for c in conds[1:]:
        out = out & c
    return out


def _run_plan(plan, *, name):
    p_in, p_out = len(plan.inputs), len(plan.out_shapes)

    def body(*refs):
        srcs, dsts, sems = refs[:p_in], refs[p_in:p_in + p_out], refs[p_in + p_out:]
        plan.start(srcs, dsts, sems)
        plan.finish(srcs, dsts, sems)

    return list(pl.pallas_call(body, name=name, in_specs=[ANY] * p_in, out_specs=[ANY] * p_out,
                               out_shape=list(plan.out_shapes), scratch_shapes=list(plan.scratch))(*plan.inputs))


def _norm_matmul(x, g, w4, *, name, plan=None):
    S, D = x.shape
    n4 = w4.shape[2]
    tm = _row_tile(S, 512)

    def body(x_ref, g_ref, w_ref, xn_ref, o_ref):
        @pl.when(pl.program_id(1) == 0)
        def _():
            xf = x_ref[...]
            r = lax.rsqrt(jnp.mean(xf * xf, axis=-1, keepdims=True) + EPS)
            xn_ref[...] = (xf * r * g_ref[...]).astype(BF16)

        o_ref[...] = _dot(xn_ref[...], w_ref[...])

    return _call(
        body, name=name, grid=(S // tm, N_CHIPS), parallel=(0,), plan=plan,
        in_specs=[pl.BlockSpec((tm, D), lambda i, j: (i, 0)),
                  pl.BlockSpec((1, D), lambda i, j: (0, 0)),
                  pl.BlockSpec((None, D, n4), lambda i, j: (j, 0, 0))],
        out_specs=[pl.BlockSpec((tm, D), lambda i, j: (i, 0)),
                   pl.BlockSpec((tm, n4), lambda i, j: (i, j))],
        out_shape=[jax.ShapeDtypeStruct((S, D), BF16), jax.ShapeDtypeStruct((S, N_CHIPS * n4), F32)],
        args=(x, g, w4))


def _piece_specs(pieces, n4, tm, row_axis):
    specs, ranges, k0 = [], [], 0
    for p in pieces:
        nk = p.shape[1] // n4
        if row_axis == 0:
            imap = lambda i, k, k0=k0, nk=nk: (i, jnp.clip(k - k0, 0, nk - 1))
        else:
            imap = lambda k, i, k0=k0, nk=nk: (jnp.where((k >= k0) & (k < k0 + nk), i, 0), jnp.clip(k - k0, 0, nk - 1))
        specs.append(pl.BlockSpec((tm, n4), imap))
        ranges.append((k0, k0 + nk))
        k0 += nk
    assert k0 == N_CHIPS
    return specs, ranges


def _dgrad_norm(dys, w4, x, g, dres, *, name, plan=None):
    S, D = x.shape
    n4 = w4.shape[2]
    tm = _row_tile(S, 512)
    specs, ranges = _piece_specs(dys, n4, tm, 0)
    n_p = len(dys)

    def body(*refs):
        dy_refs = refs[:n_p]
        w_ref, x_ref, g_ref, dres_ref, dx_ref, dg_ref, acc_ref = refs[n_p:]
        i, k = pl.program_id(0), pl.program_id(1)

        @pl.when(k == 0)
        def _():
            acc_ref[...] = jnp.zeros_like(acc_ref)

        @pl.when((i == 0) & (k == 0))
        def _():
            dg_ref[...] = jnp.zeros_like(dg_ref)

        for dy_ref, (lo, hi) in zip(dy_refs, ranges):
            @pl.when((k >= lo) & (k < hi))
            def _():
                acc_ref[...] += _dot_nt(dy_ref[...], w_ref[...])

        @pl.when(k == N_CHIPS - 1)
        def _():
            dxn = acc_ref[...]
            xf = x_ref[...]
            r = lax.rsqrt(jnp.mean(xf * xf, axis=-1, keepdims=True) + EPS)
            xhat = xf * r
            dxhat = dxn * g_ref[...]
            dx_ref[...] = dres_ref[...] + r * (dxhat - xhat * jnp.mean(dxhat * xhat, axis=-1, keepdims=True))
            dg_ref[...] += jnp.sum(dxn * xhat, axis=0, keepdims=True)

    return _call(
        body, name=name, grid=(S // tm, N_CHIPS), plan=plan,
        in_specs=specs + [pl.BlockSpec((None, D, n4), lambda i, k: (k, 0, 0)),
                          pl.BlockSpec((tm, D), lambda i, k: (i, 0)),
                          pl.BlockSpec((1, D), lambda i, k: (0, 0)),
                          pl.BlockSpec((tm, D), lambda i, k: (i, 0))],
        out_specs=[pl.BlockSpec((tm, D), lambda i, k: (i, 0)),
                   pl.BlockSpec((1, D), lambda i, k: (0, 0))],
        out_shape=[jax.ShapeDtypeStruct((S, D), F32), jax.ShapeDtypeStruct((1, D), F32)],
        scratch=[pltpu.VMEM((tm, D), F32)],
        args=(*dys, w4, x, g, dres))


def _wgrad(a, dys, *, name):
    S, K = a.shape
    n4 = sum(p.shape[1] for p in dys) // N_CHIPS
    tm = _row_tile(S, 512)
    ns = S // tm
    specs, ranges = _piece_specs(dys, n4, tm, 1)
    n_p = len(dys)

    def body(*refs):
        a_ref = refs[0]
        dy_refs = refs[1:1 + n_p]
        o_ref, o16_ref = refs[1 + n_p:]
        n, s = pl.program_id(0), pl.program_id(1)

        @pl.when(s == 0)
        def _():
            o_ref[...] = jnp.zeros_like(o_ref)

        for dy_ref, (lo, hi) in zip(dy_refs, ranges):
            @pl.when((n >= lo) & (n < hi))
            def _():
                o_ref[...] += _dot_tn(a_ref[...], dy_ref[...])

        @pl.when(s == ns - 1)
        def _():
            o16_ref[...] = o_ref[...].astype(BF16)

    out = pl.BlockSpec((None, K, n4), lambda n, s: (n, 0, 0))
    return _call(
        body, name=name, grid=(N_CHIPS, ns), parallel=(0,),
        in_specs=[pl.BlockSpec((tm, K), lambda n, s: (s, 0))] + specs,
        out_specs=[out, out],
        out_shape=[jax.ShapeDtypeStruct((N_CHIPS, K, n4), F32), jax.ShapeDtypeStruct((N_CHIPS, K, n4), BF16)],
        args=(a, *dys))[0]


def _tiles(x):
    return x.reshape(x.shape[0] // SUBLANES, SUBLANES, x.shape[1])


def _shift_down(xp, s):
    n = xp.shape[0] - SUBLANES
    if s == SUBLANES:
        return xp[:n, :]
    t = _tiles(xp)
    rot = pltpu.roll(t, s, 1)
    sub = lax.broadcasted_iota(jnp.int32, t.shape, 1)[1:]
    return jnp.where(sub >= s, rot[1:], rot[:-1]).reshape(n, xp.shape[1])


def _shift_up(xn, s):
    n = xn.shape[0] - SUBLANES
    if s == SUBLANES:
        return xn[SUBLANES:, :]
    t = _tiles(xn)
    rot = pltpu.roll(t, SUBLANES - s, 1)
    sub = lax.broadcasted_iota(jnp.int32, t.shape, 1)[1:]
    return jnp.where(sub < SUBLANES - s, rot[:-1], rot[1:]).reshape(n, xn.shape[1])


def _pooled(u, halo, first_tile, row0):
    T = u.shape[0]
    halo = jnp.where(first_tile, 0.0, halo)
    pad = jnp.zeros((SUBLANES, u.shape[1]), F32)
    up = jnp.concatenate([pad, halo, u], axis=0)
    t1 = (row0 + lax.broadcasted_iota(jnp.int32, (T, 1), 0) + 1).astype(F32)
    outs = []
    for gi, w in enumerate(POOL_WINDOWS):
        s = up[:, gi * LANES:(gi + 1) * LANES]
        k = 1
        while k < w:
            if k < SUBLANES:
                s = jnp.concatenate([s[:SUBLANES, :], s[SUBLANES:, :] + _shift_down(s, k)], axis=0)
            else:
                s = s[SUBLANES:, :] + _shift_down(s, k)
            k *= 2
        s = s[-T:, :]
        inv = 1.0 / jnp.minimum(t1, float(w))
        outs.append(s * inv - u[:, gi * LANES:(gi + 1) * LANES])
    return outs


def _pool_fwd(z, pool_w, pool_scale, *, name):
    S = z.shape[0]
    P = pool_scale.shape[1]
    T = _row_tile(S, 512)
    hb = T // HALO_POOL

    def body(u_ref, halo_ref, pw_ref, ps_ref, o_ref):
        i = pl.program_id(0)
        pooled = _pooled(u_ref[...], halo_ref[...], i == 0, i * T)
        for gi in range(len(POOL_WINDOWS)):
            mixed = _dot(pooled[gi].astype(BF16), pw_ref[gi].astype(BF16))
            cols = slice(gi * LANES, (gi + 1) * LANES)
            o_ref[:, cols] = (mixed * ps_ref[:, cols]).astype(BF16)

    return _call(
        body, name=name, grid=(S // T,), parallel=(0,),
        in_specs=[pl.BlockSpec((T, P), lambda i: (i, 0)),
                  pl.BlockSpec((HALO_POOL, P), lambda i: (jnp.maximum(i * hb - 1, 0), 0)),
                  pl.BlockSpec(pool_w.shape, lambda i: (0, 0, 0)),
                  pl.BlockSpec((1, P), lambda i: (0, 0))],
        out_specs=[pl.BlockSpec((T, P), lambda i: (i, 0))],
        out_shape=[jax.ShapeDtypeStruct((S, P), BF16)],
        args=(z, z, pool_w, pool_scale))[0][0]


def _pool_bwd(z, dpm, pool_w, pool_scale, *, name):
    S, P = dpm.shape
    T = _row_tile(S, 512)
    hb = T // HALO_POOL
    nt = S // T

    def body(u_ref, halo_ref, d_ref, dnext_ref, pw_ref, ps_ref, du_ref, dpw_ref, dps_ref):
        i = pl.program_id(0)

        @pl.when(i == 0)
        def _():
            dpw_ref[...] = jnp.zeros_like(dpw_ref)
            dps_ref[...] = jnp.zeros_like(dps_ref)

        pooled = _pooled(u_ref[...], halo_ref[...], i == 0, i * T)
        dnext = jnp.where(i == nt - 1, 0.0, dnext_ref[...])
        pad = jnp.zeros((SUBLANES, P), F32)
        dext = jnp.concatenate([d_ref[...], dnext, pad], axis=0)
        t1 = (i * T + lax.broadcasted_iota(jnp.int32, (T + HALO_POOL + SUBLANES, 1), 0) + 1).astype(F32)
        for gi, w in enumerate(POOL_WINDOWS):
            cols = slice(gi * LANES, (gi + 1) * LANES)
            pw = pw_ref[gi].astype(BF16)
            pg = pooled[gi].astype(BF16)
            mixed = _dot(pg, pw)
            dps_ref[:, cols] += jnp.sum(d_ref[:, cols] * mixed, axis=0, keepdims=True)
            dmixed = (dext[:, cols] * ps_ref[:, cols]).astype(BF16)
            dpw_ref[gi] += _dot_tn(pg, dmixed[:T, :])
            dpooled = _dot_nt(dmixed, pw)
            e = dpooled * (1.0 / jnp.minimum(t1, float(w)))
            k = 1
            while k < w:
                if k < SUBLANES:
                    e = jnp.concatenate([e[:-SUBLANES, :] + _shift_up(e, k), e[-SUBLANES:, :]], axis=0)
                else:
                    e = e[:-SUBLANES, :] + _shift_up(e, k)
                k *= 2
            du_ref[:, cols] = (e[:T, :] - dpooled[:T, :]).astype(BF16)

    return _call(
        body, name=name, grid=(nt,),
        in_specs=[pl.BlockSpec((T, P), lambda i: (i, 0)),
                  pl.BlockSpec((HALO_POOL, P), lambda i: (jnp.maximum(i * hb - 1, 0), 0)),
                  pl.BlockSpec((T, P), lambda i: (i, 0)),
                  pl.BlockSpec((HALO_POOL, P), lambda i: (jnp.minimum((i + 1) * hb, S // HALO_POOL - 1), 0)),
                  pl.BlockSpec(pool_w.shape, lambda i: (0, 0, 0)),
                  pl.BlockSpec((1, P), lambda i: (0, 0))],
        out_specs=[pl.BlockSpec((T, P), lambda i: (i, 0)),
                   pl.BlockSpec(pool_w.shape, lambda i: (0, 0, 0)),
                   pl.BlockSpec((1, P), lambda i: (0, 0))],
        out_shape=[jax.ShapeDtypeStruct((S, P), BF16),
                   jax.ShapeDtypeStruct(pool_w.shape, F32),
                   jax.ShapeDtypeStruct((1, P), F32)],
        args=(z, z, dpm, dpm, pool_w, pool_scale))[0]


def _cumsum_rows(x):
    n = x.shape[0]
    row = lax.broadcasted_iota(jnp.int32, x.shape, 0)
    s = 1
    while s < n:
        x = x + jnp.where(row >= s, pltpu.roll(x, s, 0), 0.0)
        s *= 2
    return x


def _rev_cumsum_rows(x):
    n = x.shape[0]
    row = lax.broadcasted_iota(jnp.int32, x.shape, 0)
    s = 1
    while s < n:
        x = x + jnp.where(row < n - s, pltpu.roll(x, n - s, 0), 0.0)
        s *= 2
    return x


def _chunk_prep(zq, zf, lb, b_ref):
    n_sub = CHUNK // SUB
    sq = _sigmoid(zq)
    q = zq * sq
    sf = _sigmoid(zf)
    f = lb + (1.0 - lb) * sf
    k = 1.0 - f
    b = _cumsum_rows(jnp.log(f))
    b_ref[...] = b
    shape = (SUB, b.shape[1])
    ends = [jnp.broadcast_to(b_ref[pl.ds(SUB * j + SUB - 1, 1), :], shape) for j in range(n_sub)]
    mids = [jnp.broadcast_to(b_ref[pl.ds(SUB * j + SUB // 2 - 1, 1), :], shape) for j in range(n_sub)]
    own = [b[SUB * j:SUB * (j + 1), :] for j in range(n_sub)]
    m0 = jnp.concatenate(mids, axis=0)
    e1 = jnp.concatenate(ends, axis=0)
    eq = [jnp.exp(jnp.minimum(b - m0, EXP_CLAMP))]
    for d in range(1, n_sub):
        rd = jnp.concatenate([own[j] if j < d else ends[j - d] for j in range(n_sub)], axis=0)
        eq.append(jnp.exp(b - rd))
    ek0 = jnp.exp(jnp.minimum(m0 - b, EXP_CLAMP))
    ek1 = jnp.exp(e1 - b)
    b_last = b_ref[pl.ds(CHUNK - 1, 1), :]
    return dict(q=q, k=k, f=f, sq=sq, sf=sf, b=b, eq=eq, ek0=ek0, ek1=ek1,
                eb=jnp.exp(b), ekl=jnp.exp(b_last - b), el=jnp.exp(b_last))


def _chunk_masks():
    ti = lax.broadcasted_iota(jnp.int32, (CHUNK, CHUNK), 0)
    si = lax.broadcasted_iota(jnp.int32, (CHUNK, CHUNK), 1)
    shift = SUB.bit_length() - 1
    dsub = jnp.right_shift(ti, shift) - jnp.right_shift(si, shift)
    masks = [(dsub == 0) & (si <= ti)]
    masks += [dsub == d for d in range(1, CHUNK // SUB)]
    return masks


def _chunk_attn(p, masks):
    qd = [(p["q"] * e).astype(BF16) for e in p["eq"]]
    k0 = (p["k"] * p["ek0"]).astype(BF16)
    k1 = (p["k"] * p["ek1"]).astype(BF16)
    a = jnp.where(masks[0], _dot_nt(qd[0], k0), 0.0)
    for d in range(1, len(masks)):
        a = jnp.where(masks[d], _dot_nt(qd[d], k1), a)
    return a, qd, k0, k1


def _hgrn_fwd(z, lb, norm_g, *, name, plan=None):
    S = z.shape[0]
    HW = lb.shape[1]
    NH = HW // LANES
    T = _row_tile(S, 512)
    nc = T // CHUNK

    def body(zq_ref, zf_ref, zi_ref, zo_ref, lb_ref, ng_ref, o_ref, of_ref, st_ref, s_scr, b_scr):
        @pl.when(pl.program_id(0) == 0)
        def _():
            s_scr[...] = jnp.zeros_like(s_scr)

        ng = ng_ref[...]
        masks = _chunk_masks()

        def chunk(c, carry):
            rows = pl.ds(pl.multiple_of(c * CHUNK, CHUNK), CHUNK)
            for h in range(NH):
                cols = slice(h * LANES, (h + 1) * LANES)
                p = _chunk_prep(zq_ref[rows, cols], zf_ref[rows, cols], lb_ref[:, cols], b_scr.at[h])
                v = zi_ref[rows, cols].astype(BF16)
                zo = zo_ref[rows, cols]
                st = s_scr[h]
                st_ref[c, h] = st
                a, _, _, _ = _chunk_attn(p, masks)
                o = _dot(a.astype(BF16), v) + _dot_nt((p["q"] * p["eb"]).astype(BF16), st.astype(BF16))
                s_scr[h] = st * p["el"] + _dot_tn(v, (p["k"] * p["ekl"]).astype(BF16))
                o_ref[rows, cols] = o
                r = lax.rsqrt(jnp.mean(o * o, axis=-1, keepdims=True) + EPS)
                of_ref[rows, cols] = (o * r * ng * (zo * _sigmoid(zo))).astype(BF16)
            return carry

        lax.fori_loop(0, nc, chunk, 0)

    part = lambda k: pl.BlockSpec((T, HW), lambda i, k=k: (i, k))
    return _call(
        body, name=name, grid=(S // T,), plan=plan,
        in_specs=[part(1), part(2), part(3), part(4),
                  pl.BlockSpec((1, HW), lambda i: (0, 0)), pl.BlockSpec((1, LANES), lambda i: (0, 0))],
        out_specs=[pl.BlockSpec((T, HW), lambda i: (i, 0)), pl.BlockSpec((T, HW), lambda i: (i, 0)),
                   pl.BlockSpec((nc, NH, LANES, LANES), lambda i: (i, 0, 0, 0))],
        out_shape=[jax.ShapeDtypeStruct((S, HW), F32), jax.ShapeDtypeStruct((S, HW), BF16),
                   jax.ShapeDtypeStruct((S // CHUNK, NH, LANES, LANES), F32)],
        scratch=[pltpu.VMEM((NH, LANES, LANES), F32), pltpu.VMEM((NH, CHUNK, LANES), F32)],
        args=(z, z, z, z, lb, norm_g))


def _hgrn_bwd(z, lb, norm_g, o_raw, states, dof, *, name, plan=None):
    S = z.shape[0]
    HW = lb.shape[1]
    NH = HW // LANES
    T = _row_tile(S, 512)
    nc = T // CHUNK
    nt = S // T

    def body(zq_ref, zf_ref, zi_ref, zo_ref, lb_ref, ng_ref, o_ref, st_ref, dof_ref,
             dzq_ref, dzf_ref, dzi_ref, dzo_ref, dlb_ref, dng_ref, ds_scr, b_scr):
        @pl.when(pl.program_id(0) == 0)
        def _():
            ds_scr[...] = jnp.zeros_like(ds_scr)
            dlb_ref[...] = jnp.zeros_like(dlb_ref)
            dng_ref[...] = jnp.zeros_like(dng_ref)

        ng = ng_ref[...]
        masks = _chunk_masks()
        last_row = lax.broadcasted_iota(jnp.int32, (CHUNK, 1), 0) == CHUNK - 1

        def chunk(cr, carry):
            c = nc - 1 - cr
            rows = pl.ds(pl.multiple_of(c * CHUNK, CHUNK), CHUNK)
            for h in range(NH):
                cols = slice(h * LANES, (h + 1) * LANES)
                lbv = lb_ref[:, cols]
                zq, zf, zo = zq_ref[rows, cols], zf_ref[rows, cols], zo_ref[rows, cols]
                o = o_ref[rows, cols]
                dof_c = dof_ref[rows, cols]
                st = st_ref[c, h]
                dst = ds_scr[h]

                so = _sigmoid(zo)
                r = lax.rsqrt(jnp.mean(o * o, axis=-1, keepdims=True) + EPS)
                ohat = o * r
                d_on = dof_c * (zo * so)
                dzo_ref[rows, cols] = (dof_c * ohat * ng * _dsilu(zo, so)).astype(BF16)
                dng_ref[:, cols] += jnp.sum(d_on * ohat, axis=0, keepdims=True)
                dohat = d_on * ng
                do = (r * (dohat - ohat * jnp.mean(dohat * ohat, axis=-1, keepdims=True))).astype(BF16)

                p = _chunk_prep(zq, zf, lbv, b_scr.at[h])
                q, k = p["q"], p["k"]
                v = zi_ref[rows, cols].astype(BF16)
                a, qd, k0, k1 = _chunk_attn(p, masks)
                ktl = (k * p["ekl"]).astype(BF16)
                dstb = dst.astype(BF16)

                da = _dot_nt(do, v)
                dzi_ref[rows, cols] = (_dot_tn(a.astype(BF16), do) + _dot_nt(ktl, dstb)).astype(BF16)

                da0 = jnp.where(masks[0], da, 0.0).astype(BF16)
                rq = _dot(da0, k0)
                rk0 = _dot_tn(da0, qd[0])
                dq = rq * p["eq"][0]
                db = qd[0].astype(F32) * rq - k0.astype(F32) * rk0
                rk1 = jnp.zeros_like(rk0)
                for d in range(1, len(masks)):
                    dad = jnp.where(masks[d], da, 0.0).astype(BF16)
                    rq = _dot(dad, k1)
                    dq = dq + rq * p["eq"][d]
                    db = db + qd[d].astype(F32) * rq
                    rk1 = rk1 + _dot_tn(dad, qd[d])
                dk = rk0 * p["ek0"] + rk1 * p["ek1"]
                db = db - k1.astype(F32) * rk1
                qe = (q * p["eb"]).astype(BF16)
                rq = _dot(do, st.astype(BF16))
                dq = dq + rq * p["eb"]
                db = db + qe.astype(F32) * rq
                rk = _dot(v, dstb)
                dk = dk + rk * p["ekl"]
                db = db - ktl.astype(F32) * rk

                st_new = st * p["el"] + _dot_tn(v, ktl)
                db = db + jnp.where(last_row, jnp.sum(dstb.astype(F32) * st_new, axis=0, keepdims=True), 0.0)
                dg = _rev_cumsum_rows(db)
                ds_scr[h] = dst * p["el"] + _dot_tn(do, qe)

                dzq_ref[rows, cols] = (dq * _dsilu(zq, p["sq"])).astype(BF16)
                df = dg / p["f"] - dk
                sf = p["sf"]
                dzf_ref[rows, cols] = (df * (1.0 - lbv) * sf * (1.0 - sf)).astype(BF16)
                dlb_ref[:, cols] += jnp.sum(df * (1.0 - sf), axis=0, keepdims=True)
            return carry

        lax.fori_loop(0, nc, chunk, 0)

    rev = lambda i: nt - 1 - i
    part = lambda k: pl.BlockSpec((T, HW), lambda i, k=k: (rev(i), k))
    blk = pl.BlockSpec((T, HW), lambda i: (rev(i), 0))
    vec = pl.BlockSpec((1, HW), lambda i: (0, 0))
    return _call(
        body, name=name, grid=(nt,), plan=plan,
        in_specs=[part(1), part(2), part(3), part(4), vec, pl.BlockSpec((1, LANES), lambda i: (0, 0)),
                  blk, pl.BlockSpec((nc, NH, LANES, LANES), lambda i: (rev(i), 0, 0, 0)), blk],
        out_specs=[blk, blk, blk, blk, vec, vec],
        out_shape=[jax.ShapeDtypeStruct((S, HW), BF16)] * 4 + [jax.ShapeDtypeStruct((1, HW), F32)] * 2,
        scratch=[pltpu.VMEM((NH, LANES, LANES), F32), pltpu.VMEM((NH, CHUNK, LANES), F32)],
        args=(z, z, z, z, lb, norm_g, o_raw, states, dof))


def _gate_specs(T, D):
    half = D // 2
    first = (5 * half) // half
    return [pl.BlockSpec((T, half), lambda i, k=k: (i, first + k)) for k in range(4)]


def _gates(zg_refs, bg_ref, D):
    half = D // 2
    za = jnp.concatenate([zg_refs[0][...], zg_refs[1][...]], axis=1) + bg_ref[:, :D]
    zb = jnp.concatenate([zg_refs[2][...], zg_refs[3][...]], axis=1) + bg_ref[:, D:]
    return _sigmoid(za), _sigmoid(zb)


def _mix_fwd(x, pm, of, z, b_gate, w_pa4, w_pb4, w_o4, *, name):
    S, D = x.shape
    P = pm.shape[1]
    T = _row_tile(S, 256)

    def body(x_ref, pm_ref, of_ref, g0, g1, g2, g3, bg_ref, wpa_ref, wpb_ref, wo_ref, xo_ref, ya_ref, yb_ref):
        pmv, ofv = pm_ref[...], of_ref[...]
        ya = jnp.concatenate([_dot(pmv, wpa_ref[k]) for k in range(N_CHIPS)], axis=1)
        yb = jnp.concatenate([_dot(ofv, wpb_ref[k]) for k in range(N_CHIPS)], axis=1)
        ga, gb = _gates((g0, g1, g2, g3), bg_ref, D)
        merged = (ga * ya + gb * yb).astype(BF16)
        xo_ref[...] = x_ref[...] + _dot(merged, wo_ref[...].reshape(D, D))
        ya_ref[...] = ya.astype(BF16)
        yb_ref[...] = yb.astype(BF16)

    row = lambda w: pl.BlockSpec((T, w), lambda i: (i, 0))
    full = lambda a: pl.BlockSpec(a.shape, lambda i: (0,) * a.ndim)
    return _call(
        body, name=name, grid=(S // T,), parallel=(0,),
        in_specs=[row(D), row(P), row(P)] + _gate_specs(T, D) + [full(b_gate), full(w_pa4), full(w_pb4), full(w_o4)],
        out_specs=[row(D), row(D), row(D)],
        out_shape=[jax.ShapeDtypeStruct((S, D), F32), jax.ShapeDtypeStruct((S, D), BF16),
                   jax.ShapeDtypeStruct((S, D), BF16)],
        args=(x, pm, of, z, z, z, z, b_gate, w_pa4, w_pb4, w_o4))[0]


def _mix_bwd(dxm, ya, yb, z, b_gate, pm, of, w_pa4, w_pb4, w_o4, *, name):
    S, D = dxm.shape
    P = pm.shape[1]
    q4 = D // N_CHIPS
    T = _row_tile(S, 256)
    nt = S // T

    def body(dx_ref, ya_ref, yb_ref, g0, g1, g2, g3, bg_ref, pm_ref, of_ref, wpa_ref, wpb_ref, wo_ref,
             dzg_ref, dpm_ref, dof_ref, dwo_ref, dwpa_ref, dwpb_ref, dbg_ref, dwo16_ref, dwpa16_ref, dwpb16_ref):
        i = pl.program_id(0)

        @pl.when(i == 0)
        def _():
            dwo_ref[...] = jnp.zeros_like(dwo_ref)
            dwpa_ref[...] = jnp.zeros_like(dwpa_ref)
            dwpb_ref[...] = jnp.zeros_like(dwpb_ref)
            dbg_ref[...] = jnp.zeros_like(dbg_ref)

        dxb = dx_ref[...].astype(BF16)
        ya = ya_ref[...].astype(F32)
        yb = yb_ref[...].astype(F32)
        ga, gb = _gates((g0, g1, g2, g3), bg_ref, D)
        merged = (ga * ya + gb * yb).astype(BF16)
        dwo_ref[...] += _dot_tn(merged, dxb).reshape(N_CHIPS, q4, D)
        dm = _dot_nt(dxb, wo_ref[...].reshape(D, D))
        dza = dm * ya * ga * (1.0 - ga)
        dzb = dm * yb * gb * (1.0 - gb)
        dzg_ref[:, :D] = dza.astype(BF16)
        dzg_ref[:, D:] = dzb.astype(BF16)
        dbg_ref[:, :D] += jnp.sum(dza, axis=0, keepdims=True)
        dbg_ref[:, D:] += jnp.sum(dzb, axis=0, keepdims=True)
        dya = (dm * ga).astype(BF16)
        dyb = (dm * gb).astype(BF16)
        pmv, ofv = pm_ref[...], of_ref[...]
        dpm = jnp.zeros((T, P), F32)
        dof = jnp.zeros((T, P), F32)
        for k in range(N_CHIPS):
            cols = slice(k * q4, (k + 1) * q4)
            dwpa_ref[k] += _dot_tn(pmv, dya[:, cols])
            dwpb_ref[k] += _dot_tn(ofv, dyb[:, cols])
            dpm = dpm + _dot_nt(dya[:, cols], wpa_ref[k])
            dof = dof + _dot_nt(dyb[:, cols], wpb_ref[k])
        dpm_ref[...] = dpm
        dof_ref[...] = dof

        @pl.when(i == nt - 1)
        def _():
            dwo16_ref[...] = dwo_ref[...].astype(BF16)
            dwpa16_ref[...] = dwpa_ref[...].astype(BF16)
            dwpb16_ref[...] = dwpb_ref[...].astype(BF16)

    row = lambda w: pl.BlockSpec((T, w), lambda i: (i, 0))
    full = lambda a: pl.BlockSpec(a.shape, lambda i: (0,) * a.ndim)
    like = lambda a, dt: jax.ShapeDtypeStruct(a.shape, dt)
    return _call(
        body, name=name, grid=(nt,),
        in_specs=[row(D), row(D), row(D)] + _gate_specs(T, D) + [full(b_gate), row(P), row(P),
                                                                  full(w_pa4), full(w_pb4), full(w_o4)],
        out_specs=[row(2 * D), row(P), row(P), full(w_o4), full(w_pa4), full(w_pb4), full(b_gate),
                   full(w_o4), full(w_pa4), full(w_pb4)],
        out_shape=[jax.ShapeDtypeStruct((S, 2 * D), BF16), jax.ShapeDtypeStruct((S, P), F32),
                   jax.ShapeDtypeStruct((S, P), F32), like(w_o4, F32), like(w_pa4, F32), like(w_pb4, F32),
                   like(b_gate, F32), like(w_o4, BF16), like(w_pa4, BF16), like(w_pb4, BF16)],
        args=(dxm, ya, yb, z, z, z, z, b_gate, pm, of, w_pa4, w_pb4, w_o4))[0]


def _conv3(h, halo, first_tile, cw, cb):
    halo = jnp.where(first_tile, 0.0, halo)
    hp = jnp.concatenate([halo, h], axis=0)
    h1 = _shift_down(hp, 1)
    h2 = _shift_down(hp, 2)
    return cw[0:1, :] * h2 + cw[1:2, :] * h1 + cw[2:3, :] * h + cb, h1, h2


def _ffn_down_fwd(h, conv_w, conv_b, w_down4, x, *, name):
    S, F2 = h.shape
    _, f4, D = w_down4.shape
    T = _row_tile(S, 512)
    nf = 2
    tf = 2 * f4
    hb = T // HALO_CONV

    def body(hv_ref, hg_ref, pv_ref, pg_ref, cwv_ref, cwg_ref, cbv_ref, cbg_ref, wd_ref, x_ref, o_ref, acc_ref):
        i, f = pl.program_id(0), pl.program_id(1)

        @pl.when(f == 0)
        def _():
            acc_ref[...] = jnp.zeros_like(acc_ref)

        val, _, _ = _conv3(hv_ref[...], pv_ref[...], i == 0, cwv_ref[...], cbv_ref[...])
        gate, _, _ = _conv3(hg_ref[...], pg_ref[...], i == 0, cwg_ref[...], cbg_ref[...])
        a = (gate * _sigmoid(gate) * val).astype(BF16)
        acc_ref[...] += _dot(a, wd_ref[...].reshape(tf, D))

        @pl.when(f == nf - 1)
        def _():
            o_ref[...] = x_ref[...] + acc_ref[...]

    prev = lambda i: jnp.maximum(i * hb - 1, 0)
    return _call(
        body, name=name, grid=(S // T, nf), parallel=(0,),
        in_specs=[pl.BlockSpec((T, tf), lambda i, f: (i, f)),
                  pl.BlockSpec((T, tf), lambda i, f: (i, nf + f)),
                  pl.BlockSpec((HALO_CONV, tf), lambda i, f: (prev(i), f)),
                  pl.BlockSpec((HALO_CONV, tf), lambda i, f: (prev(i), nf + f)),
                  pl.BlockSpec((3, tf), lambda i, f: (0, f)),
                  pl.BlockSpec((3, tf), lambda i, f: (0, nf + f)),
                  pl.BlockSpec((1, tf), lambda i, f: (0, f)),
                  pl.BlockSpec((1, tf), lambda i, f: (0, nf + f)),
                  pl.BlockSpec((2, f4, D), lambda i, f: (f, 0, 0)),
                  pl.BlockSpec((T, D), lambda i, f: (i, 0))],
        out_specs=[pl.BlockSpec((T, D), lambda i, f: (i, 0))],
        out_shape=[jax.ShapeDtypeStruct((S, D), F32)],
        scratch=[pltpu.VMEM((T, D), F32)],
        args=(h, h, h, h, conv_w, conv_w, conv_b, conv_b, w_down4, x))[0][0]


def _ffn_down_bwd(dxo, h, conv_w, conv_b, w_down4, *, name, plan=None):
    S, F2 = h.shape
    _, f4, D = w_down4.shape
    F = N_CHIPS * f4
    T = _row_tile(S, 256)
    nf = 2
    tf = 2 * f4
    hb = T // HALO_CONV
    nt = S // T

    def body(dx_ref, hv_ref, hg_ref, pv_ref, pg_ref, cwv_ref, cwg_ref, cbv_ref, cbg_ref, wd_ref,
             dhv_ref, dhg_ref, dwd_ref, dwd16_ref, dcwv_ref, dcwg_ref, dcbv_ref, dcbg_ref, cv_scr, cg_scr):
        i = pl.program_id(1)
        first_tile = i == nt - 1

        @pl.when(i == 0)
        def _():
            cv_scr[...] = jnp.zeros_like(cv_scr)
            cg_scr[...] = jnp.zeros_like(cg_scr)
            dwd_ref[...] = jnp.zeros_like(dwd_ref)
            dcwv_ref[...] = jnp.zeros_like(dcwv_ref)
            dcwg_ref[...] = jnp.zeros_like(dcwg_ref)
            dcbv_ref[...] = jnp.zeros_like(dcbv_ref)
            dcbg_ref[...] = jnp.zeros_like(dcbg_ref)

        dxb = dx_ref[...].astype(BF16)
        hv, hg = hv_ref[...], hg_ref[...]
        val, _, _ = _conv3(hv, pv_ref[...], first_tile, cwv_ref[...], cbv_ref[...])
        gate, _, _ = _conv3(hg, pg_ref[...], first_tile, cwg_ref[...], cbg_ref[...])
        sg = _sigmoid(gate)
        sil = gate * sg
        dwd_ref[...] += _dot_tn((sil * val).astype(BF16), dxb).reshape(2, f4, D)
        da = _dot_nt(dxb, wd_ref[...].reshape(tf, D))
        dval = da * sil
        dgate = da * val * _dsilu(gate, sg)

        def conv_bwd(dhc, h0, cw, c_scr, dh_ref, dcw_ref, dcb_ref):
            ext = jnp.concatenate([dhc, c_scr[...]], axis=0)
            n1 = _shift_up(ext, 1)
            n2 = _shift_up(ext, 2)
            dh_ref[...] = (cw[2:3, :] * dhc + cw[1:2, :] * n1 + cw[0:1, :] * n2).astype(BF16)
            c_scr[...] = dhc[:HALO_CONV, :]
            dcw_ref[0:1, :] += jnp.sum(n2 * h0, axis=0, keepdims=True)
            dcw_ref[1:2, :] += jnp.sum(n1 * h0, axis=0, keepdims=True)
            dcw_ref[2:3, :] += jnp.sum(dhc * h0, axis=0, keepdims=True)
            dcb_ref[...] += jnp.sum(dhc, axis=0, keepdims=True)

        conv_bwd(dval, hv, cwv_ref[...], cv_scr, dhv_ref, dcwv_ref, dcbv_ref)
        conv_bwd(dgate, hg, cwg_ref[...], cg_scr, dhg_ref, dcwg_ref, dcbg_ref)

        @pl.when(i == nt - 1)
        def _():
            dwd16_ref[...] = dwd_ref[...].astype(BF16)

    rev = lambda i: nt - 1 - i
    prev = lambda i: jnp.maximum(rev(i) * hb - 1, 0)
    wd_spec = pl.BlockSpec((2, f4, D), lambda f, i: (f, 0, 0))
    return _call(
        body, name=name, grid=(nf, nt), plan=plan,
        in_specs=[pl.BlockSpec((T, D), lambda f, i: (rev(i), 0)),
                  pl.BlockSpec((T, tf), lambda f, i: (rev(i), f)),
                  pl.BlockSpec((T, tf), lambda f, i: (rev(i), nf + f)),
                  pl.BlockSpec((HALO_CONV, tf), lambda f, i: (prev(i), f)),
                  pl.BlockSpec((HALO_CONV, tf), lambda f, i: (prev(i), nf + f)),
                  pl.BlockSpec((3, tf), lambda f, i: (0, f)),
                  pl.BlockSpec((3, tf), lambda f, i: (0, nf + f)),
                  pl.BlockSpec((1, tf), lambda f, i: (0, f)),
                  pl.BlockSpec((1, tf), lambda f, i: (0, nf + f)),
                  wd_spec],
        out_specs=[pl.BlockSpec((T, tf), lambda f, i: (rev(i), f)),
                   pl.BlockSpec((T, tf), lambda f, i: (rev(i), f)),
                   wd_spec, wd_spec,
                   pl.BlockSpec((3, tf), lambda f, i: (0, f)),
                   pl.BlockSpec((3, tf), lambda f, i: (0, f)),
                   pl.BlockSpec((1, tf), lambda f, i: (0, f)),
                   pl.BlockSpec((1, tf), lambda f, i: (0, f))],
        out_shape=[jax.ShapeDtypeStruct((S, F), BF16), jax.ShapeDtypeStruct((S, F), BF16),
                   jax.ShapeDtypeStruct((N_CHIPS, f4, D), F32), jax.ShapeDtypeStruct((N_CHIPS, f4, D), BF16),
                   jax.ShapeDtypeStruct((3, F), F32), jax.ShapeDtypeStruct((3, F), F32),
                   jax.ShapeDtypeStruct((1, F), F32), jax.ShapeDtypeStruct((1, F), F32)],
        scratch=[pltpu.VMEM((HALO_CONV, tf), F32), pltpu.VMEM((HALO_CONV, tf), F32)],
        args=(dxo, h, h, h, h, conv_w, conv_w, conv_b, conv_b, w_down4))


def _final_loss(x, g, target, *, name):
    S, D = x.shape
    T = _row_tile(S, 512)

    def body(x_ref, g_ref, t_ref, loss_ref, dx_ref, dg_ref):
        @pl.when(pl.program_id(0) == 0)
        def _():
            loss_ref[...] = jnp.zeros_like(loss_ref)
            dg_ref[...] = jnp.zeros_like(dg_ref)

        xf = x_ref[...]
        r = lax.rsqrt(jnp.mean(xf * xf, axis=-1, keepdims=True) + EPS)
        xhat = xf * r
        err = xhat * g_ref[...] - t_ref[...]
        loss_ref[...] += jnp.sum(err * err, axis=0, keepdims=True) * (0.5 / D)
        dy = err * (1.0 / D)
        dxhat = dy * g_ref[...]
        dx_ref[...] = r * (dxhat - xhat * jnp.mean(dxhat * xhat, axis=-1, keepdims=True))
        dg_ref[...] += jnp.sum(dy * xhat, axis=0, keepdims=True)

    return _call(
        body, name=name, grid=(S // T,),
        in_specs=[pl.BlockSpec((T, D), lambda i: (i, 0)), pl.BlockSpec((1, D), lambda i: (0, 0)),
                  pl.BlockSpec((T, D), lambda i: (i, 0))],
        out_specs=[pl.BlockSpec((1, D), lambda i: (0, 0)), pl.BlockSpec((T, D), lambda i: (i, 0)),
                   pl.BlockSpec((1, D), lambda i: (0, 0))],
        out_shape=[jax.ShapeDtypeStruct((1, D), F32), jax.ShapeDtypeStruct((S, D), F32),
                   jax.ShapeDtypeStruct((1, D), F32)],
        args=(x, g, target))[0]


BIG = ("w_in", "w_pa", "w_pb", "w_o", "w_up", "w_down")
SMALL = ("norm1_g", "b_gate", "pool_w", "pool_scale", "lb_logits", "hgrn_norm_g", "norm2_g", "conv_b", "final_g")
WEIGHTS = ("norm1_g", "w_in", "b_gate", "pool_w", "pool_scale", "lb_logits", "hgrn_norm_g", "w_pa", "w_pb", "w_o",
           "norm2_g", "w_up", "conv_w", "conv_b", "w_down", "final_g")


def _lower_bounds(lb_logits):
    soft = jax.nn.softmax(lb_logits.astype(F32), axis=0)
    cum = jnp.cumsum(soft, axis=0)
    return cum - cum[0:1]


def _step(x, target, sm, wts, shards=None):
    L = sm["norm1_g"].shape[0]
    wts = dict(wts)
    dist = shards is not None
    lbs, lb_vjp = jax.vjp(_lower_bounds, sm["lb_logits"])
    row = lambda a: a.reshape(1, -1)
    conv_w = sm.get("conv_w")

    def gather(names_layers, with_conv=False):
        items = [(shards[n], "rows", l) for n, l in names_layers]
        if with_conv:
            items.append((shards["conv_w"], "layer", None))
        return _GatherPlan(items)

    def landed(names_layers, outs):
        for key, arr in zip(names_layers, outs):
            wts[key] = arr

    saved = []
    for l in range(L):
        g1 = row(sm["norm1_g"][l])
        plan = None
        if dist and l == 0:
            rest0 = [(n, 0) for n in BIG if n != "w_in"]
            plan = gather(rest0, with_conv=True)
        (xn1, z), got = _norm_matmul(x, g1, wts[("w_in", l)], name=f"in_proj_{l}", plan=plan)
        if plan is not None:
            landed(rest0, got)
            full = got[-1]
            conv_w = jnp.concatenate([full[:, k] for k in range(N_CHIPS)], axis=2)
        pm = _pool_fwd(z, sm["pool_w"][l], row(sm["pool_scale"][l]), name=f"pool_fwd_{l}")
        plan = None
        if dist and l == 0:
            layer1 = [(n, 1) for n in BIG]
            plan = gather(layer1)
        (o_raw, of, states), got = _hgrn_fwd(z, row(lbs[l]), row(sm["hgrn_norm_g"][l]), name=f"hgrn_fwd_{l}",
                                             plan=plan)
        if plan is not None:
            landed(layer1, got)
        x_mid, ya, yb = _mix_fwd(x, pm, of, z, row(sm["b_gate"][l]), wts[("w_pa", l)], wts[("w_pb", l)],
                                 wts[("w_o", l)], name=f"mix_fwd_{l}")
        (xn2, h), _ = _norm_matmul(x_mid, row(sm["norm2_g"][l]), wts[("w_up", l)], name=f"up_{l}")
        x_out = _ffn_down_fwd(h, conv_w[l], row(sm["conv_b"][l]), wts[("w_down", l)], x_mid, name=f"down_fwd_{l}")
        saved.append(dict(x=x, xn1=xn1, z=z, pm=pm, o_raw=o_raw, of=of, states=states,
                          x_mid=x_mid, ya=ya, yb=yb, xn2=xn2, h=h))
        x = x_out

    loss_cols, dx, d_final_g = _final_loss(x, row(sm["final_g"]), target, name="final_loss")

    small = {k: [None] * L for k in ("norm1_g", "b_gate", "pool_w", "pool_scale", "hgrn_norm_g", "norm2_g",
                                     "conv_w", "conv_b")}
    big32, big16, recv = {}, {}, {}
    dlbs = [None] * L
    pending = []

    def scatter():
        keys = list(pending)
        del pending[:]
        return keys, _ScatterPlan([big16[k] for k in keys])

    def sent(keys, outs):
        for key, arr in zip(keys, outs):
            recv[key] = arr

    for l in reversed(range(L)):
        s = saved[l]
        keys, plan = scatter() if dist and pending else (None, None)
        (dhv, dhg, d_wd, d_wd16, dcwv, dcwg, dcbv, dcbg), got = _ffn_down_bwd(
            dx, s["h"], conv_w[l], row(sm["conv_b"][l]), wts[("w_down", l)], name=f"down_bwd_{l}", plan=plan)
        if plan is not None:
            sent(keys, got)
        big32[("w_down", l)], big16[("w_down", l)] = d_wd, d_wd16
        small["conv_w"][l] = jnp.concatenate([dcwv, dcwg], axis=1)
        small["conv_b"][l] = jnp.concatenate([dcbv, dcbg], axis=1)[0]
        big32[("w_up", l)], big16[("w_up", l)] = _wgrad(s["xn2"], [dhv, dhg], name=f"up_wgrad_{l}")
        (dxm, dg2), _ = _dgrad_norm([dhv, dhg], wts[("w_up", l)], s["x_mid"], row(sm["norm2_g"][l]), dx,
                                    name=f"up_dgrad_{l}")
        small["norm2_g"][l] = dg2[0]

        dzg, dpm, dof, d_wo, d_wpa, d_wpb, dbg, d_wo16, d_wpa16, d_wpb16 = _mix_bwd(
            dxm, s["ya"], s["yb"], s["z"], row(sm["b_gate"][l]), s["pm"], s["of"],
            wts[("w_pa", l)], wts[("w_pb", l)], wts[("w_o", l)], name=f"mix_bwd_{l}")
        big32[("w_o", l)], big16[("w_o", l)] = d_wo, d_wo16
        big32[("w_pa", l)], big16[("w_pa", l)] = d_wpa, d_wpa16
        big32[("w_pb", l)], big16[("w_pb", l)] = d_wpb, d_wpb16
        small["b_gate"][l] = dbg[0]
        pending.extend([("w_down", l), ("w_up", l), ("w_o", l), ("w_pa", l), ("w_pb", l)])

        du, dpw, dps = _pool_bwd(s["z"], dpm, sm["pool_w"][l], row(sm["pool_scale"][l]), name=f"pool_bwd_{l}")
        small["pool_w"][l], small["pool_scale"][l] = dpw, dps[0]

        keys, plan = scatter() if dist and l == 0 else (None, None)
        (dzq, dzf, dzi, dzo, dlb, dng), got = _hgrn_bwd(s["z"], row(lbs[l]), row(sm["hgrn_norm_g"][l]), s["o_raw"],
                                                      s["states"], dof, name=f"hgrn_bwd_{l}", plan=plan)
        if plan is not None:
            sent(keys, got)
        dlbs[l] = dlb[0]
        small["hgrn_norm_g"][l] = jnp.sum(dng.reshape(-1, LANES), axis=0)

        dz = jnp.concatenate([du, dzq, dzf, dzi, dzo, dzg], axis=1)
        big32[("w_in", l)], big16[("w_in", l)] = _wgrad(s["xn1"], [dz], name=f"in_wgrad_{l}")
        pending.append(("w_in", l))
        keys, plan = scatter() if dist and l == 0 else (None, None)
        (dx, dg1), got = _dgrad_norm([dz], wts[("w_in", l)], s["x"], row(sm["norm1_g"][l]), dxm,
                                     name=f"in_dgrad_{l}", plan=plan)
        if plan is not None:
            sent(keys, got)
        small["norm1_g"][l] = dg1[0]

    out = {k: jnp.stack(v) for k, v in small.items()}
    out["lb_logits"] = lb_vjp(jnp.stack(dlbs))[0]
    out["final_g"] = d_final_g[0]
    return loss_cols, dx, out, big32, recv


def _elementwise_rows(R, n, n_arrays):
    if 2 * n_arrays * R * n * 4 <= VMEM_LIMIT // 4 or R % 8:
        return R
    want = 8
    while want * 2 * n * 4 <= 1024 * 1024:
        want *= 2
    return _row_tile(R, want)


def _sum_layers(own, got, chip, *, name):
    L = len(own)
    _, r, n = own[0].shape
    T = _elementwise_rows(r, n, 6)
    nt = r // T

    def body(chip_ref, *refs):
        o_ref = refs[-1]
        l = pl.program_id(0)
        for k in range(L):
            @pl.when(l == k)
            def _():
                own_ref, got_ref = refs[2 * k], refs[2 * k + 1]
                acc = own_ref[...]
                for j in range(3):
                    acc = acc + got_ref[j].astype(F32)
                o_ref[...] = acc

    in_specs = []
    for k in range(L):
        hold = 0 if k else nt - 1
        in_specs.append(pl.BlockSpec((None, T, n), lambda l, i, c, k=k, hold=hold: (c[0], jnp.where(l == k, i, hold), 0)))
        in_specs.append(pl.BlockSpec((3, T, n), lambda l, i, c, k=k, hold=hold: (0, jnp.where(l == k, i, hold), 0)))
    grid_spec = pltpu.PrefetchScalarGridSpec(
        num_scalar_prefetch=1, grid=(L, nt), in_specs=in_specs,
        out_specs=pl.BlockSpec((None, T, n), lambda l, i, c: (l, i, 0)))
    args = [a for pair in zip(own, got) for a in pair]
    return pl.pallas_call(
        body, name=name, grid_spec=grid_spec, out_shape=jax.ShapeDtypeStruct((L, r, n), F32),
        compiler_params=pltpu.CompilerParams(dimension_semantics=("arbitrary", "arbitrary"),
                                             vmem_limit_bytes=VMEM_LIMIT),
    )(chip, *args)


def _sum_stack(first, rest, *, name):
    R, n = first.shape
    K = rest.shape[0]
    T = _elementwise_rows(R, n, K + 2)

    def body(a_ref, r_ref, o_ref):
        acc = a_ref[...]
        for j in range(K):
            acc = acc + r_ref[j].astype(F32)
        o_ref[...] = acc

    return _call(
        body, name=name, grid=(R // T,), parallel=(0,),
        in_specs=[pl.BlockSpec((T, n), lambda i: (i, 0)), pl.BlockSpec((K, T, n), lambda i: (0, i, 0))],
        out_specs=[pl.BlockSpec((T, n), lambda i: (i, 0))],
        out_shape=[jax.ShapeDtypeStruct((R, n), F32)],
        args=(first, rest))[0][0]


def _adamw(w, m, v, g_parts, *, name):
    R, n = w.shape
    n_g = len(g_parts)
    T = _elementwise_rows(R, n, 7 + n_g)

    def body(*refs):
        w_ref, m_ref, v_ref = refs[:3]
        g_refs = refs[3:3 + n_g]
        go_ref, d_ref, mo_ref, vo_ref = refs[3 + n_g:]
        g_ = g_refs[0][...]
        for r in g_refs[1:]:
            g_ = g_ + r[...]
        m_ = ADAM_B1 * m_ref[...] + (1.0 - ADAM_B1) * g_
        v_ = ADAM_B2 * v_ref[...] + (1.0 - ADAM_B2) * (g_ * g_)
        m_hat = m_ / (1.0 - ADAM_B1 ** ADAM_STEP)
        v_hat = v_ / (1.0 - ADAM_B2 ** ADAM_STEP)
        go_ref[...] = g_
        d_ref[...] = -ADAM_LR * (m_hat / (jnp.sqrt(v_hat) + ADAM_EPS) + ADAM_WD * w_ref[...])
        mo_ref[...] = m_
        vo_ref[...] = v_

    blk = pl.BlockSpec((T, n), lambda i: (i, 0))
    return _call(
        body, name=name, grid=(R // T,), parallel=(0,),
        in_specs=[blk] * (3 + n_g), out_specs=[blk] * 4,
        out_shape=[jax.ShapeDtypeStruct((R, n), F32)] * 4,
        args=(w, m, v, *g_parts))[0]


PACK_ALIGN = 8 * LANES


def _pack(pieces):
    flat = []
    for a in pieces:
        a = a.reshape(-1)
        pad = (-a.shape[0]) % PACK_ALIGN
        flat.append(jnp.pad(a, (0, pad)) if pad else a)
    return jnp.concatenate(flat).reshape(-1, LANES)


def _unpack(buf, shapes):
    flat = buf.reshape(-1)
    out, off = [], 0
    for shp in shapes:
        size = 1
        for s in shp:
            size *= s
        out.append(flat[off:off + size].reshape(shp))
        off += size + (-size) % PACK_ALIGN
    return out


def kernel(x, norm1_g, w_in, b_gate, pool_w, pool_scale, lb_logits, hgrn_norm_g, w_pa, w_pb, w_o, norm2_g, w_up, conv_w, conv_b, w_down, final_g, loss_target, m_norm1_g, m_w_in, m_b_gate, m_pool_w, m_pool_scale, m_lb_logits, m_hgrn_norm_g, m_w_pa, m_w_pb, m_w_o, m_norm2_g, m_w_up, m_conv_w, m_conv_b, m_w_down, m_final_g, v_norm1_g, v_w_in, v_b_gate, v_pool_w, v_pool_scale, v_lb_logits, v_hgrn_norm_g, v_w_pa, v_w_pb, v_w_o, v_norm2_g, v_w_up, v_conv_w, v_conv_b, v_w_down, v_final_g):
    env = dict(locals())
    w = {n: env[n] for n in WEIGHTS}
    m = {n: env["m_" + n] for n in WEIGHTS}
    v = {n: env["v_" + n] for n in WEIGHTS}
    my_chip = 2 * lax.axis_index("x") + lax.axis_index("y")
    L = w_in.shape[0]

    shards = {n: w[n].astype(BF16) for n in BIG}
    shards["conv_w"] = w["conv_w"]
    w_in0 = _run_plan(_GatherPlan([(shards["w_in"], "rows", 0)]), name="gather_w_in0")[0]
    sm = {n: w[n] for n in SMALL}
    loss_cols, grad_x, g_small, big32, recv = _step(x[0], loss_target[0], sm, {("w_in", 0): w_in0}, shards)

    chip = my_chip.reshape(1).astype(jnp.int32)
    sums = [_sum_layers([big32[(n, l)] for l in range(L)], [recv[(n, l)] for l in range(L)], chip,
                        name="chip_sum_" + n) for n in BIG]
    theirs = _run_plan(_SiblingPlan(sums), name="grad_sibling")
    g, delta, new_m, new_v = {}, {}, {}, {}
    for n, mine, other in zip(BIG, sums, theirs):
        shp = w[n].shape
        two_d = lambda a: a.reshape(-1, shp[-1])
        outs = _adamw(two_d(w[n]), two_d(m[n]), two_d(v[n]), [two_d(mine), two_d(other)], name="adamw_" + n)
        g[n], delta[n], new_m[n], new_v[n] = [a.reshape(shp) for a in outs]

    small_names = list(SMALL)
    small_pieces = [g_small[n] for n in small_names] + [g_small["conv_w"], loss_cols]
    small_shapes = [a.shape for a in small_pieces]
    packed = _pack(small_pieces)
    Rs = packed.shape[0]
    everyone = _run_plan(_EveryonePlan(packed), name="gather_small")[0].reshape(8, Rs, LANES)
    summed = _unpack(_sum_stack(everyone[0], everyone[1:], name="small_sum"), small_shapes)
    loss = jnp.sum(summed[-1])
    cshard = w["conv_w"].shape[2]
    gs = dict(zip(small_names, summed[:len(small_names)]))
    g_cw = lax.dynamic_slice_in_dim(summed[-2], my_chip * cshard, cshard, axis=2)

    sm_out = _adamw(_pack([w[n] for n in small_names]), _pack([m[n] for n in small_names]),
                    _pack([v[n] for n in small_names]), [_pack([gs[n] for n in small_names])], name="adamw_small")
    shapes = [w[n].shape for n in small_names]
    for n, g_, d_, m_, v_ in zip(small_names, *[_unpack(a, shapes) for a in sm_out]):
        g[n], delta[n], new_m[n], new_v[n] = g_, d_, m_, v_
    shp = w["conv_w"].shape
    two_d = lambda a: a.reshape(-1, shp[-1])
    outs = _adamw(two_d(w["conv_w"]), two_d(m["conv_w"]), two_d(v["conv_w"]), [two_d(g_cw)], name="adamw_conv_w")
    g["conv_w"], delta["conv_w"], new_m["conv_w"], new_v["conv_w"] = [a.reshape(shp) for a in outs]

    return (loss, grad_x[None], *[g[n] for n in WEIGHTS], *[delta[n] for n in WEIGHTS],
            *[new_m[n] for n in WEIGHTS], *[new_v[n] for n in WEIGHTS])
```

```python
import jax
import jax.numpy as jnp
from jax import lax
from jax.experimental import pallas as pl
from jax.experimental.pallas import tpu as pltpu

F32 = jnp.float32
BF16 = jnp.bfloat16

EPS = 1e-6
CHUNK = 64
SUB = 16
LANES = 128
SUBLANES = 8
POOL_WINDOWS = (2, 4, 8, 16)
HALO_POOL = 16
HALO_CONV = 16
EXP_CLAMP = 80.0

ADAM_LR = 0.001
ADAM_B1 = 0.9
ADAM_B2 = 0.999
ADAM_EPS = 1e-08
ADAM_WD = 0.01
ADAM_STEP = 10

VMEM_LIMIT = 56 * 1024 * 1024
MESH_ID = pl.DeviceIdType.MESH
N_CHIPS = 4
ANY = pl.BlockSpec(memory_space=pl.ANY)


def _dot(a, b):
    return jnp.dot(a, b, preferred_element_type=F32)


def _dot_nt(a, b):
    return lax.dot_general(a, b, (((1,), (1,)), ((), ())), preferred_element_type=F32)


def _dot_tn(a, b):
    return lax.dot_general(a, b, (((0,), (0,)), ((), ())), preferred_element_type=F32)


def _sigmoid(x):
    return jax.nn.sigmoid(x)


def _dsilu(x, s):
    return s * (1.0 + x * (1.0 - s))


def _row_tile(rows, want):
    t = min(rows, want)
    while rows % t:
        t //= 2
    return t


def _place():
    x, y, c = lax.axis_index("x"), lax.axis_index("y"), lax.axis_index("c")
    chips = [(1 - x, y), (x, 1 - y), (1 - x, 1 - y)]
    return x, y, c, chips


def _remote(src, dst, sems, k, to):
    return pltpu.make_async_remote_copy(src_ref=src, dst_ref=dst, send_sem=sems[0].at[k], recv_sem=sems[1].at[k],
                                        device_id=to, device_id_type=MESH_ID)


class _GatherPlan:
    def __init__(self, items):
        self.items = items
        self.inputs = [a for a, _, _ in items]
        self.out_shapes = []
        for a, kind, _ in items:
            shp = (N_CHIPS,) + a.shape[1:] if kind == "rows" else (a.shape[0], N_CHIPS) + a.shape[1:]
            self.out_shapes.append(jax.ShapeDtypeStruct(shp, a.dtype))
        n = len(items)
        self.scratch = [pltpu.SemaphoreType.DMA((6 * n,)), pltpu.SemaphoreType.DMA((6 * n,)),
                        pltpu.SemaphoreType.DMA((2 * n,))]

    def _views(self, i, src, dst):
        _, kind, l = self.items[i]
        if kind == "rows":
            half = src.shape[1] // 2
            part = lambda core: src.at[l, pl.ds(core * half, half), :]
            land = lambda chip, core: dst.at[chip, pl.ds(core * half, half), :]
        else:
            part = lambda core: src.at[core]
            land = lambda chip, core: dst.at[core, chip]
        return part, land

    def start(self, srcs, dsts, sems):
        x, y, c, chips = _place()
        me = 2 * x + y
        for i, (src, dst) in enumerate(zip(srcs, dsts)):
            part, land = self._views(i, src, dst)
            for core in range(2):
                pltpu.make_async_copy(part(core), land(me, core), sems[2].at[2 * i + core]).start()
            for j, (px, py) in enumerate(chips):
                _remote(part(c), land(me, c), sems, 6 * i + j, (px, py, c)).start()

    def finish(self, srcs, dsts, sems):
        x, y, c, chips = _place()
        me = 2 * x + y
        sibling = (x, y, 1 - c)
        for i, (src, dst) in enumerate(zip(srcs, dsts)):
            part, land = self._views(i, src, dst)
            for j, (px, py) in enumerate(chips):
                got = land(2 * px + py, c)
                _remote(got, got, sems, 6 * i + j, (px, py, c)).wait_recv()
                _remote(got, got, sems, 6 * i + 3 + j, sibling).start()
        for i, (src, dst) in enumerate(zip(srcs, dsts)):
            part, land = self._views(i, src, dst)
            for j, (px, py) in enumerate(chips):
                got = land(2 * px + py, 1 - c)
                _remote(got, got, sems, 6 * i + 3 + j, sibling).wait_recv()
            for j, (px, py) in enumerate(chips):
                _remote(part(c), land(me, c), sems, 6 * i + j, (px, py, c)).wait_send()
                mine = land(2 * px + py, c)
                _remote(mine, mine, sems, 6 * i + 3 + j, sibling).wait_send()
            for core in range(2):
                pltpu.make_async_copy(part(core), land(me, core), sems[2].at[2 * i + core]).wait()


class _ScatterPlan:
    def __init__(self, items):
        self.inputs = list(items)
        self.out_shapes = [jax.ShapeDtypeStruct((3,) + a.shape[1:], a.dtype) for a in items]
        n = len(items)
        self.scratch = [pltpu.SemaphoreType.DMA((3 * n,)), pltpu.SemaphoreType.DMA((3 * n,))]

    def _copies(self, srcs, dsts, sems):
        x, y, c, chips = _place()
        return [_remote(src.at[2 * px + py], dst.at[j], sems, 3 * i + j, (px, py, c))
                for i, (src, dst) in enumerate(zip(srcs, dsts)) for j, (px, py) in enumerate(chips)]

    def start(self, srcs, dsts, sems):
        for cp in self._copies(srcs, dsts, sems):
            cp.start()

    def finish(self, srcs, dsts, sems):
        copies = self._copies(srcs, dsts, sems)
        for cp in copies:
            cp.wait_recv()
        for cp in copies:
            cp.wait_send()


class _SiblingPlan:
    def __init__(self, items):
        self.inputs = list(items)
        self.out_shapes = [jax.ShapeDtypeStruct(a.shape, a.dtype) for a in items]
        n = len(items)
        self.scratch = [pltpu.SemaphoreType.DMA((n,)), pltpu.SemaphoreType.DMA((n,))]

    def _copies(self, srcs, dsts, sems):
        x, y, c, _ = _place()
        return [_remote(src, dst, sems, i, (x, y, 1 - c)) for i, (src, dst) in enumerate(zip(srcs, dsts))]

    def start(self, srcs, dsts, sems):
        for cp in self._copies(srcs, dsts, sems):
            cp.start()

    def finish(self, srcs, dsts, sems):
        copies = self._copies(srcs, dsts, sems)
        for cp in copies:
            cp.wait_recv()
        for cp in copies:
            cp.wait_send()


class _EveryonePlan:
    def __init__(self, block):
        self.inputs = [block]
        self.m = block.shape[0]
        self.out_shapes = [jax.ShapeDtypeStruct((8 * self.m,) + block.shape[1:], block.dtype)]
        self.scratch = [pltpu.SemaphoreType.DMA((7,)), pltpu.SemaphoreType.DMA((7,)), pltpu.SemaphoreType.DMA((1,))]

    def _rows(self, dst, px, py, pc):
        return dst.at[pl.ds((4 * px + 2 * py + pc) * self.m, self.m), :]

    def start(self, srcs, dsts, sems):
        x, y, c, chips = _place()
        src, dst = srcs[0], dsts[0]
        pltpu.make_async_copy(src, self._rows(dst, x, y, c), sems[2].at[0]).start()
        _remote(src, self._rows(dst, x, y, c), sems, 0, (x, y, 1 - c)).start()
        for j, (px, py) in enumerate(chips):
            _remote(src, self._rows(dst, x, y, c), sems, 1 + j, (px, py, c)).start()

    def finish(self, srcs, dsts, sems):
        x, y, c, chips = _place()
        src, dst = srcs[0], dsts[0]
        sibling = (x, y, 1 - c)
        for j, (px, py) in enumerate(chips):
            got = self._rows(dst, px, py, c)
            _remote(got, got, sems, 1 + j, (px, py, c)).wait_recv()
            _remote(got, got, sems, 4 + j, sibling).start()
        sib = self._rows(dst, x, y, 1 - c)
        _remote(sib, sib, sems, 0, sibling).wait_recv()
        for j, (px, py) in enumerate(chips):
            got = self._rows(dst, px, py, 1 - c)
            _remote(got, got, sems, 4 + j, sibling).wait_recv()
        mine = self._rows(dst, x, y, c)
        _remote(src, mine, sems, 0, sibling).wait_send()
        for j, (px, py) in enumerate(chips):
            _remote(src, mine, sems, 1 + j, (px, py, c)).wait_send()
            got = self._rows(dst, px, py, c)
            _remote(got, got, sems, 4 + j, sibling).wait_send()
        pltpu.make_async_copy(src, mine, sems[2].at[0]).wait()


def _call(body, *, name, grid, in_specs, out_specs, out_shape, args, scratch=(), parallel=(), plan=None):
    n_in, n_out, n_scr = len(in_specs), len(out_shape), len(scratch)
    sem = tuple("parallel" if (a in parallel and plan is None) else "arbitrary" for a in range(len(grid)))
    params = pltpu.CompilerParams(dimension_semantics=sem, vmem_limit_bytes=VMEM_LIMIT)
    if plan is None:
        outs = pl.pallas_call(body, name=name, grid=grid, in_specs=in_specs, out_specs=out_specs,
                              out_shape=out_shape, scratch_shapes=list(scratch), compiler_params=params)(*args)
        return list(outs), []
    p_in, p_out, p_scr = len(plan.inputs), len(plan.out_shapes), len(plan.scratch)

    def wrapped(*refs):
        ins, refs = refs[:n_in], refs[n_in:]
        p_ins, refs = refs[:p_in], refs[p_in:]
        outs, refs = refs[:n_out], refs[n_out:]
        p_outs, refs = refs[:p_out], refs[p_out:]
        scr, p_sems = refs[:n_scr], refs[n_scr:]
        ids = [pl.program_id(a) for a in range(len(grid))]
        first = _all([i == 0 for i in ids])
        last = _all([i == n - 1 for i, n in zip(ids, grid)])

        @pl.when(first)
        def _():
            plan.start(p_ins, p_outs, p_sems)

        body(*ins, *outs, *scr)

        @pl.when(last)
        def _():
            plan.finish(p_ins, p_outs, p_sems)

    outs = pl.pallas_call(
        wrapped, name=name, grid=grid,
        in_specs=list(in_specs) + [ANY] * p_in, out_specs=list(out_specs) + [ANY] * p_out,
        out_shape=list(out_shape) + list(plan.out_shapes),
        scratch_shapes=list(scratch) + list(plan.scratch), compiler_params=params,
    )(*args, *plan.inputs)
    return list(outs[:n_out]), list(outs[n_out:])


def _all(conds):
    out = conds[0]
    for c in conds[1:]:
        out = out & c
    return out


def _run_plan(plan, *, name):
    p_in, p_out = len(plan.inputs), len(plan.out_shapes)

    def body(*refs):
        srcs, dsts, sems = refs[:p_in], refs[p_in:p_in + p_out], refs[p_in + p_out:]
        plan.start(srcs, dsts, sems)
        plan.finish(srcs, dsts, sems)

    return list(pl.pallas_call(body, name=name, in_specs=[ANY] * p_in, out_specs=[ANY] * p_out,
                               out_shape=list(plan.out_shapes), scratch_shapes=list(plan.scratch))(*plan.inputs))


def _norm_matmul(x, g, w4, *, name, out_dtype=F32, plan=None):
    S, D = x.shape
    n4 = w4.shape[2]
    tm = _row_tile(S, 1024)

    def body(x_ref, g_ref, w_ref, xn_ref, o_ref):
        @pl.when(pl.program_id(1) == 0)
        def _():
            xf = x_ref[...]
            r = lax.rsqrt(jnp.mean(xf * xf, axis=-1, keepdims=True) + EPS)
            xn_ref[...] = (xf * r * g_ref[...]).astype(BF16)

        o_ref[...] = _dot(xn_ref[...], w_ref[...]).astype(out_dtype)

    return _call(
        body, name=name, grid=(S // tm, N_CHIPS), parallel=(0,), plan=plan,
        in_specs=[pl.BlockSpec((tm, D), lambda i, j: (i, 0)),
                  pl.BlockSpec((1, D), lambda i, j: (0, 0)),
                  pl.BlockSpec((None, D, n4), lambda i, j: (j, 0, 0))],
        out_specs=[pl.BlockSpec((tm, D), lambda i, j: (i, 0)),
                   pl.BlockSpec((tm, n4), lambda i, j: (i, j))],
        out_shape=[jax.ShapeDtypeStruct((S, D), BF16), jax.ShapeDtypeStruct((S, N_CHIPS * n4), out_dtype)],
        args=(x, g, w4))


def _piece_specs(pieces, n4, tm):
    specs, ranges, k0 = [], [], 0
    for p in pieces:
        nk = p.shape[1] // n4
        imap = lambda k, i, k0=k0, nk=nk: (jnp.where((k >= k0) & (k < k0 + nk), i, 0), jnp.clip(k - k0, 0, nk - 1))
        specs.append(pl.BlockSpec((tm, n4), imap))
        ranges.append((k0, k0 + nk))
        k0 += nk
    assert k0 == N_CHIPS
    return specs, ranges


def _dgrad_norm(dys, w4, x, g, dres, *, name, plan=None):
    S, D = x.shape
    n4 = w4.shape[2]
    tm = _row_tile(S, 512)
    tiles = [(p, j) for p, a in enumerate(dys) for j in range(a.shape[1] // n4)]
    assert len(tiles) == N_CHIPS
    n_p = len(dys)

    def body(*refs):
        dy_refs = refs[:n_p]
        w_ref, x_ref, g_ref, dres_ref, dx_ref, dg_ref = refs[n_p:]

        @pl.when(pl.program_id(0) == 0)
        def _():
            dg_ref[...] = jnp.zeros_like(dg_ref)

        dxn = None
        for k, (p, j) in enumerate(tiles):
            part = _dot_nt(dy_refs[p][:, j * n4:(j + 1) * n4], w_ref[k])
            dxn = part if dxn is None else dxn + part
        xf = x_ref[...]
        r = lax.rsqrt(jnp.mean(xf * xf, axis=-1, keepdims=True) + EPS)
        xhat = xf * r
        dxhat = dxn * g_ref[...]
        dx_ref[...] = dres_ref[...] + r * (dxhat - xhat * jnp.mean(dxhat * xhat, axis=-1, keepdims=True))
        dg_ref[...] += jnp.sum(dxn * xhat, axis=0, keepdims=True)

    row = lambda w: pl.BlockSpec((tm, w), lambda i: (i, 0))
    return _call(
        body, name=name, grid=(S // tm,), plan=plan,
        in_specs=[row(a.shape[1]) for a in dys]
        + [pl.BlockSpec(w4.shape, lambda i: (0, 0, 0), pipeline_mode=pl.Buffered(1)),
           row(D), pl.BlockSpec((1, D), lambda i: (0, 0)), row(D)],
        out_specs=[row(D), pl.BlockSpec((1, D), lambda i: (0, 0))],
        out_shape=[jax.ShapeDtypeStruct((S, D), F32), jax.ShapeDtypeStruct((1, D), F32)],
        args=(*dys, w4, x, g, dres))


def _wgrad(a, dys, *, name):
    S, K = a.shape
    n4 = sum(p.shape[1] for p in dys) // N_CHIPS
    tm = _row_tile(S, 2048)
    ns = S // tm
    specs, ranges = _piece_specs(dys, n4, tm)
    n_p = len(dys)

    def body(*refs):
        a_ref = refs[0]
        dy_refs = refs[1:1 + n_p]
        o_ref, o16_ref = refs[1 + n_p:]
        n, s = pl.program_id(0), pl.program_id(1)

        @pl.when(s == 0)
        def _():
            o_ref[...] = jnp.zeros_like(o_ref)

        for dy_ref, (lo, hi) in zip(dy_refs, ranges):
            @pl.when((n >= lo) & (n < hi))
            def _():
                o_ref[...] += _dot_tn(a_ref[...], dy_ref[...])

        @pl.when(s == ns - 1)
        def _():
            o16_ref[...] = o_ref[...].astype(BF16)

    out = pl.BlockSpec((None, K, n4), lambda n, s: (n, 0, 0))
    return _call(
        body, name=name, grid=(N_CHIPS, ns), parallel=(0,),
        in_specs=[pl.BlockSpec((tm, K), lambda n, s: (s, 0))] + specs,
        out_specs=[out, out],
        out_shape=[jax.ShapeDtypeStruct((N_CHIPS, K, n4), F32), jax.ShapeDtypeStruct((N_CHIPS, K, n4), BF16)],
        args=(a, *dys))[0]


def _tiles(x):
    return x.reshape(x.shape[0] // SUBLANES, SUBLANES, x.shape[1])


def _shift_down(xp, s):
    n = xp.shape[0] - SUBLANES
    if s == SUBLANES:
        return xp[:n, :]
    t = _tiles(xp)
    rot = pltpu.roll(t, s, 1)
    sub = lax.broadcasted_iota(jnp.int32, t.shape, 1)[1:]
    return jnp.where(sub >= s, rot[1:], rot[:-1]).reshape(n, xp.shape[1])


def _shift_up(xn, s):
    n = xn.shape[0] - SUBLANES
    if s == SUBLANES:
        return xn[SUBLANES:, :]
    t = _tiles(xn)
    rot = pltpu.roll(t, SUBLANES - s, 1)
    sub = lax.broadcasted_iota(jnp.int32, t.shape, 1)[1:]
    return jnp.where(sub < SUBLANES - s, rot[:-1], rot[1:]).reshape(n, xn.shape[1])


def _pooled(u, halo, first_tile, row0):
    T = u.shape[0]
    halo = jnp.where(first_tile, 0.0, halo)
    pad = jnp.zeros((SUBLANES, u.shape[1]), F32)
    up = jnp.concatenate([pad, halo, u], axis=0)
    t1 = (row0 + lax.broadcasted_iota(jnp.int32, (T, 1), 0) + 1).astype(F32)
    outs = []
    for gi, w in enumerate(POOL_WINDOWS):
        s = up[:, gi * LANES:(gi + 1) * LANES]
        k = 1
        while k < w:
            if k < SUBLANES:
                s = jnp.concatenate([s[:SUBLANES, :], s[SUBLANES:, :] + _shift_down(s, k)], axis=0)
            else:
                s = s[SUBLANES:, :] + _shift_down(s, k)
            k *= 2
        s = s[-T:, :]
        inv = 1.0 / jnp.minimum(t1, float(w))
        outs.append(s * inv - u[:, gi * LANES:(gi + 1) * LANES])
    return outs


def _pool_fwd(z, pool_w, pool_scale, *, name):
    S = z.shape[0]
    P = pool_scale.shape[1]
    T = _row_tile(S, 512)
    hb = T // HALO_POOL

    def body(u_ref, halo_ref, pw_ref, ps_ref, o_ref):
        i = pl.program_id(0)
        pooled = _pooled(u_ref[...], halo_ref[...], i == 0, i * T)
        for gi in range(len(POOL_WINDOWS)):
            mixed = _dot(pooled[gi].astype(BF16), pw_ref[gi].astype(BF16))
            cols = slice(gi * LANES, (gi + 1) * LANES)
            o_ref[:, cols] = (mixed * ps_ref[:, cols]).astype(BF16)

    return _call(
        body, name=name, grid=(S // T,), parallel=(0,),
        in_specs=[pl.BlockSpec((T, P), lambda i: (i, 0)),
                  pl.BlockSpec((HALO_POOL, P), lambda i: (jnp.maximum(i * hb - 1, 0), 0)),
                  pl.BlockSpec(pool_w.shape, lambda i: (0, 0, 0)),
                  pl.BlockSpec((1, P), lambda i: (0, 0))],
        out_specs=[pl.BlockSpec((T, P), lambda i: (i, 0))],
        out_shape=[jax.ShapeDtypeStruct((S, P), BF16)],
        args=(z, z, pool_w, pool_scale))[0][0]


def _pool_bwd(z, dpm, pool_w, pool_scale, *, name):
    S, P = dpm.shape
    T = _row_tile(S, 512)
    hb = T // HALO_POOL
    nt = S // T

    def body(u_ref, halo_ref, d_ref, dnext_ref, pw_ref, ps_ref, du_ref, dpw_ref, dps_ref):
        i = pl.program_id(0)

        @pl.when(i == 0)
        def _():
            dpw_ref[...] = jnp.zeros_like(dpw_ref)
            dps_ref[...] = jnp.zeros_like(dps_ref)

        pooled = _pooled(u_ref[...], halo_ref[...], i == 0, i * T)
        dnext = jnp.where(i == nt - 1, 0.0, dnext_ref[...])
        pad = jnp.zeros((SUBLANES, P), F32)
        dext = jnp.concatenate([d_ref[...], dnext, pad], axis=0)
        t1 = (i * T + lax.broadcasted_iota(jnp.int32, (T + HALO_POOL + SUBLANES, 1), 0) + 1).astype(F32)
        for gi, w in enumerate(POOL_WINDOWS):
            cols = slice(gi * LANES, (gi + 1) * LANES)
            pw = pw_ref[gi].astype(BF16)
            pg = pooled[gi].astype(BF16)
            mixed = _dot(pg, pw)
            dps_ref[:, cols] += jnp.sum(d_ref[:, cols] * mixed, axis=0, keepdims=True)
            dmixed = (dext[:, cols] * ps_ref[:, cols]).astype(BF16)
            dpw_ref[gi] += _dot_tn(pg, dmixed[:T, :])
            dpooled = _dot_nt(dmixed, pw)
            e = dpooled * (1.0 / jnp.minimum(t1, float(w)))
            k = 1
            while k < w:
                if k < SUBLANES:
                    e = jnp.concatenate([e[:-SUBLANES, :] + _shift_up(e, k), e[-SUBLANES:, :]], axis=0)
                else:
                    e = e[:-SUBLANES, :] + _shift_up(e, k)
                k *= 2
            du_ref[:, cols] = (e[:T, :] - dpooled[:T, :]).astype(BF16)

    return _call(
        body, name=name, grid=(nt,),
        in_specs=[pl.BlockSpec((T, P), lambda i: (i, 0)),
                  pl.BlockSpec((HALO_POOL, P), lambda i: (jnp.maximum(i * hb - 1, 0), 0)),
                  pl.BlockSpec((T, P), lambda i: (i, 0)),
                  pl.BlockSpec((HALO_POOL, P), lambda i: (jnp.minimum((i + 1) * hb, S // HALO_POOL - 1), 0)),
                  pl.BlockSpec(pool_w.shape, lambda i: (0, 0, 0)),
                  pl.BlockSpec((1, P), lambda i: (0, 0))],
        out_specs=[pl.BlockSpec((T, P), lambda i: (i, 0)),
                   pl.BlockSpec(pool_w.shape, lambda i: (0, 0, 0)),
                   pl.BlockSpec((1, P), lambda i: (0, 0))],
        out_shape=[jax.ShapeDtypeStruct((S, P), BF16),
                   jax.ShapeDtypeStruct(pool_w.shape, F32),
                   jax.ShapeDtypeStruct((1, P), F32)],
        args=(z, z, dpm, dpm, pool_w, pool_scale))[0]


def _cumsum_rows(x):
    n = x.shape[0]
    row = lax.broadcasted_iota(jnp.int32, x.shape, 0)
    s = 1
    while s < n:
        x = x + jnp.where(row >= s, pltpu.roll(x, s, 0), 0.0)
        s *= 2
    return x


def _rev_cumsum_rows(x):
    n = x.shape[0]
    row = lax.broadcasted_iota(jnp.int32, x.shape, 0)
    s = 1
    while s < n:
        x = x + jnp.where(row < n - s, pltpu.roll(x, n - s, 0), 0.0)
        s *= 2
    return x


def _chunk_prep(zq, zf, lb, b_ref):
    n_sub = CHUNK // SUB
    sq = _sigmoid(zq)
    q = zq * sq
    sf = _sigmoid(zf)
    f = lb + (1.0 - lb) * sf
    k = 1.0 - f
    b = _cumsum_rows(jnp.log(f))
    b_ref[...] = b
    shape = (SUB, b.shape[1])
    ends = [jnp.broadcast_to(b_ref[pl.ds(SUB * j + SUB - 1, 1), :], shape) for j in range(n_sub)]
    mids = [jnp.broadcast_to(b_ref[pl.ds(SUB * j + SUB // 2 - 1, 1), :], shape) for j in range(n_sub)]
    own = [b[SUB * j:SUB * (j + 1), :] for j in range(n_sub)]
    m0 = jnp.concatenate(mids, axis=0)
    e1 = jnp.concatenate(ends, axis=0)
    eq = [jnp.exp(jnp.minimum(b - m0, EXP_CLAMP))]
    for d in range(1, n_sub):
        rd = jnp.concatenate([own[j] if j < d else ends[j - d] for j in range(n_sub)], axis=0)
        eq.append(jnp.exp(b - rd))
    ek0 = jnp.exp(jnp.minimum(m0 - b, EXP_CLAMP))
    ek1 = jnp.exp(e1 - b)
    b_last = b_ref[pl.ds(CHUNK - 1, 1), :]
    return dict(q=q, k=k, f=f, sq=sq, sf=sf, b=b, eq=eq, ek0=ek0, ek1=ek1,
                eb=jnp.exp(b), ekl=jnp.exp(b_last - b), el=jnp.exp(b_last))


def _chunk_masks():
    ti = lax.broadcasted_iota(jnp.int32, (CHUNK, CHUNK), 0)
    si = lax.broadcasted_iota(jnp.int32, (CHUNK, CHUNK), 1)
    shift = SUB.bit_length() - 1
    dsub = jnp.right_shift(ti, shift) - jnp.right_shift(si, shift)
    masks = [(dsub == 0) & (si <= ti)]
    masks += [dsub == d for d in range(1, CHUNK // SUB)]
    return masks


def _chunk_attn(p, masks):
    qd = [(p["q"] * e).astype(BF16) for e in p["eq"]]
    k0 = (p["k"] * p["ek0"]).astype(BF16)
    k1 = (p["k"] * p["ek1"]).astype(BF16)
    a = jnp.where(masks[0], _dot_nt(qd[0], k0), 0.0)
    for d in range(1, len(masks)):
        a = jnp.where(masks[d], _dot_nt(qd[d], k1), a)
    return a, qd, k0, k1


def _hgrn_fwd(z, lb, norm_g, *, name, plan=None):
    S = z.shape[0]
    HW = lb.shape[1]
    NH = HW // LANES
    T = _row_tile(S, 512)
    nc = T // CHUNK

    def body(zq_ref, zf_ref, zi_ref, zo_ref, lb_ref, ng_ref, o_ref, of_ref, st_ref, s_scr, b_scr):
        @pl.when(pl.program_id(0) == 0)
        def _():
            s_scr[...] = jnp.zeros_like(s_scr)

        ng = ng_ref[...]
        masks = _chunk_masks()

        def chunk(c, carry):
            rows = pl.ds(pl.multiple_of(c * CHUNK, CHUNK), CHUNK)
            for h in range(NH):
                cols = slice(h * LANES, (h + 1) * LANES)
                p = _chunk_prep(zq_ref[rows, cols], zf_ref[rows, cols], lb_ref[:, cols], b_scr.at[h])
                v = zi_ref[rows, cols].astype(BF16)
                zo = zo_ref[rows, cols]
                st = s_scr[h]
                st_ref[c, h] = st
                a, _, _, _ = _chunk_attn(p, masks)
                o = _dot(a.astype(BF16), v) + _dot_nt((p["q"] * p["eb"]).astype(BF16), st.astype(BF16))
                s_scr[h] = st * p["el"] + _dot_tn(v, (p["k"] * p["ekl"]).astype(BF16))
                o_ref[rows, cols] = o
                r = lax.rsqrt(jnp.mean(o * o, axis=-1, keepdims=True) + EPS)
                of_ref[rows, cols] = (o * r * ng * (zo * _sigmoid(zo))).astype(BF16)
            return carry

        lax.fori_loop(0, nc, chunk, 0)

    part = lambda k: pl.BlockSpec((T, HW), lambda i, k=k: (i, k))
    return _call(
        body, name=name, grid=(S // T,), plan=plan,
        in_specs=[part(1), part(2), part(3), part(4),
                  pl.BlockSpec((1, HW), lambda i: (0, 0)), pl.BlockSpec((1, LANES), lambda i: (0, 0))],
        out_specs=[pl.BlockSpec((T, HW), lambda i: (i, 0)), pl.BlockSpec((T, HW), lambda i: (i, 0)),
                   pl.BlockSpec((nc, NH, LANES, LANES), lambda i: (i, 0, 0, 0))],
        out_shape=[jax.ShapeDtypeStruct((S, HW), F32), jax.ShapeDtypeStruct((S, HW), BF16),
                   jax.ShapeDtypeStruct((S // CHUNK, NH, LANES, LANES), F32)],
        scratch=[pltpu.VMEM((NH, LANES, LANES), F32), pltpu.VMEM((NH, CHUNK, LANES), F32)],
        args=(z, z, z, z, lb, norm_g))


def _hgrn_bwd(z, lb, norm_g, o_raw, states, dof, *, name, plan=None):
    S = z.shape[0]
    HW = lb.shape[1]
    NH = HW // LANES
    T = _row_tile(S, 512)
    nc = T // CHUNK
    nt = S // T

    def body(zq_ref, zf_ref, zi_ref, zo_ref, lb_ref, ng_ref, o_ref, st_ref, dof_ref,
             dzq_ref, dzf_ref, dzi_ref, dzo_ref, dlb_ref, dng_ref, ds_scr, b_scr):
        @pl.when(pl.program_id(0) == 0)
        def _():
            ds_scr[...] = jnp.zeros_like(ds_scr)
            dlb_ref[...] = jnp.zeros_like(dlb_ref)
            dng_ref[...] = jnp.zeros_like(dng_ref)

        ng = ng_ref[...]
        masks = _chunk_masks()
        last_row = lax.broadcasted_iota(jnp.int32, (CHUNK, 1), 0) == CHUNK - 1

        def chunk(cr, carry):
            c = nc - 1 - cr
            rows = pl.ds(pl.multiple_of(c * CHUNK, CHUNK), CHUNK)
            for h in range(NH):
                cols = slice(h * LANES, (h + 1) * LANES)
                lbv = lb_ref[:, cols]
                zq, zf, zo = zq_ref[rows, cols], zf_ref[rows, cols], zo_ref[rows, cols]
                o = o_ref[rows, cols]
                dof_c = dof_ref[rows, cols]
                st = st_ref[c, h]
                dst = ds_scr[h]

                so = _sigmoid(zo)
                r = lax.rsqrt(jnp.mean(o * o, axis=-1, keepdims=True) + EPS)
                ohat = o * r
                d_on = dof_c * (zo * so)
                dzo_ref[rows, cols] = (dof_c * ohat * ng * _dsilu(zo, so)).astype(BF16)
                dng_ref[:, cols] += jnp.sum(d_on * ohat, axis=0, keepdims=True)
                dohat = d_on * ng
                do = (r * (dohat - ohat * jnp.mean(dohat * ohat, axis=-1, keepdims=True))).astype(BF16)

                p = _chunk_prep(zq, zf, lbv, b_scr.at[h])
                q, k = p["q"], p["k"]
                v = zi_ref[rows, cols].astype(BF16)
                a, qd, k0, k1 = _chunk_attn(p, masks)
                ktl = (k * p["ekl"]).astype(BF16)
                dstb = dst.astype(BF16)

                da = _dot_nt(do, v)
                dzi_ref[rows, cols] = (_dot_tn(a.astype(BF16), do) + _dot_nt(ktl, dstb)).astype(BF16)

                da0 = jnp.where(masks[0], da, 0.0).astype(BF16)
                rq = _dot(da0, k0)
                rk0 = _dot_tn(da0, qd[0])
                dq = rq * p["eq"][0]
                db = qd[0].astype(F32) * rq - k0.astype(F32) * rk0
                rk1 = jnp.zeros_like(rk0)
                for d in range(1, len(masks)):
                    dad = jnp.where(masks[d], da, 0.0).astype(BF16)
                    rq = _dot(dad, k1)
                    dq = dq + rq * p["eq"][d]
                    db = db + qd[d].astype(F32) * rq
                    rk1 = rk1 + _dot_tn(dad, qd[d])
                dk = rk0 * p["ek0"] + rk1 * p["ek1"]
                db = db - k1.astype(F32) * rk1
                qe = (q * p["eb"]).astype(BF16)
                rq = _dot(do, st.astype(BF16))
                dq = dq + rq * p["eb"]
                db = db + qe.astype(F32) * rq
                rk = _dot(v, dstb)
                dk = dk + rk * p["ekl"]
                db = db - ktl.astype(F32) * rk

                st_new = st * p["el"] + _dot_tn(v, ktl)
                db = db + jnp.where(last_row, jnp.sum(dstb.astype(F32) * st_new, axis=0, keepdims=True), 0.0)
                dg = _rev_cumsum_rows(db)
                ds_scr[h] = dst * p["el"] + _dot_tn(do, qe)

                dzq_ref[rows, cols] = (dq * _dsilu(zq, p["sq"])).astype(BF16)
                df = dg / p["f"] - dk
                sf = p["sf"]
                dzf_ref[rows, cols] = (df * (1.0 - lbv) * sf * (1.0 - sf)).astype(BF16)
                dlb_ref[:, cols] += jnp.sum(df * (1.0 - sf), axis=0, keepdims=True)
            return carry

        lax.fori_loop(0, nc, chunk, 0)

    rev = lambda i: nt - 1 - i
    part = lambda k: pl.BlockSpec((T, HW), lambda i, k=k: (rev(i), k))
    blk = pl.BlockSpec((T, HW), lambda i: (rev(i), 0))
    vec = pl.BlockSpec((1, HW), lambda i: (0, 0))
    return _call(
        body, name=name, grid=(nt,), plan=plan,
        in_specs=[part(1), part(2), part(3), part(4), vec, pl.BlockSpec((1, LANES), lambda i: (0, 0)),
                  blk, pl.BlockSpec((nc, NH, LANES, LANES), lambda i: (rev(i), 0, 0, 0)), blk],
        out_specs=[blk, blk, blk, blk, vec, vec],
        out_shape=[jax.ShapeDtypeStruct((S, HW), BF16)] * 4 + [jax.ShapeDtypeStruct((1, HW), F32)] * 2,
        scratch=[pltpu.VMEM((NH, LANES, LANES), F32), pltpu.VMEM((NH, CHUNK, LANES), F32)],
        args=(z, z, z, z, lb, norm_g, o_raw, states, dof))


def _gate_specs(T, D):
    half = D // 2
    first = (5 * half) // half
    return [pl.BlockSpec((T, half), lambda i, k=k: (i, first + k)) for k in range(4)]


def _gates(zg_refs, bg_ref, D):
    half = D // 2
    za = jnp.concatenate([zg_refs[0][...], zg_refs[1][...]], axis=1) + bg_ref[:, :D]
    zb = jnp.concatenate([zg_refs[2][...], zg_refs[3][...]], axis=1) + bg_ref[:, D:]
    return _sigmoid(za), _sigmoid(zb)


def _mix_fwd(x, pm, of, z, b_gate, w_pa4, w_pb4, w_o4, *, name, plan=None):
    S, D = x.shape
    P = pm.shape[1]
    T = _row_tile(S, 512)

    def body(x_ref, pm_ref, of_ref, g0, g1, g2, g3, bg_ref, wpa_ref, wpb_ref, wo_ref, xo_ref, ya_ref, yb_ref):
        pmv, ofv = pm_ref[...], of_ref[...]
        ya = jnp.concatenate([_dot(pmv, wpa_ref[k]) for k in range(N_CHIPS)], axis=1)
        yb = jnp.concatenate([_dot(ofv, wpb_ref[k]) for k in range(N_CHIPS)], axis=1)
        ga, gb = _gates((g0, g1, g2, g3), bg_ref, D)
        merged = (ga * ya + gb * yb).astype(BF16)
        xo_ref[...] = x_ref[...] + _dot(merged, wo_ref[...].reshape(D, D))
        ya_ref[...] = ya.astype(BF16)
        yb_ref[...] = yb.astype(BF16)

    row = lambda w: pl.BlockSpec((T, w), lambda i: (i, 0))
    full = lambda a: pl.BlockSpec(a.shape, lambda i: (0,) * a.ndim)
    return _call(
        body, name=name, grid=(S // T,), parallel=(0,), plan=plan,
        in_specs=[row(D), row(P), row(P)] + _gate_specs(T, D) + [full(b_gate), full(w_pa4), full(w_pb4), full(w_o4)],
        out_specs=[row(D), row(D), row(D)],
        out_shape=[jax.ShapeDtypeStruct((S, D), F32), jax.ShapeDtypeStruct((S, D), BF16),
                   jax.ShapeDtypeStruct((S, D), BF16)],
        args=(x, pm, of, z, z, z, z, b_gate, w_pa4, w_pb4, w_o4))


def _mix_bwd(dxm, ya, yb, z, b_gate, pm, of, w_pa4, w_pb4, w_o4, *, name):
    S, D = dxm.shape
    P = pm.shape[1]
    q4 = D // N_CHIPS
    T = _row_tile(S, 256)
    nt = S // T

    def body(dx_ref, ya_ref, yb_ref, g0, g1, g2, g3, bg_ref, pm_ref, of_ref, wpa_ref, wpb_ref, wo_ref,
             dzg_ref, dpm_ref, dof_ref, dwo_ref, dwpa_ref, dwpb_ref, dbg_ref, dwo16_ref, dwpa16_ref, dwpb16_ref):
        i = pl.program_id(0)

        @pl.when(i == 0)
        def _():
            dwo_ref[...] = jnp.zeros_like(dwo_ref)
            dwpa_ref[...] = jnp.zeros_like(dwpa_ref)
            dwpb_ref[...] = jnp.zeros_like(dwpb_ref)
            dbg_ref[...] = jnp.zeros_like(dbg_ref)

        dxb = dx_ref[...].astype(BF16)
        ya = ya_ref[...].astype(F32)
        yb = yb_ref[...].astype(F32)
        ga, gb = _gates((g0, g1, g2, g3), bg_ref, D)
        merged = (ga * ya + gb * yb).astype(BF16)
        dwo_ref[...] += _dot_tn(merged, dxb).reshape(N_CHIPS, q4, D)
        dm = _dot_nt(dxb, wo_ref[...].reshape(D, D))
        dza = dm * ya * ga * (1.0 - ga)
        dzb = dm * yb * gb * (1.0 - gb)
        dzg_ref[:, :D] = dza.astype(BF16)
        dzg_ref[:, D:] = dzb.astype(BF16)
        dbg_ref[:, :D] += jnp.sum(dza, axis=0, keepdims=True)
        dbg_ref[:, D:] += jnp.sum(dzb, axis=0, keepdims=True)
        dya = (dm * ga).astype(BF16)
        dyb = (dm * gb).astype(BF16)
        pmv, ofv = pm_ref[...], of_ref[...]
        dpm = jnp.zeros((T, P), F32)
        dof = jnp.zeros((T, P), F32)
        for k in range(N_CHIPS):
            cols = slice(k * q4, (k + 1) * q4)
            dwpa_ref[k] += _dot_tn(pmv, dya[:, cols])
            dwpb_ref[k] += _dot_tn(ofv, dyb[:, cols])
            dpm = dpm + _dot_nt(dya[:, cols], wpa_ref[k])
            dof = dof + _dot_nt(dyb[:, cols], wpb_ref[k])
        dpm_ref[...] = dpm
        dof_ref[...] = dof

        @pl.when(i == nt - 1)
        def _():
            dwo16_ref[...] = dwo_ref[...].astype(BF16)
            dwpa16_ref[...] = dwpa_ref[...].astype(BF16)
            dwpb16_ref[...] = dwpb_ref[...].astype(BF16)

    row = lambda w: pl.BlockSpec((T, w), lambda i: (i, 0))
    full = lambda a: pl.BlockSpec(a.shape, lambda i: (0,) * a.ndim)
    like = lambda a, dt: jax.ShapeDtypeStruct(a.shape, dt)
    return _call(
        body, name=name, grid=(nt,),
        in_specs=[row(D), row(D), row(D)] + _gate_specs(T, D) + [full(b_gate), row(P), row(P),
                                                                  full(w_pa4), full(w_pb4), full(w_o4)],
        out_specs=[row(2 * D), row(P), row(P), full(w_o4), full(w_pa4), full(w_pb4), full(b_gate),
                   full(w_o4), full(w_pa4), full(w_pb4)],
        out_shape=[jax.ShapeDtypeStruct((S, 2 * D), BF16), jax.ShapeDtypeStruct((S, P), F32),
                   jax.ShapeDtypeStruct((S, P), F32), like(w_o4, F32), like(w_pa4, F32), like(w_pb4, F32),
                   like(b_gate, F32), like(w_o4, BF16), like(w_pa4, BF16), like(w_pb4, BF16)],
        args=(dxm, ya, yb, z, z, z, z, b_gate, pm, of, w_pa4, w_pb4, w_o4))[0]


def _conv3(h, halo, first_tile, cw, cb):
    h = h.astype(F32)
    halo = jnp.where(first_tile, 0.0, halo.astype(F32)[-SUBLANES:, :])
    hp = jnp.concatenate([halo, h], axis=0)
    h1 = _shift_down(hp, 1)
    h2 = _shift_down(hp, 2)
    return cw[0:1, :] * h2 + cw[1:2, :] * h1 + cw[2:3, :] * h + cb, h1, h2


def _ffn_down_fwd(h, conv_w, conv_b, w_down4, x, *, name):
    S, F2 = h.shape
    _, f4, D = w_down4.shape
    T = _row_tile(S, 512)
    nf = 2
    tf = 2 * f4
    hb = T // HALO_CONV

    def body(hv_ref, hg_ref, pv_ref, pg_ref, cwv_ref, cwg_ref, cbv_ref, cbg_ref, wd_ref, x_ref, o_ref, acc_ref):
        i, f = pl.program_id(0), pl.program_id(1)

        @pl.when(f == 0)
        def _():
            acc_ref[...] = jnp.zeros_like(acc_ref)

        val, _, _ = _conv3(hv_ref[...], pv_ref[...], i == 0, cwv_ref[...], cbv_ref[...])
        gate, _, _ = _conv3(hg_ref[...], pg_ref[...], i == 0, cwg_ref[...], cbg_ref[...])
        a = (gate * _sigmoid(gate) * val).astype(BF16)
        acc_ref[...] += _dot(a, wd_ref[...].reshape(tf, D))

        @pl.when(f == nf - 1)
        def _():
            o_ref[...] = x_ref[...] + acc_ref[...]

    prev = lambda i: jnp.maximum(i * hb - 1, 0)
    return _call(
        body, name=name, grid=(S // T, nf), parallel=(0,),
        in_specs=[pl.BlockSpec((T, tf), lambda i, f: (i, f)),
                  pl.BlockSpec((T, tf), lambda i, f: (i, nf + f)),
                  pl.BlockSpec((HALO_CONV, tf), lambda i, f: (prev(i), f)),
                  pl.BlockSpec((HALO_CONV, tf), lambda i, f: (prev(i), nf + f)),
                  pl.BlockSpec((3, tf), lambda i, f: (0, f)),
                  pl.BlockSpec((3, tf), lambda i, f: (0, nf + f)),
                  pl.BlockSpec((1, tf), lambda i, f: (0, f)),
                  pl.BlockSpec((1, tf), lambda i, f: (0, nf + f)),
                  pl.BlockSpec((2, f4, D), lambda i, f: (f, 0, 0)),
                  pl.BlockSpec((T, D), lambda i, f: (i, 0))],
        out_specs=[pl.BlockSpec((T, D), lambda i, f: (i, 0))],
        out_shape=[jax.ShapeDtypeStruct((S, D), F32)],
        scratch=[pltpu.VMEM((T, D), F32)],
        args=(h, h, h, h, conv_w, conv_w, conv_b, conv_b, w_down4, x))[0][0]


def _ffn_down_bwd(dxo, h, conv_w, conv_b, w_down4, *, name, plan=None):
    S, F2 = h.shape
    _, f4, D = w_down4.shape
    F = N_CHIPS * f4
    T = _row_tile(S, 256)
    nf = 2
    tf = 2 * f4
    hb = T // HALO_CONV
    nt = S // T

    def body(dx_ref, hv_ref, hg_ref, pv_ref, pg_ref, cwv_ref, cwg_ref, cbv_ref, cbg_ref, wd_ref,
             dhv_ref, dhg_ref, dwd_ref, dwd16_ref, dcwv_ref, dcwg_ref, dcbv_ref, dcbg_ref, cv_scr, cg_scr):
        i = pl.program_id(1)
        first_tile = i == nt - 1

        @pl.when(i == 0)
        def _():
            cv_scr[...] = jnp.zeros_like(cv_scr)
            cg_scr[...] = jnp.zeros_like(cg_scr)
            dwd_ref[...] = jnp.zeros_like(dwd_ref)
            dcwv_ref[...] = jnp.zeros_like(dcwv_ref)
            dcwg_ref[...] = jnp.zeros_like(dcwg_ref)
            dcbv_ref[...] = jnp.zeros_like(dcbv_ref)
            dcbg_ref[...] = jnp.zeros_like(dcbg_ref)

        dxb = dx_ref[...].astype(BF16)
        hv, hg = hv_ref[...].astype(F32), hg_ref[...].astype(F32)
        val, _, _ = _conv3(hv, pv_ref[...], first_tile, cwv_ref[...], cbv_ref[...])
        gate, _, _ = _conv3(hg, pg_ref[...], first_tile, cwg_ref[...], cbg_ref[...])
        sg = _sigmoid(gate)
        sil = gate * sg
        dwd_ref[...] += _dot_tn((sil * val).astype(BF16), dxb).reshape(2, f4, D)
        da = _dot_nt(dxb, wd_ref[...].reshape(tf, D))
        dval = da * sil
        dgate = da * val * _dsilu(gate, sg)

        def conv_bwd(dhc, h0, cw, c_scr, dh_ref, dcw_ref, dcb_ref):
            ext = jnp.concatenate([dhc, c_scr[...]], axis=0)
            n1 = _shift_up(ext, 1)
            n2 = _shift_up(ext, 2)
            dh_ref[...] = (cw[2:3, :] * dhc + cw[1:2, :] * n1 + cw[0:1, :] * n2).astype(BF16)
            c_scr[...] = dhc[:SUBLANES, :]
            dcw_ref[0:1, :] += jnp.sum(n2 * h0, axis=0, keepdims=True)
            dcw_ref[1:2, :] += jnp.sum(n1 * h0, axis=0, keepdims=True)
            dcw_ref[2:3, :] += jnp.sum(dhc * h0, axis=0, keepdims=True)
            dcb_ref[...] += jnp.sum(dhc, axis=0, keepdims=True)

        conv_bwd(dval, hv, cwv_ref[...], cv_scr, dhv_ref, dcwv_ref, dcbv_ref)
        conv_bwd(dgate, hg, cwg_ref[...], cg_scr, dhg_ref, dcwg_ref, dcbg_ref)

        @pl.when(i == nt - 1)
        def _():
            dwd16_ref[...] = dwd_ref[...].astype(BF16)

    rev = lambda i: nt - 1 - i
    prev = lambda i: jnp.maximum(rev(i) * hb - 1, 0)
    wd_spec = pl.BlockSpec((2, f4, D), lambda f, i: (f, 0, 0))
    return _call(
        body, name=name, grid=(nf, nt), plan=plan,
        in_specs=[pl.BlockSpec((T, D), lambda f, i: (rev(i), 0)),
                  pl.BlockSpec((T, tf), lambda f, i: (rev(i), f)),
                  pl.BlockSpec((T, tf), lambda f, i: (rev(i), nf + f)),
                  pl.BlockSpec((HALO_CONV, tf), lambda f, i: (prev(i), f)),
                  pl.BlockSpec((HALO_CONV, tf), lambda f, i: (prev(i), nf + f)),
                  pl.BlockSpec((3, tf), lambda f, i: (0, f)),
                  pl.BlockSpec((3, tf), lambda f, i: (0, nf + f)),
                  pl.BlockSpec((1, tf), lambda f, i: (0, f)),
                  pl.BlockSpec((1, tf), lambda f, i: (0, nf + f)),
                  wd_spec],
        out_specs=[pl.BlockSpec((T, tf), lambda f, i: (rev(i), f)),
                   pl.BlockSpec((T, tf), lambda f, i: (rev(i), f)),
                   wd_spec, wd_spec,
                   pl.BlockSpec((3, tf), lambda f, i: (0, f)),
                   pl.BlockSpec((3, tf), lambda f, i: (0, f)),
                   pl.BlockSpec((1, tf), lambda f, i: (0, f)),
                   pl.BlockSpec((1, tf), lambda f, i: (0, f))],
        out_shape=[jax.ShapeDtypeStruct((S, F), BF16), jax.ShapeDtypeStruct((S, F), BF16),
                   jax.ShapeDtypeStruct((N_CHIPS, f4, D), F32), jax.ShapeDtypeStruct((N_CHIPS, f4, D), BF16),
                   jax.ShapeDtypeStruct((3, F), F32), jax.ShapeDtypeStruct((3, F), F32),
                   jax.ShapeDtypeStruct((1, F), F32), jax.ShapeDtypeStruct((1, F), F32)],
        scratch=[pltpu.VMEM((SUBLANES, tf), F32), pltpu.VMEM((SUBLANES, tf), F32)],
        args=(dxo, h, h, h, h, conv_w, conv_w, conv_b, conv_b, w_down4))


def _final_loss(x, g, target, *, name):
    S, D = x.shape
    T = _row_tile(S, 512)

    def body(x_ref, g_ref, t_ref, loss_ref, dx_ref, dg_ref):
        @pl.when(pl.program_id(0) == 0)
        def _():
            loss_ref[...] = jnp.zeros_like(loss_ref)
            dg_ref[...] = jnp.zeros_like(dg_ref)

        xf = x_ref[...]
        r = lax.rsqrt(jnp.mean(xf * xf, axis=-1, keepdims=True) + EPS)
        xhat = xf * r
        err = xhat * g_ref[...] - t_ref[...]
        loss_ref[...] += jnp.sum(err * err, axis=0, keepdims=True) * (0.5 / D)
        dy = err * (1.0 / D)
        dxhat = dy * g_ref[...]
        dx_ref[...] = r * (dxhat - xhat * jnp.mean(dxhat * xhat, axis=-1, keepdims=True))
        dg_ref[...] += jnp.sum(dy * xhat, axis=0, keepdims=True)

    return _call(
        body, name=name, grid=(S // T,),
        in_specs=[pl.BlockSpec((T, D), lambda i: (i, 0)), pl.BlockSpec((1, D), lambda i: (0, 0)),
                  pl.BlockSpec((T, D), lambda i: (i, 0))],
        out_specs=[pl.BlockSpec((1, D), lambda i: (0, 0)), pl.BlockSpec((T, D), lambda i: (i, 0)),
                   pl.BlockSpec((1, D), lambda i: (0, 0))],
        out_shape=[jax.ShapeDtypeStruct((1, D), F32), jax.ShapeDtypeStruct((S, D), F32),
                   jax.ShapeDtypeStruct((1, D), F32)],
        args=(x, g, target))[0]


BIG = ("w_in", "w_pa", "w_pb", "w_o", "w_up", "w_down")
SMALL = ("norm1_g", "b_gate", "pool_w", "pool_scale", "lb_logits", "hgrn_norm_g", "norm2_g", "conv_b", "final_g")
WEIGHTS = ("norm1_g", "w_in", "b_gate", "pool_w", "pool_scale", "lb_logits", "hgrn_norm_g", "w_pa", "w_pb", "w_o",
           "norm2_g", "w_up", "conv_w", "conv_b", "w_down", "final_g")


def _lower_bounds(lb_logits):
    soft = jax.nn.softmax(lb_logits.astype(F32), axis=0)
    cum = jnp.cumsum(soft, axis=0)
    return cum - cum[0:1]


def _step(x, target, sm, wts, shards=None):
    L = sm["norm1_g"].shape[0]
    wts = dict(wts)
    dist = shards is not None
    lbs, lb_vjp = jax.vjp(_lower_bounds, sm["lb_logits"])
    row = lambda a: a.reshape(1, -1)
    conv_w = sm.get("conv_w")

    def gather(names_layers, with_conv=False):
        items = [(shards[n], "rows", l) for n, l in names_layers]
        if with_conv:
            items.append((shards["conv_w"], "layer", None))
        return _GatherPlan(items)

    def landed(names_layers, outs):
        for key, arr in zip(names_layers, outs):
            wts[key] = arr

    saved = []
    for l in range(L):
        g1 = row(sm["norm1_g"][l])
        plan = None
        if dist and l == 0:
            rest0 = [(n, 0) for n in BIG if n != "w_in"]
            plan = gather(rest0, with_conv=True)
        (xn1, z), got = _norm_matmul(x, g1, wts[("w_in", l)], name=f"in_proj_{l}", plan=plan)
        if plan is not None:
            landed(rest0, got)
            full = got[-1]
            conv_w = jnp.concatenate([full[:, k] for k in range(N_CHIPS)], axis=2)
        pm = _pool_fwd(z, sm["pool_w"][l], row(sm["pool_scale"][l]), name=f"pool_fwd_{l}")
        ahead = dist and l + 1 < L
        keys = [("w_in", l + 1)]
        plan = gather(keys) if ahead else None
        (o_raw, of, states), got = _hgrn_fwd(z, row(lbs[l]), row(sm["hgrn_norm_g"][l]), name=f"hgrn_fwd_{l}",
                                             plan=plan)
        landed(keys, got)
        keys = [(n, l + 1) for n in ("w_pa", "w_pb", "w_o")]
        plan = gather(keys) if ahead else None
        (x_mid, ya, yb), got = _mix_fwd(x, pm, of, z, row(sm["b_gate"][l]), wts[("w_pa", l)], wts[("w_pb", l)],
                                        wts[("w_o", l)], name=f"mix_fwd_{l}", plan=plan)
        landed(keys, got)
        keys = [(n, l + 1) for n in ("w_up", "w_down")]
        plan = gather(keys) if ahead else None
        (xn2, h), got = _norm_matmul(x_mid, row(sm["norm2_g"][l]), wts[("w_up", l)], name=f"up_{l}",
                                     out_dtype=BF16, plan=plan)
        landed(keys, got)
        x_out = _ffn_down_fwd(h, conv_w[l], row(sm["conv_b"][l]), wts[("w_down", l)], x_mid, name=f"down_fwd_{l}")
        saved.append(dict(x=x, xn1=xn1, z=z, pm=pm, o_raw=o_raw, of=of, states=states,
                          x_mid=x_mid, ya=ya, yb=yb, xn2=xn2, h=h))
        x = x_out

    loss_cols, dx, d_final_g = _final_loss(x, row(sm["final_g"]), target, name="final_loss")

    small = {k: [None] * L for k in ("norm1_g", "b_gate", "pool_w", "pool_scale", "hgrn_norm_g", "norm2_g",
                                     "conv_w", "conv_b")}
    big32, big16, recv = {}, {}, {}
    dlbs = [None] * L
    pending = []

    def scatter():
        keys = list(pending)
        del pending[:]
        return keys, _ScatterPlan([big16[k] for k in keys])

    def sent(keys, outs):
        for key, arr in zip(keys, outs):
            recv[key] = arr

    for l in reversed(range(L)):
        s = saved[l]
        keys, plan = scatter() if dist and pending else (None, None)
        (dhv, dhg, d_wd, d_wd16, dcwv, dcwg, dcbv, dcbg), got = _ffn_down_bwd(
            dx, s["h"], conv_w[l], row(sm["conv_b"][l]), wts[("w_down", l)], name=f"down_bwd_{l}", plan=plan)
        if plan is not None:
            sent(keys, got)
        big32[("w_down", l)], big16[("w_down", l)] = d_wd, d_wd16
        small["conv_w"][l] = jnp.concatenate([dcwv, dcwg], axis=1)
        small["conv_b"][l] = jnp.concatenate([dcbv, dcbg], axis=1)[0]
        big32[("w_up", l)], big16[("w_up", l)] = _wgrad(s["xn2"], [dhv, dhg], name=f"up_wgrad_{l}")
        (dxm, dg2), _ = _dgrad_norm([dhv, dhg], wts[("w_up", l)], s["x_mid"], row(sm["norm2_g"][l]), dx,
                                    name=f"up_dgrad_{l}")
        small["norm2_g"][l] = dg2[0]

        dzg, dpm, dof, d_wo, d_wpa, d_wpb, dbg, d_wo16, d_wpa16, d_wpb16 = _mix_bwd(
            dxm, s["ya"], s["yb"], s["z"], row(sm["b_gate"][l]), s["pm"], s["of"],
            wts[("w_pa", l)], wts[("w_pb", l)], wts[("w_o", l)], name=f"mix_bwd_{l}")
        big32[("w_o", l)], big16[("w_o", l)] = d_wo, d_wo16
        big32[("w_pa", l)], big16[("w_pa", l)] = d_wpa, d_wpa16
        big32[("w_pb", l)], big16[("w_pb", l)] = d_wpb, d_wpb16
        small["b_gate"][l] = dbg[0]
        pending.extend([("w_down", l), ("w_up", l), ("w_o", l), ("w_pa", l), ("w_pb", l)])

        du, dpw, dps = _pool_bwd(s["z"], dpm, sm["pool_w"][l], row(sm["pool_scale"][l]), name=f"pool_bwd_{l}")
        small["pool_w"][l], small["pool_scale"][l] = dpw, dps[0]

        keys, plan = scatter() if dist and l == 0 else (None, None)
        (dzq, dzf, dzi, dzo, dlb, dng), got = _hgrn_bwd(s["z"], row(lbs[l]), row(sm["hgrn_norm_g"][l]), s["o_raw"],
                                                      s["states"], dof, name=f"hgrn_bwd_{l}", plan=plan)
        if plan is not None:
            sent(keys, got)
        dlbs[l] = dlb[0]
        small["hgrn_norm_g"][l] = jnp.sum(dng.reshape(-1, LANES), axis=0)

        dz = jnp.concatenate([du, dzq, dzf, dzi, dzo, dzg], axis=1)
        big32[("w_in", l)], big16[("w_in", l)] = _wgrad(s["xn1"], [dz], name=f"in_wgrad_{l}")
        pending.append(("w_in", l))
        keys, plan = scatter() if dist and l == 0 else (None, None)
        (dx, dg1), got = _dgrad_norm([dz], wts[("w_in", l)], s["x"], row(sm["norm1_g"][l]), dxm,
                                     name=f"in_dgrad_{l}", plan=plan)
        if plan is not None:
            sent(keys, got)
        small["norm1_g"][l] = dg1[0]

    out = {k: jnp.stack(v) for k, v in small.items()}
    out["lb_logits"] = lb_vjp(jnp.stack(dlbs))[0]
    out["final_g"] = d_final_g[0]
    return loss_cols, dx, out, big32, recv


def _elementwise_rows(R, n, n_arrays):
    if 2 * n_arrays * R * n * 4 <= VMEM_LIMIT // 4 or R % 8:
        return R
    want = 8
    while want * 2 * n * 4 <= 1024 * 1024:
        want *= 2
    return _row_tile(R, want)


def _sum_layers(own, got, chip, *, name):
    L = len(own)
    _, r, n = own[0].shape
    T = _elementwise_rows(r, n, 6)
    nt = r // T

    def body(chip_ref, *refs):
        o_ref = refs[-1]
        l = pl.program_id(0)
        for k in range(L):
            @pl.when(l == k)
            def _():
                own_ref, got_ref = refs[2 * k], refs[2 * k + 1]
                acc = own_ref[...]
                for j in range(3):
                    acc = acc + got_ref[j].astype(F32)
                o_ref[...] = acc

    in_specs = []
    for k in range(L):
        hold = 0 if k else nt - 1
        in_specs.append(pl.BlockSpec((None, T, n), lambda l, i, c, k=k, hold=hold: (c[0], jnp.where(l == k, i, hold), 0)))
        in_specs.append(pl.BlockSpec((3, T, n), lambda l, i, c, k=k, hold=hold: (0, jnp.where(l == k, i, hold), 0)))
    grid_spec = pltpu.PrefetchScalarGridSpec(
        num_scalar_prefetch=1, grid=(L, nt), in_specs=in_specs,
        out_specs=pl.BlockSpec((None, T, n), lambda l, i, c: (l, i, 0)))
    args = [a for pair in zip(own, got) for a in pair]
    return pl.pallas_call(
        body, name=name, grid_spec=grid_spec, out_shape=jax.ShapeDtypeStruct((L, r, n), F32),
        compiler_params=pltpu.CompilerParams(dimension_semantics=("arbitrary", "arbitrary"),
                                             vmem_limit_bytes=VMEM_LIMIT),
    )(chip, *args)


def _sum_stack(first, rest, *, name):
    R, n = first.shape
    K = rest.shape[0]
    T = _elementwise_rows(R, n, K + 2)

    def body(a_ref, r_ref, o_ref):
        acc = a_ref[...]
        for j in range(K):
            acc = acc + r_ref[j].astype(F32)
        o_ref[...] = acc

    return _call(
        body, name=name, grid=(R // T,), parallel=(0,),
        in_specs=[pl.BlockSpec((T, n), lambda i: (i, 0)), pl.BlockSpec((K, T, n), lambda i: (0, i, 0))],
        out_specs=[pl.BlockSpec((T, n), lambda i: (i, 0))],
        out_shape=[jax.ShapeDtypeStruct((R, n), F32)],
        args=(first, rest))[0][0]


def _adamw(w, m, v, g_parts, *, name):
    R, n = w.shape
    n_g = len(g_parts)
    T = _elementwise_rows(R, n, 7 + n_g)

    def body(*refs):
        w_ref, m_ref, v_ref = refs[:3]
        g_refs = refs[3:3 + n_g]
        go_ref, d_ref, mo_ref, vo_ref = refs[3 + n_g:]
        g_ = g_refs[0][...]
        for r in g_refs[1:]:
            g_ = g_ + r[...]
        m_ = ADAM_B1 * m_ref[...] + (1.0 - ADAM_B1) * g_
        v_ = ADAM_B2 * v_ref[...] + (1.0 - ADAM_B2) * (g_ * g_)
        m_hat = m_ / (1.0 - ADAM_B1 ** ADAM_STEP)
        v_hat = v_ / (1.0 - ADAM_B2 ** ADAM_STEP)
        go_ref[...] = g_
        d_ref[...] = -ADAM_LR * (m_hat / (jnp.sqrt(v_hat) + ADAM_EPS) + ADAM_WD * w_ref[...])
        mo_ref[...] = m_
        vo_ref[...] = v_

    blk = pl.BlockSpec((T, n), lambda i: (i, 0))
    return _call(
        body, name=name, grid=(R // T,), parallel=(0,),
        in_specs=[blk] * (3 + n_g), out_specs=[blk] * 4,
        out_shape=[jax.ShapeDtypeStruct((R, n), F32)] * 4,
        args=(w, m, v, *g_parts))[0]


PACK_ALIGN = 8 * LANES


def _pack(pieces):
    flat = []
    for a in pieces:
        a = a.reshape(-1)
        pad = (-a.shape[0]) % PACK_ALIGN
        flat.append(jnp.pad(a, (0, pad)) if pad else a)
    return jnp.concatenate(flat).reshape(-1, LANES)


def _unpack(buf, shapes):
    flat = buf.reshape(-1)
    out, off = [], 0
    for shp in shapes:
        size = 1
        for s in shp:
            size *= s
        out.append(flat[off:off + size].reshape(shp))
        off += size + (-size) % PACK_ALIGN
    return out


def kernel(x, norm1_g, w_in, b_gate, pool_w, pool_scale, lb_logits, hgrn_norm_g, w_pa, w_pb, w_o, norm2_g, w_up, conv_w, conv_b, w_down, final_g, loss_target, m_norm1_g, m_w_in, m_b_gate, m_pool_w, m_pool_scale, m_lb_logits, m_hgrn_norm_g, m_w_pa, m_w_pb, m_w_o, m_norm2_g, m_w_up, m_conv_w, m_conv_b, m_w_down, m_final_g, v_norm1_g, v_w_in, v_b_gate, v_pool_w, v_pool_scale, v_lb_logits, v_hgrn_norm_g, v_w_pa, v_w_pb, v_w_o, v_norm2_g, v_w_up, v_conv_w, v_conv_b, v_w_down, v_final_g):
    env = dict(locals())
    w = {n: env[n] for n in WEIGHTS}
    m = {n: env["m_" + n] for n in WEIGHTS}
    v = {n: env["v_" + n] for n in WEIGHTS}
    my_chip = 2 * lax.axis_index("x") + lax.axis_index("y")
    L = w_in.shape[0]

    shards = {n: w[n].astype(BF16) for n in BIG}
    shards["conv_w"] = w["conv_w"]
    w_in0 = _run_plan(_GatherPlan([(shards["w_in"], "rows", 0)]), name="gather_w_in0")[0]
    sm = {n: w[n] for n in SMALL}
    loss_cols, grad_x, g_small, big32, recv = _step(x[0], loss_target[0], sm, {("w_in", 0): w_in0}, shards)

    chip = my_chip.reshape(1).astype(jnp.int32)
    sums = [_sum_layers([big32[(n, l)] for l in range(L)], [recv[(n, l)] for l in range(L)], chip,
                        name="chip_sum_" + n) for n in BIG]
    theirs = _run_plan(_SiblingPlan(sums), name="grad_sibling")
    g, delta, new_m, new_v = {}, {}, {}, {}
    for n, mine, other in zip(BIG, sums, theirs):
        shp = w[n].shape
        two_d = lambda a: a.reshape(-1, shp[-1])
        outs = _adamw(two_d(w[n]), two_d(m[n]), two_d(v[n]), [two_d(mine), two_d(other)], name="adamw_" + n)
        g[n], delta[n], new_m[n], new_v[n] = [a.reshape(shp) for a in outs]

    small_names = list(SMALL)
    small_pieces = [g_small[n] for n in small_names] + [g_small["conv_w"], loss_cols]
    small_shapes = [a.shape for a in small_pieces]
    packed = _pack(small_pieces)
    Rs = packed.shape[0]
    everyone = _run_plan(_EveryonePlan(packed), name="gather_small")[0].reshape(8, Rs, LANES)
    summed = _unpack(_sum_stack(everyone[0], everyone[1:], name="small_sum"), small_shapes)
    loss = jnp.sum(summed[-1])
    cshard = w["conv_w"].shape[2]
    gs = dict(zip(small_names, summed[:len(small_names)]))
    g_cw = lax.dynamic_slice_in_dim(summed[-2], my_chip * cshard, cshard, axis=2)

    sm_out = _adamw(_pack([w[n] for n in small_names]), _pack([m[n] for n in small_names]),
                    _pack([v[n] for n in small_names]), [_pack([gs[n] for n in small_names])], name="adamw_small")
    shapes = [w[n].shape for n in small_names]
    for n, g_, d_, m_, v_ in zip(small_names, *[_unpack(a, shapes) for a in sm_out]):
        g[n], delta[n], new_m[n], new_v[n] = g_, d_, m_, v_
    shp = w["conv_w"].shape
    two_d = lambda a: a.reshape(-1, shp[-1])
    outs = _adamw(two_d(w["conv_w"]), two_d(m["conv_w"]), two_d(v["conv_w"]), [two_d(g_cw)], name="adamw_conv_w")
    g["conv_w"], delta["conv_w"], new_m["conv_w"], new_v["conv_w"] = [a.reshape(shp) for a in outs]

    return (loss, grad_x[None], *[g[n] for n in WEIGHTS], *[delta[n] for n in WEIGHTS],
            *[new_m[n] for n in WEIGHTS], *[new_v[n] for n in WEIGHTS])
```

```python
import jax
import jax.numpy as jnp
from jax import lax
from jax.experimental import pallas as pl
from jax.experimental.pallas import tpu as pltpu

F32 = jnp.float32
BF16 = jnp.bfloat16

EPS = 1e-6
CHUNK = 64
SUB = 32
LANES = 128
SUBLANES = 8
POOL_WINDOWS = (2, 4, 8, 16)
HALO_POOL = 16
HALO_CONV = 16
EXP_CLAMP = 80.0

ADAM_LR = 0.001
ADAM_B1 = 0.9
ADAM_B2 = 0.999
ADAM_EPS = 1e-08
ADAM_WD = 0.01
ADAM_STEP = 10

VMEM_LIMIT = 56 * 1024 * 1024
MESH_ID = pl.DeviceIdType.MESH
N_CHIPS = 4
ANY = pl.BlockSpec(memory_space=pl.ANY)


def _dot(a, b):
    return jnp.dot(a, b, preferred_element_type=F32)


def _dot_nt(a, b):
    return lax.dot_general(a, b, (((1,), (1,)), ((), ())), preferred_element_type=F32)


def _dot_tn(a, b):
    return lax.dot_general(a, b, (((0,), (0,)), ((), ())), preferred_element_type=F32)


def _sigmoid(x):
    return jax.nn.sigmoid(x)


def _dsilu(x, s):
    return s * (1.0 + x * (1.0 - s))


def _row_tile(rows, want):
    t = min(rows, want)
    while rows % t:
        t //= 2
    return t


def _place():
    x, y, c = lax.axis_index("x"), lax.axis_index("y"), lax.axis_index("c")
    chips = [(1 - x, y), (x, 1 - y), (1 - x, 1 - y)]
    return x, y, c, chips


def _remote(src, dst, sems, k, to):
    return pltpu.make_async_remote_copy(src_ref=src, dst_ref=dst, send_sem=sems[0].at[k], recv_sem=sems[1].at[k],
                                        device_id=to, device_id_type=MESH_ID)


class _GatherPlan:
    def __init__(self, items):
        self.items = items
        self.inputs = [a for a, _, _ in items]
        self.out_shapes = []
        for a, kind, _ in items:
            shp = (N_CHIPS,) + a.shape[1:] if kind == "rows" else (a.shape[0], N_CHIPS) + a.shape[1:]
            self.out_shapes.append(jax.ShapeDtypeStruct(shp, a.dtype))
        n = len(items)
        self.scratch = [pltpu.SemaphoreType.DMA((6 * n,)), pltpu.SemaphoreType.DMA((6 * n,)),
                        pltpu.SemaphoreType.DMA((2 * n,))]

    def _views(self, i, src, dst):
        _, kind, l = self.items[i]
        if kind == "rows":
            half = src.shape[1] // 2
            part = lambda core: src.at[l, pl.ds(core * half, half), :]
            land = lambda chip, core: dst.at[chip, pl.ds(core * half, half), :]
        else:
            part = lambda core: src.at[core]
            land = lambda chip, core: dst.at[core, chip]
        return part, land

    def start(self, srcs, dsts, sems):
        x, y, c, chips = _place()
        me = 2 * x + y
        for i, (src, dst) in enumerate(zip(srcs, dsts)):
            part, land = self._views(i, src, dst)
            for core in range(2):
                pltpu.make_async_copy(part(core), land(me, core), sems[2].at[2 * i + core]).start()
            for j, (px, py) in enumerate(chips):
                _remote(part(c), land(me, c), sems, 6 * i + j, (px, py, c)).start()

    def finish(self, srcs, dsts, sems):
        x, y, c, chips = _place()
        me = 2 * x + y
        sibling = (x, y, 1 - c)
        for i, (src, dst) in enumerate(zip(srcs, dsts)):
            part, land = self._views(i, src, dst)
            for j, (px, py) in enumerate(chips):
                got = land(2 * px + py, c)
                _remote(got, got, sems, 6 * i + j, (px, py, c)).wait_recv()
                _remote(got, got, sems, 6 * i + 3 + j, sibling).start()
        for i, (src, dst) in enumerate(zip(srcs, dsts)):
            part, land = self._views(i, src, dst)
            for j, (px, py) in enumerate(chips):
                got = land(2 * px + py, 1 - c)
                _remote(got, got, sems, 6 * i + 3 + j, sibling).wait_recv()
            for j, (px, py) in enumerate(chips):
                _remote(part(c), land(me, c), sems, 6 * i + j, (px, py, c)).wait_send()
                mine = land(2 * px + py, c)
                _remote(mine, mine, sems, 6 * i + 3 + j, sibling).wait_send()
            for core in range(2):
                pltpu.make_async_copy(part(core), land(me, core), sems[2].at[2 * i + core]).wait()


class _ScatterPlan:
    def __init__(self, items):
        self.inputs = list(items)
        self.out_shapes = [jax.ShapeDtypeStruct((3,) + a.shape[1:], a.dtype) for a in items]
        n = len(items)
        self.scratch = [pltpu.SemaphoreType.DMA((3 * n,)), pltpu.SemaphoreType.DMA((3 * n,))]

    def _copies(self, srcs, dsts, sems):
        x, y, c, chips = _place()
        return [_remote(src.at[2 * px + py], dst.at[j], sems, 3 * i + j, (px, py, c))
                for i, (src, dst) in enumerate(zip(srcs, dsts)) for j, (px, py) in enumerate(chips)]

    def start(self, srcs, dsts, sems):
        for cp in self._copies(srcs, dsts, sems):
            cp.start()

    def finish(self, srcs, dsts, sems):
        copies = self._copies(srcs, dsts, sems)
        for cp in copies:
            cp.wait_recv()
        for cp in copies:
            cp.wait_send()


class _SiblingPlan:
    def __init__(self, items):
        self.inputs = list(items)
        self.out_shapes = [jax.ShapeDtypeStruct(a.shape, a.dtype) for a in items]
        n = len(items)
        self.scratch = [pltpu.SemaphoreType.DMA((n,)), pltpu.SemaphoreType.DMA((n,))]

    def _copies(self, srcs, dsts, sems):
        x, y, c, _ = _place()
        return [_remote(src, dst, sems, i, (x, y, 1 - c)) for i, (src, dst) in enumerate(zip(srcs, dsts))]

    def start(self, srcs, dsts, sems):
        for cp in self._copies(srcs, dsts, sems):
            cp.start()

    def finish(self, srcs, dsts, sems):
        copies = self._copies(srcs, dsts, sems)
        for cp in copies:
            cp.wait_recv()
        for cp in copies:
            cp.wait_send()


class _EveryonePlan:
    def __init__(self, block):
        self.inputs = [block]
        self.m = block.shape[0]
        self.out_shapes = [jax.ShapeDtypeStruct((8 * self.m,) + block.shape[1:], block.dtype)]
        self.scratch = [pltpu.SemaphoreType.DMA((7,)), pltpu.SemaphoreType.DMA((7,)), pltpu.SemaphoreType.DMA((1,))]

    def _rows(self, dst, px, py, pc):
        return dst.at[pl.ds((4 * px + 2 * py + pc) * self.m, self.m), :]

    def start(self, srcs, dsts, sems):
        x, y, c, chips = _place()
        src, dst = srcs[0], dsts[0]
        pltpu.make_async_copy(src, self._rows(dst, x, y, c), sems[2].at[0]).start()
        _remote(src, self._rows(dst, x, y, c), sems, 0, (x, y, 1 - c)).start()
        for j, (px, py) in enumerate(chips):
            _remote(src, self._rows(dst, x, y, c), sems, 1 + j, (px, py, c)).start()

    def finish(self, srcs, dsts, sems):
        x, y, c, chips = _place()
        src, dst = srcs[0], dsts[0]
        sibling = (x, y, 1 - c)
        for j, (px, py) in enumerate(chips):
            got = self._rows(dst, px, py, c)
            _remote(got, got, sems, 1 + j, (px, py, c)).wait_recv()
            _remote(got, got, sems, 4 + j, sibling).start()
        sib = self._rows(dst, x, y, 1 - c)
        _remote(sib, sib, sems, 0, sibling).wait_recv()
        for j, (px, py) in enumerate(chips):
            got = self._rows(dst, px, py, 1 - c)
            _remote(got, got, sems, 4 + j, sibling).wait_recv()
        mine = self._rows(dst, x, y, c)
        _remote(src, mine, sems, 0, sibling).wait_send()
        for j, (px, py) in enumerate(chips):
            _remote(src, mine, sems, 1 + j, (px, py, c)).wait_send()
            got = self._rows(dst, px, py, c)
            _remote(got, got, sems, 4 + j, sibling).wait_send()
        pltpu.make_async_copy(src, mine, sems[2].at[0]).wait()


def _call(body, *, name, grid, in_specs, out_specs, out_shape, args, scratch=(), parallel=(), plan=None):
    n_in, n_out, n_scr = len(in_specs), len(out_shape), len(scratch)
    sem = tuple("parallel" if (a in parallel and plan is None) else "arbitrary" for a in range(len(grid)))
    params = pltpu.CompilerParams(dimension_semantics=sem, vmem_limit_bytes=VMEM_LIMIT)
    if plan is None:
        outs = pl.pallas_call(body, name=name, grid=grid, in_specs=in_specs, out_specs=out_specs,
                              out_shape=out_shape, scratch_shapes=list(scratch), compiler_params=params)(*args)
        return list(outs), []
    p_in, p_out, p_scr = len(plan.inputs), len(plan.out_shapes), len(plan.scratch)

    def wrapped(*refs):
        ins, refs = refs[:n_in], refs[n_in:]
        p_ins, refs = refs[:p_in], refs[p_in:]
        outs, refs = refs[:n_out], refs[n_out:]
        p_outs, refs = refs[:p_out], refs[p_out:]
        scr, p_sems = refs[:n_scr], refs[n_scr:]
        ids = [pl.program_id(a) for a in range(len(grid))]
        first = _all([i == 0 for i in ids])
        last = _all([i == n - 1 for i, n in zip(ids, grid)])

        @pl.when(first)
        def _():
            plan.start(p_ins, p_outs, p_sems)

        body(*ins, *outs, *scr)

        @pl.when(last)
        def _():
            plan.finish(p_ins, p_outs, p_sems)

    outs = pl.pallas_call(
        wrapped, name=name, grid=grid,
        in_specs=list(in_specs) + [ANY] * p_in, out_specs=list(out_specs) + [ANY] * p_out,
        out_shape=list(out_shape) + list(plan.out_shapes),
        scratch_shapes=list(scratch) + list(plan.scratch), compiler_params=params,
    )(*args, *plan.inputs)
    return list(outs[:n_out]), list(outs[n_out:])


def _all(conds):
    out = conds[0]
    for c in conds[1:]:
        out = out & c
    return out


def _run_plan(plan, *, name):
    p_in, p_out = len(plan.inputs), len(plan.out_shapes)

    def body(*refs):
        srcs, dsts, sems = refs[:p_in], refs[p_in:p_in + p_out], refs[p_in + p_out:]
        plan.start(srcs, dsts, sems)
        plan.finish(srcs, dsts, sems)

    return list(pl.pallas_call(body, name=name, in_specs=[ANY] * p_in, out_specs=[ANY] * p_out,
                               out_shape=list(plan.out_shapes), scratch_shapes=list(plan.scratch))(*plan.inputs))


def _norm_matmul(x, g, w4, *, name, out_dtype=F32, plan=None):
    S, D = x.shape
    n4 = w4.shape[2]
    tm = _row_tile(S, 1024)

    def body(x_ref, g_ref, w_ref, xn_ref, o_ref):
        @pl.when(pl.program_id(1) == 0)
        def _():
            xf = x_ref[...]
            r = lax.rsqrt(jnp.mean(xf * xf, axis=-1, keepdims=True) + EPS)
            xn_ref[...] = (xf * r * g_ref[...]).astype(BF16)

        o_ref[...] = _dot(xn_ref[...], w_ref[...]).astype(out_dtype)

    return _call(
        body, name=name, grid=(S // tm, N_CHIPS), parallel=(0,), plan=plan,
        in_specs=[pl.BlockSpec((tm, D), lambda i, j: (i, 0)),
                  pl.BlockSpec((1, D), lambda i, j: (0, 0)),
                  pl.BlockSpec((None, D, n4), lambda i, j: (j, 0, 0))],
        out_specs=[pl.BlockSpec((tm, D), lambda i, j: (i, 0)),
                   pl.BlockSpec((tm, n4), lambda i, j: (i, j))],
        out_shape=[jax.ShapeDtypeStruct((S, D), BF16), jax.ShapeDtypeStruct((S, N_CHIPS * n4), out_dtype)],
        args=(x, g, w4))


def _segments(widths, n4):
    per_chip = [[] for _ in range(N_CHIPS)]
    c0 = 0
    for p, w in enumerate(widths):
        a = c0
        while a < c0 + w:
            k = a // n4
            b = min(c0 + w, (k + 1) * n4)
            per_chip[k].append((p, (a - c0, b - c0), (a - k * n4, b - k * n4)))
            a = b
        c0 += w
    assert c0 == N_CHIPS * n4
    return per_chip


def _piece_specs(pieces, n4, tm):
    per_chip = _segments([p.shape[1] for p in pieces], n4)
    specs, local, start = [], [[] for _ in range(N_CHIPS)], 0
    for p, arr in enumerate(pieces):
        chips = [k for k in range(N_CHIPS) if any(seg[0] == p for seg in per_chip[k])]
        lo, hi = chips[0], chips[-1]
        tiled = arr.shape[1] % n4 == 0 and start % n4 == 0
        start += arr.shape[1]
        if tiled:
            imap = lambda k, i, lo=lo, hi=hi: (jnp.where((k >= lo) & (k <= hi), i, 0), jnp.clip(k - lo, 0, hi - lo))
            specs.append(pl.BlockSpec((tm, n4), imap))
        else:
            imap = lambda k, i, lo=lo, hi=hi: (jnp.where((k >= lo) & (k <= hi), i, 0), 0)
            specs.append(pl.BlockSpec((tm, arr.shape[1]), imap))
        for k in chips:
            for q, (pa, pb), cols in per_chip[k]:
                if q == p:
                    local[k].append((p, (0, n4) if tiled else (pa, pb), cols))
    return specs, local


def _dgrad_norm(dys, w4, x, g, dres, *, name, plan=None):
    S, D = x.shape
    n4 = w4.shape[2]
    tm = _row_tile(S, 512)
    per_chip = _segments([a.shape[1] for a in dys], n4)
    n_p = len(dys)

    def body(*refs):
        dy_refs = refs[:n_p]
        w_ref, x_ref, g_ref, dres_ref, dx_ref, dg_ref = refs[n_p:]

        @pl.when(pl.program_id(0) == 0)
        def _():
            dg_ref[...] = jnp.zeros_like(dg_ref)

        dxn = None
        for k in range(N_CHIPS):
            for p, (pa, pb), (ca, cb) in per_chip[k]:
                part = _dot_nt(dy_refs[p][:, pa:pb], w_ref[k, :, ca:cb])
                dxn = part if dxn is None else dxn + part
        xf = x_ref[...]
        r = lax.rsqrt(jnp.mean(xf * xf, axis=-1, keepdims=True) + EPS)
        xhat = xf * r
        dxhat = dxn * g_ref[...]
        dx_ref[...] = dres_ref[...] + r * (dxhat - xhat * jnp.mean(dxhat * xhat, axis=-1, keepdims=True))
        dg_ref[...] += jnp.sum(dxn * xhat, axis=0, keepdims=True)

    row = lambda w: pl.BlockSpec((tm, w), lambda i: (i, 0))
    return _call(
        body, name=name, grid=(S // tm,), plan=plan,
        in_specs=[row(a.shape[1]) for a in dys]
        + [pl.BlockSpec(w4.shape, lambda i: (0, 0, 0), pipeline_mode=pl.Buffered(1)),
           row(D), pl.BlockSpec((1, D), lambda i: (0, 0)), row(D)],
        out_specs=[row(D), pl.BlockSpec((1, D), lambda i: (0, 0))],
        out_shape=[jax.ShapeDtypeStruct((S, D), F32), jax.ShapeDtypeStruct((1, D), F32)],
        args=(*dys, w4, x, g, dres))


def _wgrad(a, dys, *, name, rows):
    S, K = a.shape
    n4 = sum(p.shape[1] for p in dys) // N_CHIPS
    tm = _row_tile(S, rows)
    ns = S // tm
    specs, local = _piece_specs(dys, n4, tm)
    n_p = len(dys)

    def body(*refs):
        a_ref = refs[0]
        dy_refs = refs[1:1 + n_p]
        o_ref, o16_ref = refs[1 + n_p:]
        n, s = pl.program_id(0), pl.program_id(1)

        @pl.when(s == 0)
        def _():
            o_ref[...] = jnp.zeros_like(o_ref)

        for k in range(N_CHIPS):
            @pl.when(n == k)
            def _():
                av = a_ref[...]
                for p, (pa, pb), (ca, cb) in local[k]:
                    o_ref[:, ca:cb] += _dot_tn(av, dy_refs[p][:, pa:pb])

        @pl.when(s == ns - 1)
        def _():
            o16_ref[...] = o_ref[...].astype(BF16)

    out = pl.BlockSpec((None, K, n4), lambda n, s: (n, 0, 0))
    return _call(
        body, name=name, grid=(N_CHIPS, ns), parallel=(0,),
        in_specs=[pl.BlockSpec((tm, K), lambda n, s: (s, 0))] + specs,
        out_specs=[out, out],
        out_shape=[jax.ShapeDtypeStruct((N_CHIPS, K, n4), F32), jax.ShapeDtypeStruct((N_CHIPS, K, n4), BF16)],
        args=(a, *dys))[0]


def _tiles(x):
    return x.reshape(x.shape[0] // SUBLANES, SUBLANES, x.shape[1])


def _shift_down(xp, s):
    n = xp.shape[0] - SUBLANES
    if s == SUBLANES:
        return xp[:n, :]
    t = _tiles(xp)
    rot = pltpu.roll(t, s, 1)
    sub = lax.broadcasted_iota(jnp.int32, t.shape, 1)[1:]
    return jnp.where(sub >= s, rot[1:], rot[:-1]).reshape(n, xp.shape[1])


def _shift_up(xn, s):
    n = xn.shape[0] - SUBLANES
    if s == SUBLANES:
        return xn[SUBLANES:, :]
    t = _tiles(xn)
    rot = pltpu.roll(t, SUBLANES - s, 1)
    sub = lax.broadcasted_iota(jnp.int32, t.shape, 1)[1:]
    return jnp.where(sub < SUBLANES - s, rot[:-1], rot[1:]).reshape(n, xn.shape[1])


def _pooled(u, halo, first_tile, row0):
    T = u.shape[0]
    halo = jnp.where(first_tile, 0.0, halo)
    pad = jnp.zeros((SUBLANES, u.shape[1]), F32)
    up = jnp.concatenate([pad, halo, u], axis=0)
    t1 = (row0 + lax.broadcasted_iota(jnp.int32, (T, 1), 0) + 1).astype(F32)
    outs = []
    for gi, w in enumerate(POOL_WINDOWS):
        s = up[:, gi * LANES:(gi + 1) * LANES]
        k = 1
        while k < w:
            if k < SUBLANES:
                s = jnp.concatenate([s[:SUBLANES, :], s[SUBLANES:, :] + _shift_down(s, k)], axis=0)
            else:
                s = s[SUBLANES:, :] + _shift_down(s, k)
            k *= 2
        s = s[-T:, :]
        inv = 1.0 / jnp.minimum(t1, float(w))
        outs.append(s * inv - u[:, gi * LANES:(gi + 1) * LANES])
    return outs


def _pool_fwd(z, pool_w, pool_scale, *, name):
    S = z.shape[0]
    P = pool_scale.shape[1]
    T = _row_tile(S, 512)
    hb = T // HALO_POOL

    def body(u_ref, halo_ref, pw_ref, ps_ref, o_ref):
        i = pl.program_id(0)
        pooled = _pooled(u_ref[...], halo_ref[...], i == 0, i * T)
        for gi in range(len(POOL_WINDOWS)):
            mixed = _dot(pooled[gi].astype(BF16), pw_ref[gi].astype(BF16))
            cols = slice(gi * LANES, (gi + 1) * LANES)
            o_ref[:, cols] = (mixed * ps_ref[:, cols]).astype(BF16)

    return _call(
        body, name=name, grid=(S // T,), parallel=(0,),
        in_specs=[pl.BlockSpec((T, P), lambda i: (i, 0)),
                  pl.BlockSpec((HALO_POOL, P), lambda i: (jnp.maximum(i * hb - 1, 0), 0)),
                  pl.BlockSpec(pool_w.shape, lambda i: (0, 0, 0)),
                  pl.BlockSpec((1, P), lambda i: (0, 0))],
        out_specs=[pl.BlockSpec((T, P), lambda i: (i, 0))],
        out_shape=[jax.ShapeDtypeStruct((S, P), BF16)],
        args=(z, z, pool_w, pool_scale))[0][0]


def _pool_bwd(z, dpm, pool_w, pool_scale, *, name):
    S, P = dpm.shape
    T = _row_tile(S, 512)
    hb = T // HALO_POOL
    nt = S // T

    def body(u_ref, halo_ref, d_ref, dnext_ref, pw_ref, ps_ref, du_ref, dpw_ref, dps_ref):
        i = pl.program_id(0)

        @pl.when(i == 0)
        def _():
            dpw_ref[...] = jnp.zeros_like(dpw_ref)
            dps_ref[...] = jnp.zeros_like(dps_ref)

        pooled = _pooled(u_ref[...], halo_ref[...], i == 0, i * T)
        dnext = jnp.where(i == nt - 1, 0.0, dnext_ref[...])
        pad = jnp.zeros((SUBLANES, P), F32)
        dext = jnp.concatenate([d_ref[...], dnext, pad], axis=0)
        t1 = (i * T + lax.broadcasted_iota(jnp.int32, (T + HALO_POOL + SUBLANES, 1), 0) + 1).astype(F32)
        for gi, w in enumerate(POOL_WINDOWS):
            cols = slice(gi * LANES, (gi + 1) * LANES)
            pw = pw_ref[gi].astype(BF16)
            pg = pooled[gi].astype(BF16)
            mixed = _dot(pg, pw)
            dps_ref[:, cols] += jnp.sum(d_ref[:, cols] * mixed, axis=0, keepdims=True)
            dmixed = (dext[:, cols] * ps_ref[:, cols]).astype(BF16)
            dpw_ref[gi] += _dot_tn(pg, dmixed[:T, :])
            dpooled = _dot_nt(dmixed, pw)
            e = dpooled * (1.0 / jnp.minimum(t1, float(w)))
            k = 1
            while k < w:
                if k < SUBLANES:
                    e = jnp.concatenate([e[:-SUBLANES, :] + _shift_up(e, k), e[-SUBLANES:, :]], axis=0)
                else:
                    e = e[:-SUBLANES, :] + _shift_up(e, k)
                k *= 2
            du_ref[:, cols] = (e[:T, :] - dpooled[:T, :]).astype(BF16)

    return _call(
        body, name=name, grid=(nt,),
        in_specs=[pl.BlockSpec((T, P), lambda i: (i, 0)),
                  pl.BlockSpec((HALO_POOL, P), lambda i: (jnp.maximum(i * hb - 1, 0), 0)),
                  pl.BlockSpec((T, P), lambda i: (i, 0)),
                  pl.BlockSpec((HALO_POOL, P), lambda i: (jnp.minimum((i + 1) * hb, S // HALO_POOL - 1), 0)),
                  pl.BlockSpec(pool_w.shape, lambda i: (0, 0, 0)),
                  pl.BlockSpec((1, P), lambda i: (0, 0))],
        out_specs=[pl.BlockSpec((T, P), lambda i: (i, 0)),
                   pl.BlockSpec(pool_w.shape, lambda i: (0, 0, 0)),
                   pl.BlockSpec((1, P), lambda i: (0, 0))],
        out_shape=[jax.ShapeDtypeStruct((S, P), BF16),
                   jax.ShapeDtypeStruct(pool_w.shape, F32),
                   jax.ShapeDtypeStruct((1, P), F32)],
        args=(z, z, dpm, dpm, pool_w, pool_scale))[0]


def _cumsum_rows(x):
    n = x.shape[0]
    row = lax.broadcasted_iota(jnp.int32, x.shape, 0)
    s = 1
    while s < n:
        x = x + jnp.where(row >= s, pltpu.roll(x, s, 0), 0.0)
        s *= 2
    return x


def _rev_cumsum_rows(x):
    n = x.shape[0]
    row = lax.broadcasted_iota(jnp.int32, x.shape, 0)
    s = 1
    while s < n:
        x = x + jnp.where(row < n - s, pltpu.roll(x, n - s, 0), 0.0)
        s *= 2
    return x


def _chunk_prep(zq, zf, lb, b_ref):
    n_sub = CHUNK // SUB
    sq = _sigmoid(zq)
    q = zq * sq
    sf = _sigmoid(zf)
    f = lb + (1.0 - lb) * sf
    k = 1.0 - f
    b = _cumsum_rows(jnp.log(f))
    b_ref[...] = b
    shape = (SUB, b.shape[1])
    ends = [jnp.broadcast_to(b_ref[pl.ds(SUB * j + SUB - 1, 1), :], shape) for j in range(n_sub)]
    mids = [jnp.broadcast_to(b_ref[pl.ds(SUB * j + SUB // 2 - 1, 1), :], shape) for j in range(n_sub)]
    own = [b[SUB * j:SUB * (j + 1), :] for j in range(n_sub)]
    m0 = jnp.concatenate(mids, axis=0)
    e1 = jnp.concatenate(ends, axis=0)
    eq = [jnp.exp(jnp.minimum(b - m0, EXP_CLAMP))]
    for d in range(1, n_sub):
        rd = jnp.concatenate([own[j] if j < d else ends[j - d] for j in range(n_sub)], axis=0)
        eq.append(jnp.exp(b - rd))
    ek0 = jnp.exp(jnp.minimum(m0 - b, EXP_CLAMP))
    ek1 = jnp.exp(e1 - b)
    b_last = b_ref[pl.ds(CHUNK - 1, 1), :]
    return dict(q=q, k=k, f=f, sq=sq, sf=sf, b=b, eq=eq, ek0=ek0, ek1=ek1,
                eb=jnp.exp(b), ekl=jnp.exp(b_last - b), el=jnp.exp(b_last))


def _chunk_masks():
    ti = lax.broadcasted_iota(jnp.int32, (CHUNK, CHUNK), 0)
    si = lax.broadcasted_iota(jnp.int32, (CHUNK, CHUNK), 1)
    shift = SUB.bit_length() - 1
    dsub = jnp.right_shift(ti, shift) - jnp.right_shift(si, shift)
    masks = [(dsub == 0) & (si <= ti)]
    masks += [dsub == d for d in range(1, CHUNK // SUB)]
    return masks


def _chunk_attn(p, masks):
    qd = [(p["q"] * e).astype(BF16) for e in p["eq"]]
    k0 = (p["k"] * p["ek0"]).astype(BF16)
    k1 = (p["k"] * p["ek1"]).astype(BF16)
    a = jnp.where(masks[0], _dot_nt(qd[0], k0), 0.0)
    for d in range(1, len(masks)):
        a = jnp.where(masks[d], _dot_nt(qd[d], k1), a)
    return a, qd, k0, k1


def _hgrn_fwd(z, lb, norm_g, *, name, plan=None):
    S = z.shape[0]
    HW = lb.shape[1]
    NH = HW // LANES
    T = _row_tile(S, 512)
    nc = T // CHUNK

    def body(zq_ref, zf_ref, zi_ref, zo_ref, lb_ref, ng_ref, o_ref, of_ref, st_ref, s_scr, b_scr):
        @pl.when(pl.program_id(0) == 0)
        def _():
            s_scr[...] = jnp.zeros_like(s_scr)

        ng = ng_ref[...]
        masks = _chunk_masks()

        def chunk(c, carry):
            rows = pl.ds(pl.multiple_of(c * CHUNK, CHUNK), CHUNK)
            for h in range(NH):
                cols = slice(h * LANES, (h + 1) * LANES)
                p = _chunk_prep(zq_ref[rows, cols], zf_ref[rows, cols], lb_ref[:, cols], b_scr.at[h])
                v = zi_ref[rows, cols].astype(BF16)
                zo = zo_ref[rows, cols]
                st = s_scr[h]
                st_ref[c, h] = st
                a, _, _, _ = _chunk_attn(p, masks)
                o = _dot(a.astype(BF16), v) + _dot_nt((p["q"] * p["eb"]).astype(BF16), st.astype(BF16))
                s_scr[h] = st * p["el"] + _dot_tn(v, (p["k"] * p["ekl"]).astype(BF16))
                o_ref[rows, cols] = o
                r = lax.rsqrt(jnp.mean(o * o, axis=-1, keepdims=True) + EPS)
                of_ref[rows, cols] = (o * r * ng * (zo * _sigmoid(zo))).astype(BF16)
            return carry

        lax.fori_loop(0, nc, chunk, 0)

    part = lambda k: pl.BlockSpec((T, HW), lambda i, k=k: (i, k))
    return _call(
        body, name=name, grid=(S // T,), plan=plan,
        in_specs=[part(1), part(2), part(3), part(4),
                  pl.BlockSpec((1, HW), lambda i: (0, 0)), pl.BlockSpec((1, LANES), lambda i: (0, 0))],
        out_specs=[pl.BlockSpec((T, HW), lambda i: (i, 0)), pl.BlockSpec((T, HW), lambda i: (i, 0)),
                   pl.BlockSpec((nc, NH, LANES, LANES), lambda i: (i, 0, 0, 0))],
        out_shape=[jax.ShapeDtypeStruct((S, HW), F32), jax.ShapeDtypeStruct((S, HW), BF16),
                   jax.ShapeDtypeStruct((S // CHUNK, NH, LANES, LANES), F32)],
        scratch=[pltpu.VMEM((NH, LANES, LANES), F32), pltpu.VMEM((NH, CHUNK, LANES), F32)],
        args=(z, z, z, z, lb, norm_g))


def _hgrn_bwd(z, lb, norm_g, o_raw, states, dof, *, name, plan=None):
    S = z.shape[0]
    HW = lb.shape[1]
    NH = HW // LANES
    T = _row_tile(S, 512)
    nc = T // CHUNK
    nt = S // T

    def body(zq_ref, zf_ref, zi_ref, zo_ref, lb_ref, ng_ref, o_ref, st_ref, dof_ref,
             dzq_ref, dzf_ref, dzi_ref, dzo_ref, dlb_ref, dng_ref, ds_scr, b_scr):
        @pl.when(pl.program_id(0) == 0)
        def _():
            ds_scr[...] = jnp.zeros_like(ds_scr)
            dlb_ref[...] = jnp.zeros_like(dlb_ref)
            dng_ref[...] = jnp.zeros_like(dng_ref)

        ng = ng_ref[...]
        masks = _chunk_masks()
        last_row = lax.broadcasted_iota(jnp.int32, (CHUNK, 1), 0) == CHUNK - 1

        def chunk(cr, carry):
            c = nc - 1 - cr
            rows = pl.ds(pl.multiple_of(c * CHUNK, CHUNK), CHUNK)
            for h in range(NH):
                cols = slice(h * LANES, (h + 1) * LANES)
                lbv = lb_ref[:, cols]
                zq, zf, zo = zq_ref[rows, cols], zf_ref[rows, cols], zo_ref[rows, cols]
                o = o_ref[rows, cols]
                dof_c = dof_ref[rows, cols]
                st = st_ref[c, h]
                dst = ds_scr[h]

                so = _sigmoid(zo)
                r = lax.rsqrt(jnp.mean(o * o, axis=-1, keepdims=True) + EPS)
                ohat = o * r
                d_on = dof_c * (zo * so)
                dzo_ref[rows, cols] = (dof_c * ohat * ng * _dsilu(zo, so)).astype(BF16)
                dng_ref[:, cols] += jnp.sum(d_on * ohat, axis=0, keepdims=True)
                dohat = d_on * ng
                do = (r * (dohat - ohat * jnp.mean(dohat * ohat, axis=-1, keepdims=True))).astype(BF16)

                p = _chunk_prep(zq, zf, lbv, b_scr.at[h])
                q, k = p["q"], p["k"]
                v = zi_ref[rows, cols].astype(BF16)
                a, qd, k0, k1 = _chunk_attn(p, masks)
                ktl = (k * p["ekl"]).astype(BF16)
                dstb = dst.astype(BF16)

                da = _dot_nt(do, v)
                dzi_ref[rows, cols] = (_dot_tn(a.astype(BF16), do) + _dot_nt(ktl, dstb)).astype(BF16)

                da0 = jnp.where(masks[0], da, 0.0).astype(BF16)
                rq = _dot(da0, k0)
                rk0 = _dot_tn(da0, qd[0])
                dq = rq * p["eq"][0]
                db = qd[0].astype(F32) * rq - k0.astype(F32) * rk0
                rk1 = jnp.zeros_like(rk0)
                for d in range(1, len(masks)):
                    dad = jnp.where(masks[d], da, 0.0).astype(BF16)
                    rq = _dot(dad, k1)
                    dq = dq + rq * p["eq"][d]
                    db = db + qd[d].astype(F32) * rq
                    rk1 = rk1 + _dot_tn(dad, qd[d])
                dk = rk0 * p["ek0"] + rk1 * p["ek1"]
                db = db - k1.astype(F32) * rk1
                qe = (q * p["eb"]).astype(BF16)
                rq = _dot(do, st.astype(BF16))
                dq = dq + rq * p["eb"]
                db = db + qe.astype(F32) * rq
                rk = _dot(v, dstb)
                dk = dk + rk * p["ekl"]
                db = db - ktl.astype(F32) * rk

                st_new = st * p["el"] + _dot_tn(v, ktl)
                db = db + jnp.where(last_row, jnp.sum(dstb.astype(F32) * st_new, axis=0, keepdims=True), 0.0)
                dg = _rev_cumsum_rows(db)
                ds_scr[h] = dst * p["el"] + _dot_tn(do, qe)

                dzq_ref[rows, cols] = (dq * _dsilu(zq, p["sq"])).astype(BF16)
                df = dg / p["f"] - dk
                sf = p["sf"]
                dzf_ref[rows, cols] = (df * (1.0 - lbv) * sf * (1.0 - sf)).astype(BF16)
                dlb_ref[:, cols] += jnp.sum(df * (1.0 - sf), axis=0, keepdims=True)
            return carry

        lax.fori_loop(0, nc, chunk, 0)

    rev = lambda i: nt - 1 - i
    part = lambda k: pl.BlockSpec((T, HW), lambda i, k=k: (rev(i), k))
    blk = pl.BlockSpec((T, HW), lambda i: (rev(i), 0))
    vec = pl.BlockSpec((1, HW), lambda i: (0, 0))
    return _call(
        body, name=name, grid=(nt,), plan=plan,
        in_specs=[part(1), part(2), part(3), part(4), vec, pl.BlockSpec((1, LANES), lambda i: (0, 0)),
                  blk, pl.BlockSpec((nc, NH, LANES, LANES), lambda i: (rev(i), 0, 0, 0)), blk],
        out_specs=[blk, blk, blk, blk, vec, vec],
        out_shape=[jax.ShapeDtypeStruct((S, HW), BF16)] * 4 + [jax.ShapeDtypeStruct((1, HW), F32)] * 2,
        scratch=[pltpu.VMEM((NH, LANES, LANES), F32), pltpu.VMEM((NH, CHUNK, LANES), F32)],
        args=(z, z, z, z, lb, norm_g, o_raw, states, dof))


def _gate_specs(T, D):
    half = D // 2
    first = (5 * half) // half
    return [pl.BlockSpec((T, half), lambda i, k=k: (i, first + k)) for k in range(4)]


def _gates(zg_refs, bg_ref, D):
    half = D // 2
    za = jnp.concatenate([zg_refs[0][...], zg_refs[1][...]], axis=1) + bg_ref[:, :D]
    zb = jnp.concatenate([zg_refs[2][...], zg_refs[3][...]], axis=1) + bg_ref[:, D:]
    return _sigmoid(za), _sigmoid(zb)


def _mix_fwd(x, pm, of, z, b_gate, w_pa4, w_pb4, w_o4, *, name, plan=None):
    S, D = x.shape
    P = pm.shape[1]
    T = _row_tile(S, 512)

    def body(x_ref, pm_ref, of_ref, g0, g1, g2, g3, bg_ref, wpa_ref, wpb_ref, wo_ref, xo_ref, ya_ref, yb_ref):
        pmv, ofv = pm_ref[...], of_ref[...]
        ya = jnp.concatenate([_dot(pmv, wpa_ref[k]) for k in range(N_CHIPS)], axis=1)
        yb = jnp.concatenate([_dot(ofv, wpb_ref[k]) for k in range(N_CHIPS)], axis=1)
        ga, gb = _gates((g0, g1, g2, g3), bg_ref, D)
        merged = (ga * ya + gb * yb).astype(BF16)
        xo_ref[...] = x_ref[...] + _dot(merged, wo_ref[...].reshape(D, D))
        ya_ref[...] = ya.astype(BF16)
        yb_ref[...] = yb.astype(BF16)

    row = lambda w: pl.BlockSpec((T, w), lambda i: (i, 0))
    full = lambda a: pl.BlockSpec(a.shape, lambda i: (0,) * a.ndim)
    return _call(
        body, name=name, grid=(S // T,), parallel=(0,), plan=plan,
        in_specs=[row(D), row(P), row(P)] + _gate_specs(T, D) + [full(b_gate), full(w_pa4), full(w_pb4), full(w_o4)],
        out_specs=[row(D), row(D), row(D)],
        out_shape=[jax.ShapeDtypeStruct((S, D), F32), jax.ShapeDtypeStruct((S, D), BF16),
                   jax.ShapeDtypeStruct((S, D), BF16)],
        args=(x, pm, of, z, z, z, z, b_gate, w_pa4, w_pb4, w_o4))


def _mix_bwd(dxm, ya, yb, z, b_gate, pm, of, w_pa4, w_pb4, w_o4, *, name):
    S, D = dxm.shape
    P = pm.shape[1]
    q4 = D // N_CHIPS
    T = _row_tile(S, 256)
    nt = S // T

    def body(dx_ref, ya_ref, yb_ref, g0, g1, g2, g3, bg_ref, pm_ref, of_ref, wpa_ref, wpb_ref, wo_ref,
             dzg_ref, dpm_ref, dof_ref, dwo_ref, dwpa_ref, dwpb_ref, dbg_ref, dwo16_ref, dwpa16_ref, dwpb16_ref):
        i = pl.program_id(0)

        @pl.when(i == 0)
        def _():
            dwo_ref[...] = jnp.zeros_like(dwo_ref)
            dwpa_ref[...] = jnp.zeros_like(dwpa_ref)
            dwpb_ref[...] = jnp.zeros_like(dwpb_ref)
            dbg_ref[...] = jnp.zeros_like(dbg_ref)

        dxb = dx_ref[...].astype(BF16)
        ya = ya_ref[...].astype(F32)
        yb = yb_ref[...].astype(F32)
        ga, gb = _gates((g0, g1, g2, g3), bg_ref, D)
        merged = (ga * ya + gb * yb).astype(BF16)
        dwo_ref[...] += _dot_tn(merged, dxb).reshape(N_CHIPS, q4, D)
        dm = _dot_nt(dxb, wo_ref[...].reshape(D, D))
        dza = dm * ya * ga * (1.0 - ga)
        dzb = dm * yb * gb * (1.0 - gb)
        dzg_ref[:, :D] = dza.astype(BF16)
        dzg_ref[:, D:] = dzb.astype(BF16)
        dbg_ref[:, :D] += jnp.sum(dza, axis=0, keepdims=True)
        dbg_ref[:, D:] += jnp.sum(dzb, axis=0, keepdims=True)
        dya = (dm * ga).astype(BF16)
        dyb = (dm * gb).astype(BF16)
        pmv, ofv = pm_ref[...], of_ref[...]
        dpm = jnp.zeros((T, P), F32)
        dof = jnp.zeros((T, P), F32)
        for k in range(N_CHIPS):
            cols = slice(k * q4, (k + 1) * q4)
            dwpa_ref[k] += _dot_tn(pmv, dya[:, cols])
            dwpb_ref[k] += _dot_tn(ofv, dyb[:, cols])
            dpm = dpm + _dot_nt(dya[:, cols], wpa_ref[k])
            dof = dof + _dot_nt(dyb[:, cols], wpb_ref[k])
        dpm_ref[...] = dpm
        dof_ref[...] = dof

        @pl.when(i == nt - 1)
        def _():
            dwo16_ref[...] = dwo_ref[...].astype(BF16)
            dwpa16_ref[...] = dwpa_ref[...].astype(BF16)
            dwpb16_ref[...] = dwpb_ref[...].astype(BF16)

    row = lambda w: pl.BlockSpec((T, w), lambda i: (i, 0))
    full = lambda a: pl.BlockSpec(a.shape, lambda i: (0,) * a.ndim)
    like = lambda a, dt: jax.ShapeDtypeStruct(a.shape, dt)
    return _call(
        body, name=name, grid=(nt,),
        in_specs=[row(D), row(D), row(D)] + _gate_specs(T, D) + [full(b_gate), row(P), row(P),
                                                                  full(w_pa4), full(w_pb4), full(w_o4)],
        out_specs=[row(2 * D), row(P), row(P), full(w_o4), full(w_pa4), full(w_pb4), full(b_gate),
                   full(w_o4), full(w_pa4), full(w_pb4)],
        out_shape=[jax.ShapeDtypeStruct((S, 2 * D), BF16), jax.ShapeDtypeStruct((S, P), F32),
                   jax.ShapeDtypeStruct((S, P), F32), like(w_o4, F32), like(w_pa4, F32), like(w_pb4, F32),
                   like(b_gate, F32), like(w_o4, BF16), like(w_pa4, BF16), like(w_pb4, BF16)],
        args=(dxm, ya, yb, z, z, z, z, b_gate, pm, of, w_pa4, w_pb4, w_o4))[0]


def _conv3(h, halo, first_tile, cw, cb):
    h = h.astype(F32)
    halo = jnp.where(first_tile, 0.0, halo.astype(F32)[-SUBLANES:, :])
    hp = jnp.concatenate([halo, h], axis=0)
    h1 = _shift_down(hp, 1)
    h2 = _shift_down(hp, 2)
    return cw[0:1, :] * h2 + cw[1:2, :] * h1 + cw[2:3, :] * h + cb, h1, h2


def _ffn_down_fwd(h, conv_w, conv_b, w_down4, x, *, name):
    S, F2 = h.shape
    _, f4, D = w_down4.shape
    T = _row_tile(S, 512)
    nf = 2
    tf = 2 * f4
    hb = T // HALO_CONV

    def body(hv_ref, hg_ref, pv_ref, pg_ref, cwv_ref, cwg_ref, cbv_ref, cbg_ref, wd_ref, x_ref, o_ref, acc_ref):
        i, f = pl.program_id(0), pl.program_id(1)

        @pl.when(f == 0)
        def _():
            acc_ref[...] = jnp.zeros_like(acc_ref)

        val, _, _ = _conv3(hv_ref[...], pv_ref[...], i == 0, cwv_ref[...], cbv_ref[...])
        gate, _, _ = _conv3(hg_ref[...], pg_ref[...], i == 0, cwg_ref[...], cbg_ref[...])
        a = (gate * _sigmoid(gate) * val).astype(BF16)
        acc_ref[...] += _dot(a, wd_ref[...].reshape(tf, D))

        @pl.when(f == nf - 1)
        def _():
            o_ref[...] = x_ref[...] + acc_ref[...]

    prev = lambda i: jnp.maximum(i * hb - 1, 0)
    return _call(
        body, name=name, grid=(S // T, nf), parallel=(0,),
        in_specs=[pl.BlockSpec((T, tf), lambda i, f: (i, f)),
                  pl.BlockSpec((T, tf), lambda i, f: (i, nf + f)),
                  pl.BlockSpec((HALO_CONV, tf), lambda i, f: (prev(i), f)),
                  pl.BlockSpec((HALO_CONV, tf), lambda i, f: (prev(i), nf + f)),
                  pl.BlockSpec((3, tf), lambda i, f: (0, f)),
                  pl.BlockSpec((3, tf), lambda i, f: (0, nf + f)),
                  pl.BlockSpec((1, tf), lambda i, f: (0, f)),
                  pl.BlockSpec((1, tf), lambda i, f: (0, nf + f)),
                  pl.BlockSpec((2, f4, D), lambda i, f: (f, 0, 0)),
                  pl.BlockSpec((T, D), lambda i, f: (i, 0))],
        out_specs=[pl.BlockSpec((T, D), lambda i, f: (i, 0))],
        out_shape=[jax.ShapeDtypeStruct((S, D), F32)],
        scratch=[pltpu.VMEM((T, D), F32)],
        args=(h, h, h, h, conv_w, conv_w, conv_b, conv_b, w_down4, x))[0][0]


def _ffn_down_bwd(dxo, h, conv_w, conv_b, w_down4, *, name, plan=None):
    S, F2 = h.shape
    _, f4, D = w_down4.shape
    F = N_CHIPS * f4
    T = _row_tile(S, 256)
    nf = 2
    tf = 2 * f4
    hb = T // HALO_CONV
    nt = S // T

    def body(dx_ref, hv_ref, hg_ref, pv_ref, pg_ref, cwv_ref, cwg_ref, cbv_ref, cbg_ref, wd_ref,
             dhv_ref, dhg_ref, dwd_ref, dwd16_ref, dcwv_ref, dcwg_ref, dcbv_ref, dcbg_ref, cv_scr, cg_scr):
        i = pl.program_id(1)
        first_tile = i == nt - 1

        @pl.when(i == 0)
        def _():
            cv_scr[...] = jnp.zeros_like(cv_scr)
            cg_scr[...] = jnp.zeros_like(cg_scr)
            dwd_ref[...] = jnp.zeros_like(dwd_ref)
            dcwv_ref[...] = jnp.zeros_like(dcwv_ref)
            dcwg_ref[...] = jnp.zeros_like(dcwg_ref)
            dcbv_ref[...] = jnp.zeros_like(dcbv_ref)
            dcbg_ref[...] = jnp.zeros_like(dcbg_ref)

        dxb = dx_ref[...].astype(BF16)
        hv, hg = hv_ref[...].astype(F32), hg_ref[...].astype(F32)
        val, _, _ = _conv3(hv, pv_ref[...], first_tile, cwv_ref[...], cbv_ref[...])
        gate, _, _ = _conv3(hg, pg_ref[...], first_tile, cwg_ref[...], cbg_ref[...])
        sg = _sigmoid(gate)
        sil = gate * sg
        dwd_ref[...] += _dot_tn((sil * val).astype(BF16), dxb).reshape(2, f4, D)
        da = _dot_nt(dxb, wd_ref[...].reshape(tf, D))
        dval = da * sil
        dgate = da * val * _dsilu(gate, sg)

        def conv_bwd(dhc, h0, cw, c_scr, dh_ref, dcw_ref, dcb_ref):
            ext = jnp.concatenate([dhc, c_scr[...]], axis=0)
            n1 = _shift_up(ext, 1)
            n2 = _shift_up(ext, 2)
            dh_ref[...] = (cw[2:3, :] * dhc + cw[1:2, :] * n1 + cw[0:1, :] * n2).astype(BF16)
            c_scr[...] = dhc[:SUBLANES, :]
            dcw_ref[0:1, :] += jnp.sum(n2 * h0, axis=0, keepdims=True)
            dcw_ref[1:2, :] += jnp.sum(n1 * h0, axis=0, keepdims=True)
            dcw_ref[2:3, :] += jnp.sum(dhc * h0, axis=0, keepdims=True)
            dcb_ref[...] += jnp.sum(dhc, axis=0, keepdims=True)

        conv_bwd(dval, hv, cwv_ref[...], cv_scr, dhv_ref, dcwv_ref, dcbv_ref)
        conv_bwd(dgate, hg, cwg_ref[...], cg_scr, dhg_ref, dcwg_ref, dcbg_ref)

        @pl.when(i == nt - 1)
        def _():
            dwd16_ref[...] = dwd_ref[...].astype(BF16)

    rev = lambda i: nt - 1 - i
    prev = lambda i: jnp.maximum(rev(i) * hb - 1, 0)
    wd_spec = pl.BlockSpec((2, f4, D), lambda f, i: (f, 0, 0))
    return _call(
        body, name=name, grid=(nf, nt), plan=plan,
        in_specs=[pl.BlockSpec((T, D), lambda f, i: (rev(i), 0)),
                  pl.BlockSpec((T, tf), lambda f, i: (rev(i), f)),
                  pl.BlockSpec((T, tf), lambda f, i: (rev(i), nf + f)),
                  pl.BlockSpec((HALO_CONV, tf), lambda f, i: (prev(i), f)),
                  pl.BlockSpec((HALO_CONV, tf), lambda f, i: (prev(i), nf + f)),
                  pl.BlockSpec((3, tf), lambda f, i: (0, f)),
                  pl.BlockSpec((3, tf), lambda f, i: (0, nf + f)),
                  pl.BlockSpec((1, tf), lambda f, i: (0, f)),
                  pl.BlockSpec((1, tf), lambda f, i: (0, nf + f)),
                  wd_spec],
        out_specs=[pl.BlockSpec((T, tf), lambda f, i: (rev(i), f)),
                   pl.BlockSpec((T, tf), lambda f, i: (rev(i), f)),
                   wd_spec, wd_spec,
                   pl.BlockSpec((3, tf), lambda f, i: (0, f)),
                   pl.BlockSpec((3, tf), lambda f, i: (0, f)),
                   pl.BlockSpec((1, tf), lambda f, i: (0, f)),
                   pl.BlockSpec((1, tf), lambda f, i: (0, f))],
        out_shape=[jax.ShapeDtypeStruct((S, F), BF16), jax.ShapeDtypeStruct((S, F), BF16),
                   jax.ShapeDtypeStruct((N_CHIPS, f4, D), F32), jax.ShapeDtypeStruct((N_CHIPS, f4, D), BF16),
                   jax.ShapeDtypeStruct((3, F), F32), jax.ShapeDtypeStruct((3, F), F32),
                   jax.ShapeDtypeStruct((1, F), F32), jax.ShapeDtypeStruct((1, F), F32)],
        scratch=[pltpu.VMEM((SUBLANES, tf), F32), pltpu.VMEM((SUBLANES, tf), F32)],
        args=(dxo, h, h, h, h, conv_w, conv_w, conv_b, conv_b, w_down4))


def _final_loss(x, g, target, *, name):
    S, D = x.shape
    T = _row_tile(S, 512)

    def body(x_ref, g_ref, t_ref, loss_ref, dx_ref, dg_ref):
        @pl.when(pl.program_id(0) == 0)
        def _():
            loss_ref[...] = jnp.zeros_like(loss_ref)
            dg_ref[...] = jnp.zeros_like(dg_ref)

        xf = x_ref[...]
        r = lax.rsqrt(jnp.mean(xf * xf, axis=-1, keepdims=True) + EPS)
        xhat = xf * r
        err = xhat * g_ref[...] - t_ref[...]
        loss_ref[...] += jnp.sum(err * err, axis=0, keepdims=True) * (0.5 / D)
        dy = err * (1.0 / D)
        dxhat = dy * g_ref[...]
        dx_ref[...] = r * (dxhat - xhat * jnp.mean(dxhat * xhat, axis=-1, keepdims=True))
        dg_ref[...] += jnp.sum(dy * xhat, axis=0, keepdims=True)

    return _call(
        body, name=name, grid=(S // T,),
        in_specs=[pl.BlockSpec((T, D), lambda i: (i, 0)), pl.BlockSpec((1, D), lambda i: (0, 0)),
                  pl.BlockSpec((T, D), lambda i: (i, 0))],
        out_specs=[pl.BlockSpec((1, D), lambda i: (0, 0)), pl.BlockSpec((T, D), lambda i: (i, 0)),
                   pl.BlockSpec((1, D), lambda i: (0, 0))],
        out_shape=[jax.ShapeDtypeStruct((1, D), F32), jax.ShapeDtypeStruct((S, D), F32),
                   jax.ShapeDtypeStruct((1, D), F32)],
        args=(x, g, target))[0]


BIG = ("w_in", "w_pa", "w_pb", "w_o", "w_up", "w_down")
SMALL = ("norm1_g", "b_gate", "pool_w", "pool_scale", "lb_logits", "hgrn_norm_g", "norm2_g", "conv_b", "final_g")
WEIGHTS = ("norm1_g", "w_in", "b_gate", "pool_w", "pool_scale", "lb_logits", "hgrn_norm_g", "w_pa", "w_pb", "w_o",
           "norm2_g", "w_up", "conv_w", "conv_b", "w_down", "final_g")


def _lower_bounds(lb_logits):
    soft = jax.nn.softmax(lb_logits.astype(F32), axis=0)
    cum = jnp.cumsum(soft, axis=0)
    return cum - cum[0:1]


def _step(x, target, sm, wts, shards=None):
    L = sm["norm1_g"].shape[0]
    wts = dict(wts)
    dist = shards is not None
    lbs, lb_vjp = jax.vjp(_lower_bounds, sm["lb_logits"])
    row = lambda a: a.reshape(1, -1)
    conv_w = sm.get("conv_w")

    def gather(names_layers, with_conv=False):
        items = [(shards[n], "rows", l) for n, l in names_layers]
        if with_conv:
            items.append((shards["conv_w"], "layer", None))
        return _GatherPlan(items)

    def landed(names_layers, outs):
        for key, arr in zip(names_layers, outs):
            wts[key] = arr

    saved = []
    for l in range(L):
        g1 = row(sm["norm1_g"][l])
        first = dist and l == 0
        keys = [(n, 0) for n in ("w_pa", "w_pb", "w_o")]
        plan = gather(keys) if first else None
        (xn1, z), got = _norm_matmul(x, g1, wts[("w_in", l)], name=f"in_proj_{l}", plan=plan)
        landed(keys, got)
        pm = _pool_fwd(z, sm["pool_w"][l], row(sm["pool_scale"][l]), name=f"pool_fwd_{l}")
        keys = [(n, 0) for n in ("w_up", "w_down")]
        plan = gather(keys, with_conv=True) if first else None
        (o_raw, of, states), got = _hgrn_fwd(z, row(lbs[l]), row(sm["hgrn_norm_g"][l]), name=f"hgrn_fwd_{l}",
                                             plan=plan)
        landed(keys, got)
        if first:
            full = got[-1]
            conv_w = jnp.concatenate([full[:, k] for k in range(N_CHIPS)], axis=2)
        ahead = dist and l + 1 < L
        keys = [(n, l + 1) for n in ("w_in", "w_pa", "w_pb", "w_o")]
        plan = gather(keys) if ahead else None
        (x_mid, ya, yb), got = _mix_fwd(x, pm, of, z, row(sm["b_gate"][l]), wts[("w_pa", l)], wts[("w_pb", l)],
                                        wts[("w_o", l)], name=f"mix_fwd_{l}", plan=plan)
        landed(keys, got)
        keys = [(n, l + 1) for n in ("w_up", "w_down")]
        plan = gather(keys) if ahead else None
        (xn2, h), got = _norm_matmul(x_mid, row(sm["norm2_g"][l]), wts[("w_up", l)], name=f"up_{l}",
                                     out_dtype=BF16, plan=plan)
        landed(keys, got)
        x_out = _ffn_down_fwd(h, conv_w[l], row(sm["conv_b"][l]), wts[("w_down", l)], x_mid, name=f"down_fwd_{l}")
        saved.append(dict(x=x, xn1=xn1, z=z, pm=pm, o_raw=o_raw, of=of, states=states,
                          x_mid=x_mid, ya=ya, yb=yb, xn2=xn2, h=h))
        x = x_out

    loss_cols, dx, d_final_g = _final_loss(x, row(sm["final_g"]), target, name="final_loss")

    small = {k: [None] * L for k in ("norm1_g", "b_gate", "pool_w", "pool_scale", "hgrn_norm_g", "norm2_g",
                                     "conv_w", "conv_b")}
    big32, big16, recv = {}, {}, {}
    dlbs = [None] * L
    pending = []

    def scatter():
        keys = list(pending)
        del pending[:]
        return keys, _ScatterPlan([big16[k] for k in keys])

    def sent(keys, outs):
        for key, arr in zip(keys, outs):
            recv[key] = arr

    for l in reversed(range(L)):
        s = saved[l]
        keys, plan = scatter() if dist and pending else (None, None)
        (dhv, dhg, d_wd, d_wd16, dcwv, dcwg, dcbv, dcbg), got = _ffn_down_bwd(
            dx, s["h"], conv_w[l], row(sm["conv_b"][l]), wts[("w_down", l)], name=f"down_bwd_{l}", plan=plan)
        if plan is not None:
            sent(keys, got)
        big32[("w_down", l)], big16[("w_down", l)] = d_wd, d_wd16
        small["conv_w"][l] = jnp.concatenate([dcwv, dcwg], axis=1)
        small["conv_b"][l] = jnp.concatenate([dcbv, dcbg], axis=1)[0]
        big32[("w_up", l)], big16[("w_up", l)] = _wgrad(s["xn2"], [dhv, dhg], name=f"up_wgrad_{l}", rows=2048)
        (dxm, dg2), _ = _dgrad_norm([dhv, dhg], wts[("w_up", l)], s["x_mid"], row(sm["norm2_g"][l]), dx,
                                    name=f"up_dgrad_{l}")
        small["norm2_g"][l] = dg2[0]

        dzg, dpm, dof, d_wo, d_wpa, d_wpb, dbg, d_wo16, d_wpa16, d_wpb16 = _mix_bwd(
            dxm, s["ya"], s["yb"], s["z"], row(sm["b_gate"][l]), s["pm"], s["of"],
            wts[("w_pa", l)], wts[("w_pb", l)], wts[("w_o", l)], name=f"mix_bwd_{l}")
        big32[("w_o", l)], big16[("w_o", l)] = d_wo, d_wo16
        big32[("w_pa", l)], big16[("w_pa", l)] = d_wpa, d_wpa16
        big32[("w_pb", l)], big16[("w_pb", l)] = d_wpb, d_wpb16
        small["b_gate"][l] = dbg[0]
        pending.extend([("w_down", l), ("w_up", l), ("w_o", l), ("w_pa", l), ("w_pb", l)])

        du, dpw, dps = _pool_bwd(s["z"], dpm, sm["pool_w"][l], row(sm["pool_scale"][l]), name=f"pool_bwd_{l}")
        small["pool_w"][l], small["pool_scale"][l] = dpw, dps[0]

        keys, plan = scatter() if dist and l == 0 else (None, None)
        (dzq, dzf, dzi, dzo, dlb, dng), got = _hgrn_bwd(s["z"], row(lbs[l]), row(sm["hgrn_norm_g"][l]), s["o_raw"],
                                                      s["states"], dof, name=f"hgrn_bwd_{l}", plan=plan)
        if plan is not None:
            sent(keys, got)
        dlbs[l] = dlb[0]
        small["hgrn_norm_g"][l] = jnp.sum(dng.reshape(-1, LANES), axis=0)

        dz = [du, dzq, dzf, dzi, dzo, dzg]
        big32[("w_in", l)], big16[("w_in", l)] = _wgrad(s["xn1"], dz, name=f"in_wgrad_{l}", rows=1024)
        pending.append(("w_in", l))
        keys, plan = scatter() if dist and l == 0 else (None, None)
        (dx, dg1), got = _dgrad_norm(dz, wts[("w_in", l)], s["x"], row(sm["norm1_g"][l]), dxm,
                                     name=f"in_dgrad_{l}", plan=plan)
        if plan is not None:
            sent(keys, got)
        small["norm1_g"][l] = dg1[0]

    out = {k: jnp.stack(v) for k, v in small.items()}
    out["lb_logits"] = lb_vjp(jnp.stack(dlbs))[0]
    out["final_g"] = d_final_g[0]
    return loss_cols, dx, out, big32, recv


def _elementwise_rows(R, n, n_arrays):
    if 2 * n_arrays * R * n * 4 <= VMEM_LIMIT // 4 or R % 8:
        return R
    want = 8
    while want * 2 * n * 4 <= 1024 * 1024:
        want *= 2
    return _row_tile(R, want)


def _sum_layers(own, got, chip, *, name):
    L = len(own)
    _, r, n = own[0].shape
    T = _elementwise_rows(r, n, 6)
    nt = r // T

    def body(chip_ref, *refs):
        o_ref = refs[-1]
        l = pl.program_id(0)
        for k in range(L):
            @pl.when(l == k)
            def _():
                own_ref, got_ref = refs[2 * k], refs[2 * k + 1]
                acc = own_ref[...]
                for j in range(3):
                    acc = acc + got_ref[j].astype(F32)
                o_ref[...] = acc

    in_specs = []
    for k in range(L):
        hold = 0 if k else nt - 1
        in_specs.append(pl.BlockSpec((None, T, n), lambda l, i, c, k=k, hold=hold: (c[0], jnp.where(l == k, i, hold), 0)))
        in_specs.append(pl.BlockSpec((3, T, n), lambda l, i, c, k=k, hold=hold: (0, jnp.where(l == k, i, hold), 0)))
    grid_spec = pltpu.PrefetchScalarGridSpec(
        num_scalar_prefetch=1, grid=(L, nt), in_specs=in_specs,
        out_specs=pl.BlockSpec((None, T, n), lambda l, i, c: (l, i, 0)))
    args = [a for pair in zip(own, got) for a in pair]
    return pl.pallas_call(
        body, name=name, grid_spec=grid_spec, out_shape=jax.ShapeDtypeStruct((L, r, n), F32),
        compiler_params=pltpu.CompilerParams(dimension_semantics=("arbitrary", "arbitrary"),
                                             vmem_limit_bytes=VMEM_LIMIT),
    )(chip, *args)


def _sum_stack(first, rest, *, name):
    R, n = first.shape
    K = rest.shape[0]
    T = _elementwise_rows(R, n, K + 2)

    def body(a_ref, r_ref, o_ref):
        acc = a_ref[...]
        for j in range(K):
            acc = acc + r_ref[j].astype(F32)
        o_ref[...] = acc

    return _call(
        body, name=name, grid=(R // T,), parallel=(0,),
        in_specs=[pl.BlockSpec((T, n), lambda i: (i, 0)), pl.BlockSpec((K, T, n), lambda i: (0, i, 0))],
        out_specs=[pl.BlockSpec((T, n), lambda i: (i, 0))],
        out_shape=[jax.ShapeDtypeStruct((R, n), F32)],
        args=(first, rest))[0][0]


def _adamw(w, m, v, g_parts, *, name):
    R, n = w.shape
    n_g = len(g_parts)
    T = _elementwise_rows(R, n, 7 + n_g)

    def body(*refs):
        w_ref, m_ref, v_ref = refs[:3]
        g_refs = refs[3:3 + n_g]
        go_ref, d_ref, mo_ref, vo_ref = refs[3 + n_g:]
        g_ = g_refs[0][...]
        for r in g_refs[1:]:
            g_ = g_ + r[...]
        m_ = ADAM_B1 * m_ref[...] + (1.0 - ADAM_B1) * g_
        v_ = ADAM_B2 * v_ref[...] + (1.0 - ADAM_B2) * (g_ * g_)
        m_hat = m_ / (1.0 - ADAM_B1 ** ADAM_STEP)
        v_hat = v_ / (1.0 - ADAM_B2 ** ADAM_STEP)
        go_ref[...] = g_
        d_ref[...] = -ADAM_LR * (m_hat / (jnp.sqrt(v_hat) + ADAM_EPS) + ADAM_WD * w_ref[...])
        mo_ref[...] = m_
        vo_ref[...] = v_

    blk = pl.BlockSpec((T, n), lambda i: (i, 0))
    return _call(
        body, name=name, grid=(R // T,), parallel=(0,),
        in_specs=[blk] * (3 + n_g), out_specs=[blk] * 4,
        out_shape=[jax.ShapeDtypeStruct((R, n), F32)] * 4,
        args=(w, m, v, *g_parts))[0]


PACK_ALIGN = 8 * LANES


def _pack(pieces):
    flat = []
    for a in pieces:
        a = a.reshape(-1)
        pad = (-a.shape[0]) % PACK_ALIGN
        flat.append(jnp.pad(a, (0, pad)) if pad else a)
    return jnp.concatenate(flat).reshape(-1, LANES)


def _unpack(buf, shapes):
    flat = buf.reshape(-1)
    out, off = [], 0
    for shp in shapes:
        size = 1
        for s in shp:
            size *= s
        out.append(flat[off:off + size].reshape(shp))
        off += size + (-size) % PACK_ALIGN
    return out


def kernel(x, norm1_g, w_in, b_gate, pool_w, pool_scale, lb_logits, hgrn_norm_g, w_pa, w_pb, w_o, norm2_g, w_up, conv_w, conv_b, w_down, final_g, loss_target, m_norm1_g, m_w_in, m_b_gate, m_pool_w, m_pool_scale, m_lb_logits, m_hgrn_norm_g, m_w_pa, m_w_pb, m_w_o, m_norm2_g, m_w_up, m_conv_w, m_conv_b, m_w_down, m_final_g, v_norm1_g, v_w_in, v_b_gate, v_pool_w, v_pool_scale, v_lb_logits, v_hgrn_norm_g, v_w_pa, v_w_pb, v_w_o, v_norm2_g, v_w_up, v_conv_w, v_conv_b, v_w_down, v_final_g):
    env = dict(locals())
    w = {n: env[n] for n in WEIGHTS}
    m = {n: env["m_" + n] for n in WEIGHTS}
    v = {n: env["v_" + n] for n in WEIGHTS}
    my_chip = 2 * lax.axis_index("x") + lax.axis_index("y")
    L = w_in.shape[0]

    shards = {n: w[n].astype(BF16) for n in BIG}
    shards["conv_w"] = w["conv_w"]
    w_in0 = _run_plan(_GatherPlan([(shards["w_in"], "rows", 0)]), name="gather_w_in0")[0]
    sm = {n: w[n] for n in SMALL}
    loss_cols, grad_x, g_small, big32, recv = _step(x[0], loss_target[0], sm, {("w_in", 0): w_in0}, shards)

    chip = my_chip.reshape(1).astype(jnp.int32)
    sums = [_sum_layers([big32[(n, l)] for l in range(L)], [recv[(n, l)] for l in range(L)], chip,
                        name="chip_sum_" + n) for n in BIG]
    theirs = _run_plan(_SiblingPlan(sums), name="grad_sibling")
    g, delta, new_m, new_v = {}, {}, {}, {}
    for n, mine, other in zip(BIG, sums, theirs):
        shp = w[n].shape
        two_d = lambda a: a.reshape(-1, shp[-1])
        outs = _adamw(two_d(w[n]), two_d(m[n]), two_d(v[n]), [two_d(mine), two_d(other)], name="adamw_" + n)
        g[n], delta[n], new_m[n], new_v[n] = [a.reshape(shp) for a in outs]

    small_names = list(SMALL)
    small_pieces = [g_small[n] for n in small_names] + [g_small["conv_w"], loss_cols]
    small_shapes = [a.shape for a in small_pieces]
    packed = _pack(small_pieces)
    Rs = packed.shape[0]
    everyone = _run_plan(_EveryonePlan(packed), name="gather_small")[0].reshape(8, Rs, LANES)
    summed = _unpack(_sum_stack(everyone[0], everyone[1:], name="small_sum"), small_shapes)
    loss = jnp.sum(summed[-1])
    cshard = w["conv_w"].shape[2]
    gs = dict(zip(small_names, summed[:len(small_names)]))
    g_cw = lax.dynamic_slice_in_dim(summed[-2], my_chip * cshard, cshard, axis=2)

    sm_out = _adamw(_pack([w[n] for n in small_names]), _pack([m[n] for n in small_names]),
                    _pack([v[n] for n in small_names]), [_pack([gs[n] for n in small_names])], name="adamw_small")
    shapes = [w[n].shape for n in small_names]
    for n, g_, d_, m_, v_ in zip(small_names, *[_unpack(a, shapes) for a in sm_out]):
        g[n], delta[n], new_m[n], new_v[n] = g_, d_, m_, v_
    shp = w["conv_w"].shape
    two_d = lambda a: a.reshape(-1, shp[-1])
    outs = _adamw(two_d(w["conv_w"]), two_d(m["conv_w"]), two_d(v["conv_w"]), [two_d(g_cw)], name="adamw_conv_w")
    g["conv_w"], delta["conv_w"], new_m["conv_w"], new_v["conv_w"] = [a.reshape(shp) for a in outs]

    return (loss, grad_x[None], *[g[n] for n in WEIGHTS], *[delta[n] for n in WEIGHTS],
            *[new_m[n] for n in WEIGHTS], *[new_v[n] for n in WEIGHTS])
```

```python
import jax
import jax.numpy as jnp
from jax import lax
from jax.experimental import pallas as pl
from jax.experimental.pallas import tpu as pltpu

F32 = jnp.float32
BF16 = jnp.bfloat16

EPS = 1e-6
CHUNK = 64
SUB = 32
LANES = 128
SUBLANES = 8
POOL_WINDOWS = (2, 4, 8, 16)
HALO_POOL = 16
HALO_CONV = 16
EXP_CLAMP = 80.0

ADAM_LR = 0.001
ADAM_B1 = 0.9
ADAM_B2 = 0.999
ADAM_EPS = 1e-08
ADAM_WD = 0.01
ADAM_STEP = 10

VMEM_LIMIT = 56 * 1024 * 1024
MESH_ID = pl.DeviceIdType.MESH
N_CHIPS = 4
ANY = pl.BlockSpec(memory_space=pl.ANY)


def _dot(a, b):
    return jnp.dot(a, b, preferred_element_type=F32)


def _dot_nt(a, b):
    return lax.dot_general(a, b, (((1,), (1,)), ((), ())), preferred_element_type=F32)


def _dot_tn(a, b):
    return lax.dot_general(a, b, (((0,), (0,)), ((), ())), preferred_element_type=F32)


def _sigmoid(x):
    return jax.nn.sigmoid(x)


def _dsilu(x, s):
    return s * (1.0 + x * (1.0 - s))


def _row_tile(rows, want):
    t = min(rows, want)
    while rows % t:
        t //= 2
    return t


def _place():
    x, y, c = lax.axis_index("x"), lax.axis_index("y"), lax.axis_index("c")
    chips = [(1 - x, y), (x, 1 - y), (1 - x, 1 - y)]
    return x, y, c, chips


def _remote(src, dst, sems, k, to):
    return pltpu.make_async_remote_copy(src_ref=src, dst_ref=dst, send_sem=sems[0].at[k], recv_sem=sems[1].at[k],
                                        device_id=to, device_id_type=MESH_ID)


class _GatherPlan:
    def __init__(self, items):
        self.items = items
        self.inputs = [a for a, _, _ in items]
        self.out_shapes = []
        for a, kind, _ in items:
            shp = (N_CHIPS,) + a.shape[1:] if kind == "rows" else (a.shape[0], N_CHIPS) + a.shape[1:]
            self.out_shapes.append(jax.ShapeDtypeStruct(shp, a.dtype))
        n = len(items)
        self.scratch = [pltpu.SemaphoreType.DMA((6 * n,)), pltpu.SemaphoreType.DMA((6 * n,)),
                        pltpu.SemaphoreType.DMA((2 * n,))]

    def _views(self, i, src, dst):
        _, kind, l = self.items[i]
        if kind == "rows":
            half = src.shape[1] // 2
            part = lambda core: src.at[l, pl.ds(core * half, half), :]
            land = lambda chip, core: dst.at[chip, pl.ds(core * half, half), :]
        else:
            part = lambda core: src.at[core]
            land = lambda chip, core: dst.at[core, chip]
        return part, land

    def start(self, srcs, dsts, sems):
        x, y, c, chips = _place()
        me = 2 * x + y
        for i, (src, dst) in enumerate(zip(srcs, dsts)):
            part, land = self._views(i, src, dst)
            for core in range(2):
                pltpu.make_async_copy(part(core), land(me, core), sems[2].at[2 * i + core]).start()
            for j, (px, py) in enumerate(chips):
                _remote(part(c), land(me, c), sems, 6 * i + j, (px, py, c)).start()

    def finish(self, srcs, dsts, sems):
        x, y, c, chips = _place()
        me = 2 * x + y
        sibling = (x, y, 1 - c)
        for i, (src, dst) in enumerate(zip(srcs, dsts)):
            part, land = self._views(i, src, dst)
            for j, (px, py) in enumerate(chips):
                got = land(2 * px + py, c)
                _remote(got, got, sems, 6 * i + j, (px, py, c)).wait_recv()
                _remote(got, got, sems, 6 * i + 3 + j, sibling).start()
        for i, (src, dst) in enumerate(zip(srcs, dsts)):
            part, land = self._views(i, src, dst)
            for j, (px, py) in enumerate(chips):
                got = land(2 * px + py, 1 - c)
                _remote(got, got, sems, 6 * i + 3 + j, sibling).wait_recv()
            for j, (px, py) in enumerate(chips):
                _remote(part(c), land(me, c), sems, 6 * i + j, (px, py, c)).wait_send()
                mine = land(2 * px + py, c)
                _remote(mine, mine, sems, 6 * i + 3 + j, sibling).wait_send()
            for core in range(2):
                pltpu.make_async_copy(part(core), land(me, core), sems[2].at[2 * i + core]).wait()


class _ScatterPlan:
    def __init__(self, items):
        self.inputs = list(items)
        self.out_shapes = [jax.ShapeDtypeStruct((3,) + a.shape[1:], a.dtype) for a in items]
        n = len(items)
        self.scratch = [pltpu.SemaphoreType.DMA((3 * n,)), pltpu.SemaphoreType.DMA((3 * n,))]

    def _copies(self, srcs, dsts, sems):
        x, y, c, chips = _place()
        return [_remote(src.at[2 * px + py], dst.at[j], sems, 3 * i + j, (px, py, c))
                for i, (src, dst) in enumerate(zip(srcs, dsts)) for j, (px, py) in enumerate(chips)]

    def start(self, srcs, dsts, sems):
        for cp in self._copies(srcs, dsts, sems):
            cp.start()

    def finish(self, srcs, dsts, sems):
        copies = self._copies(srcs, dsts, sems)
        for cp in copies:
            cp.wait_recv()
        for cp in copies:
            cp.wait_send()


class _SiblingPlan:
    def __init__(self, items):
        self.inputs = list(items)
        self.out_shapes = [jax.ShapeDtypeStruct(a.shape, a.dtype) for a in items]
        n = len(items)
        self.scratch = [pltpu.SemaphoreType.DMA((n,)), pltpu.SemaphoreType.DMA((n,))]

    def _copies(self, srcs, dsts, sems):
        x, y, c, _ = _place()
        return [_remote(src, dst, sems, i, (x, y, 1 - c)) for i, (src, dst) in enumerate(zip(srcs, dsts))]

    def start(self, srcs, dsts, sems):
        for cp in self._copies(srcs, dsts, sems):
            cp.start()

    def finish(self, srcs, dsts, sems):
        copies = self._copies(srcs, dsts, sems)
        for cp in copies:
            cp.wait_recv()
        for cp in copies:
            cp.wait_send()


class _EveryonePlan:
    def __init__(self, block):
        self.inputs = [block]
        self.m = block.shape[0]
        self.out_shapes = [jax.ShapeDtypeStruct((8 * self.m,) + block.shape[1:], block.dtype)]
        self.scratch = [pltpu.SemaphoreType.DMA((7,)), pltpu.SemaphoreType.DMA((7,)), pltpu.SemaphoreType.DMA((1,))]

    def _rows(self, dst, px, py, pc):
        return dst.at[pl.ds((4 * px + 2 * py + pc) * self.m, self.m), :]

    def start(self, srcs, dsts, sems):
        x, y, c, chips = _place()
        src, dst = srcs[0], dsts[0]
        pltpu.make_async_copy(src, self._rows(dst, x, y, c), sems[2].at[0]).start()
        _remote(src, self._rows(dst, x, y, c), sems, 0, (x, y, 1 - c)).start()
        for j, (px, py) in enumerate(chips):
            _remote(src, self._rows(dst, x, y, c), sems, 1 + j, (px, py, c)).start()

    def finish(self, srcs, dsts, sems):
        x, y, c, chips = _place()
        src, dst = srcs[0], dsts[0]
        sibling = (x, y, 1 - c)
        for j, (px, py) in enumerate(chips):
            got = self._rows(dst, px, py, c)
            _remote(got, got, sems, 1 + j, (px, py, c)).wait_recv()
            _remote(got, got, sems, 4 + j, sibling).start()
        sib = self._rows(dst, x, y, 1 - c)
        _remote(sib, sib, sems, 0, sibling).wait_recv()
        for j, (px, py) in enumerate(chips):
            got = self._rows(dst, px, py, 1 - c)
            _remote(got, got, sems, 4 + j, sibling).wait_recv()
        mine = self._rows(dst, x, y, c)
        _remote(src, mine, sems, 0, sibling).wait_send()
        for j, (px, py) in enumerate(chips):
            _remote(src, mine, sems, 1 + j, (px, py, c)).wait_send()
            got = self._rows(dst, px, py, c)
            _remote(got, got, sems, 4 + j, sibling).wait_send()
        pltpu.make_async_copy(src, mine, sems[2].at[0]).wait()


def _call(body, *, name, grid, in_specs, out_specs, out_shape, args, scratch=(), parallel=(), plan=None):
    n_in, n_out, n_scr = len(in_specs), len(out_shape), len(scratch)
    sem = tuple("parallel" if (a in parallel and plan is None) else "arbitrary" for a in range(len(grid)))
    params = pltpu.CompilerParams(dimension_semantics=sem, vmem_limit_bytes=VMEM_LIMIT)
    if plan is None:
        outs = pl.pallas_call(body, name=name, grid=grid, in_specs=in_specs, out_specs=out_specs,
                              out_shape=out_shape, scratch_shapes=list(scratch), compiler_params=params)(*args)
        return list(outs), []
    p_in, p_out, p_scr = len(plan.inputs), len(plan.out_shapes), len(plan.scratch)

    def wrapped(*refs):
        ins, refs = refs[:n_in], refs[n_in:]
        p_ins, refs = refs[:p_in], refs[p_in:]
        outs, refs = refs[:n_out], refs[n_out:]
        p_outs, refs = refs[:p_out], refs[p_out:]
        scr, p_sems = refs[:n_scr], refs[n_scr:]
        ids = [pl.program_id(a) for a in range(len(grid))]
        first = _all([i == 0 for i in ids])
        last = _all([i == n - 1 for i, n in zip(ids, grid)])

        @pl.when(first)
        def _():
            plan.start(p_ins, p_outs, p_sems)

        body(*ins, *outs, *scr)

        @pl.when(last)
        def _():
            plan.finish(p_ins, p_outs, p_sems)

    outs = pl.pallas_call(
        wrapped, name=name, grid=grid,
        in_specs=list(in_specs) + [ANY] * p_in, out_specs=list(out_specs) + [ANY] * p_out,
        out_shape=list(out_shape) + list(plan.out_shapes),
        scratch_shapes=list(scratch) + list(plan.scratch), compiler_params=params,
    )(*args, *plan.inputs)
    return list(outs[:n_out]), list(outs[n_out:])


def _all(conds):
    out = conds[0]
    for c in conds[1:]:
        out = out & c
    return out


def _run_plan(plan, *, name):
    p_in, p_out = len(plan.inputs), len(plan.out_shapes)

    def body(*refs):
        srcs, dsts, sems = refs[:p_in], refs[p_in:p_in + p_out], refs[p_in + p_out:]
        plan.start(srcs, dsts, sems)
        plan.finish(srcs, dsts, sems)

    return list(pl.pallas_call(body, name=name, in_specs=[ANY] * p_in, out_specs=[ANY] * p_out,
                               out_shape=list(plan.out_shapes), scratch_shapes=list(plan.scratch))(*plan.inputs))


def _norm_matmul(x, g, w4, *, name, out_dtype=F32, plan=None):
    S, D = x.shape
    n4 = w4.shape[2]
    tm = _row_tile(S, 1024)

    def body(x_ref, g_ref, w_ref, xn_ref, o_ref):
        @pl.when(pl.program_id(1) == 0)
        def _():
            xf = x_ref[...]
            r = lax.rsqrt(jnp.mean(xf * xf, axis=-1, keepdims=True) + EPS)
            xn_ref[...] = (xf * r * g_ref[...]).astype(BF16)

        o_ref[...] = _dot(xn_ref[...], w_ref[...]).astype(out_dtype)

    return _call(
        body, name=name, grid=(S // tm, N_CHIPS), parallel=(0,), plan=plan,
        in_specs=[pl.BlockSpec((tm, D), lambda i, j: (i, 0)),
                  pl.BlockSpec((1, D), lambda i, j: (0, 0)),
                  pl.BlockSpec((None, D, n4), lambda i, j: (j, 0, 0))],
        out_specs=[pl.BlockSpec((tm, D), lambda i, j: (i, 0)),
                   pl.BlockSpec((tm, n4), lambda i, j: (i, j))],
        out_shape=[jax.ShapeDtypeStruct((S, D), BF16), jax.ShapeDtypeStruct((S, N_CHIPS * n4), out_dtype)],
        args=(x, g, w4))


def _segments(widths, n4):
    per_chip = [[] for _ in range(N_CHIPS)]
    c0 = 0
    for p, w in enumerate(widths):
        a = c0
        while a < c0 + w:
            k = a // n4
            b = min(c0 + w, (k + 1) * n4)
            per_chip[k].append((p, (a - c0, b - c0), (a - k * n4, b - k * n4)))
            a = b
        c0 += w
    assert c0 == N_CHIPS * n4
    return per_chip


def _piece_specs(pieces, n4, tm):
    per_chip = _segments([p.shape[1] for p in pieces], n4)
    specs, local, start = [], [[] for _ in range(N_CHIPS)], 0
    for p, arr in enumerate(pieces):
        chips = [k for k in range(N_CHIPS) if any(seg[0] == p for seg in per_chip[k])]
        lo, hi = chips[0], chips[-1]
        tiled = arr.shape[1] % n4 == 0 and start % n4 == 0
        start += arr.shape[1]
        if tiled:
            imap = lambda k, i, lo=lo, hi=hi: (jnp.where((k >= lo) & (k <= hi), i, 0), jnp.clip(k - lo, 0, hi - lo))
            specs.append(pl.BlockSpec((tm, n4), imap))
        else:
            imap = lambda k, i, lo=lo, hi=hi: (jnp.where((k >= lo) & (k <= hi), i, 0), 0)
            specs.append(pl.BlockSpec((tm, arr.shape[1]), imap))
        for k in chips:
            for q, (pa, pb), cols in per_chip[k]:
                if q == p:
                    local[k].append((p, (0, n4) if tiled else (pa, pb), cols))
    return specs, local


def _dgrad_norm(dys, w4, x, g, dres, *, name, plan=None):
    S, D = x.shape
    n4 = w4.shape[2]
    tm = _row_tile(S, 512)
    per_chip = _segments([a.shape[1] for a in dys], n4)
    n_p = len(dys)

    def body(*refs):
        dy_refs = refs[:n_p]
        w_ref, x_ref, g_ref, dres_ref, dx_ref, dg_ref = refs[n_p:]

        @pl.when(pl.program_id(0) == 0)
        def _():
            dg_ref[...] = jnp.zeros_like(dg_ref)

        dxn = None
        for k in range(N_CHIPS):
            for p, (pa, pb), (ca, cb) in per_chip[k]:
                part = _dot_nt(dy_refs[p][:, pa:pb], w_ref[k, :, ca:cb])
                dxn = part if dxn is None else dxn + part
        xf = x_ref[...]
        r = lax.rsqrt(jnp.mean(xf * xf, axis=-1, keepdims=True) + EPS)
        xhat = xf * r
        dxhat = dxn * g_ref[...]
        dx_ref[...] = dres_ref[...] + r * (dxhat - xhat * jnp.mean(dxhat * xhat, axis=-1, keepdims=True))
        dg_ref[...] += jnp.sum(dxn * xhat, axis=0, keepdims=True)

    row = lambda w: pl.BlockSpec((tm, w), lambda i: (i, 0))
    return _call(
        body, name=name, grid=(S // tm,), plan=plan,
        in_specs=[row(a.shape[1]) for a in dys]
        + [pl.BlockSpec(w4.shape, lambda i: (0, 0, 0), pipeline_mode=pl.Buffered(1)),
           row(D), pl.BlockSpec((1, D), lambda i: (0, 0)), row(D)],
        out_specs=[row(D), pl.BlockSpec((1, D), lambda i: (0, 0))],
        out_shape=[jax.ShapeDtypeStruct((S, D), F32), jax.ShapeDtypeStruct((1, D), F32)],
        args=(*dys, w4, x, g, dres))


def _wgrad(a, dys, *, name, rows, plan=None):
    S, K = a.shape
    n4 = sum(p.shape[1] for p in dys) // N_CHIPS
    tm = _row_tile(S, rows)
    ns = S // tm
    specs, local = _piece_specs(dys, n4, tm)
    n_p = len(dys)

    def body(*refs):
        a_ref = refs[0]
        dy_refs = refs[1:1 + n_p]
        o_ref, o16_ref = refs[1 + n_p:]
        n, s = pl.program_id(0), pl.program_id(1)

        @pl.when(s == 0)
        def _():
            o_ref[...] = jnp.zeros_like(o_ref)

        for k in range(N_CHIPS):
            @pl.when(n == k)
            def _():
                av = a_ref[...]
                for p, (pa, pb), (ca, cb) in local[k]:
                    o_ref[:, ca:cb] += _dot_tn(av, dy_refs[p][:, pa:pb])

        @pl.when(s == ns - 1)
        def _():
            o16_ref[...] = o_ref[...].astype(BF16)

    out = pl.BlockSpec((None, K, n4), lambda n, s: (n, 0, 0))
    return _call(
        body, name=name, grid=(N_CHIPS, ns), parallel=(0,), plan=plan,
        in_specs=[pl.BlockSpec((tm, K), lambda n, s: (s, 0))] + specs,
        out_specs=[out, out],
        out_shape=[jax.ShapeDtypeStruct((N_CHIPS, K, n4), F32), jax.ShapeDtypeStruct((N_CHIPS, K, n4), BF16)],
        args=(a, *dys))


def _tiles(x):
    return x.reshape(x.shape[0] // SUBLANES, SUBLANES, x.shape[1])


def _shift_down(xp, s):
    n = xp.shape[0] - SUBLANES
    if s == SUBLANES:
        return xp[:n, :]
    t = _tiles(xp)
    rot = pltpu.roll(t, s, 1)
    sub = lax.broadcasted_iota(jnp.int32, t.shape, 1)[1:]
    return jnp.where(sub >= s, rot[1:], rot[:-1]).reshape(n, xp.shape[1])


def _shift_up(xn, s):
    n = xn.shape[0] - SUBLANES
    if s == SUBLANES:
        return xn[SUBLANES:, :]
    t = _tiles(xn)
    rot = pltpu.roll(t, SUBLANES - s, 1)
    sub = lax.broadcasted_iota(jnp.int32, t.shape, 1)[1:]
    return jnp.where(sub < SUBLANES - s, rot[:-1], rot[1:]).reshape(n, xn.shape[1])


def _pooled(u, halo, first_tile, row0):
    T = u.shape[0]
    halo = jnp.where(first_tile, 0.0, halo)
    pad = jnp.zeros((SUBLANES, u.shape[1]), F32)
    up = jnp.concatenate([pad, halo, u], axis=0)
    t1 = (row0 + lax.broadcasted_iota(jnp.int32, (T, 1), 0) + 1).astype(F32)
    outs = []
    for gi, w in enumerate(POOL_WINDOWS):
        s = up[:, gi * LANES:(gi + 1) * LANES]
        k = 1
        while k < w:
            if k < SUBLANES:
                s = jnp.concatenate([s[:SUBLANES, :], s[SUBLANES:, :] + _shift_down(s, k)], axis=0)
            else:
                s = s[SUBLANES:, :] + _shift_down(s, k)
            k *= 2
        s = s[-T:, :]
        inv = 1.0 / jnp.minimum(t1, float(w))
        outs.append(s * inv - u[:, gi * LANES:(gi + 1) * LANES])
    return outs


def _pool_fwd(z, pool_w, pool_scale, *, name):
    S = z.shape[0]
    P = pool_scale.shape[1]
    T = _row_tile(S, 512)
    hb = T // HALO_POOL

    def body(u_ref, halo_ref, pw_ref, ps_ref, o_ref):
        i = pl.program_id(0)
        pooled = _pooled(u_ref[...], halo_ref[...], i == 0, i * T)
        for gi in range(len(POOL_WINDOWS)):
            mixed = _dot(pooled[gi].astype(BF16), pw_ref[gi].astype(BF16))
            cols = slice(gi * LANES, (gi + 1) * LANES)
            o_ref[:, cols] = (mixed * ps_ref[:, cols]).astype(BF16)

    return _call(
        body, name=name, grid=(S // T,), parallel=(0,),
        in_specs=[pl.BlockSpec((T, P), lambda i: (i, 0)),
                  pl.BlockSpec((HALO_POOL, P), lambda i: (jnp.maximum(i * hb - 1, 0), 0)),
                  pl.BlockSpec(pool_w.shape, lambda i: (0, 0, 0)),
                  pl.BlockSpec((1, P), lambda i: (0, 0))],
        out_specs=[pl.BlockSpec((T, P), lambda i: (i, 0))],
        out_shape=[jax.ShapeDtypeStruct((S, P), BF16)],
        args=(z, z, pool_w, pool_scale))[0][0]


def _pool_bwd(z, dpm, pool_w, pool_scale, *, name):
    S, P = dpm.shape
    T = _row_tile(S, 512)
    hb = T // HALO_POOL
    nt = S // T

    def body(u_ref, halo_ref, d_ref, dnext_ref, pw_ref, ps_ref, du_ref, dpw_ref, dps_ref):
        i = pl.program_id(0)

        @pl.when(i == 0)
        def _():
            dpw_ref[...] = jnp.zeros_like(dpw_ref)
            dps_ref[...] = jnp.zeros_like(dps_ref)

        pooled = _pooled(u_ref[...], halo_ref[...], i == 0, i * T)
        dnext = jnp.where(i == nt - 1, 0.0, dnext_ref[...])
        pad = jnp.zeros((SUBLANES, P), F32)
        dext = jnp.concatenate([d_ref[...], dnext, pad], axis=0)
        t1 = (i * T + lax.broadcasted_iota(jnp.int32, (T + HALO_POOL + SUBLANES, 1), 0) + 1).astype(F32)
        for gi, w in enumerate(POOL_WINDOWS):
            cols = slice(gi * LANES, (gi + 1) * LANES)
            pw = pw_ref[gi].astype(BF16)
            pg = pooled[gi].astype(BF16)
            mixed = _dot(pg, pw)
            dps_ref[:, cols] += jnp.sum(d_ref[:, cols] * mixed, axis=0, keepdims=True)
            dmixed = (dext[:, cols] * ps_ref[:, cols]).astype(BF16)
            dpw_ref[gi] += _dot_tn(pg, dmixed[:T, :])
            dpooled = _dot_nt(dmixed, pw)
            e = dpooled * (1.0 / jnp.minimum(t1, float(w)))
            k = 1
            while k < w:
                if k < SUBLANES:
                    e = jnp.concatenate([e[:-SUBLANES, :] + _shift_up(e, k), e[-SUBLANES:, :]], axis=0)
                else:
                    e = e[:-SUBLANES, :] + _shift_up(e, k)
                k *= 2
            du_ref[:, cols] = (e[:T, :] - dpooled[:T, :]).astype(BF16)

    return _call(
        body, name=name, grid=(nt,),
        in_specs=[pl.BlockSpec((T, P), lambda i: (i, 0)),
                  pl.BlockSpec((HALO_POOL, P), lambda i: (jnp.maximum(i * hb - 1, 0), 0)),
                  pl.BlockSpec((T, P), lambda i: (i, 0)),
                  pl.BlockSpec((HALO_POOL, P), lambda i: (jnp.minimum((i + 1) * hb, S // HALO_POOL - 1), 0)),
                  pl.BlockSpec(pool_w.shape, lambda i: (0, 0, 0)),
                  pl.BlockSpec((1, P), lambda i: (0, 0))],
        out_specs=[pl.BlockSpec((T, P), lambda i: (i, 0)),
                   pl.BlockSpec(pool_w.shape, lambda i: (0, 0, 0)),
                   pl.BlockSpec((1, P), lambda i: (0, 0))],
        out_shape=[jax.ShapeDtypeStruct((S, P), BF16),
                   jax.ShapeDtypeStruct(pool_w.shape, F32),
                   jax.ShapeDtypeStruct((1, P), F32)],
        args=(z, z, dpm, dpm, pool_w, pool_scale))[0]


def _cumsum_rows(x):
    n = x.shape[0]
    row = lax.broadcasted_iota(jnp.int32, x.shape, 0)
    s = 1
    while s < n:
        x = x + jnp.where(row >= s, pltpu.roll(x, s, 0), 0.0)
        s *= 2
    return x


def _rev_cumsum_rows(x):
    n = x.shape[0]
    row = lax.broadcasted_iota(jnp.int32, x.shape, 0)
    s = 1
    while s < n:
        x = x + jnp.where(row < n - s, pltpu.roll(x, n - s, 0), 0.0)
        s *= 2
    return x


def _chunk_prep(zq, zf, lb, b_ref):
    n_sub = CHUNK // SUB
    sq = _sigmoid(zq)
    q = zq * sq
    sf = _sigmoid(zf)
    f = lb + (1.0 - lb) * sf
    k = 1.0 - f
    b = _cumsum_rows(jnp.log(f))
    b_ref[...] = b
    shape = (SUB, b.shape[1])
    ends = [jnp.broadcast_to(b_ref[pl.ds(SUB * j + SUB - 1, 1), :], shape) for j in range(n_sub)]
    mids = [jnp.broadcast_to(b_ref[pl.ds(SUB * j + SUB // 2 - 1, 1), :], shape) for j in range(n_sub)]
    own = [b[SUB * j:SUB * (j + 1), :] for j in range(n_sub)]
    m0 = jnp.concatenate(mids, axis=0)
    e1 = jnp.concatenate(ends, axis=0)
    eq = [jnp.exp(jnp.minimum(b - m0, EXP_CLAMP))]
    for d in range(1, n_sub):
        rd = jnp.concatenate([own[j] if j < d else ends[j - d] for j in range(n_sub)], axis=0)
        eq.append(jnp.exp(b - rd))
    ek0 = jnp.exp(jnp.minimum(m0 - b, EXP_CLAMP))
    ek1 = jnp.exp(e1 - b)
    b_last = b_ref[pl.ds(CHUNK - 1, 1), :]
    return dict(q=q, k=k, f=f, sq=sq, sf=sf, b=b, eq=eq, ek0=ek0, ek1=ek1,
                eb=jnp.exp(b), ekl=jnp.exp(b_last - b), el=jnp.exp(b_last))


def _chunk_masks():
    ti = lax.broadcasted_iota(jnp.int32, (CHUNK, CHUNK), 0)
    si = lax.broadcasted_iota(jnp.int32, (CHUNK, CHUNK), 1)
    shift = SUB.bit_length() - 1
    dsub = jnp.right_shift(ti, shift) - jnp.right_shift(si, shift)
    masks = [(dsub == 0) & (si <= ti)]
    masks += [dsub == d for d in range(1, CHUNK // SUB)]
    return masks


def _chunk_attn(p, masks):
    qd = [(p["q"] * e).astype(BF16) for e in p["eq"]]
    k0 = (p["k"] * p["ek0"]).astype(BF16)
    k1 = (p["k"] * p["ek1"]).astype(BF16)
    a = jnp.where(masks[0], _dot_nt(qd[0], k0), 0.0)
    for d in range(1, len(masks)):
        a = jnp.where(masks[d], _dot_nt(qd[d], k1), a)
    return a, qd, k0, k1


def _hgrn_fwd(z, lb, norm_g, *, name, plan=None):
    S = z.shape[0]
    HW = lb.shape[1]
    NH = HW // LANES
    T = _row_tile(S, 512)
    nc = T // CHUNK

    def body(zq_ref, zf_ref, zi_ref, zo_ref, lb_ref, ng_ref, o_ref, of_ref, st_ref, s_scr, b_scr):
        @pl.when(pl.program_id(0) == 0)
        def _():
            s_scr[...] = jnp.zeros_like(s_scr)

        ng = ng_ref[...]
        masks = _chunk_masks()

        def chunk(c, carry):
            rows = pl.ds(pl.multiple_of(c * CHUNK, CHUNK), CHUNK)
            for h in range(NH):
                cols = slice(h * LANES, (h + 1) * LANES)
                p = _chunk_prep(zq_ref[rows, cols], zf_ref[rows, cols], lb_ref[:, cols], b_scr.at[h])
                v = zi_ref[rows, cols].astype(BF16)
                zo = zo_ref[rows, cols]
                st = s_scr[h]
                st_ref[c, h] = st
                a, _, _, _ = _chunk_attn(p, masks)
                o = _dot(a.astype(BF16), v) + _dot_nt((p["q"] * p["eb"]).astype(BF16), st.astype(BF16))
                s_scr[h] = st * p["el"] + _dot_tn(v, (p["k"] * p["ekl"]).astype(BF16))
                o_ref[rows, cols] = o
                r = lax.rsqrt(jnp.mean(o * o, axis=-1, keepdims=True) + EPS)
                of_ref[rows, cols] = (o * r * ng * (zo * _sigmoid(zo))).astype(BF16)
            return carry

        lax.fori_loop(0, nc, chunk, 0)

    part = lambda k: pl.BlockSpec((T, HW), lambda i, k=k: (i, k))
    return _call(
        body, name=name, grid=(S // T,), plan=plan,
        in_specs=[part(1), part(2), part(3), part(4),
                  pl.BlockSpec((1, HW), lambda i: (0, 0)), pl.BlockSpec((1, LANES), lambda i: (0, 0))],
        out_specs=[pl.BlockSpec((T, HW), lambda i: (i, 0)), pl.BlockSpec((T, HW), lambda i: (i, 0)),
                   pl.BlockSpec((nc, NH, LANES, LANES), lambda i: (i, 0, 0, 0))],
        out_shape=[jax.ShapeDtypeStruct((S, HW), F32), jax.ShapeDtypeStruct((S, HW), BF16),
                   jax.ShapeDtypeStruct((S // CHUNK, NH, LANES, LANES), F32)],
        scratch=[pltpu.VMEM((NH, LANES, LANES), F32), pltpu.VMEM((NH, CHUNK, LANES), F32)],
        args=(z, z, z, z, lb, norm_g))


def _hgrn_bwd(z, lb, norm_g, o_raw, states, dof, *, name, plan=None):
    S = z.shape[0]
    HW = lb.shape[1]
    NH = HW // LANES
    T = _row_tile(S, 512)
    nc = T // CHUNK
    nt = S // T

    def body(zq_ref, zf_ref, zi_ref, zo_ref, lb_ref, ng_ref, o_ref, st_ref, dof_ref,
             dzq_ref, dzf_ref, dzi_ref, dzo_ref, dlb_ref, dng_ref, ds_scr, b_scr):
        @pl.when(pl.program_id(0) == 0)
        def _():
            ds_scr[...] = jnp.zeros_like(ds_scr)
            dlb_ref[...] = jnp.zeros_like(dlb_ref)
            dng_ref[...] = jnp.zeros_like(dng_ref)

        ng = ng_ref[...]
        masks = _chunk_masks()
        last_row = lax.broadcasted_iota(jnp.int32, (CHUNK, 1), 0) == CHUNK - 1

        def chunk(cr, carry):
            c = nc - 1 - cr
            rows = pl.ds(pl.multiple_of(c * CHUNK, CHUNK), CHUNK)
            for h in range(NH):
                cols = slice(h * LANES, (h + 1) * LANES)
                lbv = lb_ref[:, cols]
                zq, zf, zo = zq_ref[rows, cols], zf_ref[rows, cols], zo_ref[rows, cols]
                o = o_ref[rows, cols]
                dof_c = dof_ref[rows, cols]
                st = st_ref[c, h]
                dst = ds_scr[h]

                so = _sigmoid(zo)
                r = lax.rsqrt(jnp.mean(o * o, axis=-1, keepdims=True) + EPS)
                ohat = o * r
                d_on = dof_c * (zo * so)
                dzo_ref[rows, cols] = (dof_c * ohat * ng * _dsilu(zo, so)).astype(BF16)
                dng_ref[:, cols] += jnp.sum(d_on * ohat, axis=0, keepdims=True)
                dohat = d_on * ng
                do = (r * (dohat - ohat * jnp.mean(dohat * ohat, axis=-1, keepdims=True))).astype(BF16)

                p = _chunk_prep(zq, zf, lbv, b_scr.at[h])
                q, k = p["q"], p["k"]
                v = zi_ref[rows, cols].astype(BF16)
                a, qd, k0, k1 = _chunk_attn(p, masks)
                ktl = (k * p["ekl"]).astype(BF16)
                dstb = dst.astype(BF16)

                da = _dot_nt(do, v)
                dzi_ref[rows, cols] = (_dot_tn(a.astype(BF16), do) + _dot_nt(ktl, dstb)).astype(BF16)

                da0 = jnp.where(masks[0], da, 0.0).astype(BF16)
                rq = _dot(da0, k0)
                rk0 = _dot_tn(da0, qd[0])
                dq = rq * p["eq"][0]
                db = qd[0].astype(F32) * rq - k0.astype(F32) * rk0
                rk1 = jnp.zeros_like(rk0)
                for d in range(1, len(masks)):
                    dad = jnp.where(masks[d], da, 0.0).astype(BF16)
                    rq = _dot(dad, k1)
                    dq = dq + rq * p["eq"][d]
                    db = db + qd[d].astype(F32) * rq
                    rk1 = rk1 + _dot_tn(dad, qd[d])
                dk = rk0 * p["ek0"] + rk1 * p["ek1"]
                db = db - k1.astype(F32) * rk1
                qe = (q * p["eb"]).astype(BF16)
                rq = _dot(do, st.astype(BF16))
                dq = dq + rq * p["eb"]
                db = db + qe.astype(F32) * rq
                rk = _dot(v, dstb)
                dk = dk + rk * p["ekl"]
                db = db - ktl.astype(F32) * rk

                st_new = st * p["el"] + _dot_tn(v, ktl)
                db = db + jnp.where(last_row, jnp.sum(dstb.astype(F32) * st_new, axis=0, keepdims=True), 0.0)
                dg = _rev_cumsum_rows(db)
                ds_scr[h] = dst * p["el"] + _dot_tn(do, qe)

                dzq_ref[rows, cols] = (dq * _dsilu(zq, p["sq"])).astype(BF16)
                df = dg / p["f"] - dk
                sf = p["sf"]
                dzf_ref[rows, cols] = (df * (1.0 - lbv) * sf * (1.0 - sf)).astype(BF16)
                dlb_ref[:, cols] += jnp.sum(df * (1.0 - sf), axis=0, keepdims=True)
            return carry

        lax.fori_loop(0, nc, chunk, 0)

    rev = lambda i: nt - 1 - i
    part = lambda k: pl.BlockSpec((T, HW), lambda i, k=k: (rev(i), k))
    blk = pl.BlockSpec((T, HW), lambda i: (rev(i), 0))
    vec = pl.BlockSpec((1, HW), lambda i: (0, 0))
    return _call(
        body, name=name, grid=(nt,), plan=plan,
        in_specs=[part(1), part(2), part(3), part(4), vec, pl.BlockSpec((1, LANES), lambda i: (0, 0)),
                  blk, pl.BlockSpec((nc, NH, LANES, LANES), lambda i: (rev(i), 0, 0, 0)), blk],
        out_specs=[blk, blk, blk, blk, vec, vec],
        out_shape=[jax.ShapeDtypeStruct((S, HW), BF16)] * 4 + [jax.ShapeDtypeStruct((1, HW), F32)] * 2,
        scratch=[pltpu.VMEM((NH, LANES, LANES), F32), pltpu.VMEM((NH, CHUNK, LANES), F32)],
        args=(z, z, z, z, lb, norm_g, o_raw, states, dof))


def _gate_specs(T, D):
    half = D // 2
    first = (5 * half) // half
    return [pl.BlockSpec((T, half), lambda i, k=k: (i, first + k)) for k in range(4)]


def _gates(zg_refs, bg_ref, D):
    half = D // 2
    za = jnp.concatenate([zg_refs[0][...], zg_refs[1][...]], axis=1) + bg_ref[:, :D]
    zb = jnp.concatenate([zg_refs[2][...], zg_refs[3][...]], axis=1) + bg_ref[:, D:]
    return _sigmoid(za), _sigmoid(zb)


def _mix_fwd(x, pm, of, z, b_gate, w_pa4, w_pb4, w_o4, *, name, plan=None):
    S, D = x.shape
    P = pm.shape[1]
    T = _row_tile(S, 512)

    def body(x_ref, pm_ref, of_ref, g0, g1, g2, g3, bg_ref, wpa_ref, wpb_ref, wo_ref, xo_ref, ya_ref, yb_ref):
        pmv, ofv = pm_ref[...], of_ref[...]
        ya = jnp.concatenate([_dot(pmv, wpa_ref[k]) for k in range(N_CHIPS)], axis=1)
        yb = jnp.concatenate([_dot(ofv, wpb_ref[k]) for k in range(N_CHIPS)], axis=1)
        ga, gb = _gates((g0, g1, g2, g3), bg_ref, D)
        merged = (ga * ya + gb * yb).astype(BF16)
        xo_ref[...] = x_ref[...] + _dot(merged, wo_ref[...].reshape(D, D))
        ya_ref[...] = ya.astype(BF16)
        yb_ref[...] = yb.astype(BF16)

    row = lambda w: pl.BlockSpec((T, w), lambda i: (i, 0))
    full = lambda a: pl.BlockSpec(a.shape, lambda i: (0,) * a.ndim)
    return _call(
        body, name=name, grid=(S // T,), parallel=(0,), plan=plan,
        in_specs=[row(D), row(P), row(P)] + _gate_specs(T, D) + [full(b_gate), full(w_pa4), full(w_pb4), full(w_o4)],
        out_specs=[row(D), row(D), row(D)],
        out_shape=[jax.ShapeDtypeStruct((S, D), F32), jax.ShapeDtypeStruct((S, D), BF16),
                   jax.ShapeDtypeStruct((S, D), BF16)],
        args=(x, pm, of, z, z, z, z, b_gate, w_pa4, w_pb4, w_o4))


def _mix_bwd(dxm, ya, yb, z, b_gate, pm, of, w_pa4, w_pb4, w_o4, *, name, plan=None):
    S, D = dxm.shape
    P = pm.shape[1]
    q4 = D // N_CHIPS
    T = _row_tile(S, 256)
    nt = S // T

    def body(dx_ref, ya_ref, yb_ref, g0, g1, g2, g3, bg_ref, pm_ref, of_ref, wpa_ref, wpb_ref, wo_ref,
             dzg_ref, dpm_ref, dof_ref, dwo_ref, dwpa_ref, dwpb_ref, dbg_ref, dwo16_ref, dwpa16_ref, dwpb16_ref):
        i = pl.program_id(0)

        @pl.when(i == 0)
        def _():
            dwo_ref[...] = jnp.zeros_like(dwo_ref)
            dwpa_ref[...] = jnp.zeros_like(dwpa_ref)
            dwpb_ref[...] = jnp.zeros_like(dwpb_ref)
            dbg_ref[...] = jnp.zeros_like(dbg_ref)

        dxb = dx_ref[...].astype(BF16)
        ya = ya_ref[...].astype(F32)
        yb = yb_ref[...].astype(F32)
        ga, gb = _gates((g0, g1, g2, g3), bg_ref, D)
        merged = (ga * ya + gb * yb).astype(BF16)
        dwo_ref[...] += _dot_tn(merged, dxb).reshape(N_CHIPS, q4, D)
        dm = _dot_nt(dxb, wo_ref[...].reshape(D, D))
        dza = dm * ya * ga * (1.0 - ga)
        dzb = dm * yb * gb * (1.0 - gb)
        dzg_ref[:, :D] = dza.astype(BF16)
        dzg_ref[:, D:] = dzb.astype(BF16)
        dbg_ref[:, :D] += jnp.sum(dza, axis=0, keepdims=True)
        dbg_ref[:, D:] += jnp.sum(dzb, axis=0, keepdims=True)
        dya = (dm * ga).astype(BF16)
        dyb = (dm * gb).astype(BF16)
        pmv, ofv = pm_ref[...], of_ref[...]
        dpm = jnp.zeros((T, P), F32)
        dof = jnp.zeros((T, P), F32)
        for k in range(N_CHIPS):
            cols = slice(k * q4, (k + 1) * q4)
            dwpa_ref[k] += _dot_tn(pmv, dya[:, cols])
            dwpb_ref[k] += _dot_tn(ofv, dyb[:, cols])
            dpm = dpm + _dot_nt(dya[:, cols], wpa_ref[k])
            dof = dof + _dot_nt(dyb[:, cols], wpb_ref[k])
        dpm_ref[...] = dpm
        dof_ref[...] = dof

        @pl.when(i == nt - 1)
        def _():
            dwo16_ref[...] = dwo_ref[...].astype(BF16)
            dwpa16_ref[...] = dwpa_ref[...].astype(BF16)
            dwpb16_ref[...] = dwpb_ref[...].astype(BF16)

    row = lambda w: pl.BlockSpec((T, w), lambda i: (i, 0))
    full = lambda a: pl.BlockSpec(a.shape, lambda i: (0,) * a.ndim)
    like = lambda a, dt: jax.ShapeDtypeStruct(a.shape, dt)
    return _call(
        body, name=name, grid=(nt,), plan=plan,
        in_specs=[row(D), row(D), row(D)] + _gate_specs(T, D) + [full(b_gate), row(P), row(P),
                                                                  full(w_pa4), full(w_pb4), full(w_o4)],
        out_specs=[row(2 * D), row(P), row(P), full(w_o4), full(w_pa4), full(w_pb4), full(b_gate),
                   full(w_o4), full(w_pa4), full(w_pb4)],
        out_shape=[jax.ShapeDtypeStruct((S, 2 * D), BF16), jax.ShapeDtypeStruct((S, P), F32),
                   jax.ShapeDtypeStruct((S, P), F32), like(w_o4, F32), like(w_pa4, F32), like(w_pb4, F32),
                   like(b_gate, F32), like(w_o4, BF16), like(w_pa4, BF16), like(w_pb4, BF16)],
        args=(dxm, ya, yb, z, z, z, z, b_gate, pm, of, w_pa4, w_pb4, w_o4))


def _conv3(h, halo, first_tile, cw, cb):
    h = h.astype(F32)
    halo = jnp.where(first_tile, 0.0, halo.astype(F32)[-SUBLANES:, :])
    hp = jnp.concatenate([halo, h], axis=0)
    h1 = _shift_down(hp, 1)
    h2 = _shift_down(hp, 2)
    return cw[0:1, :] * h2 + cw[1:2, :] * h1 + cw[2:3, :] * h + cb, h1, h2


def _ffn_down_fwd(h, conv_w, conv_b, w_down4, x, *, name, plan=None):
    S, F2 = h.shape
    _, f4, D = w_down4.shape
    T = _row_tile(S, 512)
    nf = 2
    tf = 2 * f4
    hb = T // HALO_CONV

    def body(hv_ref, hg_ref, pv_ref, pg_ref, cwv_ref, cwg_ref, cbv_ref, cbg_ref, wd_ref, x_ref,
             o_ref, val_ref, gate_ref, acc_ref):
        i, f = pl.program_id(0), pl.program_id(1)

        @pl.when(f == 0)
        def _():
            acc_ref[...] = jnp.zeros_like(acc_ref)

        val, _, _ = _conv3(hv_ref[...], pv_ref[...], i == 0, cwv_ref[...], cbv_ref[...])
        gate, _, _ = _conv3(hg_ref[...], pg_ref[...], i == 0, cwg_ref[...], cbg_ref[...])
        val_ref[...] = val.astype(BF16)
        gate_ref[...] = gate.astype(BF16)
        a = (gate * _sigmoid(gate) * val).astype(BF16)
        acc_ref[...] += _dot(a, wd_ref[...].reshape(tf, D))

        @pl.when(f == nf - 1)
        def _():
            o_ref[...] = x_ref[...] + acc_ref[...]

    prev = lambda i: jnp.maximum(i * hb - 1, 0)
    return _call(
        body, name=name, grid=(S // T, nf), parallel=(0,), plan=plan,
        in_specs=[pl.BlockSpec((T, tf), lambda i, f: (i, f)),
                  pl.BlockSpec((T, tf), lambda i, f: (i, nf + f)),
                  pl.BlockSpec((HALO_CONV, tf), lambda i, f: (prev(i), f)),
                  pl.BlockSpec((HALO_CONV, tf), lambda i, f: (prev(i), nf + f)),
                  pl.BlockSpec((3, tf), lambda i, f: (0, f)),
                  pl.BlockSpec((3, tf), lambda i, f: (0, nf + f)),
                  pl.BlockSpec((1, tf), lambda i, f: (0, f)),
                  pl.BlockSpec((1, tf), lambda i, f: (0, nf + f)),
                  pl.BlockSpec((2, f4, D), lambda i, f: (f, 0, 0)),
                  pl.BlockSpec((T, D), lambda i, f: (i, 0))],
        out_specs=[pl.BlockSpec((T, D), lambda i, f: (i, 0)),
                   pl.BlockSpec((T, tf), lambda i, f: (i, f)),
                   pl.BlockSpec((T, tf), lambda i, f: (i, f))],
        out_shape=[jax.ShapeDtypeStruct((S, D), F32), jax.ShapeDtypeStruct((S, N_CHIPS * f4), BF16),
                   jax.ShapeDtypeStruct((S, N_CHIPS * f4), BF16)],
        scratch=[pltpu.VMEM((T, D), F32)],
        args=(h, h, h, h, conv_w, conv_w, conv_b, conv_b, w_down4, x))


def _ffn_down_bwd(dxo, h, val16, gate16, conv_w, w_down4, *, name, plan=None):
    S, F2 = h.shape
    _, f4, D = w_down4.shape
    F = N_CHIPS * f4
    T = _row_tile(S, 256)
    nf = 2
    tf = 2 * f4
    nt = S // T

    def body(dx_ref, hv_ref, hg_ref, val_ref, gate_ref, cwv_ref, cwg_ref, wd_ref,
             dhv_ref, dhg_ref, dwd_ref, dwd16_ref, dcwv_ref, dcwg_ref, dcbv_ref, dcbg_ref, cv_scr, cg_scr):
        i = pl.program_id(1)

        @pl.when(i == 0)
        def _():
            cv_scr[...] = jnp.zeros_like(cv_scr)
            cg_scr[...] = jnp.zeros_like(cg_scr)
            dwd_ref[...] = jnp.zeros_like(dwd_ref)
            dcwv_ref[...] = jnp.zeros_like(dcwv_ref)
            dcwg_ref[...] = jnp.zeros_like(dcwg_ref)
            dcbv_ref[...] = jnp.zeros_like(dcbv_ref)
            dcbg_ref[...] = jnp.zeros_like(dcbg_ref)

        dxb = dx_ref[...].astype(BF16)
        val = val_ref[...].astype(F32)
        gate = gate_ref[...].astype(F32)
        sg = _sigmoid(gate)
        sil = gate * sg
        dwd_ref[...] += _dot_tn((sil * val).astype(BF16), dxb).reshape(2, f4, D)
        da = _dot_nt(dxb, wd_ref[...].reshape(tf, D))

        def conv_bwd(dhc, h0, cw, c_scr, dh_ref, dcw_ref, dcb_ref):
            ext = jnp.concatenate([dhc, c_scr[...]], axis=0)
            n1 = _shift_up(ext, 1)
            n2 = _shift_up(ext, 2)
            dh_ref[...] = (cw[2:3, :] * dhc + cw[1:2, :] * n1 + cw[0:1, :] * n2).astype(BF16)
            c_scr[...] = dhc[:SUBLANES, :]
            dcw_ref[0:1, :] += jnp.sum(n2 * h0, axis=0, keepdims=True)
            dcw_ref[1:2, :] += jnp.sum(n1 * h0, axis=0, keepdims=True)
            dcw_ref[2:3, :] += jnp.sum(dhc * h0, axis=0, keepdims=True)
            dcb_ref[...] += jnp.sum(dhc, axis=0, keepdims=True)

        conv_bwd(da * sil, hv_ref[...].astype(F32), cwv_ref[...], cv_scr, dhv_ref, dcwv_ref, dcbv_ref)
        conv_bwd(da * val * _dsilu(gate, sg), hg_ref[...].astype(F32), cwg_ref[...], cg_scr, dhg_ref, dcwg_ref,
                 dcbg_ref)

        @pl.when(i == nt - 1)
        def _():
            dwd16_ref[...] = dwd_ref[...].astype(BF16)

    rev = lambda i: nt - 1 - i
    wd_spec = pl.BlockSpec((2, f4, D), lambda f, i: (f, 0, 0))
    return _call(
        body, name=name, grid=(nf, nt), plan=plan,
        in_specs=[pl.BlockSpec((T, D), lambda f, i: (rev(i), 0)),
                  pl.BlockSpec((T, tf), lambda f, i: (rev(i), f)),
                  pl.BlockSpec((T, tf), lambda f, i: (rev(i), nf + f)),
                  pl.BlockSpec((T, tf), lambda f, i: (rev(i), f)),
                  pl.BlockSpec((T, tf), lambda f, i: (rev(i), f)),
                  pl.BlockSpec((3, tf), lambda f, i: (0, f)),
                  pl.BlockSpec((3, tf), lambda f, i: (0, nf + f)),
                  wd_spec],
        out_specs=[pl.BlockSpec((T, tf), lambda f, i: (rev(i), f)),
                   pl.BlockSpec((T, tf), lambda f, i: (rev(i), f)),
                   wd_spec, wd_spec,
                   pl.BlockSpec((3, tf), lambda f, i: (0, f)),
                   pl.BlockSpec((3, tf), lambda f, i: (0, f)),
                   pl.BlockSpec((1, tf), lambda f, i: (0, f)),
                   pl.BlockSpec((1, tf), lambda f, i: (0, f))],
        out_shape=[jax.ShapeDtypeStruct((S, F), BF16), jax.ShapeDtypeStruct((S, F), BF16),
                   jax.ShapeDtypeStruct((N_CHIPS, f4, D), F32), jax.ShapeDtypeStruct((N_CHIPS, f4, D), BF16),
                   jax.ShapeDtypeStruct((3, F), F32), jax.ShapeDtypeStruct((3, F), F32),
                   jax.ShapeDtypeStruct((1, F), F32), jax.ShapeDtypeStruct((1, F), F32)],
        scratch=[pltpu.VMEM((SUBLANES, tf), F32), pltpu.VMEM((SUBLANES, tf), F32)],
        args=(dxo, h, h, val16, gate16, conv_w, conv_w, w_down4))


def _final_loss(x, g, target, *, name):
    S, D = x.shape
    T = _row_tile(S, 512)

    def body(x_ref, g_ref, t_ref, loss_ref, dx_ref, dg_ref):
        @pl.when(pl.program_id(0) == 0)
        def _():
            loss_ref[...] = jnp.zeros_like(loss_ref)
            dg_ref[...] = jnp.zeros_like(dg_ref)

        xf = x_ref[...]
        r = lax.rsqrt(jnp.mean(xf * xf, axis=-1, keepdims=True) + EPS)
        xhat = xf * r
        err = xhat * g_ref[...] - t_ref[...]
        loss_ref[...] += jnp.sum(err * err, axis=0, keepdims=True) * (0.5 / D)
        dy = err * (1.0 / D)
        dxhat = dy * g_ref[...]
        dx_ref[...] = r * (dxhat - xhat * jnp.mean(dxhat * xhat, axis=-1, keepdims=True))
        dg_ref[...] += jnp.sum(dy * xhat, axis=0, keepdims=True)

    return _call(
        body, name=name, grid=(S // T,),
        in_specs=[pl.BlockSpec((T, D), lambda i: (i, 0)), pl.BlockSpec((1, D), lambda i: (0, 0)),
                  pl.BlockSpec((T, D), lambda i: (i, 0))],
        out_specs=[pl.BlockSpec((1, D), lambda i: (0, 0)), pl.BlockSpec((T, D), lambda i: (i, 0)),
                   pl.BlockSpec((1, D), lambda i: (0, 0))],
        out_shape=[jax.ShapeDtypeStruct((1, D), F32), jax.ShapeDtypeStruct((S, D), F32),
                   jax.ShapeDtypeStruct((1, D), F32)],
        args=(x, g, target))[0]


BIG = ("w_in", "w_pa", "w_pb", "w_o", "w_up", "w_down")
SMALL = ("norm1_g", "b_gate", "pool_w", "pool_scale", "lb_logits", "hgrn_norm_g", "norm2_g", "conv_b", "final_g")
WEIGHTS = ("norm1_g", "w_in", "b_gate", "pool_w", "pool_scale", "lb_logits", "hgrn_norm_g", "w_pa", "w_pb", "w_o",
           "norm2_g", "w_up", "conv_w", "conv_b", "w_down", "final_g")


def _lower_bounds(lb_logits):
    soft = jax.nn.softmax(lb_logits.astype(F32), axis=0)
    cum = jnp.cumsum(soft, axis=0)
    return cum - cum[0:1]


def _step(x, target, sm, wts, shards=None):
    L = sm["norm1_g"].shape[0]
    wts = dict(wts)
    dist = shards is not None
    lbs, lb_vjp = jax.vjp(_lower_bounds, sm["lb_logits"])
    row = lambda a: a.reshape(1, -1)
    conv_w = sm.get("conv_w")

    def gather(names_layers, with_conv=False):
        items = [(shards[n], "rows", l) for n, l in names_layers]
        if with_conv:
            items.append((shards["conv_w"], "layer", None))
        return _GatherPlan(items)

    def landed(names_layers, outs):
        for key, arr in zip(names_layers, outs):
            wts[key] = arr

    own = {"in_proj": ("w_pa", "w_pb", "w_o", "w_up"), "hgrn_fwd": ("w_down",)}
    ahead = {"hgrn_fwd": ("w_pa", "w_pb", "w_o"), "mix_fwd": ("w_down",), "up": ("w_in",), "down_fwd": ("w_up",)}

    def riders(l, kernel):
        if not dist:
            return [], None
        keys = [(n, l) for n in own.get(kernel, ())] if l == 0 else []
        keys += [(n, l + 1) for n in ahead.get(kernel, ())] if l + 1 < L else []
        with_conv = l == 0 and kernel == "hgrn_fwd"
        return keys, (gather(keys, with_conv) if keys or with_conv else None)

    saved = []
    for l in range(L):
        keys, plan = riders(l, "in_proj")
        (xn1, z), got = _norm_matmul(x, row(sm["norm1_g"][l]), wts[("w_in", l)], name=f"in_proj_{l}", plan=plan)
        landed(keys, got)
        pm = _pool_fwd(z, sm["pool_w"][l], row(sm["pool_scale"][l]), name=f"pool_fwd_{l}")
        keys, plan = riders(l, "hgrn_fwd")
        (o_raw, of, states), got = _hgrn_fwd(z, row(lbs[l]), row(sm["hgrn_norm_g"][l]), name=f"hgrn_fwd_{l}",
                                             plan=plan)
        landed(keys, got)
        if dist and l == 0:
            full = got[-1]
            conv_w = jnp.concatenate([full[:, k] for k in range(N_CHIPS)], axis=2)
        keys, plan = riders(l, "mix_fwd")
        (x_mid, ya, yb), got = _mix_fwd(x, pm, of, z, row(sm["b_gate"][l]), wts[("w_pa", l)], wts[("w_pb", l)],
                                        wts[("w_o", l)], name=f"mix_fwd_{l}", plan=plan)
        landed(keys, got)
        keys, plan = riders(l, "up")
        (xn2, h), got = _norm_matmul(x_mid, row(sm["norm2_g"][l]), wts[("w_up", l)], name=f"up_{l}",
                                     out_dtype=BF16, plan=plan)
        landed(keys, got)
        keys, plan = riders(l, "down_fwd")
        (x_out, val16, gate16), got = _ffn_down_fwd(h, conv_w[l], row(sm["conv_b"][l]), wts[("w_down", l)], x_mid,
                                                    name=f"down_fwd_{l}", plan=plan)
        landed(keys, got)
        saved.append(dict(x=x, xn1=xn1, z=z, pm=pm, o_raw=o_raw, of=of, states=states,
                          x_mid=x_mid, ya=ya, yb=yb, xn2=xn2, h=h, val16=val16, gate16=gate16))
        x = x_out

    loss_cols, dx, d_final_g = _final_loss(x, row(sm["final_g"]), target, name="final_loss")

    small = {k: [None] * L for k in ("norm1_g", "b_gate", "pool_w", "pool_scale", "hgrn_norm_g", "norm2_g",
                                     "conv_w", "conv_b")}
    big32, big16, recv = {}, {}, {}
    dlbs = [None] * L
    pending = []

    def scatter(now):
        if not (dist and pending and now):
            return [], None
        keys = list(pending)
        del pending[:]
        return keys, _ScatterPlan([big16[k] for k in keys])

    def sent(keys, outs):
        for key, arr in zip(keys, outs):
            recv[key] = arr

    def made(name, l, g32, g16):
        big32[(name, l)], big16[(name, l)] = g32, g16
        pending.append((name, l))

    for l in reversed(range(L)):
        s = saved[l]
        last = l == 0
        keys, plan = scatter(True)
        (dhv, dhg, d_wd, d_wd16, dcwv, dcwg, dcbv, dcbg), got = _ffn_down_bwd(
            dx, s["h"], s["val16"], s["gate16"], conv_w[l], wts[("w_down", l)], name=f"down_bwd_{l}", plan=plan)
        sent(keys, got)
        made("w_down", l, d_wd, d_wd16)
        small["conv_w"][l] = jnp.concatenate([dcwv, dcwg], axis=1)
        small["conv_b"][l] = jnp.concatenate([dcbv, dcbg], axis=1)[0]
        keys, plan = scatter(last)
        (d_wu, d_wu16), got = _wgrad(s["xn2"], [dhv, dhg], name=f"up_wgrad_{l}", rows=2048, plan=plan)
        sent(keys, got)
        made("w_up", l, d_wu, d_wu16)
        (dxm, dg2), _ = _dgrad_norm([dhv, dhg], wts[("w_up", l)], s["x_mid"], row(sm["norm2_g"][l]), dx,
                                    name=f"up_dgrad_{l}")
        small["norm2_g"][l] = dg2[0]

        keys, plan = scatter(last)
        (dzg, dpm, dof, d_wo, d_wpa, d_wpb, dbg, d_wo16, d_wpa16, d_wpb16), got = _mix_bwd(
            dxm, s["ya"], s["yb"], s["z"], row(sm["b_gate"][l]), s["pm"], s["of"],
            wts[("w_pa", l)], wts[("w_pb", l)], wts[("w_o", l)], name=f"mix_bwd_{l}", plan=plan)
        sent(keys, got)
        made("w_o", l, d_wo, d_wo16)
        made("w_pa", l, d_wpa, d_wpa16)
        made("w_pb", l, d_wpb, d_wpb16)
        small["b_gate"][l] = dbg[0]

        du, dpw, dps = _pool_bwd(s["z"], dpm, sm["pool_w"][l], row(sm["pool_scale"][l]), name=f"pool_bwd_{l}")
        small["pool_w"][l], small["pool_scale"][l] = dpw, dps[0]

        keys, plan = scatter(last)
        (dzq, dzf, dzi, dzo, dlb, dng), got = _hgrn_bwd(s["z"], row(lbs[l]), row(sm["hgrn_norm_g"][l]), s["o_raw"],
                                                      s["states"], dof, name=f"hgrn_bwd_{l}", plan=plan)
        sent(keys, got)
        dlbs[l] = dlb[0]
        small["hgrn_norm_g"][l] = jnp.sum(dng.reshape(-1, LANES), axis=0)

        dz = [du, dzq, dzf, dzi, dzo, dzg]
        (d_wi, d_wi16), _ = _wgrad(s["xn1"], dz, name=f"in_wgrad_{l}", rows=1024)
        made("w_in", l, d_wi, d_wi16)
        keys, plan = scatter(last)
        (dx, dg1), got = _dgrad_norm(dz, wts[("w_in", l)], s["x"], row(sm["norm1_g"][l]), dxm,
                                     name=f"in_dgrad_{l}", plan=plan)
        sent(keys, got)
        small["norm1_g"][l] = dg1[0]

    out = {k: jnp.stack(v) for k, v in small.items()}
    out["lb_logits"] = lb_vjp(jnp.stack(dlbs))[0]
    out["final_g"] = d_final_g[0]
    return loss_cols, dx, out, big32, recv


def _elementwise_rows(R, n, n_arrays):
    if 2 * n_arrays * R * n * 4 <= VMEM_LIMIT // 4 or R % 8:
        return R
    want = 8
    while want * 2 * n * 4 <= 1024 * 1024:
        want *= 2
    return _row_tile(R, want)


def _sum_layers(own, got, chip, *, name):
    L = len(own)
    _, r, n = own[0].shape
    T = _elementwise_rows(r, n, 6)
    nt = r // T

    def body(chip_ref, *refs):
        o_ref = refs[-1]
        l = pl.program_id(0)
        for k in range(L):
            @pl.when(l == k)
            def _():
                own_ref, got_ref = refs[2 * k], refs[2 * k + 1]
                acc = own_ref[...]
                for j in range(3):
                    acc = acc + got_ref[j].astype(F32)
                o_ref[...] = acc

    in_specs = []
    for k in range(L):
        hold = 0 if k else nt - 1
        in_specs.append(pl.BlockSpec((None, T, n), lambda l, i, c, k=k, hold=hold: (c[0], jnp.where(l == k, i, hold), 0)))
        in_specs.append(pl.BlockSpec((3, T, n), lambda l, i, c, k=k, hold=hold: (0, jnp.where(l == k, i, hold), 0)))
    grid_spec = pltpu.PrefetchScalarGridSpec(
        num_scalar_prefetch=1, grid=(L, nt), in_specs=in_specs,
        out_specs=pl.BlockSpec((None, T, n), lambda l, i, c: (l, i, 0)))
    args = [a for pair in zip(own, got) for a in pair]
    return pl.pallas_call(
        body, name=name, grid_spec=grid_spec, out_shape=jax.ShapeDtypeStruct((L, r, n), F32),
        compiler_params=pltpu.CompilerParams(dimension_semantics=("arbitrary", "arbitrary"),
                                             vmem_limit_bytes=VMEM_LIMIT),
    )(chip, *args)


def _sum_stack(first, rest, *, name):
    R, n = first.shape
    K = rest.shape[0]
    T = _elementwise_rows(R, n, K + 2)

    def body(a_ref, r_ref, o_ref):
        acc = a_ref[...]
        for j in range(K):
            acc = acc + r_ref[j].astype(F32)
        o_ref[...] = acc

    return _call(
        body, name=name, grid=(R // T,), parallel=(0,),
        in_specs=[pl.BlockSpec((T, n), lambda i: (i, 0)), pl.BlockSpec((K, T, n), lambda i: (0, i, 0))],
        out_specs=[pl.BlockSpec((T, n), lambda i: (i, 0))],
        out_shape=[jax.ShapeDtypeStruct((R, n), F32)],
        args=(first, rest))[0][0]


def _adamw(w, m, v, g_parts, *, name):
    R, n = w.shape
    n_g = len(g_parts)
    T = _elementwise_rows(R, n, 7 + n_g)

    def body(*refs):
        w_ref, m_ref, v_ref = refs[:3]
        g_refs = refs[3:3 + n_g]
        go_ref, d_ref, mo_ref, vo_ref = refs[3 + n_g:]
        g_ = g_refs[0][...]
        for r in g_refs[1:]:
            g_ = g_ + r[...]
        m_ = ADAM_B1 * m_ref[...] + (1.0 - ADAM_B1) * g_
        v_ = ADAM_B2 * v_ref[...] + (1.0 - ADAM_B2) * (g_ * g_)
        m_hat = m_ / (1.0 - ADAM_B1 ** ADAM_STEP)
        v_hat = v_ / (1.0 - ADAM_B2 ** ADAM_STEP)
        go_ref[...] = g_
        d_ref[...] = -ADAM_LR * (m_hat / (jnp.sqrt(v_hat) + ADAM_EPS) + ADAM_WD * w_ref[...])
        mo_ref[...] = m_
        vo_ref[...] = v_

    blk = pl.BlockSpec((T, n), lambda i: (i, 0))
    return _call(
        body, name=name, grid=(R // T,), parallel=(0,),
        in_specs=[blk] * (3 + n_g), out_specs=[blk] * 4,
        out_shape=[jax.ShapeDtypeStruct((R, n), F32)] * 4,
        args=(w, m, v, *g_parts))[0]


PACK_ALIGN = 8 * LANES


def _pack(pieces):
    flat = []
    for a in pieces:
        a = a.reshape(-1)
        pad = (-a.shape[0]) % PACK_ALIGN
        flat.append(jnp.pad(a, (0, pad)) if pad else a)
    return jnp.concatenate(flat).reshape(-1, LANES)


def _unpack(buf, shapes):
    flat = buf.reshape(-1)
    out, off = [], 0
    for shp in shapes:
        size = 1
        for s in shp:
            size *= s
        out.append(flat[off:off + size].reshape(shp))
        off += size + (-size) % PACK_ALIGN
    return out


def kernel(x, norm1_g, w_in, b_gate, pool_w, pool_scale, lb_logits, hgrn_norm_g, w_pa, w_pb, w_o, norm2_g, w_up, conv_w, conv_b, w_down, final_g, loss_target, m_norm1_g, m_w_in, m_b_gate, m_pool_w, m_pool_scale, m_lb_logits, m_hgrn_norm_g, m_w_pa, m_w_pb, m_w_o, m_norm2_g, m_w_up, m_conv_w, m_conv_b, m_w_down, m_final_g, v_norm1_g, v_w_in, v_b_gate, v_pool_w, v_pool_scale, v_lb_logits, v_hgrn_norm_g, v_w_pa, v_w_pb, v_w_o, v_norm2_g, v_w_up, v_conv_w, v_conv_b, v_w_down, v_final_g):
    env = dict(locals())
    w = {n: env[n] for n in WEIGHTS}
    m = {n: env["m_" + n] for n in WEIGHTS}
    v = {n: env["v_" + n] for n in WEIGHTS}
    my_chip = 2 * lax.axis_index("x") + lax.axis_index("y")
    L = w_in.shape[0]

    shards = {n: w[n].astype(BF16) for n in BIG}
    shards["conv_w"] = w["conv_w"]
    w_in0 = _run_plan(_GatherPlan([(shards["w_in"], "rows", 0)]), name="gather_w_in0")[0]
    sm = {n: w[n] for n in SMALL}
    loss_cols, grad_x, g_small, big32, recv = _step(x[0], loss_target[0], sm, {("w_in", 0): w_in0}, shards)

    chip = my_chip.reshape(1).astype(jnp.int32)
    sums = [_sum_layers([big32[(n, l)] for l in range(L)], [recv[(n, l)] for l in range(L)], chip,
                        name="chip_sum_" + n) for n in BIG]
    theirs = _run_plan(_SiblingPlan(sums), name="grad_sibling")
    g, delta, new_m, new_v = {}, {}, {}, {}
    for n, mine, other in zip(BIG, sums, theirs):
        shp = w[n].shape
        two_d = lambda a: a.reshape(-1, shp[-1])
        outs = _adamw(two_d(w[n]), two_d(m[n]), two_d(v[n]), [two_d(mine), two_d(other)], name="adamw_" + n)
        g[n], delta[n], new_m[n], new_v[n] = [a.reshape(shp) for a in outs]

    small_names = list(SMALL)
    small_pieces = [g_small[n] for n in small_names] + [g_small["conv_w"], loss_cols]
    small_shapes = [a.shape for a in small_pieces]
    packed = _pack(small_pieces)
    Rs = packed.shape[0]
    everyone = _run_plan(_EveryonePlan(packed), name="gather_small")[0].reshape(8, Rs, LANES)
    summed = _unpack(_sum_stack(everyone[0], everyone[1:], name="small_sum"), small_shapes)
    loss = jnp.sum(summed[-1])
    cshard = w["conv_w"].shape[2]
    gs = dict(zip(small_names, summed[:len(small_names)]))
    g_cw = lax.dynamic_slice_in_dim(summed[-2], my_chip * cshard, cshard, axis=2)

    sm_out = _adamw(_pack([w[n] for n in small_names]), _pack([m[n] for n in small_names]),
                    _pack([v[n] for n in small_names]), [_pack([gs[n] for n in small_names])], name="adamw_small")
    shapes = [w[n].shape for n in small_names]
    for n, g_, d_, m_, v_ in zip(small_names, *[_unpack(a, shapes) for a in sm_out]):
        g[n], delta[n], new_m[n], new_v[n] = g_, d_, m_, v_
    shp = w["conv_w"].shape
    two_d = lambda a: a.reshape(-1, shp[-1])
    outs = _adamw(two_d(w["conv_w"]), two_d(m["conv_w"]), two_d(v["conv_w"]), [two_d(g_cw)], name="adamw_conv_w")
    g["conv_w"], delta["conv_w"], new_m["conv_w"], new_v["conv_w"] = [a.reshape(shp) for a in outs]

    return (loss, grad_x[None], *[g[n] for n in WEIGHTS], *[delta[n] for n in WEIGHTS],
            *[new_m[n] for n in WEIGHTS], *[new_v[n] for n in WEIGHTS])
```

```python
import jax
import jax.numpy as jnp
from jax import lax
from jax.experimental import pallas as pl
from jax.experimental.pallas import tpu as pltpu

F32 = jnp.float32
BF16 = jnp.bfloat16

EPS = 1e-6
CHUNK = 64
SUB = 32
LANES = 128
SUBLANES = 8
POOL_WINDOWS = (2, 4, 8, 16)
HALO_POOL = 16
HALO_CONV = 16
EXP_CLAMP = 80.0

ADAM_LR = 0.001
ADAM_B1 = 0.9
ADAM_B2 = 0.999
ADAM_EPS = 1e-08
ADAM_WD = 0.01
ADAM_STEP = 10

VMEM_LIMIT = 56 * 1024 * 1024
MESH_ID = pl.DeviceIdType.MESH
N_CHIPS = 4
ANY = pl.BlockSpec(memory_space=pl.ANY)


def _dot(a, b):
    return jnp.dot(a, b, preferred_element_type=F32)


def _dot_nt(a, b):
    return lax.dot_general(a, b, (((1,), (1,)), ((), ())), preferred_element_type=F32)


def _dot_tn(a, b):
    return lax.dot_general(a, b, (((0,), (0,)), ((), ())), preferred_element_type=F32)


def _sigmoid(x):
    return jax.nn.sigmoid(x)


def _dsilu(x, s):
    return s * (1.0 + x * (1.0 - s))


def _row_tile(rows, want):
    t = min(rows, want)
    while rows % t:
        t //= 2
    return t


def _place():
    x, y, c = lax.axis_index("x"), lax.axis_index("y"), lax.axis_index("c")
    chips = [(1 - x, y), (x, 1 - y), (1 - x, 1 - y)]
    return x, y, c, chips


def _remote(src, dst, sems, k, to):
    return pltpu.make_async_remote_copy(src_ref=src, dst_ref=dst, send_sem=sems[0].at[k], recv_sem=sems[1].at[k],
                                        device_id=to, device_id_type=MESH_ID)


class _GatherPlan:
    def __init__(self, items):
        self.items = items
        self.inputs = [a for a, _, _ in items]
        self.out_shapes = []
        for a, kind, _ in items:
            shp = (N_CHIPS,) + a.shape[1:] if kind == "rows" else (a.shape[0], N_CHIPS) + a.shape[1:]
            self.out_shapes.append(jax.ShapeDtypeStruct(shp, a.dtype))
        n = len(items)
        self.scratch = [pltpu.SemaphoreType.DMA((6 * n,)), pltpu.SemaphoreType.DMA((6 * n,)),
                        pltpu.SemaphoreType.DMA((2 * n,))]

    def _views(self, i, src, dst):
        _, kind, l = self.items[i]
        if kind == "rows":
            half = src.shape[1] // 2
            part = lambda core: src.at[l, pl.ds(core * half, half), :]
            land = lambda chip, core: dst.at[chip, pl.ds(core * half, half), :]
        else:
            part = lambda core: src.at[core]
            land = lambda chip, core: dst.at[core, chip]
        return part, land

    def start(self, srcs, dsts, sems):
        x, y, c, chips = _place()
        me = 2 * x + y
        for i, (src, dst) in enumerate(zip(srcs, dsts)):
            part, land = self._views(i, src, dst)
            for core in range(2):
                pltpu.make_async_copy(part(core), land(me, core), sems[2].at[2 * i + core]).start()
            for j, (px, py) in enumerate(chips):
                _remote(part(c), land(me, c), sems, 6 * i + j, (px, py, c)).start()

    def finish(self, srcs, dsts, sems):
        x, y, c, chips = _place()
        me = 2 * x + y
        sibling = (x, y, 1 - c)
        for i, (src, dst) in enumerate(zip(srcs, dsts)):
            part, land = self._views(i, src, dst)
            for j, (px, py) in enumerate(chips):
                got = land(2 * px + py, c)
                _remote(got, got, sems, 6 * i + j, (px, py, c)).wait_recv()
                _remote(got, got, sems, 6 * i + 3 + j, sibling).start()
        for i, (src, dst) in enumerate(zip(srcs, dsts)):
            part, land = self._views(i, src, dst)
            for j, (px, py) in enumerate(chips):
                got = land(2 * px + py, 1 - c)
                _remote(got, got, sems, 6 * i + 3 + j, sibling).wait_recv()
            for j, (px, py) in enumerate(chips):
                _remote(part(c), land(me, c), sems, 6 * i + j, (px, py, c)).wait_send()
                mine = land(2 * px + py, c)
                _remote(mine, mine, sems, 6 * i + 3 + j, sibling).wait_send()
            for core in range(2):
                pltpu.make_async_copy(part(core), land(me, core), sems[2].at[2 * i + core]).wait()


class _ScatterPlan:
    def __init__(self, items):
        self.inputs = list(items)
        self.out_shapes = [jax.ShapeDtypeStruct((3,) + a.shape[1:], a.dtype) for a in items]
        n = len(items)
        self.scratch = [pltpu.SemaphoreType.DMA((3 * n,)), pltpu.SemaphoreType.DMA((3 * n,))]

    def _copies(self, srcs, dsts, sems):
        x, y, c, chips = _place()
        return [_remote(src.at[2 * px + py], dst.at[j], sems, 3 * i + j, (px, py, c))
                for i, (src, dst) in enumerate(zip(srcs, dsts)) for j, (px, py) in enumerate(chips)]

    def start(self, srcs, dsts, sems):
        for cp in self._copies(srcs, dsts, sems):
            cp.start()

    def finish(self, srcs, dsts, sems):
        copies = self._copies(srcs, dsts, sems)
        for cp in copies:
            cp.wait_recv()
        for cp in copies:
            cp.wait_send()


class _SiblingPlan:
    def __init__(self, items):
        self.inputs = list(items)
        self.out_shapes = [jax.ShapeDtypeStruct(a.shape, a.dtype) for a in items]
        n = len(items)
        self.scratch = [pltpu.SemaphoreType.DMA((n,)), pltpu.SemaphoreType.DMA((n,))]

    def _copies(self, srcs, dsts, sems):
        x, y, c, _ = _place()
        return [_remote(src, dst, sems, i, (x, y, 1 - c)) for i, (src, dst) in enumerate(zip(srcs, dsts))]

    def start(self, srcs, dsts, sems):
        for cp in self._copies(srcs, dsts, sems):
            cp.start()

    def finish(self, srcs, dsts, sems):
        copies = self._copies(srcs, dsts, sems)
        for cp in copies:
            cp.wait_recv()
        for cp in copies:
            cp.wait_send()


class _EveryonePlan:
    def __init__(self, block):
        self.inputs = [block]
        self.m = block.shape[0]
        self.out_shapes = [jax.ShapeDtypeStruct((8 * self.m,) + block.shape[1:], block.dtype)]
        self.scratch = [pltpu.SemaphoreType.DMA((7,)), pltpu.SemaphoreType.DMA((7,)), pltpu.SemaphoreType.DMA((1,))]

    def _rows(self, dst, px, py, pc):
        return dst.at[pl.ds((4 * px + 2 * py + pc) * self.m, self.m), :]

    def start(self, srcs, dsts, sems):
        x, y, c, chips = _place()
        src, dst = srcs[0], dsts[0]
        pltpu.make_async_copy(src, self._rows(dst, x, y, c), sems[2].at[0]).start()
        _remote(src, self._rows(dst, x, y, c), sems, 0, (x, y, 1 - c)).start()
        for j, (px, py) in enumerate(chips):
            _remote(src, self._rows(dst, x, y, c), sems, 1 + j, (px, py, c)).start()

    def finish(self, srcs, dsts, sems):
        x, y, c, chips = _place()
        src, dst = srcs[0], dsts[0]
        sibling = (x, y, 1 - c)
        for j, (px, py) in enumerate(chips):
            got = self._rows(dst, px, py, c)
            _remote(got, got, sems, 1 + j, (px, py, c)).wait_recv()
            _remote(got, got, sems, 4 + j, sibling).start()
        sib = self._rows(dst, x, y, 1 - c)
        _remote(sib, sib, sems, 0, sibling).wait_recv()
        for j, (px, py) in enumerate(chips):
            got = self._rows(dst, px, py, 1 - c)
            _remote(got, got, sems, 4 + j, sibling).wait_recv()
        mine = self._rows(dst, x, y, c)
        _remote(src, mine, sems, 0, sibling).wait_send()
        for j, (px, py) in enumerate(chips):
            _remote(src, mine, sems, 1 + j, (px, py, c)).wait_send()
            got = self._rows(dst, px, py, c)
            _remote(got, got, sems, 4 + j, sibling).wait_send()
        pltpu.make_async_copy(src, mine, sems[2].at[0]).wait()


def _call(body, *, name, grid, in_specs, out_specs, out_shape, args, scratch=(), parallel=(), plan=None):
    n_in, n_out, n_scr = len(in_specs), len(out_shape), len(scratch)
    sem = tuple("parallel" if (a in parallel and plan is None) else "arbitrary" for a in range(len(grid)))
    params = pltpu.CompilerParams(dimension_semantics=sem, vmem_limit_bytes=VMEM_LIMIT)
    if plan is None:
        outs = pl.pallas_call(body, name=name, grid=grid, in_specs=in_specs, out_specs=out_specs,
                              out_shape=out_shape, scratch_shapes=list(scratch), compiler_params=params)(*args)
        return list(outs), []
    p_in, p_out, p_scr = len(plan.inputs), len(plan.out_shapes), len(plan.scratch)

    def wrapped(*refs):
        ins, refs = refs[:n_in], refs[n_in:]
        p_ins, refs = refs[:p_in], refs[p_in:]
        outs, refs = refs[:n_out], refs[n_out:]
        p_outs, refs = refs[:p_out], refs[p_out:]
        scr, p_sems = refs[:n_scr], refs[n_scr:]
        ids = [pl.program_id(a) for a in range(len(grid))]
        first = _all([i == 0 for i in ids])
        last = _all([i == n - 1 for i, n in zip(ids, grid)])

        @pl.when(first)
        def _():
            plan.start(p_ins, p_outs, p_sems)

        body(*ins, *outs, *scr)

        @pl.when(last)
        def _():
            plan.finish(p_ins, p_outs, p_sems)

    outs = pl.pallas_call(
        wrapped, name=name, grid=grid,
        in_specs=list(in_specs) + [ANY] * p_in, out_specs=list(out_specs) + [ANY] * p_out,
        out_shape=list(out_shape) + list(plan.out_shapes),
        scratch_shapes=list(scratch) + list(plan.scratch), compiler_params=params,
    )(*args, *plan.inputs)
    return list(outs[:n_out]), list(outs[n_out:])


def _all(conds):
    out = conds[0]
    for c in conds[1:]:
        out = out & c
    return out


class _Together:
    def __init__(self, plans):
        self.plans = plans
        self.inputs = [a for p in plans for a in p.inputs]
        self.out_shapes = [s for p in plans for s in p.out_shapes]
        self.scratch = [s for p in plans for s in p.scratch]

    def _split(self, refs, count):
        out, at = [], 0
        for p in self.plans:
            out.append(refs[at:at + count(p)])
            at += count(p)
        return out

    def _parts(self, srcs, dsts, sems):
        return zip(self.plans, self._split(srcs, lambda p: len(p.inputs)),
                   self._split(dsts, lambda p: len(p.out_shapes)), self._split(sems, lambda p: len(p.scratch)))

    def start(self, srcs, dsts, sems):
        for p, s, d, m in self._parts(srcs, dsts, sems):
            p.start(s, d, m)

    def finish(self, srcs, dsts, sems):
        for p, s, d, m in self._parts(srcs, dsts, sems):
            p.finish(s, d, m)


def _run_plan(plan, *, name):
    p_in, p_out = len(plan.inputs), len(plan.out_shapes)

    def body(*refs):
        srcs, dsts, sems = refs[:p_in], refs[p_in:p_in + p_out], refs[p_in + p_out:]
        plan.start(srcs, dsts, sems)
        plan.finish(srcs, dsts, sems)

    return list(pl.pallas_call(body, name=name, in_specs=[ANY] * p_in, out_specs=[ANY] * p_out,
                               out_shape=list(plan.out_shapes), scratch_shapes=list(plan.scratch))(*plan.inputs))


def _norm_matmul(x, g, w4, *, name, out_dtype=F32, plan=None):
    S, D = x.shape
    n4 = w4.shape[2]
    tm = _row_tile(S, 1024)

    def body(x_ref, g_ref, w_ref, xn_ref, o_ref):
        @pl.when(pl.program_id(1) == 0)
        def _():
            xf = x_ref[...]
            r = lax.rsqrt(jnp.mean(xf * xf, axis=-1, keepdims=True) + EPS)
            xn_ref[...] = (xf * r * g_ref[...]).astype(BF16)

        o_ref[...] = _dot(xn_ref[...], w_ref[...]).astype(out_dtype)

    return _call(
        body, name=name, grid=(S // tm, N_CHIPS), parallel=(0,), plan=plan,
        in_specs=[pl.BlockSpec((tm, D), lambda i, j: (i, 0)),
                  pl.BlockSpec((1, D), lambda i, j: (0, 0)),
                  pl.BlockSpec((None, D, n4), lambda i, j: (j, 0, 0))],
        out_specs=[pl.BlockSpec((tm, D), lambda i, j: (i, 0)),
                   pl.BlockSpec((tm, n4), lambda i, j: (i, j))],
        out_shape=[jax.ShapeDtypeStruct((S, D), BF16), jax.ShapeDtypeStruct((S, N_CHIPS * n4), out_dtype)],
        args=(x, g, w4))


def _segments(widths, n4):
    per_chip = [[] for _ in range(N_CHIPS)]
    c0 = 0
    for p, w in enumerate(widths):
        a = c0
        while a < c0 + w:
            k = a // n4
            b = min(c0 + w, (k + 1) * n4)
            per_chip[k].append((p, (a - c0, b - c0), (a - k * n4, b - k * n4)))
            a = b
        c0 += w
    assert c0 == N_CHIPS * n4
    return per_chip


def _piece_specs(pieces, n4, tm):
    per_chip = _segments([p.shape[1] for p in pieces], n4)
    specs, local, start = [], [[] for _ in range(N_CHIPS)], 0
    for p, arr in enumerate(pieces):
        chips = [k for k in range(N_CHIPS) if any(seg[0] == p for seg in per_chip[k])]
        lo, hi = chips[0], chips[-1]
        tiled = arr.shape[1] % n4 == 0 and start % n4 == 0
        start += arr.shape[1]
        if tiled:
            imap = lambda k, i, lo=lo, hi=hi: (jnp.where((k >= lo) & (k <= hi), i, 0), jnp.clip(k - lo, 0, hi - lo))
            specs.append(pl.BlockSpec((tm, n4), imap))
        else:
            imap = lambda k, i, lo=lo, hi=hi: (jnp.where((k >= lo) & (k <= hi), i, 0), 0)
            specs.append(pl.BlockSpec((tm, arr.shape[1]), imap))
        for k in chips:
            for q, (pa, pb), cols in per_chip[k]:
                if q == p:
                    local[k].append((p, (0, n4) if tiled else (pa, pb), cols))
    return specs, local


def _dgrad_norm(dys, w4, x, g, dres, *, name, plan=None):
    S, D = x.shape
    n4 = w4.shape[2]
    tm = _row_tile(S, 512)
    per_chip = _segments([a.shape[1] for a in dys], n4)
    n_p = len(dys)

    def body(*refs):
        dy_refs = refs[:n_p]
        w_ref, x_ref, g_ref, dres_ref, dx_ref, dg_ref = refs[n_p:]

        @pl.when(pl.program_id(0) == 0)
        def _():
            dg_ref[...] = jnp.zeros_like(dg_ref)

        dxn = None
        for k in range(N_CHIPS):
            for p, (pa, pb), (ca, cb) in per_chip[k]:
                part = _dot_nt(dy_refs[p][:, pa:pb], w_ref[k, :, ca:cb])
                dxn = part if dxn is None else dxn + part
        xf = x_ref[...]
        r = lax.rsqrt(jnp.mean(xf * xf, axis=-1, keepdims=True) + EPS)
        xhat = xf * r
        dxhat = dxn * g_ref[...]
        dx_ref[...] = dres_ref[...] + r * (dxhat - xhat * jnp.mean(dxhat * xhat, axis=-1, keepdims=True))
        dg_ref[...] += jnp.sum(dxn * xhat, axis=0, keepdims=True)

    row = lambda w: pl.BlockSpec((tm, w), lambda i: (i, 0))
    return _call(
        body, name=name, grid=(S // tm,), plan=plan,
        in_specs=[row(a.shape[1]) for a in dys]
        + [pl.BlockSpec(w4.shape, lambda i: (0, 0, 0), pipeline_mode=pl.Buffered(1)),
           row(D), pl.BlockSpec((1, D), lambda i: (0, 0)), row(D)],
        out_specs=[row(D), pl.BlockSpec((1, D), lambda i: (0, 0))],
        out_shape=[jax.ShapeDtypeStruct((S, D), F32), jax.ShapeDtypeStruct((1, D), F32)],
        args=(*dys, w4, x, g, dres))


def _wgrad(a, dys, *, name, rows, plan=None):
    S, K = a.shape
    n4 = sum(p.shape[1] for p in dys) // N_CHIPS
    tm = _row_tile(S, rows)
    ns = S // tm
    specs, local = _piece_specs(dys, n4, tm)
    n_p = len(dys)

    def body(*refs):
        a_ref = refs[0]
        dy_refs = refs[1:1 + n_p]
        o_ref, o16_ref = refs[1 + n_p:]
        n, s = pl.program_id(0), pl.program_id(1)

        @pl.when(s == 0)
        def _():
            o_ref[...] = jnp.zeros_like(o_ref)

        for k in range(N_CHIPS):
            @pl.when(n == k)
            def _():
                av = a_ref[...]
                for p, (pa, pb), (ca, cb) in local[k]:
                    o_ref[:, ca:cb] += _dot_tn(av, dy_refs[p][:, pa:pb])

        @pl.when(s == ns - 1)
        def _():
            o16_ref[...] = o_ref[...].astype(BF16)

    out = pl.BlockSpec((None, K, n4), lambda n, s: (n, 0, 0))
    return _call(
        body, name=name, grid=(N_CHIPS, ns), parallel=(0,), plan=plan,
        in_specs=[pl.BlockSpec((tm, K), lambda n, s: (s, 0))] + specs,
        out_specs=[out, out],
        out_shape=[jax.ShapeDtypeStruct((N_CHIPS, K, n4), F32), jax.ShapeDtypeStruct((N_CHIPS, K, n4), BF16)],
        args=(a, *dys))


def _tiles(x):
    return x.reshape(x.shape[0] // SUBLANES, SUBLANES, x.shape[1])


def _shift_down(xp, s):
    n = xp.shape[0] - SUBLANES
    if s == SUBLANES:
        return xp[:n, :]
    t = _tiles(xp)
    rot = pltpu.roll(t, s, 1)
    sub = lax.broadcasted_iota(jnp.int32, t.shape, 1)[1:]
    return jnp.where(sub >= s, rot[1:], rot[:-1]).reshape(n, xp.shape[1])


def _shift_up(xn, s):
    n = xn.shape[0] - SUBLANES
    if s == SUBLANES:
        return xn[SUBLANES:, :]
    t = _tiles(xn)
    rot = pltpu.roll(t, SUBLANES - s, 1)
    sub = lax.broadcasted_iota(jnp.int32, t.shape, 1)[1:]
    return jnp.where(sub < SUBLANES - s, rot[:-1], rot[1:]).reshape(n, xn.shape[1])


def _pooled(u, halo, first_tile, row0):
    T = u.shape[0]
    halo = jnp.where(first_tile, 0.0, halo)
    pad = jnp.zeros((SUBLANES, u.shape[1]), F32)
    up = jnp.concatenate([pad, halo, u], axis=0)
    t1 = (row0 + lax.broadcasted_iota(jnp.int32, (T, 1), 0) + 1).astype(F32)
    outs = []
    for gi, w in enumerate(POOL_WINDOWS):
        s = up[:, gi * LANES:(gi + 1) * LANES]
        k = 1
        while k < w:
            if k < SUBLANES:
                s = jnp.concatenate([s[:SUBLANES, :], s[SUBLANES:, :] + _shift_down(s, k)], axis=0)
            else:
                s = s[SUBLANES:, :] + _shift_down(s, k)
            k *= 2
        s = s[-T:, :]
        inv = 1.0 / jnp.minimum(t1, float(w))
        outs.append(s * inv - u[:, gi * LANES:(gi + 1) * LANES])
    return outs


def _pool_fwd(z, pool_w, pool_scale, *, name):
    S = z.shape[0]
    P = pool_scale.shape[1]
    T = _row_tile(S, 512)
    hb = T // HALO_POOL

    def body(u_ref, halo_ref, pw_ref, ps_ref, o_ref):
        i = pl.program_id(0)
        pooled = _pooled(u_ref[...], halo_ref[...], i == 0, i * T)
        for gi in range(len(POOL_WINDOWS)):
            mixed = _dot(pooled[gi].astype(BF16), pw_ref[gi].astype(BF16))
            cols = slice(gi * LANES, (gi + 1) * LANES)
            o_ref[:, cols] = (mixed * ps_ref[:, cols]).astype(BF16)

    return _call(
        body, name=name, grid=(S // T,), parallel=(0,),
        in_specs=[pl.BlockSpec((T, P), lambda i: (i, 0)),
                  pl.BlockSpec((HALO_POOL, P), lambda i: (jnp.maximum(i * hb - 1, 0), 0)),
                  pl.BlockSpec(pool_w.shape, lambda i: (0, 0, 0)),
                  pl.BlockSpec((1, P), lambda i: (0, 0))],
        out_specs=[pl.BlockSpec((T, P), lambda i: (i, 0))],
        out_shape=[jax.ShapeDtypeStruct((S, P), BF16)],
        args=(z, z, pool_w, pool_scale))[0][0]


def _pool_bwd(z, dpm, pool_w, pool_scale, *, name):
    S, P = dpm.shape
    T = _row_tile(S, 512)
    hb = T // HALO_POOL
    nt = S // T

    def body(u_ref, halo_ref, d_ref, dnext_ref, pw_ref, ps_ref, du_ref, dpw_ref, dps_ref):
        i = pl.program_id(0)

        @pl.when(i == 0)
        def _():
            dpw_ref[...] = jnp.zeros_like(dpw_ref)
            dps_ref[...] = jnp.zeros_like(dps_ref)

        pooled = _pooled(u_ref[...], halo_ref[...], i == 0, i * T)
        dnext = jnp.where(i == nt - 1, 0.0, dnext_ref[...])
        pad = jnp.zeros((SUBLANES, P), F32)
        dext = jnp.concatenate([d_ref[...], dnext, pad], axis=0)
        t1 = (i * T + lax.broadcasted_iota(jnp.int32, (T + HALO_POOL + SUBLANES, 1), 0) + 1).astype(F32)
        for gi, w in enumerate(POOL_WINDOWS):
            cols = slice(gi * LANES, (gi + 1) * LANES)
            pw = pw_ref[gi].astype(BF16)
            pg = pooled[gi].astype(BF16)
            mixed = _dot(pg, pw)
            dps_ref[:, cols] += jnp.sum(d_ref[:, cols] * mixed, axis=0, keepdims=True)
            dmixed = (dext[:, cols] * ps_ref[:, cols]).astype(BF16)
            dpw_ref[gi] += _dot_tn(pg, dmixed[:T, :])
            dpooled = _dot_nt(dmixed, pw)
            e = dpooled * (1.0 / jnp.minimum(t1, float(w)))
            k = 1
            while k < w:
                if k < SUBLANES:
                    e = jnp.concatenate([e[:-SUBLANES, :] + _shift_up(e, k), e[-SUBLANES:, :]], axis=0)
                else:
                    e = e[:-SUBLANES, :] + _shift_up(e, k)
                k *= 2
            du_ref[:, cols] = (e[:T, :] - dpooled[:T, :]).astype(BF16)

    return _call(
        body, name=name, grid=(nt,),
        in_specs=[pl.BlockSpec((T, P), lambda i: (i, 0)),
                  pl.BlockSpec((HALO_POOL, P), lambda i: (jnp.maximum(i * hb - 1, 0), 0)),
                  pl.BlockSpec((T, P), lambda i: (i, 0)),
                  pl.BlockSpec((HALO_POOL, P), lambda i: (jnp.minimum((i + 1) * hb, S // HALO_POOL - 1), 0)),
                  pl.BlockSpec(pool_w.shape, lambda i: (0, 0, 0)),
                  pl.BlockSpec((1, P), lambda i: (0, 0))],
        out_specs=[pl.BlockSpec((T, P), lambda i: (i, 0)),
                   pl.BlockSpec(pool_w.shape, lambda i: (0, 0, 0)),
                   pl.BlockSpec((1, P), lambda i: (0, 0))],
        out_shape=[jax.ShapeDtypeStruct((S, P), BF16),
                   jax.ShapeDtypeStruct(pool_w.shape, F32),
                   jax.ShapeDtypeStruct((1, P), F32)],
        args=(z, z, dpm, dpm, pool_w, pool_scale))[0]


def _cumsum_rows(x):
    n = x.shape[0]
    row = lax.broadcasted_iota(jnp.int32, x.shape, 0)
    s = 1
    while s < n:
        x = x + jnp.where(row >= s, pltpu.roll(x, s, 0), 0.0)
        s *= 2
    return x


def _rev_cumsum_rows(x):
    n = x.shape[0]
    row = lax.broadcasted_iota(jnp.int32, x.shape, 0)
    s = 1
    while s < n:
        x = x + jnp.where(row < n - s, pltpu.roll(x, n - s, 0), 0.0)
        s *= 2
    return x


def _chunk_prep(zq, zf, lb, b_ref):
    n_sub = CHUNK // SUB
    sq = _sigmoid(zq)
    q = zq * sq
    sf = _sigmoid(zf)
    f = lb + (1.0 - lb) * sf
    k = 1.0 - f
    b = _cumsum_rows(jnp.log(f))
    b_ref[...] = b
    shape = (SUB, b.shape[1])
    ends = [jnp.broadcast_to(b_ref[pl.ds(SUB * j + SUB - 1, 1), :], shape) for j in range(n_sub)]
    mids = [jnp.broadcast_to(b_ref[pl.ds(SUB * j + SUB // 2 - 1, 1), :], shape) for j in range(n_sub)]
    own = [b[SUB * j:SUB * (j + 1), :] for j in range(n_sub)]
    m0 = jnp.concatenate(mids, axis=0)
    e1 = jnp.concatenate(ends, axis=0)
    eq = [jnp.exp(jnp.minimum(b - m0, EXP_CLAMP))]
    for d in range(1, n_sub):
        rd = jnp.concatenate([own[j] if j < d else ends[j - d] for j in range(n_sub)], axis=0)
        eq.append(jnp.exp(b - rd))
    ek0 = jnp.exp(jnp.minimum(m0 - b, EXP_CLAMP))
    ek1 = jnp.exp(e1 - b)
    b_last = b_ref[pl.ds(CHUNK - 1, 1), :]
    return dict(q=q, k=k, f=f, sq=sq, sf=sf, b=b, eq=eq, ek0=ek0, ek1=ek1,
                eb=jnp.exp(b), ekl=jnp.exp(b_last - b), el=jnp.exp(b_last))


def _chunk_masks():
    ti = lax.broadcasted_iota(jnp.int32, (CHUNK, CHUNK), 0)
    si = lax.broadcasted_iota(jnp.int32, (CHUNK, CHUNK), 1)
    shift = SUB.bit_length() - 1
    dsub = jnp.right_shift(ti, shift) - jnp.right_shift(si, shift)
    masks = [(dsub == 0) & (si <= ti)]
    masks += [dsub == d for d in range(1, CHUNK // SUB)]
    return masks


def _chunk_attn(p, masks):
    qd = [(p["q"] * e).astype(BF16) for e in p["eq"]]
    k0 = (p["k"] * p["ek0"]).astype(BF16)
    k1 = (p["k"] * p["ek1"]).astype(BF16)
    a = jnp.where(masks[0], _dot_nt(qd[0], k0), 0.0)
    for d in range(1, len(masks)):
        a = jnp.where(masks[d], _dot_nt(qd[d], k1), a)
    return a, qd, k0, k1


def _hgrn_fwd(z, lb, norm_g, *, name, plan=None):
    S = z.shape[0]
    HW = lb.shape[1]
    NH = HW // LANES
    T = _row_tile(S, 512)
    nc = T // CHUNK

    def body(zq_ref, zf_ref, zi_ref, zo_ref, lb_ref, ng_ref, o_ref, of_ref, st_ref, s_scr, b_scr):
        @pl.when(pl.program_id(0) == 0)
        def _():
            s_scr[...] = jnp.zeros_like(s_scr)

        ng = ng_ref[...]
        masks = _chunk_masks()

        def chunk(c, carry):
            rows = pl.ds(pl.multiple_of(c * CHUNK, CHUNK), CHUNK)
            for h in range(NH):
                cols = slice(h * LANES, (h + 1) * LANES)
                p = _chunk_prep(zq_ref[rows, cols], zf_ref[rows, cols], lb_ref[:, cols], b_scr.at[h])
                v = zi_ref[rows, cols].astype(BF16)
                zo = zo_ref[rows, cols]
                st = s_scr[h]
                st_ref[c, h] = st
                a, _, _, _ = _chunk_attn(p, masks)
                o = _dot(a.astype(BF16), v) + _dot_nt((p["q"] * p["eb"]).astype(BF16), st.astype(BF16))
                s_scr[h] = st * p["el"] + _dot_tn(v, (p["k"] * p["ekl"]).astype(BF16))
                o_ref[rows, cols] = o
                r = lax.rsqrt(jnp.mean(o * o, axis=-1, keepdims=True) + EPS)
                of_ref[rows, cols] = (o * r * ng * (zo * _sigmoid(zo))).astype(BF16)
            return carry

        lax.fori_loop(0, nc, chunk, 0, unroll=4)

    part = lambda k: pl.BlockSpec((T, HW), lambda i, k=k: (i, k))
    return _call(
        body, name=name, grid=(S // T,), plan=plan,
        in_specs=[part(1), part(2), part(3), part(4),
                  pl.BlockSpec((1, HW), lambda i: (0, 0)), pl.BlockSpec((1, LANES), lambda i: (0, 0))],
        out_specs=[pl.BlockSpec((T, HW), lambda i: (i, 0)), pl.BlockSpec((T, HW), lambda i: (i, 0)),
                   pl.BlockSpec((nc, NH, LANES, LANES), lambda i: (i, 0, 0, 0))],
        out_shape=[jax.ShapeDtypeStruct((S, HW), F32), jax.ShapeDtypeStruct((S, HW), BF16),
                   jax.ShapeDtypeStruct((S // CHUNK, NH, LANES, LANES), F32)],
        scratch=[pltpu.VMEM((NH, LANES, LANES), F32), pltpu.VMEM((NH, CHUNK, LANES), F32)],
        args=(z, z, z, z, lb, norm_g))


def _hgrn_bwd(z, lb, norm_g, o_raw, states, dof, *, name, plan=None):
    S = z.shape[0]
    HW = lb.shape[1]
    NH = HW // LANES
    T = _row_tile(S, 512)
    nc = T // CHUNK
    nt = S // T

    def body(zq_ref, zf_ref, zi_ref, zo_ref, lb_ref, ng_ref, o_ref, st_ref, dof_ref,
             dzq_ref, dzf_ref, dzi_ref, dzo_ref, dlb_ref, dng_ref, ds_scr, b_scr):
        @pl.when(pl.program_id(0) == 0)
        def _():
            ds_scr[...] = jnp.zeros_like(ds_scr)
            dlb_ref[...] = jnp.zeros_like(dlb_ref)
            dng_ref[...] = jnp.zeros_like(dng_ref)

        ng = ng_ref[...]
        masks = _chunk_masks()
        last_row = lax.broadcasted_iota(jnp.int32, (CHUNK, 1), 0) == CHUNK - 1

        def chunk(cr, carry):
            c = nc - 1 - cr
            rows = pl.ds(pl.multiple_of(c * CHUNK, CHUNK), CHUNK)
            for h in range(NH):
                cols = slice(h * LANES, (h + 1) * LANES)
                lbv = lb_ref[:, cols]
                zq, zf, zo = zq_ref[rows, cols], zf_ref[rows, cols], zo_ref[rows, cols]
                o = o_ref[rows, cols]
                dof_c = dof_ref[rows, cols]
                st = st_ref[c, h]
                dst = ds_scr[h]

                so = _sigmoid(zo)
                r = lax.rsqrt(jnp.mean(o * o, axis=-1, keepdims=True) + EPS)
                ohat = o * r
                d_on = dof_c * (zo * so)
                dzo_ref[rows, cols] = (dof_c * ohat * ng * _dsilu(zo, so)).astype(BF16)
                dng_ref[:, cols] += jnp.sum(d_on * ohat, axis=0, keepdims=True)
                dohat = d_on * ng
                do = (r * (dohat - ohat * jnp.mean(dohat * ohat, axis=-1, keepdims=True))).astype(BF16)

                p = _chunk_prep(zq, zf, lbv, b_scr.at[h])
                q, k = p["q"], p["k"]
                v = zi_ref[rows, cols].astype(BF16)
                a, qd, k0, k1 = _chunk_attn(p, masks)
                ktl = (k * p["ekl"]).astype(BF16)
                dstb = dst.astype(BF16)

                da = _dot_nt(do, v)
                dzi_ref[rows, cols] = (_dot_tn(a.astype(BF16), do) + _dot_nt(ktl, dstb)).astype(BF16)

                da0 = jnp.where(masks[0], da, 0.0).astype(BF16)
                rq = _dot(da0, k0)
                rk0 = _dot_tn(da0, qd[0])
                dq = rq * p["eq"][0]
                db = qd[0].astype(F32) * rq - k0.astype(F32) * rk0
                rk1 = jnp.zeros_like(rk0)
                for d in range(1, len(masks)):
                    dad = jnp.where(masks[d], da, 0.0).astype(BF16)
                    rq = _dot(dad, k1)
                    dq = dq + rq * p["eq"][d]
                    db = db + qd[d].astype(F32) * rq
                    rk1 = rk1 + _dot_tn(dad, qd[d])
                dk = rk0 * p["ek0"] + rk1 * p["ek1"]
                db = db - k1.astype(F32) * rk1
                qe = (q * p["eb"]).astype(BF16)
                rq = _dot(do, st.astype(BF16))
                dq = dq + rq * p["eb"]
                db = db + qe.astype(F32) * rq
                rk = _dot(v, dstb)
                dk = dk + rk * p["ekl"]
                db = db - ktl.astype(F32) * rk

                st_new = st * p["el"] + _dot_tn(v, ktl)
                db = db + jnp.where(last_row, jnp.sum(dstb.astype(F32) * st_new, axis=0, keepdims=True), 0.0)
                dg = _rev_cumsum_rows(db)
                ds_scr[h] = dst * p["el"] + _dot_tn(do, qe)

                dzq_ref[rows, cols] = (dq * _dsilu(zq, p["sq"])).astype(BF16)
                df = dg / p["f"] - dk
                sf = p["sf"]
                dzf_ref[rows, cols] = (df * (1.0 - lbv) * sf * (1.0 - sf)).astype(BF16)
                dlb_ref[:, cols] += jnp.sum(df * (1.0 - sf), axis=0, keepdims=True)
            return carry

        lax.fori_loop(0, nc, chunk, 0, unroll=4)

    rev = lambda i: nt - 1 - i
    part = lambda k: pl.BlockSpec((T, HW), lambda i, k=k: (rev(i), k))
    blk = pl.BlockSpec((T, HW), lambda i: (rev(i), 0))
    vec = pl.BlockSpec((1, HW), lambda i: (0, 0))
    return _call(
        body, name=name, grid=(nt,), plan=plan,
        in_specs=[part(1), part(2), part(3), part(4), vec, pl.BlockSpec((1, LANES), lambda i: (0, 0)),
                  blk, pl.BlockSpec((nc, NH, LANES, LANES), lambda i: (rev(i), 0, 0, 0)), blk],
        out_specs=[blk, blk, blk, blk, vec, vec],
        out_shape=[jax.ShapeDtypeStruct((S, HW), BF16)] * 4 + [jax.ShapeDtypeStruct((1, HW), F32)] * 2,
        scratch=[pltpu.VMEM((NH, LANES, LANES), F32), pltpu.VMEM((NH, CHUNK, LANES), F32)],
        args=(z, z, z, z, lb, norm_g, o_raw, states, dof))


def _gate_specs(T, D):
    half = D // 2
    first = (5 * half) // half
    return [pl.BlockSpec((T, half), lambda i, k=k: (i, first + k)) for k in range(4)]


def _gates(zg_refs, bg_ref, D):
    half = D // 2
    za = jnp.concatenate([zg_refs[0][...], zg_refs[1][...]], axis=1) + bg_ref[:, :D]
    zb = jnp.concatenate([zg_refs[2][...], zg_refs[3][...]], axis=1) + bg_ref[:, D:]
    return _sigmoid(za), _sigmoid(zb)


def _mix_fwd(x, pm, of, z, b_gate, w_pa4, w_pb4, w_o4, *, name, plan=None):
    S, D = x.shape
    P = pm.shape[1]
    T = _row_tile(S, 512)

    def body(x_ref, pm_ref, of_ref, g0, g1, g2, g3, bg_ref, wpa_ref, wpb_ref, wo_ref, xo_ref, ya_ref, yb_ref):
        pmv, ofv = pm_ref[...], of_ref[...]
        ya = jnp.concatenate([_dot(pmv, wpa_ref[k]) for k in range(N_CHIPS)], axis=1)
        yb = jnp.concatenate([_dot(ofv, wpb_ref[k]) for k in range(N_CHIPS)], axis=1)
        ga, gb = _gates((g0, g1, g2, g3), bg_ref, D)
        merged = (ga * ya + gb * yb).astype(BF16)
        xo_ref[...] = x_ref[...] + _dot(merged, wo_ref[...].reshape(D, D))
        ya_ref[...] = ya.astype(BF16)
        yb_ref[...] = yb.astype(BF16)

    row = lambda w: pl.BlockSpec((T, w), lambda i: (i, 0))
    full = lambda a: pl.BlockSpec(a.shape, lambda i: (0,) * a.ndim)
    return _call(
        body, name=name, grid=(S // T,), parallel=(0,), plan=plan,
        in_specs=[row(D), row(P), row(P)] + _gate_specs(T, D) + [full(b_gate), full(w_pa4), full(w_pb4), full(w_o4)],
        out_specs=[row(D), row(D), row(D)],
        out_shape=[jax.ShapeDtypeStruct((S, D), F32), jax.ShapeDtypeStruct((S, D), BF16),
                   jax.ShapeDtypeStruct((S, D), BF16)],
        args=(x, pm, of, z, z, z, z, b_gate, w_pa4, w_pb4, w_o4))


def _mix_bwd(dxm, ya, yb, z, b_gate, pm, of, w_pa4, w_pb4, w_o4, *, name, plan=None):
    S, D = dxm.shape
    P = pm.shape[1]
    q4 = D // N_CHIPS
    T = _row_tile(S, 256)
    nt = S // T

    def body(dx_ref, ya_ref, yb_ref, g0, g1, g2, g3, bg_ref, pm_ref, of_ref, wpa_ref, wpb_ref, wo_ref,
             dzg_ref, dpm_ref, dof_ref, dwo_ref, dwpa_ref, dwpb_ref, dbg_ref, dwo16_ref, dwpa16_ref, dwpb16_ref):
        i = pl.program_id(0)

        @pl.when(i == 0)
        def _():
            dwo_ref[...] = jnp.zeros_like(dwo_ref)
            dwpa_ref[...] = jnp.zeros_like(dwpa_ref)
            dwpb_ref[...] = jnp.zeros_like(dwpb_ref)
            dbg_ref[...] = jnp.zeros_like(dbg_ref)

        dxb = dx_ref[...].astype(BF16)
        ya = ya_ref[...].astype(F32)
        yb = yb_ref[...].astype(F32)
        ga, gb = _gates((g0, g1, g2, g3), bg_ref, D)
        merged = (ga * ya + gb * yb).astype(BF16)
        dwo_ref[...] += _dot_tn(merged, dxb).reshape(N_CHIPS, q4, D)
        dm = _dot_nt(dxb, wo_ref[...].reshape(D, D))
        dza = dm * ya * ga * (1.0 - ga)
        dzb = dm * yb * gb * (1.0 - gb)
        dzg_ref[:, :D] = dza.astype(BF16)
        dzg_ref[:, D:] = dzb.astype(BF16)
        dbg_ref[:, :D] += jnp.sum(dza, axis=0, keepdims=True)
        dbg_ref[:, D:] += jnp.sum(dzb, axis=0, keepdims=True)
        dya = (dm * ga).astype(BF16)
        dyb = (dm * gb).astype(BF16)
        pmv, ofv = pm_ref[...], of_ref[...]
        dpm = jnp.zeros((T, P), F32)
        dof = jnp.zeros((T, P), F32)
        for k in range(N_CHIPS):
            cols = slice(k * q4, (k + 1) * q4)
            dwpa_ref[k] += _dot_tn(pmv, dya[:, cols])
            dwpb_ref[k] += _dot_tn(ofv, dyb[:, cols])
            dpm = dpm + _dot_nt(dya[:, cols], wpa_ref[k])
            dof = dof + _dot_nt(dyb[:, cols], wpb_ref[k])
        dpm_ref[...] = dpm
        dof_ref[...] = dof

        @pl.when(i == nt - 1)
        def _():
            dwo16_ref[...] = dwo_ref[...].astype(BF16)
            dwpa16_ref[...] = dwpa_ref[...].astype(BF16)
            dwpb16_ref[...] = dwpb_ref[...].astype(BF16)

    row = lambda w: pl.BlockSpec((T, w), lambda i: (i, 0))
    full = lambda a: pl.BlockSpec(a.shape, lambda i: (0,) * a.ndim)
    like = lambda a, dt: jax.ShapeDtypeStruct(a.shape, dt)
    return _call(
        body, name=name, grid=(nt,), plan=plan,
        in_specs=[row(D), row(D), row(D)] + _gate_specs(T, D) + [full(b_gate), row(P), row(P),
                                                                  full(w_pa4), full(w_pb4), full(w_o4)],
        out_specs=[row(2 * D), row(P), row(P), full(w_o4), full(w_pa4), full(w_pb4), full(b_gate),
                   full(w_o4), full(w_pa4), full(w_pb4)],
        out_shape=[jax.ShapeDtypeStruct((S, 2 * D), BF16), jax.ShapeDtypeStruct((S, P), F32),
                   jax.ShapeDtypeStruct((S, P), F32), like(w_o4, F32), like(w_pa4, F32), like(w_pb4, F32),
                   like(b_gate, F32), like(w_o4, BF16), like(w_pa4, BF16), like(w_pb4, BF16)],
        args=(dxm, ya, yb, z, z, z, z, b_gate, pm, of, w_pa4, w_pb4, w_o4))


def _conv3(h, halo, first_tile, cw, cb):
    h = h.astype(F32)
    halo = jnp.where(first_tile, 0.0, halo.astype(F32)[-SUBLANES:, :])
    hp = jnp.concatenate([halo, h], axis=0)
    h1 = _shift_down(hp, 1)
    h2 = _shift_down(hp, 2)
    return cw[0:1, :] * h2 + cw[1:2, :] * h1 + cw[2:3, :] * h + cb, h1, h2


def _ffn_down_fwd(h, conv_w, conv_b, w_down4, x, *, name, plan=None):
    S, F2 = h.shape
    _, f4, D = w_down4.shape
    T = _row_tile(S, 512)
    nf = 2
    tf = 2 * f4
    hb = T // HALO_CONV

    def body(hv_ref, hg_ref, pv_ref, pg_ref, cwv_ref, cwg_ref, cbv_ref, cbg_ref, wd_ref, x_ref,
             o_ref, val_ref, gate_ref, acc_ref):
        i, f = pl.program_id(0), pl.program_id(1)

        @pl.when(f == 0)
        def _():
            acc_ref[...] = jnp.zeros_like(acc_ref)

        val, _, _ = _conv3(hv_ref[...], pv_ref[...], i == 0, cwv_ref[...], cbv_ref[...])
        gate, _, _ = _conv3(hg_ref[...], pg_ref[...], i == 0, cwg_ref[...], cbg_ref[...])
        val_ref[...] = val.astype(BF16)
        gate_ref[...] = gate.astype(BF16)
        a = (gate * _sigmoid(gate) * val).astype(BF16)
        acc_ref[...] += _dot(a, wd_ref[...].reshape(tf, D))

        @pl.when(f == nf - 1)
        def _():
            o_ref[...] = x_ref[...] + acc_ref[...]

    prev = lambda i: jnp.maximum(i * hb - 1, 0)
    return _call(
        body, name=name, grid=(S // T, nf), parallel=(0,), plan=plan,
        in_specs=[pl.BlockSpec((T, tf), lambda i, f: (i, f)),
                  pl.BlockSpec((T, tf), lambda i, f: (i, nf + f)),
                  pl.BlockSpec((HALO_CONV, tf), lambda i, f: (prev(i), f)),
                  pl.BlockSpec((HALO_CONV, tf), lambda i, f: (prev(i), nf + f)),
                  pl.BlockSpec((3, tf), lambda i, f: (0, f)),
                  pl.BlockSpec((3, tf), lambda i, f: (0, nf + f)),
                  pl.BlockSpec((1, tf), lambda i, f: (0, f)),
                  pl.BlockSpec((1, tf), lambda i, f: (0, nf + f)),
                  pl.BlockSpec((2, f4, D), lambda i, f: (f, 0, 0)),
                  pl.BlockSpec((T, D), lambda i, f: (i, 0))],
        out_specs=[pl.BlockSpec((T, D), lambda i, f: (i, 0)),
                   pl.BlockSpec((T, tf), lambda i, f: (i, f)),
                   pl.BlockSpec((T, tf), lambda i, f: (i, f))],
        out_shape=[jax.ShapeDtypeStruct((S, D), F32), jax.ShapeDtypeStruct((S, N_CHIPS * f4), BF16),
                   jax.ShapeDtypeStruct((S, N_CHIPS * f4), BF16)],
        scratch=[pltpu.VMEM((T, D), F32)],
        args=(h, h, h, h, conv_w, conv_w, conv_b, conv_b, w_down4, x))


def _ffn_down_bwd(dxo, h, val16, gate16, conv_w, w_down4, *, name, plan=None):
    S, F2 = h.shape
    _, f4, D = w_down4.shape
    F = N_CHIPS * f4
    T = _row_tile(S, 256)
    nf = 2
    tf = 2 * f4
    nt = S // T

    def body(dx_ref, hv_ref, hg_ref, val_ref, gate_ref, cwv_ref, cwg_ref, wd_ref,
             dhv_ref, dhg_ref, dwd_ref, dwd16_ref, dcwv_ref, dcwg_ref, dcbv_ref, dcbg_ref, cv_scr, cg_scr):
        i = pl.program_id(1)

        @pl.when(i == 0)
        def _():
            cv_scr[...] = jnp.zeros_like(cv_scr)
            cg_scr[...] = jnp.zeros_like(cg_scr)
            dwd_ref[...] = jnp.zeros_like(dwd_ref)
            dcwv_ref[...] = jnp.zeros_like(dcwv_ref)
            dcwg_ref[...] = jnp.zeros_like(dcwg_ref)
            dcbv_ref[...] = jnp.zeros_like(dcbv_ref)
            dcbg_ref[...] = jnp.zeros_like(dcbg_ref)

        dxb = dx_ref[...].astype(BF16)
        val = val_ref[...].astype(F32)
        gate = gate_ref[...].astype(F32)
        sg = _sigmoid(gate)
        sil = gate * sg
        dwd_ref[...] += _dot_tn((sil * val).astype(BF16), dxb).reshape(2, f4, D)
        da = _dot_nt(dxb, wd_ref[...].reshape(tf, D))

        def conv_bwd(dhc, h0, cw, c_scr, dh_ref, dcw_ref, dcb_ref):
            ext = jnp.concatenate([dhc, c_scr[...]], axis=0)
            n1 = _shift_up(ext, 1)
            n2 = _shift_up(ext, 2)
            dh_ref[...] = (cw[2:3, :] * dhc + cw[1:2, :] * n1 + cw[0:1, :] * n2).astype(BF16)
            c_scr[...] = dhc[:SUBLANES, :]
            dcw_ref[0:1, :] += jnp.sum(n2 * h0, axis=0, keepdims=True)
            dcw_ref[1:2, :] += jnp.sum(n1 * h0, axis=0, keepdims=True)
            dcw_ref[2:3, :] += jnp.sum(dhc * h0, axis=0, keepdims=True)
            dcb_ref[...] += jnp.sum(dhc, axis=0, keepdims=True)

        conv_bwd(da * sil, hv_ref[...].astype(F32), cwv_ref[...], cv_scr, dhv_ref, dcwv_ref, dcbv_ref)
        conv_bwd(da * val * _dsilu(gate, sg), hg_ref[...].astype(F32), cwg_ref[...], cg_scr, dhg_ref, dcwg_ref,
                 dcbg_ref)

        @pl.when(i == nt - 1)
        def _():
            dwd16_ref[...] = dwd_ref[...].astype(BF16)

    rev = lambda i: nt - 1 - i
    wd_spec = pl.BlockSpec((2, f4, D), lambda f, i: (f, 0, 0))
    return _call(
        body, name=name, grid=(nf, nt), plan=plan,
        in_specs=[pl.BlockSpec((T, D), lambda f, i: (rev(i), 0)),
                  pl.BlockSpec((T, tf), lambda f, i: (rev(i), f)),
                  pl.BlockSpec((T, tf), lambda f, i: (rev(i), nf + f)),
                  pl.BlockSpec((T, tf), lambda f, i: (rev(i), f)),
                  pl.BlockSpec((T, tf), lambda f, i: (rev(i), f)),
                  pl.BlockSpec((3, tf), lambda f, i: (0, f)),
                  pl.BlockSpec((3, tf), lambda f, i: (0, nf + f)),
                  wd_spec],
        out_specs=[pl.BlockSpec((T, tf), lambda f, i: (rev(i), f)),
                   pl.BlockSpec((T, tf), lambda f, i: (rev(i), f)),
                   wd_spec, wd_spec,
                   pl.BlockSpec((3, tf), lambda f, i: (0, f)),
                   pl.BlockSpec((3, tf), lambda f, i: (0, f)),
                   pl.BlockSpec((1, tf), lambda f, i: (0, f)),
                   pl.BlockSpec((1, tf), lambda f, i: (0, f))],
        out_shape=[jax.ShapeDtypeStruct((S, F), BF16), jax.ShapeDtypeStruct((S, F), BF16),
                   jax.ShapeDtypeStruct((N_CHIPS, f4, D), F32), jax.ShapeDtypeStruct((N_CHIPS, f4, D), BF16),
                   jax.ShapeDtypeStruct((3, F), F32), jax.ShapeDtypeStruct((3, F), F32),
                   jax.ShapeDtypeStruct((1, F), F32), jax.ShapeDtypeStruct((1, F), F32)],
        scratch=[pltpu.VMEM((SUBLANES, tf), F32), pltpu.VMEM((SUBLANES, tf), F32)],
        args=(dxo, h, h, val16, gate16, conv_w, conv_w, w_down4))


def _final_loss(x, g, target, *, name):
    S, D = x.shape
    T = _row_tile(S, 512)

    def body(x_ref, g_ref, t_ref, loss_ref, dx_ref, dg_ref):
        @pl.when(pl.program_id(0) == 0)
        def _():
            loss_ref[...] = jnp.zeros_like(loss_ref)
            dg_ref[...] = jnp.zeros_like(dg_ref)

        xf = x_ref[...]
        r = lax.rsqrt(jnp.mean(xf * xf, axis=-1, keepdims=True) + EPS)
        xhat = xf * r
        err = xhat * g_ref[...] - t_ref[...]
        loss_ref[...] += jnp.sum(err * err, axis=0, keepdims=True) * (0.5 / D)
        dy = err * (1.0 / D)
        dxhat = dy * g_ref[...]
        dx_ref[...] = r * (dxhat - xhat * jnp.mean(dxhat * xhat, axis=-1, keepdims=True))
        dg_ref[...] += jnp.sum(dy * xhat, axis=0, keepdims=True)

    return _call(
        body, name=name, grid=(S // T,),
        in_specs=[pl.BlockSpec((T, D), lambda i: (i, 0)), pl.BlockSpec((1, D), lambda i: (0, 0)),
                  pl.BlockSpec((T, D), lambda i: (i, 0))],
        out_specs=[pl.BlockSpec((1, D), lambda i: (0, 0)), pl.BlockSpec((T, D), lambda i: (i, 0)),
                   pl.BlockSpec((1, D), lambda i: (0, 0))],
        out_shape=[jax.ShapeDtypeStruct((1, D), F32), jax.ShapeDtypeStruct((S, D), F32),
                   jax.ShapeDtypeStruct((1, D), F32)],
        args=(x, g, target))[0]


BIG = ("w_in", "w_pa", "w_pb", "w_o", "w_up", "w_down")
SMALL = ("norm1_g", "b_gate", "pool_w", "pool_scale", "lb_logits", "hgrn_norm_g", "norm2_g", "conv_b", "final_g")
WEIGHTS = ("norm1_g", "w_in", "b_gate", "pool_w", "pool_scale", "lb_logits", "hgrn_norm_g", "w_pa", "w_pb", "w_o",
           "norm2_g", "w_up", "conv_w", "conv_b", "w_down", "final_g")


def _lower_bounds(lb_logits):
    soft = jax.nn.softmax(lb_logits.astype(F32), axis=0)
    cum = jnp.cumsum(soft, axis=0)
    return cum - cum[0:1]


def _step(x, target, sm, wts, shards=None):
    L = sm["norm1_g"].shape[0]
    wts = dict(wts)
    dist = shards is not None
    lbs, lb_vjp = jax.vjp(_lower_bounds, sm["lb_logits"])
    row = lambda a: a.reshape(1, -1)
    conv_w = sm.get("conv_w")

    def gather(names_layers, with_conv=False):
        items = [(shards[n], "rows", l) for n, l in names_layers]
        if with_conv:
            items.append((shards["conv_w"], "layer", None))
        return _GatherPlan(items)

    def landed(names_layers, outs):
        for key, arr in zip(names_layers, outs):
            wts[key] = arr

    own = {"in_proj": ("w_pa", "w_pb", "w_o", "w_up"), "hgrn_fwd": ("w_down",)}
    ahead = {"hgrn_fwd": ("w_pa", "w_pb", "w_o"), "mix_fwd": ("w_down",), "up": ("w_in",), "down_fwd": ("w_up",)}

    def riders(l, kernel):
        if not dist:
            return [], None
        keys = [(n, l) for n in own.get(kernel, ())] if l == 0 else []
        keys += [(n, l + 1) for n in ahead.get(kernel, ())] if l + 1 < L else []
        with_conv = l == 0 and kernel == "hgrn_fwd"
        return keys, (gather(keys, with_conv) if keys or with_conv else None)

    saved = []
    for l in range(L):
        keys, plan = riders(l, "in_proj")
        (xn1, z), got = _norm_matmul(x, row(sm["norm1_g"][l]), wts[("w_in", l)], name=f"in_proj_{l}", plan=plan)
        landed(keys, got)
        pm = _pool_fwd(z, sm["pool_w"][l], row(sm["pool_scale"][l]), name=f"pool_fwd_{l}")
        keys, plan = riders(l, "hgrn_fwd")
        (o_raw, of, states), got = _hgrn_fwd(z, row(lbs[l]), row(sm["hgrn_norm_g"][l]), name=f"hgrn_fwd_{l}",
                                             plan=plan)
        landed(keys, got)
        if dist and l == 0:
            full = got[-1]
            conv_w = jnp.concatenate([full[:, k] for k in range(N_CHIPS)], axis=2)
        keys, plan = riders(l, "mix_fwd")
        (x_mid, ya, yb), got = _mix_fwd(x, pm, of, z, row(sm["b_gate"][l]), wts[("w_pa", l)], wts[("w_pb", l)],
                                        wts[("w_o", l)], name=f"mix_fwd_{l}", plan=plan)
        landed(keys, got)
        keys, plan = riders(l, "up")
        (xn2, h), got = _norm_matmul(x_mid, row(sm["norm2_g"][l]), wts[("w_up", l)], name=f"up_{l}",
                                     out_dtype=BF16, plan=plan)
        landed(keys, got)
        keys, plan = riders(l, "down_fwd")
        (x_out, val16, gate16), got = _ffn_down_fwd(h, conv_w[l], row(sm["conv_b"][l]), wts[("w_down", l)], x_mid,
                                                    name=f"down_fwd_{l}", plan=plan)
        landed(keys, got)
        saved.append(dict(x=x, xn1=xn1, z=z, pm=pm, o_raw=o_raw, of=of, states=states,
                          x_mid=x_mid, ya=ya, yb=yb, xn2=xn2, h=h, val16=val16, gate16=gate16))
        x = x_out

    loss_cols, dx, d_final_g = _final_loss(x, row(sm["final_g"]), target, name="final_loss")

    small = {k: [None] * L for k in ("norm1_g", "b_gate", "pool_w", "pool_scale", "hgrn_norm_g", "norm2_g",
                                     "conv_w", "conv_b")}
    big32, big16, recv = {}, {}, {}
    dlbs = [None] * L
    pending = []

    def scatter(now):
        if not (dist and pending and now):
            return [], None
        keys = list(pending)
        del pending[:]
        return keys, _ScatterPlan([big16[k] for k in keys])

    def sent(keys, outs):
        for key, arr in zip(keys, outs):
            recv[key] = arr

    def made(name, l, g32, g16):
        big32[(name, l)], big16[(name, l)] = g32, g16
        pending.append((name, l))

    for l in reversed(range(L)):
        s = saved[l]
        last = l == 0
        keys, plan = scatter(True)
        (dhv, dhg, d_wd, d_wd16, dcwv, dcwg, dcbv, dcbg), got = _ffn_down_bwd(
            dx, s["h"], s["val16"], s["gate16"], conv_w[l], wts[("w_down", l)], name=f"down_bwd_{l}", plan=plan)
        sent(keys, got)
        made("w_down", l, d_wd, d_wd16)
        small["conv_w"][l] = jnp.concatenate([dcwv, dcwg], axis=1)
        small["conv_b"][l] = jnp.concatenate([dcbv, dcbg], axis=1)[0]
        keys, plan = scatter(last)
        (d_wu, d_wu16), got = _wgrad(s["xn2"], [dhv, dhg], name=f"up_wgrad_{l}", rows=2048, plan=plan)
        sent(keys, got)
        made("w_up", l, d_wu, d_wu16)
        (dxm, dg2), _ = _dgrad_norm([dhv, dhg], wts[("w_up", l)], s["x_mid"], row(sm["norm2_g"][l]), dx,
                                    name=f"up_dgrad_{l}")
        small["norm2_g"][l] = dg2[0]

        keys, plan = scatter(last)
        (dzg, dpm, dof, d_wo, d_wpa, d_wpb, dbg, d_wo16, d_wpa16, d_wpb16), got = _mix_bwd(
            dxm, s["ya"], s["yb"], s["z"], row(sm["b_gate"][l]), s["pm"], s["of"],
            wts[("w_pa", l)], wts[("w_pb", l)], wts[("w_o", l)], name=f"mix_bwd_{l}", plan=plan)
        sent(keys, got)
        made("w_o", l, d_wo, d_wo16)
        made("w_pa", l, d_wpa, d_wpa16)
        made("w_pb", l, d_wpb, d_wpb16)
        small["b_gate"][l] = dbg[0]

        du, dpw, dps = _pool_bwd(s["z"], dpm, sm["pool_w"][l], row(sm["pool_scale"][l]), name=f"pool_bwd_{l}")
        small["pool_w"][l], small["pool_scale"][l] = dpw, dps[0]

        keys, plan = scatter(last)
        (dzq, dzf, dzi, dzo, dlb, dng), got = _hgrn_bwd(s["z"], row(lbs[l]), row(sm["hgrn_norm_g"][l]), s["o_raw"],
                                                      s["states"], dof, name=f"hgrn_bwd_{l}", plan=plan)
        sent(keys, got)
        dlbs[l] = dlb[0]
        small["hgrn_norm_g"][l] = jnp.sum(dng.reshape(-1, LANES), axis=0)

        dz = [du, dzq, dzf, dzi, dzo, dzg]
        (d_wi, d_wi16), _ = _wgrad(s["xn1"], dz, name=f"in_wgrad_{l}", rows=1024)
        made("w_in", l, d_wi, d_wi16)
        keys, plan = scatter(last)
        (dx, dg1), got = _dgrad_norm(dz, wts[("w_in", l)], s["x"], row(sm["norm1_g"][l]), dxm,
                                     name=f"in_dgrad_{l}", plan=plan)
        sent(keys, got)
        small["norm1_g"][l] = dg1[0]

    out = {k: jnp.stack(v) for k, v in small.items()}
    out["lb_logits"] = lb_vjp(jnp.stack(dlbs))[0]
    out["final_g"] = d_final_g[0]
    return loss_cols, dx, out, big32, recv


def _elementwise_rows(R, n, n_arrays):
    if 2 * n_arrays * R * n * 4 <= VMEM_LIMIT // 4 or R % 8:
        return R
    block = VMEM_LIMIT // 2 // (2 * n_arrays)
    want = 8
    while want * 2 * n * 4 <= block:
        want *= 2
    return _row_tile(R, want)


def _sum_layers(own, got, chip, *, name):
    L = len(own)
    _, r, n = own[0].shape
    T = _elementwise_rows(r, n, 6)
    nt = r // T

    def body(chip_ref, *refs):
        o_ref = refs[-1]
        l = pl.program_id(0)
        for k in range(L):
            @pl.when(l == k)
            def _():
                own_ref, got_ref = refs[2 * k], refs[2 * k + 1]
                acc = own_ref[...]
                for j in range(3):
                    acc = acc + got_ref[j].astype(F32)
                o_ref[...] = acc

    in_specs = []
    for k in range(L):
        hold = 0 if k else nt - 1
        in_specs.append(pl.BlockSpec((None, T, n), lambda l, i, c, k=k, hold=hold: (c[0], jnp.where(l == k, i, hold), 0)))
        in_specs.append(pl.BlockSpec((3, T, n), lambda l, i, c, k=k, hold=hold: (0, jnp.where(l == k, i, hold), 0)))
    grid_spec = pltpu.PrefetchScalarGridSpec(
        num_scalar_prefetch=1, grid=(L, nt), in_specs=in_specs,
        out_specs=pl.BlockSpec((None, T, n), lambda l, i, c: (l, i, 0)))
    args = [a for pair in zip(own, got) for a in pair]
    return pl.pallas_call(
        body, name=name, grid_spec=grid_spec, out_shape=jax.ShapeDtypeStruct((L, r, n), F32),
        compiler_params=pltpu.CompilerParams(dimension_semantics=("arbitrary", "arbitrary"),
                                             vmem_limit_bytes=VMEM_LIMIT),
    )(chip, *args)


def _sum_stack(first, rest, *, name):
    R, n = first.shape
    K = rest.shape[0]
    T = _elementwise_rows(R, n, K + 2)

    def body(a_ref, r_ref, o_ref):
        acc = a_ref[...]
        for j in range(K):
            acc = acc + r_ref[j].astype(F32)
        o_ref[...] = acc

    return _call(
        body, name=name, grid=(R // T,), parallel=(0,),
        in_specs=[pl.BlockSpec((T, n), lambda i: (i, 0)), pl.BlockSpec((K, T, n), lambda i: (0, i, 0))],
        out_specs=[pl.BlockSpec((T, n), lambda i: (i, 0))],
        out_shape=[jax.ShapeDtypeStruct((R, n), F32)],
        args=(first, rest))[0][0]


def _adamw(w, m, v, g_parts, *, name):
    R, n = w.shape
    n_g = len(g_parts)
    T = _elementwise_rows(R, n, 7 + n_g)

    def body(*refs):
        w_ref, m_ref, v_ref = refs[:3]
        g_refs = refs[3:3 + n_g]
        go_ref, d_ref, mo_ref, vo_ref = refs[3 + n_g:]
        g_ = g_refs[0][...]
        for r in g_refs[1:]:
            g_ = g_ + r[...]
        m_ = ADAM_B1 * m_ref[...] + (1.0 - ADAM_B1) * g_
        v_ = ADAM_B2 * v_ref[...] + (1.0 - ADAM_B2) * (g_ * g_)
        m_hat = m_ / (1.0 - ADAM_B1 ** ADAM_STEP)
        v_hat = v_ / (1.0 - ADAM_B2 ** ADAM_STEP)
        go_ref[...] = g_
        d_ref[...] = -ADAM_LR * (m_hat / (jnp.sqrt(v_hat) + ADAM_EPS) + ADAM_WD * w_ref[...])
        mo_ref[...] = m_
        vo_ref[...] = v_

    blk = pl.BlockSpec((T, n), lambda i: (i, 0))
    return _call(
        body, name=name, grid=(R // T,), parallel=(0,),
        in_specs=[blk] * (3 + n_g), out_specs=[blk] * 4,
        out_shape=[jax.ShapeDtypeStruct((R, n), F32)] * 4,
        args=(w, m, v, *g_parts))[0]


PACK_ALIGN = 8 * LANES


def _pack(pieces):
    flat = []
    for a in pieces:
        a = a.reshape(-1)
        pad = (-a.shape[0]) % PACK_ALIGN
        flat.append(jnp.pad(a, (0, pad)) if pad else a)
    return jnp.concatenate(flat).reshape(-1, LANES)


def _unpack(buf, shapes):
    flat = buf.reshape(-1)
    out, off = [], 0
    for shp in shapes:
        size = 1
        for s in shp:
            size *= s
        out.append(flat[off:off + size].reshape(shp))
        off += size + (-size) % PACK_ALIGN
    return out


def kernel(x, norm1_g, w_in, b_gate, pool_w, pool_scale, lb_logits, hgrn_norm_g, w_pa, w_pb, w_o, norm2_g, w_up, conv_w, conv_b, w_down, final_g, loss_target, m_norm1_g, m_w_in, m_b_gate, m_pool_w, m_pool_scale, m_lb_logits, m_hgrn_norm_g, m_w_pa, m_w_pb, m_w_o, m_norm2_g, m_w_up, m_conv_w, m_conv_b, m_w_down, m_final_g, v_norm1_g, v_w_in, v_b_gate, v_pool_w, v_pool_scale, v_lb_logits, v_hgrn_norm_g, v_w_pa, v_w_pb, v_w_o, v_norm2_g, v_w_up, v_conv_w, v_conv_b, v_w_down, v_final_g):
    env = dict(locals())
    w = {n: env[n] for n in WEIGHTS}
    m = {n: env["m_" + n] for n in WEIGHTS}
    v = {n: env["v_" + n] for n in WEIGHTS}
    my_chip = 2 * lax.axis_index("x") + lax.axis_index("y")
    L = w_in.shape[0]

    shards = {n: w[n].astype(BF16) for n in BIG}
    shards["conv_w"] = w["conv_w"]
    w_in0 = _run_plan(_GatherPlan([(shards["w_in"], "rows", 0)]), name="gather_w_in0")[0]
    sm = {n: w[n] for n in SMALL}
    loss_cols, grad_x, g_small, big32, recv = _step(x[0], loss_target[0], sm, {("w_in", 0): w_in0}, shards)

    chip = my_chip.reshape(1).astype(jnp.int32)
    sums = [_sum_layers([big32[(n, l)] for l in range(L)], [recv[(n, l)] for l in range(L)], chip,
                        name="chip_sum_" + n) for n in BIG]
    small_names = list(SMALL)
    small_pieces = [g_small[n] for n in small_names] + [g_small["conv_w"], loss_cols]
    small_shapes = [a.shape for a in small_pieces]
    packed = _pack(small_pieces)
    Rs = packed.shape[0]
    swapped = _run_plan(_Together([_SiblingPlan(sums), _EveryonePlan(packed)]), name="tail_exchange")
    theirs, everyone = swapped[:-1], swapped[-1].reshape(8, Rs, LANES)
    g, delta, new_m, new_v = {}, {}, {}, {}
    for n, mine, other in zip(BIG, sums, theirs):
        shp = w[n].shape
        two_d = lambda a: a.reshape(-1, shp[-1])
        outs = _adamw(two_d(w[n]), two_d(m[n]), two_d(v[n]), [two_d(mine), two_d(other)], name="adamw_" + n)
        g[n], delta[n], new_m[n], new_v[n] = [a.reshape(shp) for a in outs]

    summed = _unpack(_sum_stack(everyone[0], everyone[1:], name="small_sum"), small_shapes)
    loss = jnp.sum(summed[-1])
    cshard = w["conv_w"].shape[2]
    gs = dict(zip(small_names, summed[:len(small_names)]))
    g_cw = lax.dynamic_slice_in_dim(summed[-2], my_chip * cshard, cshard, axis=2)

    sm_out = _adamw(_pack([w[n] for n in small_names]), _pack([m[n] for n in small_names]),
                    _pack([v[n] for n in small_names]), [_pack([gs[n] for n in small_names])], name="adamw_small")
    shapes = [w[n].shape for n in small_names]
    for n, g_, d_, m_, v_ in zip(small_names, *[_unpack(a, shapes) for a in sm_out]):
        g[n], delta[n], new_m[n], new_v[n] = g_, d_, m_, v_
    shp = w["conv_w"].shape
    two_d = lambda a: a.reshape(-1, shp[-1])
    outs = _adamw(two_d(w["conv_w"]), two_d(m["conv_w"]), two_d(v["conv_w"]), [two_d(g_cw)], name="adamw_conv_w")
    g["conv_w"], delta["conv_w"], new_m["conv_w"], new_v["conv_w"] = [a.reshape(shp) for a in outs]

    return (loss, grad_x[None], *[g[n] for n in WEIGHTS], *[delta[n] for n in WEIGHTS],
            *[new_m[n] for n in WEIGHTS], *[new_v[n] for n in WEIGHTS])
```

```python
import jax
import jax.numpy as jnp
from jax import lax
from jax.experimental import pallas as pl
from jax.experimental.pallas import tpu as pltpu

F32 = jnp.float32
BF16 = jnp.bfloat16

EPS = 1e-6
CHUNK = 64
SUB = 32
LANES = 128
SUBLANES = 8
POOL_WINDOWS = (2, 4, 8, 16)
HALO_POOL = 16
HALO_CONV = 16
EXP_CLAMP = 80.0

ADAM_LR = 0.001
ADAM_B1 = 0.9
ADAM_B2 = 0.999
ADAM_EPS = 1e-08
ADAM_WD = 0.01
ADAM_STEP = 10

VMEM_LIMIT = 56 * 1024 * 1024
MESH_ID = pl.DeviceIdType.MESH
N_CHIPS = 4
ANY = pl.BlockSpec(memory_space=pl.ANY)


def _dot(a, b):
    return jnp.dot(a, b, preferred_element_type=F32)


def _dot_nt(a, b):
    return lax.dot_general(a, b, (((1,), (1,)), ((), ())), preferred_element_type=F32)


def _dot_tn(a, b):
    return lax.dot_general(a, b, (((0,), (0,)), ((), ())), preferred_element_type=F32)


def _sigmoid(x):
    return jax.nn.sigmoid(x)


def _dsilu(x, s):
    return s * (1.0 + x * (1.0 - s))


def _row_tile(rows, want):
    t = min(rows, want)
    while rows % t:
        t //= 2
    return t


def _place():
    x, y, c = lax.axis_index("x"), lax.axis_index("y"), lax.axis_index("c")
    chips = [(1 - x, y), (x, 1 - y), (1 - x, 1 - y)]
    return x, y, c, chips


def _remote(src, dst, sems, k, to):
    return pltpu.make_async_remote_copy(src_ref=src, dst_ref=dst, send_sem=sems[0].at[k], recv_sem=sems[1].at[k],
                                        device_id=to, device_id_type=MESH_ID)


class _GatherPlan:
    def __init__(self, items):
        self.items = items
        self.inputs = [a for a, _, _ in items]
        self.out_shapes = []
        for a, kind, _ in items:
            shp = (N_CHIPS,) + a.shape[1:] if kind == "rows" else (a.shape[0], N_CHIPS) + a.shape[1:]
            self.out_shapes.append(jax.ShapeDtypeStruct(shp, a.dtype))
        n = len(items)
        self.scratch = [pltpu.SemaphoreType.DMA((6 * n,)), pltpu.SemaphoreType.DMA((6 * n,)),
                        pltpu.SemaphoreType.DMA((2 * n,))]

    def _views(self, i, src, dst):
        _, kind, l = self.items[i]
        if kind == "rows":
            half = src.shape[1] // 2
            part = lambda core: src.at[l, pl.ds(core * half, half), :]
            land = lambda chip, core: dst.at[chip, pl.ds(core * half, half), :]
        else:
            part = lambda core: src.at[core]
            land = lambda chip, core: dst.at[core, chip]
        return part, land

    def start(self, srcs, dsts, sems):
        x, y, c, chips = _place()
        me = 2 * x + y
        for i, (src, dst) in enumerate(zip(srcs, dsts)):
            part, land = self._views(i, src, dst)
            for core in range(2):
                pltpu.make_async_copy(part(core), land(me, core), sems[2].at[2 * i + core]).start()
            for j, (px, py) in enumerate(chips):
                _remote(part(c), land(me, c), sems, 6 * i + j, (px, py, c)).start()

    def finish(self, srcs, dsts, sems):
        x, y, c, chips = _place()
        me = 2 * x + y
        sibling = (x, y, 1 - c)
        for i, (src, dst) in enumerate(zip(srcs, dsts)):
            part, land = self._views(i, src, dst)
            for j, (px, py) in enumerate(chips):
                got = land(2 * px + py, c)
                _remote(got, got, sems, 6 * i + j, (px, py, c)).wait_recv()
                _remote(got, got, sems, 6 * i + 3 + j, sibling).start()
        for i, (src, dst) in enumerate(zip(srcs, dsts)):
            part, land = self._views(i, src, dst)
            for j, (px, py) in enumerate(chips):
                got = land(2 * px + py, 1 - c)
                _remote(got, got, sems, 6 * i + 3 + j, sibling).wait_recv()
            for j, (px, py) in enumerate(chips):
                _remote(part(c), land(me, c), sems, 6 * i + j, (px, py, c)).wait_send()
                mine = land(2 * px + py, c)
                _remote(mine, mine, sems, 6 * i + 3 + j, sibling).wait_send()
            for core in range(2):
                pltpu.make_async_copy(part(core), land(me, core), sems[2].at[2 * i + core]).wait()


class _ScatterPlan:
    def __init__(self, items):
        self.inputs = list(items)
        self.out_shapes = [jax.ShapeDtypeStruct((3,) + a.shape[1:], a.dtype) for a in items]
        n = len(items)
        self.scratch = [pltpu.SemaphoreType.DMA((3 * n,)), pltpu.SemaphoreType.DMA((3 * n,))]

    def _copies(self, srcs, dsts, sems):
        x, y, c, chips = _place()
        return [_remote(src.at[2 * px + py], dst.at[j], sems, 3 * i + j, (px, py, c))
                for i, (src, dst) in enumerate(zip(srcs, dsts)) for j, (px, py) in enumerate(chips)]

    def start(self, srcs, dsts, sems):
        for cp in self._copies(srcs, dsts, sems):
            cp.start()

    def finish(self, srcs, dsts, sems):
        copies = self._copies(srcs, dsts, sems)
        for cp in copies:
            cp.wait_recv()
        for cp in copies:
            cp.wait_send()


class _SiblingPlan:
    def __init__(self, items):
        self.inputs = list(items)
        self.out_shapes = [jax.ShapeDtypeStruct(a.shape, a.dtype) for a in items]
        n = len(items)
        self.scratch = [pltpu.SemaphoreType.DMA((n,)), pltpu.SemaphoreType.DMA((n,))]

    def _copies(self, srcs, dsts, sems):
        x, y, c, _ = _place()
        return [_remote(src, dst, sems, i, (x, y, 1 - c)) for i, (src, dst) in enumerate(zip(srcs, dsts))]

    def start(self, srcs, dsts, sems):
        for cp in self._copies(srcs, dsts, sems):
            cp.start()

    def finish(self, srcs, dsts, sems):
        copies = self._copies(srcs, dsts, sems)
        for cp in copies:
            cp.wait_recv()
        for cp in copies:
            cp.wait_send()


class _EveryonePlan:
    def __init__(self, block):
        self.inputs = [block]
        self.m = block.shape[0]
        self.out_shapes = [jax.ShapeDtypeStruct((8 * self.m,) + block.shape[1:], block.dtype)]
        self.scratch = [pltpu.SemaphoreType.DMA((7,)), pltpu.SemaphoreType.DMA((7,)), pltpu.SemaphoreType.DMA((1,))]

    def _rows(self, dst, px, py, pc):
        return dst.at[pl.ds((4 * px + 2 * py + pc) * self.m, self.m), :]

    def start(self, srcs, dsts, sems):
        x, y, c, chips = _place()
        src, dst = srcs[0], dsts[0]
        pltpu.make_async_copy(src, self._rows(dst, x, y, c), sems[2].at[0]).start()
        _remote(src, self._rows(dst, x, y, c), sems, 0, (x, y, 1 - c)).start()
        for j, (px, py) in enumerate(chips):
            _remote(src, self._rows(dst, x, y, c), sems, 1 + j, (px, py, c)).start()

    def finish(self, srcs, dsts, sems):
        x, y, c, chips = _place()
        src, dst = srcs[0], dsts[0]
        sibling = (x, y, 1 - c)
        for j, (px, py) in enumerate(chips):
            got = self._rows(dst, px, py, c)
            _remote(got, got, sems, 1 + j, (px, py, c)).wait_recv()
            _remote(got, got, sems, 4 + j, sibling).start()
        sib = self._rows(dst, x, y, 1 - c)
        _remote(sib, sib, sems, 0, sibling).wait_recv()
        for j, (px, py) in enumerate(chips):
            got = self._rows(dst, px, py, 1 - c)
            _remote(got, got, sems, 4 + j, sibling).wait_recv()
        mine = self._rows(dst, x, y, c)
        _remote(src, mine, sems, 0, sibling).wait_send()
        for j, (px, py) in enumerate(chips):
            _remote(src, mine, sems, 1 + j, (px, py, c)).wait_send()
            got = self._rows(dst, px, py, c)
            _remote(got, got, sems, 4 + j, sibling).wait_send()
        pltpu.make_async_copy(src, mine, sems[2].at[0]).wait()


def _call(body, *, name, grid, in_specs, out_specs, out_shape, args, scratch=(), parallel=(), plan=None):
    n_in, n_out, n_scr = len(in_specs), len(out_shape), len(scratch)
    sem = tuple("parallel" if (a in parallel and plan is None) else "arbitrary" for a in range(len(grid)))
    params = pltpu.CompilerParams(dimension_semantics=sem, vmem_limit_bytes=VMEM_LIMIT)
    if plan is None:
        outs = pl.pallas_call(body, name=name, grid=grid, in_specs=in_specs, out_specs=out_specs,
                              out_shape=out_shape, scratch_shapes=list(scratch), compiler_params=params)(*args)
        return list(outs), []
    p_in, p_out, p_scr = len(plan.inputs), len(plan.out_shapes), len(plan.scratch)

    def wrapped(*refs):
        ins, refs = refs[:n_in], refs[n_in:]
        p_ins, refs = refs[:p_in], refs[p_in:]
        outs, refs = refs[:n_out], refs[n_out:]
        p_outs, refs = refs[:p_out], refs[p_out:]
        scr, p_sems = refs[:n_scr], refs[n_scr:]
        ids = [pl.program_id(a) for a in range(len(grid))]
        first = _all([i == 0 for i in ids])
        last = _all([i == n - 1 for i, n in zip(ids, grid)])

        @pl.when(first)
        def _():
            plan.start(p_ins, p_outs, p_sems)

        body(*ins, *outs, *scr)

        @pl.when(last)
        def _():
            plan.finish(p_ins, p_outs, p_sems)

    outs = pl.pallas_call(
        wrapped, name=name, grid=grid,
        in_specs=list(in_specs) + [ANY] * p_in, out_specs=list(out_specs) + [ANY] * p_out,
        out_shape=list(out_shape) + list(plan.out_shapes),
        scratch_shapes=list(scratch) + list(plan.scratch), compiler_params=params,
    )(*args, *plan.inputs)
    return list(outs[:n_out]), list(outs[n_out:])


def _all(conds):
    out = conds[0]
    for c in conds[1:]:
        out = out & c
    return out


class _Together:
    def __init__(self, plans):
        self.plans = plans
        self.inputs = [a for p in plans for a in p.inputs]
        self.out_shapes = [s for p in plans for s in p.out_shapes]
        self.scratch = [s for p in plans for s in p.scratch]

    def _split(self, refs, count):
        out, at = [], 0
        for p in self.plans:
            out.append(refs[at:at + count(p)])
            at += count(p)
        return out

    def _parts(self, srcs, dsts, sems):
        return zip(self.plans, self._split(srcs, lambda p: len(p.inputs)),
                   self._split(dsts, lambda p: len(p.out_shapes)), self._split(sems, lambda p: len(p.scratch)))

    def start(self, srcs, dsts, sems):
        for p, s, d, m in self._parts(srcs, dsts, sems):
            p.start(s, d, m)

    def finish(self, srcs, dsts, sems):
        for p, s, d, m in self._parts(srcs, dsts, sems):
            p.finish(s, d, m)


def _run_plan(plan, *, name):
    p_in, p_out = len(plan.inputs), len(plan.out_shapes)

    def body(*refs):
        srcs, dsts, sems = refs[:p_in], refs[p_in:p_in + p_out], refs[p_in + p_out:]
        plan.start(srcs, dsts, sems)
        plan.finish(srcs, dsts, sems)

    return list(pl.pallas_call(body, name=name, in_specs=[ANY] * p_in, out_specs=[ANY] * p_out,
                               out_shape=list(plan.out_shapes), scratch_shapes=list(plan.scratch))(*plan.inputs))


def _rms(xf, g):
    r = lax.rsqrt(jnp.mean(xf * xf, axis=-1, keepdims=True) + EPS)
    return (xf * r * g).astype(BF16)


def _rmsnorm(x, g, *, name, plan=None):
    S, D = x.shape
    tm = _row_tile(S, 1024)

    def body(x_ref, g_ref, xn_ref):
        xn_ref[...] = _rms(x_ref[...], g_ref[...])

    return _call(
        body, name=name, grid=(S // tm,), parallel=(0,), plan=plan,
        in_specs=[pl.BlockSpec((tm, D), lambda i: (i, 0)), pl.BlockSpec((1, D), lambda i: (0, 0))],
        out_specs=[pl.BlockSpec((tm, D), lambda i: (i, 0))],
        out_shape=[jax.ShapeDtypeStruct((S, D), BF16)],
        args=(x, g))


def _matmul(xn, w4, *, name, out_dtype=F32, plan=None):
    S, D = xn.shape
    n4 = w4.shape[2]
    tm = _row_tile(S, 2048)

    def body(xn_ref, w_ref, o_ref):
        o_ref[...] = _dot(xn_ref[...], w_ref[...]).astype(out_dtype)

    return _call(
        body, name=name, grid=(S // tm, N_CHIPS), parallel=(0,), plan=plan,
        in_specs=[pl.BlockSpec((tm, D), lambda i, j: (i, 0)),
                  pl.BlockSpec((None, D, n4), lambda i, j: (j, 0, 0))],
        out_specs=[pl.BlockSpec((tm, n4), lambda i, j: (i, j))],
        out_shape=[jax.ShapeDtypeStruct((S, N_CHIPS * n4), out_dtype)],
        args=(xn, w4))


def _segments(widths, n4):
    per_chip = [[] for _ in range(N_CHIPS)]
    c0 = 0
    for p, w in enumerate(widths):
        a = c0
        while a < c0 + w:
            k = a // n4
            b = min(c0 + w, (k + 1) * n4)
            per_chip[k].append((p, (a - c0, b - c0), (a - k * n4, b - k * n4)))
            a = b
        c0 += w
    assert c0 == N_CHIPS * n4
    return per_chip


def _piece_specs(pieces, n4, tm):
    per_chip = _segments([p.shape[1] for p in pieces], n4)
    specs, local, start = [], [[] for _ in range(N_CHIPS)], 0
    for p, arr in enumerate(pieces):
        chips = [k for k in range(N_CHIPS) if any(seg[0] == p for seg in per_chip[k])]
        lo, hi = chips[0], chips[-1]
        tiled = arr.shape[1] % n4 == 0 and start % n4 == 0
        start += arr.shape[1]
        if tiled:
            imap = lambda k, i, lo=lo, hi=hi: (jnp.where((k >= lo) & (k <= hi), i, 0), jnp.clip(k - lo, 0, hi - lo))
            specs.append(pl.BlockSpec((tm, n4), imap))
        else:
            imap = lambda k, i, lo=lo, hi=hi: (jnp.where((k >= lo) & (k <= hi), i, 0), 0)
            specs.append(pl.BlockSpec((tm, arr.shape[1]), imap))
        for k in chips:
            for q, (pa, pb), cols in per_chip[k]:
                if q == p:
                    local[k].append((p, (0, n4) if tiled else (pa, pb), cols))
    return specs, local


def _dgrad_norm(dys, w4, x, g, dres, *, name, plan=None):
    S, D = x.shape
    n4 = w4.shape[2]
    tm = _row_tile(S, 512)
    per_chip = _segments([a.shape[1] for a in dys], n4)
    n_p = len(dys)

    def body(*refs):
        dy_refs = refs[:n_p]
        w_ref, x_ref, g_ref, dres_ref, dx_ref, dg_ref = refs[n_p:]

        @pl.when(pl.program_id(0) == 0)
        def _():
            dg_ref[...] = jnp.zeros_like(dg_ref)

        dxn = None
        for k in range(N_CHIPS):
            for p, (pa, pb), (ca, cb) in per_chip[k]:
                part = _dot_nt(dy_refs[p][:, pa:pb], w_ref[k, :, ca:cb])
                dxn = part if dxn is None else dxn + part
        xf = x_ref[...]
        r = lax.rsqrt(jnp.mean(xf * xf, axis=-1, keepdims=True) + EPS)
        xhat = xf * r
        dxhat = dxn * g_ref[...]
        dx_ref[...] = dres_ref[...] + r * (dxhat - xhat * jnp.mean(dxhat * xhat, axis=-1, keepdims=True))
        dg_ref[...] += jnp.sum(dxn * xhat, axis=0, keepdims=True)

    row = lambda w: pl.BlockSpec((tm, w), lambda i: (i, 0))
    return _call(
        body, name=name, grid=(S // tm,), plan=plan,
        in_specs=[row(a.shape[1]) for a in dys]
        + [pl.BlockSpec(w4.shape, lambda i: (0, 0, 0), pipeline_mode=pl.Buffered(1)),
           row(D), pl.BlockSpec((1, D), lambda i: (0, 0)), row(D)],
        out_specs=[row(D), pl.BlockSpec((1, D), lambda i: (0, 0))],
        out_shape=[jax.ShapeDtypeStruct((S, D), F32), jax.ShapeDtypeStruct((1, D), F32)],
        args=(*dys, w4, x, g, dres))


def _wgrad(a, dys, *, name, rows, plan=None):
    S, K = a.shape
    n4 = sum(p.shape[1] for p in dys) // N_CHIPS
    tm = _row_tile(S, rows)
    ns = S // tm
    specs, local = _piece_specs(dys, n4, tm)
    n_p = len(dys)

    def body(*refs):
        a_ref = refs[0]
        dy_refs = refs[1:1 + n_p]
        o_ref, o16_ref = refs[1 + n_p:]
        n, s = pl.program_id(0), pl.program_id(1)

        @pl.when(s == 0)
        def _():
            o_ref[...] = jnp.zeros_like(o_ref)

        for k in range(N_CHIPS):
            @pl.when(n == k)
            def _():
                av = a_ref[...]
                for p, (pa, pb), (ca, cb) in local[k]:
                    o_ref[:, ca:cb] += _dot_tn(av, dy_refs[p][:, pa:pb])

        @pl.when(s == ns - 1)
        def _():
            o16_ref[...] = o_ref[...].astype(BF16)

    out = pl.BlockSpec((None, K, n4), lambda n, s: (n, 0, 0))
    return _call(
        body, name=name, grid=(N_CHIPS, ns), parallel=(0,), plan=plan,
        in_specs=[pl.BlockSpec((tm, K), lambda n, s: (s, 0))] + specs,
        out_specs=[out, out],
        out_shape=[jax.ShapeDtypeStruct((N_CHIPS, K, n4), F32), jax.ShapeDtypeStruct((N_CHIPS, K, n4), BF16)],
        args=(a, *dys))


def _tiles(x):
    return x.reshape(x.shape[0] // SUBLANES, SUBLANES, x.shape[1])


def _shift_down(xp, s):
    n = xp.shape[0] - SUBLANES
    if s == SUBLANES:
        return xp[:n, :]
    t = _tiles(xp)
    rot = pltpu.roll(t, s, 1)
    sub = lax.broadcasted_iota(jnp.int32, t.shape, 1)[1:]
    return jnp.where(sub >= s, rot[1:], rot[:-1]).reshape(n, xp.shape[1])


def _shift_up(xn, s):
    n = xn.shape[0] - SUBLANES
    if s == SUBLANES:
        return xn[SUBLANES:, :]
    t = _tiles(xn)
    rot = pltpu.roll(t, SUBLANES - s, 1)
    sub = lax.broadcasted_iota(jnp.int32, t.shape, 1)[1:]
    return jnp.where(sub < SUBLANES - s, rot[:-1], rot[1:]).reshape(n, xn.shape[1])


def _pooled(u, halo, first_tile, row0):
    T = u.shape[0]
    halo = jnp.where(first_tile, 0.0, halo)
    pad = jnp.zeros((SUBLANES, u.shape[1]), F32)
    up = jnp.concatenate([pad, halo, u], axis=0)
    t1 = (row0 + lax.broadcasted_iota(jnp.int32, (T, 1), 0) + 1).astype(F32)
    outs = []
    for gi, w in enumerate(POOL_WINDOWS):
        s = up[:, gi * LANES:(gi + 1) * LANES]
        k = 1
        while k < w:
            if k < SUBLANES:
                s = jnp.concatenate([s[:SUBLANES, :], s[SUBLANES:, :] + _shift_down(s, k)], axis=0)
            else:
                s = s[SUBLANES:, :] + _shift_down(s, k)
            k *= 2
        s = s[-T:, :]
        inv = 1.0 / jnp.minimum(t1, float(w))
        outs.append(s * inv - u[:, gi * LANES:(gi + 1) * LANES])
    return outs


def _pool_fwd(z, pool_w, pool_scale, *, name):
    S = z.shape[0]
    P = pool_scale.shape[1]
    T = _row_tile(S, 512)
    hb = T // HALO_POOL

    def body(u_ref, halo_ref, pw_ref, ps_ref, o_ref):
        i = pl.program_id(0)
        pooled = _pooled(u_ref[...], halo_ref[...], i == 0, i * T)
        for gi in range(len(POOL_WINDOWS)):
            mixed = _dot(pooled[gi].astype(BF16), pw_ref[gi].astype(BF16))
            cols = slice(gi * LANES, (gi + 1) * LANES)
            o_ref[:, cols] = (mixed * ps_ref[:, cols]).astype(BF16)

    return _call(
        body, name=name, grid=(S // T,), parallel=(0,),
        in_specs=[pl.BlockSpec((T, P), lambda i: (i, 0)),
                  pl.BlockSpec((HALO_POOL, P), lambda i: (jnp.maximum(i * hb - 1, 0), 0)),
                  pl.BlockSpec(pool_w.shape, lambda i: (0, 0, 0)),
                  pl.BlockSpec((1, P), lambda i: (0, 0))],
        out_specs=[pl.BlockSpec((T, P), lambda i: (i, 0))],
        out_shape=[jax.ShapeDtypeStruct((S, P), BF16)],
        args=(z, z, pool_w, pool_scale))[0][0]


def _pool_bwd(z, dpm, pool_w, pool_scale, *, name):
    S, P = dpm.shape
    T = _row_tile(S, 512)
    hb = T // HALO_POOL
    nt = S // T

    def body(u_ref, halo_ref, d_ref, dnext_ref, pw_ref, ps_ref, du_ref, dpw_ref, dps_ref):
        i = pl.program_id(0)

        @pl.when(i == 0)
        def _():
            dpw_ref[...] = jnp.zeros_like(dpw_ref)
            dps_ref[...] = jnp.zeros_like(dps_ref)

        pooled = _pooled(u_ref[...], halo_ref[...], i == 0, i * T)
        dnext = jnp.where(i == nt - 1, 0.0, dnext_ref[...])
        pad = jnp.zeros((SUBLANES, P), F32)
        dext = jnp.concatenate([d_ref[...], dnext, pad], axis=0)
        t1 = (i * T + lax.broadcasted_iota(jnp.int32, (T + HALO_POOL + SUBLANES, 1), 0) + 1).astype(F32)
        for gi, w in enumerate(POOL_WINDOWS):
            cols = slice(gi * LANES, (gi + 1) * LANES)
            pw = pw_ref[gi].astype(BF16)
            pg = pooled[gi].astype(BF16)
            mixed = _dot(pg, pw)
            dps_ref[:, cols] += jnp.sum(d_ref[:, cols] * mixed, axis=0, keepdims=True)
            dmixed = (dext[:, cols] * ps_ref[:, cols]).astype(BF16)
            dpw_ref[gi] += _dot_tn(pg, dmixed[:T, :])
            dpooled = _dot_nt(dmixed, pw)
            e = dpooled * (1.0 / jnp.minimum(t1, float(w)))
            k = 1
            while k < w:
                if k < SUBLANES:
                    e = jnp.concatenate([e[:-SUBLANES, :] + _shift_up(e, k), e[-SUBLANES:, :]], axis=0)
                else:
                    e = e[:-SUBLANES, :] + _shift_up(e, k)
                k *= 2
            du_ref[:, cols] = (e[:T, :] - dpooled[:T, :]).astype(BF16)

    return _call(
        body, name=name, grid=(nt,),
        in_specs=[pl.BlockSpec((T, P), lambda i: (i, 0)),
                  pl.BlockSpec((HALO_POOL, P), lambda i: (jnp.maximum(i * hb - 1, 0), 0)),
                  pl.BlockSpec((T, P), lambda i: (i, 0)),
                  pl.BlockSpec((HALO_POOL, P), lambda i: (jnp.minimum((i + 1) * hb, S // HALO_POOL - 1), 0)),
                  pl.BlockSpec(pool_w.shape, lambda i: (0, 0, 0)),
                  pl.BlockSpec((1, P), lambda i: (0, 0))],
        out_specs=[pl.BlockSpec((T, P), lambda i: (i, 0)),
                   pl.BlockSpec(pool_w.shape, lambda i: (0, 0, 0)),
                   pl.BlockSpec((1, P), lambda i: (0, 0))],
        out_shape=[jax.ShapeDtypeStruct((S, P), BF16),
                   jax.ShapeDtypeStruct(pool_w.shape, F32),
                   jax.ShapeDtypeStruct((1, P), F32)],
        args=(z, z, dpm, dpm, pool_w, pool_scale))[0]


def _cumsum_rows(x):
    n = x.shape[0]
    row = lax.broadcasted_iota(jnp.int32, x.shape, 0)
    s = 1
    while s < n:
        x = x + jnp.where(row >= s, pltpu.roll(x, s, 0), 0.0)
        s *= 2
    return x


def _rev_cumsum_rows(x):
    n = x.shape[0]
    row = lax.broadcasted_iota(jnp.int32, x.shape, 0)
    s = 1
    while s < n:
        x = x + jnp.where(row < n - s, pltpu.roll(x, n - s, 0), 0.0)
        s *= 2
    return x


def _chunk_prep(zq, zf, lb, b_ref):
    n_sub = CHUNK // SUB
    sq = _sigmoid(zq)
    q = zq * sq
    sf = _sigmoid(zf)
    f = lb + (1.0 - lb) * sf
    k = 1.0 - f
    b = _cumsum_rows(jnp.log(f))
    b_ref[...] = b
    shape = (SUB, b.shape[1])
    ends = [jnp.broadcast_to(b_ref[pl.ds(SUB * j + SUB - 1, 1), :], shape) for j in range(n_sub)]
    mids = [jnp.broadcast_to(b_ref[pl.ds(SUB * j + SUB // 2 - 1, 1), :], shape) for j in range(n_sub)]
    own = [b[SUB * j:SUB * (j + 1), :] for j in range(n_sub)]
    m0 = jnp.concatenate(mids, axis=0)
    e1 = jnp.concatenate(ends, axis=0)
    eq = [jnp.exp(jnp.minimum(b - m0, EXP_CLAMP))]
    for d in range(1, n_sub):
        rd = jnp.concatenate([own[j] if j < d else ends[j - d] for j in range(n_sub)], axis=0)
        eq.append(jnp.exp(b - rd))
    ek0 = jnp.exp(jnp.minimum(m0 - b, EXP_CLAMP))
    ek1 = jnp.exp(e1 - b)
    b_last = b_ref[pl.ds(CHUNK - 1, 1), :]
    return dict(q=q, k=k, f=f, sq=sq, sf=sf, b=b, eq=eq, ek0=ek0, ek1=ek1,
                eb=jnp.exp(b), ekl=jnp.exp(b_last - b), el=jnp.exp(b_last))


def _chunk_masks():
    ti = lax.broadcasted_iota(jnp.int32, (CHUNK, CHUNK), 0)
    si = lax.broadcasted_iota(jnp.int32, (CHUNK, CHUNK), 1)
    shift = SUB.bit_length() - 1
    dsub = jnp.right_shift(ti, shift) - jnp.right_shift(si, shift)
    masks = [(dsub == 0) & (si <= ti)]
    masks += [dsub == d for d in range(1, CHUNK // SUB)]
    return masks


def _chunk_attn(p, masks):
    qd = [(p["q"] * e).astype(BF16) for e in p["eq"]]
    k0 = (p["k"] * p["ek0"]).astype(BF16)
    k1 = (p["k"] * p["ek1"]).astype(BF16)
    a = jnp.where(masks[0], _dot_nt(qd[0], k0), 0.0)
    for d in range(1, len(masks)):
        a = jnp.where(masks[d], _dot_nt(qd[d], k1), a)
    return a, qd, k0, k1


def _hgrn_fwd(z, lb, norm_g, *, name, plan=None):
    S = z.shape[0]
    HW = lb.shape[1]
    NH = HW // LANES
    T = _row_tile(S, 512)
    nc = T // CHUNK

    def body(zq_ref, zf_ref, zi_ref, zo_ref, lb_ref, ng_ref, o_ref, of_ref, st_ref, s_scr, b_scr):
        @pl.when(pl.program_id(0) == 0)
        def _():
            s_scr[...] = jnp.zeros_like(s_scr)

        ng = ng_ref[...]
        masks = _chunk_masks()

        def chunk(c, carry):
            rows = pl.ds(pl.multiple_of(c * CHUNK, CHUNK), CHUNK)
            for h in range(NH):
                cols = slice(h * LANES, (h + 1) * LANES)
                p = _chunk_prep(zq_ref[rows, cols], zf_ref[rows, cols], lb_ref[:, cols], b_scr.at[h])
                v = zi_ref[rows, cols].astype(BF16)
                zo = zo_ref[rows, cols]
                st = s_scr[h]
                st_ref[c, h] = st
                a, _, _, _ = _chunk_attn(p, masks)
                o = _dot(a.astype(BF16), v) + _dot_nt((p["q"] * p["eb"]).astype(BF16), st.astype(BF16))
                s_scr[h] = st * p["el"] + _dot_tn(v, (p["k"] * p["ekl"]).astype(BF16))
                o_ref[rows, cols] = o
                r = lax.rsqrt(jnp.mean(o * o, axis=-1, keepdims=True) + EPS)
                of_ref[rows, cols] = (o * r * ng * (zo * _sigmoid(zo))).astype(BF16)
            return carry

        lax.fori_loop(0, nc, chunk, 0, unroll=4)

    part = lambda k: pl.BlockSpec((T, HW), lambda i, k=k: (i, k))
    return _call(
        body, name=name, grid=(S // T,), plan=plan,
        in_specs=[part(1), part(2), part(3), part(4),
                  pl.BlockSpec((1, HW), lambda i: (0, 0)), pl.BlockSpec((1, LANES), lambda i: (0, 0))],
        out_specs=[pl.BlockSpec((T, HW), lambda i: (i, 0)), pl.BlockSpec((T, HW), lambda i: (i, 0)),
                   pl.BlockSpec((nc, NH, LANES, LANES), lambda i: (i, 0, 0, 0))],
        out_shape=[jax.ShapeDtypeStruct((S, HW), F32), jax.ShapeDtypeStruct((S, HW), BF16),
                   jax.ShapeDtypeStruct((S // CHUNK, NH, LANES, LANES), F32)],
        scratch=[pltpu.VMEM((NH, LANES, LANES), F32), pltpu.VMEM((NH, CHUNK, LANES), F32)],
        args=(z, z, z, z, lb, norm_g))


def _hgrn_bwd(z, lb, norm_g, o_raw, states, dof, *, name, plan=None):
    S = z.shape[0]
    HW = lb.shape[1]
    NH = HW // LANES
    T = _row_tile(S, 512)
    nc = T // CHUNK
    nt = S // T

    def body(zq_ref, zf_ref, zi_ref, zo_ref, lb_ref, ng_ref, o_ref, st_ref, dof_ref,
             dzq_ref, dzf_ref, dzi_ref, dzo_ref, dlb_ref, dng_ref, ds_scr, b_scr):
        @pl.when(pl.program_id(0) == 0)
        def _():
            ds_scr[...] = jnp.zeros_like(ds_scr)
            dlb_ref[...] = jnp.zeros_like(dlb_ref)
            dng_ref[...] = jnp.zeros_like(dng_ref)

        ng = ng_ref[...]
        masks = _chunk_masks()
        last_row = lax.broadcasted_iota(jnp.int32, (CHUNK, 1), 0) == CHUNK - 1

        def chunk(cr, carry):
            c = nc - 1 - cr
            rows = pl.ds(pl.multiple_of(c * CHUNK, CHUNK), CHUNK)
            for h in range(NH):
                cols = slice(h * LANES, (h + 1) * LANES)
                lbv = lb_ref[:, cols]
                zq, zf, zo = zq_ref[rows, cols], zf_ref[rows, cols], zo_ref[rows, cols]
                o = o_ref[rows, cols]
                dof_c = dof_ref[rows, cols]
                st = st_ref[c, h]
                dst = ds_scr[h]

                so = _sigmoid(zo)
                r = lax.rsqrt(jnp.mean(o * o, axis=-1, keepdims=True) + EPS)
                ohat = o * r
                d_on = dof_c * (zo * so)
                dzo_ref[rows, cols] = (dof_c * ohat * ng * _dsilu(zo, so)).astype(BF16)
                dng_ref[:, cols] += jnp.sum(d_on * ohat, axis=0, keepdims=True)
                dohat = d_on * ng
                do = (r * (dohat - ohat * jnp.mean(dohat * ohat, axis=-1, keepdims=True))).astype(BF16)

                p = _chunk_prep(zq, zf, lbv, b_scr.at[h])
                q, k = p["q"], p["k"]
                v = zi_ref[rows, cols].astype(BF16)
                a, qd, k0, k1 = _chunk_attn(p, masks)
                ktl = (k * p["ekl"]).astype(BF16)
                dstb = dst.astype(BF16)

                da = _dot_nt(do, v)
                dzi_ref[rows, cols] = (_dot_tn(a.astype(BF16), do) + _dot_nt(ktl, dstb)).astype(BF16)

                da0 = jnp.where(masks[0], da, 0.0).astype(BF16)
                rq = _dot(da0, k0)
                rk0 = _dot_tn(da0, qd[0])
                dq = rq * p["eq"][0]
                db = qd[0].astype(F32) * rq - k0.astype(F32) * rk0
                rk1 = jnp.zeros_like(rk0)
                for d in range(1, len(masks)):
                    dad = jnp.where(masks[d], da, 0.0).astype(BF16)
                    rq = _dot(dad, k1)
                    dq = dq + rq * p["eq"][d]
                    db = db + qd[d].astype(F32) * rq
                    rk1 = rk1 + _dot_tn(dad, qd[d])
                dk = rk0 * p["ek0"] + rk1 * p["ek1"]
                db = db - k1.astype(F32) * rk1
                qe = (q * p["eb"]).astype(BF16)
                rq = _dot(do, st.astype(BF16))
                dq = dq + rq * p["eb"]
                db = db + qe.astype(F32) * rq
                rk = _dot(v, dstb)
                dk = dk + rk * p["ekl"]
                db = db - ktl.astype(F32) * rk

                st_new = st * p["el"] + _dot_tn(v, ktl)
                db = db + jnp.where(last_row, jnp.sum(dstb.astype(F32) * st_new, axis=0, keepdims=True), 0.0)
                dg = _rev_cumsum_rows(db)
                ds_scr[h] = dst * p["el"] + _dot_tn(do, qe)

                dzq_ref[rows, cols] = (dq * _dsilu(zq, p["sq"])).astype(BF16)
                df = dg / p["f"] - dk
                sf = p["sf"]
                dzf_ref[rows, cols] = (df * (1.0 - lbv) * sf * (1.0 - sf)).astype(BF16)
                dlb_ref[:, cols] += jnp.sum(df * (1.0 - sf), axis=0, keepdims=True)
            return carry

        lax.fori_loop(0, nc, chunk, 0, unroll=4)

    rev = lambda i: nt - 1 - i
    part = lambda k: pl.BlockSpec((T, HW), lambda i, k=k: (rev(i), k))
    blk = pl.BlockSpec((T, HW), lambda i: (rev(i), 0))
    vec = pl.BlockSpec((1, HW), lambda i: (0, 0))
    return _call(
        body, name=name, grid=(nt,), plan=plan,
        in_specs=[part(1), part(2), part(3), part(4), vec, pl.BlockSpec((1, LANES), lambda i: (0, 0)),
                  blk, pl.BlockSpec((nc, NH, LANES, LANES), lambda i: (rev(i), 0, 0, 0)), blk],
        out_specs=[blk, blk, blk, blk, vec, vec],
        out_shape=[jax.ShapeDtypeStruct((S, HW), BF16)] * 4 + [jax.ShapeDtypeStruct((1, HW), F32)] * 2,
        scratch=[pltpu.VMEM((NH, LANES, LANES), F32), pltpu.VMEM((NH, CHUNK, LANES), F32)],
        args=(z, z, z, z, lb, norm_g, o_raw, states, dof))


def _gate_specs(T, D):
    half = D // 2
    first = (5 * half) // half
    return [pl.BlockSpec((T, half), lambda i, k=k: (i, first + k)) for k in range(4)]


def _gates(zg_refs, bg_ref, D):
    half = D // 2
    za = jnp.concatenate([zg_refs[0][...], zg_refs[1][...]], axis=1) + bg_ref[:, :D]
    zb = jnp.concatenate([zg_refs[2][...], zg_refs[3][...]], axis=1) + bg_ref[:, D:]
    return _sigmoid(za), _sigmoid(zb)


def _mix_fwd(x, pm, of, z, b_gate, w_pa4, w_pb4, w_o4, g_next, *, name, plan=None):
    S, D = x.shape
    P = pm.shape[1]
    T = _row_tile(S, 512)

    def body(x_ref, pm_ref, of_ref, g0, g1, g2, g3, bg_ref, wpa_ref, wpb_ref, wo_ref, gn_ref,
             xo_ref, ya_ref, yb_ref, xn_ref):
        pmv, ofv = pm_ref[...], of_ref[...]
        ya = jnp.concatenate([_dot(pmv, wpa_ref[k]) for k in range(N_CHIPS)], axis=1)
        yb = jnp.concatenate([_dot(ofv, wpb_ref[k]) for k in range(N_CHIPS)], axis=1)
        ga, gb = _gates((g0, g1, g2, g3), bg_ref, D)
        merged = (ga * ya + gb * yb).astype(BF16)
        x_mid = x_ref[...] + _dot(merged, wo_ref[...].reshape(D, D))
        xo_ref[...] = x_mid
        xn_ref[...] = _rms(x_mid, gn_ref[...])
        ya_ref[...] = ya.astype(BF16)
        yb_ref[...] = yb.astype(BF16)

    row = lambda w: pl.BlockSpec((T, w), lambda i: (i, 0))
    full = lambda a: pl.BlockSpec(a.shape, lambda i: (0,) * a.ndim)
    return _call(
        body, name=name, grid=(S // T,), parallel=(0,), plan=plan,
        in_specs=[row(D), row(P), row(P)] + _gate_specs(T, D) + [full(b_gate), full(w_pa4), full(w_pb4), full(w_o4),
                                                                  full(g_next)],
        out_specs=[row(D), row(D), row(D), row(D)],
        out_shape=[jax.ShapeDtypeStruct((S, D), F32), jax.ShapeDtypeStruct((S, D), BF16),
                   jax.ShapeDtypeStruct((S, D), BF16), jax.ShapeDtypeStruct((S, D), BF16)],
        args=(x, pm, of, z, z, z, z, b_gate, w_pa4, w_pb4, w_o4, g_next))


def _mix_bwd(dxm, ya, yb, z, b_gate, pm, of, w_pa4, w_pb4, w_o4, *, name, plan=None):
    S, D = dxm.shape
    P = pm.shape[1]
    q4 = D // N_CHIPS
    T = _row_tile(S, 256)
    nt = S // T

    def body(dx_ref, ya_ref, yb_ref, g0, g1, g2, g3, bg_ref, pm_ref, of_ref, wpa_ref, wpb_ref, wo_ref,
             dzg_ref, dpm_ref, dof_ref, dwo_ref, dwpa_ref, dwpb_ref, dbg_ref, dwo16_ref, dwpa16_ref, dwpb16_ref):
        i = pl.program_id(0)

        @pl.when(i == 0)
        def _():
            dwo_ref[...] = jnp.zeros_like(dwo_ref)
            dwpa_ref[...] = jnp.zeros_like(dwpa_ref)
            dwpb_ref[...] = jnp.zeros_like(dwpb_ref)
            dbg_ref[...] = jnp.zeros_like(dbg_ref)

        dxb = dx_ref[...].astype(BF16)
        ya = ya_ref[...].astype(F32)
        yb = yb_ref[...].astype(F32)
        ga, gb = _gates((g0, g1, g2, g3), bg_ref, D)
        merged = (ga * ya + gb * yb).astype(BF16)
        dwo_ref[...] += _dot_tn(merged, dxb).reshape(N_CHIPS, q4, D)
        dm = _dot_nt(dxb, wo_ref[...].reshape(D, D))
        dza = dm * ya * ga * (1.0 - ga)
        dzb = dm * yb * gb * (1.0 - gb)
        dzg_ref[:, :D] = dza.astype(BF16)
        dzg_ref[:, D:] = dzb.astype(BF16)
        dbg_ref[:, :D] += jnp.sum(dza, axis=0, keepdims=True)
        dbg_ref[:, D:] += jnp.sum(dzb, axis=0, keepdims=True)
        dya = (dm * ga).astype(BF16)
        dyb = (dm * gb).astype(BF16)
        pmv, ofv = pm_ref[...], of_ref[...]
        dpm = jnp.zeros((T, P), F32)
        dof = jnp.zeros((T, P), F32)
        for k in range(N_CHIPS):
            cols = slice(k * q4, (k + 1) * q4)
            dwpa_ref[k] += _dot_tn(pmv, dya[:, cols])
            dwpb_ref[k] += _dot_tn(ofv, dyb[:, cols])
            dpm = dpm + _dot_nt(dya[:, cols], wpa_ref[k])
            dof = dof + _dot_nt(dyb[:, cols], wpb_ref[k])
        dpm_ref[...] = dpm
        dof_ref[...] = dof

        @pl.when(i == nt - 1)
        def _():
            dwo16_ref[...] = dwo_ref[...].astype(BF16)
            dwpa16_ref[...] = dwpa_ref[...].astype(BF16)
            dwpb16_ref[...] = dwpb_ref[...].astype(BF16)

    row = lambda w: pl.BlockSpec((T, w), lambda i: (i, 0))
    full = lambda a: pl.BlockSpec(a.shape, lambda i: (0,) * a.ndim)
    like = lambda a, dt: jax.ShapeDtypeStruct(a.shape, dt)
    return _call(
        body, name=name, grid=(nt,), plan=plan,
        in_specs=[row(D), row(D), row(D)] + _gate_specs(T, D) + [full(b_gate), row(P), row(P),
                                                                  full(w_pa4), full(w_pb4), full(w_o4)],
        out_specs=[row(2 * D), row(P), row(P), full(w_o4), full(w_pa4), full(w_pb4), full(b_gate),
                   full(w_o4), full(w_pa4), full(w_pb4)],
        out_shape=[jax.ShapeDtypeStruct((S, 2 * D), BF16), jax.ShapeDtypeStruct((S, P), F32),
                   jax.ShapeDtypeStruct((S, P), F32), like(w_o4, F32), like(w_pa4, F32), like(w_pb4, F32),
                   like(b_gate, F32), like(w_o4, BF16), like(w_pa4, BF16), like(w_pb4, BF16)],
        args=(dxm, ya, yb, z, z, z, z, b_gate, pm, of, w_pa4, w_pb4, w_o4))


def _conv3(h, halo, first_tile, cw, cb):
    h = h.astype(F32)
    halo = jnp.where(first_tile, 0.0, halo.astype(F32)[-SUBLANES:, :])
    hp = jnp.concatenate([halo, h], axis=0)
    h1 = _shift_down(hp, 1)
    h2 = _shift_down(hp, 2)
    return cw[0:1, :] * h2 + cw[1:2, :] * h1 + cw[2:3, :] * h + cb, h1, h2


def _ffn_down_fwd(h, conv_w, conv_b, w_down4, x, g_next, *, name, plan=None):
    S, F2 = h.shape
    _, f4, D = w_down4.shape
    T = _row_tile(S, 512)
    nf = 2
    tf = 2 * f4
    hb = T // HALO_CONV

    def body(hv_ref, hg_ref, pv_ref, pg_ref, cwv_ref, cwg_ref, cbv_ref, cbg_ref, wd_ref, x_ref, gn_ref,
             o_ref, val_ref, gate_ref, xn_ref, acc_ref):
        i, f = pl.program_id(0), pl.program_id(1)

        @pl.when(f == 0)
        def _():
            acc_ref[...] = jnp.zeros_like(acc_ref)

        val, _, _ = _conv3(hv_ref[...], pv_ref[...], i == 0, cwv_ref[...], cbv_ref[...])
        gate, _, _ = _conv3(hg_ref[...], pg_ref[...], i == 0, cwg_ref[...], cbg_ref[...])
        val_ref[...] = val.astype(BF16)
        gate_ref[...] = gate.astype(BF16)
        a = (gate * _sigmoid(gate) * val).astype(BF16)
        acc_ref[...] += _dot(a, wd_ref[...].reshape(tf, D))

        @pl.when(f == nf - 1)
        def _():
            x_out = x_ref[...] + acc_ref[...]
            o_ref[...] = x_out
            xn_ref[...] = _rms(x_out, gn_ref[...])

    prev = lambda i: jnp.maximum(i * hb - 1, 0)
    return _call(
        body, name=name, grid=(S // T, nf), parallel=(0,), plan=plan,
        in_specs=[pl.BlockSpec((T, tf), lambda i, f: (i, f)),
                  pl.BlockSpec((T, tf), lambda i, f: (i, nf + f)),
                  pl.BlockSpec((HALO_CONV, tf), lambda i, f: (prev(i), f)),
                  pl.BlockSpec((HALO_CONV, tf), lambda i, f: (prev(i), nf + f)),
                  pl.BlockSpec((3, tf), lambda i, f: (0, f)),
                  pl.BlockSpec((3, tf), lambda i, f: (0, nf + f)),
                  pl.BlockSpec((1, tf), lambda i, f: (0, f)),
                  pl.BlockSpec((1, tf), lambda i, f: (0, nf + f)),
                  pl.BlockSpec((2, f4, D), lambda i, f: (f, 0, 0)),
                  pl.BlockSpec((T, D), lambda i, f: (i, 0)),
                  pl.BlockSpec((1, D), lambda i, f: (0, 0))],
        out_specs=[pl.BlockSpec((T, D), lambda i, f: (i, 0)),
                   pl.BlockSpec((T, tf), lambda i, f: (i, f)),
                   pl.BlockSpec((T, tf), lambda i, f: (i, f)),
                   pl.BlockSpec((T, D), lambda i, f: (i, 0))],
        out_shape=[jax.ShapeDtypeStruct((S, D), F32), jax.ShapeDtypeStruct((S, N_CHIPS * f4), BF16),
                   jax.ShapeDtypeStruct((S, N_CHIPS * f4), BF16), jax.ShapeDtypeStruct((S, D), BF16)],
        scratch=[pltpu.VMEM((T, D), F32)],
        args=(h, h, h, h, conv_w, conv_w, conv_b, conv_b, w_down4, x, g_next))


def _ffn_down_bwd(dxo, h, val16, gate16, conv_w, w_down4, *, name, plan=None):
    S, F2 = h.shape
    _, f4, D = w_down4.shape
    F = N_CHIPS * f4
    T = _row_tile(S, 512)
    nf = 2
    tf = 2 * f4
    nt = S // T

    def body(dx_ref, hv_ref, hg_ref, val_ref, gate_ref, cwv_ref, cwg_ref, wd_ref,
             dhv_ref, dhg_ref, dwd_ref, dwd16_ref, dcwv_ref, dcwg_ref, dcbv_ref, dcbg_ref, cv_scr, cg_scr):
        i = pl.program_id(1)

        @pl.when(i == 0)
        def _():
            cv_scr[...] = jnp.zeros_like(cv_scr)
            cg_scr[...] = jnp.zeros_like(cg_scr)
            dwd_ref[...] = jnp.zeros_like(dwd_ref)
            dcwv_ref[...] = jnp.zeros_like(dcwv_ref)
            dcwg_ref[...] = jnp.zeros_like(dcwg_ref)
            dcbv_ref[...] = jnp.zeros_like(dcbv_ref)
            dcbg_ref[...] = jnp.zeros_like(dcbg_ref)

        dxb = dx_ref[...].astype(BF16)
        val = val_ref[...].astype(F32)
        gate = gate_ref[...].astype(F32)
        sg = _sigmoid(gate)
        sil = gate * sg
        dwd_ref[...] += _dot_tn((sil * val).astype(BF16), dxb).reshape(2, f4, D)
        da = _dot_nt(dxb, wd_ref[...].reshape(tf, D))

        def conv_bwd(dhc, h0, cw, c_scr, dh_ref, dcw_ref, dcb_ref):
            ext = jnp.concatenate([dhc, c_scr[...]], axis=0)
            n1 = _shift_up(ext, 1)
            n2 = _shift_up(ext, 2)
            dh_ref[...] = (cw[2:3, :] * dhc + cw[1:2, :] * n1 + cw[0:1, :] * n2).astype(BF16)
            c_scr[...] = dhc[:SUBLANES, :]
            dcw_ref[0:1, :] += jnp.sum(n2 * h0, axis=0, keepdims=True)
            dcw_ref[1:2, :] += jnp.sum(n1 * h0, axis=0, keepdims=True)
            dcw_ref[2:3, :] += jnp.sum(dhc * h0, axis=0, keepdims=True)
            dcb_ref[...] += jnp.sum(dhc, axis=0, keepdims=True)

        conv_bwd(da * sil, hv_ref[...].astype(F32), cwv_ref[...], cv_scr, dhv_ref, dcwv_ref, dcbv_ref)
        conv_bwd(da * val * _dsilu(gate, sg), hg_ref[...].astype(F32), cwg_ref[...], cg_scr, dhg_ref, dcwg_ref,
                 dcbg_ref)

        @pl.when(i == nt - 1)
        def _():
            dwd16_ref[...] = dwd_ref[...].astype(BF16)

    rev = lambda i: nt - 1 - i
    wd_spec = pl.BlockSpec((2, f4, D), lambda f, i: (f, 0, 0))
    return _call(
        body, name=name, grid=(nf, nt), plan=plan,
        in_specs=[pl.BlockSpec((T, D), lambda f, i: (rev(i), 0)),
                  pl.BlockSpec((T, tf), lambda f, i: (rev(i), f)),
                  pl.BlockSpec((T, tf), lambda f, i: (rev(i), nf + f)),
                  pl.BlockSpec((T, tf), lambda f, i: (rev(i), f)),
                  pl.BlockSpec((T, tf), lambda f, i: (rev(i), f)),
                  pl.BlockSpec((3, tf), lambda f, i: (0, f)),
                  pl.BlockSpec((3, tf), lambda f, i: (0, nf + f)),
                  wd_spec],
        out_specs=[pl.BlockSpec((T, tf), lambda f, i: (rev(i), f)),
                   pl.BlockSpec((T, tf), lambda f, i: (rev(i), f)),
                   wd_spec, wd_spec,
                   pl.BlockSpec((3, tf), lambda f, i: (0, f)),
                   pl.BlockSpec((3, tf), lambda f, i: (0, f)),
                   pl.BlockSpec((1, tf), lambda f, i: (0, f)),
                   pl.BlockSpec((1, tf), lambda f, i: (0, f))],
        out_shape=[jax.ShapeDtypeStruct((S, F), BF16), jax.ShapeDtypeStruct((S, F), BF16),
                   jax.ShapeDtypeStruct((N_CHIPS, f4, D), F32), jax.ShapeDtypeStruct((N_CHIPS, f4, D), BF16),
                   jax.ShapeDtypeStruct((3, F), F32), jax.ShapeDtypeStruct((3, F), F32),
                   jax.ShapeDtypeStruct((1, F), F32), jax.ShapeDtypeStruct((1, F), F32)],
        scratch=[pltpu.VMEM((SUBLANES, tf), F32), pltpu.VMEM((SUBLANES, tf), F32)],
        args=(dxo, h, h, val16, gate16, conv_w, conv_w, w_down4))


def _final_loss(x, g, target, *, name):
    S, D = x.shape
    T = _row_tile(S, 512)

    def body(x_ref, g_ref, t_ref, loss_ref, dx_ref, dg_ref):
        @pl.when(pl.program_id(0) == 0)
        def _():
            loss_ref[...] = jnp.zeros_like(loss_ref)
            dg_ref[...] = jnp.zeros_like(dg_ref)

        xf = x_ref[...]
        r = lax.rsqrt(jnp.mean(xf * xf, axis=-1, keepdims=True) + EPS)
        xhat = xf * r
        err = xhat * g_ref[...] - t_ref[...]
        loss_ref[...] += jnp.sum(err * err, axis=0, keepdims=True) * (0.5 / D)
        dy = err * (1.0 / D)
        dxhat = dy * g_ref[...]
        dx_ref[...] = r * (dxhat - xhat * jnp.mean(dxhat * xhat, axis=-1, keepdims=True))
        dg_ref[...] += jnp.sum(dy * xhat, axis=0, keepdims=True)

    return _call(
        body, name=name, grid=(S // T,),
        in_specs=[pl.BlockSpec((T, D), lambda i: (i, 0)), pl.BlockSpec((1, D), lambda i: (0, 0)),
                  pl.BlockSpec((T, D), lambda i: (i, 0))],
        out_specs=[pl.BlockSpec((1, D), lambda i: (0, 0)), pl.BlockSpec((T, D), lambda i: (i, 0)),
                   pl.BlockSpec((1, D), lambda i: (0, 0))],
        out_shape=[jax.ShapeDtypeStruct((1, D), F32), jax.ShapeDtypeStruct((S, D), F32),
                   jax.ShapeDtypeStruct((1, D), F32)],
        args=(x, g, target))[0]


BIG = ("w_in", "w_pa", "w_pb", "w_o", "w_up", "w_down")
SMALL = ("norm1_g", "b_gate", "pool_w", "pool_scale", "lb_logits", "hgrn_norm_g", "norm2_g", "conv_b", "final_g")
WEIGHTS = ("norm1_g", "w_in", "b_gate", "pool_w", "pool_scale", "lb_logits", "hgrn_norm_g", "w_pa", "w_pb", "w_o",
           "norm2_g", "w_up", "conv_w", "conv_b", "w_down", "final_g")


def _lower_bounds(lb_logits):
    soft = jax.nn.softmax(lb_logits.astype(F32), axis=0)
    cum = jnp.cumsum(soft, axis=0)
    return cum - cum[0:1]


def _step(x, target, sm, wts, shards=None):
    L = sm["norm1_g"].shape[0]
    wts = dict(wts)
    dist = shards is not None
    lbs, lb_vjp = jax.vjp(_lower_bounds, sm["lb_logits"])
    row = lambda a: a.reshape(1, -1)
    conv_w = sm.get("conv_w")

    def gather(names_layers, with_conv=False):
        items = [(shards[n], "rows", l) for n, l in names_layers]
        if with_conv:
            items.append((shards["conv_w"], "layer", None))
        return _GatherPlan(items)

    def landed(names_layers, outs):
        for key, arr in zip(names_layers, outs):
            wts[key] = arr

    own = {"in_proj": ("w_pa", "w_pb", "w_o", "w_up"), "hgrn_fwd": ("w_down",)}
    ahead = {"hgrn_fwd": ("w_pa", "w_pb", "w_o"), "mix_fwd": ("w_down",), "up": ("w_in",), "down_fwd": ("w_up",)}

    def riders(l, kernel):
        if not dist:
            return [], None
        keys = [(n, l) for n in own.get(kernel, ())] if l == 0 else []
        keys += [(n, l + 1) for n in ahead.get(kernel, ())] if l + 1 < L else []
        with_conv = l == 0 and kernel == "hgrn_fwd"
        return keys, (gather(keys, with_conv) if keys or with_conv else None)

    keys = [("w_in", 0)] if dist else []
    (xn1,), got = _rmsnorm(x, row(sm["norm1_g"][0]), name="norm_in", plan=gather(keys) if keys else None)
    landed(keys, got)

    saved = []
    for l in range(L):
        keys, plan = riders(l, "in_proj")
        (z,), got = _matmul(xn1, wts[("w_in", l)], name=f"in_proj_{l}", plan=plan)
        landed(keys, got)
        pm = _pool_fwd(z, sm["pool_w"][l], row(sm["pool_scale"][l]), name=f"pool_fwd_{l}")
        keys, plan = riders(l, "hgrn_fwd")
        (o_raw, of, states), got = _hgrn_fwd(z, row(lbs[l]), row(sm["hgrn_norm_g"][l]), name=f"hgrn_fwd_{l}",
                                             plan=plan)
        landed(keys, got)
        if dist and l == 0:
            full = got[-1]
            conv_w = jnp.concatenate([full[:, k] for k in range(N_CHIPS)], axis=2)
        keys, plan = riders(l, "mix_fwd")
        (x_mid, ya, yb, xn2), got = _mix_fwd(x, pm, of, z, row(sm["b_gate"][l]), wts[("w_pa", l)], wts[("w_pb", l)],
                                             wts[("w_o", l)], row(sm["norm2_g"][l]), name=f"mix_fwd_{l}", plan=plan)
        landed(keys, got)
        keys, plan = riders(l, "up")
        (h,), got = _matmul(xn2, wts[("w_up", l)], name=f"up_{l}", out_dtype=BF16, plan=plan)
        landed(keys, got)
        keys, plan = riders(l, "down_fwd")
        g_next = row(sm["norm1_g"][l + 1]) if l + 1 < L else row(sm["final_g"])
        (x_out, val16, gate16, xn_next), got = _ffn_down_fwd(
            h, conv_w[l], row(sm["conv_b"][l]), wts[("w_down", l)], x_mid, g_next, name=f"down_fwd_{l}", plan=plan)
        landed(keys, got)
        saved.append(dict(x=x, xn1=xn1, z=z, pm=pm, o_raw=o_raw, of=of, states=states,
                          x_mid=x_mid, ya=ya, yb=yb, xn2=xn2, h=h, val16=val16, gate16=gate16))
        x, xn1 = x_out, xn_next

    loss_cols, dx, d_final_g = _final_loss(x, row(sm["final_g"]), target, name="final_loss")

    small = {k: [None] * L for k in ("norm1_g", "b_gate", "pool_w", "pool_scale", "hgrn_norm_g", "norm2_g",
                                     "conv_w", "conv_b")}
    big32, big16, recv = {}, {}, {}
    dlbs = [None] * L
    pending = []

    def scatter(now):
        if not (dist and pending and now):
            return [], None
        keys = list(pending)
        del pending[:]
        return keys, _ScatterPlan([big16[k] for k in keys])

    def sent(keys, outs):
        for key, arr in zip(keys, outs):
            recv[key] = arr

    def made(name, l, g32, g16):
        big32[(name, l)], big16[(name, l)] = g32, g16
        pending.append((name, l))

    for l in reversed(range(L)):
        s = saved[l]
        last = l == 0
        keys, plan = scatter(True)
        (dhv, dhg, d_wd, d_wd16, dcwv, dcwg, dcbv, dcbg), got = _ffn_down_bwd(
            dx, s["h"], s["val16"], s["gate16"], conv_w[l], wts[("w_down", l)], name=f"down_bwd_{l}", plan=plan)
        sent(keys, got)
        made("w_down", l, d_wd, d_wd16)
        small["conv_w"][l] = jnp.concatenate([dcwv, dcwg], axis=1)
        small["conv_b"][l] = jnp.concatenate([dcbv, dcbg], axis=1)[0]
        keys, plan = scatter(last)
        (d_wu, d_wu16), got = _wgrad(s["xn2"], [dhv, dhg], name=f"up_wgrad_{l}", rows=2048, plan=plan)
        sent(keys, got)
        made("w_up", l, d_wu, d_wu16)
        (dxm, dg2), _ = _dgrad_norm([dhv, dhg], wts[("w_up", l)], s["x_mid"], row(sm["norm2_g"][l]), dx,
                                    name=f"up_dgrad_{l}")
        small["norm2_g"][l] = dg2[0]

        keys, plan = scatter(last)
        (dzg, dpm, dof, d_wo, d_wpa, d_wpb, dbg, d_wo16, d_wpa16, d_wpb16), got = _mix_bwd(
            dxm, s["ya"], s["yb"], s["z"], row(sm["b_gate"][l]), s["pm"], s["of"],
            wts[("w_pa", l)], wts[("w_pb", l)], wts[("w_o", l)], name=f"mix_bwd_{l}", plan=plan)
        sent(keys, got)
        made("w_o", l, d_wo, d_wo16)
        made("w_pa", l, d_wpa, d_wpa16)
        made("w_pb", l, d_wpb, d_wpb16)
        small["b_gate"][l] = dbg[0]

        du, dpw, dps = _pool_bwd(s["z"], dpm, sm["pool_w"][l], row(sm["pool_scale"][l]), name=f"pool_bwd_{l}")
        small["pool_w"][l], small["pool_scale"][l] = dpw, dps[0]

        keys, plan = scatter(last)
        (dzq, dzf, dzi, dzo, dlb, dng), got = _hgrn_bwd(s["z"], row(lbs[l]), row(sm["hgrn_norm_g"][l]), s["o_raw"],
                                                      s["states"], dof, name=f"hgrn_bwd_{l}", plan=plan)
        sent(keys, got)
        dlbs[l] = dlb[0]
        small["hgrn_norm_g"][l] = jnp.sum(dng.reshape(-1, LANES), axis=0)

        dz = [du, dzq, dzf, dzi, dzo, dzg]
        (d_wi, d_wi16), _ = _wgrad(s["xn1"], dz, name=f"in_wgrad_{l}", rows=1024)
        made("w_in", l, d_wi, d_wi16)
        keys, plan = scatter(last)
        (dx, dg1), got = _dgrad_norm(dz, wts[("w_in", l)], s["x"], row(sm["norm1_g"][l]), dxm,
                                     name=f"in_dgrad_{l}", plan=plan)
        sent(keys, got)
        small["norm1_g"][l] = dg1[0]

    out = {k: jnp.stack(v) for k, v in small.items()}
    out["lb_logits"] = lb_vjp(jnp.stack(dlbs))[0]
    out["final_g"] = d_final_g[0]
    return loss_cols, dx, out, big32, recv


def _elementwise_rows(R, n, n_arrays):
    if 2 * n_arrays * R * n * 4 <= VMEM_LIMIT // 4 or R % 8:
        return R
    block = VMEM_LIMIT // 2 // (2 * n_arrays)
    want = 8
    while want * 2 * n * 4 <= block:
        want *= 2
    return _row_tile(R, want)


def _sum_layers(own, got, chip, *, name):
    L = len(own)
    _, r, n = own[0].shape
    T = _elementwise_rows(r, n, 6)
    nt = r // T

    def body(chip_ref, *refs):
        o_ref = refs[-1]
        l = pl.program_id(0)
        for k in range(L):
            @pl.when(l == k)
            def _():
                own_ref, got_ref = refs[2 * k], refs[2 * k + 1]
                acc = own_ref[...]
                for j in range(3):
                    acc = acc + got_ref[j].astype(F32)
                o_ref[...] = acc

    in_specs = []
    for k in range(L):
        hold = 0 if k else nt - 1
        in_specs.append(pl.BlockSpec((None, T, n), lambda l, i, c, k=k, hold=hold: (c[0], jnp.where(l == k, i, hold), 0)))
        in_specs.append(pl.BlockSpec((3, T, n), lambda l, i, c, k=k, hold=hold: (0, jnp.where(l == k, i, hold), 0)))
    grid_spec = pltpu.PrefetchScalarGridSpec(
        num_scalar_prefetch=1, grid=(L, nt), in_specs=in_specs,
        out_specs=pl.BlockSpec((None, T, n), lambda l, i, c: (l, i, 0)))
    args = [a for pair in zip(own, got) for a in pair]
    return pl.pallas_call(
        body, name=name, grid_spec=grid_spec, out_shape=jax.ShapeDtypeStruct((L, r, n), F32),
        compiler_params=pltpu.CompilerParams(dimension_semantics=("arbitrary", "arbitrary"),
                                             vmem_limit_bytes=VMEM_LIMIT),
    )(chip, *args)


def _sum_stack(first, rest, *, name):
    R, n = first.shape
    K = rest.shape[0]
    T = _elementwise_rows(R, n, K + 2)

    def body(a_ref, r_ref, o_ref):
        acc = a_ref[...]
        for j in range(K):
            acc = acc + r_ref[j].astype(F32)
        o_ref[...] = acc

    return _call(
        body, name=name, grid=(R // T,), parallel=(0,),
        in_specs=[pl.BlockSpec((T, n), lambda i: (i, 0)), pl.BlockSpec((K, T, n), lambda i: (0, i, 0))],
        out_specs=[pl.BlockSpec((T, n), lambda i: (i, 0))],
        out_shape=[jax.ShapeDtypeStruct((R, n), F32)],
        args=(first, rest))[0][0]


def _adamw(w, m, v, g_parts, *, name):
    R, n = w.shape
    n_g = len(g_parts)
    T = _elementwise_rows(R, n, 7 + n_g)

    def body(*refs):
        w_ref, m_ref, v_ref = refs[:3]
        g_refs = refs[3:3 + n_g]
        go_ref, d_ref, mo_ref, vo_ref = refs[3 + n_g:]
        g_ = g_refs[0][...]
        for r in g_refs[1:]:
            g_ = g_ + r[...]
        m_ = ADAM_B1 * m_ref[...] + (1.0 - ADAM_B1) * g_
        v_ = ADAM_B2 * v_ref[...] + (1.0 - ADAM_B2) * (g_ * g_)
        m_hat = m_ / (1.0 - ADAM_B1 ** ADAM_STEP)
        v_hat = v_ / (1.0 - ADAM_B2 ** ADAM_STEP)
        go_ref[...] = g_
        d_ref[...] = -ADAM_LR * (m_hat / (jnp.sqrt(v_hat) + ADAM_EPS) + ADAM_WD * w_ref[...])
        mo_ref[...] = m_
        vo_ref[...] = v_

    blk = pl.BlockSpec((T, n), lambda i: (i, 0))
    return _call(
        body, name=name, grid=(R // T,), parallel=(0,),
        in_specs=[blk] * (3 + n_g), out_specs=[blk] * 4,
        out_shape=[jax.ShapeDtypeStruct((R, n), F32)] * 4,
        args=(w, m, v, *g_parts))[0]


PACK_ALIGN = 8 * LANES


def _pack(pieces):
    flat = []
    for a in pieces:
        a = a.reshape(-1)
        pad = (-a.shape[0]) % PACK_ALIGN
        flat.append(jnp.pad(a, (0, pad)) if pad else a)
    return jnp.concatenate(flat).reshape(-1, LANES)


def _unpack(buf, shapes):
    flat = buf.reshape(-1)
    out, off = [], 0
    for shp in shapes:
        size = 1
        for s in shp:
            size *= s
        out.append(flat[off:off + size].reshape(shp))
        off += size + (-size) % PACK_ALIGN
    return out


def kernel(x, norm1_g, w_in, b_gate, pool_w, pool_scale, lb_logits, hgrn_norm_g, w_pa, w_pb, w_o, norm2_g, w_up, conv_w, conv_b, w_down, final_g, loss_target, m_norm1_g, m_w_in, m_b_gate, m_pool_w, m_pool_scale, m_lb_logits, m_hgrn_norm_g, m_w_pa, m_w_pb, m_w_o, m_norm2_g, m_w_up, m_conv_w, m_conv_b, m_w_down, m_final_g, v_norm1_g, v_w_in, v_b_gate, v_pool_w, v_pool_scale, v_lb_logits, v_hgrn_norm_g, v_w_pa, v_w_pb, v_w_o, v_norm2_g, v_w_up, v_conv_w, v_conv_b, v_w_down, v_final_g):
    env = dict(locals())
    w = {n: env[n] for n in WEIGHTS}
    m = {n: env["m_" + n] for n in WEIGHTS}
    v = {n: env["v_" + n] for n in WEIGHTS}
    my_chip = 2 * lax.axis_index("x") + lax.axis_index("y")
    L = w_in.shape[0]

    shards = {n: w[n].astype(BF16) for n in BIG}
    shards["conv_w"] = w["conv_w"]
    sm = {n: w[n] for n in SMALL}
    loss_cols, grad_x, g_small, big32, recv = _step(x[0], loss_target[0], sm, {}, shards)

    chip = my_chip.reshape(1).astype(jnp.int32)
    sums = [_sum_layers([big32[(n, l)] for l in range(L)], [recv[(n, l)] for l in range(L)], chip,
                        name="chip_sum_" + n) for n in BIG]
    small_names = list(SMALL)
    small_pieces = [g_small[n] for n in small_names] + [g_small["conv_w"], loss_cols]
    small_shapes = [a.shape for a in small_pieces]
    packed = _pack(small_pieces)
    Rs = packed.shape[0]
    swapped = _run_plan(_Together([_SiblingPlan(sums), _EveryonePlan(packed)]), name="tail_exchange")
    theirs, everyone = swapped[:-1], swapped[-1].reshape(8, Rs, LANES)
    g, delta, new_m, new_v = {}, {}, {}, {}
    for n, mine, other in zip(BIG, sums, theirs):
        shp = w[n].shape
        two_d = lambda a: a.reshape(-1, shp[-1])
        outs = _adamw(two_d(w[n]), two_d(m[n]), two_d(v[n]), [two_d(mine), two_d(other)], name="adamw_" + n)
        g[n], delta[n], new_m[n], new_v[n] = [a.reshape(shp) for a in outs]

    summed = _unpack(_sum_stack(everyone[0], everyone[1:], name="small_sum"), small_shapes)
    loss = jnp.sum(summed[-1])
    cshard = w["conv_w"].shape[2]
    gs = dict(zip(small_names, summed[:len(small_names)]))
    g_cw = lax.dynamic_slice_in_dim(summed[-2], my_chip * cshard, cshard, axis=2)

    sm_out = _adamw(_pack([w[n] for n in small_names]), _pack([m[n] for n in small_names]),
                    _pack([v[n] for n in small_names]), [_pack([gs[n] for n in small_names])], name="adamw_small")
    shapes = [w[n].shape for n in small_names]
    for n, g_, d_, m_, v_ in zip(small_names, *[_unpack(a, shapes) for a in sm_out]):
        g[n], delta[n], new_m[n], new_v[n] = g_, d_, m_, v_
    shp = w["conv_w"].shape
    two_d = lambda a: a.reshape(-1, shp[-1])
    outs = _adamw(two_d(w["conv_w"]), two_d(m["conv_w"]), two_d(v["conv_w"]), [two_d(g_cw)], name="adamw_conv_w")
    g["conv_w"], delta["conv_w"], new_m["conv_w"], new_v["conv_w"] = [a.reshape(shp) for a in outs]

    return (loss, grad_x[None], *[g[n] for n in WEIGHTS], *[delta[n] for n in WEIGHTS],
            *[new_m[n] for n in WEIGHTS], *[new_v[n] for n in WEIGHTS])
```

```python
import jax
import jax.numpy as jnp
from jax import lax
from jax.experimental import pallas as pl
from jax.experimental.pallas import tpu as pltpu

F32 = jnp.float32
BF16 = jnp.bfloat16

EPS = 1e-6
CHUNK = 64
SUB = 32
LANES = 128
SUBLANES = 8
POOL_WINDOWS = (2, 4, 8, 16)
HALO_POOL = 16
EXP_CLAMP = 80.0

ADAM_LR = 0.001
ADAM_B1 = 0.9
ADAM_B2 = 0.999
ADAM_EPS = 1e-08
ADAM_WD = 0.01
ADAM_STEP = 10

VMEM_LIMIT = 56 * 1024 * 1024
MESH_ID = pl.DeviceIdType.MESH
N_CHIPS = 4
ANY = pl.BlockSpec(memory_space=pl.ANY)


def _dot(a, b):
    return jnp.dot(a, b, preferred_element_type=F32)


def _dot_nt(a, b):
    return lax.dot_general(a, b, (((1,), (1,)), ((), ())), preferred_element_type=F32)


def _dot_tn(a, b):
    return lax.dot_general(a, b, (((0,), (0,)), ((), ())), preferred_element_type=F32)


def _sigmoid(x):
    return jax.nn.sigmoid(x)


def _dsilu(x, s):
    return s * (1.0 + x * (1.0 - s))


def _row_tile(rows, want):
    t = min(rows, want)
    while rows % t:
        t //= 2
    return t


def _place():
    x, y, c = lax.axis_index("x"), lax.axis_index("y"), lax.axis_index("c")
    chips = [(1 - x, y), (x, 1 - y), (1 - x, 1 - y)]
    return x, y, c, chips


def _remote(src, dst, sems, k, to):
    return pltpu.make_async_remote_copy(src_ref=src, dst_ref=dst, send_sem=sems[0].at[k], recv_sem=sems[1].at[k],
                                        device_id=to, device_id_type=MESH_ID)


class _GatherPlan:
    def __init__(self, items):
        self.items = items
        self.inputs = [a for a, _, _ in items]
        self.out_shapes = []
        for a, kind, _ in items:
            shp = (N_CHIPS,) + a.shape[1:] if kind == "rows" else (a.shape[0], N_CHIPS) + a.shape[1:]
            self.out_shapes.append(jax.ShapeDtypeStruct(shp, a.dtype))
        n = len(items)
        self.scratch = [pltpu.SemaphoreType.DMA((6 * n,)), pltpu.SemaphoreType.DMA((6 * n,)),
                        pltpu.SemaphoreType.DMA((2 * n,))]

    def _views(self, i, src, dst):
        _, kind, l = self.items[i]
        if kind == "rows":
            half = src.shape[1] // 2
            part = lambda core: src.at[l, pl.ds(core * half, half), :]
            land = lambda chip, core: dst.at[chip, pl.ds(core * half, half), :]
        else:
            part = lambda core: src.at[core]
            land = lambda chip, core: dst.at[core, chip]
        return part, land

    def start(self, srcs, dsts, sems):
        x, y, c, chips = _place()
        me = 2 * x + y
        for i, (src, dst) in enumerate(zip(srcs, dsts)):
            part, land = self._views(i, src, dst)
            for core in range(2):
                pltpu.make_async_copy(part(core), land(me, core), sems[2].at[2 * i + core]).start()
            for j, (px, py) in enumerate(chips):
                _remote(part(c), land(me, c), sems, 6 * i + j, (px, py, c)).start()

    def finish(self, srcs, dsts, sems):
        x, y, c, chips = _place()
        me = 2 * x + y
        sibling = (x, y, 1 - c)
        for i, (src, dst) in enumerate(zip(srcs, dsts)):
            part, land = self._views(i, src, dst)
            for j, (px, py) in enumerate(chips):
                got = land(2 * px + py, c)
                _remote(got, got, sems, 6 * i + j, (px, py, c)).wait_recv()
                _remote(got, got, sems, 6 * i + 3 + j, sibling).start()
        for i, (src, dst) in enumerate(zip(srcs, dsts)):
            part, land = self._views(i, src, dst)
            for j, (px, py) in enumerate(chips):
                got = land(2 * px + py, 1 - c)
                _remote(got, got, sems, 6 * i + 3 + j, sibling).wait_recv()
            for j, (px, py) in enumerate(chips):
                _remote(part(c), land(me, c), sems, 6 * i + j, (px, py, c)).wait_send()
                mine = land(2 * px + py, c)
                _remote(mine, mine, sems, 6 * i + 3 + j, sibling).wait_send()
            for core in range(2):
                pltpu.make_async_copy(part(core), land(me, core), sems[2].at[2 * i + core]).wait()


class _ScatterPlan:
    def __init__(self, items):
        self.inputs = list(items)
        self.out_shapes = [jax.ShapeDtypeStruct((3,) + a.shape[1:], a.dtype) for a in items]
        n = len(items)
        self.scratch = [pltpu.SemaphoreType.DMA((3 * n,)), pltpu.SemaphoreType.DMA((3 * n,))]

    def _copies(self, srcs, dsts, sems):
        x, y, c, chips = _place()
        return [_remote(src.at[2 * px + py], dst.at[j], sems, 3 * i + j, (px, py, c))
                for i, (src, dst) in enumerate(zip(srcs, dsts)) for j, (px, py) in enumerate(chips)]

    def start(self, srcs, dsts, sems):
        for cp in self._copies(srcs, dsts, sems):
            cp.start()

    def finish(self, srcs, dsts, sems):
        copies = self._copies(srcs, dsts, sems)
        for cp in copies:
            cp.wait_recv()
        for cp in copies:
            cp.wait_send()


class _SiblingPlan:
    def __init__(self, items):
        self.inputs = list(items)
        self.out_shapes = [jax.ShapeDtypeStruct(a.shape, a.dtype) for a in items]
        n = len(items)
        self.scratch = [pltpu.SemaphoreType.DMA((n,)), pltpu.SemaphoreType.DMA((n,))]

    def _copies(self, srcs, dsts, sems):
        x, y, c, _ = _place()
        return [_remote(src, dst, sems, i, (x, y, 1 - c)) for i, (src, dst) in enumerate(zip(srcs, dsts))]

    def start(self, srcs, dsts, sems):
        for cp in self._copies(srcs, dsts, sems):
            cp.start()

    def finish(self, srcs, dsts, sems):
        copies = self._copies(srcs, dsts, sems)
        for cp in copies:
            cp.wait_recv()
        for cp in copies:
            cp.wait_send()


class _EveryonePlan:
    def __init__(self, block):
        self.inputs = [block]
        self.m = block.shape[0]
        self.out_shapes = [jax.ShapeDtypeStruct((8 * self.m,) + block.shape[1:], block.dtype)]
        self.scratch = [pltpu.SemaphoreType.DMA((7,)), pltpu.SemaphoreType.DMA((7,)), pltpu.SemaphoreType.DMA((1,))]

    def _rows(self, dst, px, py, pc):
        return dst.at[pl.ds((4 * px + 2 * py + pc) * self.m, self.m), :]

    def start(self, srcs, dsts, sems):
        x, y, c, chips = _place()
        src, dst = srcs[0], dsts[0]
        pltpu.make_async_copy(src, self._rows(dst, x, y, c), sems[2].at[0]).start()
        _remote(src, self._rows(dst, x, y, c), sems, 0, (x, y, 1 - c)).start()
        for j, (px, py) in enumerate(chips):
            _remote(src, self._rows(dst, x, y, c), sems, 1 + j, (px, py, c)).start()

    def finish(self, srcs, dsts, sems):
        x, y, c, chips = _place()
        src, dst = srcs[0], dsts[0]
        sibling = (x, y, 1 - c)
        for j, (px, py) in enumerate(chips):
            got = self._rows(dst, px, py, c)
            _remote(got, got, sems, 1 + j, (px, py, c)).wait_recv()
            _remote(got, got, sems, 4 + j, sibling).start()
        sib = self._rows(dst, x, y, 1 - c)
        _remote(sib, sib, sems, 0, sibling).wait_recv()
        for j, (px, py) in enumerate(chips):
            got = self._rows(dst, px, py, 1 - c)
            _remote(got, got, sems, 4 + j, sibling).wait_recv()
        mine = self._rows(dst, x, y, c)
        _remote(src, mine, sems, 0, sibling).wait_send()
        for j, (px, py) in enumerate(chips):
            _remote(src, mine, sems, 1 + j, (px, py, c)).wait_send()
            got = self._rows(dst, px, py, c)
            _remote(got, got, sems, 4 + j, sibling).wait_send()
        pltpu.make_async_copy(src, mine, sems[2].at[0]).wait()


def _call(body, *, name, grid, in_specs, out_specs, out_shape, args, scratch=(), parallel=(), plan=None):
    n_in, n_out, n_scr = len(in_specs), len(out_shape), len(scratch)
    sem = tuple("parallel" if (a in parallel and plan is None) else "arbitrary" for a in range(len(grid)))
    params = pltpu.CompilerParams(dimension_semantics=sem, vmem_limit_bytes=VMEM_LIMIT)
    if plan is None:
        outs = pl.pallas_call(body, name=name, grid=grid, in_specs=in_specs, out_specs=out_specs,
                              out_shape=out_shape, scratch_shapes=list(scratch), compiler_params=params)(*args)
        return list(outs), []
    p_in, p_out, p_scr = len(plan.inputs), len(plan.out_shapes), len(plan.scratch)

    def wrapped(*refs):
        ins, refs = refs[:n_in], refs[n_in:]
        p_ins, refs = refs[:p_in], refs[p_in:]
        outs, refs = refs[:n_out], refs[n_out:]
        p_outs, refs = refs[:p_out], refs[p_out:]
        scr, p_sems = refs[:n_scr], refs[n_scr:]
        ids = [pl.program_id(a) for a in range(len(grid))]
        first = _all([i == 0 for i in ids])
        last = _all([i == n - 1 for i, n in zip(ids, grid)])

        @pl.when(first)
        def _():
            plan.start(p_ins, p_outs, p_sems)

        body(*ins, *outs, *scr)

        @pl.when(last)
        def _():
            plan.finish(p_ins, p_outs, p_sems)

    outs = pl.pallas_call(
        wrapped, name=name, grid=grid,
        in_specs=list(in_specs) + [ANY] * p_in, out_specs=list(out_specs) + [ANY] * p_out,
        out_shape=list(out_shape) + list(plan.out_shapes),
        scratch_shapes=list(scratch) + list(plan.scratch), compiler_params=params,
    )(*args, *plan.inputs)
    return list(outs[:n_out]), list(outs[n_out:])


def _all(conds):
    out = conds[0]
    for c in conds[1:]:
        out = out & c
    return out


class _Together:
    def __init__(self, plans):
        self.plans = plans
        self.inputs = [a for p in plans for a in p.inputs]
        self.out_shapes = [s for p in plans for s in p.out_shapes]
        self.scratch = [s for p in plans for s in p.scratch]

    def _split(self, refs, count):
        out, at = [], 0
        for p in self.plans:
            out.append(refs[at:at + count(p)])
            at += count(p)
        return out

    def _parts(self, srcs, dsts, sems):
        return zip(self.plans, self._split(srcs, lambda p: len(p.inputs)),
                   self._split(dsts, lambda p: len(p.out_shapes)), self._split(sems, lambda p: len(p.scratch)))

    def start(self, srcs, dsts, sems):
        for p, s, d, m in self._parts(srcs, dsts, sems):
            p.start(s, d, m)

    def finish(self, srcs, dsts, sems):
        for p, s, d, m in self._parts(srcs, dsts, sems):
            p.finish(s, d, m)


def _run_plan(plan, *, name):
    p_in, p_out = len(plan.inputs), len(plan.out_shapes)

    def body(*refs):
        srcs, dsts, sems = refs[:p_in], refs[p_in:p_in + p_out], refs[p_in + p_out:]
        plan.start(srcs, dsts, sems)
        plan.finish(srcs, dsts, sems)

    return list(pl.pallas_call(body, name=name, in_specs=[ANY] * p_in, out_specs=[ANY] * p_out,
                               out_shape=list(plan.out_shapes), scratch_shapes=list(plan.scratch))(*plan.inputs))


def _rms(xf, g):
    r = lax.rsqrt(jnp.mean(xf * xf, axis=-1, keepdims=True) + EPS)
    return (xf * r * g).astype(BF16)


def _rmsnorm(x, g, *, name, plan=None):
    S, D = x.shape
    tm = _row_tile(S, 1024)

    def body(x_ref, g_ref, xn_ref):
        xn_ref[...] = _rms(x_ref[...], g_ref[...])

    return _call(
        body, name=name, grid=(S // tm,), parallel=(0,), plan=plan,
        in_specs=[pl.BlockSpec((tm, D), lambda i: (i, 0)), pl.BlockSpec((1, D), lambda i: (0, 0))],
        out_specs=[pl.BlockSpec((tm, D), lambda i: (i, 0))],
        out_shape=[jax.ShapeDtypeStruct((S, D), BF16)],
        args=(x, g))


def _matmul(xn, w4, *, name, out_dtype=F32, plan=None):
    S, D = xn.shape
    n4 = w4.shape[2]
    tm = _row_tile(S, 2048)

    def body(xn_ref, w_ref, o_ref):
        o_ref[...] = _dot(xn_ref[...], w_ref[...]).astype(out_dtype)

    return _call(
        body, name=name, grid=(S // tm, N_CHIPS), parallel=(0,), plan=plan,
        in_specs=[pl.BlockSpec((tm, D), lambda i, j: (i, 0)),
                  pl.BlockSpec((None, D, n4), lambda i, j: (j, 0, 0))],
        out_specs=[pl.BlockSpec((tm, n4), lambda i, j: (i, j))],
        out_shape=[jax.ShapeDtypeStruct((S, N_CHIPS * n4), out_dtype)],
        args=(xn, w4))


def _segments(widths, n4):
    per_chip = [[] for _ in range(N_CHIPS)]
    c0 = 0
    for p, w in enumerate(widths):
        a = c0
        while a < c0 + w:
            k = a // n4
            b = min(c0 + w, (k + 1) * n4)
            per_chip[k].append((p, (a - c0, b - c0), (a - k * n4, b - k * n4)))
            a = b
        c0 += w
    assert c0 == N_CHIPS * n4
    return per_chip


def _piece_specs(pieces, n4, tm):
    per_chip = _segments([p.shape[1] for p in pieces], n4)
    specs, local, start = [], [[] for _ in range(N_CHIPS)], 0
    for p, arr in enumerate(pieces):
        chips = [k for k in range(N_CHIPS) if any(seg[0] == p for seg in per_chip[k])]
        lo, hi = chips[0], chips[-1]
        tiled = arr.shape[1] % n4 == 0 and start % n4 == 0
        start += arr.shape[1]
        if tiled:
            imap = lambda k, i, lo=lo, hi=hi: (jnp.where((k >= lo) & (k <= hi), i, 0), jnp.clip(k - lo, 0, hi - lo))
            specs.append(pl.BlockSpec((tm, n4), imap))
        else:
            imap = lambda k, i, lo=lo, hi=hi: (jnp.where((k >= lo) & (k <= hi), i, 0), 0)
            specs.append(pl.BlockSpec((tm, arr.shape[1]), imap))
        for k in chips:
            for q, (pa, pb), cols in per_chip[k]:
                if q == p:
                    local[k].append((p, (0, n4) if tiled else (pa, pb), cols))
    return specs, local


def _dgrad_norm(dys, w4, x, g, dres, *, name, plan=None):
    S, D = x.shape
    n4 = w4.shape[2]
    tm = _row_tile(S, 512)
    per_chip = _segments([a.shape[1] for a in dys], n4)
    n_p = len(dys)

    def body(*refs):
        dy_refs = refs[:n_p]
        w_ref, x_ref, g_ref, dres_ref, dx_ref, dg_ref = refs[n_p:]

        @pl.when(pl.program_id(0) == 0)
        def _():
            dg_ref[...] = jnp.zeros_like(dg_ref)

        dxn = None
        for k in range(N_CHIPS):
            for p, (pa, pb), (ca, cb) in per_chip[k]:
                part = _dot_nt(dy_refs[p][:, pa:pb], w_ref[k, :, ca:cb])
                dxn = part if dxn is None else dxn + part
        xf = x_ref[...]
        r = lax.rsqrt(jnp.mean(xf * xf, axis=-1, keepdims=True) + EPS)
        xhat = xf * r
        dxhat = dxn * g_ref[...]
        dx_ref[...] = dres_ref[...] + r * (dxhat - xhat * jnp.mean(dxhat * xhat, axis=-1, keepdims=True))
        dg_ref[...] += jnp.sum(dxn * xhat, axis=0, keepdims=True)

    row = lambda w: pl.BlockSpec((tm, w), lambda i: (i, 0))
    return _call(
        body, name=name, grid=(S // tm,), plan=plan,
        in_specs=[row(a.shape[1]) for a in dys]
        + [pl.BlockSpec(w4.shape, lambda i: (0, 0, 0), pipeline_mode=pl.Buffered(1)),
           row(D), pl.BlockSpec((1, D), lambda i: (0, 0)), row(D)],
        out_specs=[row(D), pl.BlockSpec((1, D), lambda i: (0, 0))],
        out_shape=[jax.ShapeDtypeStruct((S, D), F32), jax.ShapeDtypeStruct((1, D), F32)],
        args=(*dys, w4, x, g, dres))


def _wgrad(a, dys, *, name, rows, plan=None):
    S, K = a.shape
    n4 = sum(p.shape[1] for p in dys) // N_CHIPS
    tm = _row_tile(S, rows)
    ns = S // tm
    specs, local = _piece_specs(dys, n4, tm)
    n_p = len(dys)

    def body(*refs):
        a_ref = refs[0]
        dy_refs = refs[1:1 + n_p]
        o_ref, o16_ref = refs[1 + n_p:]
        n, s = pl.program_id(0), pl.program_id(1)

        @pl.when(s == 0)
        def _():
            o_ref[...] = jnp.zeros_like(o_ref)

        for k in range(N_CHIPS):
            @pl.when(n == k)
            def _():
                av = a_ref[...]
                for p, (pa, pb), (ca, cb) in local[k]:
                    o_ref[:, ca:cb] += _dot_tn(av, dy_refs[p][:, pa:pb])

        @pl.when(s == ns - 1)
        def _():
            o16_ref[...] = o_ref[...].astype(BF16)

    out = pl.BlockSpec((None, K, n4), lambda n, s: (n, 0, 0))
    return _call(
        body, name=name, grid=(N_CHIPS, ns), parallel=(0,), plan=plan,
        in_specs=[pl.BlockSpec((tm, K), lambda n, s: (s, 0))] + specs,
        out_specs=[out, out],
        out_shape=[jax.ShapeDtypeStruct((N_CHIPS, K, n4), F32), jax.ShapeDtypeStruct((N_CHIPS, K, n4), BF16)],
        args=(a, *dys))


def _tiles(x):
    return x.reshape(x.shape[0] // SUBLANES, SUBLANES, x.shape[1])


def _shift_down(xp, s):
    n = xp.shape[0] - SUBLANES
    if s == SUBLANES:
        return xp[:n, :]
    t = _tiles(xp)
    rot = pltpu.roll(t, s, 1)
    sub = lax.broadcasted_iota(jnp.int32, t.shape, 1)[1:]
    return jnp.where(sub >= s, rot[1:], rot[:-1]).reshape(n, xp.shape[1])


def _shift_up(xn, s):
    n = xn.shape[0] - SUBLANES
    if s == SUBLANES:
        return xn[SUBLANES:, :]
    t = _tiles(xn)
    rot = pltpu.roll(t, SUBLANES - s, 1)
    sub = lax.broadcasted_iota(jnp.int32, t.shape, 1)[1:]
    return jnp.where(sub < SUBLANES - s, rot[:-1], rot[1:]).reshape(n, xn.shape[1])


def _pooled(u, halo, first_tile, row0):
    T = u.shape[0]
    halo = jnp.where(first_tile, 0.0, halo)
    pad = jnp.zeros((SUBLANES, u.shape[1]), F32)
    up = jnp.concatenate([pad, halo, u], axis=0)
    t1 = (row0 + lax.broadcasted_iota(jnp.int32, (T, 1), 0) + 1).astype(F32)
    outs = []
    for gi, w in enumerate(POOL_WINDOWS):
        s = up[:, gi * LANES:(gi + 1) * LANES]
        k = 1
        while k < w:
            if k < SUBLANES:
                s = jnp.concatenate([s[:SUBLANES, :], s[SUBLANES:, :] + _shift_down(s, k)], axis=0)
            else:
                s = s[SUBLANES:, :] + _shift_down(s, k)
            k *= 2
        s = s[-T:, :]
        inv = 1.0 / jnp.minimum(t1, float(w))
        outs.append(s * inv - u[:, gi * LANES:(gi + 1) * LANES])
    return outs


def _pool_fwd(z, pool_w, pool_scale, *, name):
    S = z.shape[0]
    P = pool_scale.shape[1]
    T = _row_tile(S, 512)
    hb = T // HALO_POOL

    def body(u_ref, halo_ref, pw_ref, ps_ref, o_ref):
        i = pl.program_id(0)
        pooled = _pooled(u_ref[...], halo_ref[...], i == 0, i * T)
        for gi in range(len(POOL_WINDOWS)):
            mixed = _dot(pooled[gi].astype(BF16), pw_ref[gi].astype(BF16))
            cols = slice(gi * LANES, (gi + 1) * LANES)
            o_ref[:, cols] = (mixed * ps_ref[:, cols]).astype(BF16)

    return _call(
        body, name=name, grid=(S // T,), parallel=(0,),
        in_specs=[pl.BlockSpec((T, P), lambda i: (i, 0)),
                  pl.BlockSpec((HALO_POOL, P), lambda i: (jnp.maximum(i * hb - 1, 0), 0)),
                  pl.BlockSpec(pool_w.shape, lambda i: (0, 0, 0)),
                  pl.BlockSpec((1, P), lambda i: (0, 0))],
        out_specs=[pl.BlockSpec((T, P), lambda i: (i, 0))],
        out_shape=[jax.ShapeDtypeStruct((S, P), BF16)],
        args=(z, z, pool_w, pool_scale))[0][0]


def _pool_bwd(z, dpm, pool_w, pool_scale, *, name):
    S, P = dpm.shape
    T = _row_tile(S, 512)
    hb = T // HALO_POOL
    nt = S // T

    def body(u_ref, halo_ref, d_ref, dnext_ref, pw_ref, ps_ref, du_ref, dpw_ref, dps_ref):
        i = pl.program_id(0)

        @pl.when(i == 0)
        def _():
            dpw_ref[...] = jnp.zeros_like(dpw_ref)
            dps_ref[...] = jnp.zeros_like(dps_ref)

        pooled = _pooled(u_ref[...], halo_ref[...], i == 0, i * T)
        dnext = jnp.where(i == nt - 1, 0.0, dnext_ref[...])
        pad = jnp.zeros((SUBLANES, P), F32)
        dext = jnp.concatenate([d_ref[...], dnext, pad], axis=0)
        t1 = (i * T + lax.broadcasted_iota(jnp.int32, (T + HALO_POOL + SUBLANES, 1), 0) + 1).astype(F32)
        for gi, w in enumerate(POOL_WINDOWS):
            cols = slice(gi * LANES, (gi + 1) * LANES)
            pw = pw_ref[gi].astype(BF16)
            pg = pooled[gi].astype(BF16)
            mixed = _dot(pg, pw)
            dps_ref[:, cols] += jnp.sum(d_ref[:, cols] * mixed, axis=0, keepdims=True)
            dmixed = (dext[:, cols] * ps_ref[:, cols]).astype(BF16)
            dpw_ref[gi] += _dot_tn(pg, dmixed[:T, :])
            dpooled = _dot_nt(dmixed, pw)
            e = dpooled * (1.0 / jnp.minimum(t1, float(w)))
            k = 1
            while k < w:
                if k < SUBLANES:
                    e = jnp.concatenate([e[:-SUBLANES, :] + _shift_up(e, k), e[-SUBLANES:, :]], axis=0)
                else:
                    e = e[:-SUBLANES, :] + _shift_up(e, k)
                k *= 2
            du_ref[:, cols] = (e[:T, :] - dpooled[:T, :]).astype(BF16)

    return _call(
        body, name=name, grid=(nt,),
        in_specs=[pl.BlockSpec((T, P), lambda i: (i, 0)),
                  pl.BlockSpec((HALO_POOL, P), lambda i: (jnp.maximum(i * hb - 1, 0), 0)),
                  pl.BlockSpec((T, P), lambda i: (i, 0)),
                  pl.BlockSpec((HALO_POOL, P), lambda i: (jnp.minimum((i + 1) * hb, S // HALO_POOL - 1), 0)),
                  pl.BlockSpec(pool_w.shape, lambda i: (0, 0, 0)),
                  pl.BlockSpec((1, P), lambda i: (0, 0))],
        out_specs=[pl.BlockSpec((T, P), lambda i: (i, 0)),
                   pl.BlockSpec(pool_w.shape, lambda i: (0, 0, 0)),
                   pl.BlockSpec((1, P), lambda i: (0, 0))],
        out_shape=[jax.ShapeDtypeStruct((S, P), BF16),
                   jax.ShapeDtypeStruct(pool_w.shape, F32),
                   jax.ShapeDtypeStruct((1, P), F32)],
        args=(z, z, dpm, dpm, pool_w, pool_scale))[0]


def _cumsum_rows(x):
    n = x.shape[0]
    row = lax.broadcasted_iota(jnp.int32, x.shape, 0)
    s = 1
    while s < n:
        x = x + jnp.where(row >= s, pltpu.roll(x, s, 0), 0.0)
        s *= 2
    return x


def _rev_cumsum_rows(x):
    n = x.shape[0]
    row = lax.broadcasted_iota(jnp.int32, x.shape, 0)
    s = 1
    while s < n:
        x = x + jnp.where(row < n - s, pltpu.roll(x, n - s, 0), 0.0)
        s *= 2
    return x


def _chunk_prep(zq, zf, lb, b_ref):
    n_sub = CHUNK // SUB
    sq = _sigmoid(zq)
    q = zq * sq
    sf = _sigmoid(zf)
    f = lb + (1.0 - lb) * sf
    k = 1.0 - f
    b = _cumsum_rows(jnp.log(f))
    b_ref[...] = b
    shape = (SUB, b.shape[1])
    ends = [jnp.broadcast_to(b_ref[pl.ds(SUB * j + SUB - 1, 1), :], shape) for j in range(n_sub)]
    mids = [jnp.broadcast_to(b_ref[pl.ds(SUB * j + SUB // 2 - 1, 1), :], shape) for j in range(n_sub)]
    own = [b[SUB * j:SUB * (j + 1), :] for j in range(n_sub)]
    m0 = jnp.concatenate(mids, axis=0)
    e1 = jnp.concatenate(ends, axis=0)
    eq = [jnp.exp(jnp.minimum(b - m0, EXP_CLAMP))]
    for d in range(1, n_sub):
        rd = jnp.concatenate([own[j] if j < d else ends[j - d] for j in range(n_sub)], axis=0)
        eq.append(jnp.exp(b - rd))
    ek0 = jnp.exp(jnp.minimum(m0 - b, EXP_CLAMP))
    ek1 = jnp.exp(e1 - b)
    b_last = b_ref[pl.ds(CHUNK - 1, 1), :]
    return dict(q=q, k=k, f=f, sq=sq, sf=sf, b=b, eq=eq, ek0=ek0, ek1=ek1,
                eb=jnp.exp(b), ekl=jnp.exp(b_last - b), el=jnp.exp(b_last))


def _chunk_masks():
    ti = lax.broadcasted_iota(jnp.int32, (CHUNK, CHUNK), 0)
    si = lax.broadcasted_iota(jnp.int32, (CHUNK, CHUNK), 1)
    shift = SUB.bit_length() - 1
    dsub = jnp.right_shift(ti, shift) - jnp.right_shift(si, shift)
    masks = [(dsub == 0) & (si <= ti)]
    masks += [dsub == d for d in range(1, CHUNK // SUB)]
    return masks


def _chunk_attn(p, masks):
    qd = [(p["q"] * e).astype(BF16) for e in p["eq"]]
    k0 = (p["k"] * p["ek0"]).astype(BF16)
    k1 = (p["k"] * p["ek1"]).astype(BF16)
    a = jnp.where(masks[0], _dot_nt(qd[0], k0), 0.0)
    for d in range(1, len(masks)):
        a = jnp.where(masks[d], _dot_nt(qd[d], k1), a)
    return a, qd, k0, k1


def _hgrn_fwd(z, lb, norm_g, *, name, plan=None):
    S = z.shape[0]
    HW = lb.shape[1]
    NH = HW // LANES
    T = _row_tile(S, 512)
    nc = T // CHUNK

    def body(zq_ref, zf_ref, zi_ref, zo_ref, lb_ref, ng_ref, o_ref, of_ref, st_ref, s_scr, b_scr):
        @pl.when(pl.program_id(0) == 0)
        def _():
            s_scr[...] = jnp.zeros_like(s_scr)

        ng = ng_ref[...]
        masks = _chunk_masks()

        def chunk(c, carry):
            rows = pl.ds(pl.multiple_of(c * CHUNK, CHUNK), CHUNK)
            for h in range(NH):
                cols = slice(h * LANES, (h + 1) * LANES)
                p = _chunk_prep(zq_ref[rows, cols], zf_ref[rows, cols], lb_ref[:, cols], b_scr.at[h])
                v = zi_ref[rows, cols].astype(BF16)
                zo = zo_ref[rows, cols]
                st = s_scr[h]
                st_ref[c, h] = st
                a, _, _, _ = _chunk_attn(p, masks)
                o = _dot(a.astype(BF16), v) + _dot_nt((p["q"] * p["eb"]).astype(BF16), st.astype(BF16))
                s_scr[h] = st * p["el"] + _dot_tn(v, (p["k"] * p["ekl"]).astype(BF16))
                o_ref[rows, cols] = o
                r = lax.rsqrt(jnp.mean(o * o, axis=-1, keepdims=True) + EPS)
                of_ref[rows, cols] = (o * r * ng * (zo * _sigmoid(zo))).astype(BF16)
            return carry

        lax.fori_loop(0, nc, chunk, 0, unroll=4)

    part = lambda k: pl.BlockSpec((T, HW), lambda i, k=k: (i, k))
    return _call(
        body, name=name, grid=(S // T,), plan=plan,
        in_specs=[part(1), part(2), part(3), part(4),
                  pl.BlockSpec((1, HW), lambda i: (0, 0)), pl.BlockSpec((1, LANES), lambda i: (0, 0))],
        out_specs=[pl.BlockSpec((T, HW), lambda i: (i, 0)), pl.BlockSpec((T, HW), lambda i: (i, 0)),
                   pl.BlockSpec((nc, NH, LANES, LANES), lambda i: (i, 0, 0, 0))],
        out_shape=[jax.ShapeDtypeStruct((S, HW), F32), jax.ShapeDtypeStruct((S, HW), BF16),
                   jax.ShapeDtypeStruct((S // CHUNK, NH, LANES, LANES), F32)],
        scratch=[pltpu.VMEM((NH, LANES, LANES), F32), pltpu.VMEM((NH, CHUNK, LANES), F32)],
        args=(z, z, z, z, lb, norm_g))


def _hgrn_bwd(z, lb, norm_g, o_raw, states, dof, *, name, plan=None):
    S = z.shape[0]
    HW = lb.shape[1]
    NH = HW // LANES
    T = _row_tile(S, 512)
    nc = T // CHUNK
    nt = S // T

    def body(zq_ref, zf_ref, zi_ref, zo_ref, lb_ref, ng_ref, o_ref, st_ref, dof_ref,
             dzq_ref, dzf_ref, dzi_ref, dzo_ref, dlb_ref, dng_ref, ds_scr, b_scr):
        @pl.when(pl.program_id(0) == 0)
        def _():
            ds_scr[...] = jnp.zeros_like(ds_scr)
            dlb_ref[...] = jnp.zeros_like(dlb_ref)
            dng_ref[...] = jnp.zeros_like(dng_ref)

        ng = ng_ref[...]
        masks = _chunk_masks()
        last_row = lax.broadcasted_iota(jnp.int32, (CHUNK, 1), 0) == CHUNK - 1

        def chunk(cr, carry):
            c = nc - 1 - cr
            rows = pl.ds(pl.multiple_of(c * CHUNK, CHUNK), CHUNK)
            for h in range(NH):
                cols = slice(h * LANES, (h + 1) * LANES)
                lbv = lb_ref[:, cols]
                zq, zf, zo = zq_ref[rows, cols], zf_ref[rows, cols], zo_ref[rows, cols]
                o = o_ref[rows, cols]
                dof_c = dof_ref[rows, cols]
                st = st_ref[c, h]
                dst = ds_scr[h]

                so = _sigmoid(zo)
                r = lax.rsqrt(jnp.mean(o * o, axis=-1, keepdims=True) + EPS)
                ohat = o * r
                d_on = dof_c * (zo * so)
                dzo_ref[rows, cols] = (dof_c * ohat * ng * _dsilu(zo, so)).astype(BF16)
                dng_ref[:, cols] += jnp.sum(d_on * ohat, axis=0, keepdims=True)
                dohat = d_on * ng
                do = (r * (dohat - ohat * jnp.mean(dohat * ohat, axis=-1, keepdims=True))).astype(BF16)

                p = _chunk_prep(zq, zf, lbv, b_scr.at[h])
                q, k = p["q"], p["k"]
                v = zi_ref[rows, cols].astype(BF16)
                a, qd, k0, k1 = _chunk_attn(p, masks)
                ktl = (k * p["ekl"]).astype(BF16)
                dstb = dst.astype(BF16)

                da = _dot_nt(do, v)
                dzi_ref[rows, cols] = (_dot_tn(a.astype(BF16), do) + _dot_nt(ktl, dstb)).astype(BF16)

                da0 = jnp.where(masks[0], da, 0.0).astype(BF16)
                rq = _dot(da0, k0)
                rk0 = _dot_tn(da0, qd[0])
                dq = rq * p["eq"][0]
                db = qd[0].astype(F32) * rq - k0.astype(F32) * rk0
                rk1 = jnp.zeros_like(rk0)
                for d in range(1, len(masks)):
                    dad = jnp.where(masks[d], da, 0.0).astype(BF16)
                    rq = _dot(dad, k1)
                    dq = dq + rq * p["eq"][d]
                    db = db + qd[d].astype(F32) * rq
                    rk1 = rk1 + _dot_tn(dad, qd[d])
                dk = rk0 * p["ek0"] + rk1 * p["ek1"]
                db = db - k1.astype(F32) * rk1
                qe = (q * p["eb"]).astype(BF16)
                rq = _dot(do, st.astype(BF16))
                dq = dq + rq * p["eb"]
                db = db + qe.astype(F32) * rq
                rk = _dot(v, dstb)
                dk = dk + rk * p["ekl"]
                db = db - ktl.astype(F32) * rk

                st_new = st * p["el"] + _dot_tn(v, ktl)
                db = db + jnp.where(last_row, jnp.sum(dstb.astype(F32) * st_new, axis=0, keepdims=True), 0.0)
                dg = _rev_cumsum_rows(db)
                ds_scr[h] = dst * p["el"] + _dot_tn(do, qe)

                dzq_ref[rows, cols] = (dq * _dsilu(zq, p["sq"])).astype(BF16)
                df = dg / p["f"] - dk
                sf = p["sf"]
                dzf_ref[rows, cols] = (df * (1.0 - lbv) * sf * (1.0 - sf)).astype(BF16)
                dlb_ref[:, cols] += jnp.sum(df * (1.0 - sf), axis=0, keepdims=True)
            return carry

        lax.fori_loop(0, nc, chunk, 0, unroll=4)

    rev = lambda i: nt - 1 - i
    part = lambda k: pl.BlockSpec((T, HW), lambda i, k=k: (rev(i), k))
    blk = pl.BlockSpec((T, HW), lambda i: (rev(i), 0))
    vec = pl.BlockSpec((1, HW), lambda i: (0, 0))
    return _call(
        body, name=name, grid=(nt,), plan=plan,
        in_specs=[part(1), part(2), part(3), part(4), vec, pl.BlockSpec((1, LANES), lambda i: (0, 0)),
                  blk, pl.BlockSpec((nc, NH, LANES, LANES), lambda i: (rev(i), 0, 0, 0)), blk],
        out_specs=[blk, blk, blk, blk, vec, vec],
        out_shape=[jax.ShapeDtypeStruct((S, HW), BF16)] * 4 + [jax.ShapeDtypeStruct((1, HW), F32)] * 2,
        scratch=[pltpu.VMEM((NH, LANES, LANES), F32), pltpu.VMEM((NH, CHUNK, LANES), F32)],
        args=(z, z, z, z, lb, norm_g, o_raw, states, dof))


def _gate_specs(T, D):
    half = D // 2
    first = (5 * half) // half
    return [pl.BlockSpec((T, half), lambda i, k=k: (i, first + k)) for k in range(4)]


def _gates(zg_refs, bg_ref, D):
    half = D // 2
    za = jnp.concatenate([zg_refs[0][...], zg_refs[1][...]], axis=1) + bg_ref[:, :D]
    zb = jnp.concatenate([zg_refs[2][...], zg_refs[3][...]], axis=1) + bg_ref[:, D:]
    return _sigmoid(za), _sigmoid(zb)


def _mix_fwd(x, pm, of, z, b_gate, w_pa4, w_pb4, w_o4, g_next, *, name, plan=None):
    S, D = x.shape
    P = pm.shape[1]
    T = _row_tile(S, 512)

    def body(x_ref, pm_ref, of_ref, g0, g1, g2, g3, bg_ref, wpa_ref, wpb_ref, wo_ref, gn_ref,
             xo_ref, ya_ref, yb_ref, xn_ref):
        pmv, ofv = pm_ref[...], of_ref[...]
        ya = jnp.concatenate([_dot(pmv, wpa_ref[k]) for k in range(N_CHIPS)], axis=1)
        yb = jnp.concatenate([_dot(ofv, wpb_ref[k]) for k in range(N_CHIPS)], axis=1)
        ga, gb = _gates((g0, g1, g2, g3), bg_ref, D)
        merged = (ga * ya + gb * yb).astype(BF16)
        x_mid = x_ref[...] + _dot(merged, wo_ref[...].reshape(D, D))
        xo_ref[...] = x_mid
        xn_ref[...] = _rms(x_mid, gn_ref[...])
        ya_ref[...] = ya.astype(BF16)
        yb_ref[...] = yb.astype(BF16)

    row = lambda w: pl.BlockSpec((T, w), lambda i: (i, 0))
    full = lambda a: pl.BlockSpec(a.shape, lambda i: (0,) * a.ndim)
    return _call(
        body, name=name, grid=(S // T,), parallel=(0,), plan=plan,
        in_specs=[row(D), row(P), row(P)] + _gate_specs(T, D) + [full(b_gate), full(w_pa4), full(w_pb4), full(w_o4),
                                                                  full(g_next)],
        out_specs=[row(D), row(D), row(D), row(D)],
        out_shape=[jax.ShapeDtypeStruct((S, D), F32), jax.ShapeDtypeStruct((S, D), BF16),
                   jax.ShapeDtypeStruct((S, D), BF16), jax.ShapeDtypeStruct((S, D), BF16)],
        args=(x, pm, of, z, z, z, z, b_gate, w_pa4, w_pb4, w_o4, g_next))


def _mix_bwd(dxm, ya, yb, z, b_gate, pm, of, w_pa4, w_pb4, w_o4, *, name, plan=None):
    S, D = dxm.shape
    P = pm.shape[1]
    q4 = D // N_CHIPS
    T = _row_tile(S, 256)
    nt = S // T

    def body(dx_ref, ya_ref, yb_ref, g0, g1, g2, g3, bg_ref, pm_ref, of_ref, wpa_ref, wpb_ref, wo_ref,
             dzg_ref, dpm_ref, dof_ref, dwo_ref, dwpa_ref, dwpb_ref, dbg_ref, dwo16_ref, dwpa16_ref, dwpb16_ref):
        i = pl.program_id(0)

        @pl.when(i == 0)
        def _():
            dwo_ref[...] = jnp.zeros_like(dwo_ref)
            dwpa_ref[...] = jnp.zeros_like(dwpa_ref)
            dwpb_ref[...] = jnp.zeros_like(dwpb_ref)
            dbg_ref[...] = jnp.zeros_like(dbg_ref)

        dxb = dx_ref[...].astype(BF16)
        ya = ya_ref[...].astype(F32)
        yb = yb_ref[...].astype(F32)
        ga, gb = _gates((g0, g1, g2, g3), bg_ref, D)
        merged = (ga * ya + gb * yb).astype(BF16)
        dwo_ref[...] += _dot_tn(merged, dxb).reshape(N_CHIPS, q4, D)
        dm = _dot_nt(dxb, wo_ref[...].reshape(D, D))
        dza = dm * ya * ga * (1.0 - ga)
        dzb = dm * yb * gb * (1.0 - gb)
        dzg_ref[:, :D] = dza.astype(BF16)
        dzg_ref[:, D:] = dzb.astype(BF16)
        dbg_ref[:, :D] += jnp.sum(dza, axis=0, keepdims=True)
        dbg_ref[:, D:] += jnp.sum(dzb, axis=0, keepdims=True)
        dya = (dm * ga).astype(BF16)
        dyb = (dm * gb).astype(BF16)
        pmv, ofv = pm_ref[...], of_ref[...]
        dpm = jnp.zeros((T, P), F32)
        dof = jnp.zeros((T, P), F32)
        for k in range(N_CHIPS):
            cols = slice(k * q4, (k + 1) * q4)
            dwpa_ref[k] += _dot_tn(pmv, dya[:, cols])
            dwpb_ref[k] += _dot_tn(ofv, dyb[:, cols])
            dpm = dpm + _dot_nt(dya[:, cols], wpa_ref[k])
            dof = dof + _dot_nt(dyb[:, cols], wpb_ref[k])
        dpm_ref[...] = dpm
        dof_ref[...] = dof

        @pl.when(i == nt - 1)
        def _():
            dwo16_ref[...] = dwo_ref[...].astype(BF16)
            dwpa16_ref[...] = dwpa_ref[...].astype(BF16)
            dwpb16_ref[...] = dwpb_ref[...].astype(BF16)

    row = lambda w: pl.BlockSpec((T, w), lambda i: (i, 0))
    full = lambda a: pl.BlockSpec(a.shape, lambda i: (0,) * a.ndim)
    like = lambda a, dt: jax.ShapeDtypeStruct(a.shape, dt)
    return _call(
        body, name=name, grid=(nt,), plan=plan,
        in_specs=[row(D), row(D), row(D)] + _gate_specs(T, D) + [full(b_gate), row(P), row(P),
                                                                  full(w_pa4), full(w_pb4), full(w_o4)],
        out_specs=[row(2 * D), row(P), row(P), full(w_o4), full(w_pa4), full(w_pb4), full(b_gate),
                   full(w_o4), full(w_pa4), full(w_pb4)],
        out_shape=[jax.ShapeDtypeStruct((S, 2 * D), BF16), jax.ShapeDtypeStruct((S, P), F32),
                   jax.ShapeDtypeStruct((S, P), F32), like(w_o4, F32), like(w_pa4, F32), like(w_pb4, F32),
                   like(b_gate, F32), like(w_o4, BF16), like(w_pa4, BF16), like(w_pb4, BF16)],
        args=(dxm, ya, yb, z, z, z, z, b_gate, pm, of, w_pa4, w_pb4, w_o4))


def _up_conv(xn, w_up4, conv_w, conv_b, *, name, plan=None):
    S, D = xn.shape
    f4 = w_up4.shape[2]
    nf = N_CHIPS // 2
    F = nf * f4
    T = _row_tile(S, 512)

    def body(xn_ref, wv_ref, wg_ref, cwv_ref, cwg_ref, cbv_ref, cbg_ref,
             hv_ref, hg_ref, val_ref, gate_ref, a_ref, pv_scr, pg_scr):
        i = pl.program_id(1)
        xv = xn_ref[...]

        def side(w_ref, cw_ref, cb_ref, h_ref, p_scr):
            h16 = _dot(xv, w_ref[...]).astype(BF16)
            h_ref[...] = h16
            h = h16.astype(F32)
            hp = jnp.concatenate([jnp.where(i == 0, 0.0, p_scr[...]), h], axis=0)
            p_scr[...] = h[-SUBLANES:, :]
            cw = cw_ref[...]
            return cw[0:1, :] * _shift_down(hp, 2) + cw[1:2, :] * _shift_down(hp, 1) + cw[2:3, :] * h + cb_ref[...]

        val = side(wv_ref, cwv_ref, cbv_ref, hv_ref, pv_scr)
        gate = side(wg_ref, cwg_ref, cbg_ref, hg_ref, pg_scr)
        val_ref[...] = val.astype(BF16)
        gate_ref[...] = gate.astype(BF16)
        a_ref[...] = (gate * _sigmoid(gate) * val).astype(BF16)

    out = pl.BlockSpec((T, f4), lambda f, i: (i, f))
    return _call(
        body, name=name, grid=(nf, S // T), plan=plan,
        in_specs=[pl.BlockSpec((T, D), lambda f, i: (i, 0)),
                  pl.BlockSpec((None, D, f4), lambda f, i: (f, 0, 0)),
                  pl.BlockSpec((None, D, f4), lambda f, i: (nf + f, 0, 0)),
                  pl.BlockSpec((3, f4), lambda f, i: (0, f)),
                  pl.BlockSpec((3, f4), lambda f, i: (0, nf + f)),
                  pl.BlockSpec((1, f4), lambda f, i: (0, f)),
                  pl.BlockSpec((1, f4), lambda f, i: (0, nf + f))],
        out_specs=[out] * 5,
        out_shape=[jax.ShapeDtypeStruct((S, F), BF16)] * 5,
        scratch=[pltpu.VMEM((SUBLANES, f4), F32), pltpu.VMEM((SUBLANES, f4), F32)],
        args=(xn, w_up4, w_up4, conv_w, conv_w, conv_b, conv_b))


def _down(a, w_down4, x, g_next, *, name, plan=None):
    S, F = a.shape
    D = x.shape[1]
    T = _row_tile(S, 1024)

    def body(a_ref, wd_ref, x_ref, gn_ref, o_ref, xn_ref):
        x_out = x_ref[...] + _dot(a_ref[...], wd_ref[...].reshape(F, D))
        o_ref[...] = x_out
        xn_ref[...] = _rms(x_out, gn_ref[...])

    row = lambda w: pl.BlockSpec((T, w), lambda i: (i, 0))
    return _call(
        body, name=name, grid=(S // T,), parallel=(0,), plan=plan,
        in_specs=[row(F), pl.BlockSpec(w_down4.shape, lambda i: (0, 0, 0), pipeline_mode=pl.Buffered(1)),
                  row(D), pl.BlockSpec((1, D), lambda i: (0, 0))],
        out_specs=[row(D), row(D)],
        out_shape=[jax.ShapeDtypeStruct((S, D), F32), jax.ShapeDtypeStruct((S, D), BF16)],
        args=(a, w_down4, x, g_next))


def _ffn_down_bwd(dxo, hv, hg, val16, gate16, conv_w, w_down4, *, name, plan=None):
    S = hv.shape[0]
    _, f4, D = w_down4.shape
    F = N_CHIPS * f4
    T = _row_tile(S, 512)
    nf = 2
    tf = 2 * f4
    nt = S // T

    def body(dx_ref, hv_ref, hg_ref, val_ref, gate_ref, cwv_ref, cwg_ref, wd_ref,
             dhv_ref, dhg_ref, dwd_ref, dwd16_ref, dcwv_ref, dcwg_ref, dcbv_ref, dcbg_ref, cv_scr, cg_scr):
        i = pl.program_id(1)

        @pl.when(i == 0)
        def _():
            cv_scr[...] = jnp.zeros_like(cv_scr)
            cg_scr[...] = jnp.zeros_like(cg_scr)
            dwd_ref[...] = jnp.zeros_like(dwd_ref)
            dcwv_ref[...] = jnp.zeros_like(dcwv_ref)
            dcwg_ref[...] = jnp.zeros_like(dcwg_ref)
            dcbv_ref[...] = jnp.zeros_like(dcbv_ref)
            dcbg_ref[...] = jnp.zeros_like(dcbg_ref)

        dxb = dx_ref[...].astype(BF16)
        val = val_ref[...].astype(F32)
        gate = gate_ref[...].astype(F32)
        sg = _sigmoid(gate)
        sil = gate * sg
        dwd_ref[...] += _dot_tn((sil * val).astype(BF16), dxb).reshape(2, f4, D)
        da = _dot_nt(dxb, wd_ref[...].reshape(tf, D))

        def conv_bwd(dhc, h0, cw, c_scr, dh_ref, dcw_ref, dcb_ref):
            ext = jnp.concatenate([dhc, c_scr[...]], axis=0)
            n1 = _shift_up(ext, 1)
            n2 = _shift_up(ext, 2)
            dh_ref[...] = (cw[2:3, :] * dhc + cw[1:2, :] * n1 + cw[0:1, :] * n2).astype(BF16)
            c_scr[...] = dhc[:SUBLANES, :]
            dcw_ref[0:1, :] += jnp.sum(n2 * h0, axis=0, keepdims=True)
            dcw_ref[1:2, :] += jnp.sum(n1 * h0, axis=0, keepdims=True)
            dcw_ref[2:3, :] += jnp.sum(dhc * h0, axis=0, keepdims=True)
            dcb_ref[...] += jnp.sum(dhc, axis=0, keepdims=True)

        conv_bwd(da * sil, hv_ref[...].astype(F32), cwv_ref[...], cv_scr, dhv_ref, dcwv_ref, dcbv_ref)
        conv_bwd(da * val * _dsilu(gate, sg), hg_ref[...].astype(F32), cwg_ref[...], cg_scr, dhg_ref, dcwg_ref,
                 dcbg_ref)

        @pl.when(i == nt - 1)
        def _():
            dwd16_ref[...] = dwd_ref[...].astype(BF16)

    rev = lambda i: nt - 1 - i
    wd_spec = pl.BlockSpec((2, f4, D), lambda f, i: (f, 0, 0))
    return _call(
        body, name=name, grid=(nf, nt), plan=plan,
        in_specs=[pl.BlockSpec((T, D), lambda f, i: (rev(i), 0)),
                  pl.BlockSpec((T, tf), lambda f, i: (rev(i), f)),
                  pl.BlockSpec((T, tf), lambda f, i: (rev(i), f)),
                  pl.BlockSpec((T, tf), lambda f, i: (rev(i), f)),
                  pl.BlockSpec((T, tf), lambda f, i: (rev(i), f)),
                  pl.BlockSpec((3, tf), lambda f, i: (0, f)),
                  pl.BlockSpec((3, tf), lambda f, i: (0, nf + f)),
                  wd_spec],
        out_specs=[pl.BlockSpec((T, tf), lambda f, i: (rev(i), f)),
                   pl.BlockSpec((T, tf), lambda f, i: (rev(i), f)),
                   wd_spec, wd_spec,
                   pl.BlockSpec((3, tf), lambda f, i: (0, f)),
                   pl.BlockSpec((3, tf), lambda f, i: (0, f)),
                   pl.BlockSpec((1, tf), lambda f, i: (0, f)),
                   pl.BlockSpec((1, tf), lambda f, i: (0, f))],
        out_shape=[jax.ShapeDtypeStruct((S, F), BF16), jax.ShapeDtypeStruct((S, F), BF16),
                   jax.ShapeDtypeStruct((N_CHIPS, f4, D), F32), jax.ShapeDtypeStruct((N_CHIPS, f4, D), BF16),
                   jax.ShapeDtypeStruct((3, F), F32), jax.ShapeDtypeStruct((3, F), F32),
                   jax.ShapeDtypeStruct((1, F), F32), jax.ShapeDtypeStruct((1, F), F32)],
        scratch=[pltpu.VMEM((SUBLANES, tf), F32), pltpu.VMEM((SUBLANES, tf), F32)],
        args=(dxo, hv, hg, val16, gate16, conv_w, conv_w, w_down4))


def _final_loss(x, g, target, *, name):
    S, D = x.shape
    T = _row_tile(S, 512)

    def body(x_ref, g_ref, t_ref, loss_ref, dx_ref, dg_ref):
        @pl.when(pl.program_id(0) == 0)
        def _():
            loss_ref[...] = jnp.zeros_like(loss_ref)
            dg_ref[...] = jnp.zeros_like(dg_ref)

        xf = x_ref[...]
        r = lax.rsqrt(jnp.mean(xf * xf, axis=-1, keepdims=True) + EPS)
        xhat = xf * r
        err = xhat * g_ref[...] - t_ref[...]
        loss_ref[...] += jnp.sum(err * err, axis=0, keepdims=True) * (0.5 / D)
        dy = err * (1.0 / D)
        dxhat = dy * g_ref[...]
        dx_ref[...] = r * (dxhat - xhat * jnp.mean(dxhat * xhat, axis=-1, keepdims=True))
        dg_ref[...] += jnp.sum(dy * xhat, axis=0, keepdims=True)

    return _call(
        body, name=name, grid=(S // T,),
        in_specs=[pl.BlockSpec((T, D), lambda i: (i, 0)), pl.BlockSpec((1, D), lambda i: (0, 0)),
                  pl.BlockSpec((T, D), lambda i: (i, 0))],
        out_specs=[pl.BlockSpec((1, D), lambda i: (0, 0)), pl.BlockSpec((T, D), lambda i: (i, 0)),
                   pl.BlockSpec((1, D), lambda i: (0, 0))],
        out_shape=[jax.ShapeDtypeStruct((1, D), F32), jax.ShapeDtypeStruct((S, D), F32),
                   jax.ShapeDtypeStruct((1, D), F32)],
        args=(x, g, target))[0]


BIG = ("w_in", "w_pa", "w_pb", "w_o", "w_up", "w_down")
SMALL = ("norm1_g", "b_gate", "pool_w", "pool_scale", "lb_logits", "hgrn_norm_g", "norm2_g", "conv_b", "final_g")
WEIGHTS = ("norm1_g", "w_in", "b_gate", "pool_w", "pool_scale", "lb_logits", "hgrn_norm_g", "w_pa", "w_pb", "w_o",
           "norm2_g", "w_up", "conv_w", "conv_b", "w_down", "final_g")


def _lower_bounds(lb_logits):
    soft = jax.nn.softmax(lb_logits.astype(F32), axis=0)
    cum = jnp.cumsum(soft, axis=0)
    return cum - cum[0:1]


def _step(x, target, sm, wts, shards=None):
    L = sm["norm1_g"].shape[0]
    wts = dict(wts)
    dist = shards is not None
    lbs, lb_vjp = jax.vjp(_lower_bounds, sm["lb_logits"])
    row = lambda a: a.reshape(1, -1)
    conv_w = sm.get("conv_w")

    def gather(names_layers, with_conv=False):
        items = [(shards[n], "rows", l) for n, l in names_layers]
        if with_conv:
            items.append((shards["conv_w"], "layer", None))
        return _GatherPlan(items)

    def landed(names_layers, outs):
        for key, arr in zip(names_layers, outs):
            wts[key] = arr

    own = {"in_proj": ("w_up",)}
    first = {"hgrn_fwd": ("w_pa", "w_pb", "w_o"), "mix_fwd": ("w_down",)}
    ahead = {"up": ("w_in", "w_pa", "w_pb", "w_o"), "down": ("w_down",)}
    conv_rider = "mix_fwd"

    def riders(l, kernel):
        if not dist:
            return [], None
        keys = [(n, l) for n in own.get(kernel, ())]
        keys += [(n, l) for n in first.get(kernel, ())] if l == 0 else []
        keys += [(n, l + 1) for n in ahead.get(kernel, ())] if l + 1 < L else []
        with_conv = l == 0 and kernel == conv_rider
        return keys, (gather(keys, with_conv) if keys or with_conv else None)

    keys = [("w_in", 0)] if dist else []
    (xn1,), got = _rmsnorm(x, row(sm["norm1_g"][0]), name="norm_in", plan=gather(keys) if keys else None)
    landed(keys, got)

    saved = []
    for l in range(L):
        keys, plan = riders(l, "in_proj")
        (z,), got = _matmul(xn1, wts[("w_in", l)], name=f"in_proj_{l}", plan=plan)
        landed(keys, got)
        pm = _pool_fwd(z, sm["pool_w"][l], row(sm["pool_scale"][l]), name=f"pool_fwd_{l}")
        keys, plan = riders(l, "hgrn_fwd")
        (o_raw, of, states), got = _hgrn_fwd(z, row(lbs[l]), row(sm["hgrn_norm_g"][l]), name=f"hgrn_fwd_{l}",
                                             plan=plan)
        landed(keys, got)
        keys, plan = riders(l, "mix_fwd")
        (x_mid, ya, yb, xn2), got = _mix_fwd(x, pm, of, z, row(sm["b_gate"][l]), wts[("w_pa", l)], wts[("w_pb", l)],
                                             wts[("w_o", l)], row(sm["norm2_g"][l]), name=f"mix_fwd_{l}", plan=plan)
        landed(keys, got)
        if dist and l == 0:
            full = got[-1]
            conv_w = jnp.concatenate([full[:, k] for k in range(N_CHIPS)], axis=2)
        keys, plan = riders(l, "up")
        (hv, hg, val16, gate16, a16), got = _up_conv(xn2, wts[("w_up", l)], conv_w[l], row(sm["conv_b"][l]),
                                                     name=f"up_{l}", plan=plan)
        landed(keys, got)
        keys, plan = riders(l, "down")
        g_next = row(sm["norm1_g"][l + 1]) if l + 1 < L else row(sm["final_g"])
        (x_out, xn_next), got = _down(a16, wts[("w_down", l)], x_mid, g_next, name=f"down_{l}", plan=plan)
        landed(keys, got)
        saved.append(dict(x=x, xn1=xn1, z=z, pm=pm, o_raw=o_raw, of=of, states=states,
                          x_mid=x_mid, ya=ya, yb=yb, xn2=xn2, hv=hv, hg=hg, val16=val16, gate16=gate16))
        x, xn1 = x_out, xn_next

    loss_cols, dx, d_final_g = _final_loss(x, row(sm["final_g"]), target, name="final_loss")

    small = {k: [None] * L for k in ("norm1_g", "b_gate", "pool_w", "pool_scale", "hgrn_norm_g", "norm2_g",
                                     "conv_w", "conv_b")}
    big32, big16, recv = {}, {}, {}
    dlbs = [None] * L
    pending = []

    def scatter():
        if not (dist and pending):
            return [], None
        keys = list(pending)
        del pending[:]
        return keys, _ScatterPlan([big16[k] for k in keys])

    def sent(keys, outs):
        for key, arr in zip(keys, outs):
            recv[key] = arr

    def made(name, l, g32, g16):
        big32[(name, l)], big16[(name, l)] = g32, g16
        pending.append((name, l))

    for l in reversed(range(L)):
        s = saved[l]
        keys, plan = scatter()
        (dhv, dhg, d_wd, d_wd16, dcwv, dcwg, dcbv, dcbg), got = _ffn_down_bwd(
            dx, s["hv"], s["hg"], s["val16"], s["gate16"], conv_w[l], wts[("w_down", l)], name=f"down_bwd_{l}",
            plan=plan)
        sent(keys, got)
        made("w_down", l, d_wd, d_wd16)
        small["conv_w"][l] = jnp.concatenate([dcwv, dcwg], axis=1)
        small["conv_b"][l] = jnp.concatenate([dcbv, dcbg], axis=1)[0]
        keys, plan = scatter()
        (d_wu, d_wu16), got = _wgrad(s["xn2"], [dhv, dhg], name=f"up_wgrad_{l}", rows=2048, plan=plan)
        sent(keys, got)
        made("w_up", l, d_wu, d_wu16)
        (dxm, dg2), _ = _dgrad_norm([dhv, dhg], wts[("w_up", l)], s["x_mid"], row(sm["norm2_g"][l]), dx,
                                    name=f"up_dgrad_{l}")
        small["norm2_g"][l] = dg2[0]

        keys, plan = scatter()
        (dzg, dpm, dof, d_wo, d_wpa, d_wpb, dbg, d_wo16, d_wpa16, d_wpb16), got = _mix_bwd(
            dxm, s["ya"], s["yb"], s["z"], row(sm["b_gate"][l]), s["pm"], s["of"],
            wts[("w_pa", l)], wts[("w_pb", l)], wts[("w_o", l)], name=f"mix_bwd_{l}", plan=plan)
        sent(keys, got)
        made("w_o", l, d_wo, d_wo16)
        made("w_pa", l, d_wpa, d_wpa16)
        made("w_pb", l, d_wpb, d_wpb16)
        small["b_gate"][l] = dbg[0]

        du, dpw, dps = _pool_bwd(s["z"], dpm, sm["pool_w"][l], row(sm["pool_scale"][l]), name=f"pool_bwd_{l}")
        small["pool_w"][l], small["pool_scale"][l] = dpw, dps[0]

        keys, plan = scatter()
        (dzq, dzf, dzi, dzo, dlb, dng), got = _hgrn_bwd(s["z"], row(lbs[l]), row(sm["hgrn_norm_g"][l]), s["o_raw"],
                                                      s["states"], dof, name=f"hgrn_bwd_{l}", plan=plan)
        sent(keys, got)
        dlbs[l] = dlb[0]
        small["hgrn_norm_g"][l] = jnp.sum(dng.reshape(-1, LANES), axis=0)

        dz = [du, dzq, dzf, dzi, dzo, dzg]
        (d_wi, d_wi16), _ = _wgrad(s["xn1"], dz, name=f"in_wgrad_{l}", rows=1024)
        made("w_in", l, d_wi, d_wi16)
        keys, plan = scatter()
        (dx, dg1), got = _dgrad_norm(dz, wts[("w_in", l)], s["x"], row(sm["norm1_g"][l]), dxm,
                                     name=f"in_dgrad_{l}", plan=plan)
        sent(keys, got)
        small["norm1_g"][l] = dg1[0]

    out = {k: jnp.stack(v) for k, v in small.items()}
    out["lb_logits"] = lb_vjp(jnp.stack(dlbs))[0]
    out["final_g"] = d_final_g[0]
    return loss_cols, dx, out, big32, recv


def _elementwise_rows(R, n, n_arrays):
    if 2 * n_arrays * R * n * 4 <= VMEM_LIMIT // 4 or R % 8:
        return R
    block = VMEM_LIMIT // 2 // (2 * n_arrays)
    want = 8
    while want * 2 * n * 4 <= block:
        want *= 2
    return _row_tile(R, want)


def _sum_layers(own, got, chip, *, name):
    L = len(own)
    _, r, n = own[0].shape
    T = _elementwise_rows(r, n, 6)
    nt = r // T

    def body(chip_ref, *refs):
        o_ref = refs[-1]
        l = pl.program_id(0)
        for k in range(L):
            @pl.when(l == k)
            def _():
                own_ref, got_ref = refs[2 * k], refs[2 * k + 1]
                acc = own_ref[...]
                for j in range(3):
                    acc = acc + got_ref[j].astype(F32)
                o_ref[...] = acc

    in_specs = []
    for k in range(L):
        hold = 0 if k else nt - 1
        in_specs.append(pl.BlockSpec((None, T, n), lambda l, i, c, k=k, hold=hold: (c[0], jnp.where(l == k, i, hold), 0)))
        in_specs.append(pl.BlockSpec((3, T, n), lambda l, i, c, k=k, hold=hold: (0, jnp.where(l == k, i, hold), 0)))
    grid_spec = pltpu.PrefetchScalarGridSpec(
        num_scalar_prefetch=1, grid=(L, nt), in_specs=in_specs,
        out_specs=pl.BlockSpec((None, T, n), lambda l, i, c: (l, i, 0)))
    args = [a for pair in zip(own, got) for a in pair]
    return pl.pallas_call(
        body, name=name, grid_spec=grid_spec, out_shape=jax.ShapeDtypeStruct((L, r, n), F32),
        compiler_params=pltpu.CompilerParams(dimension_semantics=("arbitrary", "arbitrary"),
                                             vmem_limit_bytes=VMEM_LIMIT),
    )(chip, *args)


def _sum_stack(first, rest, *, name):
    R, n = first.shape
    K = rest.shape[0]
    T = _elementwise_rows(R, n, K + 2)

    def body(a_ref, r_ref, o_ref):
        acc = a_ref[...]
        for j in range(K):
            acc = acc + r_ref[j].astype(F32)
        o_ref[...] = acc

    return _call(
        body, name=name, grid=(R // T,), parallel=(0,),
        in_specs=[pl.BlockSpec((T, n), lambda i: (i, 0)), pl.BlockSpec((K, T, n), lambda i: (0, i, 0))],
        out_specs=[pl.BlockSpec((T, n), lambda i: (i, 0))],
        out_shape=[jax.ShapeDtypeStruct((R, n), F32)],
        args=(first, rest))[0][0]


def _adamw(w, m, v, g_parts, *, name):
    R, n = w.shape
    n_g = len(g_parts)
    T = _elementwise_rows(R, n, 7 + n_g)

    def body(*refs):
        w_ref, m_ref, v_ref = refs[:3]
        g_refs = refs[3:3 + n_g]
        go_ref, d_ref, mo_ref, vo_ref = refs[3 + n_g:]
        g_ = g_refs[0][...]
        for r in g_refs[1:]:
            g_ = g_ + r[...]
        m_ = ADAM_B1 * m_ref[...] + (1.0 - ADAM_B1) * g_
        v_ = ADAM_B2 * v_ref[...] + (1.0 - ADAM_B2) * (g_ * g_)
        m_hat = m_ / (1.0 - ADAM_B1 ** ADAM_STEP)
        v_hat = v_ / (1.0 - ADAM_B2 ** ADAM_STEP)
        go_ref[...] = g_
        d_ref[...] = -ADAM_LR * (m_hat / (jnp.sqrt(v_hat) + ADAM_EPS) + ADAM_WD * w_ref[...])
        mo_ref[...] = m_
        vo_ref[...] = v_

    blk = pl.BlockSpec((T, n), lambda i: (i, 0))
    return _call(
        body, name=name, grid=(R // T,), parallel=(0,),
        in_specs=[blk] * (3 + n_g), out_specs=[blk] * 4,
        out_shape=[jax.ShapeDtypeStruct((R, n), F32)] * 4,
        args=(w, m, v, *g_parts))[0]


PACK_ALIGN = 8 * LANES


def _pack(pieces):
    flat = []
    for a in pieces:
        a = a.reshape(-1)
        pad = (-a.shape[0]) % PACK_ALIGN
        flat.append(jnp.pad(a, (0, pad)) if pad else a)
    return jnp.concatenate(flat).reshape(-1, LANES)


def _unpack(buf, shapes):
    flat = buf.reshape(-1)
    out, off = [], 0
    for shp in shapes:
        size = 1
        for s in shp:
            size *= s
        out.append(flat[off:off + size].reshape(shp))
        off += size + (-size) % PACK_ALIGN
    return out


def kernel(x, norm1_g, w_in, b_gate, pool_w, pool_scale, lb_logits, hgrn_norm_g, w_pa, w_pb, w_o, norm2_g, w_up, conv_w, conv_b, w_down, final_g, loss_target, m_norm1_g, m_w_in, m_b_gate, m_pool_w, m_pool_scale, m_lb_logits, m_hgrn_norm_g, m_w_pa, m_w_pb, m_w_o, m_norm2_g, m_w_up, m_conv_w, m_conv_b, m_w_down, m_final_g, v_norm1_g, v_w_in, v_b_gate, v_pool_w, v_pool_scale, v_lb_logits, v_hgrn_norm_g, v_w_pa, v_w_pb, v_w_o, v_norm2_g, v_w_up, v_conv_w, v_conv_b, v_w_down, v_final_g):
    env = dict(locals())
    w = {n: env[n] for n in WEIGHTS}
    m = {n: env["m_" + n] for n in WEIGHTS}
    v = {n: env["v_" + n] for n in WEIGHTS}
    my_chip = 2 * lax.axis_index("x") + lax.axis_index("y")
    L = w_in.shape[0]

    shards = {n: w[n].astype(BF16) for n in BIG}
    shards["conv_w"] = w["conv_w"]
    sm = {n: w[n] for n in SMALL}
    loss_cols, grad_x, g_small, big32, recv = _step(x[0], loss_target[0], sm, {}, shards)

    chip = my_chip.reshape(1).astype(jnp.int32)
    sums = [_sum_layers([big32[(n, l)] for l in range(L)], [recv[(n, l)] for l in range(L)], chip,
                        name="chip_sum_" + n) for n in BIG]
    small_names = list(SMALL)
    small_pieces = [g_small[n] for n in small_names] + [g_small["conv_w"], loss_cols]
    small_shapes = [a.shape for a in small_pieces]
    packed = _pack(small_pieces)
    Rs = packed.shape[0]
    swapped = _run_plan(_Together([_SiblingPlan(sums), _EveryonePlan(packed)]), name="tail_exchange")
    theirs, everyone = swapped[:-1], swapped[-1].reshape(8, Rs, LANES)
    g, delta, new_m, new_v = {}, {}, {}, {}
    for n, mine, other in zip(BIG, sums, theirs):
        shp = w[n].shape
        two_d = lambda a: a.reshape(-1, shp[-1])
        outs = _adamw(two_d(w[n]), two_d(m[n]), two_d(v[n]), [two_d(mine), two_d(other)], name="adamw_" + n)
        g[n], delta[n], new_m[n], new_v[n] = [a.reshape(shp) for a in outs]

    summed = _unpack(_sum_stack(everyone[0], everyone[1:], name="small_sum"), small_shapes)
    loss = jnp.sum(summed[-1])
    cshard = w["conv_w"].shape[2]
    gs = dict(zip(small_names, summed[:len(small_names)]))
    g_cw = lax.dynamic_slice_in_dim(summed[-2], my_chip * cshard, cshard, axis=2)

    sm_out = _adamw(_pack([w[n] for n in small_names]), _pack([m[n] for n in small_names]),
                    _pack([v[n] for n in small_names]), [_pack([gs[n] for n in small_names])], name="adamw_small")
    shapes = [w[n].shape for n in small_names]
    for n, g_, d_, m_, v_ in zip(small_names, *[_unpack(a, shapes) for a in sm_out]):
        g[n], delta[n], new_m[n], new_v[n] = g_, d_, m_, v_
    shp = w["conv_w"].shape
    two_d = lambda a: a.reshape(-1, shp[-1])
    outs = _adamw(two_d(w["conv_w"]), two_d(m["conv_w"]), two_d(v["conv_w"]), [two_d(g_cw)], name="adamw_conv_w")
    g["conv_w"], delta["conv_w"], new_m["conv_w"], new_v["conv_w"] = [a.reshape(shp) for a in outs]

    return (loss, grad_x[None], *[g[n] for n in WEIGHTS], *[delta[n] for n in WEIGHTS],
            *[new_m[n] for n in WEIGHTS], *[new_v[n] for n in WEIGHTS])
```

```python
import jax
import jax.numpy as jnp
from jax import lax
from jax.experimental import pallas as pl
from jax.experimental.pallas import tpu as pltpu

F32 = jnp.float32
BF16 = jnp.bfloat16

EPS = 1e-6
CHUNK = 64
SUB = 32
LANES = 128
SUBLANES = 8
POOL_WINDOWS = (2, 4, 8, 16)
HALO_POOL = 16
EXP_CLAMP = 80.0

ADAM_LR = 0.001
ADAM_B1 = 0.9
ADAM_B2 = 0.999
ADAM_EPS = 1e-08
ADAM_WD = 0.01
ADAM_STEP = 10

VMEM_LIMIT = 56 * 1024 * 1024
MESH_ID = pl.DeviceIdType.MESH
N_CHIPS = 4
ANY = pl.BlockSpec(memory_space=pl.ANY)


def _dot(a, b):
    return jnp.dot(a, b, preferred_element_type=F32)


def _dot_nt(a, b):
    return lax.dot_general(a, b, (((1,), (1,)), ((), ())), preferred_element_type=F32)


def _dot_tn(a, b):
    return lax.dot_general(a, b, (((0,), (0,)), ((), ())), preferred_element_type=F32)


def _sigmoid(x):
    return jax.nn.sigmoid(x)


def _dsilu(x, s):
    return s * (1.0 + x * (1.0 - s))


def _row_tile(rows, want):
    t = min(rows, want)
    while rows % t:
        t //= 2
    return t


def _place():
    x, y, c = lax.axis_index("x"), lax.axis_index("y"), lax.axis_index("c")
    chips = [(1 - x, y), (x, 1 - y), (1 - x, 1 - y)]
    return x, y, c, chips


def _remote(src, dst, sems, k, to):
    return pltpu.make_async_remote_copy(src_ref=src, dst_ref=dst, send_sem=sems[0].at[k], recv_sem=sems[1].at[k],
                                        device_id=to, device_id_type=MESH_ID)


class _GatherPlan:
    def __init__(self, items):
        self.items = items
        self.inputs = [a for a, _, _ in items]
        self.out_shapes = []
        for a, kind, _ in items:
            shp = (N_CHIPS,) + a.shape[1:] if kind == "rows" else (a.shape[0], N_CHIPS) + a.shape[1:]
            self.out_shapes.append(jax.ShapeDtypeStruct(shp, a.dtype))
        n = len(items)
        self.scratch = [pltpu.SemaphoreType.DMA((6 * n,)), pltpu.SemaphoreType.DMA((6 * n,)),
                        pltpu.SemaphoreType.DMA((2 * n,))]

    def _views(self, i, src, dst):
        _, kind, l = self.items[i]
        if kind == "rows":
            half = src.shape[1] // 2
            part = lambda core: src.at[l, pl.ds(core * half, half), :]
            land = lambda chip, core: dst.at[chip, pl.ds(core * half, half), :]
        else:
            part = lambda core: src.at[core]
            land = lambda chip, core: dst.at[core, chip]
        return part, land

    def start(self, srcs, dsts, sems):
        x, y, c, chips = _place()
        me = 2 * x + y
        for i, (src, dst) in enumerate(zip(srcs, dsts)):
            part, land = self._views(i, src, dst)
            for core in range(2):
                pltpu.make_async_copy(part(core), land(me, core), sems[2].at[2 * i + core]).start()
            for j, (px, py) in enumerate(chips):
                _remote(part(c), land(me, c), sems, 6 * i + j, (px, py, c)).start()

    def finish(self, srcs, dsts, sems):
        x, y, c, chips = _place()
        me = 2 * x + y
        sibling = (x, y, 1 - c)
        for i, (src, dst) in enumerate(zip(srcs, dsts)):
            part, land = self._views(i, src, dst)
            for j, (px, py) in enumerate(chips):
                got = land(2 * px + py, c)
                _remote(got, got, sems, 6 * i + j, (px, py, c)).wait_recv()
                _remote(got, got, sems, 6 * i + 3 + j, sibling).start()
        for i, (src, dst) in enumerate(zip(srcs, dsts)):
            part, land = self._views(i, src, dst)
            for j, (px, py) in enumerate(chips):
                got = land(2 * px + py, 1 - c)
                _remote(got, got, sems, 6 * i + 3 + j, sibling).wait_recv()
            for j, (px, py) in enumerate(chips):
                _remote(part(c), land(me, c), sems, 6 * i + j, (px, py, c)).wait_send()
                mine = land(2 * px + py, c)
                _remote(mine, mine, sems, 6 * i + 3 + j, sibling).wait_send()
            for core in range(2):
                pltpu.make_async_copy(part(core), land(me, core), sems[2].at[2 * i + core]).wait()


class _ScatterPlan:
    def __init__(self, items):
        self.inputs = list(items)
        self.out_shapes = [jax.ShapeDtypeStruct((3,) + a.shape[1:], a.dtype) for a in items]
        n = len(items)
        self.scratch = [pltpu.SemaphoreType.DMA((3 * n,)), pltpu.SemaphoreType.DMA((3 * n,))]

    def _copies(self, srcs, dsts, sems):
        x, y, c, chips = _place()
        return [_remote(src.at[2 * px + py], dst.at[j], sems, 3 * i + j, (px, py, c))
                for i, (src, dst) in enumerate(zip(srcs, dsts)) for j, (px, py) in enumerate(chips)]

    def start(self, srcs, dsts, sems):
        for cp in self._copies(srcs, dsts, sems):
            cp.start()

    def finish(self, srcs, dsts, sems):
        copies = self._copies(srcs, dsts, sems)
        for cp in copies:
            cp.wait_recv()
        for cp in copies:
            cp.wait_send()


class _SiblingPlan:
    def __init__(self, items):
        self.inputs = list(items)
        self.out_shapes = [jax.ShapeDtypeStruct(a.shape, a.dtype) for a in items]
        n = len(items)
        self.scratch = [pltpu.SemaphoreType.DMA((n,)), pltpu.SemaphoreType.DMA((n,))]

    def _copies(self, srcs, dsts, sems):
        x, y, c, _ = _place()
        return [_remote(src, dst, sems, i, (x, y, 1 - c)) for i, (src, dst) in enumerate(zip(srcs, dsts))]

    def start(self, srcs, dsts, sems):
        for cp in self._copies(srcs, dsts, sems):
            cp.start()

    def finish(self, srcs, dsts, sems):
        copies = self._copies(srcs, dsts, sems)
        for cp in copies:
            cp.wait_recv()
        for cp in copies:
            cp.wait_send()


class _EveryonePlan:
    def __init__(self, block):
        self.inputs = [block]
        self.m = block.shape[0]
        self.out_shapes = [jax.ShapeDtypeStruct((8 * self.m,) + block.shape[1:], block.dtype)]
        self.scratch = [pltpu.SemaphoreType.DMA((7,)), pltpu.SemaphoreType.DMA((7,)), pltpu.SemaphoreType.DMA((1,))]

    def _rows(self, dst, px, py, pc):
        return dst.at[pl.ds((4 * px + 2 * py + pc) * self.m, self.m), :]

    def start(self, srcs, dsts, sems):
        x, y, c, chips = _place()
        src, dst = srcs[0], dsts[0]
        pltpu.make_async_copy(src, self._rows(dst, x, y, c), sems[2].at[0]).start()
        _remote(src, self._rows(dst, x, y, c), sems, 0, (x, y, 1 - c)).start()
        for j, (px, py) in enumerate(chips):
            _remote(src, self._rows(dst, x, y, c), sems, 1 + j, (px, py, c)).start()

    def finish(self, srcs, dsts, sems):
        x, y, c, chips = _place()
        src, dst = srcs[0], dsts[0]
        sibling = (x, y, 1 - c)
        for j, (px, py) in enumerate(chips):
            got = self._rows(dst, px, py, c)
            _remote(got, got, sems, 1 + j, (px, py, c)).wait_recv()
            _remote(got, got, sems, 4 + j, sibling).start()
        sib = self._rows(dst, x, y, 1 - c)
        _remote(sib, sib, sems, 0, sibling).wait_recv()
        for j, (px, py) in enumerate(chips):
            got = self._rows(dst, px, py, 1 - c)
            _remote(got, got, sems, 4 + j, sibling).wait_recv()
        mine = self._rows(dst, x, y, c)
        _remote(src, mine, sems, 0, sibling).wait_send()
        for j, (px, py) in enumerate(chips):
            _remote(src, mine, sems, 1 + j, (px, py, c)).wait_send()
            got = self._rows(dst, px, py, c)
            _remote(got, got, sems, 4 + j, sibling).wait_send()
        pltpu.make_async_copy(src, mine, sems[2].at[0]).wait()


def _call(body, *, name, grid, in_specs, out_specs, out_shape, args, scratch=(), parallel=(), plan=None):
    n_in, n_out, n_scr = len(in_specs), len(out_shape), len(scratch)
    sem = tuple("parallel" if (a in parallel and plan is None) else "arbitrary" for a in range(len(grid)))
    params = pltpu.CompilerParams(dimension_semantics=sem, vmem_limit_bytes=VMEM_LIMIT)
    if plan is None:
        outs = pl.pallas_call(body, name=name, grid=grid, in_specs=in_specs, out_specs=out_specs,
                              out_shape=out_shape, scratch_shapes=list(scratch), compiler_params=params)(*args)
        return list(outs), []
    p_in, p_out, p_scr = len(plan.inputs), len(plan.out_shapes), len(plan.scratch)

    def wrapped(*refs):
        ins, refs = refs[:n_in], refs[n_in:]
        p_ins, refs = refs[:p_in], refs[p_in:]
        outs, refs = refs[:n_out], refs[n_out:]
        p_outs, refs = refs[:p_out], refs[p_out:]
        scr, p_sems = refs[:n_scr], refs[n_scr:]
        ids = [pl.program_id(a) for a in range(len(grid))]
        first = _all([i == 0 for i in ids])
        last = _all([i == n - 1 for i, n in zip(ids, grid)])

        @pl.when(first)
        def _():
            plan.start(p_ins, p_outs, p_sems)

        body(*ins, *outs, *scr)

        @pl.when(last)
        def _():
            plan.finish(p_ins, p_outs, p_sems)

    outs = pl.pallas_call(
        wrapped, name=name, grid=grid,
        in_specs=list(in_specs) + [ANY] * p_in, out_specs=list(out_specs) + [ANY] * p_out,
        out_shape=list(out_shape) + list(plan.out_shapes),
        scratch_shapes=list(scratch) + list(plan.scratch), compiler_params=params,
    )(*args, *plan.inputs)
    return list(outs[:n_out]), list(outs[n_out:])


def _all(conds):
    out = conds[0]
    for c in conds[1:]:
        out = out & c
    return out


class _Together:
    def __init__(self, plans):
        self.plans = plans
        self.inputs = [a for p in plans for a in p.inputs]
        self.out_shapes = [s for p in plans for s in p.out_shapes]
        self.scratch = [s for p in plans for s in p.scratch]

    def _split(self, refs, count):
        out, at = [], 0
        for p in self.plans:
            out.append(refs[at:at + count(p)])
            at += count(p)
        return out

    def _parts(self, srcs, dsts, sems):
        return zip(self.plans, self._split(srcs, lambda p: len(p.inputs)),
                   self._split(dsts, lambda p: len(p.out_shapes)), self._split(sems, lambda p: len(p.scratch)))

    def start(self, srcs, dsts, sems):
        for p, s, d, m in self._parts(srcs, dsts, sems):
            p.start(s, d, m)

    def finish(self, srcs, dsts, sems):
        for p, s, d, m in self._parts(srcs, dsts, sems):
            p.finish(s, d, m)


def _run_plan(plan, *, name):
    p_in, p_out = len(plan.inputs), len(plan.out_shapes)

    def body(*refs):
        srcs, dsts, sems = refs[:p_in], refs[p_in:p_in + p_out], refs[p_in + p_out:]
        plan.start(srcs, dsts, sems)
        plan.finish(srcs, dsts, sems)

    return list(pl.pallas_call(body, name=name, in_specs=[ANY] * p_in, out_specs=[ANY] * p_out,
                               out_shape=list(plan.out_shapes), scratch_shapes=list(plan.scratch))(*plan.inputs))


def _rms(xf, g):
    r = lax.rsqrt(jnp.mean(xf * xf, axis=-1, keepdims=True) + EPS)
    return (xf * r * g).astype(BF16)


def _rmsnorm(x, g, *, name, plan=None):
    S, D = x.shape
    tm = _row_tile(S, 1024)

    def body(x_ref, g_ref, xn_ref):
        xn_ref[...] = _rms(x_ref[...], g_ref[...])

    return _call(
        body, name=name, grid=(S // tm,), parallel=(0,), plan=plan,
        in_specs=[pl.BlockSpec((tm, D), lambda i: (i, 0)), pl.BlockSpec((1, D), lambda i: (0, 0))],
        out_specs=[pl.BlockSpec((tm, D), lambda i: (i, 0))],
        out_shape=[jax.ShapeDtypeStruct((S, D), BF16)],
        args=(x, g))


def _matmul(xn, w4, *, name, out_dtype=F32, plan=None):
    S, D = xn.shape
    n4 = w4.shape[2]
    tm = _row_tile(S, 2048)

    def body(xn_ref, w_ref, o_ref):
        o_ref[...] = _dot(xn_ref[...], w_ref[...]).astype(out_dtype)

    return _call(
        body, name=name, grid=(S // tm, N_CHIPS), parallel=(0,), plan=plan,
        in_specs=[pl.BlockSpec((tm, D), lambda i, j: (i, 0)),
                  pl.BlockSpec((None, D, n4), lambda i, j: (j, 0, 0))],
        out_specs=[pl.BlockSpec((tm, n4), lambda i, j: (i, j))],
        out_shape=[jax.ShapeDtypeStruct((S, N_CHIPS * n4), out_dtype)],
        args=(xn, w4))


def _segments(widths, n4):
    per_chip = [[] for _ in range(N_CHIPS)]
    c0 = 0
    for p, w in enumerate(widths):
        a = c0
        while a < c0 + w:
            k = a // n4
            b = min(c0 + w, (k + 1) * n4)
            per_chip[k].append((p, (a - c0, b - c0), (a - k * n4, b - k * n4)))
            a = b
        c0 += w
    assert c0 == N_CHIPS * n4
    return per_chip


def _piece_specs(pieces, n4, tm):
    per_chip = _segments([p.shape[1] for p in pieces], n4)
    specs, local, start = [], [[] for _ in range(N_CHIPS)], 0
    for p, arr in enumerate(pieces):
        chips = [k for k in range(N_CHIPS) if any(seg[0] == p for seg in per_chip[k])]
        lo, hi = chips[0], chips[-1]
        tiled = arr.shape[1] % n4 == 0 and start % n4 == 0
        start += arr.shape[1]
        if tiled:
            imap = lambda k, i, lo=lo, hi=hi: (jnp.where((k >= lo) & (k <= hi), i, 0), jnp.clip(k - lo, 0, hi - lo))
            specs.append(pl.BlockSpec((tm, n4), imap))
        else:
            imap = lambda k, i, lo=lo, hi=hi: (jnp.where((k >= lo) & (k <= hi), i, 0), 0)
            specs.append(pl.BlockSpec((tm, arr.shape[1]), imap))
        for k in chips:
            for q, (pa, pb), cols in per_chip[k]:
                if q == p:
                    local[k].append((p, (0, n4) if tiled else (pa, pb), cols))
    return specs, local


def _dgrad_norm(dys, w4, x, g, dres, *, name, plan=None):
    S, D = x.shape
    n4 = w4.shape[2]
    tm = _row_tile(S, 512)
    per_chip = _segments([a.shape[1] for a in dys], n4)
    n_p = len(dys)

    def body(*refs):
        dy_refs = refs[:n_p]
        w_ref, x_ref, g_ref, dres_ref, dx_ref, dg_ref = refs[n_p:]

        @pl.when(pl.program_id(0) == 0)
        def _():
            dg_ref[...] = jnp.zeros_like(dg_ref)

        dxn = None
        for k in range(N_CHIPS):
            for p, (pa, pb), (ca, cb) in per_chip[k]:
                part = _dot_nt(dy_refs[p][:, pa:pb], w_ref[k, :, ca:cb])
                dxn = part if dxn is None else dxn + part
        xf = x_ref[...]
        r = lax.rsqrt(jnp.mean(xf * xf, axis=-1, keepdims=True) + EPS)
        xhat = xf * r
        dxhat = dxn * g_ref[...]
        dx_ref[...] = dres_ref[...] + r * (dxhat - xhat * jnp.mean(dxhat * xhat, axis=-1, keepdims=True))
        dg_ref[...] += jnp.sum(dxn * xhat, axis=0, keepdims=True)

    row = lambda w: pl.BlockSpec((tm, w), lambda i: (i, 0))
    return _call(
        body, name=name, grid=(S // tm,), plan=plan,
        in_specs=[row(a.shape[1]) for a in dys]
        + [pl.BlockSpec(w4.shape, lambda i: (0, 0, 0), pipeline_mode=pl.Buffered(1)),
           row(D), pl.BlockSpec((1, D), lambda i: (0, 0)), row(D)],
        out_specs=[row(D), pl.BlockSpec((1, D), lambda i: (0, 0))],
        out_shape=[jax.ShapeDtypeStruct((S, D), F32), jax.ShapeDtypeStruct((1, D), F32)],
        args=(*dys, w4, x, g, dres))


def _wgrad(a, dys, *, name, rows, plan=None):
    S, K = a.shape
    n4 = sum(p.shape[1] for p in dys) // N_CHIPS
    tm = _row_tile(S, rows)
    ns = S // tm
    specs, local = _piece_specs(dys, n4, tm)
    n_p = len(dys)

    def body(*refs):
        a_ref = refs[0]
        dy_refs = refs[1:1 + n_p]
        o_ref, o16_ref = refs[1 + n_p:]
        n, s = pl.program_id(0), pl.program_id(1)

        @pl.when(s == 0)
        def _():
            o_ref[...] = jnp.zeros_like(o_ref)

        for k in range(N_CHIPS):
            @pl.when(n == k)
            def _():
                av = a_ref[...]
                for p, (pa, pb), (ca, cb) in local[k]:
                    o_ref[:, ca:cb] += _dot_tn(av, dy_refs[p][:, pa:pb])

        @pl.when(s == ns - 1)
        def _():
            o16_ref[...] = o_ref[...].astype(BF16)

    out = pl.BlockSpec((None, K, n4), lambda n, s: (n, 0, 0))
    return _call(
        body, name=name, grid=(N_CHIPS, ns), parallel=(0,), plan=plan,
        in_specs=[pl.BlockSpec((tm, K), lambda n, s: (s, 0))] + specs,
        out_specs=[out, out],
        out_shape=[jax.ShapeDtypeStruct((N_CHIPS, K, n4), F32), jax.ShapeDtypeStruct((N_CHIPS, K, n4), BF16)],
        args=(a, *dys))


def _tiles(x):
    return x.reshape(x.shape[0] // SUBLANES, SUBLANES, x.shape[1])


def _shift_down(xp, s):
    n = xp.shape[0] - SUBLANES
    if s == SUBLANES:
        return xp[:n, :]
    t = _tiles(xp)
    rot = pltpu.roll(t, s, 1)
    sub = lax.broadcasted_iota(jnp.int32, t.shape, 1)[1:]
    return jnp.where(sub >= s, rot[1:], rot[:-1]).reshape(n, xp.shape[1])


def _shift_up(xn, s):
    n = xn.shape[0] - SUBLANES
    if s == SUBLANES:
        return xn[SUBLANES:, :]
    t = _tiles(xn)
    rot = pltpu.roll(t, SUBLANES - s, 1)
    sub = lax.broadcasted_iota(jnp.int32, t.shape, 1)[1:]
    return jnp.where(sub < SUBLANES - s, rot[:-1], rot[1:]).reshape(n, xn.shape[1])


def _pooled(u, halo, first_tile, row0):
    T = u.shape[0]
    halo = jnp.where(first_tile, 0.0, halo)
    pad = jnp.zeros((SUBLANES, u.shape[1]), F32)
    up = jnp.concatenate([pad, halo, u], axis=0)
    t1 = (row0 + lax.broadcasted_iota(jnp.int32, (T, 1), 0) + 1).astype(F32)
    outs = []
    for gi, w in enumerate(POOL_WINDOWS):
        s = up[:, gi * LANES:(gi + 1) * LANES]
        k = 1
        while k < w:
            if k < SUBLANES:
                s = jnp.concatenate([s[:SUBLANES, :], s[SUBLANES:, :] + _shift_down(s, k)], axis=0)
            else:
                s = s[SUBLANES:, :] + _shift_down(s, k)
            k *= 2
        s = s[-T:, :]
        inv = 1.0 / jnp.minimum(t1, float(w))
        outs.append(s * inv - u[:, gi * LANES:(gi + 1) * LANES])
    return outs


def _pool_fwd(z, pool_w, pool_scale, *, name):
    S = z.shape[0]
    P = pool_scale.shape[1]
    T = _row_tile(S, 512)
    hb = T // HALO_POOL

    def body(u_ref, halo_ref, pw_ref, ps_ref, o_ref):
        i = pl.program_id(0)
        pooled = _pooled(u_ref[...], halo_ref[...], i == 0, i * T)
        for gi in range(len(POOL_WINDOWS)):
            mixed = _dot(pooled[gi].astype(BF16), pw_ref[gi].astype(BF16))
            cols = slice(gi * LANES, (gi + 1) * LANES)
            o_ref[:, cols] = (mixed * ps_ref[:, cols]).astype(BF16)

    return _call(
        body, name=name, grid=(S // T,), parallel=(0,),
        in_specs=[pl.BlockSpec((T, P), lambda i: (i, 0)),
                  pl.BlockSpec((HALO_POOL, P), lambda i: (jnp.maximum(i * hb - 1, 0), 0)),
                  pl.BlockSpec(pool_w.shape, lambda i: (0, 0, 0)),
                  pl.BlockSpec((1, P), lambda i: (0, 0))],
        out_specs=[pl.BlockSpec((T, P), lambda i: (i, 0))],
        out_shape=[jax.ShapeDtypeStruct((S, P), BF16)],
        args=(z, z, pool_w, pool_scale))[0][0]


def _pool_bwd(z, dpm, pool_w, pool_scale, *, name):
    S, P = dpm.shape
    T = _row_tile(S, 512)
    hb = T // HALO_POOL
    nt = S // T

    def body(u_ref, halo_ref, d_ref, dnext_ref, pw_ref, ps_ref, du_ref, dpw_ref, dps_ref):
        i = pl.program_id(0)

        @pl.when(i == 0)
        def _():
            dpw_ref[...] = jnp.zeros_like(dpw_ref)
            dps_ref[...] = jnp.zeros_like(dps_ref)

        pooled = _pooled(u_ref[...], halo_ref[...], i == 0, i * T)
        dnext = jnp.where(i == nt - 1, 0.0, dnext_ref[...])
        pad = jnp.zeros((SUBLANES, P), F32)
        dext = jnp.concatenate([d_ref[...], dnext, pad], axis=0)
        t1 = (i * T + lax.broadcasted_iota(jnp.int32, (T + HALO_POOL + SUBLANES, 1), 0) + 1).astype(F32)
        for gi, w in enumerate(POOL_WINDOWS):
            cols = slice(gi * LANES, (gi + 1) * LANES)
            pw = pw_ref[gi].astype(BF16)
            pg = pooled[gi].astype(BF16)
            mixed = _dot(pg, pw)
            dps_ref[:, cols] += jnp.sum(d_ref[:, cols] * mixed, axis=0, keepdims=True)
            dmixed = (dext[:, cols] * ps_ref[:, cols]).astype(BF16)
            dpw_ref[gi] += _dot_tn(pg, dmixed[:T, :])
            dpooled = _dot_nt(dmixed, pw)
            e = dpooled * (1.0 / jnp.minimum(t1, float(w)))
            k = 1
            while k < w:
                if k < SUBLANES:
                    e = jnp.concatenate([e[:-SUBLANES, :] + _shift_up(e, k), e[-SUBLANES:, :]], axis=0)
                else:
                    e = e[:-SUBLANES, :] + _shift_up(e, k)
                k *= 2
            du_ref[:, cols] = (e[:T, :] - dpooled[:T, :]).astype(BF16)

    return _call(
        body, name=name, grid=(nt,),
        in_specs=[pl.BlockSpec((T, P), lambda i: (i, 0)),
                  pl.BlockSpec((HALO_POOL, P), lambda i: (jnp.maximum(i * hb - 1, 0), 0)),
                  pl.BlockSpec((T, P), lambda i: (i, 0)),
                  pl.BlockSpec((HALO_POOL, P), lambda i: (jnp.minimum((i + 1) * hb, S // HALO_POOL - 1), 0)),
                  pl.BlockSpec(pool_w.shape, lambda i: (0, 0, 0)),
                  pl.BlockSpec((1, P), lambda i: (0, 0))],
        out_specs=[pl.BlockSpec((T, P), lambda i: (i, 0)),
                   pl.BlockSpec(pool_w.shape, lambda i: (0, 0, 0)),
                   pl.BlockSpec((1, P), lambda i: (0, 0))],
        out_shape=[jax.ShapeDtypeStruct((S, P), BF16),
                   jax.ShapeDtypeStruct(pool_w.shape, F32),
                   jax.ShapeDtypeStruct((1, P), F32)],
        args=(z, z, dpm, dpm, pool_w, pool_scale))[0]


def _cumsum_rows(x):
    n = x.shape[0]
    row = lax.broadcasted_iota(jnp.int32, x.shape, 0)
    s = 1
    while s < n:
        x = x + jnp.where(row >= s, pltpu.roll(x, s, 0), 0.0)
        s *= 2
    return x


def _rev_cumsum_rows(x):
    n = x.shape[0]
    row = lax.broadcasted_iota(jnp.int32, x.shape, 0)
    s = 1
    while s < n:
        x = x + jnp.where(row < n - s, pltpu.roll(x, n - s, 0), 0.0)
        s *= 2
    return x


def _chunk_prep(zq, zf, lb, b_ref):
    n_sub = CHUNK // SUB
    sq = _sigmoid(zq)
    q = zq * sq
    sf = _sigmoid(zf)
    f = lb + (1.0 - lb) * sf
    k = 1.0 - f
    b = _cumsum_rows(jnp.log(f))
    b_ref[...] = b
    shape = (SUB, b.shape[1])
    ends = [jnp.broadcast_to(b_ref[pl.ds(SUB * j + SUB - 1, 1), :], shape) for j in range(n_sub)]
    mids = [jnp.broadcast_to(b_ref[pl.ds(SUB * j + SUB // 2 - 1, 1), :], shape) for j in range(n_sub)]
    own = [b[SUB * j:SUB * (j + 1), :] for j in range(n_sub)]
    m0 = jnp.concatenate(mids, axis=0)
    e1 = jnp.concatenate(ends, axis=0)
    eq = [jnp.exp(jnp.minimum(b - m0, EXP_CLAMP))]
    for d in range(1, n_sub):
        rd = jnp.concatenate([own[j] if j < d else ends[j - d] for j in range(n_sub)], axis=0)
        eq.append(jnp.exp(b - rd))
    ek0 = jnp.exp(jnp.minimum(m0 - b, EXP_CLAMP))
    ek1 = jnp.exp(e1 - b)
    b_last = b_ref[pl.ds(CHUNK - 1, 1), :]
    return dict(q=q, k=k, f=f, sq=sq, sf=sf, b=b, eq=eq, ek0=ek0, ek1=ek1,
                eb=jnp.exp(b), ekl=jnp.exp(b_last - b), el=jnp.exp(b_last))


def _chunk_masks():
    ti = lax.broadcasted_iota(jnp.int32, (CHUNK, CHUNK), 0)
    si = lax.broadcasted_iota(jnp.int32, (CHUNK, CHUNK), 1)
    shift = SUB.bit_length() - 1
    dsub = jnp.right_shift(ti, shift) - jnp.right_shift(si, shift)
    masks = [(dsub == 0) & (si <= ti)]
    masks += [dsub == d for d in range(1, CHUNK // SUB)]
    return masks


def _chunk_attn(p, masks):
    qd = [(p["q"] * e).astype(BF16) for e in p["eq"]]
    k0 = (p["k"] * p["ek0"]).astype(BF16)
    k1 = (p["k"] * p["ek1"]).astype(BF16)
    a = jnp.where(masks[0], _dot_nt(qd[0], k0), 0.0)
    for d in range(1, len(masks)):
        a = jnp.where(masks[d], _dot_nt(qd[d], k1), a)
    return a, qd, k0, k1


def _hgrn_fwd(z, lb, norm_g, *, name, plan=None):
    S = z.shape[0]
    HW = lb.shape[1]
    NH = HW // LANES
    T = _row_tile(S, 512)
    nc = T // CHUNK

    def body(zq_ref, zf_ref, zi_ref, zo_ref, lb_ref, ng_ref, o_ref, of_ref, st_ref, s_scr, b_scr):
        @pl.when(pl.program_id(0) == 0)
        def _():
            s_scr[...] = jnp.zeros_like(s_scr)

        ng = ng_ref[...]
        masks = _chunk_masks()

        def chunk(c, carry):
            rows = pl.ds(pl.multiple_of(c * CHUNK, CHUNK), CHUNK)
            for h in range(NH):
                cols = slice(h * LANES, (h + 1) * LANES)
                p = _chunk_prep(zq_ref[rows, cols], zf_ref[rows, cols], lb_ref[:, cols], b_scr.at[h])
                v = zi_ref[rows, cols].astype(BF16)
                zo = zo_ref[rows, cols]
                st = s_scr[h]
                st_ref[c, h] = st
                a, _, _, _ = _chunk_attn(p, masks)
                o = _dot(a.astype(BF16), v) + _dot_nt((p["q"] * p["eb"]).astype(BF16), st.astype(BF16))
                s_scr[h] = st * p["el"] + _dot_tn(v, (p["k"] * p["ekl"]).astype(BF16))
                o_ref[rows, cols] = o
                r = lax.rsqrt(jnp.mean(o * o, axis=-1, keepdims=True) + EPS)
                of_ref[rows, cols] = (o * r * ng * (zo * _sigmoid(zo))).astype(BF16)
            return carry

        lax.fori_loop(0, nc, chunk, 0, unroll=4)

    part = lambda k: pl.BlockSpec((T, HW), lambda i, k=k: (i, k))
    return _call(
        body, name=name, grid=(S // T,), plan=plan,
        in_specs=[part(1), part(2), part(3), part(4),
                  pl.BlockSpec((1, HW), lambda i: (0, 0)), pl.BlockSpec((1, LANES), lambda i: (0, 0))],
        out_specs=[pl.BlockSpec((T, HW), lambda i: (i, 0)), pl.BlockSpec((T, HW), lambda i: (i, 0)),
                   pl.BlockSpec((nc, NH, LANES, LANES), lambda i: (i, 0, 0, 0))],
        out_shape=[jax.ShapeDtypeStruct((S, HW), F32), jax.ShapeDtypeStruct((S, HW), BF16),
                   jax.ShapeDtypeStruct((S // CHUNK, NH, LANES, LANES), F32)],
        scratch=[pltpu.VMEM((NH, LANES, LANES), F32), pltpu.VMEM((NH, CHUNK, LANES), F32)],
        args=(z, z, z, z, lb, norm_g))


def _hgrn_bwd(z, lb, norm_g, o_raw, states, dof, *, name, plan=None):
    S = z.shape[0]
    HW = lb.shape[1]
    NH = HW // LANES
    T = _row_tile(S, 512)
    nc = T // CHUNK
    nt = S // T

    def body(zq_ref, zf_ref, zi_ref, zo_ref, lb_ref, ng_ref, o_ref, st_ref, dof_ref,
             dzq_ref, dzf_ref, dzi_ref, dzo_ref, dlb_ref, dng_ref, ds_scr, b_scr):
        @pl.when(pl.program_id(0) == 0)
        def _():
            ds_scr[...] = jnp.zeros_like(ds_scr)
            dlb_ref[...] = jnp.zeros_like(dlb_ref)
            dng_ref[...] = jnp.zeros_like(dng_ref)

        ng = ng_ref[...]
        masks = _chunk_masks()
        last_row = lax.broadcasted_iota(jnp.int32, (CHUNK, 1), 0) == CHUNK - 1

        def chunk(cr, carry):
            c = nc - 1 - cr
            rows = pl.ds(pl.multiple_of(c * CHUNK, CHUNK), CHUNK)
            for h in range(NH):
                cols = slice(h * LANES, (h + 1) * LANES)
                lbv = lb_ref[:, cols]
                zq, zf, zo = zq_ref[rows, cols], zf_ref[rows, cols], zo_ref[rows, cols]
                o = o_ref[rows, cols]
                dof_c = dof_ref[rows, cols]
                st = st_ref[c, h]
                dst = ds_scr[h]

                so = _sigmoid(zo)
                r = lax.rsqrt(jnp.mean(o * o, axis=-1, keepdims=True) + EPS)
                ohat = o * r
                d_on = dof_c * (zo * so)
                dzo_ref[rows, cols] = (dof_c * ohat * ng * _dsilu(zo, so)).astype(BF16)
                dng_ref[:, cols] += jnp.sum(d_on * ohat, axis=0, keepdims=True)
                dohat = d_on * ng
                do = (r * (dohat - ohat * jnp.mean(dohat * ohat, axis=-1, keepdims=True))).astype(BF16)

                p = _chunk_prep(zq, zf, lbv, b_scr.at[h])
                q, k = p["q"], p["k"]
                v = zi_ref[rows, cols].astype(BF16)
                a, qd, k0, k1 = _chunk_attn(p, masks)
                ktl = (k * p["ekl"]).astype(BF16)
                dstb = dst.astype(BF16)

                da = _dot_nt(do, v)
                dzi_ref[rows, cols] = (_dot_tn(a.astype(BF16), do) + _dot_nt(ktl, dstb)).astype(BF16)

                da0 = jnp.where(masks[0], da, 0.0).astype(BF16)
                rq = _dot(da0, k0)
                rk0 = _dot_tn(da0, qd[0])
                dq = rq * p["eq"][0]
                db = qd[0].astype(F32) * rq - k0.astype(F32) * rk0
                rk1 = jnp.zeros_like(rk0)
                for d in range(1, len(masks)):
                    dad = jnp.where(masks[d], da, 0.0).astype(BF16)
                    rq = _dot(dad, k1)
                    dq = dq + rq * p["eq"][d]
                    db = db + qd[d].astype(F32) * rq
                    rk1 = rk1 + _dot_tn(dad, qd[d])
                dk = rk0 * p["ek0"] + rk1 * p["ek1"]
                db = db - k1.astype(F32) * rk1
                qe = (q * p["eb"]).astype(BF16)
                rq = _dot(do, st.astype(BF16))
                dq = dq + rq * p["eb"]
                db = db + qe.astype(F32) * rq
                rk = _dot(v, dstb)
                dk = dk + rk * p["ekl"]
                db = db - ktl.astype(F32) * rk

                st_new = st * p["el"] + _dot_tn(v, ktl)
                db = db + jnp.where(last_row, jnp.sum(dstb.astype(F32) * st_new, axis=0, keepdims=True), 0.0)
                dg = _rev_cumsum_rows(db)
                ds_scr[h] = dst * p["el"] + _dot_tn(do, qe)

                dzq_ref[rows, cols] = (dq * _dsilu(zq, p["sq"])).astype(BF16)
                df = dg / p["f"] - dk
                sf = p["sf"]
                dzf_ref[rows, cols] = (df * (1.0 - lbv) * sf * (1.0 - sf)).astype(BF16)
                dlb_ref[:, cols] += jnp.sum(df * (1.0 - sf), axis=0, keepdims=True)
            return carry

        lax.fori_loop(0, nc, chunk, 0, unroll=4)

    rev = lambda i: nt - 1 - i
    part = lambda k: pl.BlockSpec((T, HW), lambda i, k=k: (rev(i), k))
    blk = pl.BlockSpec((T, HW), lambda i: (rev(i), 0))
    vec = pl.BlockSpec((1, HW), lambda i: (0, 0))
    return _call(
        body, name=name, grid=(nt,), plan=plan,
        in_specs=[part(1), part(2), part(3), part(4), vec, pl.BlockSpec((1, LANES), lambda i: (0, 0)),
                  blk, pl.BlockSpec((nc, NH, LANES, LANES), lambda i: (rev(i), 0, 0, 0)), blk],
        out_specs=[blk, blk, blk, blk, vec, vec],
        out_shape=[jax.ShapeDtypeStruct((S, HW), BF16)] * 4 + [jax.ShapeDtypeStruct((1, HW), F32)] * 2,
        scratch=[pltpu.VMEM((NH, LANES, LANES), F32), pltpu.VMEM((NH, CHUNK, LANES), F32)],
        args=(z, z, z, z, lb, norm_g, o_raw, states, dof))


def _gate_specs(T, D):
    half = D // 2
    first = (5 * half) // half
    return [pl.BlockSpec((T, half), lambda i, k=k: (i, first + k)) for k in range(4)]


def _gates(zg_refs, bg_ref, D):
    half = D // 2
    za = jnp.concatenate([zg_refs[0][...], zg_refs[1][...]], axis=1) + bg_ref[:, :D]
    zb = jnp.concatenate([zg_refs[2][...], zg_refs[3][...]], axis=1) + bg_ref[:, D:]
    return _sigmoid(za), _sigmoid(zb)


def _mix_fwd(x, pm, of, z, b_gate, w_pa4, w_pb4, w_o4, g_next, *, name, plan=None):
    S, D = x.shape
    P = pm.shape[1]
    T = _row_tile(S, 512)

    def body(x_ref, pm_ref, of_ref, g0, g1, g2, g3, bg_ref, wpa_ref, wpb_ref, wo_ref, gn_ref,
             xo_ref, ya_ref, yb_ref, xn_ref):
        pmv, ofv = pm_ref[...], of_ref[...]
        ya = jnp.concatenate([_dot(pmv, wpa_ref[k]) for k in range(N_CHIPS)], axis=1)
        yb = jnp.concatenate([_dot(ofv, wpb_ref[k]) for k in range(N_CHIPS)], axis=1)
        ga, gb = _gates((g0, g1, g2, g3), bg_ref, D)
        merged = (ga * ya + gb * yb).astype(BF16)
        x_mid = x_ref[...] + _dot(merged, wo_ref[...].reshape(D, D))
        xo_ref[...] = x_mid
        xn_ref[...] = _rms(x_mid, gn_ref[...])
        ya_ref[...] = ya.astype(BF16)
        yb_ref[...] = yb.astype(BF16)

    row = lambda w: pl.BlockSpec((T, w), lambda i: (i, 0))
    full = lambda a: pl.BlockSpec(a.shape, lambda i: (0,) * a.ndim)
    return _call(
        body, name=name, grid=(S // T,), parallel=(0,), plan=plan,
        in_specs=[row(D), row(P), row(P)] + _gate_specs(T, D) + [full(b_gate), full(w_pa4), full(w_pb4), full(w_o4),
                                                                  full(g_next)],
        out_specs=[row(D), row(D), row(D), row(D)],
        out_shape=[jax.ShapeDtypeStruct((S, D), F32), jax.ShapeDtypeStruct((S, D), BF16),
                   jax.ShapeDtypeStruct((S, D), BF16), jax.ShapeDtypeStruct((S, D), BF16)],
        args=(x, pm, of, z, z, z, z, b_gate, w_pa4, w_pb4, w_o4, g_next))


def _mix_bwd(dxm, ya, yb, z, b_gate, pm, of, w_pa4, w_pb4, w_o4, *, name, plan=None):
    S, D = dxm.shape
    P = pm.shape[1]
    q4 = D // N_CHIPS
    T = _row_tile(S, 512)
    nt = S // T

    def body(dx_ref, ya_ref, yb_ref, g0, g1, g2, g3, bg_ref, pm_ref, of_ref, wpa_ref, wpb_ref, wo_ref,
             dzg_ref, dpm_ref, dof_ref, dwo_ref, dwpa_ref, dwpb_ref, dbg_ref, dwo16_ref, dwpa16_ref, dwpb16_ref):
        i = pl.program_id(0)

        @pl.when(i == 0)
        def _():
            dwo_ref[...] = jnp.zeros_like(dwo_ref)
            dwpa_ref[...] = jnp.zeros_like(dwpa_ref)
            dwpb_ref[...] = jnp.zeros_like(dwpb_ref)
            dbg_ref[...] = jnp.zeros_like(dbg_ref)

        dxb = dx_ref[...].astype(BF16)
        ya = ya_ref[...].astype(F32)
        yb = yb_ref[...].astype(F32)
        ga, gb = _gates((g0, g1, g2, g3), bg_ref, D)
        merged = (ga * ya + gb * yb).astype(BF16)
        dwo_ref[...] += _dot_tn(merged, dxb).reshape(N_CHIPS, q4, D)
        dm = _dot_nt(dxb, wo_ref[...].reshape(D, D))
        dza = dm * ya * ga * (1.0 - ga)
        dzb = dm * yb * gb * (1.0 - gb)
        dzg_ref[:, :D] = dza.astype(BF16)
        dzg_ref[:, D:] = dzb.astype(BF16)
        dbg_ref[:, :D] += jnp.sum(dza, axis=0, keepdims=True)
        dbg_ref[:, D:] += jnp.sum(dzb, axis=0, keepdims=True)
        dya = (dm * ga).astype(BF16)
        dyb = (dm * gb).astype(BF16)
        pmv, ofv = pm_ref[...], of_ref[...]
        dpm = jnp.zeros((T, P), F32)
        dof = jnp.zeros((T, P), F32)
        for k in range(N_CHIPS):
            cols = slice(k * q4, (k + 1) * q4)
            dwpa_ref[k] += _dot_tn(pmv, dya[:, cols])
            dwpb_ref[k] += _dot_tn(ofv, dyb[:, cols])
            dpm = dpm + _dot_nt(dya[:, cols], wpa_ref[k])
            dof = dof + _dot_nt(dyb[:, cols], wpb_ref[k])
        dpm_ref[...] = dpm
        dof_ref[...] = dof

        @pl.when(i == nt - 1)
        def _():
            dwo16_ref[...] = dwo_ref[...].astype(BF16)
            dwpa16_ref[...] = dwpa_ref[...].astype(BF16)
            dwpb16_ref[...] = dwpb_ref[...].astype(BF16)

    row = lambda w: pl.BlockSpec((T, w), lambda i: (i, 0))
    full = lambda a: pl.BlockSpec(a.shape, lambda i: (0,) * a.ndim, pipeline_mode=pl.Buffered(1))
    like = lambda a, dt: jax.ShapeDtypeStruct(a.shape, dt)
    return _call(
        body, name=name, grid=(nt,), plan=plan,
        in_specs=[row(D), row(D), row(D)] + _gate_specs(T, D) + [full(b_gate), row(P), row(P),
                                                                  full(w_pa4), full(w_pb4), full(w_o4)],
        out_specs=[row(2 * D), row(P), row(P), full(w_o4), full(w_pa4), full(w_pb4), full(b_gate),
                   full(w_o4), full(w_pa4), full(w_pb4)],
        out_shape=[jax.ShapeDtypeStruct((S, 2 * D), BF16), jax.ShapeDtypeStruct((S, P), F32),
                   jax.ShapeDtypeStruct((S, P), F32), like(w_o4, F32), like(w_pa4, F32), like(w_pb4, F32),
                   like(b_gate, F32), like(w_o4, BF16), like(w_pa4, BF16), like(w_pb4, BF16)],
        args=(dxm, ya, yb, z, z, z, z, b_gate, pm, of, w_pa4, w_pb4, w_o4))


def _up_conv(xn, w_up4, conv_w, conv_b, *, name, plan=None):
    S, D = xn.shape
    f4 = w_up4.shape[2]
    nf = N_CHIPS // 2
    F = nf * f4
    T = _row_tile(S, 512)

    def body(xn_ref, wv_ref, wg_ref, cwv_ref, cwg_ref, cbv_ref, cbg_ref,
             hv_ref, hg_ref, val_ref, gate_ref, a_ref, pv_scr, pg_scr):
        i = pl.program_id(1)
        xv = xn_ref[...]

        def side(w_ref, cw_ref, cb_ref, h_ref, p_scr):
            h16 = _dot(xv, w_ref[...]).astype(BF16)
            h_ref[...] = h16
            h = h16.astype(F32)
            hp = jnp.concatenate([jnp.where(i == 0, 0.0, p_scr[...]), h], axis=0)
            p_scr[...] = h[-SUBLANES:, :]
            cw = cw_ref[...]
            return cw[0:1, :] * _shift_down(hp, 2) + cw[1:2, :] * _shift_down(hp, 1) + cw[2:3, :] * h + cb_ref[...]

        val = side(wv_ref, cwv_ref, cbv_ref, hv_ref, pv_scr)
        gate = side(wg_ref, cwg_ref, cbg_ref, hg_ref, pg_scr)
        val_ref[...] = val.astype(BF16)
        gate_ref[...] = gate.astype(BF16)
        a_ref[...] = (gate * _sigmoid(gate) * val).astype(BF16)

    out = pl.BlockSpec((T, f4), lambda f, i: (i, f))
    return _call(
        body, name=name, grid=(nf, S // T), plan=plan,
        in_specs=[pl.BlockSpec((T, D), lambda f, i: (i, 0)),
                  pl.BlockSpec((None, D, f4), lambda f, i: (f, 0, 0)),
                  pl.BlockSpec((None, D, f4), lambda f, i: (nf + f, 0, 0)),
                  pl.BlockSpec((3, f4), lambda f, i: (0, f)),
                  pl.BlockSpec((3, f4), lambda f, i: (0, nf + f)),
                  pl.BlockSpec((1, f4), lambda f, i: (0, f)),
                  pl.BlockSpec((1, f4), lambda f, i: (0, nf + f))],
        out_specs=[out] * 5,
        out_shape=[jax.ShapeDtypeStruct((S, F), BF16)] * 5,
        scratch=[pltpu.VMEM((SUBLANES, f4), F32), pltpu.VMEM((SUBLANES, f4), F32)],
        args=(xn, w_up4, w_up4, conv_w, conv_w, conv_b, conv_b))


def _down(a, w_down4, x, g_next, *, name, plan=None):
    S, F = a.shape
    D = x.shape[1]
    T = _row_tile(S, 1024)

    def body(a_ref, wd_ref, x_ref, gn_ref, o_ref, xn_ref):
        x_out = x_ref[...] + _dot(a_ref[...], wd_ref[...].reshape(F, D))
        o_ref[...] = x_out
        xn_ref[...] = _rms(x_out, gn_ref[...])

    row = lambda w: pl.BlockSpec((T, w), lambda i: (i, 0))
    return _call(
        body, name=name, grid=(S // T,), parallel=(0,), plan=plan,
        in_specs=[row(F), pl.BlockSpec(w_down4.shape, lambda i: (0, 0, 0), pipeline_mode=pl.Buffered(1)),
                  row(D), pl.BlockSpec((1, D), lambda i: (0, 0))],
        out_specs=[row(D), row(D)],
        out_shape=[jax.ShapeDtypeStruct((S, D), F32), jax.ShapeDtypeStruct((S, D), BF16)],
        args=(a, w_down4, x, g_next))


def _ffn_down_bwd(dxo, hv, hg, val16, gate16, conv_w, w_down4, *, name, plan=None):
    S = hv.shape[0]
    _, f4, D = w_down4.shape
    F = N_CHIPS * f4
    T = _row_tile(S, 512)
    nf = 2
    tf = 2 * f4
    nt = S // T

    def body(dx_ref, hv_ref, hg_ref, val_ref, gate_ref, cwv_ref, cwg_ref, wd_ref,
             dhv_ref, dhg_ref, dwd_ref, dwd16_ref, dcwv_ref, dcwg_ref, dcbv_ref, dcbg_ref, cv_scr, cg_scr):
        i = pl.program_id(1)

        @pl.when(i == 0)
        def _():
            cv_scr[...] = jnp.zeros_like(cv_scr)
            cg_scr[...] = jnp.zeros_like(cg_scr)
            dwd_ref[...] = jnp.zeros_like(dwd_ref)
            dcwv_ref[...] = jnp.zeros_like(dcwv_ref)
            dcwg_ref[...] = jnp.zeros_like(dcwg_ref)
            dcbv_ref[...] = jnp.zeros_like(dcbv_ref)
            dcbg_ref[...] = jnp.zeros_like(dcbg_ref)

        dxb = dx_ref[...].astype(BF16)
        val = val_ref[...].astype(F32)
        gate = gate_ref[...].astype(F32)
        sg = _sigmoid(gate)
        sil = gate * sg
        dwd_ref[...] += _dot_tn((sil * val).astype(BF16), dxb).reshape(2, f4, D)
        da = _dot_nt(dxb, wd_ref[...].reshape(tf, D))

        def conv_bwd(dhc, h0, cw, c_scr, dh_ref, dcw_ref, dcb_ref):
            ext = jnp.concatenate([dhc, c_scr[...]], axis=0)
            n1 = _shift_up(ext, 1)
            n2 = _shift_up(ext, 2)
            dh_ref[...] = (cw[2:3, :] * dhc + cw[1:2, :] * n1 + cw[0:1, :] * n2).astype(BF16)
            c_scr[...] = dhc[:SUBLANES, :]
            dcw_ref[0:1, :] += jnp.sum(n2 * h0, axis=0, keepdims=True)
            dcw_ref[1:2, :] += jnp.sum(n1 * h0, axis=0, keepdims=True)
            dcw_ref[2:3, :] += jnp.sum(dhc * h0, axis=0, keepdims=True)
            dcb_ref[...] += jnp.sum(dhc, axis=0, keepdims=True)

        conv_bwd(da * sil, hv_ref[...].astype(F32), cwv_ref[...], cv_scr, dhv_ref, dcwv_ref, dcbv_ref)
        conv_bwd(da * val * _dsilu(gate, sg), hg_ref[...].astype(F32), cwg_ref[...], cg_scr, dhg_ref, dcwg_ref,
                 dcbg_ref)

        @pl.when(i == nt - 1)
        def _():
            dwd16_ref[...] = dwd_ref[...].astype(BF16)

    rev = lambda i: nt - 1 - i
    wd_spec = pl.BlockSpec((2, f4, D), lambda f, i: (f, 0, 0))
    return _call(
        body, name=name, grid=(nf, nt), plan=plan,
        in_specs=[pl.BlockSpec((T, D), lambda f, i: (rev(i), 0)),
                  pl.BlockSpec((T, tf), lambda f, i: (rev(i), f)),
                  pl.BlockSpec((T, tf), lambda f, i: (rev(i), f)),
                  pl.BlockSpec((T, tf), lambda f, i: (rev(i), f)),
                  pl.BlockSpec((T, tf), lambda f, i: (rev(i), f)),
                  pl.BlockSpec((3, tf), lambda f, i: (0, f)),
                  pl.BlockSpec((3, tf), lambda f, i: (0, nf + f)),
                  wd_spec],
        out_specs=[pl.BlockSpec((T, tf), lambda f, i: (rev(i), f)),
                   pl.BlockSpec((T, tf), lambda f, i: (rev(i), f)),
                   wd_spec, wd_spec,
                   pl.BlockSpec((3, tf), lambda f, i: (0, f)),
                   pl.BlockSpec((3, tf), lambda f, i: (0, f)),
                   pl.BlockSpec((1, tf), lambda f, i: (0, f)),
                   pl.BlockSpec((1, tf), lambda f, i: (0, f))],
        out_shape=[jax.ShapeDtypeStruct((S, F), BF16), jax.ShapeDtypeStruct((S, F), BF16),
                   jax.ShapeDtypeStruct((N_CHIPS, f4, D), F32), jax.ShapeDtypeStruct((N_CHIPS, f4, D), BF16),
                   jax.ShapeDtypeStruct((3, F), F32), jax.ShapeDtypeStruct((3, F), F32),
                   jax.ShapeDtypeStruct((1, F), F32), jax.ShapeDtypeStruct((1, F), F32)],
        scratch=[pltpu.VMEM((SUBLANES, tf), F32), pltpu.VMEM((SUBLANES, tf), F32)],
        args=(dxo, hv, hg, val16, gate16, conv_w, conv_w, w_down4))


def _down_loss(a, w_down4, x, g, target, *, name):
    S, F = a.shape
    D = x.shape[1]
    T = _row_tile(S, 512)

    def body(a_ref, wd_ref, x_ref, g_ref, t_ref, loss_ref, dx_ref, dg_ref):
        @pl.when(pl.program_id(0) == 0)
        def _():
            loss_ref[...] = jnp.zeros_like(loss_ref)
            dg_ref[...] = jnp.zeros_like(dg_ref)

        xf = x_ref[...] + _dot(a_ref[...], wd_ref[...].reshape(F, D))
        r = lax.rsqrt(jnp.mean(xf * xf, axis=-1, keepdims=True) + EPS)
        xhat = xf * r
        err = xhat * g_ref[...] - t_ref[...]
        loss_ref[...] += jnp.sum(err * err, axis=0, keepdims=True) * (0.5 / D)
        dy = err * (1.0 / D)
        dxhat = dy * g_ref[...]
        dx_ref[...] = r * (dxhat - xhat * jnp.mean(dxhat * xhat, axis=-1, keepdims=True))
        dg_ref[...] += jnp.sum(dy * xhat, axis=0, keepdims=True)

    row = lambda w: pl.BlockSpec((T, w), lambda i: (i, 0))
    vec = pl.BlockSpec((1, D), lambda i: (0, 0))
    return _call(
        body, name=name, grid=(S // T,),
        in_specs=[row(F), pl.BlockSpec(w_down4.shape, lambda i: (0, 0, 0), pipeline_mode=pl.Buffered(1)),
                  row(D), vec, row(D)],
        out_specs=[vec, row(D), vec],
        out_shape=[jax.ShapeDtypeStruct((1, D), F32), jax.ShapeDtypeStruct((S, D), F32),
                   jax.ShapeDtypeStruct((1, D), F32)],
        args=(a, w_down4, x, g, target))[0]


BIG = ("w_in", "w_pa", "w_pb", "w_o", "w_up", "w_down")
SMALL = ("norm1_g", "b_gate", "pool_w", "pool_scale", "lb_logits", "hgrn_norm_g", "norm2_g", "conv_b", "final_g")
WEIGHTS = ("norm1_g", "w_in", "b_gate", "pool_w", "pool_scale", "lb_logits", "hgrn_norm_g", "w_pa", "w_pb", "w_o",
           "norm2_g", "w_up", "conv_w", "conv_b", "w_down", "final_g")


def _lower_bounds(lb_logits):
    soft = jax.nn.softmax(lb_logits.astype(F32), axis=0)
    cum = jnp.cumsum(soft, axis=0)
    return cum - cum[0:1]


def _step(x, target, sm, wts, shards=None):
    L = sm["norm1_g"].shape[0]
    wts = dict(wts)
    dist = shards is not None
    lbs, lb_vjp = jax.vjp(_lower_bounds, sm["lb_logits"])
    row = lambda a: a.reshape(1, -1)
    conv_w = sm.get("conv_w")

    def gather(names_layers, with_conv=False):
        items = [(shards[n], "rows", l) for n, l in names_layers]
        if with_conv:
            items.append((shards["conv_w"], "layer", None))
        return _GatherPlan(items)

    def landed(names_layers, outs):
        for key, arr in zip(names_layers, outs):
            wts[key] = arr

    own = {"in_proj": ("w_up",)}
    first = {"hgrn_fwd": ("w_pa", "w_pb", "w_o"), "mix_fwd": ("w_down",)}
    ahead = {"up": ("w_in", "w_pa", "w_pb", "w_o"), "down": ("w_down",)}
    conv_rider = "mix_fwd"

    def riders(l, kernel):
        if not dist:
            return [], None
        keys = [(n, l) for n in own.get(kernel, ())]
        keys += [(n, l) for n in first.get(kernel, ())] if l == 0 else []
        keys += [(n, l + 1) for n in ahead.get(kernel, ())] if l + 1 < L else []
        with_conv = l == 0 and kernel == conv_rider
        return keys, (gather(keys, with_conv) if keys or with_conv else None)

    keys = [("w_in", 0)] if dist else []
    (xn1,), got = _rmsnorm(x, row(sm["norm1_g"][0]), name="norm_in", plan=gather(keys) if keys else None)
    landed(keys, got)

    saved = []
    for l in range(L):
        keys, plan = riders(l, "in_proj")
        (z,), got = _matmul(xn1, wts[("w_in", l)], name=f"in_proj_{l}", plan=plan)
        landed(keys, got)
        pm = _pool_fwd(z, sm["pool_w"][l], row(sm["pool_scale"][l]), name=f"pool_fwd_{l}")
        keys, plan = riders(l, "hgrn_fwd")
        (o_raw, of, states), got = _hgrn_fwd(z, row(lbs[l]), row(sm["hgrn_norm_g"][l]), name=f"hgrn_fwd_{l}",
                                             plan=plan)
        landed(keys, got)
        keys, plan = riders(l, "mix_fwd")
        (x_mid, ya, yb, xn2), got = _mix_fwd(x, pm, of, z, row(sm["b_gate"][l]), wts[("w_pa", l)], wts[("w_pb", l)],
                                             wts[("w_o", l)], row(sm["norm2_g"][l]), name=f"mix_fwd_{l}", plan=plan)
        landed(keys, got)
        if dist and l == 0:
            full = got[-1]
            conv_w = jnp.concatenate([full[:, k] for k in range(N_CHIPS)], axis=2)
        keys, plan = riders(l, "up")
        (hv, hg, val16, gate16, a16), got = _up_conv(xn2, wts[("w_up", l)], conv_w[l], row(sm["conv_b"][l]),
                                                     name=f"up_{l}", plan=plan)
        landed(keys, got)
        saved.append(dict(x=x, xn1=xn1, z=z, pm=pm, o_raw=o_raw, of=of, states=states,
                          x_mid=x_mid, ya=ya, yb=yb, xn2=xn2, hv=hv, hg=hg, val16=val16, gate16=gate16))
        if l + 1 < L:
            keys, plan = riders(l, "down")
            (x, xn1), got = _down(a16, wts[("w_down", l)], x_mid, row(sm["norm1_g"][l + 1]), name=f"down_{l}",
                                  plan=plan)
            landed(keys, got)
        else:
            loss_cols, dx, d_final_g = _down_loss(a16, wts[("w_down", l)], x_mid, row(sm["final_g"]), target,
                                                  name="down_loss")

    small = {k: [None] * L for k in ("norm1_g", "b_gate", "pool_w", "pool_scale", "hgrn_norm_g", "norm2_g",
                                     "conv_w", "conv_b")}
    big32, big16, recv = {}, {}, {}
    dlbs = [None] * L
    pending = []

    def scatter():
        if not (dist and pending):
            return [], None
        keys = list(pending)
        del pending[:]
        return keys, _ScatterPlan([big16[k] for k in keys])

    def sent(keys, outs):
        for key, arr in zip(keys, outs):
            recv[key] = arr

    def made(name, l, g32, g16):
        big32[(name, l)], big16[(name, l)] = g32, g16
        pending.append((name, l))

    for l in reversed(range(L)):
        s = saved[l]
        keys, plan = scatter()
        (dhv, dhg, d_wd, d_wd16, dcwv, dcwg, dcbv, dcbg), got = _ffn_down_bwd(
            dx, s["hv"], s["hg"], s["val16"], s["gate16"], conv_w[l], wts[("w_down", l)], name=f"down_bwd_{l}",
            plan=plan)
        sent(keys, got)
        made("w_down", l, d_wd, d_wd16)
        small["conv_w"][l] = jnp.concatenate([dcwv, dcwg], axis=1)
        small["conv_b"][l] = jnp.concatenate([dcbv, dcbg], axis=1)[0]
        keys, plan = scatter()
        (d_wu, d_wu16), got = _wgrad(s["xn2"], [dhv, dhg], name=f"up_wgrad_{l}", rows=2048, plan=plan)
        sent(keys, got)
        made("w_up", l, d_wu, d_wu16)
        (dxm, dg2), _ = _dgrad_norm([dhv, dhg], wts[("w_up", l)], s["x_mid"], row(sm["norm2_g"][l]), dx,
                                    name=f"up_dgrad_{l}")
        small["norm2_g"][l] = dg2[0]

        keys, plan = scatter()
        (dzg, dpm, dof, d_wo, d_wpa, d_wpb, dbg, d_wo16, d_wpa16, d_wpb16), got = _mix_bwd(
            dxm, s["ya"], s["yb"], s["z"], row(sm["b_gate"][l]), s["pm"], s["of"],
            wts[("w_pa", l)], wts[("w_pb", l)], wts[("w_o", l)], name=f"mix_bwd_{l}", plan=plan)
        sent(keys, got)
        made("w_o", l, d_wo, d_wo16)
        made("w_pa", l, d_wpa, d_wpa16)
        made("w_pb", l, d_wpb, d_wpb16)
        small["b_gate"][l] = dbg[0]

        du, dpw, dps = _pool_bwd(s["z"], dpm, sm["pool_w"][l], row(sm["pool_scale"][l]), name=f"pool_bwd_{l}")
        small["pool_w"][l], small["pool_scale"][l] = dpw, dps[0]

        keys, plan = scatter()
        (dzq, dzf, dzi, dzo, dlb, dng), got = _hgrn_bwd(s["z"], row(lbs[l]), row(sm["hgrn_norm_g"][l]), s["o_raw"],
                                                      s["states"], dof, name=f"hgrn_bwd_{l}", plan=plan)
        sent(keys, got)
        dlbs[l] = dlb[0]
        small["hgrn_norm_g"][l] = jnp.sum(dng.reshape(-1, LANES), axis=0)

        dz = [du, dzq, dzf, dzi, dzo, dzg]
        (d_wi, d_wi16), _ = _wgrad(s["xn1"], dz, name=f"in_wgrad_{l}", rows=1024)
        made("w_in", l, d_wi, d_wi16)
        keys, plan = scatter()
        (dx, dg1), got = _dgrad_norm(dz, wts[("w_in", l)], s["x"], row(sm["norm1_g"][l]), dxm,
                                     name=f"in_dgrad_{l}", plan=plan)
        sent(keys, got)
        small["norm1_g"][l] = dg1[0]

    out = {k: jnp.stack(v) for k, v in small.items()}
    out["lb_logits"] = lb_vjp(jnp.stack(dlbs))[0]
    out["final_g"] = d_final_g[0]
    return loss_cols, dx, out, big32, recv


def _elementwise_rows(R, n, n_arrays):
    if 2 * n_arrays * R * n * 4 <= VMEM_LIMIT // 4 or R % 8:
        return R
    block = VMEM_LIMIT // 2 // (2 * n_arrays)
    want = 8
    while want * 2 * n * 4 <= block:
        want *= 2
    return _row_tile(R, want)


def _sum_layers(own, got, chip, *, name):
    L = len(own)
    _, r, n = own[0].shape
    T = _elementwise_rows(r, n, 6)
    nt = r // T

    def body(chip_ref, *refs):
        o_ref = refs[-1]
        l = pl.program_id(0)
        for k in range(L):
            @pl.when(l == k)
            def _():
                own_ref, got_ref = refs[2 * k], refs[2 * k + 1]
                acc = own_ref[...]
                for j in range(3):
                    acc = acc + got_ref[j].astype(F32)
                o_ref[...] = acc

    in_specs = []
    for k in range(L):
        hold = 0 if k else nt - 1
        in_specs.append(pl.BlockSpec((None, T, n), lambda l, i, c, k=k, hold=hold: (c[0], jnp.where(l == k, i, hold), 0)))
        in_specs.append(pl.BlockSpec((3, T, n), lambda l, i, c, k=k, hold=hold: (0, jnp.where(l == k, i, hold), 0)))
    grid_spec = pltpu.PrefetchScalarGridSpec(
        num_scalar_prefetch=1, grid=(L, nt), in_specs=in_specs,
        out_specs=pl.BlockSpec((None, T, n), lambda l, i, c: (l, i, 0)))
    args = [a for pair in zip(own, got) for a in pair]
    return pl.pallas_call(
        body, name=name, grid_spec=grid_spec, out_shape=jax.ShapeDtypeStruct((L, r, n), F32),
        compiler_params=pltpu.CompilerParams(dimension_semantics=("arbitrary", "arbitrary"),
                                             vmem_limit_bytes=VMEM_LIMIT),
    )(chip, *args)


def _sum_stack(first, rest, *, name):
    R, n = first.shape
    K = rest.shape[0]
    T = _elementwise_rows(R, n, K + 2)

    def body(a_ref, r_ref, o_ref):
        acc = a_ref[...]
        for j in range(K):
            acc = acc + r_ref[j].astype(F32)
        o_ref[...] = acc

    return _call(
        body, name=name, grid=(R // T,), parallel=(0,),
        in_specs=[pl.BlockSpec((T, n), lambda i: (i, 0)), pl.BlockSpec((K, T, n), lambda i: (0, i, 0))],
        out_specs=[pl.BlockSpec((T, n), lambda i: (i, 0))],
        out_shape=[jax.ShapeDtypeStruct((R, n), F32)],
        args=(first, rest))[0][0]


def _adamw(w, m, v, g_parts, *, name):
    R, n = w.shape
    n_g = len(g_parts)
    T = _elementwise_rows(R, n, 7 + n_g)

    def body(*refs):
        w_ref, m_ref, v_ref = refs[:3]
        g_refs = refs[3:3 + n_g]
        go_ref, d_ref, mo_ref, vo_ref = refs[3 + n_g:]
        g_ = g_refs[0][...]
        for r in g_refs[1:]:
            g_ = g_ + r[...]
        m_ = ADAM_B1 * m_ref[...] + (1.0 - ADAM_B1) * g_
        v_ = ADAM_B2 * v_ref[...] + (1.0 - ADAM_B2) * (g_ * g_)
        m_hat = m_ / (1.0 - ADAM_B1 ** ADAM_STEP)
        v_hat = v_ / (1.0 - ADAM_B2 ** ADAM_STEP)
        go_ref[...] = g_
        d_ref[...] = -ADAM_LR * (m_hat / (jnp.sqrt(v_hat) + ADAM_EPS) + ADAM_WD * w_ref[...])
        mo_ref[...] = m_
        vo_ref[...] = v_

    blk = pl.BlockSpec((T, n), lambda i: (i, 0))
    return _call(
        body, name=name, grid=(R // T,), parallel=(0,),
        in_specs=[blk] * (3 + n_g), out_specs=[blk] * 4,
        out_shape=[jax.ShapeDtypeStruct((R, n), F32)] * 4,
        args=(w, m, v, *g_parts))[0]


PACK_ALIGN = 8 * LANES


def _pack(pieces):
    flat = []
    for a in pieces:
        a = a.reshape(-1)
        pad = (-a.shape[0]) % PACK_ALIGN
        flat.append(jnp.pad(a, (0, pad)) if pad else a)
    return jnp.concatenate(flat).reshape(-1, LANES)


def _unpack(buf, shapes):
    flat = buf.reshape(-1)
    out, off = [], 0
    for shp in shapes:
        size = 1
        for s in shp:
            size *= s
        out.append(flat[off:off + size].reshape(shp))
        off += size + (-size) % PACK_ALIGN
    return out


def kernel(x, norm1_g, w_in, b_gate, pool_w, pool_scale, lb_logits, hgrn_norm_g, w_pa, w_pb, w_o, norm2_g, w_up, conv_w, conv_b, w_down, final_g, loss_target, m_norm1_g, m_w_in, m_b_gate, m_pool_w, m_pool_scale, m_lb_logits, m_hgrn_norm_g, m_w_pa, m_w_pb, m_w_o, m_norm2_g, m_w_up, m_conv_w, m_conv_b, m_w_down, m_final_g, v_norm1_g, v_w_in, v_b_gate, v_pool_w, v_pool_scale, v_lb_logits, v_hgrn_norm_g, v_w_pa, v_w_pb, v_w_o, v_norm2_g, v_w_up, v_conv_w, v_conv_b, v_w_down, v_final_g):
    env = dict(locals())
    w = {n: env[n] for n in WEIGHTS}
    m = {n: env["m_" + n] for n in WEIGHTS}
    v = {n: env["v_" + n] for n in WEIGHTS}
    my_chip = 2 * lax.axis_index("x") + lax.axis_index("y")
    L = w_in.shape[0]

    shards = {n: w[n].astype(BF16) for n in BIG}
    shards["conv_w"] = w["conv_w"]
    sm = {n: w[n] for n in SMALL}
    loss_cols, grad_x, g_small, big32, recv = _step(x[0], loss_target[0], sm, {}, shards)

    chip = my_chip.reshape(1).astype(jnp.int32)
    sums = [_sum_layers([big32[(n, l)] for l in range(L)], [recv[(n, l)] for l in range(L)], chip,
                        name="chip_sum_" + n) for n in BIG]
    small_names = list(SMALL)
    small_pieces = [g_small[n] for n in small_names] + [g_small["conv_w"], loss_cols]
    small_shapes = [a.shape for a in small_pieces]
    packed = _pack(small_pieces)
    Rs = packed.shape[0]
    swapped = _run_plan(_Together([_SiblingPlan(sums), _EveryonePlan(packed)]), name="tail_exchange")
    theirs, everyone = swapped[:-1], swapped[-1].reshape(8, Rs, LANES)
    g, delta, new_m, new_v = {}, {}, {}, {}
    for n, mine, other in zip(BIG, sums, theirs):
        shp = w[n].shape
        two_d = lambda a: a.reshape(-1, shp[-1])
        outs = _adamw(two_d(w[n]), two_d(m[n]), two_d(v[n]), [two_d(mine), two_d(other)], name="adamw_" + n)
        g[n], delta[n], new_m[n], new_v[n] = [a.reshape(shp) for a in outs]

    summed = _unpack(_sum_stack(everyone[0], everyone[1:], name="small_sum"), small_shapes)
    loss = jnp.sum(summed[-1])
    cshard = w["conv_w"].shape[2]
    gs = dict(zip(small_names, summed[:len(small_names)]))
    g_cw = lax.dynamic_slice_in_dim(summed[-2], my_chip * cshard, cshard, axis=2)

    sm_out = _adamw(_pack([w[n] for n in small_names]), _pack([m[n] for n in small_names]),
                    _pack([v[n] for n in small_names]), [_pack([gs[n] for n in small_names])], name="adamw_small")
    shapes = [w[n].shape for n in small_names]
    for n, g_, d_, m_, v_ in zip(small_names, *[_unpack(a, shapes) for a in sm_out]):
        g[n], delta[n], new_m[n], new_v[n] = g_, d_, m_, v_
    shp = w["conv_w"].shape
    two_d = lambda a: a.reshape(-1, shp[-1])
    outs = _adamw(two_d(w["conv_w"]), two_d(m["conv_w"]), two_d(v["conv_w"]), [two_d(g_cw)], name="adamw_conv_w")
    g["conv_w"], delta["conv_w"], new_m["conv_w"], new_v["conv_w"] = [a.reshape(shp) for a in outs]

    return (loss, grad_x[None], *[g[n] for n in WEIGHTS], *[delta[n] for n in WEIGHTS],
            *[new_m[n] for n in WEIGHTS], *[new_v[n] for n in WEIGHTS])
```

```python
import jax
import jax.numpy as jnp
from jax import lax
from jax.experimental import pallas as pl
from jax.experimental.pallas import tpu as pltpu

F32 = jnp.float32
BF16 = jnp.bfloat16

EPS = 1e-6
CHUNK = 64
SUB = 32
LANES = 128
SUBLANES = 8
POOL_WINDOWS = (2, 4, 8, 16)
HALO_POOL = 16
EXP_CLAMP = 80.0

ADAM_LR = 0.001
ADAM_B1 = 0.9
ADAM_B2 = 0.999
ADAM_EPS = 1e-08
ADAM_WD = 0.01
ADAM_STEP = 10

VMEM_LIMIT = 56 * 1024 * 1024
MESH_ID = pl.DeviceIdType.MESH
N_CHIPS = 4
ANY = pl.BlockSpec(memory_space=pl.ANY)


def _dot(a, b):
    return jnp.dot(a, b, preferred_element_type=F32)


def _dot_nt(a, b):
    return lax.dot_general(a, b, (((1,), (1,)), ((), ())), preferred_element_type=F32)


def _dot_tn(a, b):
    return lax.dot_general(a, b, (((0,), (0,)), ((), ())), preferred_element_type=F32)


def _sigmoid(x):
    return jax.nn.sigmoid(x)


def _dsilu(x, s):
    return s * (1.0 + x * (1.0 - s))


def _row_tile(rows, want):
    t = min(rows, want)
    while rows % t:
        t //= 2
    return t


def _place():
    x, y, c = lax.axis_index("x"), lax.axis_index("y"), lax.axis_index("c")
    chips = [(1 - x, y), (x, 1 - y), (1 - x, 1 - y)]
    return x, y, c, chips


def _remote(src, dst, sems, k, to):
    return pltpu.make_async_remote_copy(src_ref=src, dst_ref=dst, send_sem=sems[0].at[k], recv_sem=sems[1].at[k],
                                        device_id=to, device_id_type=MESH_ID)


class _GatherPlan:
    def __init__(self, items):
        self.items = items
        self.inputs = [a for a, _, _ in items]
        self.out_shapes = []
        for a, kind, _ in items:
            shp = (N_CHIPS,) + a.shape[1:] if kind == "rows" else (a.shape[0], N_CHIPS) + a.shape[1:]
            self.out_shapes.append(jax.ShapeDtypeStruct(shp, a.dtype))
        n = len(items)
        self.scratch = [pltpu.SemaphoreType.DMA((6 * n,)), pltpu.SemaphoreType.DMA((6 * n,)),
                        pltpu.SemaphoreType.DMA((2 * n,))]

    def _views(self, i, src, dst):
        _, kind, l = self.items[i]
        if kind == "rows":
            half = src.shape[1] // 2
            part = lambda core: src.at[l, pl.ds(core * half, half), :]
            land = lambda chip, core: dst.at[chip, pl.ds(core * half, half), :]
        else:
            part = lambda core: src.at[core]
            land = lambda chip, core: dst.at[core, chip]
        return part, land

    def start(self, srcs, dsts, sems):
        x, y, c, chips = _place()
        me = 2 * x + y
        for i, (src, dst) in enumerate(zip(srcs, dsts)):
            part, land = self._views(i, src, dst)
            for core in range(2):
                pltpu.make_async_copy(part(core), land(me, core), sems[2].at[2 * i + core]).start()
            for j, (px, py) in enumerate(chips):
                _remote(part(c), land(me, c), sems, 6 * i + j, (px, py, c)).start()

    def finish(self, srcs, dsts, sems):
        x, y, c, chips = _place()
        me = 2 * x + y
        sibling = (x, y, 1 - c)
        for i, (src, dst) in enumerate(zip(srcs, dsts)):
            part, land = self._views(i, src, dst)
            for j, (px, py) in enumerate(chips):
                got = land(2 * px + py, c)
                _remote(got, got, sems, 6 * i + j, (px, py, c)).wait_recv()
                _remote(got, got, sems, 6 * i + 3 + j, sibling).start()
        for i, (src, dst) in enumerate(zip(srcs, dsts)):
            part, land = self._views(i, src, dst)
            for j, (px, py) in enumerate(chips):
                got = land(2 * px + py, 1 - c)
                _remote(got, got, sems, 6 * i + 3 + j, sibling).wait_recv()
            for j, (px, py) in enumerate(chips):
                _remote(part(c), land(me, c), sems, 6 * i + j, (px, py, c)).wait_send()
                mine = land(2 * px + py, c)
                _remote(mine, mine, sems, 6 * i + 3 + j, sibling).wait_send()
            for core in range(2):
                pltpu.make_async_copy(part(core), land(me, core), sems[2].at[2 * i + core]).wait()


class _ScatterPlan:
    def __init__(self, items):
        self.inputs = list(items)
        self.out_shapes = [jax.ShapeDtypeStruct((3,) + a.shape[1:], a.dtype) for a in items]
        n = len(items)
        self.scratch = [pltpu.SemaphoreType.DMA((3 * n,)), pltpu.SemaphoreType.DMA((3 * n,))]

    def _copies(self, srcs, dsts, sems):
        x, y, c, chips = _place()
        return [_remote(src.at[2 * px + py], dst.at[j], sems, 3 * i + j, (px, py, c))
                for i, (src, dst) in enumerate(zip(srcs, dsts)) for j, (px, py) in enumerate(chips)]

    def start(self, srcs, dsts, sems):
        for cp in self._copies(srcs, dsts, sems):
            cp.start()

    def finish(self, srcs, dsts, sems):
        copies = self._copies(srcs, dsts, sems)
        for cp in copies:
            cp.wait_recv()
        for cp in copies:
            cp.wait_send()


class _SiblingPlan:
    def __init__(self, items):
        self.inputs = list(items)
        self.out_shapes = [jax.ShapeDtypeStruct(a.shape, a.dtype) for a in items]
        n = len(items)
        self.scratch = [pltpu.SemaphoreType.DMA((n,)), pltpu.SemaphoreType.DMA((n,))]

    def _copies(self, srcs, dsts, sems):
        x, y, c, _ = _place()
        return [_remote(src, dst, sems, i, (x, y, 1 - c)) for i, (src, dst) in enumerate(zip(srcs, dsts))]

    def start(self, srcs, dsts, sems):
        for cp in self._copies(srcs, dsts, sems):
            cp.start()

    def finish(self, srcs, dsts, sems):
        copies = self._copies(srcs, dsts, sems)
        for cp in copies:
            cp.wait_recv()
        for cp in copies:
            cp.wait_send()


class _EveryonePlan:
    def __init__(self, block):
        self.inputs = [block]
        self.m = block.shape[0]
        self.out_shapes = [jax.ShapeDtypeStruct((8 * self.m,) + block.shape[1:], block.dtype)]
        self.scratch = [pltpu.SemaphoreType.DMA((7,)), pltpu.SemaphoreType.DMA((7,)), pltpu.SemaphoreType.DMA((1,))]

    def _rows(self, dst, px, py, pc):
        return dst.at[pl.ds((4 * px + 2 * py + pc) * self.m, self.m), :]

    def start(self, srcs, dsts, sems):
        x, y, c, chips = _place()
        src, dst = srcs[0], dsts[0]
        pltpu.make_async_copy(src, self._rows(dst, x, y, c), sems[2].at[0]).start()
        _remote(src, self._rows(dst, x, y, c), sems, 0, (x, y, 1 - c)).start()
        for j, (px, py) in enumerate(chips):
            _remote(src, self._rows(dst, x, y, c), sems, 1 + j, (px, py, c)).start()

    def finish(self, srcs, dsts, sems):
        x, y, c, chips = _place()
        src, dst = srcs[0], dsts[0]
        sibling = (x, y, 1 - c)
        for j, (px, py) in enumerate(chips):
            got = self._rows(dst, px, py, c)
            _remote(got, got, sems, 1 + j, (px, py, c)).wait_recv()
            _remote(got, got, sems, 4 + j, sibling).start()
        sib = self._rows(dst, x, y, 1 - c)
        _remote(sib, sib, sems, 0, sibling).wait_recv()
        for j, (px, py) in enumerate(chips):
            got = self._rows(dst, px, py, 1 - c)
            _remote(got, got, sems, 4 + j, sibling).wait_recv()
        mine = self._rows(dst, x, y, c)
        _remote(src, mine, sems, 0, sibling).wait_send()
        for j, (px, py) in enumerate(chips):
            _remote(src, mine, sems, 1 + j, (px, py, c)).wait_send()
            got = self._rows(dst, px, py, c)
            _remote(got, got, sems, 4 + j, sibling).wait_send()
        pltpu.make_async_copy(src, mine, sems[2].at[0]).wait()


def _call(body, *, name, grid, in_specs, out_specs, out_shape, args, scratch=(), parallel=(), plan=None):
    n_in, n_out, n_scr = len(in_specs), len(out_shape), len(scratch)
    sem = tuple("parallel" if (a in parallel and plan is None) else "arbitrary" for a in range(len(grid)))
    params = pltpu.CompilerParams(dimension_semantics=sem, vmem_limit_bytes=VMEM_LIMIT)
    if plan is None:
        outs = pl.pallas_call(body, name=name, grid=grid, in_specs=in_specs, out_specs=out_specs,
                              out_shape=out_shape, scratch_shapes=list(scratch), compiler_params=params)(*args)
        return list(outs), []
    p_in, p_out, p_scr = len(plan.inputs), len(plan.out_shapes), len(plan.scratch)

    def wrapped(*refs):
        ins, refs = refs[:n_in], refs[n_in:]
        p_ins, refs = refs[:p_in], refs[p_in:]
        outs, refs = refs[:n_out], refs[n_out:]
        p_outs, refs = refs[:p_out], refs[p_out:]
        scr, p_sems = refs[:n_scr], refs[n_scr:]
        ids = [pl.program_id(a) for a in range(len(grid))]
        first = _all([i == 0 for i in ids])
        last = _all([i == n - 1 for i, n in zip(ids, grid)])

        @pl.when(first)
        def _():
            plan.start(p_ins, p_outs, p_sems)

        body(*ins, *outs, *scr)

        @pl.when(last)
        def _():
            plan.finish(p_ins, p_outs, p_sems)

    outs = pl.pallas_call(
        wrapped, name=name, grid=grid,
        in_specs=list(in_specs) + [ANY] * p_in, out_specs=list(out_specs) + [ANY] * p_out,
        out_shape=list(out_shape) + list(plan.out_shapes),
        scratch_shapes=list(scratch) + list(plan.scratch), compiler_params=params,
    )(*args, *plan.inputs)
    return list(outs[:n_out]), list(outs[n_out:])


def _all(conds):
    out = conds[0]
    for c in conds[1:]:
        out = out & c
    return out


class _Together:
    def __init__(self, plans):
        self.plans = plans
        self.inputs = [a for p in plans for a in p.inputs]
        self.out_shapes = [s for p in plans for s in p.out_shapes]
        self.scratch = [s for p in plans for s in p.scratch]

    def _split(self, refs, count):
        out, at = [], 0
        for p in self.plans:
            out.append(refs[at:at + count(p)])
            at += count(p)
        return out

    def _parts(self, srcs, dsts, sems):
        return zip(self.plans, self._split(srcs, lambda p: len(p.inputs)),
                   self._split(dsts, lambda p: len(p.out_shapes)), self._split(sems, lambda p: len(p.scratch)))

    def start(self, srcs, dsts, sems):
        for p, s, d, m in self._parts(srcs, dsts, sems):
            p.start(s, d, m)

    def finish(self, srcs, dsts, sems):
        for p, s, d, m in self._parts(srcs, dsts, sems):
            p.finish(s, d, m)


def _run_plan(plan, *, name):
    p_in, p_out = len(plan.inputs), len(plan.out_shapes)

    def body(*refs):
        srcs, dsts, sems = refs[:p_in], refs[p_in:p_in + p_out], refs[p_in + p_out:]
        plan.start(srcs, dsts, sems)
        plan.finish(srcs, dsts, sems)

    return list(pl.pallas_call(body, name=name, in_specs=[ANY] * p_in, out_specs=[ANY] * p_out,
                               out_shape=list(plan.out_shapes), scratch_shapes=list(plan.scratch))(*plan.inputs))


def _rms(xf, g):
    r = lax.rsqrt(jnp.mean(xf * xf, axis=-1, keepdims=True) + EPS)
    return (xf * r * g).astype(BF16)


def _rmsnorm(x, g, *, name, plan=None):
    S, D = x.shape
    tm = _row_tile(S, 1024)

    def body(x_ref, g_ref, xn_ref):
        xn_ref[...] = _rms(x_ref[...], g_ref[...])

    return _call(
        body, name=name, grid=(S // tm,), parallel=(0,), plan=plan,
        in_specs=[pl.BlockSpec((tm, D), lambda i: (i, 0)), pl.BlockSpec((1, D), lambda i: (0, 0))],
        out_specs=[pl.BlockSpec((tm, D), lambda i: (i, 0))],
        out_shape=[jax.ShapeDtypeStruct((S, D), BF16)],
        args=(x, g))


def _matmul(xn, w4, *, name, out_dtype=F32, plan=None):
    S, D = xn.shape
    n4 = w4.shape[2]
    tm = _row_tile(S, 2048)

    def body(xn_ref, w_ref, o_ref):
        o_ref[...] = _dot(xn_ref[...], w_ref[...]).astype(out_dtype)

    return _call(
        body, name=name, grid=(S // tm, N_CHIPS), parallel=(0,), plan=plan,
        in_specs=[pl.BlockSpec((tm, D), lambda i, j: (i, 0)),
                  pl.BlockSpec((None, D, n4), lambda i, j: (j, 0, 0))],
        out_specs=[pl.BlockSpec((tm, n4), lambda i, j: (i, j))],
        out_shape=[jax.ShapeDtypeStruct((S, N_CHIPS * n4), out_dtype)],
        args=(xn, w4))


def _segments(widths, n4):
    per_chip = [[] for _ in range(N_CHIPS)]
    c0 = 0
    for p, w in enumerate(widths):
        a = c0
        while a < c0 + w:
            k = a // n4
            b = min(c0 + w, (k + 1) * n4)
            per_chip[k].append((p, (a - c0, b - c0), (a - k * n4, b - k * n4)))
            a = b
        c0 += w
    assert c0 == N_CHIPS * n4
    return per_chip


def _piece_specs(pieces, n4, tm):
    per_chip = _segments([p.shape[1] for p in pieces], n4)
    specs, local, start = [], [[] for _ in range(N_CHIPS)], 0
    for p, arr in enumerate(pieces):
        chips = [k for k in range(N_CHIPS) if any(seg[0] == p for seg in per_chip[k])]
        lo, hi = chips[0], chips[-1]
        tiled = arr.shape[1] % n4 == 0 and start % n4 == 0
        start += arr.shape[1]
        if tiled:
            imap = lambda k, i, lo=lo, hi=hi: (jnp.where((k >= lo) & (k <= hi), i, 0), jnp.clip(k - lo, 0, hi - lo))
            specs.append(pl.BlockSpec((tm, n4), imap))
        else:
            imap = lambda k, i, lo=lo, hi=hi: (jnp.where((k >= lo) & (k <= hi), i, 0), 0)
            specs.append(pl.BlockSpec((tm, arr.shape[1]), imap))
        for k in chips:
            for q, (pa, pb), cols in per_chip[k]:
                if q == p:
                    local[k].append((p, (0, n4) if tiled else (pa, pb), cols))
    return specs, local


def _dgrad_norm(dys, w4, x, g, dres, *, name, plan=None):
    S, D = x.shape
    n4 = w4.shape[2]
    tm = _row_tile(S, 512)
    per_chip = _segments([a.shape[1] for a in dys], n4)
    n_p = len(dys)

    def body(*refs):
        dy_refs = refs[:n_p]
        w_ref, x_ref, g_ref, dres_ref, dx_ref, dg_ref = refs[n_p:]

        @pl.when(pl.program_id(0) == 0)
        def _():
            dg_ref[...] = jnp.zeros_like(dg_ref)

        dxn = None
        for k in range(N_CHIPS):
            for p, (pa, pb), (ca, cb) in per_chip[k]:
                part = _dot_nt(dy_refs[p][:, pa:pb], w_ref[k, :, ca:cb])
                dxn = part if dxn is None else dxn + part
        xf = x_ref[...]
        r = lax.rsqrt(jnp.mean(xf * xf, axis=-1, keepdims=True) + EPS)
        xhat = xf * r
        dxhat = dxn * g_ref[...]
        dx_ref[...] = dres_ref[...] + r * (dxhat - xhat * jnp.mean(dxhat * xhat, axis=-1, keepdims=True))
        dg_ref[...] += jnp.sum(dxn * xhat, axis=0, keepdims=True)

    row = lambda w: pl.BlockSpec((tm, w), lambda i: (i, 0))
    return _call(
        body, name=name, grid=(S // tm,), plan=plan,
        in_specs=[row(a.shape[1]) for a in dys]
        + [pl.BlockSpec(w4.shape, lambda i: (0, 0, 0), pipeline_mode=pl.Buffered(1)),
           row(D), pl.BlockSpec((1, D), lambda i: (0, 0)), row(D)],
        out_specs=[row(D), pl.BlockSpec((1, D), lambda i: (0, 0))],
        out_shape=[jax.ShapeDtypeStruct((S, D), F32), jax.ShapeDtypeStruct((1, D), F32)],
        args=(*dys, w4, x, g, dres))


def _wgrad(a, dys, *, name, rows, plan=None):
    S, K = a.shape
    n4 = sum(p.shape[1] for p in dys) // N_CHIPS
    tm = _row_tile(S, rows)
    ns = S // tm
    specs, local = _piece_specs(dys, n4, tm)
    n_p = len(dys)

    def body(*refs):
        a_ref = refs[0]
        dy_refs = refs[1:1 + n_p]
        o_ref, o16_ref = refs[1 + n_p:]
        n, s = pl.program_id(0), pl.program_id(1)

        @pl.when(s == 0)
        def _():
            o_ref[...] = jnp.zeros_like(o_ref)

        for k in range(N_CHIPS):
            @pl.when(n == k)
            def _():
                av = a_ref[...]
                for p, (pa, pb), (ca, cb) in local[k]:
                    o_ref[:, ca:cb] += _dot_tn(av, dy_refs[p][:, pa:pb])

        @pl.when(s == ns - 1)
        def _():
            o16_ref[...] = o_ref[...].astype(BF16)

    out = pl.BlockSpec((None, K, n4), lambda n, s: (n, 0, 0))
    return _call(
        body, name=name, grid=(N_CHIPS, ns), parallel=(0,), plan=plan,
        in_specs=[pl.BlockSpec((tm, K), lambda n, s: (s, 0))] + specs,
        out_specs=[out, out],
        out_shape=[jax.ShapeDtypeStruct((N_CHIPS, K, n4), F32), jax.ShapeDtypeStruct((N_CHIPS, K, n4), BF16)],
        args=(a, *dys))


def _tiles(x):
    return x.reshape(x.shape[0] // SUBLANES, SUBLANES, x.shape[1])


def _shift_down(xp, s):
    n = xp.shape[0] - SUBLANES
    if s == SUBLANES:
        return xp[:n, :]
    t = _tiles(xp)
    rot = pltpu.roll(t, s, 1)
    sub = lax.broadcasted_iota(jnp.int32, t.shape, 1)[1:]
    return jnp.where(sub >= s, rot[1:], rot[:-1]).reshape(n, xp.shape[1])


def _shift_up(xn, s):
    n = xn.shape[0] - SUBLANES
    if s == SUBLANES:
        return xn[SUBLANES:, :]
    t = _tiles(xn)
    rot = pltpu.roll(t, SUBLANES - s, 1)
    sub = lax.broadcasted_iota(jnp.int32, t.shape, 1)[1:]
    return jnp.where(sub < SUBLANES - s, rot[:-1], rot[1:]).reshape(n, xn.shape[1])


def _pooled(u, halo, first_tile, row0):
    T = u.shape[0]
    halo = jnp.where(first_tile, 0.0, halo)
    pad = jnp.zeros((SUBLANES, u.shape[1]), F32)
    up = jnp.concatenate([pad, halo, u], axis=0)
    t1 = (row0 + lax.broadcasted_iota(jnp.int32, (T, 1), 0) + 1).astype(F32)
    outs = []
    for gi, w in enumerate(POOL_WINDOWS):
        s = up[:, gi * LANES:(gi + 1) * LANES]
        k = 1
        while k < w:
            if k < SUBLANES:
                s = jnp.concatenate([s[:SUBLANES, :], s[SUBLANES:, :] + _shift_down(s, k)], axis=0)
            else:
                s = s[SUBLANES:, :] + _shift_down(s, k)
            k *= 2
        s = s[-T:, :]
        inv = 1.0 / jnp.minimum(t1, float(w))
        outs.append(s * inv - u[:, gi * LANES:(gi + 1) * LANES])
    return outs


def _pool_fwd(z, pool_w, pool_scale, *, name):
    S = z.shape[0]
    P = pool_scale.shape[1]
    T = _row_tile(S, 512)
    hb = T // HALO_POOL

    def body(u_ref, halo_ref, pw_ref, ps_ref, o_ref):
        i = pl.program_id(0)
        pooled = _pooled(u_ref[...], halo_ref[...], i == 0, i * T)
        for gi in range(len(POOL_WINDOWS)):
            mixed = _dot(pooled[gi].astype(BF16), pw_ref[gi].astype(BF16))
            cols = slice(gi * LANES, (gi + 1) * LANES)
            o_ref[:, cols] = (mixed * ps_ref[:, cols]).astype(BF16)

    return _call(
        body, name=name, grid=(S // T,), parallel=(0,),
        in_specs=[pl.BlockSpec((T, P), lambda i: (i, 0)),
                  pl.BlockSpec((HALO_POOL, P), lambda i: (jnp.maximum(i * hb - 1, 0), 0)),
                  pl.BlockSpec(pool_w.shape, lambda i: (0, 0, 0)),
                  pl.BlockSpec((1, P), lambda i: (0, 0))],
        out_specs=[pl.BlockSpec((T, P), lambda i: (i, 0))],
        out_shape=[jax.ShapeDtypeStruct((S, P), BF16)],
        args=(z, z, pool_w, pool_scale))[0][0]


def _pool_bwd(z, dpm, pool_w, pool_scale, *, name):
    S, P = dpm.shape
    T = _row_tile(S, 512)
    hb = T // HALO_POOL
    nt = S // T

    def body(u_ref, halo_ref, d_ref, dnext_ref, pw_ref, ps_ref, du_ref, dpw_ref, dps_ref):
        i = pl.program_id(0)

        @pl.when(i == 0)
        def _():
            dpw_ref[...] = jnp.zeros_like(dpw_ref)
            dps_ref[...] = jnp.zeros_like(dps_ref)

        pooled = _pooled(u_ref[...], halo_ref[...], i == 0, i * T)
        dnext = jnp.where(i == nt - 1, 0.0, dnext_ref[...])
        pad = jnp.zeros((SUBLANES, P), F32)
        dext = jnp.concatenate([d_ref[...], dnext, pad], axis=0)
        t1 = (i * T + lax.broadcasted_iota(jnp.int32, (T + HALO_POOL + SUBLANES, 1), 0) + 1).astype(F32)
        for gi, w in enumerate(POOL_WINDOWS):
            cols = slice(gi * LANES, (gi + 1) * LANES)
            pw = pw_ref[gi].astype(BF16)
            pg = pooled[gi].astype(BF16)
            mixed = _dot(pg, pw)
            dps_ref[:, cols] += jnp.sum(d_ref[:, cols] * mixed, axis=0, keepdims=True)
            dmixed = (dext[:, cols] * ps_ref[:, cols]).astype(BF16)
            dpw_ref[gi] += _dot_tn(pg, dmixed[:T, :])
            dpooled = _dot_nt(dmixed, pw)
            e = dpooled * (1.0 / jnp.minimum(t1, float(w)))
            k = 1
            while k < w:
                if k < SUBLANES:
                    e = jnp.concatenate([e[:-SUBLANES, :] + _shift_up(e, k), e[-SUBLANES:, :]], axis=0)
                else:
                    e = e[:-SUBLANES, :] + _shift_up(e, k)
                k *= 2
            du_ref[:, cols] = (e[:T, :] - dpooled[:T, :]).astype(BF16)

    return _call(
        body, name=name, grid=(nt,),
        in_specs=[pl.BlockSpec((T, P), lambda i: (i, 0)),
                  pl.BlockSpec((HALO_POOL, P), lambda i: (jnp.maximum(i * hb - 1, 0), 0)),
                  pl.BlockSpec((T, P), lambda i: (i, 0)),
                  pl.BlockSpec((HALO_POOL, P), lambda i: (jnp.minimum((i + 1) * hb, S // HALO_POOL - 1), 0)),
                  pl.BlockSpec(pool_w.shape, lambda i: (0, 0, 0)),
                  pl.BlockSpec((1, P), lambda i: (0, 0))],
        out_specs=[pl.BlockSpec((T, P), lambda i: (i, 0)),
                   pl.BlockSpec(pool_w.shape, lambda i: (0, 0, 0)),
                   pl.BlockSpec((1, P), lambda i: (0, 0))],
        out_shape=[jax.ShapeDtypeStruct((S, P), BF16),
                   jax.ShapeDtypeStruct(pool_w.shape, F32),
                   jax.ShapeDtypeStruct((1, P), F32)],
        args=(z, z, dpm, dpm, pool_w, pool_scale))[0]


def _cumsum_rows(x):
    n = x.shape[0]
    row = lax.broadcasted_iota(jnp.int32, x.shape, 0)
    s = 1
    while s < n:
        x = x + jnp.where(row >= s, pltpu.roll(x, s, 0), 0.0)
        s *= 2
    return x


def _rev_cumsum_rows(x):
    n = x.shape[0]
    row = lax.broadcasted_iota(jnp.int32, x.shape, 0)
    s = 1
    while s < n:
        x = x + jnp.where(row < n - s, pltpu.roll(x, n - s, 0), 0.0)
        s *= 2
    return x


def _chunk_prep(zq, zf, lb, b_ref):
    n_sub = CHUNK // SUB
    sq = _sigmoid(zq)
    q = zq * sq
    sf = _sigmoid(zf)
    f = lb + (1.0 - lb) * sf
    k = 1.0 - f
    b = _cumsum_rows(jnp.log(f))
    b_ref[...] = b
    shape = (SUB, b.shape[1])
    ends = [jnp.broadcast_to(b_ref[pl.ds(SUB * j + SUB - 1, 1), :], shape) for j in range(n_sub)]
    mids = [jnp.broadcast_to(b_ref[pl.ds(SUB * j + SUB // 2 - 1, 1), :], shape) for j in range(n_sub)]
    own = [b[SUB * j:SUB * (j + 1), :] for j in range(n_sub)]
    m0 = jnp.concatenate(mids, axis=0)
    e1 = jnp.concatenate(ends, axis=0)
    eq = [jnp.exp(jnp.minimum(b - m0, EXP_CLAMP))]
    for d in range(1, n_sub):
        rd = jnp.concatenate([own[j] if j < d else ends[j - d] for j in range(n_sub)], axis=0)
        eq.append(jnp.exp(b - rd))
    ek0 = jnp.exp(jnp.minimum(m0 - b, EXP_CLAMP))
    ek1 = jnp.exp(e1 - b)
    b_last = b_ref[pl.ds(CHUNK - 1, 1), :]
    return dict(q=q, k=k, f=f, sq=sq, sf=sf, b=b, eq=eq, ek0=ek0, ek1=ek1,
                eb=jnp.exp(b), ekl=jnp.exp(b_last - b), el=jnp.exp(b_last))


def _chunk_masks():
    ti = lax.broadcasted_iota(jnp.int32, (CHUNK, CHUNK), 0)
    si = lax.broadcasted_iota(jnp.int32, (CHUNK, CHUNK), 1)
    shift = SUB.bit_length() - 1
    dsub = jnp.right_shift(ti, shift) - jnp.right_shift(si, shift)
    masks = [(dsub == 0) & (si <= ti)]
    masks += [dsub == d for d in range(1, CHUNK // SUB)]
    return masks


def _chunk_attn(p, masks):
    qd = [(p["q"] * e).astype(BF16) for e in p["eq"]]
    k0 = (p["k"] * p["ek0"]).astype(BF16)
    k1 = (p["k"] * p["ek1"]).astype(BF16)
    a = jnp.where(masks[0], _dot_nt(qd[0], k0), 0.0)
    for d in range(1, len(masks)):
        a = jnp.where(masks[d], _dot_nt(qd[d], k1), a)
    return a, qd, k0, k1


def _hgrn_fwd(z, lb, norm_g, *, name, plan=None):
    S = z.shape[0]
    HW = lb.shape[1]
    NH = HW // LANES
    T = _row_tile(S, 512)
    nc = T // CHUNK

    def body(zq_ref, zf_ref, zi_ref, zo_ref, lb_ref, ng_ref, o_ref, of_ref, st_ref, s_scr, b_scr):
        @pl.when(pl.program_id(0) == 0)
        def _():
            s_scr[...] = jnp.zeros_like(s_scr)

        ng = ng_ref[...]
        masks = _chunk_masks()

        def chunk(c, carry):
            rows = pl.ds(pl.multiple_of(c * CHUNK, CHUNK), CHUNK)
            for h in range(NH):
                cols = slice(h * LANES, (h + 1) * LANES)
                p = _chunk_prep(zq_ref[rows, cols], zf_ref[rows, cols], lb_ref[:, cols], b_scr.at[h])
                v = zi_ref[rows, cols].astype(BF16)
                zo = zo_ref[rows, cols]
                st = s_scr[h]
                st_ref[c, h] = st
                a, _, _, _ = _chunk_attn(p, masks)
                o = _dot(a.astype(BF16), v) + _dot_nt((p["q"] * p["eb"]).astype(BF16), st.astype(BF16))
                s_scr[h] = st * p["el"] + _dot_tn(v, (p["k"] * p["ekl"]).astype(BF16))
                o_ref[rows, cols] = o
                r = lax.rsqrt(jnp.mean(o * o, axis=-1, keepdims=True) + EPS)
                of_ref[rows, cols] = (o * r * ng * (zo * _sigmoid(zo))).astype(BF16)
            return carry

        lax.fori_loop(0, nc, chunk, 0, unroll=8)

    part = lambda k: pl.BlockSpec((T, HW), lambda i, k=k: (i, k))
    return _call(
        body, name=name, grid=(S // T,), plan=plan,
        in_specs=[part(1), part(2), part(3), part(4),
                  pl.BlockSpec((1, HW), lambda i: (0, 0)), pl.BlockSpec((1, LANES), lambda i: (0, 0))],
        out_specs=[pl.BlockSpec((T, HW), lambda i: (i, 0)), pl.BlockSpec((T, HW), lambda i: (i, 0)),
                   pl.BlockSpec((nc, NH, LANES, LANES), lambda i: (i, 0, 0, 0))],
        out_shape=[jax.ShapeDtypeStruct((S, HW), F32), jax.ShapeDtypeStruct((S, HW), BF16),
                   jax.ShapeDtypeStruct((S // CHUNK, NH, LANES, LANES), F32)],
        scratch=[pltpu.VMEM((NH, LANES, LANES), F32), pltpu.VMEM((NH, CHUNK, LANES), F32)],
        args=(z, z, z, z, lb, norm_g))


def _hgrn_bwd(z, lb, norm_g, o_raw, states, dof, *, name, plan=None):
    S = z.shape[0]
    HW = lb.shape[1]
    NH = HW // LANES
    T = _row_tile(S, 512)
    nc = T // CHUNK
    nt = S // T

    def body(zq_ref, zf_ref, zi_ref, zo_ref, lb_ref, ng_ref, o_ref, st_ref, dof_ref,
             dzq_ref, dzf_ref, dzi_ref, dzo_ref, dlb_ref, dng_ref, ds_scr, b_scr):
        @pl.when(pl.program_id(0) == 0)
        def _():
            ds_scr[...] = jnp.zeros_like(ds_scr)
            dlb_ref[...] = jnp.zeros_like(dlb_ref)
            dng_ref[...] = jnp.zeros_like(dng_ref)

        ng = ng_ref[...]
        masks = _chunk_masks()
        last_row = lax.broadcasted_iota(jnp.int32, (CHUNK, 1), 0) == CHUNK - 1

        def chunk(cr, carry):
            c = nc - 1 - cr
            rows = pl.ds(pl.multiple_of(c * CHUNK, CHUNK), CHUNK)
            for h in range(NH):
                cols = slice(h * LANES, (h + 1) * LANES)
                lbv = lb_ref[:, cols]
                zq, zf, zo = zq_ref[rows, cols], zf_ref[rows, cols], zo_ref[rows, cols]
                o = o_ref[rows, cols]
                dof_c = dof_ref[rows, cols]
                st = st_ref[c, h]
                dst = ds_scr[h]

                so = _sigmoid(zo)
                r = lax.rsqrt(jnp.mean(o * o, axis=-1, keepdims=True) + EPS)
                ohat = o * r
                d_on = dof_c * (zo * so)
                dzo_ref[rows, cols] = (dof_c * ohat * ng * _dsilu(zo, so)).astype(BF16)
                dng_ref[:, cols] += jnp.sum(d_on * ohat, axis=0, keepdims=True)
                dohat = d_on * ng
                do = (r * (dohat - ohat * jnp.mean(dohat * ohat, axis=-1, keepdims=True))).astype(BF16)

                p = _chunk_prep(zq, zf, lbv, b_scr.at[h])
                q, k = p["q"], p["k"]
                v = zi_ref[rows, cols].astype(BF16)
                a, qd, k0, k1 = _chunk_attn(p, masks)
                ktl = (k * p["ekl"]).astype(BF16)
                dstb = dst.astype(BF16)

                da = _dot_nt(do, v)
                dzi_ref[rows, cols] = (_dot_tn(a.astype(BF16), do) + _dot_nt(ktl, dstb)).astype(BF16)

                da0 = jnp.where(masks[0], da, 0.0).astype(BF16)
                rq = _dot(da0, k0)
                rk0 = _dot_tn(da0, qd[0])
                dq = rq * p["eq"][0]
                db = qd[0].astype(F32) * rq - k0.astype(F32) * rk0
                rk1 = jnp.zeros_like(rk0)
                for d in range(1, len(masks)):
                    dad = jnp.where(masks[d], da, 0.0).astype(BF16)
                    rq = _dot(dad, k1)
                    dq = dq + rq * p["eq"][d]
                    db = db + qd[d].astype(F32) * rq
                    rk1 = rk1 + _dot_tn(dad, qd[d])
                dk = rk0 * p["ek0"] + rk1 * p["ek1"]
                db = db - k1.astype(F32) * rk1
                qe = (q * p["eb"]).astype(BF16)
                rq = _dot(do, st.astype(BF16))
                dq = dq + rq * p["eb"]
                db = db + qe.astype(F32) * rq
                rk = _dot(v, dstb)
                dk = dk + rk * p["ekl"]
                db = db - ktl.astype(F32) * rk

                st_new = st * p["el"] + _dot_tn(v, ktl)
                db = db + jnp.where(last_row, jnp.sum(dstb.astype(F32) * st_new, axis=0, keepdims=True), 0.0)
                dg = _rev_cumsum_rows(db)
                ds_scr[h] = dst * p["el"] + _dot_tn(do, qe)

                dzq_ref[rows, cols] = (dq * _dsilu(zq, p["sq"])).astype(BF16)
                df = dg / p["f"] - dk
                sf = p["sf"]
                dzf_ref[rows, cols] = (df * (1.0 - lbv) * sf * (1.0 - sf)).astype(BF16)
                dlb_ref[:, cols] += jnp.sum(df * (1.0 - sf), axis=0, keepdims=True)
            return carry

        lax.fori_loop(0, nc, chunk, 0, unroll=4)

    rev = lambda i: nt - 1 - i
    part = lambda k: pl.BlockSpec((T, HW), lambda i, k=k: (rev(i), k))
    blk = pl.BlockSpec((T, HW), lambda i: (rev(i), 0))
    vec = pl.BlockSpec((1, HW), lambda i: (0, 0))
    return _call(
        body, name=name, grid=(nt,), plan=plan,
        in_specs=[part(1), part(2), part(3), part(4), vec, pl.BlockSpec((1, LANES), lambda i: (0, 0)),
                  blk, pl.BlockSpec((nc, NH, LANES, LANES), lambda i: (rev(i), 0, 0, 0)), blk],
        out_specs=[blk, blk, blk, blk, vec, vec],
        out_shape=[jax.ShapeDtypeStruct((S, HW), BF16)] * 4 + [jax.ShapeDtypeStruct((1, HW), F32)] * 2,
        scratch=[pltpu.VMEM((NH, LANES, LANES), F32), pltpu.VMEM((NH, CHUNK, LANES), F32)],
        args=(z, z, z, z, lb, norm_g, o_raw, states, dof))


def _gate_specs(T, D):
    half = D // 2
    first = (5 * half) // half
    return [pl.BlockSpec((T, half), lambda i, k=k: (i, first + k)) for k in range(4)]


def _gates(zg_refs, bg_ref, D):
    half = D // 2
    za = jnp.concatenate([zg_refs[0][...], zg_refs[1][...]], axis=1) + bg_ref[:, :D]
    zb = jnp.concatenate([zg_refs[2][...], zg_refs[3][...]], axis=1) + bg_ref[:, D:]
    return _sigmoid(za), _sigmoid(zb)


def _mix_fwd(x, pm, of, z, b_gate, w_pa4, w_pb4, w_o4, g_next, *, name, plan=None):
    S, D = x.shape
    P = pm.shape[1]
    T = _row_tile(S, 512)

    def body(x_ref, pm_ref, of_ref, g0, g1, g2, g3, bg_ref, wpa_ref, wpb_ref, wo_ref, gn_ref,
             xo_ref, ya_ref, yb_ref, xn_ref):
        pmv, ofv = pm_ref[...], of_ref[...]
        ya = jnp.concatenate([_dot(pmv, wpa_ref[k]) for k in range(N_CHIPS)], axis=1)
        yb = jnp.concatenate([_dot(ofv, wpb_ref[k]) for k in range(N_CHIPS)], axis=1)
        ga, gb = _gates((g0, g1, g2, g3), bg_ref, D)
        merged = (ga * ya + gb * yb).astype(BF16)
        x_mid = x_ref[...] + _dot(merged, wo_ref[...].reshape(D, D))
        xo_ref[...] = x_mid
        xn_ref[...] = _rms(x_mid, gn_ref[...])
        ya_ref[...] = ya.astype(BF16)
        yb_ref[...] = yb.astype(BF16)

    row = lambda w: pl.BlockSpec((T, w), lambda i: (i, 0))
    full = lambda a: pl.BlockSpec(a.shape, lambda i: (0,) * a.ndim)
    return _call(
        body, name=name, grid=(S // T,), parallel=(0,), plan=plan,
        in_specs=[row(D), row(P), row(P)] + _gate_specs(T, D) + [full(b_gate), full(w_pa4), full(w_pb4), full(w_o4),
                                                                  full(g_next)],
        out_specs=[row(D), row(D), row(D), row(D)],
        out_shape=[jax.ShapeDtypeStruct((S, D), F32), jax.ShapeDtypeStruct((S, D), BF16),
                   jax.ShapeDtypeStruct((S, D), BF16), jax.ShapeDtypeStruct((S, D), BF16)],
        args=(x, pm, of, z, z, z, z, b_gate, w_pa4, w_pb4, w_o4, g_next))


def _mix_bwd(dxm, ya, yb, z, b_gate, pm, of, w_pa4, w_pb4, w_o4, *, name, plan=None):
    S, D = dxm.shape
    P = pm.shape[1]
    q4 = D // N_CHIPS
    T = _row_tile(S, 512)
    nt = S // T

    def body(dx_ref, ya_ref, yb_ref, g0, g1, g2, g3, bg_ref, pm_ref, of_ref, wpa_ref, wpb_ref, wo_ref,
             dzg_ref, dpm_ref, dof_ref, dwo_ref, dwpa_ref, dwpb_ref, dbg_ref, dwo16_ref, dwpa16_ref, dwpb16_ref):
        i = pl.program_id(0)

        @pl.when(i == 0)
        def _():
            dwo_ref[...] = jnp.zeros_like(dwo_ref)
            dwpa_ref[...] = jnp.zeros_like(dwpa_ref)
            dwpb_ref[...] = jnp.zeros_like(dwpb_ref)
            dbg_ref[...] = jnp.zeros_like(dbg_ref)

        dxb = dx_ref[...].astype(BF16)
        ya = ya_ref[...].astype(F32)
        yb = yb_ref[...].astype(F32)
        ga, gb = _gates((g0, g1, g2, g3), bg_ref, D)
        merged = (ga * ya + gb * yb).astype(BF16)
        dwo_ref[...] += _dot_tn(merged, dxb).reshape(N_CHIPS, q4, D)
        dm = _dot_nt(dxb, wo_ref[...].reshape(D, D))
        dza = dm * ya * ga * (1.0 - ga)
        dzb = dm * yb * gb * (1.0 - gb)
        dzg_ref[:, :D] = dza.astype(BF16)
        dzg_ref[:, D:] = dzb.astype(BF16)
        dbg_ref[:, :D] += jnp.sum(dza, axis=0, keepdims=True)
        dbg_ref[:, D:] += jnp.sum(dzb, axis=0, keepdims=True)
        dya = (dm * ga).astype(BF16)
        dyb = (dm * gb).astype(BF16)
        pmv, ofv = pm_ref[...], of_ref[...]
        dpm = jnp.zeros((T, P), F32)
        dof = jnp.zeros((T, P), F32)
        for k in range(N_CHIPS):
            cols = slice(k * q4, (k + 1) * q4)
            dwpa_ref[k] += _dot_tn(pmv, dya[:, cols])
            dwpb_ref[k] += _dot_tn(ofv, dyb[:, cols])
            dpm = dpm + _dot_nt(dya[:, cols], wpa_ref[k])
            dof = dof + _dot_nt(dyb[:, cols], wpb_ref[k])
        dpm_ref[...] = dpm
        dof_ref[...] = dof

        @pl.when(i == nt - 1)
        def _():
            dwo16_ref[...] = dwo_ref[...].astype(BF16)
            dwpa16_ref[...] = dwpa_ref[...].astype(BF16)
            dwpb16_ref[...] = dwpb_ref[...].astype(BF16)

    row = lambda w: pl.BlockSpec((T, w), lambda i: (i, 0))
    full = lambda a: pl.BlockSpec(a.shape, lambda i: (0,) * a.ndim, pipeline_mode=pl.Buffered(1))
    like = lambda a, dt: jax.ShapeDtypeStruct(a.shape, dt)
    return _call(
        body, name=name, grid=(nt,), plan=plan,
        in_specs=[row(D), row(D), row(D)] + _gate_specs(T, D) + [full(b_gate), row(P), row(P),
                                                                  full(w_pa4), full(w_pb4), full(w_o4)],
        out_specs=[row(2 * D), row(P), row(P), full(w_o4), full(w_pa4), full(w_pb4), full(b_gate),
                   full(w_o4), full(w_pa4), full(w_pb4)],
        out_shape=[jax.ShapeDtypeStruct((S, 2 * D), BF16), jax.ShapeDtypeStruct((S, P), F32),
                   jax.ShapeDtypeStruct((S, P), F32), like(w_o4, F32), like(w_pa4, F32), like(w_pb4, F32),
                   like(b_gate, F32), like(w_o4, BF16), like(w_pa4, BF16), like(w_pb4, BF16)],
        args=(dxm, ya, yb, z, z, z, z, b_gate, pm, of, w_pa4, w_pb4, w_o4))


def _up_conv(xn, w_up4, conv_w, conv_b, *, name, plan=None):
    S, D = xn.shape
    f4 = w_up4.shape[2]
    nf = N_CHIPS // 2
    F = nf * f4
    T = _row_tile(S, 512)

    def body(xn_ref, wv_ref, wg_ref, cwv_ref, cwg_ref, cbv_ref, cbg_ref,
             hv_ref, hg_ref, val_ref, gate_ref, a_ref, pv_scr, pg_scr):
        i = pl.program_id(1)
        xv = xn_ref[...]

        def side(w_ref, cw_ref, cb_ref, h_ref, p_scr):
            h16 = _dot(xv, w_ref[...]).astype(BF16)
            h_ref[...] = h16
            h = h16.astype(F32)
            hp = jnp.concatenate([jnp.where(i == 0, 0.0, p_scr[...]), h], axis=0)
            p_scr[...] = h[-SUBLANES:, :]
            cw = cw_ref[...]
            return cw[0:1, :] * _shift_down(hp, 2) + cw[1:2, :] * _shift_down(hp, 1) + cw[2:3, :] * h + cb_ref[...]

        val = side(wv_ref, cwv_ref, cbv_ref, hv_ref, pv_scr)
        gate = side(wg_ref, cwg_ref, cbg_ref, hg_ref, pg_scr)
        val_ref[...] = val.astype(BF16)
        gate_ref[...] = gate.astype(BF16)
        a_ref[...] = (gate * _sigmoid(gate) * val).astype(BF16)

    out = pl.BlockSpec((T, f4), lambda f, i: (i, f))
    return _call(
        body, name=name, grid=(nf, S // T), plan=plan,
        in_specs=[pl.BlockSpec((T, D), lambda f, i: (i, 0)),
                  pl.BlockSpec((None, D, f4), lambda f, i: (f, 0, 0)),
                  pl.BlockSpec((None, D, f4), lambda f, i: (nf + f, 0, 0)),
                  pl.BlockSpec((3, f4), lambda f, i: (0, f)),
                  pl.BlockSpec((3, f4), lambda f, i: (0, nf + f)),
                  pl.BlockSpec((1, f4), lambda f, i: (0, f)),
                  pl.BlockSpec((1, f4), lambda f, i: (0, nf + f))],
        out_specs=[out] * 5,
        out_shape=[jax.ShapeDtypeStruct((S, F), BF16)] * 5,
        scratch=[pltpu.VMEM((SUBLANES, f4), F32), pltpu.VMEM((SUBLANES, f4), F32)],
        args=(xn, w_up4, w_up4, conv_w, conv_w, conv_b, conv_b))


def _down(a, w_down4, x, g_next, *, name, plan=None):
    S, F = a.shape
    D = x.shape[1]
    T = _row_tile(S, 1024)

    def body(a_ref, wd_ref, x_ref, gn_ref, o_ref, xn_ref):
        x_out = x_ref[...] + _dot(a_ref[...], wd_ref[...].reshape(F, D))
        o_ref[...] = x_out
        xn_ref[...] = _rms(x_out, gn_ref[...])

    row = lambda w: pl.BlockSpec((T, w), lambda i: (i, 0))
    return _call(
        body, name=name, grid=(S // T,), parallel=(0,), plan=plan,
        in_specs=[row(F), pl.BlockSpec(w_down4.shape, lambda i: (0, 0, 0), pipeline_mode=pl.Buffered(1)),
                  row(D), pl.BlockSpec((1, D), lambda i: (0, 0))],
        out_specs=[row(D), row(D)],
        out_shape=[jax.ShapeDtypeStruct((S, D), F32), jax.ShapeDtypeStruct((S, D), BF16)],
        args=(a, w_down4, x, g_next))


def _ffn_down_bwd(dxo, hv, hg, val16, gate16, conv_w, w_down4, *, name, plan=None):
    S = hv.shape[0]
    _, f4, D = w_down4.shape
    F = N_CHIPS * f4
    T = _row_tile(S, 512)
    nf = 2
    tf = 2 * f4
    nt = S // T

    def body(dx_ref, hv_ref, hg_ref, val_ref, gate_ref, cwv_ref, cwg_ref, wd_ref,
             dhv_ref, dhg_ref, dwd_ref, dwd16_ref, dcwv_ref, dcwg_ref, dcbv_ref, dcbg_ref, cv_scr, cg_scr):
        i = pl.program_id(1)

        @pl.when(i == 0)
        def _():
            cv_scr[...] = jnp.zeros_like(cv_scr)
            cg_scr[...] = jnp.zeros_like(cg_scr)
            dwd_ref[...] = jnp.zeros_like(dwd_ref)
            dcwv_ref[...] = jnp.zeros_like(dcwv_ref)
            dcwg_ref[...] = jnp.zeros_like(dcwg_ref)
            dcbv_ref[...] = jnp.zeros_like(dcbv_ref)
            dcbg_ref[...] = jnp.zeros_like(dcbg_ref)

        dxb = dx_ref[...].astype(BF16)
        val = val_ref[...].astype(F32)
        gate = gate_ref[...].astype(F32)
        sg = _sigmoid(gate)
        sil = gate * sg
        dwd_ref[...] += _dot_tn((sil * val).astype(BF16), dxb).reshape(2, f4, D)
        da = _dot_nt(dxb, wd_ref[...].reshape(tf, D))

        def conv_bwd(dhc, h0, cw, c_scr, dh_ref, dcw_ref, dcb_ref):
            ext = jnp.concatenate([dhc, c_scr[...]], axis=0)
            n1 = _shift_up(ext, 1)
            n2 = _shift_up(ext, 2)
            dh_ref[...] = (cw[2:3, :] * dhc + cw[1:2, :] * n1 + cw[0:1, :] * n2).astype(BF16)
            c_scr[...] = dhc[:SUBLANES, :]
            dcw_ref[0:1, :] += jnp.sum(n2 * h0, axis=0, keepdims=True)
            dcw_ref[1:2, :] += jnp.sum(n1 * h0, axis=0, keepdims=True)
            dcw_ref[2:3, :] += jnp.sum(dhc * h0, axis=0, keepdims=True)
            dcb_ref[...] += jnp.sum(dhc, axis=0, keepdims=True)

        conv_bwd(da * sil, hv_ref[...].astype(F32), cwv_ref[...], cv_scr, dhv_ref, dcwv_ref, dcbv_ref)
        conv_bwd(da * val * _dsilu(gate, sg), hg_ref[...].astype(F32), cwg_ref[...], cg_scr, dhg_ref, dcwg_ref,
                 dcbg_ref)

        @pl.when(i == nt - 1)
        def _():
            dwd16_ref[...] = dwd_ref[...].astype(BF16)

    rev = lambda i: nt - 1 - i
    wd_spec = pl.BlockSpec((2, f4, D), lambda f, i: (f, 0, 0))
    return _call(
        body, name=name, grid=(nf, nt), plan=plan,
        in_specs=[pl.BlockSpec((T, D), lambda f, i: (rev(i), 0)),
                  pl.BlockSpec((T, tf), lambda f, i: (rev(i), f)),
                  pl.BlockSpec((T, tf), lambda f, i: (rev(i), f)),
                  pl.BlockSpec((T, tf), lambda f, i: (rev(i), f)),
                  pl.BlockSpec((T, tf), lambda f, i: (rev(i), f)),
                  pl.BlockSpec((3, tf), lambda f, i: (0, f)),
                  pl.BlockSpec((3, tf), lambda f, i: (0, nf + f)),
                  wd_spec],
        out_specs=[pl.BlockSpec((T, tf), lambda f, i: (rev(i), f)),
                   pl.BlockSpec((T, tf), lambda f, i: (rev(i), f)),
                   wd_spec, wd_spec,
                   pl.BlockSpec((3, tf), lambda f, i: (0, f)),
                   pl.BlockSpec((3, tf), lambda f, i: (0, f)),
                   pl.BlockSpec((1, tf), lambda f, i: (0, f)),
                   pl.BlockSpec((1, tf), lambda f, i: (0, f))],
        out_shape=[jax.ShapeDtypeStruct((S, F), BF16), jax.ShapeDtypeStruct((S, F), BF16),
                   jax.ShapeDtypeStruct((N_CHIPS, f4, D), F32), jax.ShapeDtypeStruct((N_CHIPS, f4, D), BF16),
                   jax.ShapeDtypeStruct((3, F), F32), jax.ShapeDtypeStruct((3, F), F32),
                   jax.ShapeDtypeStruct((1, F), F32), jax.ShapeDtypeStruct((1, F), F32)],
        scratch=[pltpu.VMEM((SUBLANES, tf), F32), pltpu.VMEM((SUBLANES, tf), F32)],
        args=(dxo, hv, hg, val16, gate16, conv_w, conv_w, w_down4))


def _down_loss(a, w_down4, x, g, target, *, name):
    S, F = a.shape
    D = x.shape[1]
    T = _row_tile(S, 512)

    def body(a_ref, wd_ref, x_ref, g_ref, t_ref, loss_ref, dx_ref, dg_ref):
        @pl.when(pl.program_id(0) == 0)
        def _():
            loss_ref[...] = jnp.zeros_like(loss_ref)
            dg_ref[...] = jnp.zeros_like(dg_ref)

        xf = x_ref[...] + _dot(a_ref[...], wd_ref[...].reshape(F, D))
        r = lax.rsqrt(jnp.mean(xf * xf, axis=-1, keepdims=True) + EPS)
        xhat = xf * r
        err = xhat * g_ref[...] - t_ref[...]
        loss_ref[...] += jnp.sum(err * err, axis=0, keepdims=True) * (0.5 / D)
        dy = err * (1.0 / D)
        dxhat = dy * g_ref[...]
        dx_ref[...] = r * (dxhat - xhat * jnp.mean(dxhat * xhat, axis=-1, keepdims=True))
        dg_ref[...] += jnp.sum(dy * xhat, axis=0, keepdims=True)

    row = lambda w: pl.BlockSpec((T, w), lambda i: (i, 0))
    vec = pl.BlockSpec((1, D), lambda i: (0, 0))
    return _call(
        body, name=name, grid=(S // T,),
        in_specs=[row(F), pl.BlockSpec(w_down4.shape, lambda i: (0, 0, 0), pipeline_mode=pl.Buffered(1)),
                  row(D), vec, row(D)],
        out_specs=[vec, row(D), vec],
        out_shape=[jax.ShapeDtypeStruct((1, D), F32), jax.ShapeDtypeStruct((S, D), F32),
                   jax.ShapeDtypeStruct((1, D), F32)],
        args=(a, w_down4, x, g, target))[0]


BIG = ("w_in", "w_pa", "w_pb", "w_o", "w_up", "w_down")
SMALL = ("norm1_g", "b_gate", "pool_w", "pool_scale", "lb_logits", "hgrn_norm_g", "norm2_g", "conv_b", "final_g")
WEIGHTS = ("norm1_g", "w_in", "b_gate", "pool_w", "pool_scale", "lb_logits", "hgrn_norm_g", "w_pa", "w_pb", "w_o",
           "norm2_g", "w_up", "conv_w", "conv_b", "w_down", "final_g")


def _lower_bounds(lb_logits):
    soft = jax.nn.softmax(lb_logits.astype(F32), axis=0)
    cum = jnp.cumsum(soft, axis=0)
    return cum - cum[0:1]


def _step(x, target, sm, wts, shards=None):
    L = sm["norm1_g"].shape[0]
    wts = dict(wts)
    dist = shards is not None
    lbs, lb_vjp = jax.vjp(_lower_bounds, sm["lb_logits"])
    row = lambda a: a.reshape(1, -1)
    conv_w = sm.get("conv_w")

    def gather(names_layers, with_conv=False):
        items = [(shards[n], "rows", l) for n, l in names_layers]
        if with_conv:
            items.append((shards["conv_w"], "layer", None))
        return _GatherPlan(items)

    def landed(names_layers, outs):
        for key, arr in zip(names_layers, outs):
            wts[key] = arr

    own = {"in_proj": ("w_up",)}
    first = {"hgrn_fwd": ("w_pa", "w_pb", "w_o"), "mix_fwd": ("w_down",)}
    ahead = {"up": ("w_in", "w_pa", "w_pb", "w_o"), "down": ("w_down",)}
    conv_rider = "mix_fwd"

    def riders(l, kernel):
        if not dist:
            return [], None
        keys = [(n, l) for n in own.get(kernel, ())]
        keys += [(n, l) for n in first.get(kernel, ())] if l == 0 else []
        keys += [(n, l + 1) for n in ahead.get(kernel, ())] if l + 1 < L else []
        with_conv = l == 0 and kernel == conv_rider
        return keys, (gather(keys, with_conv) if keys or with_conv else None)

    keys = [("w_in", 0)] if dist else []
    (xn1,), got = _rmsnorm(x, row(sm["norm1_g"][0]), name="norm_in", plan=gather(keys) if keys else None)
    landed(keys, got)

    saved = []
    for l in range(L):
        keys, plan = riders(l, "in_proj")
        (z,), got = _matmul(xn1, wts[("w_in", l)], name=f"in_proj_{l}", plan=plan)
        landed(keys, got)
        pm = _pool_fwd(z, sm["pool_w"][l], row(sm["pool_scale"][l]), name=f"pool_fwd_{l}")
        keys, plan = riders(l, "hgrn_fwd")
        (o_raw, of, states), got = _hgrn_fwd(z, row(lbs[l]), row(sm["hgrn_norm_g"][l]), name=f"hgrn_fwd_{l}",
                                             plan=plan)
        landed(keys, got)
        keys, plan = riders(l, "mix_fwd")
        (x_mid, ya, yb, xn2), got = _mix_fwd(x, pm, of, z, row(sm["b_gate"][l]), wts[("w_pa", l)], wts[("w_pb", l)],
                                             wts[("w_o", l)], row(sm["norm2_g"][l]), name=f"mix_fwd_{l}", plan=plan)
        landed(keys, got)
        if dist and l == 0:
            full = got[-1]
            conv_w = jnp.concatenate([full[:, k] for k in range(N_CHIPS)], axis=2)
        keys, plan = riders(l, "up")
        (hv, hg, val16, gate16, a16), got = _up_conv(xn2, wts[("w_up", l)], conv_w[l], row(sm["conv_b"][l]),
                                                     name=f"up_{l}", plan=plan)
        landed(keys, got)
        saved.append(dict(x=x, xn1=xn1, z=z, pm=pm, o_raw=o_raw, of=of, states=states,
                          x_mid=x_mid, ya=ya, yb=yb, xn2=xn2, hv=hv, hg=hg, val16=val16, gate16=gate16))
        if l + 1 < L:
            keys, plan = riders(l, "down")
            (x, xn1), got = _down(a16, wts[("w_down", l)], x_mid, row(sm["norm1_g"][l + 1]), name=f"down_{l}",
                                  plan=plan)
            landed(keys, got)
        else:
            loss_cols, dx, d_final_g = _down_loss(a16, wts[("w_down", l)], x_mid, row(sm["final_g"]), target,
                                                  name="down_loss")

    small = {k: [None] * L for k in ("norm1_g", "b_gate", "pool_w", "pool_scale", "hgrn_norm_g", "norm2_g",
                                     "conv_w", "conv_b")}
    big32, big16, recv = {}, {}, {}
    dlbs = [None] * L
    pending = []

    def scatter():
        if not (dist and pending):
            return [], None
        keys = list(pending)
        del pending[:]
        return keys, _ScatterPlan([big16[k] for k in keys])

    def sent(keys, outs):
        for key, arr in zip(keys, outs):
            recv[key] = arr

    def made(name, l, g32, g16):
        big32[(name, l)], big16[(name, l)] = g32, g16
        pending.append((name, l))

    for l in reversed(range(L)):
        s = saved[l]
        keys, plan = scatter()
        (dhv, dhg, d_wd, d_wd16, dcwv, dcwg, dcbv, dcbg), got = _ffn_down_bwd(
            dx, s["hv"], s["hg"], s["val16"], s["gate16"], conv_w[l], wts[("w_down", l)], name=f"down_bwd_{l}",
            plan=plan)
        sent(keys, got)
        made("w_down", l, d_wd, d_wd16)
        small["conv_w"][l] = jnp.concatenate([dcwv, dcwg], axis=1)
        small["conv_b"][l] = jnp.concatenate([dcbv, dcbg], axis=1)[0]
        keys, plan = scatter()
        (d_wu, d_wu16), got = _wgrad(s["xn2"], [dhv, dhg], name=f"up_wgrad_{l}", rows=2048, plan=plan)
        sent(keys, got)
        made("w_up", l, d_wu, d_wu16)
        keys, plan = scatter()
        (dxm, dg2), got = _dgrad_norm([dhv, dhg], wts[("w_up", l)], s["x_mid"], row(sm["norm2_g"][l]), dx,
                                      name=f"up_dgrad_{l}", plan=plan)
        sent(keys, got)
        small["norm2_g"][l] = dg2[0]

        (dzg, dpm, dof, d_wo, d_wpa, d_wpb, dbg, d_wo16, d_wpa16, d_wpb16), _ = _mix_bwd(
            dxm, s["ya"], s["yb"], s["z"], row(sm["b_gate"][l]), s["pm"], s["of"],
            wts[("w_pa", l)], wts[("w_pb", l)], wts[("w_o", l)], name=f"mix_bwd_{l}")
        made("w_o", l, d_wo, d_wo16)
        made("w_pa", l, d_wpa, d_wpa16)
        made("w_pb", l, d_wpb, d_wpb16)
        small["b_gate"][l] = dbg[0]

        du, dpw, dps = _pool_bwd(s["z"], dpm, sm["pool_w"][l], row(sm["pool_scale"][l]), name=f"pool_bwd_{l}")
        small["pool_w"][l], small["pool_scale"][l] = dpw, dps[0]

        keys, plan = scatter()
        (dzq, dzf, dzi, dzo, dlb, dng), got = _hgrn_bwd(s["z"], row(lbs[l]), row(sm["hgrn_norm_g"][l]), s["o_raw"],
                                                      s["states"], dof, name=f"hgrn_bwd_{l}", plan=plan)
        sent(keys, got)
        dlbs[l] = dlb[0]
        small["hgrn_norm_g"][l] = jnp.sum(dng.reshape(-1, LANES), axis=0)

        dz = [du, dzq, dzf, dzi, dzo, dzg]
        (d_wi, d_wi16), _ = _wgrad(s["xn1"], dz, name=f"in_wgrad_{l}", rows=1024)
        made("w_in", l, d_wi, d_wi16)
        keys, plan = scatter()
        (dx, dg1), got = _dgrad_norm(dz, wts[("w_in", l)], s["x"], row(sm["norm1_g"][l]), dxm,
                                     name=f"in_dgrad_{l}", plan=plan)
        sent(keys, got)
        small["norm1_g"][l] = dg1[0]

    out = {k: jnp.stack(v) for k, v in small.items()}
    out["lb_logits"] = lb_vjp(jnp.stack(dlbs))[0]
    out["final_g"] = d_final_g[0]
    return loss_cols, dx, out, big32, recv


def _elementwise_rows(R, n, n_arrays):
    if 2 * n_arrays * R * n * 4 <= VMEM_LIMIT // 4 or R % 8:
        return R
    block = VMEM_LIMIT // 2 // (2 * n_arrays)
    want = 8
    while want * 2 * n * 4 <= block:
        want *= 2
    return _row_tile(R, want)


def _sum_layers(own, got, chip, *, name):
    L = len(own)
    _, r, n = own[0].shape
    T = _elementwise_rows(r, n, 6)
    nt = r // T

    def body(chip_ref, *refs):
        o_ref = refs[-1]
        l = pl.program_id(0)
        for k in range(L):
            @pl.when(l == k)
            def _():
                own_ref, got_ref = refs[2 * k], refs[2 * k + 1]
                acc = own_ref[...]
                for j in range(3):
                    acc = acc + got_ref[j].astype(F32)
                o_ref[...] = acc

    in_specs = []
    for k in range(L):
        hold = 0 if k else nt - 1
        in_specs.append(pl.BlockSpec((None, T, n), lambda l, i, c, k=k, hold=hold: (c[0], jnp.where(l == k, i, hold), 0)))
        in_specs.append(pl.BlockSpec((3, T, n), lambda l, i, c, k=k, hold=hold: (0, jnp.where(l == k, i, hold), 0)))
    grid_spec = pltpu.PrefetchScalarGridSpec(
        num_scalar_prefetch=1, grid=(L, nt), in_specs=in_specs,
        out_specs=pl.BlockSpec((None, T, n), lambda l, i, c: (l, i, 0)))
    args = [a for pair in zip(own, got) for a in pair]
    return pl.pallas_call(
        body, name=name, grid_spec=grid_spec, out_shape=jax.ShapeDtypeStruct((L, r, n), F32),
        compiler_params=pltpu.CompilerParams(dimension_semantics=("arbitrary", "arbitrary"),
                                             vmem_limit_bytes=VMEM_LIMIT),
    )(chip, *args)


def _sum_stack(first, rest, *, name):
    R, n = first.shape
    K = rest.shape[0]
    T = _elementwise_rows(R, n, K + 2)

    def body(a_ref, r_ref, o_ref):
        acc = a_ref[...]
        for j in range(K):
            acc = acc + r_ref[j].astype(F32)
        o_ref[...] = acc

    return _call(
        body, name=name, grid=(R // T,), parallel=(0,),
        in_specs=[pl.BlockSpec((T, n), lambda i: (i, 0)), pl.BlockSpec((K, T, n), lambda i: (0, i, 0))],
        out_specs=[pl.BlockSpec((T, n), lambda i: (i, 0))],
        out_shape=[jax.ShapeDtypeStruct((R, n), F32)],
        args=(first, rest))[0][0]


def _adamw(w, m, v, g_parts, *, name):
    R, n = w.shape
    n_g = len(g_parts)
    T = _elementwise_rows(R, n, 7 + n_g)

    def body(*refs):
        w_ref, m_ref, v_ref = refs[:3]
        g_refs = refs[3:3 + n_g]
        go_ref, d_ref, mo_ref, vo_ref = refs[3 + n_g:]
        g_ = g_refs[0][...]
        for r in g_refs[1:]:
            g_ = g_ + r[...]
        m_ = ADAM_B1 * m_ref[...] + (1.0 - ADAM_B1) * g_
        v_ = ADAM_B2 * v_ref[...] + (1.0 - ADAM_B2) * (g_ * g_)
        m_hat = m_ / (1.0 - ADAM_B1 ** ADAM_STEP)
        v_hat = v_ / (1.0 - ADAM_B2 ** ADAM_STEP)
        go_ref[...] = g_
        d_ref[...] = -ADAM_LR * (m_hat / (jnp.sqrt(v_hat) + ADAM_EPS) + ADAM_WD * w_ref[...])
        mo_ref[...] = m_
        vo_ref[...] = v_

    blk = pl.BlockSpec((T, n), lambda i: (i, 0))
    return _call(
        body, name=name, grid=(R // T,), parallel=(0,),
        in_specs=[blk] * (3 + n_g), out_specs=[blk] * 4,
        out_shape=[jax.ShapeDtypeStruct((R, n), F32)] * 4,
        args=(w, m, v, *g_parts))[0]


PACK_ALIGN = 8 * LANES


def _pack(pieces):
    flat = []
    for a in pieces:
        a = a.reshape(-1)
        pad = (-a.shape[0]) % PACK_ALIGN
        flat.append(jnp.pad(a, (0, pad)) if pad else a)
    return jnp.concatenate(flat).reshape(-1, LANES)


def _unpack(buf, shapes):
    flat = buf.reshape(-1)
    out, off = [], 0
    for shp in shapes:
        size = 1
        for s in shp:
            size *= s
        out.append(flat[off:off + size].reshape(shp))
        off += size + (-size) % PACK_ALIGN
    return out


def kernel(x, norm1_g, w_in, b_gate, pool_w, pool_scale, lb_logits, hgrn_norm_g, w_pa, w_pb, w_o, norm2_g, w_up, conv_w, conv_b, w_down, final_g, loss_target, m_norm1_g, m_w_in, m_b_gate, m_pool_w, m_pool_scale, m_lb_logits, m_hgrn_norm_g, m_w_pa, m_w_pb, m_w_o, m_norm2_g, m_w_up, m_conv_w, m_conv_b, m_w_down, m_final_g, v_norm1_g, v_w_in, v_b_gate, v_pool_w, v_pool_scale, v_lb_logits, v_hgrn_norm_g, v_w_pa, v_w_pb, v_w_o, v_norm2_g, v_w_up, v_conv_w, v_conv_b, v_w_down, v_final_g):
    env = dict(locals())
    w = {n: env[n] for n in WEIGHTS}
    m = {n: env["m_" + n] for n in WEIGHTS}
    v = {n: env["v_" + n] for n in WEIGHTS}
    my_chip = 2 * lax.axis_index("x") + lax.axis_index("y")
    L = w_in.shape[0]

    shards = {n: w[n].astype(BF16) for n in BIG}
    shards["conv_w"] = w["conv_w"]
    sm = {n: w[n] for n in SMALL}
    loss_cols, grad_x, g_small, big32, recv = _step(x[0], loss_target[0], sm, {}, shards)

    chip = my_chip.reshape(1).astype(jnp.int32)
    sums = [_sum_layers([big32[(n, l)] for l in range(L)], [recv[(n, l)] for l in range(L)], chip,
                        name="chip_sum_" + n) for n in BIG]
    small_names = list(SMALL)
    small_pieces = [g_small[n] for n in small_names] + [g_small["conv_w"], loss_cols]
    small_shapes = [a.shape for a in small_pieces]
    packed = _pack(small_pieces)
    Rs = packed.shape[0]
    swapped = _run_plan(_Together([_SiblingPlan(sums), _EveryonePlan(packed)]), name="tail_exchange")
    theirs, everyone = swapped[:-1], swapped[-1].reshape(8, Rs, LANES)
    g, delta, new_m, new_v = {}, {}, {}, {}
    for n, mine, other in zip(BIG, sums, theirs):
        shp = w[n].shape
        two_d = lambda a: a.reshape(-1, shp[-1])
        outs = _adamw(two_d(w[n]), two_d(m[n]), two_d(v[n]), [two_d(mine), two_d(other)], name="adamw_" + n)
        g[n], delta[n], new_m[n], new_v[n] = [a.reshape(shp) for a in outs]

    summed = _unpack(_sum_stack(everyone[0], everyone[1:], name="small_sum"), small_shapes)
    loss = jnp.sum(summed[-1])
    cshard = w["conv_w"].shape[2]
    gs = dict(zip(small_names, summed[:len(small_names)]))
    g_cw = lax.dynamic_slice_in_dim(summed[-2], my_chip * cshard, cshard, axis=2)

    sm_out = _adamw(_pack([w[n] for n in small_names]), _pack([m[n] for n in small_names]),
                    _pack([v[n] for n in small_names]), [_pack([gs[n] for n in small_names])], name="adamw_small")
    shapes = [w[n].shape for n in small_names]
    for n, g_, d_, m_, v_ in zip(small_names, *[_unpack(a, shapes) for a in sm_out]):
        g[n], delta[n], new_m[n], new_v[n] = g_, d_, m_, v_
    shp = w["conv_w"].shape
    two_d = lambda a: a.reshape(-1, shp[-1])
    outs = _adamw(two_d(w["conv_w"]), two_d(m["conv_w"]), two_d(v["conv_w"]), [two_d(g_cw)], name="adamw_conv_w")
    g["conv_w"], delta["conv_w"], new_m["conv_w"], new_v["conv_w"] = [a.reshape(shp) for a in outs]

    return (loss, grad_x[None], *[g[n] for n in WEIGHTS], *[delta[n] for n in WEIGHTS],
            *[new_m[n] for n in WEIGHTS], *[new_v[n] for n in WEIGHTS])
```

```python
import jax
import jax.numpy as jnp
from jax import lax
from jax.experimental import pallas as pl
from jax.experimental.pallas import tpu as pltpu

F32 = jnp.float32
BF16 = jnp.bfloat16

EPS = 1e-6
CHUNK = 64
SUB = 32
LANES = 128
SUBLANES = 8
POOL_WINDOWS = (2, 4, 8, 16)
HALO_POOL = 16
EXP_CLAMP = 80.0

ADAM_LR = 0.001
ADAM_B1 = 0.9
ADAM_B2 = 0.999
ADAM_EPS = 1e-08
ADAM_WD = 0.01
ADAM_STEP = 10

VMEM_LIMIT = 56 * 1024 * 1024
MESH_ID = pl.DeviceIdType.MESH
N_CHIPS = 4
ANY = pl.BlockSpec(memory_space=pl.ANY)


def _dot(a, b):
    return jnp.dot(a, b, preferred_element_type=F32)


def _dot_nt(a, b):
    return lax.dot_general(a, b, (((1,), (1,)), ((), ())), preferred_element_type=F32)


def _dot_tn(a, b):
    return lax.dot_general(a, b, (((0,), (0,)), ((), ())), preferred_element_type=F32)


def _sigmoid(x):
    return jax.nn.sigmoid(x)


def _dsilu(x, s):
    return s * (1.0 + x * (1.0 - s))


def _row_tile(rows, want):
    t = min(rows, want)
    while rows % t:
        t //= 2
    return t


def _place():
    x, y, c = lax.axis_index("x"), lax.axis_index("y"), lax.axis_index("c")
    chips = [(1 - x, y), (x, 1 - y), (1 - x, 1 - y)]
    return x, y, c, chips


def _remote(src, dst, sems, k, to):
    return pltpu.make_async_remote_copy(src_ref=src, dst_ref=dst, send_sem=sems[0].at[k], recv_sem=sems[1].at[k],
                                        device_id=to, device_id_type=MESH_ID)


class _GatherPlan:
    def __init__(self, items):
        self.items = items
        self.inputs = [a for a, _, _ in items]
        self.out_shapes = []
        for a, kind, _ in items:
            shp = (N_CHIPS,) + a.shape[1:] if kind == "rows" else (a.shape[0], N_CHIPS) + a.shape[1:]
            self.out_shapes.append(jax.ShapeDtypeStruct(shp, a.dtype))
        n = len(items)
        self.scratch = [pltpu.SemaphoreType.DMA((6 * n,)), pltpu.SemaphoreType.DMA((6 * n,)),
                        pltpu.SemaphoreType.DMA((2 * n,))]

    def _views(self, i, src, dst):
        _, kind, l = self.items[i]
        if kind == "rows":
            half = src.shape[1] // 2
            part = lambda core: src.at[l, pl.ds(core * half, half), :]
            land = lambda chip, core: dst.at[chip, pl.ds(core * half, half), :]
        else:
            part = lambda core: src.at[core]
            land = lambda chip, core: dst.at[core, chip]
        return part, land

    def start(self, srcs, dsts, sems):
        x, y, c, chips = _place()
        me = 2 * x + y
        for i, (src, dst) in enumerate(zip(srcs, dsts)):
            part, land = self._views(i, src, dst)
            for core in range(2):
                pltpu.make_async_copy(part(core), land(me, core), sems[2].at[2 * i + core]).start()
            for j, (px, py) in enumerate(chips):
                _remote(part(c), land(me, c), sems, 6 * i + j, (px, py, c)).start()

    def finish(self, srcs, dsts, sems):
        x, y, c, chips = _place()
        me = 2 * x + y
        sibling = (x, y, 1 - c)
        for i, (src, dst) in enumerate(zip(srcs, dsts)):
            part, land = self._views(i, src, dst)
            for j, (px, py) in enumerate(chips):
                got = land(2 * px + py, c)
                _remote(got, got, sems, 6 * i + j, (px, py, c)).wait_recv()
                _remote(got, got, sems, 6 * i + 3 + j, sibling).start()
        for i, (src, dst) in enumerate(zip(srcs, dsts)):
            part, land = self._views(i, src, dst)
            for j, (px, py) in enumerate(chips):
                got = land(2 * px + py, 1 - c)
                _remote(got, got, sems, 6 * i + 3 + j, sibling).wait_recv()
            for j, (px, py) in enumerate(chips):
                _remote(part(c), land(me, c), sems, 6 * i + j, (px, py, c)).wait_send()
                mine = land(2 * px + py, c)
                _remote(mine, mine, sems, 6 * i + 3 + j, sibling).wait_send()
            for core in range(2):
                pltpu.make_async_copy(part(core), land(me, core), sems[2].at[2 * i + core]).wait()


class _ScatterPlan:
    def __init__(self, items):
        self.inputs = list(items)
        self.out_shapes = [jax.ShapeDtypeStruct((3,) + a.shape[1:], a.dtype) for a in items]
        n = len(items)
        self.scratch = [pltpu.SemaphoreType.DMA((3 * n,)), pltpu.SemaphoreType.DMA((3 * n,))]

    def _copies(self, srcs, dsts, sems):
        x, y, c, chips = _place()
        return [_remote(src.at[2 * px + py], dst.at[j], sems, 3 * i + j, (px, py, c))
                for i, (src, dst) in enumerate(zip(srcs, dsts)) for j, (px, py) in enumerate(chips)]

    def start(self, srcs, dsts, sems):
        for cp in self._copies(srcs, dsts, sems):
            cp.start()

    def finish(self, srcs, dsts, sems):
        copies = self._copies(srcs, dsts, sems)
        for cp in copies:
            cp.wait_recv()
        for cp in copies:
            cp.wait_send()


class _SiblingPlan:
    def __init__(self, items):
        self.inputs = list(items)
        self.out_shapes = [jax.ShapeDtypeStruct(a.shape, a.dtype) for a in items]
        n = len(items)
        self.scratch = [pltpu.SemaphoreType.DMA((n,)), pltpu.SemaphoreType.DMA((n,))]

    def _copies(self, srcs, dsts, sems):
        x, y, c, _ = _place()
        return [_remote(src, dst, sems, i, (x, y, 1 - c)) for i, (src, dst) in enumerate(zip(srcs, dsts))]

    def start(self, srcs, dsts, sems):
        for cp in self._copies(srcs, dsts, sems):
            cp.start()

    def finish(self, srcs, dsts, sems):
        copies = self._copies(srcs, dsts, sems)
        for cp in copies:
            cp.wait_recv()
        for cp in copies:
            cp.wait_send()


class _EveryonePlan:
    def __init__(self, block):
        self.inputs = [block]
        self.m = block.shape[0]
        self.out_shapes = [jax.ShapeDtypeStruct((8 * self.m,) + block.shape[1:], block.dtype)]
        self.scratch = [pltpu.SemaphoreType.DMA((7,)), pltpu.SemaphoreType.DMA((7,)), pltpu.SemaphoreType.DMA((1,))]

    def _rows(self, dst, px, py, pc):
        return dst.at[pl.ds((4 * px + 2 * py + pc) * self.m, self.m), :]

    def start(self, srcs, dsts, sems):
        x, y, c, chips = _place()
        src, dst = srcs[0], dsts[0]
        pltpu.make_async_copy(src, self._rows(dst, x, y, c), sems[2].at[0]).start()
        _remote(src, self._rows(dst, x, y, c), sems, 0, (x, y, 1 - c)).start()
        for j, (px, py) in enumerate(chips):
            _remote(src, self._rows(dst, x, y, c), sems, 1 + j, (px, py, c)).start()

    def finish(self, srcs, dsts, sems):
        x, y, c, chips = _place()
        src, dst = srcs[0], dsts[0]
        sibling = (x, y, 1 - c)
        for j, (px, py) in enumerate(chips):
            got = self._rows(dst, px, py, c)
            _remote(got, got, sems, 1 + j, (px, py, c)).wait_recv()
            _remote(got, got, sems, 4 + j, sibling).start()
        sib = self._rows(dst, x, y, 1 - c)
        _remote(sib, sib, sems, 0, sibling).wait_recv()
        for j, (px, py) in enumerate(chips):
            got = self._rows(dst, px, py, 1 - c)
            _remote(got, got, sems, 4 + j, sibling).wait_recv()
        mine = self._rows(dst, x, y, c)
        _remote(src, mine, sems, 0, sibling).wait_send()
        for j, (px, py) in enumerate(chips):
            _remote(src, mine, sems, 1 + j, (px, py, c)).wait_send()
            got = self._rows(dst, px, py, c)
            _remote(got, got, sems, 4 + j, sibling).wait_send()
        pltpu.make_async_copy(src, mine, sems[2].at[0]).wait()


def _call(body, *, name, grid, in_specs, out_specs, out_shape, args, scratch=(), parallel=(), plan=None):
    n_in, n_out, n_scr = len(in_specs), len(out_shape), len(scratch)
    sem = tuple("parallel" if (a in parallel and plan is None) else "arbitrary" for a in range(len(grid)))
    params = pltpu.CompilerParams(dimension_semantics=sem, vmem_limit_bytes=VMEM_LIMIT)
    if plan is None:
        outs = pl.pallas_call(body, name=name, grid=grid, in_specs=in_specs, out_specs=out_specs,
                              out_shape=out_shape, scratch_shapes=list(scratch), compiler_params=params)(*args)
        return list(outs), []
    p_in, p_out, p_scr = len(plan.inputs), len(plan.out_shapes), len(plan.scratch)

    def wrapped(*refs):
        ins, refs = refs[:n_in], refs[n_in:]
        p_ins, refs = refs[:p_in], refs[p_in:]
        outs, refs = refs[:n_out], refs[n_out:]
        p_outs, refs = refs[:p_out], refs[p_out:]
        scr, p_sems = refs[:n_scr], refs[n_scr:]
        ids = [pl.program_id(a) for a in range(len(grid))]
        first = _all([i == 0 for i in ids])
        last = _all([i == n - 1 for i, n in zip(ids, grid)])

        @pl.when(first)
        def _():
            plan.start(p_ins, p_outs, p_sems)

        body(*ins, *outs, *scr)

        @pl.when(last)
        def _():
            plan.finish(p_ins, p_outs, p_sems)

    outs = pl.pallas_call(
        wrapped, name=name, grid=grid,
        in_specs=list(in_specs) + [ANY] * p_in, out_specs=list(out_specs) + [ANY] * p_out,
        out_shape=list(out_shape) + list(plan.out_shapes),
        scratch_shapes=list(scratch) + list(plan.scratch), compiler_params=params,
    )(*args, *plan.inputs)
    return list(outs[:n_out]), list(outs[n_out:])


def _all(conds):
    out = conds[0]
    for c in conds[1:]:
        out = out & c
    return out


class _Together:
    def __init__(self, plans):
        self.plans = plans
        self.inputs = [a for p in plans for a in p.inputs]
        self.out_shapes = [s for p in plans for s in p.out_shapes]
        self.scratch = [s for p in plans for s in p.scratch]

    def _split(self, refs, count):
        out, at = [], 0
        for p in self.plans:
            out.append(refs[at:at + count(p)])
            at += count(p)
        return out

    def _parts(self, srcs, dsts, sems):
        return zip(self.plans, self._split(srcs, lambda p: len(p.inputs)),
                   self._split(dsts, lambda p: len(p.out_shapes)), self._split(sems, lambda p: len(p.scratch)))

    def start(self, srcs, dsts, sems):
        for p, s, d, m in self._parts(srcs, dsts, sems):
            p.start(s, d, m)

    def finish(self, srcs, dsts, sems):
        for p, s, d, m in self._parts(srcs, dsts, sems):
            p.finish(s, d, m)


def _run_plan(plan, *, name):
    p_in, p_out = len(plan.inputs), len(plan.out_shapes)

    def body(*refs):
        srcs, dsts, sems = refs[:p_in], refs[p_in:p_in + p_out], refs[p_in + p_out:]
        plan.start(srcs, dsts, sems)
        plan.finish(srcs, dsts, sems)

    return list(pl.pallas_call(body, name=name, in_specs=[ANY] * p_in, out_specs=[ANY] * p_out,
                               out_shape=list(plan.out_shapes), scratch_shapes=list(plan.scratch))(*plan.inputs))


def _rms(xf, g):
    r = lax.rsqrt(jnp.mean(xf * xf, axis=-1, keepdims=True) + EPS)
    return (xf * r * g).astype(BF16)


def _rmsnorm(x, g, *, name, plan=None):
    S, D = x.shape
    tm = _row_tile(S, 1024)

    def body(x_ref, g_ref, xn_ref):
        xn_ref[...] = _rms(x_ref[...], g_ref[...])

    return _call(
        body, name=name, grid=(S // tm,), parallel=(0,), plan=plan,
        in_specs=[pl.BlockSpec((tm, D), lambda i: (i, 0)), pl.BlockSpec((1, D), lambda i: (0, 0))],
        out_specs=[pl.BlockSpec((tm, D), lambda i: (i, 0))],
        out_shape=[jax.ShapeDtypeStruct((S, D), BF16)],
        args=(x, g))


def _matmul(xn, w4, *, name, out_dtype=F32, plan=None):
    S, D = xn.shape
    n4 = w4.shape[2]
    tm = _row_tile(S, 2048)

    def body(xn_ref, w_ref, o_ref):
        o_ref[...] = _dot(xn_ref[...], w_ref[...]).astype(out_dtype)

    return _call(
        body, name=name, grid=(S // tm, N_CHIPS), parallel=(0,), plan=plan,
        in_specs=[pl.BlockSpec((tm, D), lambda i, j: (i, 0)),
                  pl.BlockSpec((None, D, n4), lambda i, j: (j, 0, 0))],
        out_specs=[pl.BlockSpec((tm, n4), lambda i, j: (i, j))],
        out_shape=[jax.ShapeDtypeStruct((S, N_CHIPS * n4), out_dtype)],
        args=(xn, w4))


def _segments(widths, n4):
    per_chip = [[] for _ in range(N_CHIPS)]
    c0 = 0
    for p, w in enumerate(widths):
        a = c0
        while a < c0 + w:
            k = a // n4
            b = min(c0 + w, (k + 1) * n4)
            per_chip[k].append((p, (a - c0, b - c0), (a - k * n4, b - k * n4)))
            a = b
        c0 += w
    assert c0 == N_CHIPS * n4
    return per_chip


def _piece_specs(pieces, n4, tm):
    per_chip = _segments([p.shape[1] for p in pieces], n4)
    specs, local, start = [], [[] for _ in range(N_CHIPS)], 0
    for p, arr in enumerate(pieces):
        chips = [k for k in range(N_CHIPS) if any(seg[0] == p for seg in per_chip[k])]
        lo, hi = chips[0], chips[-1]
        tiled = arr.shape[1] % n4 == 0 and start % n4 == 0
        start += arr.shape[1]
        if tiled:
            imap = lambda k, i, lo=lo, hi=hi: (jnp.where((k >= lo) & (k <= hi), i, 0), jnp.clip(k - lo, 0, hi - lo))
            specs.append(pl.BlockSpec((tm, n4), imap))
        else:
            imap = lambda k, i, lo=lo, hi=hi: (jnp.where((k >= lo) & (k <= hi), i, 0), 0)
            specs.append(pl.BlockSpec((tm, arr.shape[1]), imap))
        for k in chips:
            for q, (pa, pb), cols in per_chip[k]:
                if q == p:
                    local[k].append((p, (0, n4) if tiled else (pa, pb), cols))
    return specs, local


def _dgrad_norm(dys, w4, x, g, dres, *, name, plan=None):
    S, D = x.shape
    n4 = w4.shape[2]
    tm = _row_tile(S, 512)
    per_chip = _segments([a.shape[1] for a in dys], n4)
    n_p = len(dys)

    def body(*refs):
        dy_refs = refs[:n_p]
        w_ref, x_ref, g_ref, dres_ref, dx_ref, dg_ref = refs[n_p:]

        @pl.when(pl.program_id(0) == 0)
        def _():
            dg_ref[...] = jnp.zeros_like(dg_ref)

        dxn = None
        for k in range(N_CHIPS):
            for p, (pa, pb), (ca, cb) in per_chip[k]:
                part = _dot_nt(dy_refs[p][:, pa:pb], w_ref[k, :, ca:cb])
                dxn = part if dxn is None else dxn + part
        xf = x_ref[...]
        r = lax.rsqrt(jnp.mean(xf * xf, axis=-1, keepdims=True) + EPS)
        xhat = xf * r
        dxhat = dxn * g_ref[...]
        dx_ref[...] = dres_ref[...] + r * (dxhat - xhat * jnp.mean(dxhat * xhat, axis=-1, keepdims=True))
        dg_ref[...] += jnp.sum(dxn * xhat, axis=0, keepdims=True)

    row = lambda w: pl.BlockSpec((tm, w), lambda i: (i, 0))
    return _call(
        body, name=name, grid=(S // tm,), plan=plan,
        in_specs=[row(a.shape[1]) for a in dys]
        + [pl.BlockSpec(w4.shape, lambda i: (0, 0, 0), pipeline_mode=pl.Buffered(1)),
           row(D), pl.BlockSpec((1, D), lambda i: (0, 0)), row(D)],
        out_specs=[row(D), pl.BlockSpec((1, D), lambda i: (0, 0))],
        out_shape=[jax.ShapeDtypeStruct((S, D), F32), jax.ShapeDtypeStruct((1, D), F32)],
        args=(*dys, w4, x, g, dres))


def _wgrad(a, dys, *, name, rows, plan=None):
    S, K = a.shape
    n4 = sum(p.shape[1] for p in dys) // N_CHIPS
    tm = _row_tile(S, rows)
    ns = S // tm
    specs, local = _piece_specs(dys, n4, tm)
    n_p = len(dys)

    def body(*refs):
        a_ref = refs[0]
        dy_refs = refs[1:1 + n_p]
        o_ref, o16_ref = refs[1 + n_p:]
        n, s = pl.program_id(0), pl.program_id(1)

        @pl.when(s == 0)
        def _():
            o_ref[...] = jnp.zeros_like(o_ref)

        for k in range(N_CHIPS):
            @pl.when(n == k)
            def _():
                av = a_ref[...]
                for p, (pa, pb), (ca, cb) in local[k]:
                    o_ref[:, ca:cb] += _dot_tn(av, dy_refs[p][:, pa:pb])

        @pl.when(s == ns - 1)
        def _():
            o16_ref[...] = o_ref[...].astype(BF16)

    out = pl.BlockSpec((None, K, n4), lambda n, s: (n, 0, 0))
    return _call(
        body, name=name, grid=(N_CHIPS, ns), parallel=(0,), plan=plan,
        in_specs=[pl.BlockSpec((tm, K), lambda n, s: (s, 0))] + specs,
        out_specs=[out, out],
        out_shape=[jax.ShapeDtypeStruct((N_CHIPS, K, n4), F32), jax.ShapeDtypeStruct((N_CHIPS, K, n4), BF16)],
        args=(a, *dys))


def _tiles(x):
    return x.reshape(x.shape[0] // SUBLANES, SUBLANES, x.shape[1])


def _shift_down(xp, s):
    n = xp.shape[0] - SUBLANES
    if s == SUBLANES:
        return xp[:n, :]
    t = _tiles(xp)
    rot = pltpu.roll(t, s, 1)
    sub = lax.broadcasted_iota(jnp.int32, t.shape, 1)[1:]
    return jnp.where(sub >= s, rot[1:], rot[:-1]).reshape(n, xp.shape[1])


def _shift_up(xn, s):
    n = xn.shape[0] - SUBLANES
    if s == SUBLANES:
        return xn[SUBLANES:, :]
    t = _tiles(xn)
    rot = pltpu.roll(t, SUBLANES - s, 1)
    sub = lax.broadcasted_iota(jnp.int32, t.shape, 1)[1:]
    return jnp.where(sub < SUBLANES - s, rot[:-1], rot[1:]).reshape(n, xn.shape[1])


def _pooled(u, halo, first_tile, row0):
    T = u.shape[0]
    halo = jnp.where(first_tile, 0.0, halo)
    pad = jnp.zeros((SUBLANES, u.shape[1]), F32)
    up = jnp.concatenate([pad, halo, u], axis=0)
    t1 = (row0 + lax.broadcasted_iota(jnp.int32, (T, 1), 0) + 1).astype(F32)
    outs = []
    for gi, w in enumerate(POOL_WINDOWS):
        s = up[:, gi * LANES:(gi + 1) * LANES]
        k = 1
        while k < w:
            if k < SUBLANES:
                s = jnp.concatenate([s[:SUBLANES, :], s[SUBLANES:, :] + _shift_down(s, k)], axis=0)
            else:
                s = s[SUBLANES:, :] + _shift_down(s, k)
            k *= 2
        s = s[-T:, :]
        inv = 1.0 / jnp.minimum(t1, float(w))
        outs.append(s * inv - u[:, gi * LANES:(gi + 1) * LANES])
    return outs


def _pool_fwd(z, pool_w, pool_scale, *, name):
    S = z.shape[0]
    P = pool_scale.shape[1]
    T = _row_tile(S, 512)
    hb = T // HALO_POOL

    def body(u_ref, halo_ref, pw_ref, ps_ref, o_ref):
        i = pl.program_id(0)
        pooled = _pooled(u_ref[...], halo_ref[...], i == 0, i * T)
        for gi in range(len(POOL_WINDOWS)):
            mixed = _dot(pooled[gi].astype(BF16), pw_ref[gi].astype(BF16))
            cols = slice(gi * LANES, (gi + 1) * LANES)
            o_ref[:, cols] = (mixed * ps_ref[:, cols]).astype(BF16)

    return _call(
        body, name=name, grid=(S // T,), parallel=(0,),
        in_specs=[pl.BlockSpec((T, P), lambda i: (i, 0)),
                  pl.BlockSpec((HALO_POOL, P), lambda i: (jnp.maximum(i * hb - 1, 0), 0)),
                  pl.BlockSpec(pool_w.shape, lambda i: (0, 0, 0)),
                  pl.BlockSpec((1, P), lambda i: (0, 0))],
        out_specs=[pl.BlockSpec((T, P), lambda i: (i, 0))],
        out_shape=[jax.ShapeDtypeStruct((S, P), BF16)],
        args=(z, z, pool_w, pool_scale))[0][0]


def _pool_bwd(z, dpm, pool_w, pool_scale, *, name):
    S, P = dpm.shape
    T = _row_tile(S, 512)
    hb = T // HALO_POOL
    nt = S // T

    def body(u_ref, halo_ref, d_ref, dnext_ref, pw_ref, ps_ref, du_ref, dpw_ref, dps_ref):
        i = pl.program_id(0)

        @pl.when(i == 0)
        def _():
            dpw_ref[...] = jnp.zeros_like(dpw_ref)
            dps_ref[...] = jnp.zeros_like(dps_ref)

        pooled = _pooled(u_ref[...], halo_ref[...], i == 0, i * T)
        dnext = jnp.where(i == nt - 1, 0.0, dnext_ref[...])
        pad = jnp.zeros((SUBLANES, P), F32)
        dext = jnp.concatenate([d_ref[...], dnext, pad], axis=0)
        t1 = (i * T + lax.broadcasted_iota(jnp.int32, (T + HALO_POOL + SUBLANES, 1), 0) + 1).astype(F32)
        for gi, w in enumerate(POOL_WINDOWS):
            cols = slice(gi * LANES, (gi + 1) * LANES)
            pw = pw_ref[gi].astype(BF16)
            pg = pooled[gi].astype(BF16)
            mixed = _dot(pg, pw)
            dps_ref[:, cols] += jnp.sum(d_ref[:, cols] * mixed, axis=0, keepdims=True)
            dmixed = (dext[:, cols] * ps_ref[:, cols]).astype(BF16)
            dpw_ref[gi] += _dot_tn(pg, dmixed[:T, :])
            dpooled = _dot_nt(dmixed, pw)
            e = dpooled * (1.0 / jnp.minimum(t1, float(w)))
            k = 1
            while k < w:
                if k < SUBLANES:
                    e = jnp.concatenate([e[:-SUBLANES, :] + _shift_up(e, k), e[-SUBLANES:, :]], axis=0)
                else:
                    e = e[:-SUBLANES, :] + _shift_up(e, k)
                k *= 2
            du_ref[:, cols] = (e[:T, :] - dpooled[:T, :]).astype(BF16)

    return _call(
        body, name=name, grid=(nt,),
        in_specs=[pl.BlockSpec((T, P), lambda i: (i, 0)),
                  pl.BlockSpec((HALO_POOL, P), lambda i: (jnp.maximum(i * hb - 1, 0), 0)),
                  pl.BlockSpec((T, P), lambda i: (i, 0)),
                  pl.BlockSpec((HALO_POOL, P), lambda i: (jnp.minimum((i + 1) * hb, S // HALO_POOL - 1), 0)),
                  pl.BlockSpec(pool_w.shape, lambda i: (0, 0, 0)),
                  pl.BlockSpec((1, P), lambda i: (0, 0))],
        out_specs=[pl.BlockSpec((T, P), lambda i: (i, 0)),
                   pl.BlockSpec(pool_w.shape, lambda i: (0, 0, 0)),
                   pl.BlockSpec((1, P), lambda i: (0, 0))],
        out_shape=[jax.ShapeDtypeStruct((S, P), BF16),
                   jax.ShapeDtypeStruct(pool_w.shape, F32),
                   jax.ShapeDtypeStruct((1, P), F32)],
        args=(z, z, dpm, dpm, pool_w, pool_scale))[0]


def _cumsum_rows(x):
    n = x.shape[0]
    row = lax.broadcasted_iota(jnp.int32, x.shape, 0)
    s = 1
    while s < n:
        x = x + jnp.where(row >= s, pltpu.roll(x, s, 0), 0.0)
        s *= 2
    return x


def _rev_cumsum_rows(x):
    n = x.shape[0]
    row = lax.broadcasted_iota(jnp.int32, x.shape, 0)
    s = 1
    while s < n:
        x = x + jnp.where(row < n - s, pltpu.roll(x, n - s, 0), 0.0)
        s *= 2
    return x


def _chunk_prep(zq, zf, lb, b_ref):
    n_sub = CHUNK // SUB
    sq = _sigmoid(zq)
    q = zq * sq
    sf = _sigmoid(zf)
    f = lb + (1.0 - lb) * sf
    k = 1.0 - f
    b = _cumsum_rows(jnp.log(f))
    b_ref[...] = b
    shape = (SUB, b.shape[1])
    ends = [jnp.broadcast_to(b_ref[pl.ds(SUB * j + SUB - 1, 1), :], shape) for j in range(n_sub)]
    mids = [jnp.broadcast_to(b_ref[pl.ds(SUB * j + SUB // 2 - 1, 1), :], shape) for j in range(n_sub)]
    own = [b[SUB * j:SUB * (j + 1), :] for j in range(n_sub)]
    m0 = jnp.concatenate(mids, axis=0)
    e1 = jnp.concatenate(ends, axis=0)
    eq = [jnp.exp(jnp.minimum(b - m0, EXP_CLAMP))]
    for d in range(1, n_sub):
        rd = jnp.concatenate([own[j] if j < d else ends[j - d] for j in range(n_sub)], axis=0)
        eq.append(jnp.exp(b - rd))
    ek0 = jnp.exp(jnp.minimum(m0 - b, EXP_CLAMP))
    ek1 = jnp.exp(e1 - b)
    b_last = b_ref[pl.ds(CHUNK - 1, 1), :]
    return dict(q=q, k=k, f=f, sq=sq, sf=sf, b=b, eq=eq, ek0=ek0, ek1=ek1,
                eb=jnp.exp(b), ekl=jnp.exp(b_last - b), el=jnp.exp(b_last))


def _chunk_masks():
    ti = lax.broadcasted_iota(jnp.int32, (CHUNK, CHUNK), 0)
    si = lax.broadcasted_iota(jnp.int32, (CHUNK, CHUNK), 1)
    shift = SUB.bit_length() - 1
    dsub = jnp.right_shift(ti, shift) - jnp.right_shift(si, shift)
    masks = [(dsub == 0) & (si <= ti)]
    masks += [dsub == d for d in range(1, CHUNK // SUB)]
    return masks


def _chunk_attn(p, masks):
    qd = [(p["q"] * e).astype(BF16) for e in p["eq"]]
    k0 = (p["k"] * p["ek0"]).astype(BF16)
    k1 = (p["k"] * p["ek1"]).astype(BF16)
    a = jnp.where(masks[0], _dot_nt(qd[0], k0), 0.0)
    for d in range(1, len(masks)):
        a = jnp.where(masks[d], _dot_nt(qd[d], k1), a)
    return a, qd, k0, k1


def _hgrn_fwd(z, lb, norm_g, *, name, plan=None):
    S = z.shape[0]
    HW = lb.shape[1]
    NH = HW // LANES
    T = _row_tile(S, 512)
    nc = T // CHUNK

    def body(zq_ref, zf_ref, zi_ref, zo_ref, lb_ref, ng_ref, o_ref, of_ref, st_ref, s_scr, b_scr):
        @pl.when(pl.program_id(0) == 0)
        def _():
            s_scr[...] = jnp.zeros_like(s_scr)

        ng = ng_ref[...]
        masks = _chunk_masks()

        def chunk(c, carry):
            rows = pl.ds(pl.multiple_of(c * CHUNK, CHUNK), CHUNK)
            for h in range(NH):
                cols = slice(h * LANES, (h + 1) * LANES)
                p = _chunk_prep(zq_ref[rows, cols], zf_ref[rows, cols], lb_ref[:, cols], b_scr.at[h])
                v = zi_ref[rows, cols].astype(BF16)
                zo = zo_ref[rows, cols]
                st = s_scr[h]
                st_ref[c, h] = st
                a, _, _, _ = _chunk_attn(p, masks)
                o = _dot(a.astype(BF16), v) + _dot_nt((p["q"] * p["eb"]).astype(BF16), st.astype(BF16))
                s_scr[h] = st * p["el"] + _dot_tn(v, (p["k"] * p["ekl"]).astype(BF16))
                o_ref[rows, cols] = o
                r = lax.rsqrt(jnp.mean(o * o, axis=-1, keepdims=True) + EPS)
                of_ref[rows, cols] = (o * r * ng * (zo * _sigmoid(zo))).astype(BF16)
            return carry

        lax.fori_loop(0, nc, chunk, 0, unroll=8)

    part = lambda k: pl.BlockSpec((T, HW), lambda i, k=k: (i, k))
    return _call(
        body, name=name, grid=(S // T,), plan=plan,
        in_specs=[part(1), part(2), part(3), part(4),
                  pl.BlockSpec((1, HW), lambda i: (0, 0)), pl.BlockSpec((1, LANES), lambda i: (0, 0))],
        out_specs=[pl.BlockSpec((T, HW), lambda i: (i, 0)), pl.BlockSpec((T, HW), lambda i: (i, 0)),
                   pl.BlockSpec((nc, NH, LANES, LANES), lambda i: (i, 0, 0, 0))],
        out_shape=[jax.ShapeDtypeStruct((S, HW), F32), jax.ShapeDtypeStruct((S, HW), BF16),
                   jax.ShapeDtypeStruct((S // CHUNK, NH, LANES, LANES), F32)],
        scratch=[pltpu.VMEM((NH, LANES, LANES), F32), pltpu.VMEM((NH, CHUNK, LANES), F32)],
        args=(z, z, z, z, lb, norm_g))


def _hgrn_bwd(z, lb, norm_g, o_raw, states, dof, *, name, plan=None):
    S = z.shape[0]
    HW = lb.shape[1]
    NH = HW // LANES
    T = _row_tile(S, 512)
    nc = T // CHUNK
    nt = S // T

    def body(zq_ref, zf_ref, zi_ref, zo_ref, lb_ref, ng_ref, o_ref, st_ref, dof_ref,
             dzq_ref, dzf_ref, dzi_ref, dzo_ref, dlb_ref, dng_ref, ds_scr, b_scr):
        @pl.when(pl.program_id(0) == 0)
        def _():
            ds_scr[...] = jnp.zeros_like(ds_scr)
            dlb_ref[...] = jnp.zeros_like(dlb_ref)
            dng_ref[...] = jnp.zeros_like(dng_ref)

        ng = ng_ref[...]
        masks = _chunk_masks()
        last_row = lax.broadcasted_iota(jnp.int32, (CHUNK, 1), 0) == CHUNK - 1

        def chunk(cr, carry):
            c = nc - 1 - cr
            rows = pl.ds(pl.multiple_of(c * CHUNK, CHUNK), CHUNK)
            for h in range(NH):
                cols = slice(h * LANES, (h + 1) * LANES)
                lbv = lb_ref[:, cols]
                zq, zf, zo = zq_ref[rows, cols], zf_ref[rows, cols], zo_ref[rows, cols]
                o = o_ref[rows, cols]
                dof_c = dof_ref[rows, cols]
                st = st_ref[c, h]
                dst = ds_scr[h]

                so = _sigmoid(zo)
                r = lax.rsqrt(jnp.mean(o * o, axis=-1, keepdims=True) + EPS)
                ohat = o * r
                d_on = dof_c * (zo * so)
                dzo_ref[rows, cols] = (dof_c * ohat * ng * _dsilu(zo, so)).astype(BF16)
                dng_ref[:, cols] += jnp.sum(d_on * ohat, axis=0, keepdims=True)
                dohat = d_on * ng
                do = (r * (dohat - ohat * jnp.mean(dohat * ohat, axis=-1, keepdims=True))).astype(BF16)

                p = _chunk_prep(zq, zf, lbv, b_scr.at[h])
                q, k = p["q"], p["k"]
                v = zi_ref[rows, cols].astype(BF16)
                a, qd, k0, k1 = _chunk_attn(p, masks)
                ktl = (k * p["ekl"]).astype(BF16)
                dstb = dst.astype(BF16)

                da = _dot_nt(do, v)
                dzi_ref[rows, cols] = (_dot_tn(a.astype(BF16), do) + _dot_nt(ktl, dstb)).astype(BF16)

                da0 = jnp.where(masks[0], da, 0.0).astype(BF16)
                rq = _dot(da0, k0)
                rk0 = _dot_tn(da0, qd[0])
                dq = rq * p["eq"][0]
                db = qd[0].astype(F32) * rq - k0.astype(F32) * rk0
                rk1 = jnp.zeros_like(rk0)
                for d in range(1, len(masks)):
                    dad = jnp.where(masks[d], da, 0.0).astype(BF16)
                    rq = _dot(dad, k1)
                    dq = dq + rq * p["eq"][d]
                    db = db + qd[d].astype(F32) * rq
                    rk1 = rk1 + _dot_tn(dad, qd[d])
                dk = rk0 * p["ek0"] + rk1 * p["ek1"]
                db = db - k1.astype(F32) * rk1
                qe = (q * p["eb"]).astype(BF16)
                rq = _dot(do, st.astype(BF16))
                dq = dq + rq * p["eb"]
                db = db + qe.astype(F32) * rq
                rk = _dot(v, dstb)
                dk = dk + rk * p["ekl"]
                db = db - ktl.astype(F32) * rk

                st_new = st * p["el"] + _dot_tn(v, ktl)
                db = db + jnp.where(last_row, jnp.sum(dstb.astype(F32) * st_new, axis=0, keepdims=True), 0.0)
                dg = _rev_cumsum_rows(db)
                ds_scr[h] = dst * p["el"] + _dot_tn(do, qe)

                dzq_ref[rows, cols] = (dq * _dsilu(zq, p["sq"])).astype(BF16)
                df = dg / p["f"] - dk
                sf = p["sf"]
                dzf_ref[rows, cols] = (df * (1.0 - lbv) * sf * (1.0 - sf)).astype(BF16)
                dlb_ref[:, cols] += jnp.sum(df * (1.0 - sf), axis=0, keepdims=True)
            return carry

        lax.fori_loop(0, nc, chunk, 0, unroll=4)

    rev = lambda i: nt - 1 - i
    part = lambda k: pl.BlockSpec((T, HW), lambda i, k=k: (rev(i), k))
    blk = pl.BlockSpec((T, HW), lambda i: (rev(i), 0))
    vec = pl.BlockSpec((1, HW), lambda i: (0, 0))
    return _call(
        body, name=name, grid=(nt,), plan=plan,
        in_specs=[part(1), part(2), part(3), part(4), vec, pl.BlockSpec((1, LANES), lambda i: (0, 0)),
                  blk, pl.BlockSpec((nc, NH, LANES, LANES), lambda i: (rev(i), 0, 0, 0)), blk],
        out_specs=[blk, blk, blk, blk, vec, vec],
        out_shape=[jax.ShapeDtypeStruct((S, HW), BF16)] * 4 + [jax.ShapeDtypeStruct((1, HW), F32)] * 2,
        scratch=[pltpu.VMEM((NH, LANES, LANES), F32), pltpu.VMEM((NH, CHUNK, LANES), F32)],
        args=(z, z, z, z, lb, norm_g, o_raw, states, dof))


def _gate_specs(T, D):
    half = D // 2
    first = (5 * half) // half
    return [pl.BlockSpec((T, half), lambda i, k=k: (i, first + k)) for k in range(4)]


def _gates(zg_refs, bg_ref, D):
    half = D // 2
    za = jnp.concatenate([zg_refs[0][...], zg_refs[1][...]], axis=1) + bg_ref[:, :D]
    zb = jnp.concatenate([zg_refs[2][...], zg_refs[3][...]], axis=1) + bg_ref[:, D:]
    return _sigmoid(za), _sigmoid(zb)


def _mix_fwd(x, pm, of, z, b_gate, w_pa4, w_pb4, w_o4, g_next, *, name, plan=None):
    S, D = x.shape
    P = pm.shape[1]
    T = _row_tile(S, 512)

    def body(x_ref, pm_ref, of_ref, g0, g1, g2, g3, bg_ref, wpa_ref, wpb_ref, wo_ref, gn_ref,
             xo_ref, ya_ref, yb_ref, xn_ref):
        pmv, ofv = pm_ref[...], of_ref[...]
        ya = jnp.concatenate([_dot(pmv, wpa_ref[k]) for k in range(N_CHIPS)], axis=1)
        yb = jnp.concatenate([_dot(ofv, wpb_ref[k]) for k in range(N_CHIPS)], axis=1)
        ga, gb = _gates((g0, g1, g2, g3), bg_ref, D)
        merged = (ga * ya + gb * yb).astype(BF16)
        x_mid = x_ref[...] + _dot(merged, wo_ref[...].reshape(D, D))
        xo_ref[...] = x_mid
        xn_ref[...] = _rms(x_mid, gn_ref[...])
        ya_ref[...] = ya.astype(BF16)
        yb_ref[...] = yb.astype(BF16)

    row = lambda w: pl.BlockSpec((T, w), lambda i: (i, 0))
    full = lambda a: pl.BlockSpec(a.shape, lambda i: (0,) * a.ndim)
    return _call(
        body, name=name, grid=(S // T,), parallel=(0,), plan=plan,
        in_specs=[row(D), row(P), row(P)] + _gate_specs(T, D) + [full(b_gate), full(w_pa4), full(w_pb4), full(w_o4),
                                                                  full(g_next)],
        out_specs=[row(D), row(D), row(D), row(D)],
        out_shape=[jax.ShapeDtypeStruct((S, D), F32), jax.ShapeDtypeStruct((S, D), BF16),
                   jax.ShapeDtypeStruct((S, D), BF16), jax.ShapeDtypeStruct((S, D), BF16)],
        args=(x, pm, of, z, z, z, z, b_gate, w_pa4, w_pb4, w_o4, g_next))


def _mix_bwd(dxm, ya, yb, z, b_gate, pm, of, w_pa4, w_pb4, w_o4, *, name, plan=None):
    S, D = dxm.shape
    P = pm.shape[1]
    q4 = D // N_CHIPS
    T = _row_tile(S, 512)
    nt = S // T

    def body(dx_ref, ya_ref, yb_ref, g0, g1, g2, g3, bg_ref, pm_ref, of_ref, wpa_ref, wpb_ref, wo_ref,
             dzg_ref, dpm_ref, dof_ref, dwo_ref, dwpa_ref, dwpb_ref, dbg_ref, dwo16_ref, dwpa16_ref, dwpb16_ref):
        i = pl.program_id(0)

        @pl.when(i == 0)
        def _():
            dwo_ref[...] = jnp.zeros_like(dwo_ref)
            dwpa_ref[...] = jnp.zeros_like(dwpa_ref)
            dwpb_ref[...] = jnp.zeros_like(dwpb_ref)
            dbg_ref[...] = jnp.zeros_like(dbg_ref)

        dxb = dx_ref[...].astype(BF16)
        ya = ya_ref[...].astype(F32)
        yb = yb_ref[...].astype(F32)
        ga, gb = _gates((g0, g1, g2, g3), bg_ref, D)
        merged = (ga * ya + gb * yb).astype(BF16)
        dwo_ref[...] += _dot_tn(merged, dxb).reshape(N_CHIPS, q4, D)
        dm = _dot_nt(dxb, wo_ref[...].reshape(D, D))
        dza = dm * ya * ga * (1.0 - ga)
        dzb = dm * yb * gb * (1.0 - gb)
        dzg_ref[:, :D] = dza.astype(BF16)
        dzg_ref[:, D:] = dzb.astype(BF16)
        dbg_ref[:, :D] += jnp.sum(dza, axis=0, keepdims=True)
        dbg_ref[:, D:] += jnp.sum(dzb, axis=0, keepdims=True)
        dya = (dm * ga).astype(BF16)
        dyb = (dm * gb).astype(BF16)
        pmv, ofv = pm_ref[...], of_ref[...]
        dpm = jnp.zeros((T, P), F32)
        dof = jnp.zeros((T, P), F32)
        for k in range(N_CHIPS):
            cols = slice(k * q4, (k + 1) * q4)
            dwpa_ref[k] += _dot_tn(pmv, dya[:, cols])
            dwpb_ref[k] += _dot_tn(ofv, dyb[:, cols])
            dpm = dpm + _dot_nt(dya[:, cols], wpa_ref[k])
            dof = dof + _dot_nt(dyb[:, cols], wpb_ref[k])
        dpm_ref[...] = dpm
        dof_ref[...] = dof

        @pl.when(i == nt - 1)
        def _():
            dwo16_ref[...] = dwo_ref[...].astype(BF16)
            dwpa16_ref[...] = dwpa_ref[...].astype(BF16)
            dwpb16_ref[...] = dwpb_ref[...].astype(BF16)

    row = lambda w: pl.BlockSpec((T, w), lambda i: (i, 0))
    full = lambda a: pl.BlockSpec(a.shape, lambda i: (0,) * a.ndim, pipeline_mode=pl.Buffered(1))
    like = lambda a, dt: jax.ShapeDtypeStruct(a.shape, dt)
    return _call(
        body, name=name, grid=(nt,), plan=plan,
        in_specs=[row(D), row(D), row(D)] + _gate_specs(T, D) + [full(b_gate), row(P), row(P),
                                                                  full(w_pa4), full(w_pb4), full(w_o4)],
        out_specs=[row(2 * D), row(P), row(P), full(w_o4), full(w_pa4), full(w_pb4), full(b_gate),
                   full(w_o4), full(w_pa4), full(w_pb4)],
        out_shape=[jax.ShapeDtypeStruct((S, 2 * D), BF16), jax.ShapeDtypeStruct((S, P), F32),
                   jax.ShapeDtypeStruct((S, P), F32), like(w_o4, F32), like(w_pa4, F32), like(w_pb4, F32),
                   like(b_gate, F32), like(w_o4, BF16), like(w_pa4, BF16), like(w_pb4, BF16)],
        args=(dxm, ya, yb, z, z, z, z, b_gate, pm, of, w_pa4, w_pb4, w_o4))


def _up_conv(xn, w_up4, conv_w, conv_b, *, name, plan=None):
    S, D = xn.shape
    f4 = w_up4.shape[2]
    nf = N_CHIPS // 2
    F = nf * f4
    T = _row_tile(S, 512)

    def body(xn_ref, wv_ref, wg_ref, cwv_ref, cwg_ref, cbv_ref, cbg_ref,
             hv_ref, hg_ref, val_ref, gate_ref, a_ref, pv_scr, pg_scr):
        i = pl.program_id(1)
        xv = xn_ref[...]

        def side(w_ref, cw_ref, cb_ref, h_ref, p_scr):
            h = _dot(xv, w_ref[...])
            h_ref[...] = h.astype(BF16)
            hp = jnp.concatenate([jnp.where(i == 0, 0.0, p_scr[...]), h], axis=0)
            p_scr[...] = h[-SUBLANES:, :]
            cw = cw_ref[...]
            return cw[0:1, :] * _shift_down(hp, 2) + cw[1:2, :] * _shift_down(hp, 1) + cw[2:3, :] * h + cb_ref[...]

        val = side(wv_ref, cwv_ref, cbv_ref, hv_ref, pv_scr)
        gate = side(wg_ref, cwg_ref, cbg_ref, hg_ref, pg_scr)
        val_ref[...] = val.astype(BF16)
        gate_ref[...] = gate.astype(BF16)
        a_ref[...] = (gate * _sigmoid(gate) * val).astype(BF16)

    out = pl.BlockSpec((T, f4), lambda f, i: (i, f))
    return _call(
        body, name=name, grid=(nf, S // T), plan=plan,
        in_specs=[pl.BlockSpec((T, D), lambda f, i: (i, 0)),
                  pl.BlockSpec((None, D, f4), lambda f, i: (f, 0, 0)),
                  pl.BlockSpec((None, D, f4), lambda f, i: (nf + f, 0, 0)),
                  pl.BlockSpec((3, f4), lambda f, i: (0, f)),
                  pl.BlockSpec((3, f4), lambda f, i: (0, nf + f)),
                  pl.BlockSpec((1, f4), lambda f, i: (0, f)),
                  pl.BlockSpec((1, f4), lambda f, i: (0, nf + f))],
        out_specs=[out] * 5,
        out_shape=[jax.ShapeDtypeStruct((S, F), BF16)] * 5,
        scratch=[pltpu.VMEM((SUBLANES, f4), F32), pltpu.VMEM((SUBLANES, f4), F32)],
        args=(xn, w_up4, w_up4, conv_w, conv_w, conv_b, conv_b))


def _down(a, w_down4, x, g_next, *, name, plan=None):
    S, F = a.shape
    D = x.shape[1]
    T = _row_tile(S, 1024)

    def body(a_ref, wd_ref, x_ref, gn_ref, o_ref, xn_ref):
        x_out = x_ref[...] + _dot(a_ref[...], wd_ref[...].reshape(F, D))
        o_ref[...] = x_out
        xn_ref[...] = _rms(x_out, gn_ref[...])

    row = lambda w: pl.BlockSpec((T, w), lambda i: (i, 0))
    return _call(
        body, name=name, grid=(S // T,), parallel=(0,), plan=plan,
        in_specs=[row(F), pl.BlockSpec(w_down4.shape, lambda i: (0, 0, 0), pipeline_mode=pl.Buffered(1)),
                  row(D), pl.BlockSpec((1, D), lambda i: (0, 0))],
        out_specs=[row(D), row(D)],
        out_shape=[jax.ShapeDtypeStruct((S, D), F32), jax.ShapeDtypeStruct((S, D), BF16)],
        args=(a, w_down4, x, g_next))


def _ffn_down_bwd(dxo, hv, hg, val16, gate16, conv_w, w_down4, *, name, plan=None):
    S = hv.shape[0]
    _, f4, D = w_down4.shape
    F = N_CHIPS * f4
    T = _row_tile(S, 512)
    nf = 2
    tf = 2 * f4
    nt = S // T

    def body(dx_ref, hv_ref, hg_ref, val_ref, gate_ref, cwv_ref, cwg_ref, wd_ref,
             dhv_ref, dhg_ref, dwd_ref, dwd16_ref, dcwv_ref, dcwg_ref, dcbv_ref, dcbg_ref, cv_scr, cg_scr):
        i = pl.program_id(1)

        @pl.when(i == 0)
        def _():
            cv_scr[...] = jnp.zeros_like(cv_scr)
            cg_scr[...] = jnp.zeros_like(cg_scr)
            dwd_ref[...] = jnp.zeros_like(dwd_ref)
            dcwv_ref[...] = jnp.zeros_like(dcwv_ref)
            dcwg_ref[...] = jnp.zeros_like(dcwg_ref)
            dcbv_ref[...] = jnp.zeros_like(dcbv_ref)
            dcbg_ref[...] = jnp.zeros_like(dcbg_ref)

        dxb = dx_ref[...].astype(BF16)
        val = val_ref[...].astype(F32)
        gate = gate_ref[...].astype(F32)
        sg = _sigmoid(gate)
        sil = gate * sg
        dwd_ref[...] += _dot_tn((sil * val).astype(BF16), dxb).reshape(2, f4, D)
        da = _dot_nt(dxb, wd_ref[...].reshape(tf, D))

        def conv_bwd(dhc, h0, cw, c_scr, dh_ref, dcw_ref, dcb_ref):
            ext = jnp.concatenate([dhc, c_scr[...]], axis=0)
            n1 = _shift_up(ext, 1)
            n2 = _shift_up(ext, 2)
            dh_ref[...] = (cw[2:3, :] * dhc + cw[1:2, :] * n1 + cw[0:1, :] * n2).astype(BF16)
            c_scr[...] = dhc[:SUBLANES, :]
            dcw_ref[0:1, :] += jnp.sum(n2 * h0, axis=0, keepdims=True)
            dcw_ref[1:2, :] += jnp.sum(n1 * h0, axis=0, keepdims=True)
            dcw_ref[2:3, :] += jnp.sum(dhc * h0, axis=0, keepdims=True)
            dcb_ref[...] += jnp.sum(dhc, axis=0, keepdims=True)

        conv_bwd(da * sil, hv_ref[...].astype(F32), cwv_ref[...], cv_scr, dhv_ref, dcwv_ref, dcbv_ref)
        conv_bwd(da * val * _dsilu(gate, sg), hg_ref[...].astype(F32), cwg_ref[...], cg_scr, dhg_ref, dcwg_ref,
                 dcbg_ref)

        @pl.when(i == nt - 1)
        def _():
            dwd16_ref[...] = dwd_ref[...].astype(BF16)

    rev = lambda i: nt - 1 - i
    wd_spec = pl.BlockSpec((2, f4, D), lambda f, i: (f, 0, 0))
    return _call(
        body, name=name, grid=(nf, nt), plan=plan,
        in_specs=[pl.BlockSpec((T, D), lambda f, i: (rev(i), 0)),
                  pl.BlockSpec((T, tf), lambda f, i: (rev(i), f)),
                  pl.BlockSpec((T, tf), lambda f, i: (rev(i), f)),
                  pl.BlockSpec((T, tf), lambda f, i: (rev(i), f)),
                  pl.BlockSpec((T, tf), lambda f, i: (rev(i), f)),
                  pl.BlockSpec((3, tf), lambda f, i: (0, f)),
                  pl.BlockSpec((3, tf), lambda f, i: (0, nf + f)),
                  wd_spec],
        out_specs=[pl.BlockSpec((T, tf), lambda f, i: (rev(i), f)),
                   pl.BlockSpec((T, tf), lambda f, i: (rev(i), f)),
                   wd_spec, wd_spec,
                   pl.BlockSpec((3, tf), lambda f, i: (0, f)),
                   pl.BlockSpec((3, tf), lambda f, i: (0, f)),
                   pl.BlockSpec((1, tf), lambda f, i: (0, f)),
                   pl.BlockSpec((1, tf), lambda f, i: (0, f))],
        out_shape=[jax.ShapeDtypeStruct((S, F), BF16), jax.ShapeDtypeStruct((S, F), BF16),
                   jax.ShapeDtypeStruct((N_CHIPS, f4, D), F32), jax.ShapeDtypeStruct((N_CHIPS, f4, D), BF16),
                   jax.ShapeDtypeStruct((3, F), F32), jax.ShapeDtypeStruct((3, F), F32),
                   jax.ShapeDtypeStruct((1, F), F32), jax.ShapeDtypeStruct((1, F), F32)],
        scratch=[pltpu.VMEM((SUBLANES, tf), F32), pltpu.VMEM((SUBLANES, tf), F32)],
        args=(dxo, hv, hg, val16, gate16, conv_w, conv_w, w_down4))


def _down_loss(a, w_down4, x, g, target, *, name):
    S, F = a.shape
    D = x.shape[1]
    T = _row_tile(S, 512)

    def body(a_ref, wd_ref, x_ref, g_ref, t_ref, loss_ref, dx_ref, dg_ref):
        @pl.when(pl.program_id(0) == 0)
        def _():
            loss_ref[...] = jnp.zeros_like(loss_ref)
            dg_ref[...] = jnp.zeros_like(dg_ref)

        xf = x_ref[...] + _dot(a_ref[...], wd_ref[...].reshape(F, D))
        r = lax.rsqrt(jnp.mean(xf * xf, axis=-1, keepdims=True) + EPS)
        xhat = xf * r
        err = xhat * g_ref[...] - t_ref[...]
        loss_ref[...] += jnp.sum(err * err, axis=0, keepdims=True) * (0.5 / D)
        dy = err * (1.0 / D)
        dxhat = dy * g_ref[...]
        dx_ref[...] = r * (dxhat - xhat * jnp.mean(dxhat * xhat, axis=-1, keepdims=True))
        dg_ref[...] += jnp.sum(dy * xhat, axis=0, keepdims=True)

    row = lambda w: pl.BlockSpec((T, w), lambda i: (i, 0))
    vec = pl.BlockSpec((1, D), lambda i: (0, 0))
    return _call(
        body, name=name, grid=(S // T,),
        in_specs=[row(F), pl.BlockSpec(w_down4.shape, lambda i: (0, 0, 0), pipeline_mode=pl.Buffered(1)),
                  row(D), vec, row(D)],
        out_specs=[vec, row(D), vec],
        out_shape=[jax.ShapeDtypeStruct((1, D), F32), jax.ShapeDtypeStruct((S, D), F32),
                   jax.ShapeDtypeStruct((1, D), F32)],
        args=(a, w_down4, x, g, target))[0]


BIG = ("w_in", "w_pa", "w_pb", "w_o", "w_up", "w_down")
SMALL = ("norm1_g", "b_gate", "pool_w", "pool_scale", "lb_logits", "hgrn_norm_g", "norm2_g", "conv_b", "final_g")
WEIGHTS = ("norm1_g", "w_in", "b_gate", "pool_w", "pool_scale", "lb_logits", "hgrn_norm_g", "w_pa", "w_pb", "w_o",
           "norm2_g", "w_up", "conv_w", "conv_b", "w_down", "final_g")


def _lower_bounds(lb_logits):
    soft = jax.nn.softmax(lb_logits.astype(F32), axis=0)
    cum = jnp.cumsum(soft, axis=0)
    return cum - cum[0:1]


def _step(x, target, sm, wts, shards=None):
    L = sm["norm1_g"].shape[0]
    wts = dict(wts)
    dist = shards is not None
    lbs, lb_vjp = jax.vjp(_lower_bounds, sm["lb_logits"])
    row = lambda a: a.reshape(1, -1)
    conv_w = sm.get("conv_w")

    def gather(names_layers, with_conv=False):
        items = [(shards[n], "rows", l) for n, l in names_layers]
        if with_conv:
            items.append((shards["conv_w"], "layer", None))
        return _GatherPlan(items)

    def landed(names_layers, outs):
        for key, arr in zip(names_layers, outs):
            wts[key] = arr

    own = {"in_proj": ("w_up",)}
    first = {"hgrn_fwd": ("w_pa", "w_pb", "w_o"), "mix_fwd": ("w_down",)}
    ahead = {"up": ("w_in", "w_pa", "w_pb", "w_o"), "down": ("w_down",)}
    conv_rider = "mix_fwd"

    def riders(l, kernel):
        if not dist:
            return [], None
        keys = [(n, l) for n in own.get(kernel, ())]
        keys += [(n, l) for n in first.get(kernel, ())] if l == 0 else []
        keys += [(n, l + 1) for n in ahead.get(kernel, ())] if l + 1 < L else []
        with_conv = l == 0 and kernel == conv_rider
        return keys, (gather(keys, with_conv) if keys or with_conv else None)

    keys = [("w_in", 0)] if dist else []
    (xn1,), got = _rmsnorm(x, row(sm["norm1_g"][0]), name="norm_in", plan=gather(keys) if keys else None)
    landed(keys, got)

    saved = []
    for l in range(L):
        keys, plan = riders(l, "in_proj")
        (z,), got = _matmul(xn1, wts[("w_in", l)], name=f"in_proj_{l}", plan=plan)
        landed(keys, got)
        pm = _pool_fwd(z, sm["pool_w"][l], row(sm["pool_scale"][l]), name=f"pool_fwd_{l}")
        keys, plan = riders(l, "hgrn_fwd")
        (o_raw, of, states), got = _hgrn_fwd(z, row(lbs[l]), row(sm["hgrn_norm_g"][l]), name=f"hgrn_fwd_{l}",
                                             plan=plan)
        landed(keys, got)
        keys, plan = riders(l, "mix_fwd")
        (x_mid, ya, yb, xn2), got = _mix_fwd(x, pm, of, z, row(sm["b_gate"][l]), wts[("w_pa", l)], wts[("w_pb", l)],
                                             wts[("w_o", l)], row(sm["norm2_g"][l]), name=f"mix_fwd_{l}", plan=plan)
        landed(keys, got)
        if dist and l == 0:
            full = got[-1]
            conv_w = jnp.concatenate([full[:, k] for k in range(N_CHIPS)], axis=2)
        keys, plan = riders(l, "up")
        (hv, hg, val16, gate16, a16), got = _up_conv(xn2, wts[("w_up", l)], conv_w[l], row(sm["conv_b"][l]),
                                                     name=f"up_{l}", plan=plan)
        landed(keys, got)
        saved.append(dict(x=x, xn1=xn1, z=z, pm=pm, o_raw=o_raw, of=of, states=states,
                          x_mid=x_mid, ya=ya, yb=yb, xn2=xn2, hv=hv, hg=hg, val16=val16, gate16=gate16))
        if l + 1 < L:
            keys, plan = riders(l, "down")
            (x, xn1), got = _down(a16, wts[("w_down", l)], x_mid, row(sm["norm1_g"][l + 1]), name=f"down_{l}",
                                  plan=plan)
            landed(keys, got)
        else:
            loss_cols, dx, d_final_g = _down_loss(a16, wts[("w_down", l)], x_mid, row(sm["final_g"]), target,
                                                  name="down_loss")

    small = {k: [None] * L for k in ("norm1_g", "b_gate", "pool_w", "pool_scale", "hgrn_norm_g", "norm2_g",
                                     "conv_w", "conv_b")}
    big32, big16, recv = {}, {}, {}
    dlbs = [None] * L
    pending = []

    def scatter():
        if not (dist and pending):
            return [], None
        keys = list(pending)
        del pending[:]
        return keys, _ScatterPlan([big16[k] for k in keys])

    def sent(keys, outs):
        for key, arr in zip(keys, outs):
            recv[key] = arr

    def made(name, l, g32, g16):
        big32[(name, l)], big16[(name, l)] = g32, g16
        pending.append((name, l))

    for l in reversed(range(L)):
        s = saved[l]
        keys, plan = scatter()
        (dhv, dhg, d_wd, d_wd16, dcwv, dcwg, dcbv, dcbg), got = _ffn_down_bwd(
            dx, s["hv"], s["hg"], s["val16"], s["gate16"], conv_w[l], wts[("w_down", l)], name=f"down_bwd_{l}",
            plan=plan)
        sent(keys, got)
        made("w_down", l, d_wd, d_wd16)
        small["conv_w"][l] = jnp.concatenate([dcwv, dcwg], axis=1)
        small["conv_b"][l] = jnp.concatenate([dcbv, dcbg], axis=1)[0]
        keys, plan = scatter()
        (d_wu, d_wu16), got = _wgrad(s["xn2"], [dhv, dhg], name=f"up_wgrad_{l}", rows=2048, plan=plan)
        sent(keys, got)
        made("w_up", l, d_wu, d_wu16)
        (dxm, dg2), _ = _dgrad_norm([dhv, dhg], wts[("w_up", l)], s["x_mid"], row(sm["norm2_g"][l]), dx,
                                    name=f"up_dgrad_{l}")
        small["norm2_g"][l] = dg2[0]

        (dzg, dpm, dof, d_wo, d_wpa, d_wpb, dbg, d_wo16, d_wpa16, d_wpb16), _ = _mix_bwd(
            dxm, s["ya"], s["yb"], s["z"], row(sm["b_gate"][l]), s["pm"], s["of"],
            wts[("w_pa", l)], wts[("w_pb", l)], wts[("w_o", l)], name=f"mix_bwd_{l}")
        made("w_o", l, d_wo, d_wo16)
        made("w_pa", l, d_wpa, d_wpa16)
        made("w_pb", l, d_wpb, d_wpb16)
        small["b_gate"][l] = dbg[0]

        du, dpw, dps = _pool_bwd(s["z"], dpm, sm["pool_w"][l], row(sm["pool_scale"][l]), name=f"pool_bwd_{l}")
        small["pool_w"][l], small["pool_scale"][l] = dpw, dps[0]

        keys, plan = scatter()
        (dzq, dzf, dzi, dzo, dlb, dng), got = _hgrn_bwd(s["z"], row(lbs[l]), row(sm["hgrn_norm_g"][l]), s["o_raw"],
                                                      s["states"], dof, name=f"hgrn_bwd_{l}", plan=plan)
        sent(keys, got)
        dlbs[l] = dlb[0]
        small["hgrn_norm_g"][l] = jnp.sum(dng.reshape(-1, LANES), axis=0)

        dz = [du, dzq, dzf, dzi, dzo, dzg]
        (d_wi, d_wi16), _ = _wgrad(s["xn1"], dz, name=f"in_wgrad_{l}", rows=1024)
        made("w_in", l, d_wi, d_wi16)
        keys, plan = scatter()
        (dx, dg1), got = _dgrad_norm(dz, wts[("w_in", l)], s["x"], row(sm["norm1_g"][l]), dxm,
                                     name=f"in_dgrad_{l}", plan=plan)
        sent(keys, got)
        small["norm1_g"][l] = dg1[0]

    out = {k: jnp.stack(v) for k, v in small.items()}
    out["lb_logits"] = lb_vjp(jnp.stack(dlbs))[0]
    out["final_g"] = d_final_g[0]
    return loss_cols, dx, out, big32, recv


def _elementwise_rows(R, n, n_arrays):
    if 2 * n_arrays * R * n * 4 <= VMEM_LIMIT // 4 or R % 8:
        return R
    block = VMEM_LIMIT // 2 // (2 * n_arrays)
    want = 8
    while want * 2 * n * 4 <= block:
        want *= 2
    return _row_tile(R, want)


def _sum_layers(own, got, chip, *, name):
    L = len(own)
    _, r, n = own[0].shape
    T = _elementwise_rows(r, n, 6)
    nt = r // T

    def body(chip_ref, *refs):
        o_ref = refs[-1]
        l = pl.program_id(0)
        for k in range(L):
            @pl.when(l == k)
            def _():
                own_ref, got_ref = refs[2 * k], refs[2 * k + 1]
                acc = own_ref[...]
                for j in range(3):
                    acc = acc + got_ref[j].astype(F32)
                o_ref[...] = acc

    in_specs = []
    for k in range(L):
        hold = 0 if k else nt - 1
        in_specs.append(pl.BlockSpec((None, T, n), lambda l, i, c, k=k, hold=hold: (c[0], jnp.where(l == k, i, hold), 0)))
        in_specs.append(pl.BlockSpec((3, T, n), lambda l, i, c, k=k, hold=hold: (0, jnp.where(l == k, i, hold), 0)))
    grid_spec = pltpu.PrefetchScalarGridSpec(
        num_scalar_prefetch=1, grid=(L, nt), in_specs=in_specs,
        out_specs=pl.BlockSpec((None, T, n), lambda l, i, c: (l, i, 0)))
    args = [a for pair in zip(own, got) for a in pair]
    return pl.pallas_call(
        body, name=name, grid_spec=grid_spec, out_shape=jax.ShapeDtypeStruct((L, r, n), F32),
        compiler_params=pltpu.CompilerParams(dimension_semantics=("arbitrary", "arbitrary"),
                                             vmem_limit_bytes=VMEM_LIMIT),
    )(chip, *args)


def _sum_stack(first, rest, *, name):
    R, n = first.shape
    K = rest.shape[0]
    T = _elementwise_rows(R, n, K + 2)

    def body(a_ref, r_ref, o_ref):
        acc = a_ref[...]
        for j in range(K):
            acc = acc + r_ref[j].astype(F32)
        o_ref[...] = acc

    return _call(
        body, name=name, grid=(R // T,), parallel=(0,),
        in_specs=[pl.BlockSpec((T, n), lambda i: (i, 0)), pl.BlockSpec((K, T, n), lambda i: (0, i, 0))],
        out_specs=[pl.BlockSpec((T, n), lambda i: (i, 0))],
        out_shape=[jax.ShapeDtypeStruct((R, n), F32)],
        args=(first, rest))[0][0]


def _adamw(w, m, v, g_parts, *, name):
    R, n = w.shape
    n_g = len(g_parts)
    T = _elementwise_rows(R, n, 7 + n_g)

    def body(*refs):
        w_ref, m_ref, v_ref = refs[:3]
        g_refs = refs[3:3 + n_g]
        go_ref, d_ref, mo_ref, vo_ref = refs[3 + n_g:]
        g_ = g_refs[0][...]
        for r in g_refs[1:]:
            g_ = g_ + r[...]
        m_ = ADAM_B1 * m_ref[...] + (1.0 - ADAM_B1) * g_
        v_ = ADAM_B2 * v_ref[...] + (1.0 - ADAM_B2) * (g_ * g_)
        m_hat = m_ / (1.0 - ADAM_B1 ** ADAM_STEP)
        v_hat = v_ / (1.0 - ADAM_B2 ** ADAM_STEP)
        go_ref[...] = g_
        d_ref[...] = -ADAM_LR * (m_hat / (jnp.sqrt(v_hat) + ADAM_EPS) + ADAM_WD * w_ref[...])
        mo_ref[...] = m_
        vo_ref[...] = v_

    blk = pl.BlockSpec((T, n), lambda i: (i, 0))
    return _call(
        body, name=name, grid=(R // T,), parallel=(0,),
        in_specs=[blk] * (3 + n_g), out_specs=[blk] * 4,
        out_shape=[jax.ShapeDtypeStruct((R, n), F32)] * 4,
        args=(w, m, v, *g_parts))[0]


PACK_ALIGN = 8 * LANES


def _pack(pieces):
    flat = []
    for a in pieces:
        a = a.reshape(-1)
        pad = (-a.shape[0]) % PACK_ALIGN
        flat.append(jnp.pad(a, (0, pad)) if pad else a)
    return jnp.concatenate(flat).reshape(-1, LANES)


def _unpack(buf, shapes):
    flat = buf.reshape(-1)
    out, off = [], 0
    for shp in shapes:
        size = 1
        for s in shp:
            size *= s
        out.append(flat[off:off + size].reshape(shp))
        off += size + (-size) % PACK_ALIGN
    return out


def kernel(x, norm1_g, w_in, b_gate, pool_w, pool_scale, lb_logits, hgrn_norm_g, w_pa, w_pb, w_o, norm2_g, w_up, conv_w, conv_b, w_down, final_g, loss_target, m_norm1_g, m_w_in, m_b_gate, m_pool_w, m_pool_scale, m_lb_logits, m_hgrn_norm_g, m_w_pa, m_w_pb, m_w_o, m_norm2_g, m_w_up, m_conv_w, m_conv_b, m_w_down, m_final_g, v_norm1_g, v_w_in, v_b_gate, v_pool_w, v_pool_scale, v_lb_logits, v_hgrn_norm_g, v_w_pa, v_w_pb, v_w_o, v_norm2_g, v_w_up, v_conv_w, v_conv_b, v_w_down, v_final_g):
    env = dict(locals())
    w = {n: env[n] for n in WEIGHTS}
    m = {n: env["m_" + n] for n in WEIGHTS}
    v = {n: env["v_" + n] for n in WEIGHTS}
    my_chip = 2 * lax.axis_index("x") + lax.axis_index("y")
    L = w_in.shape[0]

    shards = {n: w[n].astype(BF16) for n in BIG}
    shards["conv_w"] = w["conv_w"]
    sm = {n: w[n] for n in SMALL}
    loss_cols, grad_x, g_small, big32, recv = _step(x[0], loss_target[0], sm, {}, shards)

    chip = my_chip.reshape(1).astype(jnp.int32)
    sums = [_sum_layers([big32[(n, l)] for l in range(L)], [recv[(n, l)] for l in range(L)], chip,
                        name="chip_sum_" + n) for n in BIG]
    small_names = list(SMALL)
    small_pieces = [g_small[n] for n in small_names] + [g_small["conv_w"], loss_cols]
    small_shapes = [a.shape for a in small_pieces]
    packed = _pack(small_pieces)
    Rs = packed.shape[0]
    swapped = _run_plan(_Together([_SiblingPlan(sums), _EveryonePlan(packed)]), name="tail_exchange")
    theirs, everyone = swapped[:-1], swapped[-1].reshape(8, Rs, LANES)
    g, delta, new_m, new_v = {}, {}, {}, {}
    for n, mine, other in zip(BIG, sums, theirs):
        shp = w[n].shape
        two_d = lambda a: a.reshape(-1, shp[-1])
        outs = _adamw(two_d(w[n]), two_d(m[n]), two_d(v[n]), [two_d(mine), two_d(other)], name="adamw_" + n)
        g[n], delta[n], new_m[n], new_v[n] = [a.reshape(shp) for a in outs]

    summed = _unpack(_sum_stack(everyone[0], everyone[1:], name="small_sum"), small_shapes)
    loss = jnp.sum(summed[-1])
    cshard = w["conv_w"].shape[2]
    gs = dict(zip(small_names, summed[:len(small_names)]))
    g_cw = lax.dynamic_slice_in_dim(summed[-2], my_chip * cshard, cshard, axis=2)

    sm_out = _adamw(_pack([w[n] for n in small_names]), _pack([m[n] for n in small_names]),
                    _pack([v[n] for n in small_names]), [_pack([gs[n] for n in small_names])], name="adamw_small")
    shapes = [w[n].shape for n in small_names]
    for n, g_, d_, m_, v_ in zip(small_names, *[_unpack(a, shapes) for a in sm_out]):
        g[n], delta[n], new_m[n], new_v[n] = g_, d_, m_, v_
    shp = w["conv_w"].shape
    two_d = lambda a: a.reshape(-1, shp[-1])
    outs = _adamw(two_d(w["conv_w"]), two_d(m["conv_w"]), two_d(v["conv_w"]), [two_d(g_cw)], name="adamw_conv_w")
    g["conv_w"], delta["conv_w"], new_m["conv_w"], new_v["conv_w"] = [a.reshape(shp) for a in outs]

    return (loss, grad_x[None], *[g[n] for n in WEIGHTS], *[delta[n] for n in WEIGHTS],
            *[new_m[n] for n in WEIGHTS], *[new_v[n] for n in WEIGHTS])
```

```python
import jax
import jax.numpy as jnp
from jax import lax
from jax.experimental import pallas as pl
from jax.experimental.pallas import tpu as pltpu

F32 = jnp.float32
BF16 = jnp.bfloat16

EPS = 1e-6
CHUNK = 64
SUB = 32
LANES = 128
SUBLANES = 8
POOL_WINDOWS = (2, 4, 8, 16)
HALO_POOL = 16
EXP_CLAMP = 80.0

ADAM_LR = 0.001
ADAM_B1 = 0.9
ADAM_B2 = 0.999
ADAM_EPS = 1e-08
ADAM_WD = 0.01
ADAM_STEP = 10

VMEM_LIMIT = 56 * 1024 * 1024
MESH_ID = pl.DeviceIdType.MESH
N_CHIPS = 4
ANY = pl.BlockSpec(memory_space=pl.ANY)


def _dot(a, b):
    return jnp.dot(a, b, preferred_element_type=F32)


def _dot_nt(a, b):
    return lax.dot_general(a, b, (((1,), (1,)), ((), ())), preferred_element_type=F32)


def _dot_tn(a, b):
    return lax.dot_general(a, b, (((0,), (0,)), ((), ())), preferred_element_type=F32)


def _sigmoid(x):
    return jax.nn.sigmoid(x)


def _dsilu(x, s):
    return s * (1.0 + x * (1.0 - s))


def _row_tile(rows, want):
    t = min(rows, want)
    while rows % t:
        t //= 2
    return t


def _place():
    x, y, c = lax.axis_index("x"), lax.axis_index("y"), lax.axis_index("c")
    chips = [(1 - x, y), (x, 1 - y), (1 - x, 1 - y)]
    return x, y, c, chips


def _remote(src, dst, sems, k, to):
    return pltpu.make_async_remote_copy(src_ref=src, dst_ref=dst, send_sem=sems[0].at[k], recv_sem=sems[1].at[k],
                                        device_id=to, device_id_type=MESH_ID)


class _GatherPlan:
    def __init__(self, items):
        self.items = items
        self.inputs = [a for a, _, _ in items]
        self.out_shapes = []
        for a, kind, _ in items:
            shp = (N_CHIPS,) + a.shape[1:] if kind == "rows" else (a.shape[0], N_CHIPS) + a.shape[1:]
            self.out_shapes.append(jax.ShapeDtypeStruct(shp, a.dtype))
        n = len(items)
        self.scratch = [pltpu.SemaphoreType.DMA((6 * n,)), pltpu.SemaphoreType.DMA((6 * n,)),
                        pltpu.SemaphoreType.DMA((2 * n,))]

    def _views(self, i, src, dst):
        _, kind, l = self.items[i]
        if kind == "rows":
            half = src.shape[1] // 2
            part = lambda core: src.at[l, pl.ds(core * half, half), :]
            land = lambda chip, core: dst.at[chip, pl.ds(core * half, half), :]
        else:
            part = lambda core: src.at[core]
            land = lambda chip, core: dst.at[core, chip]
        return part, land

    def start(self, srcs, dsts, sems):
        x, y, c, chips = _place()
        me = 2 * x + y
        for i, (src, dst) in enumerate(zip(srcs, dsts)):
            part, land = self._views(i, src, dst)
            for core in range(2):
                pltpu.make_async_copy(part(core), land(me, core), sems[2].at[2 * i + core]).start()
            for j, (px, py) in enumerate(chips):
                _remote(part(c), land(me, c), sems, 6 * i + j, (px, py, c)).start()

    def finish(self, srcs, dsts, sems):
        x, y, c, chips = _place()
        me = 2 * x + y
        sibling = (x, y, 1 - c)
        for i, (src, dst) in enumerate(zip(srcs, dsts)):
            part, land = self._views(i, src, dst)
            for j, (px, py) in enumerate(chips):
                got = land(2 * px + py, c)
                _remote(got, got, sems, 6 * i + j, (px, py, c)).wait_recv()
                _remote(got, got, sems, 6 * i + 3 + j, sibling).start()
        for i, (src, dst) in enumerate(zip(srcs, dsts)):
            part, land = self._views(i, src, dst)
            for j, (px, py) in enumerate(chips):
                got = land(2 * px + py, 1 - c)
                _remote(got, got, sems, 6 * i + 3 + j, sibling).wait_recv()
            for j, (px, py) in enumerate(chips):
                _remote(part(c), land(me, c), sems, 6 * i + j, (px, py, c)).wait_send()
                mine = land(2 * px + py, c)
                _remote(mine, mine, sems, 6 * i + 3 + j, sibling).wait_send()
            for core in range(2):
                pltpu.make_async_copy(part(core), land(me, core), sems[2].at[2 * i + core]).wait()


class _ScatterPlan:
    def __init__(self, items):
        self.inputs = list(items)
        self.out_shapes = [jax.ShapeDtypeStruct((3,) + a.shape[1:], a.dtype) for a in items]
        n = len(items)
        self.scratch = [pltpu.SemaphoreType.DMA((3 * n,)), pltpu.SemaphoreType.DMA((3 * n,))]

    def _copies(self, srcs, dsts, sems):
        x, y, c, chips = _place()
        return [_remote(src.at[2 * px + py], dst.at[j], sems, 3 * i + j, (px, py, c))
                for i, (src, dst) in enumerate(zip(srcs, dsts)) for j, (px, py) in enumerate(chips)]

    def start(self, srcs, dsts, sems):
        for cp in self._copies(srcs, dsts, sems):
            cp.start()

    def finish(self, srcs, dsts, sems):
        copies = self._copies(srcs, dsts, sems)
        for cp in copies:
            cp.wait_recv()
        for cp in copies:
            cp.wait_send()


class _SiblingPlan:
    def __init__(self, items):
        self.inputs = list(items)
        self.out_shapes = [jax.ShapeDtypeStruct(a.shape, a.dtype) for a in items]
        n = len(items)
        self.scratch = [pltpu.SemaphoreType.DMA((n,)), pltpu.SemaphoreType.DMA((n,))]

    def _copies(self, srcs, dsts, sems):
        x, y, c, _ = _place()
        return [_remote(src, dst, sems, i, (x, y, 1 - c)) for i, (src, dst) in enumerate(zip(srcs, dsts))]

    def start(self, srcs, dsts, sems):
        for cp in self._copies(srcs, dsts, sems):
            cp.start()

    def finish(self, srcs, dsts, sems):
        copies = self._copies(srcs, dsts, sems)
        for cp in copies:
            cp.wait_recv()
        for cp in copies:
            cp.wait_send()


class _EveryonePlan:
    def __init__(self, block):
        self.inputs = [block]
        self.m = block.shape[0]
        self.out_shapes = [jax.ShapeDtypeStruct((8 * self.m,) + block.shape[1:], block.dtype)]
        self.scratch = [pltpu.SemaphoreType.DMA((7,)), pltpu.SemaphoreType.DMA((7,)), pltpu.SemaphoreType.DMA((1,))]

    def _rows(self, dst, px, py, pc):
        return dst.at[pl.ds((4 * px + 2 * py + pc) * self.m, self.m), :]

    def start(self, srcs, dsts, sems):
        x, y, c, chips = _place()
        src, dst = srcs[0], dsts[0]
        pltpu.make_async_copy(src, self._rows(dst, x, y, c), sems[2].at[0]).start()
        _remote(src, self._rows(dst, x, y, c), sems, 0, (x, y, 1 - c)).start()
        for j, (px, py) in enumerate(chips):
            _remote(src, self._rows(dst, x, y, c), sems, 1 + j, (px, py, c)).start()

    def finish(self, srcs, dsts, sems):
        x, y, c, chips = _place()
        src, dst = srcs[0], dsts[0]
        sibling = (x, y, 1 - c)
        for j, (px, py) in enumerate(chips):
            got = self._rows(dst, px, py, c)
            _remote(got, got, sems, 1 + j, (px, py, c)).wait_recv()
            _remote(got, got, sems, 4 + j, sibling).start()
        sib = self._rows(dst, x, y, 1 - c)
        _remote(sib, sib, sems, 0, sibling).wait_recv()
        for j, (px, py) in enumerate(chips):
            got = self._rows(dst, px, py, 1 - c)
            _remote(got, got, sems, 4 + j, sibling).wait_recv()
        mine = self._rows(dst, x, y, c)
        _remote(src, mine, sems, 0, sibling).wait_send()
        for j, (px, py) in enumerate(chips):
            _remote(src, mine, sems, 1 + j, (px, py, c)).wait_send()
            got = self._rows(dst, px, py, c)
            _remote(got, got, sems, 4 + j, sibling).wait_send()
        pltpu.make_async_copy(src, mine, sems[2].at[0]).wait()


def _call(body, *, name, grid, in_specs, out_specs, out_shape, args, scratch=(), parallel=(), plan=None):
    n_in, n_out, n_scr = len(in_specs), len(out_shape), len(scratch)
    sem = tuple("parallel" if (a in parallel and plan is None) else "arbitrary" for a in range(len(grid)))
    params = pltpu.CompilerParams(dimension_semantics=sem, vmem_limit_bytes=VMEM_LIMIT)
    if plan is None:
        outs = pl.pallas_call(body, name=name, grid=grid, in_specs=in_specs, out_specs=out_specs,
                              out_shape=out_shape, scratch_shapes=list(scratch), compiler_params=params)(*args)
        return list(outs), []
    p_in, p_out, p_scr = len(plan.inputs), len(plan.out_shapes), len(plan.scratch)

    def wrapped(*refs):
        ins, refs = refs[:n_in], refs[n_in:]
        p_ins, refs = refs[:p_in], refs[p_in:]
        outs, refs = refs[:n_out], refs[n_out:]
        p_outs, refs = refs[:p_out], refs[p_out:]
        scr, p_sems = refs[:n_scr], refs[n_scr:]
        ids = [pl.program_id(a) for a in range(len(grid))]
        first = _all([i == 0 for i in ids])
        last = _all([i == n - 1 for i, n in zip(ids, grid)])

        @pl.when(first)
        def _():
            plan.start(p_ins, p_outs, p_sems)

        body(*ins, *outs, *scr)

        @pl.when(last)
        def _():
            plan.finish(p_ins, p_outs, p_sems)

    outs = pl.pallas_call(
        wrapped, name=name, grid=grid,
        in_specs=list(in_specs) + [ANY] * p_in, out_specs=list(out_specs) + [ANY] * p_out,
        out_shape=list(out_shape) + list(plan.out_shapes),
        scratch_shapes=list(scratch) + list(plan.scratch), compiler_params=params,
    )(*args, *plan.inputs)
    return list(outs[:n_out]), list(outs[n_out:])


def _all(conds):
    out = conds[0]
    for c in conds[1:]:
        out = out & c
    return out


class _Together:
    def __init__(self, plans):
        self.plans = plans
        self.inputs = [a for p in plans for a in p.inputs]
        self.out_shapes = [s for p in plans for s in p.out_shapes]
        self.scratch = [s for p in plans for s in p.scratch]

    def _split(self, refs, count):
        out, at = [], 0
        for p in self.plans:
            out.append(refs[at:at + count(p)])
            at += count(p)
        return out

    def _parts(self, srcs, dsts, sems):
        return zip(self.plans, self._split(srcs, lambda p: len(p.inputs)),
                   self._split(dsts, lambda p: len(p.out_shapes)), self._split(sems, lambda p: len(p.scratch)))

    def start(self, srcs, dsts, sems):
        for p, s, d, m in self._parts(srcs, dsts, sems):
            p.start(s, d, m)

    def finish(self, srcs, dsts, sems):
        for p, s, d, m in self._parts(srcs, dsts, sems):
            p.finish(s, d, m)


def _run_plan(plan, *, name):
    p_in, p_out = len(plan.inputs), len(plan.out_shapes)

    def body(*refs):
        srcs, dsts, sems = refs[:p_in], refs[p_in:p_in + p_out], refs[p_in + p_out:]
        plan.start(srcs, dsts, sems)
        plan.finish(srcs, dsts, sems)

    return list(pl.pallas_call(body, name=name, in_specs=[ANY] * p_in, out_specs=[ANY] * p_out,
                               out_shape=list(plan.out_shapes), scratch_shapes=list(plan.scratch))(*plan.inputs))


def _rms(xf, g):
    r = lax.rsqrt(jnp.mean(xf * xf, axis=-1, keepdims=True) + EPS)
    return (xf * r * g).astype(BF16)


def _rmsnorm(x, g, *, name, plan=None):
    S, D = x.shape
    tm = _row_tile(S, 1024)

    def body(x_ref, g_ref, xn_ref):
        xn_ref[...] = _rms(x_ref[...], g_ref[...])

    return _call(
        body, name=name, grid=(S // tm,), parallel=(0,), plan=plan,
        in_specs=[pl.BlockSpec((tm, D), lambda i: (i, 0)), pl.BlockSpec((1, D), lambda i: (0, 0))],
        out_specs=[pl.BlockSpec((tm, D), lambda i: (i, 0))],
        out_shape=[jax.ShapeDtypeStruct((S, D), BF16)],
        args=(x, g))


def _matmul(xn, w4, *, name, out_dtype=F32, plan=None):
    S, D = xn.shape
    n4 = w4.shape[2]
    tm = _row_tile(S, 2048)

    def body(xn_ref, w_ref, o_ref):
        o_ref[...] = _dot(xn_ref[...], w_ref[...]).astype(out_dtype)

    return _call(
        body, name=name, grid=(S // tm, N_CHIPS), parallel=(0,), plan=plan,
        in_specs=[pl.BlockSpec((tm, D), lambda i, j: (i, 0)),
                  pl.BlockSpec((None, D, n4), lambda i, j: (j, 0, 0))],
        out_specs=[pl.BlockSpec((tm, n4), lambda i, j: (i, j))],
        out_shape=[jax.ShapeDtypeStruct((S, N_CHIPS * n4), out_dtype)],
        args=(xn, w4))


def _segments(widths, n4):
    per_chip = [[] for _ in range(N_CHIPS)]
    c0 = 0
    for p, w in enumerate(widths):
        a = c0
        while a < c0 + w:
            k = a // n4
            b = min(c0 + w, (k + 1) * n4)
            per_chip[k].append((p, (a - c0, b - c0), (a - k * n4, b - k * n4)))
            a = b
        c0 += w
    assert c0 == N_CHIPS * n4
    return per_chip


def _piece_specs(pieces, n4, tm):
    per_chip = _segments([p.shape[1] for p in pieces], n4)
    specs, local, start = [], [[] for _ in range(N_CHIPS)], 0
    for p, arr in enumerate(pieces):
        chips = [k for k in range(N_CHIPS) if any(seg[0] == p for seg in per_chip[k])]
        lo, hi = chips[0], chips[-1]
        tiled = arr.shape[1] % n4 == 0 and start % n4 == 0
        start += arr.shape[1]
        if tiled:
            imap = lambda k, i, lo=lo, hi=hi: (jnp.where((k >= lo) & (k <= hi), i, 0), jnp.clip(k - lo, 0, hi - lo))
            specs.append(pl.BlockSpec((tm, n4), imap))
        else:
            imap = lambda k, i, lo=lo, hi=hi: (jnp.where((k >= lo) & (k <= hi), i, 0), 0)
            specs.append(pl.BlockSpec((tm, arr.shape[1]), imap))
        for k in chips:
            for q, (pa, pb), cols in per_chip[k]:
                if q == p:
                    local[k].append((p, (0, n4) if tiled else (pa, pb), cols))
    return specs, local


def _dgrad_norm(dys, w4, x, g, dres, *, name, plan=None):
    S, D = x.shape
    n4 = w4.shape[2]
    tm = _row_tile(S, 512)
    per_chip = _segments([a.shape[1] for a in dys], n4)
    n_p = len(dys)

    def body(*refs):
        dy_refs = refs[:n_p]
        w_ref, x_ref, g_ref, dres_ref, dx_ref, dg_ref = refs[n_p:]

        @pl.when(pl.program_id(0) == 0)
        def _():
            dg_ref[...] = jnp.zeros_like(dg_ref)

        dxn = None
        for k in range(N_CHIPS):
            for p, (pa, pb), (ca, cb) in per_chip[k]:
                part = _dot_nt(dy_refs[p][:, pa:pb], w_ref[k, :, ca:cb])
                dxn = part if dxn is None else dxn + part
        xf = x_ref[...]
        r = lax.rsqrt(jnp.mean(xf * xf, axis=-1, keepdims=True) + EPS)
        xhat = xf * r
        dxhat = dxn * g_ref[...]
        dx_ref[...] = dres_ref[...] + r * (dxhat - xhat * jnp.mean(dxhat * xhat, axis=-1, keepdims=True))
        dg_ref[...] += jnp.sum(dxn * xhat, axis=0, keepdims=True)

    row = lambda w: pl.BlockSpec((tm, w), lambda i: (i, 0))
    return _call(
        body, name=name, grid=(S // tm,), plan=plan,
        in_specs=[row(a.shape[1]) for a in dys]
        + [pl.BlockSpec(w4.shape, lambda i: (0, 0, 0), pipeline_mode=pl.Buffered(1)),
           row(D), pl.BlockSpec((1, D), lambda i: (0, 0)), row(D)],
        out_specs=[row(D), pl.BlockSpec((1, D), lambda i: (0, 0))],
        out_shape=[jax.ShapeDtypeStruct((S, D), F32), jax.ShapeDtypeStruct((1, D), F32)],
        args=(*dys, w4, x, g, dres))


def _wgrad(a, dys, *, name, rows, plan=None):
    S, K = a.shape
    n4 = sum(p.shape[1] for p in dys) // N_CHIPS
    tm = _row_tile(S, rows)
    ns = S // tm
    specs, local = _piece_specs(dys, n4, tm)
    n_p = len(dys)

    def body(*refs):
        a_ref = refs[0]
        dy_refs = refs[1:1 + n_p]
        o_ref, o16_ref = refs[1 + n_p:]
        n, s = pl.program_id(0), pl.program_id(1)

        @pl.when(s == 0)
        def _():
            o_ref[...] = jnp.zeros_like(o_ref)

        for k in range(N_CHIPS):
            @pl.when(n == k)
            def _():
                av = a_ref[...]
                for p, (pa, pb), (ca, cb) in local[k]:
                    o_ref[:, ca:cb] += _dot_tn(av, dy_refs[p][:, pa:pb])

        @pl.when(s == ns - 1)
        def _():
            o16_ref[...] = o_ref[...].astype(BF16)

    out = pl.BlockSpec((None, K, n4), lambda n, s: (n, 0, 0))
    return _call(
        body, name=name, grid=(N_CHIPS, ns), parallel=(0,), plan=plan,
        in_specs=[pl.BlockSpec((tm, K), lambda n, s: (s, 0))] + specs,
        out_specs=[out, out],
        out_shape=[jax.ShapeDtypeStruct((N_CHIPS, K, n4), F32), jax.ShapeDtypeStruct((N_CHIPS, K, n4), BF16)],
        args=(a, *dys))


def _tiles(x):
    return x.reshape(x.shape[0] // SUBLANES, SUBLANES, x.shape[1])


def _shift_down(xp, s):
    n = xp.shape[0] - SUBLANES
    if s == SUBLANES:
        return xp[:n, :]
    t = _tiles(xp)
    rot = pltpu.roll(t, s, 1)
    sub = lax.broadcasted_iota(jnp.int32, t.shape, 1)[1:]
    return jnp.where(sub >= s, rot[1:], rot[:-1]).reshape(n, xp.shape[1])


def _shift_up(xn, s):
    n = xn.shape[0] - SUBLANES
    if s == SUBLANES:
        return xn[SUBLANES:, :]
    t = _tiles(xn)
    rot = pltpu.roll(t, SUBLANES - s, 1)
    sub = lax.broadcasted_iota(jnp.int32, t.shape, 1)[1:]
    return jnp.where(sub < SUBLANES - s, rot[:-1], rot[1:]).reshape(n, xn.shape[1])


def _pooled(u, halo, first_tile, row0):
    T = u.shape[0]
    halo = jnp.where(first_tile, 0.0, halo)
    pad = jnp.zeros((SUBLANES, u.shape[1]), F32)
    up = jnp.concatenate([pad, halo, u], axis=0)
    t1 = (row0 + lax.broadcasted_iota(jnp.int32, (T, 1), 0) + 1).astype(F32)
    outs = []
    for gi, w in enumerate(POOL_WINDOWS):
        s = up[:, gi * LANES:(gi + 1) * LANES]
        k = 1
        while k < w:
            if k < SUBLANES:
                s = jnp.concatenate([s[:SUBLANES, :], s[SUBLANES:, :] + _shift_down(s, k)], axis=0)
            else:
                s = s[SUBLANES:, :] + _shift_down(s, k)
            k *= 2
        s = s[-T:, :]
        inv = 1.0 / jnp.minimum(t1, float(w))
        outs.append(s * inv - u[:, gi * LANES:(gi + 1) * LANES])
    return outs


def _pool_mix(u, halo, first_tile, row0, pw_ref, ps_ref):
    pooled = _pooled(u, halo, first_tile, row0)
    outs = []
    for gi in range(len(POOL_WINDOWS)):
        mixed = _dot(pooled[gi].astype(BF16), pw_ref[gi].astype(BF16))
        outs.append((mixed * ps_ref[:, gi * LANES:(gi + 1) * LANES]).astype(BF16))
    return jnp.concatenate(outs, axis=1)


def _pool_bwd(z, dpm, pool_w, pool_scale, *, name):
    S, P = dpm.shape
    T = _row_tile(S, 512)
    hb = T // HALO_POOL
    nt = S // T

    def body(u_ref, halo_ref, d_ref, dnext_ref, pw_ref, ps_ref, du_ref, dpw_ref, dps_ref):
        i = pl.program_id(0)

        @pl.when(i == 0)
        def _():
            dpw_ref[...] = jnp.zeros_like(dpw_ref)
            dps_ref[...] = jnp.zeros_like(dps_ref)

        pooled = _pooled(u_ref[...], halo_ref[...], i == 0, i * T)
        dnext = jnp.where(i == nt - 1, 0.0, dnext_ref[...])
        pad = jnp.zeros((SUBLANES, P), F32)
        dext = jnp.concatenate([d_ref[...], dnext, pad], axis=0)
        t1 = (i * T + lax.broadcasted_iota(jnp.int32, (T + HALO_POOL + SUBLANES, 1), 0) + 1).astype(F32)
        for gi, w in enumerate(POOL_WINDOWS):
            cols = slice(gi * LANES, (gi + 1) * LANES)
            pw = pw_ref[gi].astype(BF16)
            pg = pooled[gi].astype(BF16)
            mixed = _dot(pg, pw)
            dps_ref[:, cols] += jnp.sum(d_ref[:, cols] * mixed, axis=0, keepdims=True)
            dmixed = (dext[:, cols] * ps_ref[:, cols]).astype(BF16)
            dpw_ref[gi] += _dot_tn(pg, dmixed[:T, :])
            dpooled = _dot_nt(dmixed, pw)
            e = dpooled * (1.0 / jnp.minimum(t1, float(w)))
            k = 1
            while k < w:
                if k < SUBLANES:
                    e = jnp.concatenate([e[:-SUBLANES, :] + _shift_up(e, k), e[-SUBLANES:, :]], axis=0)
                else:
                    e = e[:-SUBLANES, :] + _shift_up(e, k)
                k *= 2
            du_ref[:, cols] = (e[:T, :] - dpooled[:T, :]).astype(BF16)

    return _call(
        body, name=name, grid=(nt,),
        in_specs=[pl.BlockSpec((T, P), lambda i: (i, 0)),
                  pl.BlockSpec((HALO_POOL, P), lambda i: (jnp.maximum(i * hb - 1, 0), 0)),
                  pl.BlockSpec((T, P), lambda i: (i, 0)),
                  pl.BlockSpec((HALO_POOL, P), lambda i: (jnp.minimum((i + 1) * hb, S // HALO_POOL - 1), 0)),
                  pl.BlockSpec(pool_w.shape, lambda i: (0, 0, 0)),
                  pl.BlockSpec((1, P), lambda i: (0, 0))],
        out_specs=[pl.BlockSpec((T, P), lambda i: (i, 0)),
                   pl.BlockSpec(pool_w.shape, lambda i: (0, 0, 0)),
                   pl.BlockSpec((1, P), lambda i: (0, 0))],
        out_shape=[jax.ShapeDtypeStruct((S, P), BF16),
                   jax.ShapeDtypeStruct(pool_w.shape, F32),
                   jax.ShapeDtypeStruct((1, P), F32)],
        args=(z, z, dpm, dpm, pool_w, pool_scale))[0]


def _cumsum_rows(x):
    n = x.shape[0]
    row = lax.broadcasted_iota(jnp.int32, x.shape, 0)
    s = 1
    while s < n:
        x = x + jnp.where(row >= s, pltpu.roll(x, s, 0), 0.0)
        s *= 2
    return x


def _rev_cumsum_rows(x):
    n = x.shape[0]
    row = lax.broadcasted_iota(jnp.int32, x.shape, 0)
    s = 1
    while s < n:
        x = x + jnp.where(row < n - s, pltpu.roll(x, n - s, 0), 0.0)
        s *= 2
    return x


def _chunk_prep(zq, zf, lb, b_ref):
    n_sub = CHUNK // SUB
    sq = _sigmoid(zq)
    q = zq * sq
    sf = _sigmoid(zf)
    f = lb + (1.0 - lb) * sf
    k = 1.0 - f
    b = _cumsum_rows(jnp.log(f))
    b_ref[...] = b
    shape = (SUB, b.shape[1])
    ends = [jnp.broadcast_to(b_ref[pl.ds(SUB * j + SUB - 1, 1), :], shape) for j in range(n_sub)]
    mids = [jnp.broadcast_to(b_ref[pl.ds(SUB * j + SUB // 2 - 1, 1), :], shape) for j in range(n_sub)]
    own = [b[SUB * j:SUB * (j + 1), :] for j in range(n_sub)]
    m0 = jnp.concatenate(mids, axis=0)
    e1 = jnp.concatenate(ends, axis=0)
    eq = [jnp.exp(jnp.minimum(b - m0, EXP_CLAMP))]
    for d in range(1, n_sub):
        rd = jnp.concatenate([own[j] if j < d else ends[j - d] for j in range(n_sub)], axis=0)
        eq.append(jnp.exp(b - rd))
    ek0 = jnp.exp(jnp.minimum(m0 - b, EXP_CLAMP))
    ek1 = jnp.exp(e1 - b)
    b_last = b_ref[pl.ds(CHUNK - 1, 1), :]
    return dict(q=q, k=k, f=f, sq=sq, sf=sf, b=b, eq=eq, ek0=ek0, ek1=ek1,
                eb=jnp.exp(b), ekl=jnp.exp(b_last - b), el=jnp.exp(b_last))


def _chunk_masks():
    ti = lax.broadcasted_iota(jnp.int32, (CHUNK, CHUNK), 0)
    si = lax.broadcasted_iota(jnp.int32, (CHUNK, CHUNK), 1)
    shift = SUB.bit_length() - 1
    dsub = jnp.right_shift(ti, shift) - jnp.right_shift(si, shift)
    masks = [(dsub == 0) & (si <= ti)]
    masks += [dsub == d for d in range(1, CHUNK // SUB)]
    return masks


def _chunk_attn(p, masks):
    qd = [(p["q"] * e).astype(BF16) for e in p["eq"]]
    k0 = (p["k"] * p["ek0"]).astype(BF16)
    k1 = (p["k"] * p["ek1"]).astype(BF16)
    a = jnp.where(masks[0], _dot_nt(qd[0], k0), 0.0)
    for d in range(1, len(masks)):
        a = jnp.where(masks[d], _dot_nt(qd[d], k1), a)
    return a, qd, k0, k1


def _hgrn_fwd(z, lb, norm_g, *, name, plan=None):
    S = z.shape[0]
    HW = lb.shape[1]
    NH = HW // LANES
    T = _row_tile(S, 512)
    nc = T // CHUNK

    def body(zq_ref, zf_ref, zi_ref, zo_ref, lb_ref, ng_ref, o_ref, of_ref, st_ref, s_scr, b_scr):
        @pl.when(pl.program_id(0) == 0)
        def _():
            s_scr[...] = jnp.zeros_like(s_scr)

        ng = ng_ref[...]
        masks = _chunk_masks()

        def chunk(c, carry):
            rows = pl.ds(pl.multiple_of(c * CHUNK, CHUNK), CHUNK)
            for h in range(NH):
                cols = slice(h * LANES, (h + 1) * LANES)
                p = _chunk_prep(zq_ref[rows, cols], zf_ref[rows, cols], lb_ref[:, cols], b_scr.at[h])
                v = zi_ref[rows, cols].astype(BF16)
                zo = zo_ref[rows, cols]
                st = s_scr[h]
                st_ref[c, h] = st
                a, _, _, _ = _chunk_attn(p, masks)
                o = _dot(a.astype(BF16), v) + _dot_nt((p["q"] * p["eb"]).astype(BF16), st.astype(BF16))
                s_scr[h] = st * p["el"] + _dot_tn(v, (p["k"] * p["ekl"]).astype(BF16))
                o_ref[rows, cols] = o
                r = lax.rsqrt(jnp.mean(o * o, axis=-1, keepdims=True) + EPS)
                of_ref[rows, cols] = (o * r * ng * (zo * _sigmoid(zo))).astype(BF16)
            return carry

        lax.fori_loop(0, nc, chunk, 0, unroll=8)

    part = lambda k: pl.BlockSpec((T, HW), lambda i, k=k: (i, k))
    return _call(
        body, name=name, grid=(S // T,), plan=plan,
        in_specs=[part(1), part(2), part(3), part(4),
                  pl.BlockSpec((1, HW), lambda i: (0, 0)), pl.BlockSpec((1, LANES), lambda i: (0, 0))],
        out_specs=[pl.BlockSpec((T, HW), lambda i: (i, 0)), pl.BlockSpec((T, HW), lambda i: (i, 0)),
                   pl.BlockSpec((nc, NH, LANES, LANES), lambda i: (i, 0, 0, 0))],
        out_shape=[jax.ShapeDtypeStruct((S, HW), F32), jax.ShapeDtypeStruct((S, HW), BF16),
                   jax.ShapeDtypeStruct((S // CHUNK, NH, LANES, LANES), F32)],
        scratch=[pltpu.VMEM((NH, LANES, LANES), F32), pltpu.VMEM((NH, CHUNK, LANES), F32)],
        args=(z, z, z, z, lb, norm_g))


def _hgrn_bwd(z, lb, norm_g, o_raw, states, dof, *, name, plan=None):
    S = z.shape[0]
    HW = lb.shape[1]
    NH = HW // LANES
    T = _row_tile(S, 512)
    nc = T // CHUNK
    nt = S // T

    def body(zq_ref, zf_ref, zi_ref, zo_ref, lb_ref, ng_ref, o_ref, st_ref, dof_ref,
             dzq_ref, dzf_ref, dzi_ref, dzo_ref, dlb_ref, dng_ref, ds_scr, b_scr):
        @pl.when(pl.program_id(0) == 0)
        def _():
            ds_scr[...] = jnp.zeros_like(ds_scr)
            dlb_ref[...] = jnp.zeros_like(dlb_ref)
            dng_ref[...] = jnp.zeros_like(dng_ref)

        ng = ng_ref[...]
        masks = _chunk_masks()
        last_row = lax.broadcasted_iota(jnp.int32, (CHUNK, 1), 0) == CHUNK - 1

        def chunk(cr, carry):
            c = nc - 1 - cr
            rows = pl.ds(pl.multiple_of(c * CHUNK, CHUNK), CHUNK)
            for h in range(NH):
                cols = slice(h * LANES, (h + 1) * LANES)
                lbv = lb_ref[:, cols]
                zq, zf, zo = zq_ref[rows, cols], zf_ref[rows, cols], zo_ref[rows, cols]
                o = o_ref[rows, cols]
                dof_c = dof_ref[rows, cols]
                st = st_ref[c, h]
                dst = ds_scr[h]

                so = _sigmoid(zo)
                r = lax.rsqrt(jnp.mean(o * o, axis=-1, keepdims=True) + EPS)
                ohat = o * r
                d_on = dof_c * (zo * so)
                dzo_ref[rows, cols] = (dof_c * ohat * ng * _dsilu(zo, so)).astype(BF16)
                dng_ref[:, cols] += jnp.sum(d_on * ohat, axis=0, keepdims=True)
                dohat = d_on * ng
                do = (r * (dohat - ohat * jnp.mean(dohat * ohat, axis=-1, keepdims=True))).astype(BF16)

                p = _chunk_prep(zq, zf, lbv, b_scr.at[h])
                q, k = p["q"], p["k"]
                v = zi_ref[rows, cols].astype(BF16)
                a, qd, k0, k1 = _chunk_attn(p, masks)
                ktl = (k * p["ekl"]).astype(BF16)
                dstb = dst.astype(BF16)

                da = _dot_nt(do, v)
                dzi_ref[rows, cols] = (_dot_tn(a.astype(BF16), do) + _dot_nt(ktl, dstb)).astype(BF16)

                da0 = jnp.where(masks[0], da, 0.0).astype(BF16)
                rq = _dot(da0, k0)
                rk0 = _dot_tn(da0, qd[0])
                dq = rq * p["eq"][0]
                db = qd[0].astype(F32) * rq - k0.astype(F32) * rk0
                rk1 = jnp.zeros_like(rk0)
                for d in range(1, len(masks)):
                    dad = jnp.where(masks[d], da, 0.0).astype(BF16)
                    rq = _dot(dad, k1)
                    dq = dq + rq * p["eq"][d]
                    db = db + qd[d].astype(F32) * rq
                    rk1 = rk1 + _dot_tn(dad, qd[d])
                dk = rk0 * p["ek0"] + rk1 * p["ek1"]
                db = db - k1.astype(F32) * rk1
                qe = (q * p["eb"]).astype(BF16)
                rq = _dot(do, st.astype(BF16))
                dq = dq + rq * p["eb"]
                db = db + qe.astype(F32) * rq
                rk = _dot(v, dstb)
                dk = dk + rk * p["ekl"]
                db = db - ktl.astype(F32) * rk

                st_new = st * p["el"] + _dot_tn(v, ktl)
                db = db + jnp.where(last_row, jnp.sum(dstb.astype(F32) * st_new, axis=0, keepdims=True), 0.0)
                dg = _rev_cumsum_rows(db)
                ds_scr[h] = dst * p["el"] + _dot_tn(do, qe)

                dzq_ref[rows, cols] = (dq * _dsilu(zq, p["sq"])).astype(BF16)
                df = dg / p["f"] - dk
                sf = p["sf"]
                dzf_ref[rows, cols] = (df * (1.0 - lbv) * sf * (1.0 - sf)).astype(BF16)
                dlb_ref[:, cols] += jnp.sum(df * (1.0 - sf), axis=0, keepdims=True)
            return carry

        lax.fori_loop(0, nc, chunk, 0, unroll=4)

    rev = lambda i: nt - 1 - i
    part = lambda k: pl.BlockSpec((T, HW), lambda i, k=k: (rev(i), k))
    blk = pl.BlockSpec((T, HW), lambda i: (rev(i), 0))
    vec = pl.BlockSpec((1, HW), lambda i: (0, 0))
    return _call(
        body, name=name, grid=(nt,), plan=plan,
        in_specs=[part(1), part(2), part(3), part(4), vec, pl.BlockSpec((1, LANES), lambda i: (0, 0)),
                  blk, pl.BlockSpec((nc, NH, LANES, LANES), lambda i: (rev(i), 0, 0, 0)), blk],
        out_specs=[blk, blk, blk, blk, vec, vec],
        out_shape=[jax.ShapeDtypeStruct((S, HW), BF16)] * 4 + [jax.ShapeDtypeStruct((1, HW), F32)] * 2,
        scratch=[pltpu.VMEM((NH, LANES, LANES), F32), pltpu.VMEM((NH, CHUNK, LANES), F32)],
        args=(z, z, z, z, lb, norm_g, o_raw, states, dof))


def _gate_specs(T, D):
    half = D // 2
    first = (5 * half) // half
    return [pl.BlockSpec((T, half), lambda i, k=k: (i, first + k)) for k in range(4)]


def _gates(zg_refs, bg_ref, D):
    half = D // 2
    za = jnp.concatenate([zg_refs[0][...], zg_refs[1][...]], axis=1) + bg_ref[:, :D]
    zb = jnp.concatenate([zg_refs[2][...], zg_refs[3][...]], axis=1) + bg_ref[:, D:]
    return _sigmoid(za), _sigmoid(zb)


def _mix_fwd(x, of, z, pool_w, pool_scale, b_gate, w_pa4, w_pb4, w_o4, g_next, *, name, plan=None):
    S, D = x.shape
    P = of.shape[1]
    T = _row_tile(S, 512)
    hb = T // HALO_POOL

    def body(x_ref, u_ref, halo_ref, of_ref, g0, g1, g2, g3, pw_ref, ps_ref, bg_ref, wpa_ref, wpb_ref, wo_ref,
             gn_ref, xo_ref, ya_ref, yb_ref, xn_ref, pm_ref):
        i = pl.program_id(0)
        pmv = _pool_mix(u_ref[...], halo_ref[...], i == 0, i * T, pw_ref, ps_ref)
        pm_ref[...] = pmv
        ofv = of_ref[...]
        ya = jnp.concatenate([_dot(pmv, wpa_ref[k]) for k in range(N_CHIPS)], axis=1)
        yb = jnp.concatenate([_dot(ofv, wpb_ref[k]) for k in range(N_CHIPS)], axis=1)
        ga, gb = _gates((g0, g1, g2, g3), bg_ref, D)
        merged = (ga * ya + gb * yb).astype(BF16)
        x_mid = x_ref[...] + _dot(merged, wo_ref[...].reshape(D, D))
        xo_ref[...] = x_mid
        xn_ref[...] = _rms(x_mid, gn_ref[...])
        ya_ref[...] = ya.astype(BF16)
        yb_ref[...] = yb.astype(BF16)

    row = lambda w: pl.BlockSpec((T, w), lambda i: (i, 0))
    full = lambda a: pl.BlockSpec(a.shape, lambda i: (0,) * a.ndim)
    return _call(
        body, name=name, grid=(S // T,), parallel=(0,), plan=plan,
        in_specs=[row(D), row(P), pl.BlockSpec((HALO_POOL, P), lambda i: (jnp.maximum(i * hb - 1, 0), 0)), row(P)]
        + _gate_specs(T, D) + [full(pool_w), full(pool_scale), full(b_gate), full(w_pa4), full(w_pb4), full(w_o4),
                               full(g_next)],
        out_specs=[row(D), row(D), row(D), row(D), row(P)],
        out_shape=[jax.ShapeDtypeStruct((S, D), F32), jax.ShapeDtypeStruct((S, D), BF16),
                   jax.ShapeDtypeStruct((S, D), BF16), jax.ShapeDtypeStruct((S, D), BF16),
                   jax.ShapeDtypeStruct((S, P), BF16)],
        args=(x, z, z, of, z, z, z, z, pool_w, pool_scale, b_gate, w_pa4, w_pb4, w_o4, g_next))


def _mix_bwd(dxm, ya, yb, z, b_gate, pm, of, w_pa4, w_pb4, w_o4, *, name, plan=None):
    S, D = dxm.shape
    P = pm.shape[1]
    q4 = D // N_CHIPS
    T = _row_tile(S, 512)
    nt = S // T

    def body(dx_ref, ya_ref, yb_ref, g0, g1, g2, g3, bg_ref, pm_ref, of_ref, wpa_ref, wpb_ref, wo_ref,
             dzg_ref, dpm_ref, dof_ref, dwo_ref, dwpa_ref, dwpb_ref, dbg_ref, dwo16_ref, dwpa16_ref, dwpb16_ref):
        i = pl.program_id(0)

        @pl.when(i == 0)
        def _():
            dwo_ref[...] = jnp.zeros_like(dwo_ref)
            dwpa_ref[...] = jnp.zeros_like(dwpa_ref)
            dwpb_ref[...] = jnp.zeros_like(dwpb_ref)
            dbg_ref[...] = jnp.zeros_like(dbg_ref)

        dxb = dx_ref[...].astype(BF16)
        ya = ya_ref[...].astype(F32)
        yb = yb_ref[...].astype(F32)
        ga, gb = _gates((g0, g1, g2, g3), bg_ref, D)
        merged = (ga * ya + gb * yb).astype(BF16)
        dwo_ref[...] += _dot_tn(merged, dxb).reshape(N_CHIPS, q4, D)
        dm = _dot_nt(dxb, wo_ref[...].reshape(D, D))
        dza = dm * ya * ga * (1.0 - ga)
        dzb = dm * yb * gb * (1.0 - gb)
        dzg_ref[:, :D] = dza.astype(BF16)
        dzg_ref[:, D:] = dzb.astype(BF16)
        dbg_ref[:, :D] += jnp.sum(dza, axis=0, keepdims=True)
        dbg_ref[:, D:] += jnp.sum(dzb, axis=0, keepdims=True)
        dya = (dm * ga).astype(BF16)
        dyb = (dm * gb).astype(BF16)
        pmv, ofv = pm_ref[...], of_ref[...]
        dpm = jnp.zeros((T, P), F32)
        dof = jnp.zeros((T, P), F32)
        for k in range(N_CHIPS):
            cols = slice(k * q4, (k + 1) * q4)
            dwpa_ref[k] += _dot_tn(pmv, dya[:, cols])
            dwpb_ref[k] += _dot_tn(ofv, dyb[:, cols])
            dpm = dpm + _dot_nt(dya[:, cols], wpa_ref[k])
            dof = dof + _dot_nt(dyb[:, cols], wpb_ref[k])
        dpm_ref[...] = dpm
        dof_ref[...] = dof

        @pl.when(i == nt - 1)
        def _():
            dwo16_ref[...] = dwo_ref[...].astype(BF16)
            dwpa16_ref[...] = dwpa_ref[...].astype(BF16)
            dwpb16_ref[...] = dwpb_ref[...].astype(BF16)

    row = lambda w: pl.BlockSpec((T, w), lambda i: (i, 0))
    full = lambda a: pl.BlockSpec(a.shape, lambda i: (0,) * a.ndim, pipeline_mode=pl.Buffered(1))
    like = lambda a, dt: jax.ShapeDtypeStruct(a.shape, dt)
    return _call(
        body, name=name, grid=(nt,), plan=plan,
        in_specs=[row(D), row(D), row(D)] + _gate_specs(T, D) + [full(b_gate), row(P), row(P),
                                                                  full(w_pa4), full(w_pb4), full(w_o4)],
        out_specs=[row(2 * D), row(P), row(P), full(w_o4), full(w_pa4), full(w_pb4), full(b_gate),
                   full(w_o4), full(w_pa4), full(w_pb4)],
        out_shape=[jax.ShapeDtypeStruct((S, 2 * D), BF16), jax.ShapeDtypeStruct((S, P), F32),
                   jax.ShapeDtypeStruct((S, P), F32), like(w_o4, F32), like(w_pa4, F32), like(w_pb4, F32),
                   like(b_gate, F32), like(w_o4, BF16), like(w_pa4, BF16), like(w_pb4, BF16)],
        args=(dxm, ya, yb, z, z, z, z, b_gate, pm, of, w_pa4, w_pb4, w_o4))


def _up_conv(xn, w_up4, conv_w, conv_b, *, name, plan=None):
    S, D = xn.shape
    f4 = w_up4.shape[2]
    nf = N_CHIPS // 2
    F = nf * f4
    T = _row_tile(S, 512)

    def body(xn_ref, wv_ref, wg_ref, cwv_ref, cwg_ref, cbv_ref, cbg_ref,
             hv_ref, hg_ref, val_ref, gate_ref, a_ref, pv_scr, pg_scr):
        i = pl.program_id(1)
        xv = xn_ref[...]

        def side(w_ref, cw_ref, cb_ref, h_ref, p_scr):
            h = _dot(xv, w_ref[...])
            h_ref[...] = h.astype(BF16)
            hp = jnp.concatenate([jnp.where(i == 0, 0.0, p_scr[...]), h], axis=0)
            p_scr[...] = h[-SUBLANES:, :]
            cw = cw_ref[...]
            return cw[0:1, :] * _shift_down(hp, 2) + cw[1:2, :] * _shift_down(hp, 1) + cw[2:3, :] * h + cb_ref[...]

        val = side(wv_ref, cwv_ref, cbv_ref, hv_ref, pv_scr)
        gate = side(wg_ref, cwg_ref, cbg_ref, hg_ref, pg_scr)
        val_ref[...] = val.astype(BF16)
        gate_ref[...] = gate.astype(BF16)
        a_ref[...] = (gate * _sigmoid(gate) * val).astype(BF16)

    out = pl.BlockSpec((T, f4), lambda f, i: (i, f))
    return _call(
        body, name=name, grid=(nf, S // T), plan=plan,
        in_specs=[pl.BlockSpec((T, D), lambda f, i: (i, 0)),
                  pl.BlockSpec((None, D, f4), lambda f, i: (f, 0, 0)),
                  pl.BlockSpec((None, D, f4), lambda f, i: (nf + f, 0, 0)),
                  pl.BlockSpec((3, f4), lambda f, i: (0, f)),
                  pl.BlockSpec((3, f4), lambda f, i: (0, nf + f)),
                  pl.BlockSpec((1, f4), lambda f, i: (0, f)),
                  pl.BlockSpec((1, f4), lambda f, i: (0, nf + f))],
        out_specs=[out] * 5,
        out_shape=[jax.ShapeDtypeStruct((S, F), BF16)] * 5,
        scratch=[pltpu.VMEM((SUBLANES, f4), F32), pltpu.VMEM((SUBLANES, f4), F32)],
        args=(xn, w_up4, w_up4, conv_w, conv_w, conv_b, conv_b))


def _down(a, w_down4, x, g_next, *, name, plan=None):
    S, F = a.shape
    D = x.shape[1]
    T = _row_tile(S, 1024)

    def body(a_ref, wd_ref, x_ref, gn_ref, o_ref, xn_ref):
        x_out = x_ref[...] + _dot(a_ref[...], wd_ref[...].reshape(F, D))
        o_ref[...] = x_out
        xn_ref[...] = _rms(x_out, gn_ref[...])

    row = lambda w: pl.BlockSpec((T, w), lambda i: (i, 0))
    return _call(
        body, name=name, grid=(S // T,), parallel=(0,), plan=plan,
        in_specs=[row(F), pl.BlockSpec(w_down4.shape, lambda i: (0, 0, 0), pipeline_mode=pl.Buffered(1)),
                  row(D), pl.BlockSpec((1, D), lambda i: (0, 0))],
        out_specs=[row(D), row(D)],
        out_shape=[jax.ShapeDtypeStruct((S, D), F32), jax.ShapeDtypeStruct((S, D), BF16)],
        args=(a, w_down4, x, g_next))


def _ffn_down_bwd(dxo, hv, hg, val16, gate16, conv_w, w_down4, *, name, plan=None):
    S = hv.shape[0]
    _, f4, D = w_down4.shape
    F = N_CHIPS * f4
    T = _row_tile(S, 512)
    nf = 2
    tf = 2 * f4
    nt = S // T

    def body(dx_ref, hv_ref, hg_ref, val_ref, gate_ref, cwv_ref, cwg_ref, wd_ref,
             dhv_ref, dhg_ref, dwd_ref, dwd16_ref, dcwv_ref, dcwg_ref, dcbv_ref, dcbg_ref, cv_scr, cg_scr):
        i = pl.program_id(1)

        @pl.when(i == 0)
        def _():
            cv_scr[...] = jnp.zeros_like(cv_scr)
            cg_scr[...] = jnp.zeros_like(cg_scr)
            dwd_ref[...] = jnp.zeros_like(dwd_ref)
            dcwv_ref[...] = jnp.zeros_like(dcwv_ref)
            dcwg_ref[...] = jnp.zeros_like(dcwg_ref)
            dcbv_ref[...] = jnp.zeros_like(dcbv_ref)
            dcbg_ref[...] = jnp.zeros_like(dcbg_ref)

        dxb = dx_ref[...].astype(BF16)
        val = val_ref[...].astype(F32)
        gate = gate_ref[...].astype(F32)
        sg = _sigmoid(gate)
        sil = gate * sg
        dwd_ref[...] += _dot_tn((sil * val).astype(BF16), dxb).reshape(2, f4, D)
        da = _dot_nt(dxb, wd_ref[...].reshape(tf, D))

        def conv_bwd(dhc, h0, cw, c_scr, dh_ref, dcw_ref, dcb_ref):
            ext = jnp.concatenate([dhc, c_scr[...]], axis=0)
            n1 = _shift_up(ext, 1)
            n2 = _shift_up(ext, 2)
            dh_ref[...] = (cw[2:3, :] * dhc + cw[1:2, :] * n1 + cw[0:1, :] * n2).astype(BF16)
            c_scr[...] = dhc[:SUBLANES, :]
            dcw_ref[0:1, :] += jnp.sum(n2 * h0, axis=0, keepdims=True)
            dcw_ref[1:2, :] += jnp.sum(n1 * h0, axis=0, keepdims=True)
            dcw_ref[2:3, :] += jnp.sum(dhc * h0, axis=0, keepdims=True)
            dcb_ref[...] += jnp.sum(dhc, axis=0, keepdims=True)

        conv_bwd(da * sil, hv_ref[...].astype(F32), cwv_ref[...], cv_scr, dhv_ref, dcwv_ref, dcbv_ref)
        conv_bwd(da * val * _dsilu(gate, sg), hg_ref[...].astype(F32), cwg_ref[...], cg_scr, dhg_ref, dcwg_ref,
                 dcbg_ref)

        @pl.when(i == nt - 1)
        def _():
            dwd16_ref[...] = dwd_ref[...].astype(BF16)

    rev = lambda i: nt - 1 - i
    wd_spec = pl.BlockSpec((2, f4, D), lambda f, i: (f, 0, 0))
    return _call(
        body, name=name, grid=(nf, nt), plan=plan,
        in_specs=[pl.BlockSpec((T, D), lambda f, i: (rev(i), 0)),
                  pl.BlockSpec((T, tf), lambda f, i: (rev(i), f)),
                  pl.BlockSpec((T, tf), lambda f, i: (rev(i), f)),
                  pl.BlockSpec((T, tf), lambda f, i: (rev(i), f)),
                  pl.BlockSpec((T, tf), lambda f, i: (rev(i), f)),
                  pl.BlockSpec((3, tf), lambda f, i: (0, f)),
                  pl.BlockSpec((3, tf), lambda f, i: (0, nf + f)),
                  wd_spec],
        out_specs=[pl.BlockSpec((T, tf), lambda f, i: (rev(i), f)),
                   pl.BlockSpec((T, tf), lambda f, i: (rev(i), f)),
                   wd_spec, wd_spec,
                   pl.BlockSpec((3, tf), lambda f, i: (0, f)),
                   pl.BlockSpec((3, tf), lambda f, i: (0, f)),
                   pl.BlockSpec((1, tf), lambda f, i: (0, f)),
                   pl.BlockSpec((1, tf), lambda f, i: (0, f))],
        out_shape=[jax.ShapeDtypeStruct((S, F), BF16), jax.ShapeDtypeStruct((S, F), BF16),
                   jax.ShapeDtypeStruct((N_CHIPS, f4, D), F32), jax.ShapeDtypeStruct((N_CHIPS, f4, D), BF16),
                   jax.ShapeDtypeStruct((3, F), F32), jax.ShapeDtypeStruct((3, F), F32),
                   jax.ShapeDtypeStruct((1, F), F32), jax.ShapeDtypeStruct((1, F), F32)],
        scratch=[pltpu.VMEM((SUBLANES, tf), F32), pltpu.VMEM((SUBLANES, tf), F32)],
        args=(dxo, hv, hg, val16, gate16, conv_w, conv_w, w_down4))


def _down_loss(a, w_down4, x, g, target, *, name):
    S, F = a.shape
    D = x.shape[1]
    T = _row_tile(S, 512)

    def body(a_ref, wd_ref, x_ref, g_ref, t_ref, loss_ref, dx_ref, dg_ref):
        @pl.when(pl.program_id(0) == 0)
        def _():
            loss_ref[...] = jnp.zeros_like(loss_ref)
            dg_ref[...] = jnp.zeros_like(dg_ref)

        xf = x_ref[...] + _dot(a_ref[...], wd_ref[...].reshape(F, D))
        r = lax.rsqrt(jnp.mean(xf * xf, axis=-1, keepdims=True) + EPS)
        xhat = xf * r
        err = xhat * g_ref[...] - t_ref[...]
        loss_ref[...] += jnp.sum(err * err, axis=0, keepdims=True) * (0.5 / D)
        dy = err * (1.0 / D)
        dxhat = dy * g_ref[...]
        dx_ref[...] = r * (dxhat - xhat * jnp.mean(dxhat * xhat, axis=-1, keepdims=True))
        dg_ref[...] += jnp.sum(dy * xhat, axis=0, keepdims=True)

    row = lambda w: pl.BlockSpec((T, w), lambda i: (i, 0))
    vec = pl.BlockSpec((1, D), lambda i: (0, 0))
    return _call(
        body, name=name, grid=(S // T,),
        in_specs=[row(F), pl.BlockSpec(w_down4.shape, lambda i: (0, 0, 0), pipeline_mode=pl.Buffered(1)),
                  row(D), vec, row(D)],
        out_specs=[vec, row(D), vec],
        out_shape=[jax.ShapeDtypeStruct((1, D), F32), jax.ShapeDtypeStruct((S, D), F32),
                   jax.ShapeDtypeStruct((1, D), F32)],
        args=(a, w_down4, x, g, target))[0]


BIG = ("w_in", "w_pa", "w_pb", "w_o", "w_up", "w_down")
SMALL = ("norm1_g", "b_gate", "pool_w", "pool_scale", "lb_logits", "hgrn_norm_g", "norm2_g", "conv_b", "final_g")
WEIGHTS = ("norm1_g", "w_in", "b_gate", "pool_w", "pool_scale", "lb_logits", "hgrn_norm_g", "w_pa", "w_pb", "w_o",
           "norm2_g", "w_up", "conv_w", "conv_b", "w_down", "final_g")


def _lower_bounds(lb_logits):
    soft = jax.nn.softmax(lb_logits.astype(F32), axis=0)
    cum = jnp.cumsum(soft, axis=0)
    return cum - cum[0:1]


def _step(x, target, sm, wts, shards=None):
    L = sm["norm1_g"].shape[0]
    wts = dict(wts)
    dist = shards is not None
    lbs, lb_vjp = jax.vjp(_lower_bounds, sm["lb_logits"])
    row = lambda a: a.reshape(1, -1)
    conv_w = sm.get("conv_w")

    def gather(names_layers, with_conv=False):
        items = [(shards[n], "rows", l) for n, l in names_layers]
        if with_conv:
            items.append((shards["conv_w"], "layer", None))
        return _GatherPlan(items)

    def landed(names_layers, outs):
        for key, arr in zip(names_layers, outs):
            wts[key] = arr

    own = {"in_proj": ("w_up",)}
    first = {"hgrn_fwd": ("w_pa", "w_pb", "w_o"), "mix_fwd": ("w_down",)}
    ahead = {"up": ("w_in", "w_pa", "w_pb", "w_o"), "down": ("w_down",)}
    conv_rider = "mix_fwd"

    def riders(l, kernel):
        if not dist:
            return [], None
        keys = [(n, l) for n in own.get(kernel, ())]
        keys += [(n, l) for n in first.get(kernel, ())] if l == 0 else []
        keys += [(n, l + 1) for n in ahead.get(kernel, ())] if l + 1 < L else []
        with_conv = l == 0 and kernel == conv_rider
        return keys, (gather(keys, with_conv) if keys or with_conv else None)

    keys = [("w_in", 0)] if dist else []
    (xn1,), got = _rmsnorm(x, row(sm["norm1_g"][0]), name="norm_in", plan=gather(keys) if keys else None)
    landed(keys, got)

    saved = []
    for l in range(L):
        keys, plan = riders(l, "in_proj")
        (z,), got = _matmul(xn1, wts[("w_in", l)], name=f"in_proj_{l}", plan=plan)
        landed(keys, got)
        keys, plan = riders(l, "hgrn_fwd")
        (o_raw, of, states), got = _hgrn_fwd(z, row(lbs[l]), row(sm["hgrn_norm_g"][l]), name=f"hgrn_fwd_{l}",
                                             plan=plan)
        landed(keys, got)
        keys, plan = riders(l, "mix_fwd")
        (x_mid, ya, yb, xn2, pm), got = _mix_fwd(
            x, of, z, sm["pool_w"][l], row(sm["pool_scale"][l]), row(sm["b_gate"][l]), wts[("w_pa", l)],
            wts[("w_pb", l)], wts[("w_o", l)], row(sm["norm2_g"][l]), name=f"mix_fwd_{l}", plan=plan)
        landed(keys, got)
        if dist and l == 0:
            full = got[-1]
            conv_w = jnp.concatenate([full[:, k] for k in range(N_CHIPS)], axis=2)
        keys, plan = riders(l, "up")
        (hv, hg, val16, gate16, a16), got = _up_conv(xn2, wts[("w_up", l)], conv_w[l], row(sm["conv_b"][l]),
                                                     name=f"up_{l}", plan=plan)
        landed(keys, got)
        saved.append(dict(x=x, xn1=xn1, z=z, pm=pm, o_raw=o_raw, of=of, states=states,
                          x_mid=x_mid, ya=ya, yb=yb, xn2=xn2, hv=hv, hg=hg, val16=val16, gate16=gate16))
        if l + 1 < L:
            keys, plan = riders(l, "down")
            (x, xn1), got = _down(a16, wts[("w_down", l)], x_mid, row(sm["norm1_g"][l + 1]), name=f"down_{l}",
                                  plan=plan)
            landed(keys, got)
        else:
            loss_cols, dx, d_final_g = _down_loss(a16, wts[("w_down", l)], x_mid, row(sm["final_g"]), target,
                                                  name="down_loss")

    small = {k: [None] * L for k in ("norm1_g", "b_gate", "pool_w", "pool_scale", "hgrn_norm_g", "norm2_g",
                                     "conv_w", "conv_b")}
    big32, big16, recv = {}, {}, {}
    dlbs = [None] * L
    pending = []

    def scatter():
        if not (dist and pending):
            return [], None
        keys = list(pending)
        del pending[:]
        return keys, _ScatterPlan([big16[k] for k in keys])

    def sent(keys, outs):
        for key, arr in zip(keys, outs):
            recv[key] = arr

    def made(name, l, g32, g16):
        big32[(name, l)], big16[(name, l)] = g32, g16
        pending.append((name, l))

    for l in reversed(range(L)):
        s = saved[l]
        keys, plan = scatter()
        (dhv, dhg, d_wd, d_wd16, dcwv, dcwg, dcbv, dcbg), got = _ffn_down_bwd(
            dx, s["hv"], s["hg"], s["val16"], s["gate16"], conv_w[l], wts[("w_down", l)], name=f"down_bwd_{l}",
            plan=plan)
        sent(keys, got)
        made("w_down", l, d_wd, d_wd16)
        small["conv_w"][l] = jnp.concatenate([dcwv, dcwg], axis=1)
        small["conv_b"][l] = jnp.concatenate([dcbv, dcbg], axis=1)[0]
        keys, plan = scatter()
        (d_wu, d_wu16), got = _wgrad(s["xn2"], [dhv, dhg], name=f"up_wgrad_{l}", rows=2048, plan=plan)
        sent(keys, got)
        made("w_up", l, d_wu, d_wu16)
        (dxm, dg2), _ = _dgrad_norm([dhv, dhg], wts[("w_up", l)], s["x_mid"], row(sm["norm2_g"][l]), dx,
                                    name=f"up_dgrad_{l}")
        small["norm2_g"][l] = dg2[0]

        (dzg, dpm, dof, d_wo, d_wpa, d_wpb, dbg, d_wo16, d_wpa16, d_wpb16), _ = _mix_bwd(
            dxm, s["ya"], s["yb"], s["z"], row(sm["b_gate"][l]), s["pm"], s["of"],
            wts[("w_pa", l)], wts[("w_pb", l)], wts[("w_o", l)], name=f"mix_bwd_{l}")
        made("w_o", l, d_wo, d_wo16)
        made("w_pa", l, d_wpa, d_wpa16)
        made("w_pb", l, d_wpb, d_wpb16)
        small["b_gate"][l] = dbg[0]

        du, dpw, dps = _pool_bwd(s["z"], dpm, sm["pool_w"][l], row(sm["pool_scale"][l]), name=f"pool_bwd_{l}")
        small["pool_w"][l], small["pool_scale"][l] = dpw, dps[0]

        keys, plan = scatter()
        (dzq, dzf, dzi, dzo, dlb, dng), got = _hgrn_bwd(s["z"], row(lbs[l]), row(sm["hgrn_norm_g"][l]), s["o_raw"],
                                                      s["states"], dof, name=f"hgrn_bwd_{l}", plan=plan)
        sent(keys, got)
        dlbs[l] = dlb[0]
        small["hgrn_norm_g"][l] = jnp.sum(dng.reshape(-1, LANES), axis=0)

        dz = [du, dzq, dzf, dzi, dzo, dzg]
        (d_wi, d_wi16), _ = _wgrad(s["xn1"], dz, name=f"in_wgrad_{l}", rows=1024)
        made("w_in", l, d_wi, d_wi16)
        keys, plan = scatter()
        (dx, dg1), got = _dgrad_norm(dz, wts[("w_in", l)], s["x"], row(sm["norm1_g"][l]), dxm,
                                     name=f"in_dgrad_{l}", plan=plan)
        sent(keys, got)
        small["norm1_g"][l] = dg1[0]

    out = {k: jnp.stack(v) for k, v in small.items()}
    out["lb_logits"] = lb_vjp(jnp.stack(dlbs))[0]
    out["final_g"] = d_final_g[0]
    return loss_cols, dx, out, big32, recv


def _elementwise_rows(R, n, n_arrays):
    if 2 * n_arrays * R * n * 4 <= VMEM_LIMIT // 4 or R % 8:
        return R
    block = VMEM_LIMIT // 2 // (2 * n_arrays)
    want = 8
    while want * 2 * n * 4 <= block:
        want *= 2
    return _row_tile(R, want)


def _sum_layers(own, got, chip, *, name):
    L = len(own)
    _, r, n = own[0].shape
    T = _elementwise_rows(r, n, 6)
    nt = r // T

    def body(chip_ref, *refs):
        o_ref = refs[-1]
        l = pl.program_id(0)
        for k in range(L):
            @pl.when(l == k)
            def _():
                own_ref, got_ref = refs[2 * k], refs[2 * k + 1]
                acc = own_ref[...]
                for j in range(3):
                    acc = acc + got_ref[j].astype(F32)
                o_ref[...] = acc

    in_specs = []
    for k in range(L):
        hold = 0 if k else nt - 1
        in_specs.append(pl.BlockSpec((None, T, n), lambda l, i, c, k=k, hold=hold: (c[0], jnp.where(l == k, i, hold), 0)))
        in_specs.append(pl.BlockSpec((3, T, n), lambda l, i, c, k=k, hold=hold: (0, jnp.where(l == k, i, hold), 0)))
    grid_spec = pltpu.PrefetchScalarGridSpec(
        num_scalar_prefetch=1, grid=(L, nt), in_specs=in_specs,
        out_specs=pl.BlockSpec((None, T, n), lambda l, i, c: (l, i, 0)))
    args = [a for pair in zip(own, got) for a in pair]
    return pl.pallas_call(
        body, name=name, grid_spec=grid_spec, out_shape=jax.ShapeDtypeStruct((L, r, n), F32),
        compiler_params=pltpu.CompilerParams(dimension_semantics=("arbitrary", "arbitrary"),
                                             vmem_limit_bytes=VMEM_LIMIT),
    )(chip, *args)


def _sum_stack(parts, *, name):
    K, R, n = parts.shape
    T = _elementwise_rows(R, n, K + 1)

    def body(p_ref, o_ref):
        acc = p_ref[0]
        for j in range(1, K):
            acc = acc + p_ref[j]
        o_ref[...] = acc

    return _call(
        body, name=name, grid=(R // T,), parallel=(0,),
        in_specs=[pl.BlockSpec((K, T, n), lambda i: (0, i, 0))],
        out_specs=[pl.BlockSpec((T, n), lambda i: (i, 0))],
        out_shape=[jax.ShapeDtypeStruct((R, n), F32)],
        args=(parts,))[0][0]


def _adamw(w, m, v, g_parts, *, name):
    R, n = w.shape
    n_g = len(g_parts)
    T = _elementwise_rows(R, n, 7 + n_g)

    def body(*refs):
        w_ref, m_ref, v_ref = refs[:3]
        g_refs = refs[3:3 + n_g]
        go_ref, d_ref, mo_ref, vo_ref = refs[3 + n_g:]
        g_ = g_refs[0][...]
        for r in g_refs[1:]:
            g_ = g_ + r[...]
        m_ = ADAM_B1 * m_ref[...] + (1.0 - ADAM_B1) * g_
        v_ = ADAM_B2 * v_ref[...] + (1.0 - ADAM_B2) * (g_ * g_)
        m_hat = m_ / (1.0 - ADAM_B1 ** ADAM_STEP)
        v_hat = v_ / (1.0 - ADAM_B2 ** ADAM_STEP)
        go_ref[...] = g_
        d_ref[...] = -ADAM_LR * (m_hat / (jnp.sqrt(v_hat) + ADAM_EPS) + ADAM_WD * w_ref[...])
        mo_ref[...] = m_
        vo_ref[...] = v_

    blk = pl.BlockSpec((T, n), lambda i: (i, 0))
    return _call(
        body, name=name, grid=(R // T,), parallel=(0,),
        in_specs=[blk] * (3 + n_g), out_specs=[blk] * 4,
        out_shape=[jax.ShapeDtypeStruct((R, n), F32)] * 4,
        args=(w, m, v, *g_parts))[0]


PACK_ALIGN = 8 * LANES


def _pack(pieces):
    flat = []
    for a in pieces:
        a = a.reshape(-1)
        pad = (-a.shape[0]) % PACK_ALIGN
        flat.append(jnp.pad(a, (0, pad)) if pad else a)
    return jnp.concatenate(flat).reshape(-1, LANES)


def _unpack(buf, shapes):
    flat = buf.reshape(-1)
    out, off = [], 0
    for shp in shapes:
        size = 1
        for s in shp:
            size *= s
        out.append(flat[off:off + size].reshape(shp))
        off += size + (-size) % PACK_ALIGN
    return out


def kernel(x, norm1_g, w_in, b_gate, pool_w, pool_scale, lb_logits, hgrn_norm_g, w_pa, w_pb, w_o, norm2_g, w_up, conv_w, conv_b, w_down, final_g, loss_target, m_norm1_g, m_w_in, m_b_gate, m_pool_w, m_pool_scale, m_lb_logits, m_hgrn_norm_g, m_w_pa, m_w_pb, m_w_o, m_norm2_g, m_w_up, m_conv_w, m_conv_b, m_w_down, m_final_g, v_norm1_g, v_w_in, v_b_gate, v_pool_w, v_pool_scale, v_lb_logits, v_hgrn_norm_g, v_w_pa, v_w_pb, v_w_o, v_norm2_g, v_w_up, v_conv_w, v_conv_b, v_w_down, v_final_g):
    env = dict(locals())
    w = {n: env[n] for n in WEIGHTS}
    m = {n: env["m_" + n] for n in WEIGHTS}
    v = {n: env["v_" + n] for n in WEIGHTS}
    my_chip = 2 * lax.axis_index("x") + lax.axis_index("y")
    L = w_in.shape[0]

    shards = {n: w[n].astype(BF16) for n in BIG}
    shards["conv_w"] = w["conv_w"]
    sm = {n: w[n] for n in SMALL}
    loss_cols, grad_x, g_small, big32, recv = _step(x[0], loss_target[0], sm, {}, shards)

    chip = my_chip.reshape(1).astype(jnp.int32)
    sums = [_sum_layers([big32[(n, l)] for l in range(L)], [recv[(n, l)] for l in range(L)], chip,
                        name="chip_sum_" + n) for n in BIG]
    small_names = list(SMALL)
    small_pieces = [g_small[n] for n in small_names] + [g_small["conv_w"], loss_cols]
    small_shapes = [a.shape for a in small_pieces]
    packed = _pack(small_pieces)
    Rs = packed.shape[0]
    swapped = _run_plan(_Together([_SiblingPlan(sums), _EveryonePlan(packed)]), name="tail_exchange")
    theirs, everyone = swapped[:-1], swapped[-1].reshape(8, Rs, LANES)
    g, delta, new_m, new_v = {}, {}, {}, {}
    for n, mine, other in zip(BIG, sums, theirs):
        shp = w[n].shape
        two_d = lambda a: a.reshape(-1, shp[-1])
        outs = _adamw(two_d(w[n]), two_d(m[n]), two_d(v[n]), [two_d(mine), two_d(other)], name="adamw_" + n)
        g[n], delta[n], new_m[n], new_v[n] = [a.reshape(shp) for a in outs]

    summed = _unpack(_sum_stack(everyone, name="small_sum"), small_shapes)
    loss = jnp.sum(summed[-1])
    cshard = w["conv_w"].shape[2]
    gs = dict(zip(small_names, summed[:len(small_names)]))
    g_cw = lax.dynamic_slice_in_dim(summed[-2], my_chip * cshard, cshard, axis=2)

    sm_out = _adamw(_pack([w[n] for n in small_names]), _pack([m[n] for n in small_names]),
                    _pack([v[n] for n in small_names]), [_pack([gs[n] for n in small_names])], name="adamw_small")
    shapes = [w[n].shape for n in small_names]
    for n, g_, d_, m_, v_ in zip(small_names, *[_unpack(a, shapes) for a in sm_out]):
        g[n], delta[n], new_m[n], new_v[n] = g_, d_, m_, v_
    shp = w["conv_w"].shape
    two_d = lambda a: a.reshape(-1, shp[-1])
    outs = _adamw(two_d(w["conv_w"]), two_d(m["conv_w"]), two_d(v["conv_w"]), [two_d(g_cw)], name="adamw_conv_w")
    g["conv_w"], delta["conv_w"], new_m["conv_w"], new_v["conv_w"] = [a.reshape(shp) for a in outs]

    return (loss, grad_x[None], *[g[n] for n in WEIGHTS], *[delta[n] for n in WEIGHTS],
            *[new_m[n] for n in WEIGHTS], *[new_v[n] for n in WEIGHTS])
```

```python
import jax
import jax.numpy as jnp
from jax import lax
from jax.experimental import pallas as pl
from jax.experimental.pallas import tpu as pltpu

F32 = jnp.float32
BF16 = jnp.bfloat16

EPS = 1e-6
CHUNK = 64
SUB = 32
LANES = 128
SUBLANES = 8
POOL_WINDOWS = (2, 4, 8, 16)
HALO_POOL = 16
EXP_CLAMP = 80.0

ADAM_LR = 0.001
ADAM_B1 = 0.9
ADAM_B2 = 0.999
ADAM_EPS = 1e-08
ADAM_WD = 0.01
ADAM_STEP = 10

VMEM_LIMIT = 56 * 1024 * 1024
MESH_ID = pl.DeviceIdType.MESH
N_CHIPS = 4
ANY = pl.BlockSpec(memory_space=pl.ANY)


def _dot(a, b):
    return jnp.dot(a, b, preferred_element_type=F32)


def _dot_nt(a, b):
    return lax.dot_general(a, b, (((1,), (1,)), ((), ())), preferred_element_type=F32)


def _dot_tn(a, b):
    return lax.dot_general(a, b, (((0,), (0,)), ((), ())), preferred_element_type=F32)


def _sigmoid(x):
    return jax.nn.sigmoid(x)


def _dsilu(x, s):
    return s * (1.0 + x * (1.0 - s))


def _row_tile(rows, want):
    t = min(rows, want)
    while rows % t:
        t //= 2
    return t


def _place():
    x, y, c = lax.axis_index("x"), lax.axis_index("y"), lax.axis_index("c")
    chips = [(1 - x, y), (x, 1 - y), (1 - x, 1 - y)]
    return x, y, c, chips


def _remote(src, dst, sems, k, to):
    return pltpu.make_async_remote_copy(src_ref=src, dst_ref=dst, send_sem=sems[0].at[k], recv_sem=sems[1].at[k],
                                        device_id=to, device_id_type=MESH_ID)


class _GatherPlan:
    def __init__(self, items):
        self.items = items
        self.inputs = [a for a, _, _ in items]
        self.out_shapes = []
        for a, kind, _ in items:
            shp = (N_CHIPS,) + a.shape[1:] if kind == "rows" else (a.shape[0], N_CHIPS) + a.shape[1:]
            self.out_shapes.append(jax.ShapeDtypeStruct(shp, a.dtype))
        n = len(items)
        self.scratch = [pltpu.SemaphoreType.DMA((6 * n,)), pltpu.SemaphoreType.DMA((6 * n,)),
                        pltpu.SemaphoreType.DMA((2 * n,))]

    def _views(self, i, src, dst):
        _, kind, l = self.items[i]
        if kind == "rows":
            half = src.shape[1] // 2
            part = lambda core: src.at[l, pl.ds(core * half, half), :]
            land = lambda chip, core: dst.at[chip, pl.ds(core * half, half), :]
        else:
            part = lambda core: src.at[core]
            land = lambda chip, core: dst.at[core, chip]
        return part, land

    def start(self, srcs, dsts, sems):
        x, y, c, chips = _place()
        me = 2 * x + y
        for i, (src, dst) in enumerate(zip(srcs, dsts)):
            part, land = self._views(i, src, dst)
            for core in range(2):
                pltpu.make_async_copy(part(core), land(me, core), sems[2].at[2 * i + core]).start()
            for j, (px, py) in enumerate(chips):
                _remote(part(c), land(me, c), sems, 6 * i + j, (px, py, c)).start()

    def finish(self, srcs, dsts, sems):
        x, y, c, chips = _place()
        me = 2 * x + y
        sibling = (x, y, 1 - c)
        for i, (src, dst) in enumerate(zip(srcs, dsts)):
            part, land = self._views(i, src, dst)
            for j, (px, py) in enumerate(chips):
                got = land(2 * px + py, c)
                _remote(got, got, sems, 6 * i + j, (px, py, c)).wait_recv()
                _remote(got, got, sems, 6 * i + 3 + j, sibling).start()
        for i, (src, dst) in enumerate(zip(srcs, dsts)):
            part, land = self._views(i, src, dst)
            for j, (px, py) in enumerate(chips):
                got = land(2 * px + py, 1 - c)
                _remote(got, got, sems, 6 * i + 3 + j, sibling).wait_recv()
            for j, (px, py) in enumerate(chips):
                _remote(part(c), land(me, c), sems, 6 * i + j, (px, py, c)).wait_send()
                mine = land(2 * px + py, c)
                _remote(mine, mine, sems, 6 * i + 3 + j, sibling).wait_send()
            for core in range(2):
                pltpu.make_async_copy(part(core), land(me, core), sems[2].at[2 * i + core]).wait()


class _ScatterPlan:
    def __init__(self, items):
        self.inputs = list(items)
        self.out_shapes = [jax.ShapeDtypeStruct((3,) + a.shape[1:], a.dtype) for a in items]
        n = len(items)
        self.scratch = [pltpu.SemaphoreType.DMA((3 * n,)), pltpu.SemaphoreType.DMA((3 * n,))]

    def _copies(self, srcs, dsts, sems):
        x, y, c, chips = _place()
        return [_remote(src.at[2 * px + py], dst.at[j], sems, 3 * i + j, (px, py, c))
                for i, (src, dst) in enumerate(zip(srcs, dsts)) for j, (px, py) in enumerate(chips)]

    def start(self, srcs, dsts, sems):
        for cp in self._copies(srcs, dsts, sems):
            cp.start()

    def finish(self, srcs, dsts, sems):
        copies = self._copies(srcs, dsts, sems)
        for cp in copies:
            cp.wait_recv()
        for cp in copies:
            cp.wait_send()


class _SiblingPlan:
    def __init__(self, items):
        self.inputs = list(items)
        self.out_shapes = [jax.ShapeDtypeStruct(a.shape, a.dtype) for a in items]
        n = len(items)
        self.scratch = [pltpu.SemaphoreType.DMA((n,)), pltpu.SemaphoreType.DMA((n,))]

    def _copies(self, srcs, dsts, sems):
        x, y, c, _ = _place()
        return [_remote(src, dst, sems, i, (x, y, 1 - c)) for i, (src, dst) in enumerate(zip(srcs, dsts))]

    def start(self, srcs, dsts, sems):
        for cp in self._copies(srcs, dsts, sems):
            cp.start()

    def finish(self, srcs, dsts, sems):
        copies = self._copies(srcs, dsts, sems)
        for cp in copies:
            cp.wait_recv()
        for cp in copies:
            cp.wait_send()


class _EveryonePlan:
    def __init__(self, block):
        self.inputs = [block]
        self.m = block.shape[0]
        self.out_shapes = [jax.ShapeDtypeStruct((8 * self.m,) + block.shape[1:], block.dtype)]
        self.scratch = [pltpu.SemaphoreType.DMA((7,)), pltpu.SemaphoreType.DMA((7,)), pltpu.SemaphoreType.DMA((1,))]

    def _rows(self, dst, px, py, pc):
        return dst.at[pl.ds((4 * px + 2 * py + pc) * self.m, self.m), :]

    def start(self, srcs, dsts, sems):
        x, y, c, chips = _place()
        src, dst = srcs[0], dsts[0]
        pltpu.make_async_copy(src, self._rows(dst, x, y, c), sems[2].at[0]).start()
        _remote(src, self._rows(dst, x, y, c), sems, 0, (x, y, 1 - c)).start()
        for j, (px, py) in enumerate(chips):
            _remote(src, self._rows(dst, x, y, c), sems, 1 + j, (px, py, c)).start()

    def finish(self, srcs, dsts, sems):
        x, y, c, chips = _place()
        src, dst = srcs[0], dsts[0]
        sibling = (x, y, 1 - c)
        for j, (px, py) in enumerate(chips):
            got = self._rows(dst, px, py, c)
            _remote(got, got, sems, 1 + j, (px, py, c)).wait_recv()
            _remote(got, got, sems, 4 + j, sibling).start()
        sib = self._rows(dst, x, y, 1 - c)
        _remote(sib, sib, sems, 0, sibling).wait_recv()
        for j, (px, py) in enumerate(chips):
            got = self._rows(dst, px, py, 1 - c)
            _remote(got, got, sems, 4 + j, sibling).wait_recv()
        mine = self._rows(dst, x, y, c)
        _remote(src, mine, sems, 0, sibling).wait_send()
        for j, (px, py) in enumerate(chips):
            _remote(src, mine, sems, 1 + j, (px, py, c)).wait_send()
            got = self._rows(dst, px, py, c)
            _remote(got, got, sems, 4 + j, sibling).wait_send()
        pltpu.make_async_copy(src, mine, sems[2].at[0]).wait()


def _call(body, *, name, grid, in_specs, out_specs, out_shape, args, scratch=(), parallel=(), plan=None):
    n_in, n_out, n_scr = len(in_specs), len(out_shape), len(scratch)
    sem = tuple("parallel" if (a in parallel and plan is None) else "arbitrary" for a in range(len(grid)))
    params = pltpu.CompilerParams(dimension_semantics=sem, vmem_limit_bytes=VMEM_LIMIT)
    if plan is None:
        outs = pl.pallas_call(body, name=name, grid=grid, in_specs=in_specs, out_specs=out_specs,
                              out_shape=out_shape, scratch_shapes=list(scratch), compiler_params=params)(*args)
        return list(outs), []
    p_in, p_out, p_scr = len(plan.inputs), len(plan.out_shapes), len(plan.scratch)

    def wrapped(*refs):
        ins, refs = refs[:n_in], refs[n_in:]
        p_ins, refs = refs[:p_in], refs[p_in:]
        outs, refs = refs[:n_out], refs[n_out:]
        p_outs, refs = refs[:p_out], refs[p_out:]
        scr, p_sems = refs[:n_scr], refs[n_scr:]
        ids = [pl.program_id(a) for a in range(len(grid))]
        first = _all([i == 0 for i in ids])
        last = _all([i == n - 1 for i, n in zip(ids, grid)])

        @pl.when(first)
        def _():
            plan.start(p_ins, p_outs, p_sems)

        body(*ins, *outs, *scr)

        @pl.when(last)
        def _():
            plan.finish(p_ins, p_outs, p_sems)

    outs = pl.pallas_call(
        wrapped, name=name, grid=grid,
        in_specs=list(in_specs) + [ANY] * p_in, out_specs=list(out_specs) + [ANY] * p_out,
        out_shape=list(out_shape) + list(plan.out_shapes),
        scratch_shapes=list(scratch) + list(plan.scratch), compiler_params=params,
    )(*args, *plan.inputs)
    return list(outs[:n_out]), list(outs[n_out:])


def _all(conds):
    out = conds[0]
    for c in conds[1:]:
        out = out & c
    return out


class _Together:
    def __init__(self, plans):
        self.plans = plans
        self.inputs = [a for p in plans for a in p.inputs]
        self.out_shapes = [s for p in plans for s in p.out_shapes]
        self.scratch = [s for p in plans for s in p.scratch]

    def _split(self, refs, count):
        out, at = [], 0
        for p in self.plans:
            out.append(refs[at:at + count(p)])
            at += count(p)
        return out

    def _parts(self, srcs, dsts, sems):
        return zip(self.plans, self._split(srcs, lambda p: len(p.inputs)),
                   self._split(dsts, lambda p: len(p.out_shapes)), self._split(sems, lambda p: len(p.scratch)))

    def start(self, srcs, dsts, sems):
        for p, s, d, m in self._parts(srcs, dsts, sems):
            p.start(s, d, m)

    def finish(self, srcs, dsts, sems):
        for p, s, d, m in self._parts(srcs, dsts, sems):
            p.finish(s, d, m)


def _run_plan(plan, *, name):
    p_in, p_out = len(plan.inputs), len(plan.out_shapes)

    def body(*refs):
        srcs, dsts, sems = refs[:p_in], refs[p_in:p_in + p_out], refs[p_in + p_out:]
        plan.start(srcs, dsts, sems)
        plan.finish(srcs, dsts, sems)

    return list(pl.pallas_call(body, name=name, in_specs=[ANY] * p_in, out_specs=[ANY] * p_out,
                               out_shape=list(plan.out_shapes), scratch_shapes=list(plan.scratch))(*plan.inputs))


def _rms(xf, g):
    r = lax.rsqrt(jnp.mean(xf * xf, axis=-1, keepdims=True) + EPS)
    return (xf * r * g).astype(BF16)


def _rmsnorm(x, g, *, name, plan=None):
    S, D = x.shape
    tm = _row_tile(S, 1024)

    def body(x_ref, g_ref, xn_ref):
        xn_ref[...] = _rms(x_ref[...], g_ref[...])

    return _call(
        body, name=name, grid=(S // tm,), parallel=(0,), plan=plan,
        in_specs=[pl.BlockSpec((tm, D), lambda i: (i, 0)), pl.BlockSpec((1, D), lambda i: (0, 0))],
        out_specs=[pl.BlockSpec((tm, D), lambda i: (i, 0))],
        out_shape=[jax.ShapeDtypeStruct((S, D), BF16)],
        args=(x, g))


def _matmul(xn, w4, *, name, out_dtype=F32, plan=None):
    S, D = xn.shape
    n4 = w4.shape[2]
    tm = _row_tile(S, 2048)

    def body(xn_ref, w_ref, o_ref):
        o_ref[...] = _dot(xn_ref[...], w_ref[...]).astype(out_dtype)

    return _call(
        body, name=name, grid=(S // tm, N_CHIPS), parallel=(0,), plan=plan,
        in_specs=[pl.BlockSpec((tm, D), lambda i, j: (i, 0)),
                  pl.BlockSpec((None, D, n4), lambda i, j: (j, 0, 0))],
        out_specs=[pl.BlockSpec((tm, n4), lambda i, j: (i, j))],
        out_shape=[jax.ShapeDtypeStruct((S, N_CHIPS * n4), out_dtype)],
        args=(xn, w4))


def _segments(widths, n4):
    per_chip = [[] for _ in range(N_CHIPS)]
    c0 = 0
    for p, w in enumerate(widths):
        a = c0
        while a < c0 + w:
            k = a // n4
            b = min(c0 + w, (k + 1) * n4)
            per_chip[k].append((p, (a - c0, b - c0), (a - k * n4, b - k * n4)))
            a = b
        c0 += w
    assert c0 == N_CHIPS * n4
    return per_chip


def _piece_specs(pieces, n4, tm):
    per_chip = _segments([p.shape[1] for p in pieces], n4)
    specs, local, start = [], [[] for _ in range(N_CHIPS)], 0
    for p, arr in enumerate(pieces):
        chips = [k for k in range(N_CHIPS) if any(seg[0] == p for seg in per_chip[k])]
        lo, hi = chips[0], chips[-1]
        tiled = arr.shape[1] % n4 == 0 and start % n4 == 0
        start += arr.shape[1]
        if tiled:
            imap = lambda k, i, lo=lo, hi=hi: (jnp.where((k >= lo) & (k <= hi), i, 0), jnp.clip(k - lo, 0, hi - lo))
            specs.append(pl.BlockSpec((tm, n4), imap))
        else:
            imap = lambda k, i, lo=lo, hi=hi: (jnp.where((k >= lo) & (k <= hi), i, 0), 0)
            specs.append(pl.BlockSpec((tm, arr.shape[1]), imap))
        for k in chips:
            for q, (pa, pb), cols in per_chip[k]:
                if q == p:
                    local[k].append((p, (0, n4) if tiled else (pa, pb), cols))
    return specs, local


def _dgrad_norm(dys, w4, x, g, dres, *, name, plan=None):
    S, D = x.shape
    n4 = w4.shape[2]
    tm = _row_tile(S, 512)
    per_chip = _segments([a.shape[1] for a in dys], n4)
    n_p = len(dys)

    def body(*refs):
        dy_refs = refs[:n_p]
        w_ref, x_ref, g_ref, dres_ref, dx_ref, dg_ref = refs[n_p:]

        @pl.when(pl.program_id(0) == 0)
        def _():
            dg_ref[...] = jnp.zeros_like(dg_ref)

        dxn = None
        for k in range(N_CHIPS):
            for p, (pa, pb), (ca, cb) in per_chip[k]:
                part = _dot_nt(dy_refs[p][:, pa:pb], w_ref[k, :, ca:cb])
                dxn = part if dxn is None else dxn + part
        xf = x_ref[...]
        r = lax.rsqrt(jnp.mean(xf * xf, axis=-1, keepdims=True) + EPS)
        xhat = xf * r
        dxhat = dxn * g_ref[...]
        dx_ref[...] = dres_ref[...] + r * (dxhat - xhat * jnp.mean(dxhat * xhat, axis=-1, keepdims=True))
        dg_ref[...] += jnp.sum(dxn * xhat, axis=0, keepdims=True)

    row = lambda w: pl.BlockSpec((tm, w), lambda i: (i, 0))
    return _call(
        body, name=name, grid=(S // tm,), plan=plan,
        in_specs=[row(a.shape[1]) for a in dys]
        + [pl.BlockSpec(w4.shape, lambda i: (0, 0, 0), pipeline_mode=pl.Buffered(1)),
           row(D), pl.BlockSpec((1, D), lambda i: (0, 0)), row(D)],
        out_specs=[row(D), pl.BlockSpec((1, D), lambda i: (0, 0))],
        out_shape=[jax.ShapeDtypeStruct((S, D), F32), jax.ShapeDtypeStruct((1, D), F32)],
        args=(*dys, w4, x, g, dres))


def _wgrad(a, dys, *, name, rows, plan=None):
    S, K = a.shape
    n4 = sum(p.shape[1] for p in dys) // N_CHIPS
    tm = _row_tile(S, rows)
    ns = S // tm
    specs, local = _piece_specs(dys, n4, tm)
    n_p = len(dys)

    def body(*refs):
        a_ref = refs[0]
        dy_refs = refs[1:1 + n_p]
        o_ref, o16_ref = refs[1 + n_p:]
        n, s = pl.program_id(0), pl.program_id(1)

        @pl.when(s == 0)
        def _():
            o_ref[...] = jnp.zeros_like(o_ref)

        for k in range(N_CHIPS):
            @pl.when(n == k)
            def _():
                av = a_ref[...]
                for p, (pa, pb), (ca, cb) in local[k]:
                    o_ref[:, ca:cb] += _dot_tn(av, dy_refs[p][:, pa:pb])

        @pl.when(s == ns - 1)
        def _():
            o16_ref[...] = o_ref[...].astype(BF16)

    out = pl.BlockSpec((None, K, n4), lambda n, s: (n, 0, 0))
    return _call(
        body, name=name, grid=(N_CHIPS, ns), parallel=(0,), plan=plan,
        in_specs=[pl.BlockSpec((tm, K), lambda n, s: (s, 0))] + specs,
        out_specs=[out, out],
        out_shape=[jax.ShapeDtypeStruct((N_CHIPS, K, n4), F32), jax.ShapeDtypeStruct((N_CHIPS, K, n4), BF16)],
        args=(a, *dys))


def _tiles(x):
    return x.reshape(x.shape[0] // SUBLANES, SUBLANES, x.shape[1])


def _shift_down(xp, s):
    n = xp.shape[0] - SUBLANES
    if s == SUBLANES:
        return xp[:n, :]
    t = _tiles(xp)
    rot = pltpu.roll(t, s, 1)
    sub = lax.broadcasted_iota(jnp.int32, t.shape, 1)[1:]
    return jnp.where(sub >= s, rot[1:], rot[:-1]).reshape(n, xp.shape[1])


def _shift_up(xn, s):
    n = xn.shape[0] - SUBLANES
    if s == SUBLANES:
        return xn[SUBLANES:, :]
    t = _tiles(xn)
    rot = pltpu.roll(t, SUBLANES - s, 1)
    sub = lax.broadcasted_iota(jnp.int32, t.shape, 1)[1:]
    return jnp.where(sub < SUBLANES - s, rot[:-1], rot[1:]).reshape(n, xn.shape[1])


def _pooled(u, halo, first_tile, row0):
    T = u.shape[0]
    halo = jnp.where(first_tile, 0.0, halo)
    pad = jnp.zeros((SUBLANES, u.shape[1]), F32)
    up = jnp.concatenate([pad, halo, u], axis=0)
    t1 = (row0 + lax.broadcasted_iota(jnp.int32, (T, 1), 0) + 1).astype(F32)
    outs = []
    for gi, w in enumerate(POOL_WINDOWS):
        s = up[:, gi * LANES:(gi + 1) * LANES]
        k = 1
        while k < w:
            if k < SUBLANES:
                s = jnp.concatenate([s[:SUBLANES, :], s[SUBLANES:, :] + _shift_down(s, k)], axis=0)
            else:
                s = s[SUBLANES:, :] + _shift_down(s, k)
            k *= 2
        s = s[-T:, :]
        inv = 1.0 / jnp.minimum(t1, float(w))
        outs.append(s * inv - u[:, gi * LANES:(gi + 1) * LANES])
    return outs


def _pool_mix(u, halo, first_tile, row0, pw_ref, ps_ref):
    pooled = _pooled(u, halo, first_tile, row0)
    outs = []
    for gi in range(len(POOL_WINDOWS)):
        mixed = _dot(pooled[gi].astype(BF16), pw_ref[gi].astype(BF16))
        outs.append((mixed * ps_ref[:, gi * LANES:(gi + 1) * LANES]).astype(BF16))
    return jnp.concatenate(outs, axis=1)


def _pool_grad(u, halo, first_tile, row0, d, dnext, pw_ref, ps_ref, du_ref, dpw_ref, dps_ref):
    T, P = d.shape
    pooled = _pooled(u, halo, first_tile, row0)
    pad = jnp.zeros((SUBLANES, P), F32)
    dext = jnp.concatenate([d, dnext, pad], axis=0)
    t1 = (row0 + lax.broadcasted_iota(jnp.int32, (T + HALO_POOL + SUBLANES, 1), 0) + 1).astype(F32)
    for gi, w in enumerate(POOL_WINDOWS):
        cols = slice(gi * LANES, (gi + 1) * LANES)
        pw = pw_ref[gi].astype(BF16)
        pg = pooled[gi].astype(BF16)
        mixed = _dot(pg, pw)
        dps_ref[:, cols] += jnp.sum(d[:, cols] * mixed, axis=0, keepdims=True)
        dmixed = (dext[:, cols] * ps_ref[:, cols]).astype(BF16)
        dpw_ref[gi] += _dot_tn(pg, dmixed[:T, :])
        dpooled = _dot_nt(dmixed, pw)
        e = dpooled * (1.0 / jnp.minimum(t1, float(w)))
        k = 1
        while k < w:
            if k < SUBLANES:
                e = jnp.concatenate([e[:-SUBLANES, :] + _shift_up(e, k), e[-SUBLANES:, :]], axis=0)
            else:
                e = e[:-SUBLANES, :] + _shift_up(e, k)
            k *= 2
        du_ref[:, cols] = (e[:T, :] - dpooled[:T, :]).astype(BF16)


def _cumsum_rows(x):
    n = x.shape[0]
    row = lax.broadcasted_iota(jnp.int32, x.shape, 0)
    s = 1
    while s < n:
        x = x + jnp.where(row >= s, pltpu.roll(x, s, 0), 0.0)
        s *= 2
    return x


def _rev_cumsum_rows(x):
    n = x.shape[0]
    row = lax.broadcasted_iota(jnp.int32, x.shape, 0)
    s = 1
    while s < n:
        x = x + jnp.where(row < n - s, pltpu.roll(x, n - s, 0), 0.0)
        s *= 2
    return x


def _chunk_prep(zq, zf, lb, b_ref):
    n_sub = CHUNK // SUB
    sq = _sigmoid(zq)
    q = zq * sq
    sf = _sigmoid(zf)
    f = lb + (1.0 - lb) * sf
    k = 1.0 - f
    b = _cumsum_rows(jnp.log(f))
    b_ref[...] = b
    shape = (SUB, b.shape[1])
    ends = [jnp.broadcast_to(b_ref[pl.ds(SUB * j + SUB - 1, 1), :], shape) for j in range(n_sub)]
    mids = [jnp.broadcast_to(b_ref[pl.ds(SUB * j + SUB // 2 - 1, 1), :], shape) for j in range(n_sub)]
    own = [b[SUB * j:SUB * (j + 1), :] for j in range(n_sub)]
    m0 = jnp.concatenate(mids, axis=0)
    e1 = jnp.concatenate(ends, axis=0)
    eq = [jnp.exp(jnp.minimum(b - m0, EXP_CLAMP))]
    for d in range(1, n_sub):
        rd = jnp.concatenate([own[j] if j < d else ends[j - d] for j in range(n_sub)], axis=0)
        eq.append(jnp.exp(b - rd))
    ek0 = jnp.exp(jnp.minimum(m0 - b, EXP_CLAMP))
    ek1 = jnp.exp(e1 - b)
    b_last = b_ref[pl.ds(CHUNK - 1, 1), :]
    return dict(q=q, k=k, f=f, sq=sq, sf=sf, b=b, eq=eq, ek0=ek0, ek1=ek1,
                eb=jnp.exp(b), ekl=jnp.exp(b_last - b), el=jnp.exp(b_last))


def _chunk_masks():
    ti = lax.broadcasted_iota(jnp.int32, (CHUNK, CHUNK), 0)
    si = lax.broadcasted_iota(jnp.int32, (CHUNK, CHUNK), 1)
    shift = SUB.bit_length() - 1
    dsub = jnp.right_shift(ti, shift) - jnp.right_shift(si, shift)
    masks = [(dsub == 0) & (si <= ti)]
    masks += [dsub == d for d in range(1, CHUNK // SUB)]
    return masks


def _chunk_attn(p, masks):
    qd = [(p["q"] * e).astype(BF16) for e in p["eq"]]
    k0 = (p["k"] * p["ek0"]).astype(BF16)
    k1 = (p["k"] * p["ek1"]).astype(BF16)
    a = jnp.where(masks[0], _dot_nt(qd[0], k0), 0.0)
    for d in range(1, len(masks)):
        a = jnp.where(masks[d], _dot_nt(qd[d], k1), a)
    return a, qd, k0, k1


def _hgrn_fwd(z, lb, norm_g, *, name, plan=None):
    S = z.shape[0]
    HW = lb.shape[1]
    NH = HW // LANES
    T = _row_tile(S, 512)
    nc = T // CHUNK

    def body(zq_ref, zf_ref, zi_ref, zo_ref, lb_ref, ng_ref, o_ref, of_ref, st_ref, s_scr, b_scr):
        @pl.when(pl.program_id(0) == 0)
        def _():
            s_scr[...] = jnp.zeros_like(s_scr)

        ng = ng_ref[...]
        masks = _chunk_masks()

        def chunk(c, carry):
            rows = pl.ds(pl.multiple_of(c * CHUNK, CHUNK), CHUNK)
            for h in range(NH):
                cols = slice(h * LANES, (h + 1) * LANES)
                p = _chunk_prep(zq_ref[rows, cols], zf_ref[rows, cols], lb_ref[:, cols], b_scr.at[h])
                v = zi_ref[rows, cols].astype(BF16)
                zo = zo_ref[rows, cols]
                st = s_scr[h]
                st_ref[c, h] = st
                a, _, _, _ = _chunk_attn(p, masks)
                o = _dot(a.astype(BF16), v) + _dot_nt((p["q"] * p["eb"]).astype(BF16), st.astype(BF16))
                s_scr[h] = st * p["el"] + _dot_tn(v, (p["k"] * p["ekl"]).astype(BF16))
                o_ref[rows, cols] = o
                r = lax.rsqrt(jnp.mean(o * o, axis=-1, keepdims=True) + EPS)
                of_ref[rows, cols] = (o * r * ng * (zo * _sigmoid(zo))).astype(BF16)
            return carry

        lax.fori_loop(0, nc, chunk, 0, unroll=8)

    part = lambda k: pl.BlockSpec((T, HW), lambda i, k=k: (i, k))
    return _call(
        body, name=name, grid=(S // T,), plan=plan,
        in_specs=[part(1), part(2), part(3), part(4),
                  pl.BlockSpec((1, HW), lambda i: (0, 0)), pl.BlockSpec((1, LANES), lambda i: (0, 0))],
        out_specs=[pl.BlockSpec((T, HW), lambda i: (i, 0)), pl.BlockSpec((T, HW), lambda i: (i, 0)),
                   pl.BlockSpec((nc, NH, LANES, LANES), lambda i: (i, 0, 0, 0))],
        out_shape=[jax.ShapeDtypeStruct((S, HW), F32), jax.ShapeDtypeStruct((S, HW), BF16),
                   jax.ShapeDtypeStruct((S // CHUNK, NH, LANES, LANES), F32)],
        scratch=[pltpu.VMEM((NH, LANES, LANES), F32), pltpu.VMEM((NH, CHUNK, LANES), F32)],
        args=(z, z, z, z, lb, norm_g))


def _hgrn_bwd(z, lb, norm_g, o_raw, states, dof, *, name, plan=None):
    S = z.shape[0]
    HW = lb.shape[1]
    NH = HW // LANES
    T = _row_tile(S, 512)
    nc = T // CHUNK
    nt = S // T

    def body(zq_ref, zf_ref, zi_ref, zo_ref, lb_ref, ng_ref, o_ref, st_ref, dof_ref,
             dzq_ref, dzf_ref, dzi_ref, dzo_ref, dlb_ref, dng_ref, ds_scr, b_scr):
        @pl.when(pl.program_id(0) == 0)
        def _():
            ds_scr[...] = jnp.zeros_like(ds_scr)
            dlb_ref[...] = jnp.zeros_like(dlb_ref)
            dng_ref[...] = jnp.zeros_like(dng_ref)

        ng = ng_ref[...]
        masks = _chunk_masks()
        last_row = lax.broadcasted_iota(jnp.int32, (CHUNK, 1), 0) == CHUNK - 1

        def chunk(cr, carry):
            c = nc - 1 - cr
            rows = pl.ds(pl.multiple_of(c * CHUNK, CHUNK), CHUNK)
            for h in range(NH):
                cols = slice(h * LANES, (h + 1) * LANES)
                lbv = lb_ref[:, cols]
                zq, zf, zo = zq_ref[rows, cols], zf_ref[rows, cols], zo_ref[rows, cols]
                o = o_ref[rows, cols]
                dof_c = dof_ref[rows, cols]
                st = st_ref[c, h]
                dst = ds_scr[h]

                so = _sigmoid(zo)
                r = lax.rsqrt(jnp.mean(o * o, axis=-1, keepdims=True) + EPS)
                ohat = o * r
                d_on = dof_c * (zo * so)
                dzo_ref[rows, cols] = (dof_c * ohat * ng * _dsilu(zo, so)).astype(BF16)
                dng_ref[:, cols] += jnp.sum(d_on * ohat, axis=0, keepdims=True)
                dohat = d_on * ng
                do = (r * (dohat - ohat * jnp.mean(dohat * ohat, axis=-1, keepdims=True))).astype(BF16)

                p = _chunk_prep(zq, zf, lbv, b_scr.at[h])
                q, k = p["q"], p["k"]
                v = zi_ref[rows, cols].astype(BF16)
                a, qd, k0, k1 = _chunk_attn(p, masks)
                ktl = (k * p["ekl"]).astype(BF16)
                dstb = dst.astype(BF16)

                da = _dot_nt(do, v)
                dzi_ref[rows, cols] = (_dot_tn(a.astype(BF16), do) + _dot_nt(ktl, dstb)).astype(BF16)

                da0 = jnp.where(masks[0], da, 0.0).astype(BF16)
                rq = _dot(da0, k0)
                rk0 = _dot_tn(da0, qd[0])
                dq = rq * p["eq"][0]
                db = qd[0].astype(F32) * rq - k0.astype(F32) * rk0
                rk1 = jnp.zeros_like(rk0)
                for d in range(1, len(masks)):
                    dad = jnp.where(masks[d], da, 0.0).astype(BF16)
                    rq = _dot(dad, k1)
                    dq = dq + rq * p["eq"][d]
                    db = db + qd[d].astype(F32) * rq
                    rk1 = rk1 + _dot_tn(dad, qd[d])
                dk = rk0 * p["ek0"] + rk1 * p["ek1"]
                db = db - k1.astype(F32) * rk1
                qe = (q * p["eb"]).astype(BF16)
                rq = _dot(do, st.astype(BF16))
                dq = dq + rq * p["eb"]
                db = db + qe.astype(F32) * rq
                rk = _dot(v, dstb)
                dk = dk + rk * p["ekl"]
                db = db - ktl.astype(F32) * rk

                st_new = st * p["el"] + _dot_tn(v, ktl)
                db = db + jnp.where(last_row, jnp.sum(dstb.astype(F32) * st_new, axis=0, keepdims=True), 0.0)
                dg = _rev_cumsum_rows(db)
                ds_scr[h] = dst * p["el"] + _dot_tn(do, qe)

                dzq_ref[rows, cols] = (dq * _dsilu(zq, p["sq"])).astype(BF16)
                df = dg / p["f"] - dk
                sf = p["sf"]
                dzf_ref[rows, cols] = (df * (1.0 - lbv) * sf * (1.0 - sf)).astype(BF16)
                dlb_ref[:, cols] += jnp.sum(df * (1.0 - sf), axis=0, keepdims=True)
            return carry

        lax.fori_loop(0, nc, chunk, 0, unroll=4)

    rev = lambda i: nt - 1 - i
    part = lambda k: pl.BlockSpec((T, HW), lambda i, k=k: (rev(i), k))
    blk = pl.BlockSpec((T, HW), lambda i: (rev(i), 0))
    vec = pl.BlockSpec((1, HW), lambda i: (0, 0))
    return _call(
        body, name=name, grid=(nt,), plan=plan,
        in_specs=[part(1), part(2), part(3), part(4), vec, pl.BlockSpec((1, LANES), lambda i: (0, 0)),
                  blk, pl.BlockSpec((nc, NH, LANES, LANES), lambda i: (rev(i), 0, 0, 0)), blk],
        out_specs=[blk, blk, blk, blk, vec, vec],
        out_shape=[jax.ShapeDtypeStruct((S, HW), BF16)] * 4 + [jax.ShapeDtypeStruct((1, HW), F32)] * 2,
        scratch=[pltpu.VMEM((NH, LANES, LANES), F32), pltpu.VMEM((NH, CHUNK, LANES), F32)],
        args=(z, z, z, z, lb, norm_g, o_raw, states, dof))


def _gate_specs(T, D, rows=lambda i: i):
    half = D // 2
    return [pl.BlockSpec((T, half), lambda i, k=k: (rows(i), 5 + k)) for k in range(4)]


def _gates(zg_refs, bg_ref, D):
    half = D // 2
    za = jnp.concatenate([zg_refs[0][...], zg_refs[1][...]], axis=1) + bg_ref[:, :D]
    zb = jnp.concatenate([zg_refs[2][...], zg_refs[3][...]], axis=1) + bg_ref[:, D:]
    return _sigmoid(za), _sigmoid(zb)


def _mix_fwd(x, of, z, pool_w, pool_scale, b_gate, w_pa4, w_pb4, w_o4, g_next, *, name, plan=None):
    S, D = x.shape
    P = of.shape[1]
    T = _row_tile(S, 512)
    hb = T // HALO_POOL

    def body(x_ref, u_ref, halo_ref, of_ref, g0, g1, g2, g3, pw_ref, ps_ref, bg_ref, wpa_ref, wpb_ref, wo_ref,
             gn_ref, xo_ref, ya_ref, yb_ref, xn_ref, pm_ref):
        i = pl.program_id(0)
        pmv = _pool_mix(u_ref[...], halo_ref[...], i == 0, i * T, pw_ref, ps_ref)
        pm_ref[...] = pmv
        ofv = of_ref[...]
        ya = jnp.concatenate([_dot(pmv, wpa_ref[k]) for k in range(N_CHIPS)], axis=1)
        yb = jnp.concatenate([_dot(ofv, wpb_ref[k]) for k in range(N_CHIPS)], axis=1)
        ga, gb = _gates((g0, g1, g2, g3), bg_ref, D)
        merged = (ga * ya + gb * yb).astype(BF16)
        x_mid = x_ref[...] + _dot(merged, wo_ref[...].reshape(D, D))
        xo_ref[...] = x_mid
        xn_ref[...] = _rms(x_mid, gn_ref[...])
        ya_ref[...] = ya.astype(BF16)
        yb_ref[...] = yb.astype(BF16)

    row = lambda w: pl.BlockSpec((T, w), lambda i: (i, 0))
    full = lambda a: pl.BlockSpec(a.shape, lambda i: (0,) * a.ndim)
    return _call(
        body, name=name, grid=(S // T,), parallel=(0,), plan=plan,
        in_specs=[row(D), row(P), pl.BlockSpec((HALO_POOL, P), lambda i: (jnp.maximum(i * hb - 1, 0), 0)), row(P)]
        + _gate_specs(T, D) + [full(pool_w), full(pool_scale), full(b_gate), full(w_pa4), full(w_pb4), full(w_o4),
                               full(g_next)],
        out_specs=[row(D), row(D), row(D), row(D), row(P)],
        out_shape=[jax.ShapeDtypeStruct((S, D), F32), jax.ShapeDtypeStruct((S, D), BF16),
                   jax.ShapeDtypeStruct((S, D), BF16), jax.ShapeDtypeStruct((S, D), BF16),
                   jax.ShapeDtypeStruct((S, P), BF16)],
        args=(x, z, z, of, z, z, z, z, pool_w, pool_scale, b_gate, w_pa4, w_pb4, w_o4, g_next))


def _mix_bwd(dxm, ya, yb, z, b_gate, pm, of, pool_w, pool_scale, w_pa4, w_pb4, w_o4, *, name):
    S, D = dxm.shape
    P = pm.shape[1]
    q4 = D // N_CHIPS
    T = _row_tile(S, 512)
    nt = S // T
    hb = T // HALO_POOL

    def body(dx_ref, ya_ref, yb_ref, g0, g1, g2, g3, bg_ref, pm_ref, of_ref, u_ref, halo_ref, pw_ref, ps_ref,
             wpa_ref, wpb_ref, wo_ref,
             dzg_ref, dof_ref, du_ref, dwo_ref, dwpa_ref, dwpb_ref, dbg_ref, dpw_ref, dps_ref,
             dwo16_ref, dwpa16_ref, dwpb16_ref, next_scr):
        i = pl.program_id(0)

        @pl.when(i == 0)
        def _():
            for ref in (dwo_ref, dwpa_ref, dwpb_ref, dbg_ref, dpw_ref, dps_ref, next_scr):
                ref[...] = jnp.zeros_like(ref)

        dxb = dx_ref[...].astype(BF16)
        ya = ya_ref[...].astype(F32)
        yb = yb_ref[...].astype(F32)
        ga, gb = _gates((g0, g1, g2, g3), bg_ref, D)
        merged = (ga * ya + gb * yb).astype(BF16)
        dwo_ref[...] += _dot_tn(merged, dxb).reshape(N_CHIPS, q4, D)
        dm = _dot_nt(dxb, wo_ref[...].reshape(D, D))
        dza = dm * ya * ga * (1.0 - ga)
        dzb = dm * yb * gb * (1.0 - gb)
        dzg_ref[:, :D] = dza.astype(BF16)
        dzg_ref[:, D:] = dzb.astype(BF16)
        dbg_ref[:, :D] += jnp.sum(dza, axis=0, keepdims=True)
        dbg_ref[:, D:] += jnp.sum(dzb, axis=0, keepdims=True)
        dya = (dm * ga).astype(BF16)
        dyb = (dm * gb).astype(BF16)
        pmv, ofv = pm_ref[...], of_ref[...]
        dpm = jnp.zeros((T, P), F32)
        dof = jnp.zeros((T, P), F32)
        for k in range(N_CHIPS):
            cols = slice(k * q4, (k + 1) * q4)
            dwpa_ref[k] += _dot_tn(pmv, dya[:, cols])
            dwpb_ref[k] += _dot_tn(ofv, dyb[:, cols])
            dpm = dpm + _dot_nt(dya[:, cols], wpa_ref[k])
            dof = dof + _dot_nt(dyb[:, cols], wpb_ref[k])
        dof_ref[...] = dof
        _pool_grad(u_ref[...], halo_ref[...], i == nt - 1, (nt - 1 - i) * T, dpm, next_scr[...], pw_ref, ps_ref,
                   du_ref, dpw_ref, dps_ref)
        next_scr[...] = dpm[:HALO_POOL, :]

        @pl.when(i == nt - 1)
        def _():
            dwo16_ref[...] = dwo_ref[...].astype(BF16)
            dwpa16_ref[...] = dwpa_ref[...].astype(BF16)
            dwpb16_ref[...] = dwpb_ref[...].astype(BF16)

    rev = lambda i: nt - 1 - i
    row = lambda w: pl.BlockSpec((T, w), lambda i: (rev(i), 0))
    full = lambda a: pl.BlockSpec(a.shape, lambda i: (0,) * a.ndim, pipeline_mode=pl.Buffered(1))
    like = lambda a, dt: jax.ShapeDtypeStruct(a.shape, dt)
    return _call(
        body, name=name, grid=(nt,),
        in_specs=[row(D), row(D), row(D)] + _gate_specs(T, D, rev)
        + [full(b_gate), row(P), row(P), row(P),
           pl.BlockSpec((HALO_POOL, P), lambda i: (jnp.maximum(rev(i) * hb - 1, 0), 0)),
           full(pool_w), full(pool_scale), full(w_pa4), full(w_pb4), full(w_o4)],
        out_specs=[row(2 * D), row(P), row(P), full(w_o4), full(w_pa4), full(w_pb4), full(b_gate),
                   full(pool_w), full(pool_scale), full(w_o4), full(w_pa4), full(w_pb4)],
        out_shape=[jax.ShapeDtypeStruct((S, 2 * D), BF16), jax.ShapeDtypeStruct((S, P), F32),
                   jax.ShapeDtypeStruct((S, P), BF16), like(w_o4, F32), like(w_pa4, F32), like(w_pb4, F32),
                   like(b_gate, F32), like(pool_w, F32), like(pool_scale, F32),
                   like(w_o4, BF16), like(w_pa4, BF16), like(w_pb4, BF16)],
        scratch=[pltpu.VMEM((HALO_POOL, P), F32)],
        args=(dxm, ya, yb, z, z, z, z, b_gate, pm, of, z, z, pool_w, pool_scale, w_pa4, w_pb4, w_o4))[0]


def _up_conv(xn, w_up4, conv_w, conv_b, *, name, plan=None):
    S, D = xn.shape
    f4 = w_up4.shape[2]
    nf = N_CHIPS // 2
    F = nf * f4
    T = _row_tile(S, 512)

    def body(xn_ref, wv_ref, wg_ref, cwv_ref, cwg_ref, cbv_ref, cbg_ref,
             hv_ref, hg_ref, val_ref, gate_ref, a_ref, pv_scr, pg_scr):
        i = pl.program_id(1)
        xv = xn_ref[...]

        def side(w_ref, cw_ref, cb_ref, h_ref, p_scr):
            h = _dot(xv, w_ref[...])
            h_ref[...] = h.astype(BF16)
            hp = jnp.concatenate([jnp.where(i == 0, 0.0, p_scr[...]), h], axis=0)
            p_scr[...] = h[-SUBLANES:, :]
            cw = cw_ref[...]
            return cw[0:1, :] * _shift_down(hp, 2) + cw[1:2, :] * _shift_down(hp, 1) + cw[2:3, :] * h + cb_ref[...]

        val = side(wv_ref, cwv_ref, cbv_ref, hv_ref, pv_scr)
        gate = side(wg_ref, cwg_ref, cbg_ref, hg_ref, pg_scr)
        val_ref[...] = val.astype(BF16)
        gate_ref[...] = gate.astype(BF16)
        a_ref[...] = (gate * _sigmoid(gate) * val).astype(BF16)

    out = pl.BlockSpec((T, f4), lambda f, i: (i, f))
    return _call(
        body, name=name, grid=(nf, S // T), plan=plan,
        in_specs=[pl.BlockSpec((T, D), lambda f, i: (i, 0)),
                  pl.BlockSpec((None, D, f4), lambda f, i: (f, 0, 0)),
                  pl.BlockSpec((None, D, f4), lambda f, i: (nf + f, 0, 0)),
                  pl.BlockSpec((3, f4), lambda f, i: (0, f)),
                  pl.BlockSpec((3, f4), lambda f, i: (0, nf + f)),
                  pl.BlockSpec((1, f4), lambda f, i: (0, f)),
                  pl.BlockSpec((1, f4), lambda f, i: (0, nf + f))],
        out_specs=[out] * 5,
        out_shape=[jax.ShapeDtypeStruct((S, F), BF16)] * 5,
        scratch=[pltpu.VMEM((SUBLANES, f4), F32), pltpu.VMEM((SUBLANES, f4), F32)],
        args=(xn, w_up4, w_up4, conv_w, conv_w, conv_b, conv_b))


def _down(a, w_down4, x, g_next, *, name, plan=None):
    S, F = a.shape
    D = x.shape[1]
    T = _row_tile(S, 1024)

    def body(a_ref, wd_ref, x_ref, gn_ref, o_ref, xn_ref):
        x_out = x_ref[...] + _dot(a_ref[...], wd_ref[...].reshape(F, D))
        o_ref[...] = x_out
        xn_ref[...] = _rms(x_out, gn_ref[...])

    row = lambda w: pl.BlockSpec((T, w), lambda i: (i, 0))
    return _call(
        body, name=name, grid=(S // T,), parallel=(0,), plan=plan,
        in_specs=[row(F), pl.BlockSpec(w_down4.shape, lambda i: (0, 0, 0), pipeline_mode=pl.Buffered(1)),
                  row(D), pl.BlockSpec((1, D), lambda i: (0, 0))],
        out_specs=[row(D), row(D)],
        out_shape=[jax.ShapeDtypeStruct((S, D), F32), jax.ShapeDtypeStruct((S, D), BF16)],
        args=(a, w_down4, x, g_next))


def _ffn_down_bwd(dxo, hv, hg, val16, gate16, conv_w, w_down4, *, name, plan=None):
    S = hv.shape[0]
    _, f4, D = w_down4.shape
    F = N_CHIPS * f4
    T = _row_tile(S, 512)
    nf = 2
    tf = 2 * f4
    nt = S // T

    def body(dx_ref, hv_ref, hg_ref, val_ref, gate_ref, cwv_ref, cwg_ref, wd_ref,
             dhv_ref, dhg_ref, dwd_ref, dwd16_ref, dcwv_ref, dcwg_ref, dcbv_ref, dcbg_ref, cv_scr, cg_scr):
        i = pl.program_id(1)

        @pl.when(i == 0)
        def _():
            cv_scr[...] = jnp.zeros_like(cv_scr)
            cg_scr[...] = jnp.zeros_like(cg_scr)
            dwd_ref[...] = jnp.zeros_like(dwd_ref)
            dcwv_ref[...] = jnp.zeros_like(dcwv_ref)
            dcwg_ref[...] = jnp.zeros_like(dcwg_ref)
            dcbv_ref[...] = jnp.zeros_like(dcbv_ref)
            dcbg_ref[...] = jnp.zeros_like(dcbg_ref)

        dxb = dx_ref[...].astype(BF16)
        val = val_ref[...].astype(F32)
        gate = gate_ref[...].astype(F32)
        sg = _sigmoid(gate)
        sil = gate * sg
        dwd_ref[...] += _dot_tn((sil * val).astype(BF16), dxb).reshape(2, f4, D)
        da = _dot_nt(dxb, wd_ref[...].reshape(tf, D))

        def conv_bwd(dhc, h0, cw, c_scr, dh_ref, dcw_ref, dcb_ref):
            ext = jnp.concatenate([dhc, c_scr[...]], axis=0)
            n1 = _shift_up(ext, 1)
            n2 = _shift_up(ext, 2)
            dh_ref[...] = (cw[2:3, :] * dhc + cw[1:2, :] * n1 + cw[0:1, :] * n2).astype(BF16)
            c_scr[...] = dhc[:SUBLANES, :]
            dcw_ref[0:1, :] += jnp.sum(n2 * h0, axis=0, keepdims=True)
            dcw_ref[1:2, :] += jnp.sum(n1 * h0, axis=0, keepdims=True)
            dcw_ref[2:3, :] += jnp.sum(dhc * h0, axis=0, keepdims=True)
            dcb_ref[...] += jnp.sum(dhc, axis=0, keepdims=True)

        conv_bwd(da * sil, hv_ref[...].astype(F32), cwv_ref[...], cv_scr, dhv_ref, dcwv_ref, dcbv_ref)
        conv_bwd(da * val * _dsilu(gate, sg), hg_ref[...].astype(F32), cwg_ref[...], cg_scr, dhg_ref, dcwg_ref,
                 dcbg_ref)

        @pl.when(i == nt - 1)
        def _():
            dwd16_ref[...] = dwd_ref[...].astype(BF16)

    rev = lambda i: nt - 1 - i
    wd_spec = pl.BlockSpec((2, f4, D), lambda f, i: (f, 0, 0))
    return _call(
        body, name=name, grid=(nf, nt), plan=plan,
        in_specs=[pl.BlockSpec((T, D), lambda f, i: (rev(i), 0)),
                  pl.BlockSpec((T, tf), lambda f, i: (rev(i), f)),
                  pl.BlockSpec((T, tf), lambda f, i: (rev(i), f)),
                  pl.BlockSpec((T, tf), lambda f, i: (rev(i), f)),
                  pl.BlockSpec((T, tf), lambda f, i: (rev(i), f)),
                  pl.BlockSpec((3, tf), lambda f, i: (0, f)),
                  pl.BlockSpec((3, tf), lambda f, i: (0, nf + f)),
                  wd_spec],
        out_specs=[pl.BlockSpec((T, tf), lambda f, i: (rev(i), f)),
                   pl.BlockSpec((T, tf), lambda f, i: (rev(i), f)),
                   wd_spec, wd_spec,
                   pl.BlockSpec((3, tf), lambda f, i: (0, f)),
                   pl.BlockSpec((3, tf), lambda f, i: (0, f)),
                   pl.BlockSpec((1, tf), lambda f, i: (0, f)),
                   pl.BlockSpec((1, tf), lambda f, i: (0, f))],
        out_shape=[jax.ShapeDtypeStruct((S, F), BF16), jax.ShapeDtypeStruct((S, F), BF16),
                   jax.ShapeDtypeStruct((N_CHIPS, f4, D), F32), jax.ShapeDtypeStruct((N_CHIPS, f4, D), BF16),
                   jax.ShapeDtypeStruct((3, F), F32), jax.ShapeDtypeStruct((3, F), F32),
                   jax.ShapeDtypeStruct((1, F), F32), jax.ShapeDtypeStruct((1, F), F32)],
        scratch=[pltpu.VMEM((SUBLANES, tf), F32), pltpu.VMEM((SUBLANES, tf), F32)],
        args=(dxo, hv, hg, val16, gate16, conv_w, conv_w, w_down4))


def _down_loss(a, w_down4, x, g, target, *, name):
    S, F = a.shape
    D = x.shape[1]
    T = _row_tile(S, 512)

    def body(a_ref, wd_ref, x_ref, g_ref, t_ref, loss_ref, dx_ref, dg_ref):
        @pl.when(pl.program_id(0) == 0)
        def _():
            loss_ref[...] = jnp.zeros_like(loss_ref)
            dg_ref[...] = jnp.zeros_like(dg_ref)

        xf = x_ref[...] + _dot(a_ref[...], wd_ref[...].reshape(F, D))
        r = lax.rsqrt(jnp.mean(xf * xf, axis=-1, keepdims=True) + EPS)
        xhat = xf * r
        err = xhat * g_ref[...] - t_ref[...]
        loss_ref[...] += jnp.sum(err * err, axis=0, keepdims=True) * (0.5 / D)
        dy = err * (1.0 / D)
        dxhat = dy * g_ref[...]
        dx_ref[...] = r * (dxhat - xhat * jnp.mean(dxhat * xhat, axis=-1, keepdims=True))
        dg_ref[...] += jnp.sum(dy * xhat, axis=0, keepdims=True)

    row = lambda w: pl.BlockSpec((T, w), lambda i: (i, 0))
    vec = pl.BlockSpec((1, D), lambda i: (0, 0))
    return _call(
        body, name=name, grid=(S // T,),
        in_specs=[row(F), pl.BlockSpec(w_down4.shape, lambda i: (0, 0, 0), pipeline_mode=pl.Buffered(1)),
                  row(D), vec, row(D)],
        out_specs=[vec, row(D), vec],
        out_shape=[jax.ShapeDtypeStruct((1, D), F32), jax.ShapeDtypeStruct((S, D), F32),
                   jax.ShapeDtypeStruct((1, D), F32)],
        args=(a, w_down4, x, g, target))[0]


BIG = ("w_in", "w_pa", "w_pb", "w_o", "w_up", "w_down")
SMALL = ("norm1_g", "b_gate", "pool_w", "pool_scale", "lb_logits", "hgrn_norm_g", "norm2_g", "conv_b", "final_g")
WEIGHTS = ("norm1_g", "w_in", "b_gate", "pool_w", "pool_scale", "lb_logits", "hgrn_norm_g", "w_pa", "w_pb", "w_o",
           "norm2_g", "w_up", "conv_w", "conv_b", "w_down", "final_g")


def _lower_bounds(lb_logits):
    soft = jax.nn.softmax(lb_logits.astype(F32), axis=0)
    cum = jnp.cumsum(soft, axis=0)
    return cum - cum[0:1]


def _step(x, target, sm, wts, shards=None):
    L = sm["norm1_g"].shape[0]
    wts = dict(wts)
    dist = shards is not None
    lbs, lb_vjp = jax.vjp(_lower_bounds, sm["lb_logits"])
    row = lambda a: a.reshape(1, -1)
    conv_w = sm.get("conv_w")

    def gather(names_layers, with_conv=False):
        items = [(shards[n], "rows", l) for n, l in names_layers]
        if with_conv:
            items.append((shards["conv_w"], "layer", None))
        return _GatherPlan(items)

    def landed(names_layers, outs):
        for key, arr in zip(names_layers, outs):
            wts[key] = arr

    own = {"in_proj": ("w_up",)}
    first = {"hgrn_fwd": ("w_pa", "w_pb", "w_o"), "mix_fwd": ("w_down",)}
    ahead = {"up": ("w_in", "w_pa", "w_pb", "w_o"), "down": ("w_down",)}
    conv_rider = "mix_fwd"

    def riders(l, kernel):
        if not dist:
            return [], None
        keys = [(n, l) for n in own.get(kernel, ())]
        keys += [(n, l) for n in first.get(kernel, ())] if l == 0 else []
        keys += [(n, l + 1) for n in ahead.get(kernel, ())] if l + 1 < L else []
        with_conv = l == 0 and kernel == conv_rider
        return keys, (gather(keys, with_conv) if keys or with_conv else None)

    keys = [("w_in", 0)] if dist else []
    (xn1,), got = _rmsnorm(x, row(sm["norm1_g"][0]), name="norm_in", plan=gather(keys) if keys else None)
    landed(keys, got)

    saved = []
    for l in range(L):
        keys, plan = riders(l, "in_proj")
        (z,), got = _matmul(xn1, wts[("w_in", l)], name=f"in_proj_{l}", plan=plan)
        landed(keys, got)
        keys, plan = riders(l, "hgrn_fwd")
        (o_raw, of, states), got = _hgrn_fwd(z, row(lbs[l]), row(sm["hgrn_norm_g"][l]), name=f"hgrn_fwd_{l}",
                                             plan=plan)
        landed(keys, got)
        keys, plan = riders(l, "mix_fwd")
        (x_mid, ya, yb, xn2, pm), got = _mix_fwd(
            x, of, z, sm["pool_w"][l], row(sm["pool_scale"][l]), row(sm["b_gate"][l]), wts[("w_pa", l)],
            wts[("w_pb", l)], wts[("w_o", l)], row(sm["norm2_g"][l]), name=f"mix_fwd_{l}", plan=plan)
        landed(keys, got)
        if dist and l == 0:
            full = got[-1]
            conv_w = jnp.concatenate([full[:, k] for k in range(N_CHIPS)], axis=2)
        keys, plan = riders(l, "up")
        (hv, hg, val16, gate16, a16), got = _up_conv(xn2, wts[("w_up", l)], conv_w[l], row(sm["conv_b"][l]),
                                                     name=f"up_{l}", plan=plan)
        landed(keys, got)
        saved.append(dict(x=x, xn1=xn1, z=z, pm=pm, o_raw=o_raw, of=of, states=states,
                          x_mid=x_mid, ya=ya, yb=yb, xn2=xn2, hv=hv, hg=hg, val16=val16, gate16=gate16))
        if l + 1 < L:
            keys, plan = riders(l, "down")
            (x, xn1), got = _down(a16, wts[("w_down", l)], x_mid, row(sm["norm1_g"][l + 1]), name=f"down_{l}",
                                  plan=plan)
            landed(keys, got)
        else:
            loss_cols, dx, d_final_g = _down_loss(a16, wts[("w_down", l)], x_mid, row(sm["final_g"]), target,
                                                  name="down_loss")

    small = {k: [None] * L for k in ("norm1_g", "b_gate", "pool_w", "pool_scale", "hgrn_norm_g", "norm2_g",
                                     "conv_w", "conv_b")}
    big32, big16, recv = {}, {}, {}
    dlbs = [None] * L
    pending = []

    def scatter():
        if not (dist and pending):
            return [], None
        keys = list(pending)
        del pending[:]
        return keys, _ScatterPlan([big16[k] for k in keys])

    def sent(keys, outs):
        for key, arr in zip(keys, outs):
            recv[key] = arr

    def made(name, l, g32, g16):
        big32[(name, l)], big16[(name, l)] = g32, g16
        pending.append((name, l))

    for l in reversed(range(L)):
        s = saved[l]
        keys, plan = scatter()
        (dhv, dhg, d_wd, d_wd16, dcwv, dcwg, dcbv, dcbg), got = _ffn_down_bwd(
            dx, s["hv"], s["hg"], s["val16"], s["gate16"], conv_w[l], wts[("w_down", l)], name=f"down_bwd_{l}",
            plan=plan)
        sent(keys, got)
        made("w_down", l, d_wd, d_wd16)
        small["conv_w"][l] = jnp.concatenate([dcwv, dcwg], axis=1)
        small["conv_b"][l] = jnp.concatenate([dcbv, dcbg], axis=1)[0]
        keys, plan = scatter()
        (d_wu, d_wu16), got = _wgrad(s["xn2"], [dhv, dhg], name=f"up_wgrad_{l}", rows=2048, plan=plan)
        sent(keys, got)
        made("w_up", l, d_wu, d_wu16)
        (dxm, dg2), _ = _dgrad_norm([dhv, dhg], wts[("w_up", l)], s["x_mid"], row(sm["norm2_g"][l]), dx,
                                    name=f"up_dgrad_{l}")
        small["norm2_g"][l] = dg2[0]

        dzg, dof, du, d_wo, d_wpa, d_wpb, dbg, dpw, dps, d_wo16, d_wpa16, d_wpb16 = _mix_bwd(
            dxm, s["ya"], s["yb"], s["z"], row(sm["b_gate"][l]), s["pm"], s["of"], sm["pool_w"][l],
            row(sm["pool_scale"][l]), wts[("w_pa", l)], wts[("w_pb", l)], wts[("w_o", l)], name=f"mix_bwd_{l}")
        small["pool_w"][l], small["pool_scale"][l] = dpw, dps[0]
        made("w_o", l, d_wo, d_wo16)
        made("w_pa", l, d_wpa, d_wpa16)
        made("w_pb", l, d_wpb, d_wpb16)
        small["b_gate"][l] = dbg[0]

        keys, plan = scatter()
        (dzq, dzf, dzi, dzo, dlb, dng), got = _hgrn_bwd(s["z"], row(lbs[l]), row(sm["hgrn_norm_g"][l]), s["o_raw"],
                                                      s["states"], dof, name=f"hgrn_bwd_{l}", plan=plan)
        sent(keys, got)
        dlbs[l] = dlb[0]
        small["hgrn_norm_g"][l] = jnp.sum(dng.reshape(-1, LANES), axis=0)

        dz = [du, dzq, dzf, dzi, dzo, dzg]
        (d_wi, d_wi16), _ = _wgrad(s["xn1"], dz, name=f"in_wgrad_{l}", rows=1024)
        made("w_in", l, d_wi, d_wi16)
        keys, plan = scatter()
        (dx, dg1), got = _dgrad_norm(dz, wts[("w_in", l)], s["x"], row(sm["norm1_g"][l]), dxm,
                                     name=f"in_dgrad_{l}", plan=plan)
        sent(keys, got)
        small["norm1_g"][l] = dg1[0]

    out = {k: jnp.stack(v) for k, v in small.items()}
    out["lb_logits"] = lb_vjp(jnp.stack(dlbs))[0]
    out["final_g"] = d_final_g[0]
    return loss_cols, dx, out, big32, recv


def _elementwise_rows(R, n, n_arrays):
    if 2 * n_arrays * R * n * 4 <= VMEM_LIMIT // 4 or R % 8:
        return R
    block = VMEM_LIMIT // 2 // (2 * n_arrays)
    want = 8
    while want * 2 * n * 4 <= block:
        want *= 2
    return _row_tile(R, want)


def _sum_layers(own, got, chip, *, name):
    L = len(own)
    _, r, n = own[0].shape
    T = _elementwise_rows(r, n, 6)
    nt = r // T

    def body(chip_ref, *refs):
        o_ref = refs[-1]
        l = pl.program_id(0)
        for k in range(L):
            @pl.when(l == k)
            def _():
                own_ref, got_ref = refs[2 * k], refs[2 * k + 1]
                acc = own_ref[...]
                for j in range(3):
                    acc = acc + got_ref[j].astype(F32)
                o_ref[...] = acc

    in_specs = []
    for k in range(L):
        hold = 0 if k else nt - 1
        in_specs.append(pl.BlockSpec((None, T, n), lambda l, i, c, k=k, hold=hold: (c[0], jnp.where(l == k, i, hold), 0)))
        in_specs.append(pl.BlockSpec((3, T, n), lambda l, i, c, k=k, hold=hold: (0, jnp.where(l == k, i, hold), 0)))
    grid_spec = pltpu.PrefetchScalarGridSpec(
        num_scalar_prefetch=1, grid=(L, nt), in_specs=in_specs,
        out_specs=pl.BlockSpec((None, T, n), lambda l, i, c: (l, i, 0)))
    args = [a for pair in zip(own, got) for a in pair]
    return pl.pallas_call(
        body, name=name, grid_spec=grid_spec, out_shape=jax.ShapeDtypeStruct((L, r, n), F32),
        compiler_params=pltpu.CompilerParams(dimension_semantics=("arbitrary", "arbitrary"),
                                             vmem_limit_bytes=VMEM_LIMIT),
    )(chip, *args)


def _sum_stack(parts, *, name):
    K, R, n = parts.shape
    T = _elementwise_rows(R, n, K + 1)

    def body(p_ref, o_ref):
        acc = p_ref[0]
        for j in range(1, K):
            acc = acc + p_ref[j]
        o_ref[...] = acc

    return _call(
        body, name=name, grid=(R // T,), parallel=(0,),
        in_specs=[pl.BlockSpec((K, T, n), lambda i: (0, i, 0))],
        out_specs=[pl.BlockSpec((T, n), lambda i: (i, 0))],
        out_shape=[jax.ShapeDtypeStruct((R, n), F32)],
        args=(parts,))[0][0]


def _adamw(w, m, v, g_parts, *, name):
    R, n = w.shape
    n_g = len(g_parts)
    T = _elementwise_rows(R, n, 7 + n_g)

    def body(*refs):
        w_ref, m_ref, v_ref = refs[:3]
        g_refs = refs[3:3 + n_g]
        go_ref, d_ref, mo_ref, vo_ref = refs[3 + n_g:]
        g_ = g_refs[0][...]
        for r in g_refs[1:]:
            g_ = g_ + r[...]
        m_ = ADAM_B1 * m_ref[...] + (1.0 - ADAM_B1) * g_
        v_ = ADAM_B2 * v_ref[...] + (1.0 - ADAM_B2) * (g_ * g_)
        m_hat = m_ / (1.0 - ADAM_B1 ** ADAM_STEP)
        v_hat = v_ / (1.0 - ADAM_B2 ** ADAM_STEP)
        go_ref[...] = g_
        d_ref[...] = -ADAM_LR * (m_hat / (jnp.sqrt(v_hat) + ADAM_EPS) + ADAM_WD * w_ref[...])
        mo_ref[...] = m_
        vo_ref[...] = v_

    blk = pl.BlockSpec((T, n), lambda i: (i, 0))
    return _call(
        body, name=name, grid=(R // T,), parallel=(0,),
        in_specs=[blk] * (3 + n_g), out_specs=[blk] * 4,
        out_shape=[jax.ShapeDtypeStruct((R, n), F32)] * 4,
        args=(w, m, v, *g_parts))[0]


PACK_ALIGN = 8 * LANES


def _pack(pieces):
    flat = []
    for a in pieces:
        a = a.reshape(-1)
        pad = (-a.shape[0]) % PACK_ALIGN
        flat.append(jnp.pad(a, (0, pad)) if pad else a)
    return jnp.concatenate(flat).reshape(-1, LANES)


def _unpack(buf, shapes):
    flat = buf.reshape(-1)
    out, off = [], 0
    for shp in shapes:
        size = 1
        for s in shp:
            size *= s
        out.append(flat[off:off + size].reshape(shp))
        off += size + (-size) % PACK_ALIGN
    return out


def kernel(x, norm1_g, w_in, b_gate, pool_w, pool_scale, lb_logits, hgrn_norm_g, w_pa, w_pb, w_o, norm2_g, w_up, conv_w, conv_b, w_down, final_g, loss_target, m_norm1_g, m_w_in, m_b_gate, m_pool_w, m_pool_scale, m_lb_logits, m_hgrn_norm_g, m_w_pa, m_w_pb, m_w_o, m_norm2_g, m_w_up, m_conv_w, m_conv_b, m_w_down, m_final_g, v_norm1_g, v_w_in, v_b_gate, v_pool_w, v_pool_scale, v_lb_logits, v_hgrn_norm_g, v_w_pa, v_w_pb, v_w_o, v_norm2_g, v_w_up, v_conv_w, v_conv_b, v_w_down, v_final_g):
    env = dict(locals())
    w = {n: env[n] for n in WEIGHTS}
    m = {n: env["m_" + n] for n in WEIGHTS}
    v = {n: env["v_" + n] for n in WEIGHTS}
    my_chip = 2 * lax.axis_index("x") + lax.axis_index("y")
    L = w_in.shape[0]

    shards = {n: w[n].astype(BF16) for n in BIG}
    shards["conv_w"] = w["conv_w"]
    sm = {n: w[n] for n in SMALL}
    loss_cols, grad_x, g_small, big32, recv = _step(x[0], loss_target[0], sm, {}, shards)

    chip = my_chip.reshape(1).astype(jnp.int32)
    sums = [_sum_layers([big32[(n, l)] for l in range(L)], [recv[(n, l)] for l in range(L)], chip,
                        name="chip_sum_" + n) for n in BIG]
    small_names = list(SMALL)
    small_pieces = [g_small[n] for n in small_names] + [g_small["conv_w"], loss_cols]
    small_shapes = [a.shape for a in small_pieces]
    packed = _pack(small_pieces)
    Rs = packed.shape[0]
    swapped = _run_plan(_Together([_SiblingPlan(sums), _EveryonePlan(packed)]), name="tail_exchange")
    theirs, everyone = swapped[:-1], swapped[-1].reshape(8, Rs, LANES)
    g, delta, new_m, new_v = {}, {}, {}, {}
    for n, mine, other in zip(BIG, sums, theirs):
        shp = w[n].shape
        two_d = lambda a: a.reshape(-1, shp[-1])
        outs = _adamw(two_d(w[n]), two_d(m[n]), two_d(v[n]), [two_d(mine), two_d(other)], name="adamw_" + n)
        g[n], delta[n], new_m[n], new_v[n] = [a.reshape(shp) for a in outs]

    summed = _unpack(_sum_stack(everyone, name="small_sum"), small_shapes)
    loss = jnp.sum(summed[-1])
    cshard = w["conv_w"].shape[2]
    gs = dict(zip(small_names, summed[:len(small_names)]))
    g_cw = lax.dynamic_slice_in_dim(summed[-2], my_chip * cshard, cshard, axis=2)

    sm_out = _adamw(_pack([w[n] for n in small_names]), _pack([m[n] for n in small_names]),
                    _pack([v[n] for n in small_names]), [_pack([gs[n] for n in small_names])], name="adamw_small")
    shapes = [w[n].shape for n in small_names]
    for n, g_, d_, m_, v_ in zip(small_names, *[_unpack(a, shapes) for a in sm_out]):
        g[n], delta[n], new_m[n], new_v[n] = g_, d_, m_, v_
    shp = w["conv_w"].shape
    two_d = lambda a: a.reshape(-1, shp[-1])
    outs = _adamw(two_d(w["conv_w"]), two_d(m["conv_w"]), two_d(v["conv_w"]), [two_d(g_cw)], name="adamw_conv_w")
    g["conv_w"], delta["conv_w"], new_m["conv_w"], new_v["conv_w"] = [a.reshape(shp) for a in outs]

    return (loss, grad_x[None], *[g[n] for n in WEIGHTS], *[delta[n] for n in WEIGHTS],
            *[new_m[n] for n in WEIGHTS], *[new_v[n] for n in WEIGHTS])
```

```python
import jax
import jax.numpy as jnp
from jax import lax
from jax.experimental import pallas as pl
from jax.experimental.pallas import tpu as pltpu

F32 = jnp.float32
BF16 = jnp.bfloat16

EPS = 1e-6
CHUNK = 64
SUB = 32
LANES = 128
SUBLANES = 8
POOL_WINDOWS = (2, 4, 8, 16)
HALO_POOL = 16
EXP_CLAMP = 80.0

ADAM_LR = 0.001
ADAM_B1 = 0.9
ADAM_B2 = 0.999
ADAM_EPS = 1e-08
ADAM_WD = 0.01
ADAM_STEP = 10

VMEM_LIMIT = 56 * 1024 * 1024
MESH_ID = pl.DeviceIdType.MESH
N_CHIPS = 4
ANY = pl.BlockSpec(memory_space=pl.ANY)


def _dot(a, b):
    return jnp.dot(a, b, preferred_element_type=F32)


def _dot_nt(a, b):
    return lax.dot_general(a, b, (((1,), (1,)), ((), ())), preferred_element_type=F32)


def _dot_tn(a, b):
    return lax.dot_general(a, b, (((0,), (0,)), ((), ())), preferred_element_type=F32)


def _sigmoid(x):
    return jax.nn.sigmoid(x)


def _dsilu(x, s):
    return s * (1.0 + x * (1.0 - s))


def _row_tile(rows, want):
    t = min(rows, want)
    while rows % t:
        t //= 2
    return t


def _place():
    x, y, c = lax.axis_index("x"), lax.axis_index("y"), lax.axis_index("c")
    chips = [(1 - x, y), (x, 1 - y), (1 - x, 1 - y)]
    return x, y, c, chips


def _remote(src, dst, sems, k, to):
    return pltpu.make_async_remote_copy(src_ref=src, dst_ref=dst, send_sem=sems[0].at[k], recv_sem=sems[1].at[k],
                                        device_id=to, device_id_type=MESH_ID)


class _GatherPlan:
    def __init__(self, items):
        self.items = items
        self.inputs = [a for a, _, _ in items]
        self.out_shapes = []
        for a, kind, _ in items:
            shp = (N_CHIPS,) + a.shape[1:] if kind == "rows" else (a.shape[0], N_CHIPS) + a.shape[1:]
            self.out_shapes.append(jax.ShapeDtypeStruct(shp, a.dtype))
        n = len(items)
        self.scratch = [pltpu.SemaphoreType.DMA((6 * n,)), pltpu.SemaphoreType.DMA((6 * n,)),
                        pltpu.SemaphoreType.DMA((2 * n,))]

    def _views(self, i, src, dst):
        _, kind, l = self.items[i]
        if kind == "rows":
            half = src.shape[1] // 2
            part = lambda core: src.at[l, pl.ds(core * half, half), :]
            land = lambda chip, core: dst.at[chip, pl.ds(core * half, half), :]
        else:
            part = lambda core: src.at[core]
            land = lambda chip, core: dst.at[core, chip]
        return part, land

    def start(self, srcs, dsts, sems):
        x, y, c, chips = _place()
        me = 2 * x + y
        for i, (src, dst) in enumerate(zip(srcs, dsts)):
            part, land = self._views(i, src, dst)
            for core in range(2):
                pltpu.make_async_copy(part(core), land(me, core), sems[2].at[2 * i + core]).start()
            for j, (px, py) in enumerate(chips):
                _remote(part(c), land(me, c), sems, 6 * i + j, (px, py, c)).start()

    def finish(self, srcs, dsts, sems):
        x, y, c, chips = _place()
        me = 2 * x + y
        sibling = (x, y, 1 - c)
        for i, (src, dst) in enumerate(zip(srcs, dsts)):
            part, land = self._views(i, src, dst)
            for j, (px, py) in enumerate(chips):
                got = land(2 * px + py, c)
                _remote(got, got, sems, 6 * i + j, (px, py, c)).wait_recv()
                _remote(got, got, sems, 6 * i + 3 + j, sibling).start()
        for i, (src, dst) in enumerate(zip(srcs, dsts)):
            part, land = self._views(i, src, dst)
            for j, (px, py) in enumerate(chips):
                got = land(2 * px + py, 1 - c)
                _remote(got, got, sems, 6 * i + 3 + j, sibling).wait_recv()
            for j, (px, py) in enumerate(chips):
                _remote(part(c), land(me, c), sems, 6 * i + j, (px, py, c)).wait_send()
                mine = land(2 * px + py, c)
                _remote(mine, mine, sems, 6 * i + 3 + j, sibling).wait_send()
            for core in range(2):
                pltpu.make_async_copy(part(core), land(me, core), sems[2].at[2 * i + core]).wait()


class _ScatterPlan:
    def __init__(self, items):
        self.inputs = list(items)
        self.out_shapes = [jax.ShapeDtypeStruct((3,) + a.shape[1:], a.dtype) for a in items]
        n = len(items)
        self.scratch = [pltpu.SemaphoreType.DMA((3 * n,)), pltpu.SemaphoreType.DMA((3 * n,))]

    def _copies(self, srcs, dsts, sems):
        x, y, c, chips = _place()
        return [_remote(src.at[2 * px + py], dst.at[j], sems, 3 * i + j, (px, py, c))
                for i, (src, dst) in enumerate(zip(srcs, dsts)) for j, (px, py) in enumerate(chips)]

    def start(self, srcs, dsts, sems):
        for cp in self._copies(srcs, dsts, sems):
            cp.start()

    def finish(self, srcs, dsts, sems):
        copies = self._copies(srcs, dsts, sems)
        for cp in copies:
            cp.wait_recv()
        for cp in copies:
            cp.wait_send()


class _SiblingPlan:
    def __init__(self, items):
        self.inputs = list(items)
        self.out_shapes = [jax.ShapeDtypeStruct(a.shape, a.dtype) for a in items]
        n = len(items)
        self.scratch = [pltpu.SemaphoreType.DMA((n,)), pltpu.SemaphoreType.DMA((n,))]

    def _copies(self, srcs, dsts, sems):
        x, y, c, _ = _place()
        return [_remote(src, dst, sems, i, (x, y, 1 - c)) for i, (src, dst) in enumerate(zip(srcs, dsts))]

    def start(self, srcs, dsts, sems):
        for cp in self._copies(srcs, dsts, sems):
            cp.start()

    def finish(self, srcs, dsts, sems):
        copies = self._copies(srcs, dsts, sems)
        for cp in copies:
            cp.wait_recv()
        for cp in copies:
            cp.wait_send()


class _EveryonePlan:
    def __init__(self, block):
        self.inputs = [block]
        self.m = block.shape[0]
        self.out_shapes = [jax.ShapeDtypeStruct((8 * self.m,) + block.shape[1:], block.dtype)]
        self.scratch = [pltpu.SemaphoreType.DMA((7,)), pltpu.SemaphoreType.DMA((7,)), pltpu.SemaphoreType.DMA((1,))]

    def _rows(self, dst, px, py, pc):
        return dst.at[pl.ds((4 * px + 2 * py + pc) * self.m, self.m), :]

    def start(self, srcs, dsts, sems):
        x, y, c, chips = _place()
        src, dst = srcs[0], dsts[0]
        pltpu.make_async_copy(src, self._rows(dst, x, y, c), sems[2].at[0]).start()
        _remote(src, self._rows(dst, x, y, c), sems, 0, (x, y, 1 - c)).start()
        for j, (px, py) in enumerate(chips):
            _remote(src, self._rows(dst, x, y, c), sems, 1 + j, (px, py, c)).start()

    def finish(self, srcs, dsts, sems):
        x, y, c, chips = _place()
        src, dst = srcs[0], dsts[0]
        sibling = (x, y, 1 - c)
        for j, (px, py) in enumerate(chips):
            got = self._rows(dst, px, py, c)
            _remote(got, got, sems, 1 + j, (px, py, c)).wait_recv()
            _remote(got, got, sems, 4 + j, sibling).start()
        sib = self._rows(dst, x, y, 1 - c)
        _remote(sib, sib, sems, 0, sibling).wait_recv()
        for j, (px, py) in enumerate(chips):
            got = self._rows(dst, px, py, 1 - c)
            _remote(got, got, sems, 4 + j, sibling).wait_recv()
        mine = self._rows(dst, x, y, c)
        _remote(src, mine, sems, 0, sibling).wait_send()
        for j, (px, py) in enumerate(chips):
            _remote(src, mine, sems, 1 + j, (px, py, c)).wait_send()
            got = self._rows(dst, px, py, c)
            _remote(got, got, sems, 4 + j, sibling).wait_send()
        pltpu.make_async_copy(src, mine, sems[2].at[0]).wait()


def _call(body, *, name, grid, in_specs, out_specs, out_shape, args, scratch=(), parallel=(), plan=None):
    n_in, n_out, n_scr = len(in_specs), len(out_shape), len(scratch)
    sem = tuple("parallel" if (a in parallel and plan is None) else "arbitrary" for a in range(len(grid)))
    params = pltpu.CompilerParams(dimension_semantics=sem, vmem_limit_bytes=VMEM_LIMIT)
    if plan is None:
        outs = pl.pallas_call(body, name=name, grid=grid, in_specs=in_specs, out_specs=out_specs,
                              out_shape=out_shape, scratch_shapes=list(scratch), compiler_params=params)(*args)
        return list(outs), []
    p_in, p_out, p_scr = len(plan.inputs), len(plan.out_shapes), len(plan.scratch)

    def wrapped(*refs):
        ins, refs = refs[:n_in], refs[n_in:]
        p_ins, refs = refs[:p_in], refs[p_in:]
        outs, refs = refs[:n_out], refs[n_out:]
        p_outs, refs = refs[:p_out], refs[p_out:]
        scr, p_sems = refs[:n_scr], refs[n_scr:]
        ids = [pl.program_id(a) for a in range(len(grid))]
        first = _all([i == 0 for i in ids])
        last = _all([i == n - 1 for i, n in zip(ids, grid)])

        @pl.when(first)
        def _():
            plan.start(p_ins, p_outs, p_sems)

        body(*ins, *outs, *scr)

        @pl.when(last)
        def _():
            plan.finish(p_ins, p_outs, p_sems)

    outs = pl.pallas_call(
        wrapped, name=name, grid=grid,
        in_specs=list(in_specs) + [ANY] * p_in, out_specs=list(out_specs) + [ANY] * p_out,
        out_shape=list(out_shape) + list(plan.out_shapes),
        scratch_shapes=list(scratch) + list(plan.scratch), compiler_params=params,
    )(*args, *plan.inputs)
    return list(outs[:n_out]), list(outs[n_out:])


def _all(conds):
    out = conds[0]
    for c in conds[1:]:
        out = out & c
    return out


class _Together:
    def __init__(self, plans):
        self.plans = plans
        self.inputs = [a for p in plans for a in p.inputs]
        self.out_shapes = [s for p in plans for s in p.out_shapes]
        self.scratch = [s for p in plans for s in p.scratch]

    def _split(self, refs, count):
        out, at = [], 0
        for p in self.plans:
            out.append(refs[at:at + count(p)])
            at += count(p)
        return out

    def _parts(self, srcs, dsts, sems):
        return zip(self.plans, self._split(srcs, lambda p: len(p.inputs)),
                   self._split(dsts, lambda p: len(p.out_shapes)), self._split(sems, lambda p: len(p.scratch)))

    def start(self, srcs, dsts, sems):
        for p, s, d, m in self._parts(srcs, dsts, sems):
            p.start(s, d, m)

    def finish(self, srcs, dsts, sems):
        for p, s, d, m in self._parts(srcs, dsts, sems):
            p.finish(s, d, m)


def _run_plan(plan, *, name):
    p_in, p_out = len(plan.inputs), len(plan.out_shapes)

    def body(*refs):
        srcs, dsts, sems = refs[:p_in], refs[p_in:p_in + p_out], refs[p_in + p_out:]
        plan.start(srcs, dsts, sems)
        plan.finish(srcs, dsts, sems)

    return list(pl.pallas_call(body, name=name, in_specs=[ANY] * p_in, out_specs=[ANY] * p_out,
                               out_shape=list(plan.out_shapes), scratch_shapes=list(plan.scratch))(*plan.inputs))


def _rms(xf, g):
    r = lax.rsqrt(jnp.mean(xf * xf, axis=-1, keepdims=True) + EPS)
    return (xf * r * g).astype(BF16)


def _rmsnorm(x, g, *, name, plan=None):
    S, D = x.shape
    tm = _row_tile(S, 1024)

    def body(x_ref, g_ref, xn_ref):
        xn_ref[...] = _rms(x_ref[...], g_ref[...])

    return _call(
        body, name=name, grid=(S // tm,), parallel=(0,), plan=plan,
        in_specs=[pl.BlockSpec((tm, D), lambda i: (i, 0)), pl.BlockSpec((1, D), lambda i: (0, 0))],
        out_specs=[pl.BlockSpec((tm, D), lambda i: (i, 0))],
        out_shape=[jax.ShapeDtypeStruct((S, D), BF16)],
        args=(x, g))


def _matmul(xn, w4, *, name, out_dtype=F32, plan=None):
    S, D = xn.shape
    n4 = w4.shape[2]
    tm = _row_tile(S, 2048)

    def body(xn_ref, w_ref, o_ref):
        o_ref[...] = _dot(xn_ref[...], w_ref[...]).astype(out_dtype)

    return _call(
        body, name=name, grid=(S // tm, N_CHIPS), parallel=(0,), plan=plan,
        in_specs=[pl.BlockSpec((tm, D), lambda i, j: (i, 0)),
                  pl.BlockSpec((None, D, n4), lambda i, j: (j, 0, 0))],
        out_specs=[pl.BlockSpec((tm, n4), lambda i, j: (i, j))],
        out_shape=[jax.ShapeDtypeStruct((S, N_CHIPS * n4), out_dtype)],
        args=(xn, w4))


def _segments(widths, n4):
    per_chip = [[] for _ in range(N_CHIPS)]
    c0 = 0
    for p, w in enumerate(widths):
        a = c0
        while a < c0 + w:
            k = a // n4
            b = min(c0 + w, (k + 1) * n4)
            per_chip[k].append((p, (a - c0, b - c0), (a - k * n4, b - k * n4)))
            a = b
        c0 += w
    assert c0 == N_CHIPS * n4
    return per_chip


def _piece_specs(pieces, n4, tm):
    per_chip = _segments([p.shape[1] for p in pieces], n4)
    specs, local, start = [], [[] for _ in range(N_CHIPS)], 0
    for p, arr in enumerate(pieces):
        chips = [k for k in range(N_CHIPS) if any(seg[0] == p for seg in per_chip[k])]
        lo, hi = chips[0], chips[-1]
        tiled = arr.shape[1] % n4 == 0 and start % n4 == 0
        start += arr.shape[1]
        if tiled:
            imap = lambda k, i, lo=lo, hi=hi: (jnp.where((k >= lo) & (k <= hi), i, 0), jnp.clip(k - lo, 0, hi - lo))
            specs.append(pl.BlockSpec((tm, n4), imap))
        else:
            imap = lambda k, i, lo=lo, hi=hi: (jnp.where((k >= lo) & (k <= hi), i, 0), 0)
            specs.append(pl.BlockSpec((tm, arr.shape[1]), imap))
        for k in chips:
            for q, (pa, pb), cols in per_chip[k]:
                if q == p:
                    local[k].append((p, (0, n4) if tiled else (pa, pb), cols))
    return specs, local


def _dgrad_norm(dys, w4, x, g, dres, *, name, plan=None):
    S, D = x.shape
    n4 = w4.shape[2]
    tm = _row_tile(S, 512)
    per_chip = _segments([a.shape[1] for a in dys], n4)
    n_p = len(dys)

    def body(*refs):
        dy_refs = refs[:n_p]
        w_ref, x_ref, g_ref, dres_ref, dx_ref, dg_ref = refs[n_p:]

        @pl.when(pl.program_id(0) == 0)
        def _():
            dg_ref[...] = jnp.zeros_like(dg_ref)

        dxn = None
        for k in range(N_CHIPS):
            for p, (pa, pb), (ca, cb) in per_chip[k]:
                part = _dot_nt(dy_refs[p][:, pa:pb], w_ref[k, :, ca:cb])
                dxn = part if dxn is None else dxn + part
        xf = x_ref[...]
        r = lax.rsqrt(jnp.mean(xf * xf, axis=-1, keepdims=True) + EPS)
        xhat = xf * r
        dxhat = dxn * g_ref[...]
        dx_ref[...] = dres_ref[...] + r * (dxhat - xhat * jnp.mean(dxhat * xhat, axis=-1, keepdims=True))
        dg_ref[...] += jnp.sum(dxn * xhat, axis=0, keepdims=True)

    row = lambda w: pl.BlockSpec((tm, w), lambda i: (i, 0))
    return _call(
        body, name=name, grid=(S // tm,), plan=plan,
        in_specs=[row(a.shape[1]) for a in dys]
        + [pl.BlockSpec(w4.shape, lambda i: (0, 0, 0), pipeline_mode=pl.Buffered(1)),
           row(D), pl.BlockSpec((1, D), lambda i: (0, 0)), row(D)],
        out_specs=[row(D), pl.BlockSpec((1, D), lambda i: (0, 0))],
        out_shape=[jax.ShapeDtypeStruct((S, D), F32), jax.ShapeDtypeStruct((1, D), F32)],
        args=(*dys, w4, x, g, dres))


def _wgrad(a, dys, *, name, rows, plan=None):
    S, K = a.shape
    n4 = sum(p.shape[1] for p in dys) // N_CHIPS
    tm = _row_tile(S, rows)
    ns = S // tm
    specs, local = _piece_specs(dys, n4, tm)
    n_p = len(dys)

    def body(*refs):
        a_ref = refs[0]
        dy_refs = refs[1:1 + n_p]
        o_ref, o16_ref = refs[1 + n_p:]
        n, s = pl.program_id(0), pl.program_id(1)

        @pl.when(s == 0)
        def _():
            o_ref[...] = jnp.zeros_like(o_ref)

        for k in range(N_CHIPS):
            @pl.when(n == k)
            def _():
                av = a_ref[...]
                for p, (pa, pb), (ca, cb) in local[k]:
                    o_ref[:, ca:cb] += _dot_tn(av, dy_refs[p][:, pa:pb])

        @pl.when(s == ns - 1)
        def _():
            o16_ref[...] = o_ref[...].astype(BF16)

    out = pl.BlockSpec((None, K, n4), lambda n, s: (n, 0, 0))
    return _call(
        body, name=name, grid=(N_CHIPS, ns), parallel=(0,), plan=plan,
        in_specs=[pl.BlockSpec((tm, K), lambda n, s: (s, 0))] + specs,
        out_specs=[out, out],
        out_shape=[jax.ShapeDtypeStruct((N_CHIPS, K, n4), F32), jax.ShapeDtypeStruct((N_CHIPS, K, n4), BF16)],
        args=(a, *dys))


def _tiles(x):
    return x.reshape(x.shape[0] // SUBLANES, SUBLANES, x.shape[1])


def _shift_down(xp, s):
    n = xp.shape[0] - SUBLANES
    if s == SUBLANES:
        return xp[:n, :]
    t = _tiles(xp)
    rot = pltpu.roll(t, s, 1)
    sub = lax.broadcasted_iota(jnp.int32, t.shape, 1)[1:]
    return jnp.where(sub >= s, rot[1:], rot[:-1]).reshape(n, xp.shape[1])


def _shift_up(xn, s):
    n = xn.shape[0] - SUBLANES
    if s == SUBLANES:
        return xn[SUBLANES:, :]
    t = _tiles(xn)
    rot = pltpu.roll(t, SUBLANES - s, 1)
    sub = lax.broadcasted_iota(jnp.int32, t.shape, 1)[1:]
    return jnp.where(sub < SUBLANES - s, rot[:-1], rot[1:]).reshape(n, xn.shape[1])


def _pooled(u, halo, first_tile, row0):
    T = u.shape[0]
    halo = jnp.where(first_tile, 0.0, halo)
    pad = jnp.zeros((SUBLANES, u.shape[1]), F32)
    up = jnp.concatenate([pad, halo, u], axis=0)
    t1 = (row0 + lax.broadcasted_iota(jnp.int32, (T, 1), 0) + 1).astype(F32)
    outs = []
    for gi, w in enumerate(POOL_WINDOWS):
        s = up[:, gi * LANES:(gi + 1) * LANES]
        k = 1
        while k < w:
            if k < SUBLANES:
                s = jnp.concatenate([s[:SUBLANES, :], s[SUBLANES:, :] + _shift_down(s, k)], axis=0)
            else:
                s = s[SUBLANES:, :] + _shift_down(s, k)
            k *= 2
        s = s[-T:, :]
        inv = 1.0 / jnp.minimum(t1, float(w))
        outs.append(s * inv - u[:, gi * LANES:(gi + 1) * LANES])
    return outs


def _pool_mix(u, halo, first_tile, row0, pw_ref, ps_ref):
    pooled = _pooled(u, halo, first_tile, row0)
    outs = []
    for gi in range(len(POOL_WINDOWS)):
        mixed = _dot(pooled[gi].astype(BF16), pw_ref[gi].astype(BF16))
        outs.append((mixed * ps_ref[:, gi * LANES:(gi + 1) * LANES]).astype(BF16))
    return jnp.concatenate(outs, axis=1)


def _pool_grad(u, halo, first_tile, row0, d, dnext, pw_ref, ps_ref, du_ref, dpw_ref, dps_ref):
    T, P = d.shape
    pooled = _pooled(u, halo, first_tile, row0)
    pad = jnp.zeros((SUBLANES, P), F32)
    dext = jnp.concatenate([d, dnext, pad], axis=0)
    t1 = (row0 + lax.broadcasted_iota(jnp.int32, (T + HALO_POOL + SUBLANES, 1), 0) + 1).astype(F32)
    for gi, w in enumerate(POOL_WINDOWS):
        cols = slice(gi * LANES, (gi + 1) * LANES)
        pw = pw_ref[gi].astype(BF16)
        pg = pooled[gi].astype(BF16)
        mixed = _dot(pg, pw)
        dps_ref[:, cols] += jnp.sum(d[:, cols] * mixed, axis=0, keepdims=True)
        dmixed = (dext[:, cols] * ps_ref[:, cols]).astype(BF16)
        dpw_ref[gi] += _dot_tn(pg, dmixed[:T, :])
        dpooled = _dot_nt(dmixed, pw)
        e = dpooled * (1.0 / jnp.minimum(t1, float(w)))
        k = 1
        while k < w:
            if k < SUBLANES:
                e = jnp.concatenate([e[:-SUBLANES, :] + _shift_up(e, k), e[-SUBLANES:, :]], axis=0)
            else:
                e = e[:-SUBLANES, :] + _shift_up(e, k)
            k *= 2
        du_ref[:, cols] = (e[:T, :] - dpooled[:T, :]).astype(BF16)


def _cumsum_rows(x):
    n = x.shape[0]
    row = lax.broadcasted_iota(jnp.int32, x.shape, 0)
    s = 1
    while s < n:
        x = x + jnp.where(row >= s, pltpu.roll(x, s, 0), 0.0)
        s *= 2
    return x


def _rev_cumsum_rows(x):
    n = x.shape[0]
    row = lax.broadcasted_iota(jnp.int32, x.shape, 0)
    s = 1
    while s < n:
        x = x + jnp.where(row < n - s, pltpu.roll(x, n - s, 0), 0.0)
        s *= 2
    return x


def _chunk_prep(zq, zf, lb, b_ref):
    n_sub = CHUNK // SUB
    sq = _sigmoid(zq)
    q = zq * sq
    sf = _sigmoid(zf)
    f = lb + (1.0 - lb) * sf
    k = 1.0 - f
    b = _cumsum_rows(jnp.log(f))
    b_ref[...] = b
    shape = (SUB, b.shape[1])
    ends = [jnp.broadcast_to(b_ref[pl.ds(SUB * j + SUB - 1, 1), :], shape) for j in range(n_sub)]
    mids = [jnp.broadcast_to(b_ref[pl.ds(SUB * j + SUB // 2 - 1, 1), :], shape) for j in range(n_sub)]
    own = [b[SUB * j:SUB * (j + 1), :] for j in range(n_sub)]
    m0 = jnp.concatenate(mids, axis=0)
    e1 = jnp.concatenate(ends, axis=0)
    eq = [jnp.exp(jnp.minimum(b - m0, EXP_CLAMP))]
    for d in range(1, n_sub):
        rd = jnp.concatenate([own[j] if j < d else ends[j - d] for j in range(n_sub)], axis=0)
        eq.append(jnp.exp(b - rd))
    ek0 = jnp.exp(jnp.minimum(m0 - b, EXP_CLAMP))
    ek1 = jnp.exp(e1 - b)
    b_last = b_ref[pl.ds(CHUNK - 1, 1), :]
    return dict(q=q, k=k, f=f, sq=sq, sf=sf, b=b, eq=eq, ek0=ek0, ek1=ek1,
                eb=jnp.exp(b), ekl=jnp.exp(b_last - b), el=jnp.exp(b_last))


def _chunk_masks():
    ti = lax.broadcasted_iota(jnp.int32, (CHUNK, CHUNK), 0)
    si = lax.broadcasted_iota(jnp.int32, (CHUNK, CHUNK), 1)
    shift = SUB.bit_length() - 1
    dsub = jnp.right_shift(ti, shift) - jnp.right_shift(si, shift)
    masks = [(dsub == 0) & (si <= ti)]
    masks += [dsub == d for d in range(1, CHUNK // SUB)]
    return masks


def _chunk_attn(p, masks):
    qd = [(p["q"] * e).astype(BF16) for e in p["eq"]]
    k0 = (p["k"] * p["ek0"]).astype(BF16)
    k1 = (p["k"] * p["ek1"]).astype(BF16)
    a = jnp.where(masks[0], _dot_nt(qd[0], k0), 0.0)
    for d in range(1, len(masks)):
        a = jnp.where(masks[d], _dot_nt(qd[d], k1), a)
    return a, qd, k0, k1


def _hgrn_fwd(z, lb, norm_g, *, name, plan=None):
    S = z.shape[0]
    HW = lb.shape[1]
    NH = HW // LANES
    T = _row_tile(S, 512)
    nc = T // CHUNK

    def body(zq_ref, zf_ref, zi_ref, zo_ref, lb_ref, ng_ref, o_ref, of_ref, st_ref, s_scr, b_scr):
        @pl.when(pl.program_id(0) == 0)
        def _():
            s_scr[...] = jnp.zeros_like(s_scr)

        ng = ng_ref[...]
        masks = _chunk_masks()

        def chunk(c, carry):
            rows = pl.ds(pl.multiple_of(c * CHUNK, CHUNK), CHUNK)
            for h in range(NH):
                cols = slice(h * LANES, (h + 1) * LANES)
                p = _chunk_prep(zq_ref[rows, cols], zf_ref[rows, cols], lb_ref[:, cols], b_scr.at[h])
                v = zi_ref[rows, cols].astype(BF16)
                zo = zo_ref[rows, cols]
                st = s_scr[h]
                st_ref[c, h] = st
                a, _, _, _ = _chunk_attn(p, masks)
                o = _dot(a.astype(BF16), v) + _dot_nt((p["q"] * p["eb"]).astype(BF16), st.astype(BF16))
                s_scr[h] = st * p["el"] + _dot_tn(v, (p["k"] * p["ekl"]).astype(BF16))
                o_ref[rows, cols] = o
                r = lax.rsqrt(jnp.mean(o * o, axis=-1, keepdims=True) + EPS)
                of_ref[rows, cols] = (o * r * ng * (zo * _sigmoid(zo))).astype(BF16)
            return carry

        lax.fori_loop(0, nc, chunk, 0, unroll=8)

    part = lambda k: pl.BlockSpec((T, HW), lambda i, k=k: (i, k))
    return _call(
        body, name=name, grid=(S // T,), plan=plan,
        in_specs=[part(1), part(2), part(3), part(4),
                  pl.BlockSpec((1, HW), lambda i: (0, 0)), pl.BlockSpec((1, LANES), lambda i: (0, 0))],
        out_specs=[pl.BlockSpec((T, HW), lambda i: (i, 0)), pl.BlockSpec((T, HW), lambda i: (i, 0)),
                   pl.BlockSpec((nc, NH, LANES, LANES), lambda i: (i, 0, 0, 0))],
        out_shape=[jax.ShapeDtypeStruct((S, HW), F32), jax.ShapeDtypeStruct((S, HW), BF16),
                   jax.ShapeDtypeStruct((S // CHUNK, NH, LANES, LANES), F32)],
        scratch=[pltpu.VMEM((NH, LANES, LANES), F32), pltpu.VMEM((NH, CHUNK, LANES), F32)],
        args=(z, z, z, z, lb, norm_g))


def _hgrn_bwd(z, lb, norm_g, o_raw, states, dof, *, name, plan=None):
    S = z.shape[0]
    HW = lb.shape[1]
    NH = HW // LANES
    T = _row_tile(S, 512)
    nc = T // CHUNK
    nt = S // T

    def body(zq_ref, zf_ref, zi_ref, zo_ref, lb_ref, ng_ref, o_ref, st_ref, dof_ref,
             dzq_ref, dzf_ref, dzi_ref, dzo_ref, dlb_ref, dng_ref, ds_scr, b_scr):
        @pl.when(pl.program_id(0) == 0)
        def _():
            ds_scr[...] = jnp.zeros_like(ds_scr)
            dlb_ref[...] = jnp.zeros_like(dlb_ref)
            dng_ref[...] = jnp.zeros_like(dng_ref)

        ng = ng_ref[...]
        masks = _chunk_masks()
        last_row = lax.broadcasted_iota(jnp.int32, (CHUNK, 1), 0) == CHUNK - 1

        def chunk(cr, carry):
            c = nc - 1 - cr
            rows = pl.ds(pl.multiple_of(c * CHUNK, CHUNK), CHUNK)
            for h in range(NH):
                cols = slice(h * LANES, (h + 1) * LANES)
                lbv = lb_ref[:, cols]
                zq, zf, zo = zq_ref[rows, cols], zf_ref[rows, cols], zo_ref[rows, cols]
                o = o_ref[rows, cols]
                dof_c = dof_ref[rows, cols]
                st = st_ref[c, h]
                dst = ds_scr[h]

                so = _sigmoid(zo)
                r = lax.rsqrt(jnp.mean(o * o, axis=-1, keepdims=True) + EPS)
                ohat = o * r
                d_on = dof_c * (zo * so)
                dzo_ref[rows, cols] = (dof_c * ohat * ng * _dsilu(zo, so)).astype(BF16)
                dng_ref[:, cols] += jnp.sum(d_on * ohat, axis=0, keepdims=True)
                dohat = d_on * ng
                do = (r * (dohat - ohat * jnp.mean(dohat * ohat, axis=-1, keepdims=True))).astype(BF16)

                p = _chunk_prep(zq, zf, lbv, b_scr.at[h])
                q, k = p["q"], p["k"]
                v = zi_ref[rows, cols].astype(BF16)
                a, qd, k0, k1 = _chunk_attn(p, masks)
                ktl = (k * p["ekl"]).astype(BF16)
                dstb = dst.astype(BF16)

                da = _dot_nt(do, v)
                dzi_ref[rows, cols] = (_dot_tn(a.astype(BF16), do) + _dot_nt(ktl, dstb)).astype(BF16)

                da0 = jnp.where(masks[0], da, 0.0).astype(BF16)
                rq = _dot(da0, k0)
                rk0 = _dot_tn(da0, qd[0])
                dq = rq * p["eq"][0]
                db = qd[0].astype(F32) * rq - k0.astype(F32) * rk0
                rk1 = jnp.zeros_like(rk0)
                for d in range(1, len(masks)):
                    dad = jnp.where(masks[d], da, 0.0).astype(BF16)
                    rq = _dot(dad, k1)
                    dq = dq + rq * p["eq"][d]
                    db = db + qd[d].astype(F32) * rq
                    rk1 = rk1 + _dot_tn(dad, qd[d])
                dk = rk0 * p["ek0"] + rk1 * p["ek1"]
                db = db - k1.astype(F32) * rk1
                qe = (q * p["eb"]).astype(BF16)
                rq = _dot(do, st.astype(BF16))
                dq = dq + rq * p["eb"]
                db = db + qe.astype(F32) * rq
                rk = _dot(v, dstb)
                dk = dk + rk * p["ekl"]
                db = db - ktl.astype(F32) * rk

                st_new = st * p["el"] + _dot_tn(v, ktl)
                db = db + jnp.where(last_row, jnp.sum(dstb.astype(F32) * st_new, axis=0, keepdims=True), 0.0)
                dg = _rev_cumsum_rows(db)
                ds_scr[h] = dst * p["el"] + _dot_tn(do, qe)

                dzq_ref[rows, cols] = (dq * _dsilu(zq, p["sq"])).astype(BF16)
                df = dg / p["f"] - dk
                sf = p["sf"]
                dzf_ref[rows, cols] = (df * (1.0 - lbv) * sf * (1.0 - sf)).astype(BF16)
                dlb_ref[:, cols] += jnp.sum(df * (1.0 - sf), axis=0, keepdims=True)
            return carry

        lax.fori_loop(0, nc, chunk, 0, unroll=4)

    rev = lambda i: nt - 1 - i
    part = lambda k: pl.BlockSpec((T, HW), lambda i, k=k: (rev(i), k))
    blk = pl.BlockSpec((T, HW), lambda i: (rev(i), 0))
    vec = pl.BlockSpec((1, HW), lambda i: (0, 0))
    return _call(
        body, name=name, grid=(nt,), plan=plan,
        in_specs=[part(1), part(2), part(3), part(4), vec, pl.BlockSpec((1, LANES), lambda i: (0, 0)),
                  blk, pl.BlockSpec((nc, NH, LANES, LANES), lambda i: (rev(i), 0, 0, 0)), blk],
        out_specs=[blk, blk, blk, blk, vec, vec],
        out_shape=[jax.ShapeDtypeStruct((S, HW), BF16)] * 4 + [jax.ShapeDtypeStruct((1, HW), F32)] * 2,
        scratch=[pltpu.VMEM((NH, LANES, LANES), F32), pltpu.VMEM((NH, CHUNK, LANES), F32)],
        args=(z, z, z, z, lb, norm_g, o_raw, states, dof))


def _gate_specs(T, D, rows=lambda i: i):
    half = D // 2
    return [pl.BlockSpec((T, half), lambda i, k=k: (rows(i), 5 + k)) for k in range(4)]


def _gates(zg_refs, bg_ref, D):
    half = D // 2
    za = jnp.concatenate([zg_refs[0][...], zg_refs[1][...]], axis=1) + bg_ref[:, :D]
    zb = jnp.concatenate([zg_refs[2][...], zg_refs[3][...]], axis=1) + bg_ref[:, D:]
    return _sigmoid(za), _sigmoid(zb)


def _mix_fwd(x, of, z, pool_w, pool_scale, b_gate, w_pa4, w_pb4, w_o4, g_next, *, name, plan=None):
    S, D = x.shape
    P = of.shape[1]
    T = _row_tile(S, 512)
    hb = T // HALO_POOL

    def body(x_ref, u_ref, halo_ref, of_ref, g0, g1, g2, g3, pw_ref, ps_ref, bg_ref, wpa_ref, wpb_ref, wo_ref,
             gn_ref, xo_ref, ya_ref, yb_ref, xn_ref, pm_ref):
        i = pl.program_id(0)
        pmv = _pool_mix(u_ref[...], halo_ref[...], i == 0, i * T, pw_ref, ps_ref)
        pm_ref[...] = pmv
        ofv = of_ref[...]
        ya = jnp.concatenate([_dot(pmv, wpa_ref[k]) for k in range(N_CHIPS)], axis=1)
        yb = jnp.concatenate([_dot(ofv, wpb_ref[k]) for k in range(N_CHIPS)], axis=1)
        ga, gb = _gates((g0, g1, g2, g3), bg_ref, D)
        merged = (ga * ya + gb * yb).astype(BF16)
        x_mid = x_ref[...] + _dot(merged, wo_ref[...].reshape(D, D))
        xo_ref[...] = x_mid
        xn_ref[...] = _rms(x_mid, gn_ref[...])
        ya_ref[...] = ya.astype(BF16)
        yb_ref[...] = yb.astype(BF16)

    row = lambda w: pl.BlockSpec((T, w), lambda i: (i, 0))
    full = lambda a: pl.BlockSpec(a.shape, lambda i: (0,) * a.ndim)
    return _call(
        body, name=name, grid=(S // T,), parallel=(0,), plan=plan,
        in_specs=[row(D), row(P), pl.BlockSpec((HALO_POOL, P), lambda i: (jnp.maximum(i * hb - 1, 0), 0)), row(P)]
        + _gate_specs(T, D) + [full(pool_w), full(pool_scale), full(b_gate), full(w_pa4), full(w_pb4), full(w_o4),
                               full(g_next)],
        out_specs=[row(D), row(D), row(D), row(D), row(P)],
        out_shape=[jax.ShapeDtypeStruct((S, D), F32), jax.ShapeDtypeStruct((S, D), BF16),
                   jax.ShapeDtypeStruct((S, D), BF16), jax.ShapeDtypeStruct((S, D), BF16),
                   jax.ShapeDtypeStruct((S, P), BF16)],
        args=(x, z, z, of, z, z, z, z, pool_w, pool_scale, b_gate, w_pa4, w_pb4, w_o4, g_next))


def _mix_bwd(dxm, ya, yb, z, b_gate, pm, of, pool_w, pool_scale, w_pa4, w_pb4, w_o4, *, name):
    S, D = dxm.shape
    P = pm.shape[1]
    q4 = D // N_CHIPS
    T = _row_tile(S, 512)
    nt = S // T
    hb = T // HALO_POOL

    def body(dx_ref, ya_ref, yb_ref, g0, g1, g2, g3, bg_ref, pm_ref, of_ref, u_ref, halo_ref, pw_ref, ps_ref,
             wpa_ref, wpb_ref, wo_ref,
             dzg_ref, dof_ref, du_ref, dwo_ref, dwpa_ref, dwpb_ref, dbg_ref, dpw_ref, dps_ref,
             dwo16_ref, dwpa16_ref, dwpb16_ref, next_scr):
        i = pl.program_id(0)

        @pl.when(i == 0)
        def _():
            for ref in (dwo_ref, dwpa_ref, dwpb_ref, dbg_ref, dpw_ref, dps_ref, next_scr):
                ref[...] = jnp.zeros_like(ref)

        dxb = dx_ref[...].astype(BF16)
        ya = ya_ref[...].astype(F32)
        yb = yb_ref[...].astype(F32)
        ga, gb = _gates((g0, g1, g2, g3), bg_ref, D)
        merged = (ga * ya + gb * yb).astype(BF16)
        dwo_ref[...] += _dot_tn(merged, dxb).reshape(N_CHIPS, q4, D)
        dm = _dot_nt(dxb, wo_ref[...].reshape(D, D))
        dza = dm * ya * ga * (1.0 - ga)
        dzb = dm * yb * gb * (1.0 - gb)
        dzg_ref[:, :D] = dza.astype(BF16)
        dzg_ref[:, D:] = dzb.astype(BF16)
        dbg_ref[:, :D] += jnp.sum(dza, axis=0, keepdims=True)
        dbg_ref[:, D:] += jnp.sum(dzb, axis=0, keepdims=True)
        dya = (dm * ga).astype(BF16)
        dyb = (dm * gb).astype(BF16)
        pmv, ofv = pm_ref[...], of_ref[...]
        dpm = jnp.zeros((T, P), F32)
        dof = jnp.zeros((T, P), F32)
        for k in range(N_CHIPS):
            cols = slice(k * q4, (k + 1) * q4)
            dwpa_ref[k] += _dot_tn(pmv, dya[:, cols])
            dwpb_ref[k] += _dot_tn(ofv, dyb[:, cols])
            dpm = dpm + _dot_nt(dya[:, cols], wpa_ref[k])
            dof = dof + _dot_nt(dyb[:, cols], wpb_ref[k])
        dof_ref[...] = dof
        _pool_grad(u_ref[...], halo_ref[...], i == nt - 1, (nt - 1 - i) * T, dpm, next_scr[...], pw_ref, ps_ref,
                   du_ref, dpw_ref, dps_ref)
        next_scr[...] = dpm[:HALO_POOL, :]

        @pl.when(i == nt - 1)
        def _():
            dwo16_ref[...] = dwo_ref[...].astype(BF16)
            dwpa16_ref[...] = dwpa_ref[...].astype(BF16)
            dwpb16_ref[...] = dwpb_ref[...].astype(BF16)

    rev = lambda i: nt - 1 - i
    row = lambda w: pl.BlockSpec((T, w), lambda i: (rev(i), 0))
    full = lambda a: pl.BlockSpec(a.shape, lambda i: (0,) * a.ndim, pipeline_mode=pl.Buffered(1))
    like = lambda a, dt: jax.ShapeDtypeStruct(a.shape, dt)
    return _call(
        body, name=name, grid=(nt,),
        in_specs=[row(D), row(D), row(D)] + _gate_specs(T, D, rev)
        + [full(b_gate), row(P), row(P), row(P),
           pl.BlockSpec((HALO_POOL, P), lambda i: (jnp.maximum(rev(i) * hb - 1, 0), 0)),
           full(pool_w), full(pool_scale), full(w_pa4), full(w_pb4), full(w_o4)],
        out_specs=[row(2 * D), row(P), row(P), full(w_o4), full(w_pa4), full(w_pb4), full(b_gate),
                   full(pool_w), full(pool_scale), full(w_o4), full(w_pa4), full(w_pb4)],
        out_shape=[jax.ShapeDtypeStruct((S, 2 * D), BF16), jax.ShapeDtypeStruct((S, P), F32),
                   jax.ShapeDtypeStruct((S, P), BF16), like(w_o4, F32), like(w_pa4, F32), like(w_pb4, F32),
                   like(b_gate, F32), like(pool_w, F32), like(pool_scale, F32),
                   like(w_o4, BF16), like(w_pa4, BF16), like(w_pb4, BF16)],
        scratch=[pltpu.VMEM((HALO_POOL, P), F32)],
        args=(dxm, ya, yb, z, z, z, z, b_gate, pm, of, z, z, pool_w, pool_scale, w_pa4, w_pb4, w_o4))[0]


def _up_conv(xn, w_up4, conv_w, conv_b, *, name, plan=None):
    S, D = xn.shape
    f4 = w_up4.shape[2]
    nf = N_CHIPS // 2
    F = nf * f4
    T = _row_tile(S, 512)

    def body(xn_ref, wv_ref, wg_ref, cwv_ref, cwg_ref, cbv_ref, cbg_ref,
             hv_ref, hg_ref, val_ref, gate_ref, a_ref, pv_scr, pg_scr):
        i = pl.program_id(1)
        xv = xn_ref[...]

        def side(w_ref, cw_ref, cb_ref, h_ref, p_scr):
            h = _dot(xv, w_ref[...])
            h_ref[...] = h.astype(BF16)
            hp = jnp.concatenate([jnp.where(i == 0, 0.0, p_scr[...]), h], axis=0)
            p_scr[...] = h[-SUBLANES:, :]
            cw = cw_ref[...]
            return cw[0:1, :] * _shift_down(hp, 2) + cw[1:2, :] * _shift_down(hp, 1) + cw[2:3, :] * h + cb_ref[...]

        val = side(wv_ref, cwv_ref, cbv_ref, hv_ref, pv_scr)
        gate = side(wg_ref, cwg_ref, cbg_ref, hg_ref, pg_scr)
        val_ref[...] = val.astype(BF16)
        gate_ref[...] = gate.astype(BF16)
        a_ref[...] = (gate * _sigmoid(gate) * val).astype(BF16)

    out = pl.BlockSpec((T, f4), lambda f, i: (i, f))
    return _call(
        body, name=name, grid=(nf, S // T), plan=plan,
        in_specs=[pl.BlockSpec((T, D), lambda f, i: (i, 0)),
                  pl.BlockSpec((None, D, f4), lambda f, i: (f, 0, 0)),
                  pl.BlockSpec((None, D, f4), lambda f, i: (nf + f, 0, 0)),
                  pl.BlockSpec((3, f4), lambda f, i: (0, f)),
                  pl.BlockSpec((3, f4), lambda f, i: (0, nf + f)),
                  pl.BlockSpec((1, f4), lambda f, i: (0, f)),
                  pl.BlockSpec((1, f4), lambda f, i: (0, nf + f))],
        out_specs=[out] * 5,
        out_shape=[jax.ShapeDtypeStruct((S, F), BF16)] * 5,
        scratch=[pltpu.VMEM((SUBLANES, f4), F32), pltpu.VMEM((SUBLANES, f4), F32)],
        args=(xn, w_up4, w_up4, conv_w, conv_w, conv_b, conv_b))


def _down(a, w_down4, x, g_next, *, name, plan=None):
    S, F = a.shape
    D = x.shape[1]
    T = _row_tile(S, 1024)

    def body(a_ref, wd_ref, x_ref, gn_ref, o_ref, xn_ref):
        x_out = x_ref[...] + _dot(a_ref[...], wd_ref[...].reshape(F, D))
        o_ref[...] = x_out
        xn_ref[...] = _rms(x_out, gn_ref[...])

    row = lambda w: pl.BlockSpec((T, w), lambda i: (i, 0))
    return _call(
        body, name=name, grid=(S // T,), parallel=(0,), plan=plan,
        in_specs=[row(F), pl.BlockSpec(w_down4.shape, lambda i: (0, 0, 0), pipeline_mode=pl.Buffered(1)),
                  row(D), pl.BlockSpec((1, D), lambda i: (0, 0))],
        out_specs=[row(D), row(D)],
        out_shape=[jax.ShapeDtypeStruct((S, D), F32), jax.ShapeDtypeStruct((S, D), BF16)],
        args=(a, w_down4, x, g_next))


def _ffn_down_bwd(dxo, hv, hg, val16, gate16, conv_w, w_down4, *, name, plan=None):
    S = hv.shape[0]
    _, f4, D = w_down4.shape
    F = N_CHIPS * f4
    T = _row_tile(S, 512)
    nf = 2
    tf = 2 * f4
    nt = S // T

    def body(dx_ref, hv_ref, hg_ref, val_ref, gate_ref, cwv_ref, cwg_ref, wd_ref,
             dhv_ref, dhg_ref, dwd_ref, dwd16_ref, dcwv_ref, dcwg_ref, dcbv_ref, dcbg_ref, cv_scr, cg_scr):
        i = pl.program_id(1)

        @pl.when(i == 0)
        def _():
            cv_scr[...] = jnp.zeros_like(cv_scr)
            cg_scr[...] = jnp.zeros_like(cg_scr)
            dwd_ref[...] = jnp.zeros_like(dwd_ref)
            dcwv_ref[...] = jnp.zeros_like(dcwv_ref)
            dcwg_ref[...] = jnp.zeros_like(dcwg_ref)
            dcbv_ref[...] = jnp.zeros_like(dcbv_ref)
            dcbg_ref[...] = jnp.zeros_like(dcbg_ref)

        dxb = dx_ref[...].astype(BF16)
        val = val_ref[...].astype(F32)
        gate = gate_ref[...].astype(F32)
        sg = _sigmoid(gate)
        sil = gate * sg
        dwd_ref[...] += _dot_tn((sil * val).astype(BF16), dxb).reshape(2, f4, D)
        da = _dot_nt(dxb, wd_ref[...].reshape(tf, D))

        def conv_bwd(dhc, h0, cw, c_scr, dh_ref, dcw_ref, dcb_ref):
            ext = jnp.concatenate([dhc, c_scr[...]], axis=0)
            n1 = _shift_up(ext, 1)
            n2 = _shift_up(ext, 2)
            dh_ref[...] = (cw[2:3, :] * dhc + cw[1:2, :] * n1 + cw[0:1, :] * n2).astype(BF16)
            c_scr[...] = dhc[:SUBLANES, :]
            dcw_ref[0:1, :] += jnp.sum(n2 * h0, axis=0, keepdims=True)
            dcw_ref[1:2, :] += jnp.sum(n1 * h0, axis=0, keepdims=True)
            dcw_ref[2:3, :] += jnp.sum(dhc * h0, axis=0, keepdims=True)
            dcb_ref[...] += jnp.sum(dhc, axis=0, keepdims=True)

        conv_bwd(da * sil, hv_ref[...].astype(F32), cwv_ref[...], cv_scr, dhv_ref, dcwv_ref, dcbv_ref)
        conv_bwd(da * val * _dsilu(gate, sg), hg_ref[...].astype(F32), cwg_ref[...], cg_scr, dhg_ref, dcwg_ref,
                 dcbg_ref)

        @pl.when(i == nt - 1)
        def _():
            dwd16_ref[...] = dwd_ref[...].astype(BF16)

    rev = lambda i: nt - 1 - i
    wd_spec = pl.BlockSpec((2, f4, D), lambda f, i: (f, 0, 0))
    return _call(
        body, name=name, grid=(nf, nt), plan=plan,
        in_specs=[pl.BlockSpec((T, D), lambda f, i: (rev(i), 0)),
                  pl.BlockSpec((T, tf), lambda f, i: (rev(i), f)),
                  pl.BlockSpec((T, tf), lambda f, i: (rev(i), f)),
                  pl.BlockSpec((T, tf), lambda f, i: (rev(i), f)),
                  pl.BlockSpec((T, tf), lambda f, i: (rev(i), f)),
                  pl.BlockSpec((3, tf), lambda f, i: (0, f)),
                  pl.BlockSpec((3, tf), lambda f, i: (0, nf + f)),
                  wd_spec],
        out_specs=[pl.BlockSpec((T, tf), lambda f, i: (rev(i), f)),
                   pl.BlockSpec((T, tf), lambda f, i: (rev(i), f)),
                   wd_spec, wd_spec,
                   pl.BlockSpec((3, tf), lambda f, i: (0, f)),
                   pl.BlockSpec((3, tf), lambda f, i: (0, f)),
                   pl.BlockSpec((1, tf), lambda f, i: (0, f)),
                   pl.BlockSpec((1, tf), lambda f, i: (0, f))],
        out_shape=[jax.ShapeDtypeStruct((S, F), BF16), jax.ShapeDtypeStruct((S, F), BF16),
                   jax.ShapeDtypeStruct((N_CHIPS, f4, D), F32), jax.ShapeDtypeStruct((N_CHIPS, f4, D), BF16),
                   jax.ShapeDtypeStruct((3, F), F32), jax.ShapeDtypeStruct((3, F), F32),
                   jax.ShapeDtypeStruct((1, F), F32), jax.ShapeDtypeStruct((1, F), F32)],
        scratch=[pltpu.VMEM((SUBLANES, tf), F32), pltpu.VMEM((SUBLANES, tf), F32)],
        args=(dxo, hv, hg, val16, gate16, conv_w, conv_w, w_down4))


def _down_loss(a, w_down4, x, g, target, *, name):
    S, F = a.shape
    D = x.shape[1]
    T = _row_tile(S, 512)

    def body(a_ref, wd_ref, x_ref, g_ref, t_ref, loss_ref, dx_ref, dg_ref):
        @pl.when(pl.program_id(0) == 0)
        def _():
            loss_ref[...] = jnp.zeros_like(loss_ref)
            dg_ref[...] = jnp.zeros_like(dg_ref)

        xf = x_ref[...] + _dot(a_ref[...], wd_ref[...].reshape(F, D))
        r = lax.rsqrt(jnp.mean(xf * xf, axis=-1, keepdims=True) + EPS)
        xhat = xf * r
        err = xhat * g_ref[...] - t_ref[...]
        loss_ref[...] += jnp.sum(err * err, axis=0, keepdims=True) * (0.5 / D)
        dy = err * (1.0 / D)
        dxhat = dy * g_ref[...]
        dx_ref[...] = r * (dxhat - xhat * jnp.mean(dxhat * xhat, axis=-1, keepdims=True))
        dg_ref[...] += jnp.sum(dy * xhat, axis=0, keepdims=True)

    row = lambda w: pl.BlockSpec((T, w), lambda i: (i, 0))
    vec = pl.BlockSpec((1, D), lambda i: (0, 0))
    return _call(
        body, name=name, grid=(S // T,),
        in_specs=[row(F), pl.BlockSpec(w_down4.shape, lambda i: (0, 0, 0), pipeline_mode=pl.Buffered(1)),
                  row(D), vec, row(D)],
        out_specs=[vec, row(D), vec],
        out_shape=[jax.ShapeDtypeStruct((1, D), F32), jax.ShapeDtypeStruct((S, D), F32),
                   jax.ShapeDtypeStruct((1, D), F32)],
        args=(a, w_down4, x, g, target))[0]


BIG = ("w_in", "w_pa", "w_pb", "w_o", "w_up", "w_down")
SMALL = ("norm1_g", "b_gate", "pool_w", "pool_scale", "lb_logits", "hgrn_norm_g", "norm2_g", "conv_b", "final_g")
WEIGHTS = ("norm1_g", "w_in", "b_gate", "pool_w", "pool_scale", "lb_logits", "hgrn_norm_g", "w_pa", "w_pb", "w_o",
           "norm2_g", "w_up", "conv_w", "conv_b", "w_down", "final_g")


def _lower_bounds(lb_logits):
    soft = jax.nn.softmax(lb_logits.astype(F32), axis=0)
    cum = jnp.cumsum(soft, axis=0)
    return cum - cum[0:1]


def _step(x, target, sm, wts, shards=None):
    L = sm["norm1_g"].shape[0]
    wts = dict(wts)
    dist = shards is not None
    lbs, lb_vjp = jax.vjp(_lower_bounds, sm["lb_logits"])
    row = lambda a: a.reshape(1, -1)
    conv_w = sm.get("conv_w")

    def gather(names_layers, with_conv=False):
        items = [(shards[n], "rows", l) for n, l in names_layers]
        if with_conv:
            items.append((shards["conv_w"], "layer", None))
        return _GatherPlan(items)

    def landed(names_layers, outs):
        for key, arr in zip(names_layers, outs):
            wts[key] = arr

    own = {"in_proj": ("w_up",)}
    first = {"hgrn_fwd": ("w_pa", "w_pb", "w_o"), "mix_fwd": ("w_down",)}
    ahead = {"up": ("w_in", "w_pa", "w_pb", "w_o", "w_down")}
    conv_rider = "mix_fwd"

    def riders(l, kernel):
        if not dist:
            return [], None
        keys = [(n, l) for n in own.get(kernel, ())]
        keys += [(n, l) for n in first.get(kernel, ())] if l == 0 else []
        keys += [(n, l + 1) for n in ahead.get(kernel, ())] if l + 1 < L else []
        with_conv = l == 0 and kernel == conv_rider
        return keys, (gather(keys, with_conv) if keys or with_conv else None)

    keys = [("w_in", 0)] if dist else []
    (xn1,), got = _rmsnorm(x, row(sm["norm1_g"][0]), name="norm_in", plan=gather(keys) if keys else None)
    landed(keys, got)

    saved = []
    for l in range(L):
        keys, plan = riders(l, "in_proj")
        (z,), got = _matmul(xn1, wts[("w_in", l)], name=f"in_proj_{l}", plan=plan)
        landed(keys, got)
        keys, plan = riders(l, "hgrn_fwd")
        (o_raw, of, states), got = _hgrn_fwd(z, row(lbs[l]), row(sm["hgrn_norm_g"][l]), name=f"hgrn_fwd_{l}",
                                             plan=plan)
        landed(keys, got)
        keys, plan = riders(l, "mix_fwd")
        (x_mid, ya, yb, xn2, pm), got = _mix_fwd(
            x, of, z, sm["pool_w"][l], row(sm["pool_scale"][l]), row(sm["b_gate"][l]), wts[("w_pa", l)],
            wts[("w_pb", l)], wts[("w_o", l)], row(sm["norm2_g"][l]), name=f"mix_fwd_{l}", plan=plan)
        landed(keys, got)
        if dist and l == 0:
            full = got[-1]
            conv_w = jnp.concatenate([full[:, k] for k in range(N_CHIPS)], axis=2)
        keys, plan = riders(l, "up")
        (hv, hg, val16, gate16, a16), got = _up_conv(xn2, wts[("w_up", l)], conv_w[l], row(sm["conv_b"][l]),
                                                     name=f"up_{l}", plan=plan)
        landed(keys, got)
        saved.append(dict(x=x, xn1=xn1, z=z, pm=pm, o_raw=o_raw, of=of, states=states,
                          x_mid=x_mid, ya=ya, yb=yb, xn2=xn2, hv=hv, hg=hg, val16=val16, gate16=gate16))
        if l + 1 < L:
            keys, plan = riders(l, "down")
            (x, xn1), got = _down(a16, wts[("w_down", l)], x_mid, row(sm["norm1_g"][l + 1]), name=f"down_{l}",
                                  plan=plan)
            landed(keys, got)
        else:
            loss_cols, dx, d_final_g = _down_loss(a16, wts[("w_down", l)], x_mid, row(sm["final_g"]), target,
                                                  name="down_loss")

    small = {k: [None] * L for k in ("norm1_g", "b_gate", "pool_w", "pool_scale", "hgrn_norm_g", "norm2_g",
                                     "conv_w", "conv_b")}
    big32, big16, recv = {}, {}, {}
    dlbs = [None] * L
    pending = []

    def scatter():
        if not (dist and pending):
            return [], None
        keys = list(pending)
        del pending[:]
        return keys, _ScatterPlan([big16[k] for k in keys])

    def sent(keys, outs):
        for key, arr in zip(keys, outs):
            recv[key] = arr

    def made(name, l, g32, g16):
        big32[(name, l)], big16[(name, l)] = g32, g16
        pending.append((name, l))

    for l in reversed(range(L)):
        s = saved[l]
        keys, plan = scatter()
        (dhv, dhg, d_wd, d_wd16, dcwv, dcwg, dcbv, dcbg), got = _ffn_down_bwd(
            dx, s["hv"], s["hg"], s["val16"], s["gate16"], conv_w[l], wts[("w_down", l)], name=f"down_bwd_{l}",
            plan=plan)
        sent(keys, got)
        made("w_down", l, d_wd, d_wd16)
        small["conv_w"][l] = jnp.concatenate([dcwv, dcwg], axis=1)
        small["conv_b"][l] = jnp.concatenate([dcbv, dcbg], axis=1)[0]
        keys, plan = scatter()
        (d_wu, d_wu16), got = _wgrad(s["xn2"], [dhv, dhg], name=f"up_wgrad_{l}", rows=2048, plan=plan)
        sent(keys, got)
        made("w_up", l, d_wu, d_wu16)
        (dxm, dg2), _ = _dgrad_norm([dhv, dhg], wts[("w_up", l)], s["x_mid"], row(sm["norm2_g"][l]), dx,
                                    name=f"up_dgrad_{l}")
        small["norm2_g"][l] = dg2[0]

        dzg, dof, du, d_wo, d_wpa, d_wpb, dbg, dpw, dps, d_wo16, d_wpa16, d_wpb16 = _mix_bwd(
            dxm, s["ya"], s["yb"], s["z"], row(sm["b_gate"][l]), s["pm"], s["of"], sm["pool_w"][l],
            row(sm["pool_scale"][l]), wts[("w_pa", l)], wts[("w_pb", l)], wts[("w_o", l)], name=f"mix_bwd_{l}")
        small["pool_w"][l], small["pool_scale"][l] = dpw, dps[0]
        made("w_o", l, d_wo, d_wo16)
        made("w_pa", l, d_wpa, d_wpa16)
        made("w_pb", l, d_wpb, d_wpb16)
        small["b_gate"][l] = dbg[0]

        keys, plan = scatter()
        (dzq, dzf, dzi, dzo, dlb, dng), got = _hgrn_bwd(s["z"], row(lbs[l]), row(sm["hgrn_norm_g"][l]), s["o_raw"],
                                                      s["states"], dof, name=f"hgrn_bwd_{l}", plan=plan)
        sent(keys, got)
        dlbs[l] = dlb[0]
        small["hgrn_norm_g"][l] = jnp.sum(dng.reshape(-1, LANES), axis=0)

        dz = [du, dzq, dzf, dzi, dzo, dzg]
        (d_wi, d_wi16), _ = _wgrad(s["xn1"], dz, name=f"in_wgrad_{l}", rows=1024)
        made("w_in", l, d_wi, d_wi16)
        keys, plan = scatter()
        (dx, dg1), got = _dgrad_norm(dz, wts[("w_in", l)], s["x"], row(sm["norm1_g"][l]), dxm,
                                     name=f"in_dgrad_{l}", plan=plan)
        sent(keys, got)
        small["norm1_g"][l] = dg1[0]

    out = {k: jnp.stack(v) for k, v in small.items()}
    out["lb_logits"] = lb_vjp(jnp.stack(dlbs))[0]
    out["final_g"] = d_final_g[0]
    return loss_cols, dx, out, big32, recv


def _elementwise_rows(R, n, n_arrays):
    if 2 * n_arrays * R * n * 4 <= VMEM_LIMIT // 4 or R % 8:
        return R
    block = VMEM_LIMIT // 2 // (2 * n_arrays)
    want = 8
    while want * 2 * n * 4 <= block:
        want *= 2
    return _row_tile(R, want)


def _sum_layers(own, got, chip, *, name):
    L = len(own)
    _, r, n = own[0].shape
    T = _elementwise_rows(r, n, 6)
    nt = r // T

    def body(chip_ref, *refs):
        o_ref = refs[-1]
        l = pl.program_id(0)
        for k in range(L):
            @pl.when(l == k)
            def _():
                own_ref, got_ref = refs[2 * k], refs[2 * k + 1]
                acc = own_ref[...]
                for j in range(3):
                    acc = acc + got_ref[j].astype(F32)
                o_ref[...] = acc

    in_specs = []
    for k in range(L):
        hold = 0 if k else nt - 1
        in_specs.append(pl.BlockSpec((None, T, n), lambda l, i, c, k=k, hold=hold: (c[0], jnp.where(l == k, i, hold), 0)))
        in_specs.append(pl.BlockSpec((3, T, n), lambda l, i, c, k=k, hold=hold: (0, jnp.where(l == k, i, hold), 0)))
    grid_spec = pltpu.PrefetchScalarGridSpec(
        num_scalar_prefetch=1, grid=(L, nt), in_specs=in_specs,
        out_specs=pl.BlockSpec((None, T, n), lambda l, i, c: (l, i, 0)))
    args = [a for pair in zip(own, got) for a in pair]
    return pl.pallas_call(
        body, name=name, grid_spec=grid_spec, out_shape=jax.ShapeDtypeStruct((L, r, n), F32),
        compiler_params=pltpu.CompilerParams(dimension_semantics=("arbitrary", "arbitrary"),
                                             vmem_limit_bytes=VMEM_LIMIT),
    )(chip, *args)


def _sum_stack(parts, *, name):
    K, R, n = parts.shape
    T = _elementwise_rows(R, n, K + 1)

    def body(p_ref, o_ref):
        acc = p_ref[0]
        for j in range(1, K):
            acc = acc + p_ref[j]
        o_ref[...] = acc

    return _call(
        body, name=name, grid=(R // T,), parallel=(0,),
        in_specs=[pl.BlockSpec((K, T, n), lambda i: (0, i, 0))],
        out_specs=[pl.BlockSpec((T, n), lambda i: (i, 0))],
        out_shape=[jax.ShapeDtypeStruct((R, n), F32)],
        args=(parts,))[0][0]


def _adamw(w, m, v, g_parts, *, name):
    R, n = w.shape
    n_g = len(g_parts)
    T = _elementwise_rows(R, n, 7 + n_g)

    def body(*refs):
        w_ref, m_ref, v_ref = refs[:3]
        g_refs = refs[3:3 + n_g]
        go_ref, d_ref, mo_ref, vo_ref = refs[3 + n_g:]
        g_ = g_refs[0][...]
        for r in g_refs[1:]:
            g_ = g_ + r[...]
        m_ = ADAM_B1 * m_ref[...] + (1.0 - ADAM_B1) * g_
        v_ = ADAM_B2 * v_ref[...] + (1.0 - ADAM_B2) * (g_ * g_)
        m_hat = m_ / (1.0 - ADAM_B1 ** ADAM_STEP)
        v_hat = v_ / (1.0 - ADAM_B2 ** ADAM_STEP)
        go_ref[...] = g_
        d_ref[...] = -ADAM_LR * (m_hat / (jnp.sqrt(v_hat) + ADAM_EPS) + ADAM_WD * w_ref[...])
        mo_ref[...] = m_
        vo_ref[...] = v_

    blk = pl.BlockSpec((T, n), lambda i: (i, 0))
    return _call(
        body, name=name, grid=(R // T,), parallel=(0,),
        in_specs=[blk] * (3 + n_g), out_specs=[blk] * 4,
        out_shape=[jax.ShapeDtypeStruct((R, n), F32)] * 4,
        args=(w, m, v, *g_parts))[0]


PACK_ALIGN = 8 * LANES


def _pack(pieces):
    flat = []
    for a in pieces:
        a = a.reshape(-1)
        pad = (-a.shape[0]) % PACK_ALIGN
        flat.append(jnp.pad(a, (0, pad)) if pad else a)
    return jnp.concatenate(flat).reshape(-1, LANES)


def _unpack(buf, shapes):
    flat = buf.reshape(-1)
    out, off = [], 0
    for shp in shapes:
        size = 1
        for s in shp:
            size *= s
        out.append(flat[off:off + size].reshape(shp))
        off += size + (-size) % PACK_ALIGN
    return out


def kernel(x, norm1_g, w_in, b_gate, pool_w, pool_scale, lb_logits, hgrn_norm_g, w_pa, w_pb, w_o, norm2_g, w_up, conv_w, conv_b, w_down, final_g, loss_target, m_norm1_g, m_w_in, m_b_gate, m_pool_w, m_pool_scale, m_lb_logits, m_hgrn_norm_g, m_w_pa, m_w_pb, m_w_o, m_norm2_g, m_w_up, m_conv_w, m_conv_b, m_w_down, m_final_g, v_norm1_g, v_w_in, v_b_gate, v_pool_w, v_pool_scale, v_lb_logits, v_hgrn_norm_g, v_w_pa, v_w_pb, v_w_o, v_norm2_g, v_w_up, v_conv_w, v_conv_b, v_w_down, v_final_g):
    env = dict(locals())
    w = {n: env[n] for n in WEIGHTS}
    m = {n: env["m_" + n] for n in WEIGHTS}
    v = {n: env["v_" + n] for n in WEIGHTS}
    my_chip = 2 * lax.axis_index("x") + lax.axis_index("y")
    L = w_in.shape[0]

    shards = {n: w[n].astype(BF16) for n in BIG}
    shards["conv_w"] = w["conv_w"]
    sm = {n: w[n] for n in SMALL}
    loss_cols, grad_x, g_small, big32, recv = _step(x[0], loss_target[0], sm, {}, shards)

    chip = my_chip.reshape(1).astype(jnp.int32)
    sums = [_sum_layers([big32[(n, l)] for l in range(L)], [recv[(n, l)] for l in range(L)], chip,
                        name="chip_sum_" + n) for n in BIG]
    small_names = list(SMALL)
    small_pieces = [g_small[n] for n in small_names] + [g_small["conv_w"], loss_cols]
    small_shapes = [a.shape for a in small_pieces]
    packed = _pack(small_pieces)
    Rs = packed.shape[0]
    swapped = _run_plan(_Together([_SiblingPlan(sums), _EveryonePlan(packed)]), name="tail_exchange")
    theirs, everyone = swapped[:-1], swapped[-1].reshape(8, Rs, LANES)
    g, delta, new_m, new_v = {}, {}, {}, {}
    for n, mine, other in zip(BIG, sums, theirs):
        shp = w[n].shape
        two_d = lambda a: a.reshape(-1, shp[-1])
        outs = _adamw(two_d(w[n]), two_d(m[n]), two_d(v[n]), [two_d(mine), two_d(other)], name="adamw_" + n)
        g[n], delta[n], new_m[n], new_v[n] = [a.reshape(shp) for a in outs]

    summed = _unpack(_sum_stack(everyone, name="small_sum"), small_shapes)
    loss = jnp.sum(summed[-1])
    cshard = w["conv_w"].shape[2]
    gs = dict(zip(small_names, summed[:len(small_names)]))
    g_cw = lax.dynamic_slice_in_dim(summed[-2], my_chip * cshard, cshard, axis=2)

    sm_out = _adamw(_pack([w[n] for n in small_names]), _pack([m[n] for n in small_names]),
                    _pack([v[n] for n in small_names]), [_pack([gs[n] for n in small_names])], name="adamw_small")
    shapes = [w[n].shape for n in small_names]
    for n, g_, d_, m_, v_ in zip(small_names, *[_unpack(a, shapes) for a in sm_out]):
        g[n], delta[n], new_m[n], new_v[n] = g_, d_, m_, v_
    shp = w["conv_w"].shape
    two_d = lambda a: a.reshape(-1, shp[-1])
    outs = _adamw(two_d(w["conv_w"]), two_d(m["conv_w"]), two_d(v["conv_w"]), [two_d(g_cw)], name="adamw_conv_w")
    g["conv_w"], delta["conv_w"], new_m["conv_w"], new_v["conv_w"] = [a.reshape(shp) for a in outs]

    return (loss, grad_x[None], *[g[n] for n in WEIGHTS], *[delta[n] for n in WEIGHTS],
            *[new_m[n] for n in WEIGHTS], *[new_v[n] for n in WEIGHTS])
```

```python
import jax
import jax.numpy as jnp
from jax import lax
from jax.experimental import pallas as pl
from jax.experimental.pallas import tpu as pltpu

F32 = jnp.float32
BF16 = jnp.bfloat16

EPS = 1e-6
CHUNK = 64
SUB = 32
LANES = 128
SUBLANES = 8
POOL_WINDOWS = (2, 4, 8, 16)
HALO_POOL = 16
EXP_CLAMP = 80.0

ADAM_LR = 0.001
ADAM_B1 = 0.9
ADAM_B2 = 0.999
ADAM_EPS = 1e-08
ADAM_WD = 0.01
ADAM_STEP = 10

VMEM_LIMIT = 56 * 1024 * 1024
MESH_ID = pl.DeviceIdType.MESH
N_CHIPS = 4
ANY = pl.BlockSpec(memory_space=pl.ANY)


def _dot(a, b):
    return jnp.dot(a, b, preferred_element_type=F32)


def _dot_nt(a, b):
    return lax.dot_general(a, b, (((1,), (1,)), ((), ())), preferred_element_type=F32)


def _dot_tn(a, b):
    return lax.dot_general(a, b, (((0,), (0,)), ((), ())), preferred_element_type=F32)


def _sigmoid(x):
    return jax.nn.sigmoid(x)


def _dsilu(x, s):
    return s * (1.0 + x * (1.0 - s))


def _row_tile(rows, want):
    t = min(rows, want)
    while rows % t:
        t //= 2
    return t


def _place():
    x, y, c = lax.axis_index("x"), lax.axis_index("y"), lax.axis_index("c")
    chips = [(1 - x, y), (x, 1 - y), (1 - x, 1 - y)]
    return x, y, c, chips


def _remote(src, dst, sems, k, to):
    return pltpu.make_async_remote_copy(src_ref=src, dst_ref=dst, send_sem=sems[0].at[k], recv_sem=sems[1].at[k],
                                        device_id=to, device_id_type=MESH_ID)


class _GatherPlan:
    def __init__(self, items):
        self.items = items
        self.inputs = [a for a, _, _ in items]
        self.out_shapes = []
        for a, kind, _ in items:
            shp = (N_CHIPS,) + a.shape[1:] if kind == "rows" else (a.shape[0], N_CHIPS) + a.shape[1:]
            self.out_shapes.append(jax.ShapeDtypeStruct(shp, a.dtype))
        n = len(items)
        self.scratch = [pltpu.SemaphoreType.DMA((6 * n,)), pltpu.SemaphoreType.DMA((6 * n,)),
                        pltpu.SemaphoreType.DMA((2 * n,))]

    def _views(self, i, src, dst):
        _, kind, l = self.items[i]
        if kind == "rows":
            half = src.shape[1] // 2
            part = lambda core: src.at[l, pl.ds(core * half, half), :]
            land = lambda chip, core: dst.at[chip, pl.ds(core * half, half), :]
        else:
            part = lambda core: src.at[core]
            land = lambda chip, core: dst.at[core, chip]
        return part, land

    def start(self, srcs, dsts, sems):
        x, y, c, chips = _place()
        me = 2 * x + y
        for i, (src, dst) in enumerate(zip(srcs, dsts)):
            part, land = self._views(i, src, dst)
            for core in range(2):
                pltpu.make_async_copy(part(core), land(me, core), sems[2].at[2 * i + core]).start()
            for j, (px, py) in enumerate(chips):
                _remote(part(c), land(me, c), sems, 6 * i + j, (px, py, c)).start()

    def finish(self, srcs, dsts, sems):
        x, y, c, chips = _place()
        me = 2 * x + y
        sibling = (x, y, 1 - c)
        for i, (src, dst) in enumerate(zip(srcs, dsts)):
            part, land = self._views(i, src, dst)
            for j, (px, py) in enumerate(chips):
                got = land(2 * px + py, c)
                _remote(got, got, sems, 6 * i + j, (px, py, c)).wait_recv()
                _remote(got, got, sems, 6 * i + 3 + j, sibling).start()
        for i, (src, dst) in enumerate(zip(srcs, dsts)):
            part, land = self._views(i, src, dst)
            for j, (px, py) in enumerate(chips):
                got = land(2 * px + py, 1 - c)
                _remote(got, got, sems, 6 * i + 3 + j, sibling).wait_recv()
            for j, (px, py) in enumerate(chips):
                _remote(part(c), land(me, c), sems, 6 * i + j, (px, py, c)).wait_send()
                mine = land(2 * px + py, c)
                _remote(mine, mine, sems, 6 * i + 3 + j, sibling).wait_send()
            for core in range(2):
                pltpu.make_async_copy(part(core), land(me, core), sems[2].at[2 * i + core]).wait()


class _ScatterPlan:
    def __init__(self, items):
        self.inputs = list(items)
        self.out_shapes = [jax.ShapeDtypeStruct((3,) + a.shape[1:], a.dtype) for a in items]
        n = len(items)
        self.scratch = [pltpu.SemaphoreType.DMA((3 * n,)), pltpu.SemaphoreType.DMA((3 * n,))]

    def _copies(self, srcs, dsts, sems):
        x, y, c, chips = _place()
        return [_remote(src.at[2 * px + py], dst.at[j], sems, 3 * i + j, (px, py, c))
                for i, (src, dst) in enumerate(zip(srcs, dsts)) for j, (px, py) in enumerate(chips)]

    def start(self, srcs, dsts, sems):
        for cp in self._copies(srcs, dsts, sems):
            cp.start()

    def finish(self, srcs, dsts, sems):
        copies = self._copies(srcs, dsts, sems)
        for cp in copies:
            cp.wait_recv()
        for cp in copies:
            cp.wait_send()


class _SiblingPlan:
    def __init__(self, items):
        self.inputs = list(items)
        self.out_shapes = [jax.ShapeDtypeStruct(a.shape, a.dtype) for a in items]
        n = len(items)
        self.scratch = [pltpu.SemaphoreType.DMA((n,)), pltpu.SemaphoreType.DMA((n,))]

    def _copies(self, srcs, dsts, sems):
        x, y, c, _ = _place()
        return [_remote(src, dst, sems, i, (x, y, 1 - c)) for i, (src, dst) in enumerate(zip(srcs, dsts))]

    def start(self, srcs, dsts, sems):
        for cp in self._copies(srcs, dsts, sems):
            cp.start()

    def finish(self, srcs, dsts, sems):
        copies = self._copies(srcs, dsts, sems)
        for cp in copies:
            cp.wait_recv()
        for cp in copies:
            cp.wait_send()


class _EveryonePlan:
    def __init__(self, block):
        self.inputs = [block]
        self.m = block.shape[0]
        self.out_shapes = [jax.ShapeDtypeStruct((8 * self.m,) + block.shape[1:], block.dtype)]
        self.scratch = [pltpu.SemaphoreType.DMA((7,)), pltpu.SemaphoreType.DMA((7,)), pltpu.SemaphoreType.DMA((1,))]

    def _rows(self, dst, px, py, pc):
        return dst.at[pl.ds((4 * px + 2 * py + pc) * self.m, self.m), :]

    def start(self, srcs, dsts, sems):
        x, y, c, chips = _place()
        src, dst = srcs[0], dsts[0]
        pltpu.make_async_copy(src, self._rows(dst, x, y, c), sems[2].at[0]).start()
        _remote(src, self._rows(dst, x, y, c), sems, 0, (x, y, 1 - c)).start()
        for j, (px, py) in enumerate(chips):
            _remote(src, self._rows(dst, x, y, c), sems, 1 + j, (px, py, c)).start()

    def finish(self, srcs, dsts, sems):
        x, y, c, chips = _place()
        src, dst = srcs[0], dsts[0]
        sibling = (x, y, 1 - c)
        for j, (px, py) in enumerate(chips):
            got = self._rows(dst, px, py, c)
            _remote(got, got, sems, 1 + j, (px, py, c)).wait_recv()
            _remote(got, got, sems, 4 + j, sibling).start()
        sib = self._rows(dst, x, y, 1 - c)
        _remote(sib, sib, sems, 0, sibling).wait_recv()
        for j, (px, py) in enumerate(chips):
            got = self._rows(dst, px, py, 1 - c)
            _remote(got, got, sems, 4 + j, sibling).wait_recv()
        mine = self._rows(dst, x, y, c)
        _remote(src, mine, sems, 0, sibling).wait_send()
        for j, (px, py) in enumerate(chips):
            _remote(src, mine, sems, 1 + j, (px, py, c)).wait_send()
            got = self._rows(dst, px, py, c)
            _remote(got, got, sems, 4 + j, sibling).wait_send()
        pltpu.make_async_copy(src, mine, sems[2].at[0]).wait()


def _call(body, *, name, grid, in_specs, out_specs, out_shape, args, scratch=(), parallel=(), plan=None):
    n_in, n_out, n_scr = len(in_specs), len(out_shape), len(scratch)
    sem = tuple("parallel" if (a in parallel and plan is None) else "arbitrary" for a in range(len(grid)))
    params = pltpu.CompilerParams(dimension_semantics=sem, vmem_limit_bytes=VMEM_LIMIT)
    if plan is None:
        outs = pl.pallas_call(body, name=name, grid=grid, in_specs=in_specs, out_specs=out_specs,
                              out_shape=out_shape, scratch_shapes=list(scratch), compiler_params=params)(*args)
        return list(outs), []
    p_in, p_out, p_scr = len(plan.inputs), len(plan.out_shapes), len(plan.scratch)

    def wrapped(*refs):
        ins, refs = refs[:n_in], refs[n_in:]
        p_ins, refs = refs[:p_in], refs[p_in:]
        outs, refs = refs[:n_out], refs[n_out:]
        p_outs, refs = refs[:p_out], refs[p_out:]
        scr, p_sems = refs[:n_scr], refs[n_scr:]
        ids = [pl.program_id(a) for a in range(len(grid))]
        first = _all([i == 0 for i in ids])
        last = _all([i == n - 1 for i, n in zip(ids, grid)])

        @pl.when(first)
        def _():
            plan.start(p_ins, p_outs, p_sems)

        body(*ins, *outs, *scr)

        @pl.when(last)
        def _():
            plan.finish(p_ins, p_outs, p_sems)

    outs = pl.pallas_call(
        wrapped, name=name, grid=grid,
        in_specs=list(in_specs) + [ANY] * p_in, out_specs=list(out_specs) + [ANY] * p_out,
        out_shape=list(out_shape) + list(plan.out_shapes),
        scratch_shapes=list(scratch) + list(plan.scratch), compiler_params=params,
    )(*args, *plan.inputs)
    return list(outs[:n_out]), list(outs[n_out:])


def _all(conds):
    out = conds[0]
    for c in conds[1:]:
        out = out & c
    return out


class _Together:
    def __init__(self, plans):
        self.plans = plans
        self.inputs = [a for p in plans for a in p.inputs]
        self.out_shapes = [s for p in plans for s in p.out_shapes]
        self.scratch = [s for p in plans for s in p.scratch]

    def _split(self, refs, count):
        out, at = [], 0
        for p in self.plans:
            out.append(refs[at:at + count(p)])
            at += count(p)
        return out

    def _parts(self, srcs, dsts, sems):
        return zip(self.plans, self._split(srcs, lambda p: len(p.inputs)),
                   self._split(dsts, lambda p: len(p.out_shapes)), self._split(sems, lambda p: len(p.scratch)))

    def start(self, srcs, dsts, sems):
        for p, s, d, m in self._parts(srcs, dsts, sems):
            p.start(s, d, m)

    def finish(self, srcs, dsts, sems):
        for p, s, d, m in self._parts(srcs, dsts, sems):
            p.finish(s, d, m)


def _run_plan(plan, *, name):
    p_in, p_out = len(plan.inputs), len(plan.out_shapes)

    def body(*refs):
        srcs, dsts, sems = refs[:p_in], refs[p_in:p_in + p_out], refs[p_in + p_out:]
        plan.start(srcs, dsts, sems)
        plan.finish(srcs, dsts, sems)

    return list(pl.pallas_call(body, name=name, in_specs=[ANY] * p_in, out_specs=[ANY] * p_out,
                               out_shape=list(plan.out_shapes), scratch_shapes=list(plan.scratch))(*plan.inputs))


def _rms(xf, g):
    r = lax.rsqrt(jnp.mean(xf * xf, axis=-1, keepdims=True) + EPS)
    return (xf * r * g).astype(BF16)


def _rmsnorm(x, g, *, name, plan=None):
    S, D = x.shape
    tm = _row_tile(S, 1024)

    def body(x_ref, g_ref, xn_ref):
        xn_ref[...] = _rms(x_ref[...], g_ref[...])

    return _call(
        body, name=name, grid=(S // tm,), parallel=(0,), plan=plan,
        in_specs=[pl.BlockSpec((tm, D), lambda i: (i, 0)), pl.BlockSpec((1, D), lambda i: (0, 0))],
        out_specs=[pl.BlockSpec((tm, D), lambda i: (i, 0))],
        out_shape=[jax.ShapeDtypeStruct((S, D), BF16)],
        args=(x, g))


def _matmul(xn, w4, *, name, out_dtype=F32, plan=None):
    S, D = xn.shape
    n4 = w4.shape[2]
    tm = _row_tile(S, 2048)

    def body(xn_ref, w_ref, o_ref):
        o_ref[...] = _dot(xn_ref[...], w_ref[...]).astype(out_dtype)

    return _call(
        body, name=name, grid=(S // tm, N_CHIPS), parallel=(0,), plan=plan,
        in_specs=[pl.BlockSpec((tm, D), lambda i, j: (i, 0)),
                  pl.BlockSpec((None, D, n4), lambda i, j: (j, 0, 0))],
        out_specs=[pl.BlockSpec((tm, n4), lambda i, j: (i, j))],
        out_shape=[jax.ShapeDtypeStruct((S, N_CHIPS * n4), out_dtype)],
        args=(xn, w4))


def _segments(widths, n4):
    per_chip = [[] for _ in range(N_CHIPS)]
    c0 = 0
    for p, w in enumerate(widths):
        a = c0
        while a < c0 + w:
            k = a // n4
            b = min(c0 + w, (k + 1) * n4)
            per_chip[k].append((p, (a - c0, b - c0), (a - k * n4, b - k * n4)))
            a = b
        c0 += w
    assert c0 == N_CHIPS * n4
    return per_chip


def _piece_specs(pieces, n4, tm):
    per_chip = _segments([p.shape[1] for p in pieces], n4)
    specs, local, start = [], [[] for _ in range(N_CHIPS)], 0
    for p, arr in enumerate(pieces):
        chips = [k for k in range(N_CHIPS) if any(seg[0] == p for seg in per_chip[k])]
        lo, hi = chips[0], chips[-1]
        tiled = arr.shape[1] % n4 == 0 and start % n4 == 0
        start += arr.shape[1]
        if tiled:
            imap = lambda k, i, lo=lo, hi=hi: (jnp.where((k >= lo) & (k <= hi), i, 0), jnp.clip(k - lo, 0, hi - lo))
            specs.append(pl.BlockSpec((tm, n4), imap))
        else:
            imap = lambda k, i, lo=lo, hi=hi: (jnp.where((k >= lo) & (k <= hi), i, 0), 0)
            specs.append(pl.BlockSpec((tm, arr.shape[1]), imap))
        for k in chips:
            for q, (pa, pb), cols in per_chip[k]:
                if q == p:
                    local[k].append((p, (0, n4) if tiled else (pa, pb), cols))
    return specs, local


def _dgrad_norm(dys, w4, x, g, dres, *, name, plan=None):
    S, D = x.shape
    n4 = w4.shape[2]
    tm = _row_tile(S, 512)
    per_chip = _segments([a.shape[1] for a in dys], n4)
    n_p = len(dys)

    def body(*refs):
        dy_refs = refs[:n_p]
        w_ref, x_ref, g_ref, dres_ref, dx_ref, dg_ref = refs[n_p:]

        @pl.when(pl.program_id(0) == 0)
        def _():
            dg_ref[...] = jnp.zeros_like(dg_ref)

        dxn = None
        for k in range(N_CHIPS):
            for p, (pa, pb), (ca, cb) in per_chip[k]:
                part = _dot_nt(dy_refs[p][:, pa:pb], w_ref[k, :, ca:cb])
                dxn = part if dxn is None else dxn + part
        xf = x_ref[...]
        r = lax.rsqrt(jnp.mean(xf * xf, axis=-1, keepdims=True) + EPS)
        xhat = xf * r
        dxhat = dxn * g_ref[...]
        dx_ref[...] = dres_ref[...] + r * (dxhat - xhat * jnp.mean(dxhat * xhat, axis=-1, keepdims=True))
        dg_ref[...] += jnp.sum(dxn * xhat, axis=0, keepdims=True)

    row = lambda w: pl.BlockSpec((tm, w), lambda i: (i, 0))
    return _call(
        body, name=name, grid=(S // tm,), plan=plan,
        in_specs=[row(a.shape[1]) for a in dys]
        + [pl.BlockSpec(w4.shape, lambda i: (0, 0, 0), pipeline_mode=pl.Buffered(1)),
           row(D), pl.BlockSpec((1, D), lambda i: (0, 0)), row(D)],
        out_specs=[row(D), pl.BlockSpec((1, D), lambda i: (0, 0))],
        out_shape=[jax.ShapeDtypeStruct((S, D), F32), jax.ShapeDtypeStruct((1, D), F32)],
        args=(*dys, w4, x, g, dres))


def _wgrad(a, dys, *, name, rows, plan=None):
    S, K = a.shape
    n4 = sum(p.shape[1] for p in dys) // N_CHIPS
    tm = _row_tile(S, rows)
    ns = S // tm
    specs, local = _piece_specs(dys, n4, tm)
    n_p = len(dys)

    def body(*refs):
        a_ref = refs[0]
        dy_refs = refs[1:1 + n_p]
        o_ref, o16_ref = refs[1 + n_p:]
        n, s = pl.program_id(0), pl.program_id(1)

        @pl.when(s == 0)
        def _():
            o_ref[...] = jnp.zeros_like(o_ref)

        for k in range(N_CHIPS):
            @pl.when(n == k)
            def _():
                av = a_ref[...]
                for p, (pa, pb), (ca, cb) in local[k]:
                    o_ref[:, ca:cb] += _dot_tn(av, dy_refs[p][:, pa:pb])

        @pl.when(s == ns - 1)
        def _():
            o16_ref[...] = o_ref[...].astype(BF16)

    out = pl.BlockSpec((None, K, n4), lambda n, s: (n, 0, 0))
    return _call(
        body, name=name, grid=(N_CHIPS, ns), parallel=(0,), plan=plan,
        in_specs=[pl.BlockSpec((tm, K), lambda n, s: (s, 0))] + specs,
        out_specs=[out, out],
        out_shape=[jax.ShapeDtypeStruct((N_CHIPS, K, n4), F32), jax.ShapeDtypeStruct((N_CHIPS, K, n4), BF16)],
        args=(a, *dys))


def _tiles(x):
    return x.reshape(x.shape[0] // SUBLANES, SUBLANES, x.shape[1])


def _shift_down(xp, s):
    n = xp.shape[0] - SUBLANES
    if s == SUBLANES:
        return xp[:n, :]
    t = _tiles(xp)
    rot = pltpu.roll(t, s, 1)
    sub = lax.broadcasted_iota(jnp.int32, t.shape, 1)[1:]
    return jnp.where(sub >= s, rot[1:], rot[:-1]).reshape(n, xp.shape[1])


def _shift_up(xn, s):
    n = xn.shape[0] - SUBLANES
    if s == SUBLANES:
        return xn[SUBLANES:, :]
    t = _tiles(xn)
    rot = pltpu.roll(t, SUBLANES - s, 1)
    sub = lax.broadcasted_iota(jnp.int32, t.shape, 1)[1:]
    return jnp.where(sub < SUBLANES - s, rot[:-1], rot[1:]).reshape(n, xn.shape[1])


def _pooled(u, halo, first_tile, row0):
    T = u.shape[0]
    halo = jnp.where(first_tile, 0.0, halo)
    pad = jnp.zeros((SUBLANES, u.shape[1]), F32)
    up = jnp.concatenate([pad, halo, u], axis=0)
    t1 = (row0 + lax.broadcasted_iota(jnp.int32, (T, 1), 0) + 1).astype(F32)
    outs = []
    for gi, w in enumerate(POOL_WINDOWS):
        s = up[:, gi * LANES:(gi + 1) * LANES]
        k = 1
        while k < w:
            if k < SUBLANES:
                s = jnp.concatenate([s[:SUBLANES, :], s[SUBLANES:, :] + _shift_down(s, k)], axis=0)
            else:
                s = s[SUBLANES:, :] + _shift_down(s, k)
            k *= 2
        s = s[-T:, :]
        inv = 1.0 / jnp.minimum(t1, float(w))
        outs.append(s * inv - u[:, gi * LANES:(gi + 1) * LANES])
    return outs


def _pool_mix(u, halo, first_tile, row0, pw_ref, ps_ref):
    pooled = _pooled(u, halo, first_tile, row0)
    outs = []
    for gi in range(len(POOL_WINDOWS)):
        mixed = _dot(pooled[gi].astype(BF16), pw_ref[gi].astype(BF16))
        outs.append((mixed * ps_ref[:, gi * LANES:(gi + 1) * LANES]).astype(BF16))
    return jnp.concatenate(outs, axis=1)


def _pool_grad(u, halo, first_tile, row0, d, dnext, pw_ref, ps_ref, du_ref, dpw_ref, dps_ref):
    T, P = d.shape
    pooled = _pooled(u, halo, first_tile, row0)
    pad = jnp.zeros((SUBLANES, P), F32)
    dext = jnp.concatenate([d, dnext, pad], axis=0)
    t1 = (row0 + lax.broadcasted_iota(jnp.int32, (T + HALO_POOL + SUBLANES, 1), 0) + 1).astype(F32)
    for gi, w in enumerate(POOL_WINDOWS):
        cols = slice(gi * LANES, (gi + 1) * LANES)
        pw = pw_ref[gi].astype(BF16)
        pg = pooled[gi].astype(BF16)
        mixed = _dot(pg, pw)
        dps_ref[:, cols] += jnp.sum(d[:, cols] * mixed, axis=0, keepdims=True)
        dmixed = (dext[:, cols] * ps_ref[:, cols]).astype(BF16)
        dpw_ref[gi] += _dot_tn(pg, dmixed[:T, :])
        dpooled = _dot_nt(dmixed, pw)
        e = dpooled * (1.0 / jnp.minimum(t1, float(w)))
        k = 1
        while k < w:
            if k < SUBLANES:
                e = jnp.concatenate([e[:-SUBLANES, :] + _shift_up(e, k), e[-SUBLANES:, :]], axis=0)
            else:
                e = e[:-SUBLANES, :] + _shift_up(e, k)
            k *= 2
        du_ref[:, cols] = (e[:T, :] - dpooled[:T, :]).astype(BF16)


def _cumsum_rows(x):
    n = x.shape[0]
    row = lax.broadcasted_iota(jnp.int32, x.shape, 0)
    s = 1
    while s < n:
        x = x + jnp.where(row >= s, pltpu.roll(x, s, 0), 0.0)
        s *= 2
    return x


def _rev_cumsum_rows(x):
    n = x.shape[0]
    row = lax.broadcasted_iota(jnp.int32, x.shape, 0)
    s = 1
    while s < n:
        x = x + jnp.where(row < n - s, pltpu.roll(x, n - s, 0), 0.0)
        s *= 2
    return x


def _chunk_prep(zq, zf, lb, b_ref):
    n_sub = CHUNK // SUB
    sq = _sigmoid(zq)
    q = zq * sq
    sf = _sigmoid(zf)
    f = lb + (1.0 - lb) * sf
    k = 1.0 - f
    b = _cumsum_rows(jnp.log(f))
    b_ref[...] = b
    shape = (SUB, b.shape[1])
    ends = [jnp.broadcast_to(b_ref[pl.ds(SUB * j + SUB - 1, 1), :], shape) for j in range(n_sub)]
    mids = [jnp.broadcast_to(b_ref[pl.ds(SUB * j + SUB // 2 - 1, 1), :], shape) for j in range(n_sub)]
    own = [b[SUB * j:SUB * (j + 1), :] for j in range(n_sub)]
    m0 = jnp.concatenate(mids, axis=0)
    e1 = jnp.concatenate(ends, axis=0)
    eq = [jnp.exp(jnp.minimum(b - m0, EXP_CLAMP))]
    for d in range(1, n_sub):
        rd = jnp.concatenate([own[j] if j < d else ends[j - d] for j in range(n_sub)], axis=0)
        eq.append(jnp.exp(b - rd))
    ek0 = jnp.exp(jnp.minimum(m0 - b, EXP_CLAMP))
    ek1 = jnp.exp(e1 - b)
    b_last = b_ref[pl.ds(CHUNK - 1, 1), :]
    return dict(q=q, k=k, f=f, sq=sq, sf=sf, b=b, eq=eq, ek0=ek0, ek1=ek1,
                eb=jnp.exp(b), ekl=jnp.exp(b_last - b), el=jnp.exp(b_last))


def _chunk_masks():
    ti = lax.broadcasted_iota(jnp.int32, (CHUNK, CHUNK), 0)
    si = lax.broadcasted_iota(jnp.int32, (CHUNK, CHUNK), 1)
    shift = SUB.bit_length() - 1
    dsub = jnp.right_shift(ti, shift) - jnp.right_shift(si, shift)
    masks = [(dsub == 0) & (si <= ti)]
    masks += [dsub == d for d in range(1, CHUNK // SUB)]
    return masks


def _chunk_attn(p, masks):
    qd = [(p["q"] * e).astype(BF16) for e in p["eq"]]
    k0 = (p["k"] * p["ek0"]).astype(BF16)
    k1 = (p["k"] * p["ek1"]).astype(BF16)
    a = jnp.where(masks[0], _dot_nt(qd[0], k0), 0.0)
    for d in range(1, len(masks)):
        a = jnp.where(masks[d], _dot_nt(qd[d], k1), a)
    return a, qd, k0, k1


def _hgrn_fwd(z, lb, norm_g, *, name, plan=None):
    S = z.shape[0]
    HW = lb.shape[1]
    NH = HW // LANES
    T = _row_tile(S, 512)
    nc = T // CHUNK

    def body(zq_ref, zf_ref, zi_ref, zo_ref, lb_ref, ng_ref, o_ref, of_ref, st_ref, s_scr, b_scr):
        @pl.when(pl.program_id(0) == 0)
        def _():
            s_scr[...] = jnp.zeros_like(s_scr)

        ng = ng_ref[...]
        masks = _chunk_masks()

        def chunk(c, carry):
            rows = pl.ds(pl.multiple_of(c * CHUNK, CHUNK), CHUNK)
            for h in range(NH):
                cols = slice(h * LANES, (h + 1) * LANES)
                p = _chunk_prep(zq_ref[rows, cols], zf_ref[rows, cols], lb_ref[:, cols], b_scr.at[h])
                v = zi_ref[rows, cols].astype(BF16)
                zo = zo_ref[rows, cols]
                st = s_scr[h]
                st_ref[c, h] = st
                a, _, _, _ = _chunk_attn(p, masks)
                o = _dot(a.astype(BF16), v) + _dot_nt((p["q"] * p["eb"]).astype(BF16), st.astype(BF16))
                s_scr[h] = st * p["el"] + _dot_tn(v, (p["k"] * p["ekl"]).astype(BF16))
                o_ref[rows, cols] = o
                r = lax.rsqrt(jnp.mean(o * o, axis=-1, keepdims=True) + EPS)
                of_ref[rows, cols] = (o * r * ng * (zo * _sigmoid(zo))).astype(BF16)
            return carry

        lax.fori_loop(0, nc, chunk, 0, unroll=8)

    part = lambda k: pl.BlockSpec((T, HW), lambda i, k=k: (i, k))
    return _call(
        body, name=name, grid=(S // T,), plan=plan,
        in_specs=[part(1), part(2), part(3), part(4),
                  pl.BlockSpec((1, HW), lambda i: (0, 0)), pl.BlockSpec((1, LANES), lambda i: (0, 0))],
        out_specs=[pl.BlockSpec((T, HW), lambda i: (i, 0)), pl.BlockSpec((T, HW), lambda i: (i, 0)),
                   pl.BlockSpec((nc, NH, LANES, LANES), lambda i: (i, 0, 0, 0))],
        out_shape=[jax.ShapeDtypeStruct((S, HW), F32), jax.ShapeDtypeStruct((S, HW), BF16),
                   jax.ShapeDtypeStruct((S // CHUNK, NH, LANES, LANES), F32)],
        scratch=[pltpu.VMEM((NH, LANES, LANES), F32), pltpu.VMEM((NH, CHUNK, LANES), F32)],
        args=(z, z, z, z, lb, norm_g))


def _hgrn_bwd(z, lb, norm_g, o_raw, states, dof, *, name, plan=None):
    S = z.shape[0]
    HW = lb.shape[1]
    NH = HW // LANES
    T = _row_tile(S, 512)
    nc = T // CHUNK
    nt = S // T

    def body(zq_ref, zf_ref, zi_ref, zo_ref, lb_ref, ng_ref, o_ref, st_ref, dof_ref,
             dzq_ref, dzf_ref, dzi_ref, dzo_ref, dlb_ref, dng_ref, ds_scr, b_scr):
        @pl.when(pl.program_id(0) == 0)
        def _():
            ds_scr[...] = jnp.zeros_like(ds_scr)
            dlb_ref[...] = jnp.zeros_like(dlb_ref)
            dng_ref[...] = jnp.zeros_like(dng_ref)

        ng = ng_ref[...]
        masks = _chunk_masks()
        last_row = lax.broadcasted_iota(jnp.int32, (CHUNK, 1), 0) == CHUNK - 1

        def chunk(cr, carry):
            c = nc - 1 - cr
            rows = pl.ds(pl.multiple_of(c * CHUNK, CHUNK), CHUNK)
            for h in range(NH):
                cols = slice(h * LANES, (h + 1) * LANES)
                lbv = lb_ref[:, cols]
                zq, zf, zo = zq_ref[rows, cols], zf_ref[rows, cols], zo_ref[rows, cols]
                o = o_ref[rows, cols]
                dof_c = dof_ref[rows, cols]
                st = st_ref[c, h]
                dst = ds_scr[h]

                so = _sigmoid(zo)
                r = lax.rsqrt(jnp.mean(o * o, axis=-1, keepdims=True) + EPS)
                ohat = o * r
                d_on = dof_c * (zo * so)
                dzo_ref[rows, cols] = (dof_c * ohat * ng * _dsilu(zo, so)).astype(BF16)
                dng_ref[:, cols] += jnp.sum(d_on * ohat, axis=0, keepdims=True)
                dohat = d_on * ng
                do = (r * (dohat - ohat * jnp.mean(dohat * ohat, axis=-1, keepdims=True))).astype(BF16)

                p = _chunk_prep(zq, zf, lbv, b_scr.at[h])
                q, k = p["q"], p["k"]
                v = zi_ref[rows, cols].astype(BF16)
                a, qd, k0, k1 = _chunk_attn(p, masks)
                ktl = (k * p["ekl"]).astype(BF16)
                dstb = dst.astype(BF16)

                da = _dot_nt(do, v)
                dzi_ref[rows, cols] = (_dot_tn(a.astype(BF16), do) + _dot_nt(ktl, dstb)).astype(BF16)

                da0 = jnp.where(masks[0], da, 0.0).astype(BF16)
                rq = _dot(da0, k0)
                rk0 = _dot_tn(da0, qd[0])
                dq = rq * p["eq"][0]
                db = qd[0].astype(F32) * rq - k0.astype(F32) * rk0
                rk1 = jnp.zeros_like(rk0)
                for d in range(1, len(masks)):
                    dad = jnp.where(masks[d], da, 0.0).astype(BF16)
                    rq = _dot(dad, k1)
                    dq = dq + rq * p["eq"][d]
                    db = db + qd[d].astype(F32) * rq
                    rk1 = rk1 + _dot_tn(dad, qd[d])
                dk = rk0 * p["ek0"] + rk1 * p["ek1"]
                db = db - k1.astype(F32) * rk1
                qe = (q * p["eb"]).astype(BF16)
                rq = _dot(do, st.astype(BF16))
                dq = dq + rq * p["eb"]
                db = db + qe.astype(F32) * rq
                rk = _dot(v, dstb)
                dk = dk + rk * p["ekl"]
                db = db - ktl.astype(F32) * rk

                st_new = st * p["el"] + _dot_tn(v, ktl)
                db = db + jnp.where(last_row, jnp.sum(dstb.astype(F32) * st_new, axis=0, keepdims=True), 0.0)
                dg = _rev_cumsum_rows(db)
                ds_scr[h] = dst * p["el"] + _dot_tn(do, qe)

                dzq_ref[rows, cols] = (dq * _dsilu(zq, p["sq"])).astype(BF16)
                df = dg / p["f"] - dk
                sf = p["sf"]
                dzf_ref[rows, cols] = (df * (1.0 - lbv) * sf * (1.0 - sf)).astype(BF16)
                dlb_ref[:, cols] += jnp.sum(df * (1.0 - sf), axis=0, keepdims=True)
            return carry

        lax.fori_loop(0, nc, chunk, 0, unroll=4)

    rev = lambda i: nt - 1 - i
    part = lambda k: pl.BlockSpec((T, HW), lambda i, k=k: (rev(i), k))
    blk = pl.BlockSpec((T, HW), lambda i: (rev(i), 0))
    vec = pl.BlockSpec((1, HW), lambda i: (0, 0))
    return _call(
        body, name=name, grid=(nt,), plan=plan,
        in_specs=[part(1), part(2), part(3), part(4), vec, pl.BlockSpec((1, LANES), lambda i: (0, 0)),
                  blk, pl.BlockSpec((nc, NH, LANES, LANES), lambda i: (rev(i), 0, 0, 0)), blk],
        out_specs=[blk, blk, blk, blk, vec, vec],
        out_shape=[jax.ShapeDtypeStruct((S, HW), BF16)] * 4 + [jax.ShapeDtypeStruct((1, HW), F32)] * 2,
        scratch=[pltpu.VMEM((NH, LANES, LANES), F32), pltpu.VMEM((NH, CHUNK, LANES), F32)],
        args=(z, z, z, z, lb, norm_g, o_raw, states, dof))


def _gate_specs(T, D, rows=lambda i: i):
    half = D // 2
    return [pl.BlockSpec((T, half), lambda i, k=k: (rows(i), 5 + k)) for k in range(4)]


def _gates(zg_refs, bg_ref, D):
    half = D // 2
    za = jnp.concatenate([zg_refs[0][...], zg_refs[1][...]], axis=1) + bg_ref[:, :D]
    zb = jnp.concatenate([zg_refs[2][...], zg_refs[3][...]], axis=1) + bg_ref[:, D:]
    return _sigmoid(za), _sigmoid(zb)


def _mix_fwd(x, of, z, pool_w, pool_scale, b_gate, w_pa4, w_pb4, w_o4, g_next, *, name, plan=None):
    S, D = x.shape
    P = of.shape[1]
    T = _row_tile(S, 512)
    hb = T // HALO_POOL

    def body(x_ref, u_ref, halo_ref, of_ref, g0, g1, g2, g3, pw_ref, ps_ref, bg_ref, wpa_ref, wpb_ref, wo_ref,
             gn_ref, xo_ref, ya_ref, yb_ref, xn_ref, pm_ref):
        i = pl.program_id(0)
        pmv = _pool_mix(u_ref[...], halo_ref[...], i == 0, i * T, pw_ref, ps_ref)
        pm_ref[...] = pmv
        ofv = of_ref[...]
        ya = jnp.concatenate([_dot(pmv, wpa_ref[k]) for k in range(N_CHIPS)], axis=1)
        yb = jnp.concatenate([_dot(ofv, wpb_ref[k]) for k in range(N_CHIPS)], axis=1)
        ga, gb = _gates((g0, g1, g2, g3), bg_ref, D)
        merged = (ga * ya + gb * yb).astype(BF16)
        x_mid = x_ref[...] + _dot(merged, wo_ref[...].reshape(D, D))
        xo_ref[...] = x_mid
        xn_ref[...] = _rms(x_mid, gn_ref[...])
        ya_ref[...] = ya.astype(BF16)
        yb_ref[...] = yb.astype(BF16)

    row = lambda w: pl.BlockSpec((T, w), lambda i: (i, 0))
    full = lambda a: pl.BlockSpec(a.shape, lambda i: (0,) * a.ndim)
    return _call(
        body, name=name, grid=(S // T,), parallel=(0,), plan=plan,
        in_specs=[row(D), row(P), pl.BlockSpec((HALO_POOL, P), lambda i: (jnp.maximum(i * hb - 1, 0), 0)), row(P)]
        + _gate_specs(T, D) + [full(pool_w), full(pool_scale), full(b_gate), full(w_pa4), full(w_pb4), full(w_o4),
                               full(g_next)],
        out_specs=[row(D), row(D), row(D), row(D), row(P)],
        out_shape=[jax.ShapeDtypeStruct((S, D), F32), jax.ShapeDtypeStruct((S, D), BF16),
                   jax.ShapeDtypeStruct((S, D), BF16), jax.ShapeDtypeStruct((S, D), BF16),
                   jax.ShapeDtypeStruct((S, P), BF16)],
        args=(x, z, z, of, z, z, z, z, pool_w, pool_scale, b_gate, w_pa4, w_pb4, w_o4, g_next))


def _mix_bwd(dxm, ya, yb, z, b_gate, pm, of, pool_w, pool_scale, w_pa4, w_pb4, w_o4, *, name):
    S, D = dxm.shape
    P = pm.shape[1]
    q4 = D // N_CHIPS
    T = _row_tile(S, 512)
    nt = S // T
    hb = T // HALO_POOL

    def body(dx_ref, ya_ref, yb_ref, g0, g1, g2, g3, bg_ref, pm_ref, of_ref, u_ref, halo_ref, pw_ref, ps_ref,
             wpa_ref, wpb_ref, wo_ref,
             dzg_ref, dof_ref, du_ref, dwo_ref, dwpa_ref, dwpb_ref, dbg_ref, dpw_ref, dps_ref,
             dwo16_ref, dwpa16_ref, dwpb16_ref, next_scr):
        i = pl.program_id(0)

        @pl.when(i == 0)
        def _():
            for ref in (dwo_ref, dwpa_ref, dwpb_ref, dbg_ref, dpw_ref, dps_ref, next_scr):
                ref[...] = jnp.zeros_like(ref)

        dxb = dx_ref[...].astype(BF16)
        ya = ya_ref[...].astype(F32)
        yb = yb_ref[...].astype(F32)
        ga, gb = _gates((g0, g1, g2, g3), bg_ref, D)
        merged = (ga * ya + gb * yb).astype(BF16)
        dwo_ref[...] += _dot_tn(merged, dxb).reshape(N_CHIPS, q4, D)
        dm = _dot_nt(dxb, wo_ref[...].reshape(D, D))
        dza = dm * ya * ga * (1.0 - ga)
        dzb = dm * yb * gb * (1.0 - gb)
        dzg_ref[:, :D] = dza.astype(BF16)
        dzg_ref[:, D:] = dzb.astype(BF16)
        dbg_ref[:, :D] += jnp.sum(dza, axis=0, keepdims=True)
        dbg_ref[:, D:] += jnp.sum(dzb, axis=0, keepdims=True)
        dya = (dm * ga).astype(BF16)
        dyb = (dm * gb).astype(BF16)
        pmv, ofv = pm_ref[...], of_ref[...]
        dpm = jnp.zeros((T, P), F32)
        dof = jnp.zeros((T, P), F32)
        for k in range(N_CHIPS):
            cols = slice(k * q4, (k + 1) * q4)
            dwpa_ref[k] += _dot_tn(pmv, dya[:, cols])
            dwpb_ref[k] += _dot_tn(ofv, dyb[:, cols])
            dpm = dpm + _dot_nt(dya[:, cols], wpa_ref[k])
            dof = dof + _dot_nt(dyb[:, cols], wpb_ref[k])
        dof_ref[...] = dof
        _pool_grad(u_ref[...], halo_ref[...], i == nt - 1, (nt - 1 - i) * T, dpm, next_scr[...], pw_ref, ps_ref,
                   du_ref, dpw_ref, dps_ref)
        next_scr[...] = dpm[:HALO_POOL, :]

        @pl.when(i == nt - 1)
        def _():
            dwo16_ref[...] = dwo_ref[...].astype(BF16)
            dwpa16_ref[...] = dwpa_ref[...].astype(BF16)
            dwpb16_ref[...] = dwpb_ref[...].astype(BF16)

    rev = lambda i: nt - 1 - i
    row = lambda w: pl.BlockSpec((T, w), lambda i: (rev(i), 0))
    full = lambda a: pl.BlockSpec(a.shape, lambda i: (0,) * a.ndim, pipeline_mode=pl.Buffered(1))
    like = lambda a, dt: jax.ShapeDtypeStruct(a.shape, dt)
    return _call(
        body, name=name, grid=(nt,),
        in_specs=[row(D), row(D), row(D)] + _gate_specs(T, D, rev)
        + [full(b_gate), row(P), row(P), row(P),
           pl.BlockSpec((HALO_POOL, P), lambda i: (jnp.maximum(rev(i) * hb - 1, 0), 0)),
           full(pool_w), full(pool_scale), full(w_pa4), full(w_pb4), full(w_o4)],
        out_specs=[row(2 * D), row(P), row(P), full(w_o4), full(w_pa4), full(w_pb4), full(b_gate),
                   full(pool_w), full(pool_scale), full(w_o4), full(w_pa4), full(w_pb4)],
        out_shape=[jax.ShapeDtypeStruct((S, 2 * D), BF16), jax.ShapeDtypeStruct((S, P), F32),
                   jax.ShapeDtypeStruct((S, P), BF16), like(w_o4, F32), like(w_pa4, F32), like(w_pb4, F32),
                   like(b_gate, F32), like(pool_w, F32), like(pool_scale, F32),
                   like(w_o4, BF16), like(w_pa4, BF16), like(w_pb4, BF16)],
        scratch=[pltpu.VMEM((HALO_POOL, P), F32)],
        args=(dxm, ya, yb, z, z, z, z, b_gate, pm, of, z, z, pool_w, pool_scale, w_pa4, w_pb4, w_o4))[0]


def _up_conv(xn, w_up4, conv_w, conv_b, *, name, plan=None):
    S, D = xn.shape
    f4 = w_up4.shape[2]
    nf = N_CHIPS // 2
    F = nf * f4
    T = _row_tile(S, 512)

    def body(xn_ref, wv_ref, wg_ref, cwv_ref, cwg_ref, cbv_ref, cbg_ref,
             hv_ref, hg_ref, val_ref, gate_ref, a_ref, pv_scr, pg_scr):
        i = pl.program_id(1)
        xv = xn_ref[...]

        def side(w_ref, cw_ref, cb_ref, h_ref, p_scr):
            h = _dot(xv, w_ref[...])
            h_ref[...] = h.astype(BF16)
            hp = jnp.concatenate([jnp.where(i == 0, 0.0, p_scr[...]), h], axis=0)
            p_scr[...] = h[-SUBLANES:, :]
            cw = cw_ref[...]
            return cw[0:1, :] * _shift_down(hp, 2) + cw[1:2, :] * _shift_down(hp, 1) + cw[2:3, :] * h + cb_ref[...]

        val = side(wv_ref, cwv_ref, cbv_ref, hv_ref, pv_scr)
        gate = side(wg_ref, cwg_ref, cbg_ref, hg_ref, pg_scr)
        val16, gate16 = val.astype(BF16), gate.astype(BF16)
        val_ref[...] = val16
        gate_ref[...] = gate16
        a_ref[...] = gate16 * _sigmoid(gate16) * val16

    out = pl.BlockSpec((T, f4), lambda f, i: (i, f))
    return _call(
        body, name=name, grid=(nf, S // T), plan=plan,
        in_specs=[pl.BlockSpec((T, D), lambda f, i: (i, 0)),
                  pl.BlockSpec((None, D, f4), lambda f, i: (f, 0, 0)),
                  pl.BlockSpec((None, D, f4), lambda f, i: (nf + f, 0, 0)),
                  pl.BlockSpec((3, f4), lambda f, i: (0, f)),
                  pl.BlockSpec((3, f4), lambda f, i: (0, nf + f)),
                  pl.BlockSpec((1, f4), lambda f, i: (0, f)),
                  pl.BlockSpec((1, f4), lambda f, i: (0, nf + f))],
        out_specs=[out] * 5,
        out_shape=[jax.ShapeDtypeStruct((S, F), BF16)] * 5,
        scratch=[pltpu.VMEM((SUBLANES, f4), F32), pltpu.VMEM((SUBLANES, f4), F32)],
        args=(xn, w_up4, w_up4, conv_w, conv_w, conv_b, conv_b))


def _down(a, w_down4, x, g_next, *, name, plan=None):
    S, F = a.shape
    D = x.shape[1]
    T = _row_tile(S, 1024)

    def body(a_ref, wd_ref, x_ref, gn_ref, o_ref, xn_ref):
        x_out = x_ref[...] + _dot(a_ref[...], wd_ref[...].reshape(F, D))
        o_ref[...] = x_out
        xn_ref[...] = _rms(x_out, gn_ref[...])

    row = lambda w: pl.BlockSpec((T, w), lambda i: (i, 0))
    return _call(
        body, name=name, grid=(S // T,), parallel=(0,), plan=plan,
        in_specs=[row(F), pl.BlockSpec(w_down4.shape, lambda i: (0, 0, 0), pipeline_mode=pl.Buffered(1)),
                  row(D), pl.BlockSpec((1, D), lambda i: (0, 0))],
        out_specs=[row(D), row(D)],
        out_shape=[jax.ShapeDtypeStruct((S, D), F32), jax.ShapeDtypeStruct((S, D), BF16)],
        args=(a, w_down4, x, g_next))


def _ffn_down_bwd(dxo, hv, hg, val16, gate16, a16, conv_w, w_down4, *, name, plan=None):
    S = hv.shape[0]
    _, f4, D = w_down4.shape
    F = N_CHIPS * f4
    T = _row_tile(S, 512)
    nf = 2
    tf = 2 * f4
    nt = S // T

    def body(dx_ref, hv_ref, hg_ref, val_ref, gate_ref, a_ref, cwv_ref, cwg_ref, wd_ref,
             dhv_ref, dhg_ref, dwd_ref, dwd16_ref, dcwv_ref, dcwg_ref, dcbv_ref, dcbg_ref, cv_scr, cg_scr):
        i = pl.program_id(1)

        @pl.when(i == 0)
        def _():
            cv_scr[...] = jnp.zeros_like(cv_scr)
            cg_scr[...] = jnp.zeros_like(cg_scr)
            dwd_ref[...] = jnp.zeros_like(dwd_ref)
            dcwv_ref[...] = jnp.zeros_like(dcwv_ref)
            dcwg_ref[...] = jnp.zeros_like(dcwg_ref)
            dcbv_ref[...] = jnp.zeros_like(dcbv_ref)
            dcbg_ref[...] = jnp.zeros_like(dcbg_ref)

        dxb = dx_ref[...].astype(BF16)
        val = val_ref[...].astype(F32)
        gate = gate_ref[...].astype(F32)
        sg = _sigmoid(gate)
        sil = gate * sg
        dwd_ref[...] += _dot_tn(a_ref[...], dxb).reshape(2, f4, D)
        da = _dot_nt(dxb, wd_ref[...].reshape(tf, D))

        def conv_bwd(dhc, h0, cw, c_scr, dh_ref, dcw_ref, dcb_ref):
            ext = jnp.concatenate([dhc, c_scr[...]], axis=0)
            n1 = _shift_up(ext, 1)
            n2 = _shift_up(ext, 2)
            dh_ref[...] = (cw[2:3, :] * dhc + cw[1:2, :] * n1 + cw[0:1, :] * n2).astype(BF16)
            c_scr[...] = dhc[:SUBLANES, :]
            dcw_ref[0:1, :] += jnp.sum(n2 * h0, axis=0, keepdims=True)
            dcw_ref[1:2, :] += jnp.sum(n1 * h0, axis=0, keepdims=True)
            dcw_ref[2:3, :] += jnp.sum(dhc * h0, axis=0, keepdims=True)
            dcb_ref[...] += jnp.sum(dhc, axis=0, keepdims=True)

        conv_bwd(da * sil, hv_ref[...].astype(F32), cwv_ref[...], cv_scr, dhv_ref, dcwv_ref, dcbv_ref)
        conv_bwd(da * val * _dsilu(gate, sg), hg_ref[...].astype(F32), cwg_ref[...], cg_scr, dhg_ref, dcwg_ref,
                 dcbg_ref)

        @pl.when(i == nt - 1)
        def _():
            dwd16_ref[...] = dwd_ref[...].astype(BF16)

    rev = lambda i: nt - 1 - i
    wd_spec = pl.BlockSpec((2, f4, D), lambda f, i: (f, 0, 0))
    return _call(
        body, name=name, grid=(nf, nt), plan=plan,
        in_specs=[pl.BlockSpec((T, D), lambda f, i: (rev(i), 0)),
                  pl.BlockSpec((T, tf), lambda f, i: (rev(i), f)),
                  pl.BlockSpec((T, tf), lambda f, i: (rev(i), f)),
                  pl.BlockSpec((T, tf), lambda f, i: (rev(i), f)),
                  pl.BlockSpec((T, tf), lambda f, i: (rev(i), f)),
                  pl.BlockSpec((T, tf), lambda f, i: (rev(i), f)),
                  pl.BlockSpec((3, tf), lambda f, i: (0, f)),
                  pl.BlockSpec((3, tf), lambda f, i: (0, nf + f)),
                  wd_spec],
        out_specs=[pl.BlockSpec((T, tf), lambda f, i: (rev(i), f)),
                   pl.BlockSpec((T, tf), lambda f, i: (rev(i), f)),
                   wd_spec, wd_spec,
                   pl.BlockSpec((3, tf), lambda f, i: (0, f)),
                   pl.BlockSpec((3, tf), lambda f, i: (0, f)),
                   pl.BlockSpec((1, tf), lambda f, i: (0, f)),
                   pl.BlockSpec((1, tf), lambda f, i: (0, f))],
        out_shape=[jax.ShapeDtypeStruct((S, F), BF16), jax.ShapeDtypeStruct((S, F), BF16),
                   jax.ShapeDtypeStruct((N_CHIPS, f4, D), F32), jax.ShapeDtypeStruct((N_CHIPS, f4, D), BF16),
                   jax.ShapeDtypeStruct((3, F), F32), jax.ShapeDtypeStruct((3, F), F32),
                   jax.ShapeDtypeStruct((1, F), F32), jax.ShapeDtypeStruct((1, F), F32)],
        scratch=[pltpu.VMEM((SUBLANES, tf), F32), pltpu.VMEM((SUBLANES, tf), F32)],
        args=(dxo, hv, hg, val16, gate16, a16, conv_w, conv_w, w_down4))


def _down_loss(a, w_down4, x, g, target, *, name):
    S, F = a.shape
    D = x.shape[1]
    T = _row_tile(S, 512)

    def body(a_ref, wd_ref, x_ref, g_ref, t_ref, loss_ref, dx_ref, dg_ref):
        @pl.when(pl.program_id(0) == 0)
        def _():
            loss_ref[...] = jnp.zeros_like(loss_ref)
            dg_ref[...] = jnp.zeros_like(dg_ref)

        xf = x_ref[...] + _dot(a_ref[...], wd_ref[...].reshape(F, D))
        r = lax.rsqrt(jnp.mean(xf * xf, axis=-1, keepdims=True) + EPS)
        xhat = xf * r
        err = xhat * g_ref[...] - t_ref[...]
        loss_ref[...] += jnp.sum(err * err, axis=0, keepdims=True) * (0.5 / D)
        dy = err * (1.0 / D)
        dxhat = dy * g_ref[...]
        dx_ref[...] = r * (dxhat - xhat * jnp.mean(dxhat * xhat, axis=-1, keepdims=True))
        dg_ref[...] += jnp.sum(dy * xhat, axis=0, keepdims=True)

    row = lambda w: pl.BlockSpec((T, w), lambda i: (i, 0))
    vec = pl.BlockSpec((1, D), lambda i: (0, 0))
    return _call(
        body, name=name, grid=(S // T,),
        in_specs=[row(F), pl.BlockSpec(w_down4.shape, lambda i: (0, 0, 0), pipeline_mode=pl.Buffered(1)),
                  row(D), vec, row(D)],
        out_specs=[vec, row(D), vec],
        out_shape=[jax.ShapeDtypeStruct((1, D), F32), jax.ShapeDtypeStruct((S, D), F32),
                   jax.ShapeDtypeStruct((1, D), F32)],
        args=(a, w_down4, x, g, target))[0]


BIG = ("w_in", "w_pa", "w_pb", "w_o", "w_up", "w_down")
SMALL = ("norm1_g", "b_gate", "pool_w", "pool_scale", "lb_logits", "hgrn_norm_g", "norm2_g", "conv_b", "final_g")
WEIGHTS = ("norm1_g", "w_in", "b_gate", "pool_w", "pool_scale", "lb_logits", "hgrn_norm_g", "w_pa", "w_pb", "w_o",
           "norm2_g", "w_up", "conv_w", "conv_b", "w_down", "final_g")


def _lower_bounds(lb_logits):
    soft = jax.nn.softmax(lb_logits.astype(F32), axis=0)
    cum = jnp.cumsum(soft, axis=0)
    return cum - cum[0:1]


def _step(x, target, sm, wts, shards=None):
    L = sm["norm1_g"].shape[0]
    wts = dict(wts)
    dist = shards is not None
    lbs, lb_vjp = jax.vjp(_lower_bounds, sm["lb_logits"])
    row = lambda a: a.reshape(1, -1)
    conv_w = sm.get("conv_w")

    def gather(names_layers, with_conv=False):
        items = [(shards[n], "rows", l) for n, l in names_layers]
        if with_conv:
            items.append((shards["conv_w"], "layer", None))
        return _GatherPlan(items)

    def landed(names_layers, outs):
        for key, arr in zip(names_layers, outs):
            wts[key] = arr

    own = {"in_proj": ("w_up",)}
    first = {"hgrn_fwd": ("w_pa", "w_pb", "w_o"), "mix_fwd": ("w_down",)}
    ahead = {"up": ("w_in", "w_pa", "w_pb", "w_o", "w_down")}
    conv_rider = "mix_fwd"

    def riders(l, kernel):
        if not dist:
            return [], None
        keys = [(n, l) for n in own.get(kernel, ())]
        keys += [(n, l) for n in first.get(kernel, ())] if l == 0 else []
        keys += [(n, l + 1) for n in ahead.get(kernel, ())] if l + 1 < L else []
        with_conv = l == 0 and kernel == conv_rider
        return keys, (gather(keys, with_conv) if keys or with_conv else None)

    keys = [("w_in", 0)] if dist else []
    (xn1,), got = _rmsnorm(x, row(sm["norm1_g"][0]), name="norm_in", plan=gather(keys) if keys else None)
    landed(keys, got)

    saved = []
    for l in range(L):
        keys, plan = riders(l, "in_proj")
        (z,), got = _matmul(xn1, wts[("w_in", l)], name=f"in_proj_{l}", plan=plan)
        landed(keys, got)
        keys, plan = riders(l, "hgrn_fwd")
        (o_raw, of, states), got = _hgrn_fwd(z, row(lbs[l]), row(sm["hgrn_norm_g"][l]), name=f"hgrn_fwd_{l}",
                                             plan=plan)
        landed(keys, got)
        keys, plan = riders(l, "mix_fwd")
        (x_mid, ya, yb, xn2, pm), got = _mix_fwd(
            x, of, z, sm["pool_w"][l], row(sm["pool_scale"][l]), row(sm["b_gate"][l]), wts[("w_pa", l)],
            wts[("w_pb", l)], wts[("w_o", l)], row(sm["norm2_g"][l]), name=f"mix_fwd_{l}", plan=plan)
        landed(keys, got)
        if dist and l == 0:
            full = got[-1]
            conv_w = jnp.concatenate([full[:, k] for k in range(N_CHIPS)], axis=2)
        keys, plan = riders(l, "up")
        (hv, hg, val16, gate16, a16), got = _up_conv(xn2, wts[("w_up", l)], conv_w[l], row(sm["conv_b"][l]),
                                                     name=f"up_{l}", plan=plan)
        landed(keys, got)
        saved.append(dict(x=x, xn1=xn1, z=z, pm=pm, o_raw=o_raw, of=of, states=states,
                          x_mid=x_mid, ya=ya, yb=yb, xn2=xn2, hv=hv, hg=hg, val16=val16, gate16=gate16, a16=a16))
        if l + 1 < L:
            keys, plan = riders(l, "down")
            (x, xn1), got = _down(a16, wts[("w_down", l)], x_mid, row(sm["norm1_g"][l + 1]), name=f"down_{l}",
                                  plan=plan)
            landed(keys, got)
        else:
            loss_cols, dx, d_final_g = _down_loss(a16, wts[("w_down", l)], x_mid, row(sm["final_g"]), target,
                                                  name="down_loss")

    small = {k: [None] * L for k in ("norm1_g", "b_gate", "pool_w", "pool_scale", "hgrn_norm_g", "norm2_g",
                                     "conv_w", "conv_b")}
    big32, big16, recv = {}, {}, {}
    dlbs = [None] * L
    pending = []

    def scatter():
        if not (dist and pending):
            return [], None
        keys = list(pending)
        del pending[:]
        return keys, _ScatterPlan([big16[k] for k in keys])

    def sent(keys, outs):
        for key, arr in zip(keys, outs):
            recv[key] = arr

    def made(name, l, g32, g16):
        big32[(name, l)], big16[(name, l)] = g32, g16
        pending.append((name, l))

    for l in reversed(range(L)):
        s = saved[l]
        keys, plan = scatter()
        (dhv, dhg, d_wd, d_wd16, dcwv, dcwg, dcbv, dcbg), got = _ffn_down_bwd(
            dx, s["hv"], s["hg"], s["val16"], s["gate16"], s["a16"], conv_w[l], wts[("w_down", l)], name=f"down_bwd_{l}",
            plan=plan)
        sent(keys, got)
        made("w_down", l, d_wd, d_wd16)
        small["conv_w"][l] = jnp.concatenate([dcwv, dcwg], axis=1)
        small["conv_b"][l] = jnp.concatenate([dcbv, dcbg], axis=1)[0]
        keys, plan = scatter()
        (d_wu, d_wu16), got = _wgrad(s["xn2"], [dhv, dhg], name=f"up_wgrad_{l}", rows=2048, plan=plan)
        sent(keys, got)
        made("w_up", l, d_wu, d_wu16)
        (dxm, dg2), _ = _dgrad_norm([dhv, dhg], wts[("w_up", l)], s["x_mid"], row(sm["norm2_g"][l]), dx,
                                    name=f"up_dgrad_{l}")
        small["norm2_g"][l] = dg2[0]

        dzg, dof, du, d_wo, d_wpa, d_wpb, dbg, dpw, dps, d_wo16, d_wpa16, d_wpb16 = _mix_bwd(
            dxm, s["ya"], s["yb"], s["z"], row(sm["b_gate"][l]), s["pm"], s["of"], sm["pool_w"][l],
            row(sm["pool_scale"][l]), wts[("w_pa", l)], wts[("w_pb", l)], wts[("w_o", l)], name=f"mix_bwd_{l}")
        small["pool_w"][l], small["pool_scale"][l] = dpw, dps[0]
        made("w_o", l, d_wo, d_wo16)
        made("w_pa", l, d_wpa, d_wpa16)
        made("w_pb", l, d_wpb, d_wpb16)
        small["b_gate"][l] = dbg[0]

        keys, plan = scatter()
        (dzq, dzf, dzi, dzo, dlb, dng), got = _hgrn_bwd(s["z"], row(lbs[l]), row(sm["hgrn_norm_g"][l]), s["o_raw"],
                                                      s["states"], dof, name=f"hgrn_bwd_{l}", plan=plan)
        sent(keys, got)
        dlbs[l] = dlb[0]
        small["hgrn_norm_g"][l] = jnp.sum(dng.reshape(-1, LANES), axis=0)

        dz = [du, dzq, dzf, dzi, dzo, dzg]
        (d_wi, d_wi16), _ = _wgrad(s["xn1"], dz, name=f"in_wgrad_{l}", rows=1024)
        made("w_in", l, d_wi, d_wi16)
        keys, plan = scatter()
        (dx, dg1), got = _dgrad_norm(dz, wts[("w_in", l)], s["x"], row(sm["norm1_g"][l]), dxm,
                                     name=f"in_dgrad_{l}", plan=plan)
        sent(keys, got)
        small["norm1_g"][l] = dg1[0]

    out = {k: jnp.stack(v) for k, v in small.items()}
    out["lb_logits"] = lb_vjp(jnp.stack(dlbs))[0]
    out["final_g"] = d_final_g[0]
    return loss_cols, dx, out, big32, recv


def _elementwise_rows(R, n, n_arrays):
    if 2 * n_arrays * R * n * 4 <= VMEM_LIMIT // 4 or R % 8:
        return R
    block = VMEM_LIMIT // 2 // (2 * n_arrays)
    want = 8
    while want * 2 * n * 4 <= block:
        want *= 2
    return _row_tile(R, want)


def _sum_layers(own, got, chip, *, name):
    L = len(own)
    _, r, n = own[0].shape
    T = _elementwise_rows(r, n, 6)
    nt = r // T

    def body(chip_ref, *refs):
        o_ref = refs[-1]
        l = pl.program_id(0)
        for k in range(L):
            @pl.when(l == k)
            def _():
                own_ref, got_ref = refs[2 * k], refs[2 * k + 1]
                acc = own_ref[...]
                for j in range(3):
                    acc = acc + got_ref[j].astype(F32)
                o_ref[...] = acc

    in_specs = []
    for k in range(L):
        hold = 0 if k else nt - 1
        in_specs.append(pl.BlockSpec((None, T, n), lambda l, i, c, k=k, hold=hold: (c[0], jnp.where(l == k, i, hold), 0)))
        in_specs.append(pl.BlockSpec((3, T, n), lambda l, i, c, k=k, hold=hold: (0, jnp.where(l == k, i, hold), 0)))
    grid_spec = pltpu.PrefetchScalarGridSpec(
        num_scalar_prefetch=1, grid=(L, nt), in_specs=in_specs,
        out_specs=pl.BlockSpec((None, T, n), lambda l, i, c: (l, i, 0)))
    args = [a for pair in zip(own, got) for a in pair]
    return pl.pallas_call(
        body, name=name, grid_spec=grid_spec, out_shape=jax.ShapeDtypeStruct((L, r, n), F32),
        compiler_params=pltpu.CompilerParams(dimension_semantics=("arbitrary", "arbitrary"),
                                             vmem_limit_bytes=VMEM_LIMIT),
    )(chip, *args)


def _sum_stack(parts, *, name):
    K, R, n = parts.shape
    T = _elementwise_rows(R, n, K + 1)

    def body(p_ref, o_ref):
        acc = p_ref[0]
        for j in range(1, K):
            acc = acc + p_ref[j]
        o_ref[...] = acc

    return _call(
        body, name=name, grid=(R // T,), parallel=(0,),
        in_specs=[pl.BlockSpec((K, T, n), lambda i: (0, i, 0))],
        out_specs=[pl.BlockSpec((T, n), lambda i: (i, 0))],
        out_shape=[jax.ShapeDtypeStruct((R, n), F32)],
        args=(parts,))[0][0]


def _adamw(w, m, v, g_parts, *, name):
    R, n = w.shape
    n_g = len(g_parts)
    T = _elementwise_rows(R, n, 7 + n_g)

    def body(*refs):
        w_ref, m_ref, v_ref = refs[:3]
        g_refs = refs[3:3 + n_g]
        go_ref, d_ref, mo_ref, vo_ref = refs[3 + n_g:]
        g_ = g_refs[0][...]
        for r in g_refs[1:]:
            g_ = g_ + r[...]
        m_ = ADAM_B1 * m_ref[...] + (1.0 - ADAM_B1) * g_
        v_ = ADAM_B2 * v_ref[...] + (1.0 - ADAM_B2) * (g_ * g_)
        m_hat = m_ / (1.0 - ADAM_B1 ** ADAM_STEP)
        v_hat = v_ / (1.0 - ADAM_B2 ** ADAM_STEP)
        go_ref[...] = g_
        d_ref[...] = -ADAM_LR * (m_hat / (jnp.sqrt(v_hat) + ADAM_EPS) + ADAM_WD * w_ref[...])
        mo_ref[...] = m_
        vo_ref[...] = v_

    blk = pl.BlockSpec((T, n), lambda i: (i, 0))
    return _call(
        body, name=name, grid=(R // T,), parallel=(0,),
        in_specs=[blk] * (3 + n_g), out_specs=[blk] * 4,
        out_shape=[jax.ShapeDtypeStruct((R, n), F32)] * 4,
        args=(w, m, v, *g_parts))[0]


PACK_ALIGN = 8 * LANES


def _pack(pieces):
    flat = []
    for a in pieces:
        a = a.reshape(-1)
        pad = (-a.shape[0]) % PACK_ALIGN
        flat.append(jnp.pad(a, (0, pad)) if pad else a)
    return jnp.concatenate(flat).reshape(-1, LANES)


def _unpack(buf, shapes):
    flat = buf.reshape(-1)
    out, off = [], 0
    for shp in shapes:
        size = 1
        for s in shp:
            size *= s
        out.append(flat[off:off + size].reshape(shp))
        off += size + (-size) % PACK_ALIGN
    return out


def kernel(x, norm1_g, w_in, b_gate, pool_w, pool_scale, lb_logits, hgrn_norm_g, w_pa, w_pb, w_o, norm2_g, w_up, conv_w, conv_b, w_down, final_g, loss_target, m_norm1_g, m_w_in, m_b_gate, m_pool_w, m_pool_scale, m_lb_logits, m_hgrn_norm_g, m_w_pa, m_w_pb, m_w_o, m_norm2_g, m_w_up, m_conv_w, m_conv_b, m_w_down, m_final_g, v_norm1_g, v_w_in, v_b_gate, v_pool_w, v_pool_scale, v_lb_logits, v_hgrn_norm_g, v_w_pa, v_w_pb, v_w_o, v_norm2_g, v_w_up, v_conv_w, v_conv_b, v_w_down, v_final_g):
    env = dict(locals())
    w = {n: env[n] for n in WEIGHTS}
    m = {n: env["m_" + n] for n in WEIGHTS}
    v = {n: env["v_" + n] for n in WEIGHTS}
    my_chip = 2 * lax.axis_index("x") + lax.axis_index("y")
    L = w_in.shape[0]

    shards = {n: w[n].astype(BF16) for n in BIG}
    shards["conv_w"] = w["conv_w"]
    sm = {n: w[n] for n in SMALL}
    loss_cols, grad_x, g_small, big32, recv = _step(x[0], loss_target[0], sm, {}, shards)

    chip = my_chip.reshape(1).astype(jnp.int32)
    sums = [_sum_layers([big32[(n, l)] for l in range(L)], [recv[(n, l)] for l in range(L)], chip,
                        name="chip_sum_" + n) for n in BIG]
    small_names = list(SMALL)
    small_pieces = [g_small[n] for n in small_names] + [g_small["conv_w"], loss_cols]
    small_shapes = [a.shape for a in small_pieces]
    packed = _pack(small_pieces)
    Rs = packed.shape[0]
    swapped = _run_plan(_Together([_SiblingPlan(sums), _EveryonePlan(packed)]), name="tail_exchange")
    theirs, everyone = swapped[:-1], swapped[-1].reshape(8, Rs, LANES)
    g, delta, new_m, new_v = {}, {}, {}, {}
    for n, mine, other in zip(BIG, sums, theirs):
        shp = w[n].shape
        two_d = lambda a: a.reshape(-1, shp[-1])
        outs = _adamw(two_d(w[n]), two_d(m[n]), two_d(v[n]), [two_d(mine), two_d(other)], name="adamw_" + n)
        g[n], delta[n], new_m[n], new_v[n] = [a.reshape(shp) for a in outs]

    summed = _unpack(_sum_stack(everyone, name="small_sum"), small_shapes)
    loss = jnp.sum(summed[-1])
    cshard = w["conv_w"].shape[2]
    gs = dict(zip(small_names, summed[:len(small_names)]))
    g_cw = lax.dynamic_slice_in_dim(summed[-2], my_chip * cshard, cshard, axis=2)

    sm_out = _adamw(_pack([w[n] for n in small_names]), _pack([m[n] for n in small_names]),
                    _pack([v[n] for n in small_names]), [_pack([gs[n] for n in small_names])], name="adamw_small")
    shapes = [w[n].shape for n in small_names]
    for n, g_, d_, m_, v_ in zip(small_names, *[_unpack(a, shapes) for a in sm_out]):
        g[n], delta[n], new_m[n], new_v[n] = g_, d_, m_, v_
    shp = w["conv_w"].shape
    two_d = lambda a: a.reshape(-1, shp[-1])
    outs = _adamw(two_d(w["conv_w"]), two_d(m["conv_w"]), two_d(v["conv_w"]), [two_d(g_cw)], name="adamw_conv_w")
    g["conv_w"], delta["conv_w"], new_m["conv_w"], new_v["conv_w"] = [a.reshape(shp) for a in outs]

    return (loss, grad_x[None], *[g[n] for n in WEIGHTS], *[delta[n] for n in WEIGHTS],
            *[new_m[n] for n in WEIGHTS], *[new_v[n] for n in WEIGHTS])
```

```python
import jax
import jax.numpy as jnp
from jax import lax
from jax.experimental import pallas as pl
from jax.experimental.pallas import tpu as pltpu

F32 = jnp.float32
BF16 = jnp.bfloat16

EPS = 1e-6
CHUNK = 64
SUB = 32
LANES = 128
SUBLANES = 8
POOL_WINDOWS = (2, 4, 8, 16)
HALO_POOL = 16
EXP_CLAMP = 80.0

ADAM_LR = 0.001
ADAM_B1 = 0.9
ADAM_B2 = 0.999
ADAM_EPS = 1e-08
ADAM_WD = 0.01
ADAM_STEP = 10

VMEM_LIMIT = 56 * 1024 * 1024
MESH_ID = pl.DeviceIdType.MESH
N_CHIPS = 4
ANY = pl.BlockSpec(memory_space=pl.ANY)


def _dot(a, b):
    return jnp.dot(a, b, preferred_element_type=F32)


def _dot_nt(a, b):
    return lax.dot_general(a, b, (((1,), (1,)), ((), ())), preferred_element_type=F32)


def _dot_tn(a, b):
    return lax.dot_general(a, b, (((0,), (0,)), ((), ())), preferred_element_type=F32)


def _sigmoid(x):
    return jax.nn.sigmoid(x)


def _dsilu(x, s):
    return s * (1.0 + x * (1.0 - s))


def _row_tile(rows, want):
    t = min(rows, want)
    while rows % t:
        t //= 2
    return t


def _place():
    x, y, c = lax.axis_index("x"), lax.axis_index("y"), lax.axis_index("c")
    chips = [(1 - x, y), (x, 1 - y), (1 - x, 1 - y)]
    return x, y, c, chips


def _remote(src, dst, sems, k, to):
    return pltpu.make_async_remote_copy(src_ref=src, dst_ref=dst, send_sem=sems[0].at[k], recv_sem=sems[1].at[k],
                                        device_id=to, device_id_type=MESH_ID)


class _GatherPlan:
    def __init__(self, items):
        self.items = items
        self.inputs = [a for a, _, _ in items]
        self.out_shapes = []
        for a, kind, _ in items:
            shp = (N_CHIPS,) + a.shape[1:] if kind == "rows" else (a.shape[0], N_CHIPS) + a.shape[1:]
            self.out_shapes.append(jax.ShapeDtypeStruct(shp, a.dtype))
        n = len(items)
        self.scratch = [pltpu.SemaphoreType.DMA((6 * n,)), pltpu.SemaphoreType.DMA((6 * n,)),
                        pltpu.SemaphoreType.DMA((2 * n,))]

    def _views(self, i, src, dst):
        _, kind, l = self.items[i]
        if kind == "rows":
            half = src.shape[1] // 2
            part = lambda core: src.at[l, pl.ds(core * half, half), :]
            land = lambda chip, core: dst.at[chip, pl.ds(core * half, half), :]
        else:
            part = lambda core: src.at[core]
            land = lambda chip, core: dst.at[core, chip]
        return part, land

    def start(self, srcs, dsts, sems):
        x, y, c, chips = _place()
        me = 2 * x + y
        for i, (src, dst) in enumerate(zip(srcs, dsts)):
            part, land = self._views(i, src, dst)
            for core in range(2):
                pltpu.make_async_copy(part(core), land(me, core), sems[2].at[2 * i + core]).start()
            for j, (px, py) in enumerate(chips):
                _remote(part(c), land(me, c), sems, 6 * i + j, (px, py, c)).start()

    def finish(self, srcs, dsts, sems):
        x, y, c, chips = _place()
        me = 2 * x + y
        sibling = (x, y, 1 - c)
        for i, (src, dst) in enumerate(zip(srcs, dsts)):
            part, land = self._views(i, src, dst)
            for j, (px, py) in enumerate(chips):
                got = land(2 * px + py, c)
                _remote(got, got, sems, 6 * i + j, (px, py, c)).wait_recv()
                _remote(got, got, sems, 6 * i + 3 + j, sibling).start()
        for i, (src, dst) in enumerate(zip(srcs, dsts)):
            part, land = self._views(i, src, dst)
            for j, (px, py) in enumerate(chips):
                got = land(2 * px + py, 1 - c)
                _remote(got, got, sems, 6 * i + 3 + j, sibling).wait_recv()
            for j, (px, py) in enumerate(chips):
                _remote(part(c), land(me, c), sems, 6 * i + j, (px, py, c)).wait_send()
                mine = land(2 * px + py, c)
                _remote(mine, mine, sems, 6 * i + 3 + j, sibling).wait_send()
            for core in range(2):
                pltpu.make_async_copy(part(core), land(me, core), sems[2].at[2 * i + core]).wait()


class _ScatterPlan:
    def __init__(self, items):
        self.inputs = list(items)
        self.out_shapes = [jax.ShapeDtypeStruct((3,) + a.shape[1:], a.dtype) for a in items]
        n = len(items)
        self.scratch = [pltpu.SemaphoreType.DMA((3 * n,)), pltpu.SemaphoreType.DMA((3 * n,))]

    def _copies(self, srcs, dsts, sems):
        x, y, c, chips = _place()
        return [_remote(src.at[2 * px + py], dst.at[j], sems, 3 * i + j, (px, py, c))
                for i, (src, dst) in enumerate(zip(srcs, dsts)) for j, (px, py) in enumerate(chips)]

    def start(self, srcs, dsts, sems):
        for cp in self._copies(srcs, dsts, sems):
            cp.start()

    def finish(self, srcs, dsts, sems):
        copies = self._copies(srcs, dsts, sems)
        for cp in copies:
            cp.wait_recv()
        for cp in copies:
            cp.wait_send()


class _SiblingPlan:
    def __init__(self, items):
        self.inputs = list(items)
        self.out_shapes = [jax.ShapeDtypeStruct(a.shape, a.dtype) for a in items]
        n = len(items)
        self.scratch = [pltpu.SemaphoreType.DMA((n,)), pltpu.SemaphoreType.DMA((n,))]

    def _copies(self, srcs, dsts, sems):
        x, y, c, _ = _place()
        return [_remote(src, dst, sems, i, (x, y, 1 - c)) for i, (src, dst) in enumerate(zip(srcs, dsts))]

    def start(self, srcs, dsts, sems):
        for cp in self._copies(srcs, dsts, sems):
            cp.start()

    def finish(self, srcs, dsts, sems):
        copies = self._copies(srcs, dsts, sems)
        for cp in copies:
            cp.wait_recv()
        for cp in copies:
            cp.wait_send()


class _EveryonePlan:
    def __init__(self, block):
        self.inputs = [block]
        self.m = block.shape[0]
        self.out_shapes = [jax.ShapeDtypeStruct((8 * self.m,) + block.shape[1:], block.dtype)]
        self.scratch = [pltpu.SemaphoreType.DMA((7,)), pltpu.SemaphoreType.DMA((7,)), pltpu.SemaphoreType.DMA((1,))]

    def _rows(self, dst, px, py, pc):
        return dst.at[pl.ds((4 * px + 2 * py + pc) * self.m, self.m), :]

    def start(self, srcs, dsts, sems):
        x, y, c, chips = _place()
        src, dst = srcs[0], dsts[0]
        pltpu.make_async_copy(src, self._rows(dst, x, y, c), sems[2].at[0]).start()
        _remote(src, self._rows(dst, x, y, c), sems, 0, (x, y, 1 - c)).start()
        for j, (px, py) in enumerate(chips):
            _remote(src, self._rows(dst, x, y, c), sems, 1 + j, (px, py, c)).start()

    def finish(self, srcs, dsts, sems):
        x, y, c, chips = _place()
        src, dst = srcs[0], dsts[0]
        sibling = (x, y, 1 - c)
        for j, (px, py) in enumerate(chips):
            got = self._rows(dst, px, py, c)
            _remote(got, got, sems, 1 + j, (px, py, c)).wait_recv()
            _remote(got, got, sems, 4 + j, sibling).start()
        sib = self._rows(dst, x, y, 1 - c)
        _remote(sib, sib, sems, 0, sibling).wait_recv()
        for j, (px, py) in enumerate(chips):
            got = self._rows(dst, px, py, 1 - c)
            _remote(got, got, sems, 4 + j, sibling).wait_recv()
        mine = self._rows(dst, x, y, c)
        _remote(src, mine, sems, 0, sibling).wait_send()
        for j, (px, py) in enumerate(chips):
            _remote(src, mine, sems, 1 + j, (px, py, c)).wait_send()
            got = self._rows(dst, px, py, c)
            _remote(got, got, sems, 4 + j, sibling).wait_send()
        pltpu.make_async_copy(src, mine, sems[2].at[0]).wait()


def _call(body, *, name, grid, in_specs, out_specs, out_shape, args, scratch=(), parallel=(), plan=None):
    n_in, n_out, n_scr = len(in_specs), len(out_shape), len(scratch)
    sem = tuple("parallel" if (a in parallel and plan is None) else "arbitrary" for a in range(len(grid)))
    params = pltpu.CompilerParams(dimension_semantics=sem, vmem_limit_bytes=VMEM_LIMIT)
    if plan is None:
        outs = pl.pallas_call(body, name=name, grid=grid, in_specs=in_specs, out_specs=out_specs,
                              out_shape=out_shape, scratch_shapes=list(scratch), compiler_params=params)(*args)
        return list(outs), []
    p_in, p_out, p_scr = len(plan.inputs), len(plan.out_shapes), len(plan.scratch)

    def wrapped(*refs):
        ins, refs = refs[:n_in], refs[n_in:]
        p_ins, refs = refs[:p_in], refs[p_in:]
        outs, refs = refs[:n_out], refs[n_out:]
        p_outs, refs = refs[:p_out], refs[p_out:]
        scr, p_sems = refs[:n_scr], refs[n_scr:]
        ids = [pl.program_id(a) for a in range(len(grid))]
        first = _all([i == 0 for i in ids])
        last = _all([i == n - 1 for i, n in zip(ids, grid)])

        @pl.when(first)
        def _():
            plan.start(p_ins, p_outs, p_sems)

        body(*ins, *outs, *scr)

        @pl.when(last)
        def _():
            plan.finish(p_ins, p_outs, p_sems)

    outs = pl.pallas_call(
        wrapped, name=name, grid=grid,
        in_specs=list(in_specs) + [ANY] * p_in, out_specs=list(out_specs) + [ANY] * p_out,
        out_shape=list(out_shape) + list(plan.out_shapes),
        scratch_shapes=list(scratch) + list(plan.scratch), compiler_params=params,
    )(*args, *plan.inputs)
    return list(outs[:n_out]), list(outs[n_out:])


def _all(conds):
    out = conds[0]
    for c in conds[1:]:
        out = out & c
    return out


class _Together:
    def __init__(self, plans):
        self.plans = plans
        self.inputs = [a for p in plans for a in p.inputs]
        self.out_shapes = [s for p in plans for s in p.out_shapes]
        self.scratch = [s for p in plans for s in p.scratch]

    def _split(self, refs, count):
        out, at = [], 0
        for p in self.plans:
            out.append(refs[at:at + count(p)])
            at += count(p)
        return out

    def _parts(self, srcs, dsts, sems):
        return zip(self.plans, self._split(srcs, lambda p: len(p.inputs)),
                   self._split(dsts, lambda p: len(p.out_shapes)), self._split(sems, lambda p: len(p.scratch)))

    def start(self, srcs, dsts, sems):
        for p, s, d, m in self._parts(srcs, dsts, sems):
            p.start(s, d, m)

    def finish(self, srcs, dsts, sems):
        for p, s, d, m in self._parts(srcs, dsts, sems):
            p.finish(s, d, m)


def _run_plan(plan, *, name):
    p_in, p_out = len(plan.inputs), len(plan.out_shapes)

    def body(*refs):
        srcs, dsts, sems = refs[:p_in], refs[p_in:p_in + p_out], refs[p_in + p_out:]
        plan.start(srcs, dsts, sems)
        plan.finish(srcs, dsts, sems)

    return list(pl.pallas_call(body, name=name, in_specs=[ANY] * p_in, out_specs=[ANY] * p_out,
                               out_shape=list(plan.out_shapes), scratch_shapes=list(plan.scratch))(*plan.inputs))


def _rms(xf, g):
    r = lax.rsqrt(jnp.mean(xf * xf, axis=-1, keepdims=True) + EPS)
    return (xf * r * g).astype(BF16)


def _rmsnorm(x, g, *, name, plan=None):
    S, D = x.shape
    tm = _row_tile(S, 1024)

    def body(x_ref, g_ref, xn_ref):
        xn_ref[...] = _rms(x_ref[...], g_ref[...])

    return _call(
        body, name=name, grid=(S // tm,), parallel=(0,), plan=plan,
        in_specs=[pl.BlockSpec((tm, D), lambda i: (i, 0)), pl.BlockSpec((1, D), lambda i: (0, 0))],
        out_specs=[pl.BlockSpec((tm, D), lambda i: (i, 0))],
        out_shape=[jax.ShapeDtypeStruct((S, D), BF16)],
        args=(x, g))


def _matmul(xn, w4, *, name, out_dtype=F32, plan=None):
    S, D = xn.shape
    n4 = w4.shape[2]
    tm = _row_tile(S, 2048)

    def body(xn_ref, w_ref, o_ref):
        o_ref[...] = _dot(xn_ref[...], w_ref[...]).astype(out_dtype)

    return _call(
        body, name=name, grid=(S // tm, N_CHIPS), parallel=(0,), plan=plan,
        in_specs=[pl.BlockSpec((tm, D), lambda i, j: (i, 0)),
                  pl.BlockSpec((None, D, n4), lambda i, j: (j, 0, 0))],
        out_specs=[pl.BlockSpec((tm, n4), lambda i, j: (i, j))],
        out_shape=[jax.ShapeDtypeStruct((S, N_CHIPS * n4), out_dtype)],
        args=(xn, w4))


def _segments(widths, n4):
    per_chip = [[] for _ in range(N_CHIPS)]
    c0 = 0
    for p, w in enumerate(widths):
        a = c0
        while a < c0 + w:
            k = a // n4
            b = min(c0 + w, (k + 1) * n4)
            per_chip[k].append((p, (a - c0, b - c0), (a - k * n4, b - k * n4)))
            a = b
        c0 += w
    assert c0 == N_CHIPS * n4
    return per_chip


def _piece_specs(pieces, n4, tm):
    per_chip = _segments([p.shape[1] for p in pieces], n4)
    specs, local, start = [], [[] for _ in range(N_CHIPS)], 0
    for p, arr in enumerate(pieces):
        chips = [k for k in range(N_CHIPS) if any(seg[0] == p for seg in per_chip[k])]
        lo, hi = chips[0], chips[-1]
        tiled = arr.shape[1] % n4 == 0 and start % n4 == 0
        start += arr.shape[1]
        if tiled:
            imap = lambda k, i, lo=lo, hi=hi: (jnp.where((k >= lo) & (k <= hi), i, 0), jnp.clip(k - lo, 0, hi - lo))
            specs.append(pl.BlockSpec((tm, n4), imap))
        else:
            imap = lambda k, i, lo=lo, hi=hi: (jnp.where((k >= lo) & (k <= hi), i, 0), 0)
            specs.append(pl.BlockSpec((tm, arr.shape[1]), imap))
        for k in chips:
            for q, (pa, pb), cols in per_chip[k]:
                if q == p:
                    local[k].append((p, (0, n4) if tiled else (pa, pb), cols))
    return specs, local


def _dgrad_norm(dys, w4, x, g, dres, *, name, plan=None):
    S, D = x.shape
    n4 = w4.shape[2]
    tm = _row_tile(S, 512)
    per_chip = _segments([a.shape[1] for a in dys], n4)
    n_p = len(dys)

    def body(*refs):
        dy_refs = refs[:n_p]
        w_ref, x_ref, g_ref, dres_ref, dx_ref, dg_ref = refs[n_p:]

        @pl.when(pl.program_id(0) == 0)
        def _():
            dg_ref[...] = jnp.zeros_like(dg_ref)

        dxn = None
        for k in range(N_CHIPS):
            for p, (pa, pb), (ca, cb) in per_chip[k]:
                part = _dot_nt(dy_refs[p][:, pa:pb], w_ref[k, :, ca:cb])
                dxn = part if dxn is None else dxn + part
        xf = x_ref[...]
        r = lax.rsqrt(jnp.mean(xf * xf, axis=-1, keepdims=True) + EPS)
        xhat = xf * r
        dxhat = dxn * g_ref[...]
        dx_ref[...] = dres_ref[...] + r * (dxhat - xhat * jnp.mean(dxhat * xhat, axis=-1, keepdims=True))
        dg_ref[...] += jnp.sum(dxn * xhat, axis=0, keepdims=True)

    row = lambda w: pl.BlockSpec((tm, w), lambda i: (i, 0))
    return _call(
        body, name=name, grid=(S // tm,), plan=plan,
        in_specs=[row(a.shape[1]) for a in dys]
        + [pl.BlockSpec(w4.shape, lambda i: (0, 0, 0), pipeline_mode=pl.Buffered(1)),
           row(D), pl.BlockSpec((1, D), lambda i: (0, 0)), row(D)],
        out_specs=[row(D), pl.BlockSpec((1, D), lambda i: (0, 0))],
        out_shape=[jax.ShapeDtypeStruct((S, D), F32), jax.ShapeDtypeStruct((1, D), F32)],
        args=(*dys, w4, x, g, dres))


def _wgrad(a, dys, *, name, rows, plan=None):
    S, K = a.shape
    n4 = sum(p.shape[1] for p in dys) // N_CHIPS
    tm = _row_tile(S, rows)
    ns = S // tm
    specs, local = _piece_specs(dys, n4, tm)
    n_p = len(dys)

    def body(*refs):
        a_ref = refs[0]
        dy_refs = refs[1:1 + n_p]
        o_ref, o16_ref = refs[1 + n_p:]
        n, s = pl.program_id(0), pl.program_id(1)

        @pl.when(s == 0)
        def _():
            o_ref[...] = jnp.zeros_like(o_ref)

        for k in range(N_CHIPS):
            @pl.when(n == k)
            def _():
                av = a_ref[...]
                for p, (pa, pb), (ca, cb) in local[k]:
                    o_ref[:, ca:cb] += _dot_tn(av, dy_refs[p][:, pa:pb])

        @pl.when(s == ns - 1)
        def _():
            o16_ref[...] = o_ref[...].astype(BF16)

    out = pl.BlockSpec((None, K, n4), lambda n, s: (n, 0, 0))
    return _call(
        body, name=name, grid=(N_CHIPS, ns), parallel=(0,), plan=plan,
        in_specs=[pl.BlockSpec((tm, K), lambda n, s: (s, 0))] + specs,
        out_specs=[out, out],
        out_shape=[jax.ShapeDtypeStruct((N_CHIPS, K, n4), F32), jax.ShapeDtypeStruct((N_CHIPS, K, n4), BF16)],
        args=(a, *dys))


def _tiles(x):
    return x.reshape(x.shape[0] // SUBLANES, SUBLANES, x.shape[1])


def _shift_down(xp, s):
    n = xp.shape[0] - SUBLANES
    if s == SUBLANES:
        return xp[:n, :]
    t = _tiles(xp)
    rot = pltpu.roll(t, s, 1)
    sub = lax.broadcasted_iota(jnp.int32, t.shape, 1)[1:]
    return jnp.where(sub >= s, rot[1:], rot[:-1]).reshape(n, xp.shape[1])


def _shift_up(xn, s):
    n = xn.shape[0] - SUBLANES
    if s == SUBLANES:
        return xn[SUBLANES:, :]
    t = _tiles(xn)
    rot = pltpu.roll(t, SUBLANES - s, 1)
    sub = lax.broadcasted_iota(jnp.int32, t.shape, 1)[1:]
    return jnp.where(sub < SUBLANES - s, rot[:-1], rot[1:]).reshape(n, xn.shape[1])


def _pooled(u, halo, first_tile, row0):
    T = u.shape[0]
    halo = jnp.where(first_tile, 0.0, halo)
    pad = jnp.zeros((SUBLANES, u.shape[1]), F32)
    up = jnp.concatenate([pad, halo, u], axis=0)
    t1 = (row0 + lax.broadcasted_iota(jnp.int32, (T, 1), 0) + 1).astype(F32)
    outs = []
    for gi, w in enumerate(POOL_WINDOWS):
        s = up[:, gi * LANES:(gi + 1) * LANES]
        k = 1
        while k < w:
            if k < SUBLANES:
                s = jnp.concatenate([s[:SUBLANES, :], s[SUBLANES:, :] + _shift_down(s, k)], axis=0)
            else:
                s = s[SUBLANES:, :] + _shift_down(s, k)
            k *= 2
        s = s[-T:, :]
        inv = 1.0 / jnp.minimum(t1, float(w))
        outs.append(s * inv - u[:, gi * LANES:(gi + 1) * LANES])
    return outs


def _pool_mix(u, halo, first_tile, row0, pw_ref, ps_ref):
    pooled = _pooled(u, halo, first_tile, row0)
    outs = []
    for gi in range(len(POOL_WINDOWS)):
        mixed = _dot(pooled[gi].astype(BF16), pw_ref[gi].astype(BF16))
        outs.append((mixed * ps_ref[:, gi * LANES:(gi + 1) * LANES]).astype(BF16))
    return jnp.concatenate(outs, axis=1)


def _pool_grad(u, halo, first_tile, row0, d, dnext, pw_ref, ps_ref, du_ref, dpw_ref, dps_ref):
    T, P = d.shape
    pooled = _pooled(u, halo, first_tile, row0)
    pad = jnp.zeros((SUBLANES, P), F32)
    dext = jnp.concatenate([d, dnext, pad], axis=0)
    t1 = (row0 + lax.broadcasted_iota(jnp.int32, (T + HALO_POOL + SUBLANES, 1), 0) + 1).astype(F32)
    for gi, w in enumerate(POOL_WINDOWS):
        cols = slice(gi * LANES, (gi + 1) * LANES)
        pw = pw_ref[gi].astype(BF16)
        pg = pooled[gi].astype(BF16)
        mixed = _dot(pg, pw)
        dps_ref[:, cols] += jnp.sum(d[:, cols] * mixed, axis=0, keepdims=True)
        dmixed = (dext[:, cols] * ps_ref[:, cols]).astype(BF16)
        dpw_ref[gi] += _dot_tn(pg, dmixed[:T, :])
        dpooled = _dot_nt(dmixed, pw)
        e = dpooled * (1.0 / jnp.minimum(t1, float(w)))
        k = 1
        while k < w:
            if k < SUBLANES:
                e = jnp.concatenate([e[:-SUBLANES, :] + _shift_up(e, k), e[-SUBLANES:, :]], axis=0)
            else:
                e = e[:-SUBLANES, :] + _shift_up(e, k)
            k *= 2
        du_ref[:, cols] = (e[:T, :] - dpooled[:T, :]).astype(BF16)


def _cumsum_rows(x):
    n = x.shape[0]
    row = lax.broadcasted_iota(jnp.int32, x.shape, 0)
    s = 1
    while s < n:
        x = x + jnp.where(row >= s, pltpu.roll(x, s, 0), 0.0)
        s *= 2
    return x


def _rev_cumsum_rows(x):
    n = x.shape[0]
    row = lax.broadcasted_iota(jnp.int32, x.shape, 0)
    s = 1
    while s < n:
        x = x + jnp.where(row < n - s, pltpu.roll(x, n - s, 0), 0.0)
        s *= 2
    return x


def _chunk_prep(zq, zf, lb, b_ref):
    n_sub = CHUNK // SUB
    sq = _sigmoid(zq)
    q = zq * sq
    sf = _sigmoid(zf)
    f = lb + (1.0 - lb) * sf
    k = 1.0 - f
    b = _cumsum_rows(jnp.log(f))
    b_ref[...] = b
    shape = (SUB, b.shape[1])
    ends = [jnp.broadcast_to(b_ref[pl.ds(SUB * j + SUB - 1, 1), :], shape) for j in range(n_sub)]
    mids = [jnp.broadcast_to(b_ref[pl.ds(SUB * j + SUB // 2 - 1, 1), :], shape) for j in range(n_sub)]
    own = [b[SUB * j:SUB * (j + 1), :] for j in range(n_sub)]
    m0 = jnp.concatenate(mids, axis=0)
    e1 = jnp.concatenate(ends, axis=0)
    eq = [jnp.exp(jnp.minimum(b - m0, EXP_CLAMP))]
    for d in range(1, n_sub):
        rd = jnp.concatenate([own[j] if j < d else ends[j - d] for j in range(n_sub)], axis=0)
        eq.append(jnp.exp(b - rd))
    ek0 = jnp.exp(jnp.minimum(m0 - b, EXP_CLAMP))
    ek1 = jnp.exp(e1 - b)
    b_last = b_ref[pl.ds(CHUNK - 1, 1), :]
    return dict(q=q, k=k, f=f, sq=sq, sf=sf, b=b, eq=eq, ek0=ek0, ek1=ek1,
                eb=jnp.exp(b), ekl=jnp.exp(b_last - b), el=jnp.exp(b_last))


def _chunk_masks():
    ti = lax.broadcasted_iota(jnp.int32, (CHUNK, CHUNK), 0)
    si = lax.broadcasted_iota(jnp.int32, (CHUNK, CHUNK), 1)
    shift = SUB.bit_length() - 1
    dsub = jnp.right_shift(ti, shift) - jnp.right_shift(si, shift)
    masks = [(dsub == 0) & (si <= ti)]
    masks += [dsub == d for d in range(1, CHUNK // SUB)]
    return masks


def _chunk_attn(p, masks):
    qd = [(p["q"] * e).astype(BF16) for e in p["eq"]]
    k0 = (p["k"] * p["ek0"]).astype(BF16)
    k1 = (p["k"] * p["ek1"]).astype(BF16)
    a = jnp.where(masks[0], _dot_nt(qd[0], k0), 0.0)
    for d in range(1, len(masks)):
        a = jnp.where(masks[d], _dot_nt(qd[d], k1), a)
    return a, qd, k0, k1


def _hgrn_fwd(z, lb, norm_g, *, name, plan=None):
    S = z.shape[0]
    HW = lb.shape[1]
    NH = HW // LANES
    T = _row_tile(S, 512)
    nc = T // CHUNK

    def body(zq_ref, zf_ref, zi_ref, zo_ref, lb_ref, ng_ref, o_ref, of_ref, st_ref, s_scr, b_scr):
        @pl.when(pl.program_id(0) == 0)
        def _():
            s_scr[...] = jnp.zeros_like(s_scr)

        ng = ng_ref[...]
        masks = _chunk_masks()

        def chunk(c, carry):
            rows = pl.ds(pl.multiple_of(c * CHUNK, CHUNK), CHUNK)
            for h in range(NH):
                cols = slice(h * LANES, (h + 1) * LANES)
                p = _chunk_prep(zq_ref[rows, cols], zf_ref[rows, cols], lb_ref[:, cols], b_scr.at[h])
                v = zi_ref[rows, cols].astype(BF16)
                zo = zo_ref[rows, cols]
                st = s_scr[h]
                st_ref[c, h] = st
                a, _, _, _ = _chunk_attn(p, masks)
                o = _dot(a.astype(BF16), v) + _dot_nt((p["q"] * p["eb"]).astype(BF16), st.astype(BF16))
                s_scr[h] = st * p["el"] + _dot_tn(v, (p["k"] * p["ekl"]).astype(BF16))
                o_ref[rows, cols] = o
                r = lax.rsqrt(jnp.mean(o * o, axis=-1, keepdims=True) + EPS)
                of_ref[rows, cols] = (o * r * ng * (zo * _sigmoid(zo))).astype(BF16)
            return carry

        lax.fori_loop(0, nc, chunk, 0, unroll=8)

    part = lambda k: pl.BlockSpec((T, HW), lambda i, k=k: (i, k))
    return _call(
        body, name=name, grid=(S // T,), plan=plan,
        in_specs=[part(1), part(2), part(3), part(4),
                  pl.BlockSpec((1, HW), lambda i: (0, 0)), pl.BlockSpec((1, LANES), lambda i: (0, 0))],
        out_specs=[pl.BlockSpec((T, HW), lambda i: (i, 0)), pl.BlockSpec((T, HW), lambda i: (i, 0)),
                   pl.BlockSpec((nc, NH, LANES, LANES), lambda i: (i, 0, 0, 0))],
        out_shape=[jax.ShapeDtypeStruct((S, HW), F32), jax.ShapeDtypeStruct((S, HW), BF16),
                   jax.ShapeDtypeStruct((S // CHUNK, NH, LANES, LANES), F32)],
        scratch=[pltpu.VMEM((NH, LANES, LANES), F32), pltpu.VMEM((NH, CHUNK, LANES), F32)],
        args=(z, z, z, z, lb, norm_g))


def _hgrn_bwd(z, lb, norm_g, o_raw, states, dof, *, name, plan=None):
    S = z.shape[0]
    HW = lb.shape[1]
    NH = HW // LANES
    T = _row_tile(S, 512)
    nc = T // CHUNK
    nt = S // T

    def body(zq_ref, zf_ref, zi_ref, zo_ref, lb_ref, ng_ref, o_ref, st_ref, dof_ref,
             dzq_ref, dzf_ref, dzi_ref, dzo_ref, dlb_ref, dng_ref, ds_scr, b_scr):
        @pl.when(pl.program_id(0) == 0)
        def _():
            ds_scr[...] = jnp.zeros_like(ds_scr)
            dlb_ref[...] = jnp.zeros_like(dlb_ref)
            dng_ref[...] = jnp.zeros_like(dng_ref)

        ng = ng_ref[...]
        masks = _chunk_masks()
        last_row = lax.broadcasted_iota(jnp.int32, (CHUNK, 1), 0) == CHUNK - 1

        def chunk(cr, carry):
            c = nc - 1 - cr
            rows = pl.ds(pl.multiple_of(c * CHUNK, CHUNK), CHUNK)
            for h in range(NH):
                cols = slice(h * LANES, (h + 1) * LANES)
                lbv = lb_ref[:, cols]
                zq, zf, zo = zq_ref[rows, cols], zf_ref[rows, cols], zo_ref[rows, cols]
                o = o_ref[rows, cols]
                dof_c = dof_ref[rows, cols]
                st = st_ref[c, h]
                dst = ds_scr[h]

                so = _sigmoid(zo)
                r = lax.rsqrt(jnp.mean(o * o, axis=-1, keepdims=True) + EPS)
                ohat = o * r
                d_on = dof_c * (zo * so)
                dzo_ref[rows, cols] = (dof_c * ohat * ng * _dsilu(zo, so)).astype(BF16)
                dng_ref[:, cols] += jnp.sum(d_on * ohat, axis=0, keepdims=True)
                dohat = d_on * ng
                do = (r * (dohat - ohat * jnp.mean(dohat * ohat, axis=-1, keepdims=True))).astype(BF16)

                p = _chunk_prep(zq, zf, lbv, b_scr.at[h])
                q, k = p["q"], p["k"]
                v = zi_ref[rows, cols].astype(BF16)
                a, qd, k0, k1 = _chunk_attn(p, masks)
                ktl = (k * p["ekl"]).astype(BF16)
                dstb = dst.astype(BF16)

                da = _dot_nt(do, v)
                dzi_ref[rows, cols] = (_dot_tn(a.astype(BF16), do) + _dot_nt(ktl, dstb)).astype(BF16)

                da0 = jnp.where(masks[0], da, 0.0).astype(BF16)
                rq = _dot(da0, k0)
                rk0 = _dot_tn(da0, qd[0])
                dq = rq * p["eq"][0]
                db = qd[0].astype(F32) * rq - k0.astype(F32) * rk0
                rk1 = jnp.zeros_like(rk0)
                for d in range(1, len(masks)):
                    dad = jnp.where(masks[d], da, 0.0).astype(BF16)
                    rq = _dot(dad, k1)
                    dq = dq + rq * p["eq"][d]
                    db = db + qd[d].astype(F32) * rq
                    rk1 = rk1 + _dot_tn(dad, qd[d])
                dk = rk0 * p["ek0"] + rk1 * p["ek1"]
                db = db - k1.astype(F32) * rk1
                qe = (q * p["eb"]).astype(BF16)
                rq = _dot(do, st.astype(BF16))
                dq = dq + rq * p["eb"]
                db = db + qe.astype(F32) * rq
                rk = _dot(v, dstb)
                dk = dk + rk * p["ekl"]
                db = db - ktl.astype(F32) * rk

                st_new = st * p["el"] + _dot_tn(v, ktl)
                db = db + jnp.where(last_row, jnp.sum(dstb.astype(F32) * st_new, axis=0, keepdims=True), 0.0)
                dg = _rev_cumsum_rows(db)
                ds_scr[h] = dst * p["el"] + _dot_tn(do, qe)

                dzq_ref[rows, cols] = (dq * _dsilu(zq, p["sq"])).astype(BF16)
                df = dg / p["f"] - dk
                sf = p["sf"]
                dzf_ref[rows, cols] = (df * (1.0 - lbv) * sf * (1.0 - sf)).astype(BF16)
                dlb_ref[:, cols] += jnp.sum(df * (1.0 - sf), axis=0, keepdims=True)
            return carry

        lax.fori_loop(0, nc, chunk, 0, unroll=4)

    rev = lambda i: nt - 1 - i
    part = lambda k: pl.BlockSpec((T, HW), lambda i, k=k: (rev(i), k))
    blk = pl.BlockSpec((T, HW), lambda i: (rev(i), 0))
    vec = pl.BlockSpec((1, HW), lambda i: (0, 0))
    return _call(
        body, name=name, grid=(nt,), plan=plan,
        in_specs=[part(1), part(2), part(3), part(4), vec, pl.BlockSpec((1, LANES), lambda i: (0, 0)),
                  blk, pl.BlockSpec((nc, NH, LANES, LANES), lambda i: (rev(i), 0, 0, 0)), blk],
        out_specs=[blk, blk, blk, blk, vec, vec],
        out_shape=[jax.ShapeDtypeStruct((S, HW), BF16)] * 4 + [jax.ShapeDtypeStruct((1, HW), F32)] * 2,
        scratch=[pltpu.VMEM((NH, LANES, LANES), F32), pltpu.VMEM((NH, CHUNK, LANES), F32)],
        args=(z, z, z, z, lb, norm_g, o_raw, states, dof))


def _gate_specs(T, D, rows=lambda i: i):
    half = D // 2
    return [pl.BlockSpec((T, half), lambda i, k=k: (rows(i), 5 + k)) for k in range(4)]


def _gates(zg_refs, bg_ref, D):
    half = D // 2
    za = jnp.concatenate([zg_refs[0][...], zg_refs[1][...]], axis=1) + bg_ref[:, :D]
    zb = jnp.concatenate([zg_refs[2][...], zg_refs[3][...]], axis=1) + bg_ref[:, D:]
    return _sigmoid(za), _sigmoid(zb)


def _mix_fwd(x, of, z, pool_w, pool_scale, b_gate, w_pa4, w_pb4, w_o4, g_next, *, name, plan=None):
    S, D = x.shape
    P = of.shape[1]
    T = _row_tile(S, 512)
    hb = T // HALO_POOL

    def body(x_ref, u_ref, halo_ref, of_ref, g0, g1, g2, g3, pw_ref, ps_ref, bg_ref, wpa_ref, wpb_ref, wo_ref,
             gn_ref, xo_ref, ya_ref, yb_ref, xn_ref, pm_ref):
        i = pl.program_id(0)
        pmv = _pool_mix(u_ref[...], halo_ref[...], i == 0, i * T, pw_ref, ps_ref)
        pm_ref[...] = pmv
        ofv = of_ref[...]
        ya = jnp.concatenate([_dot(pmv, wpa_ref[k]) for k in range(N_CHIPS)], axis=1)
        yb = jnp.concatenate([_dot(ofv, wpb_ref[k]) for k in range(N_CHIPS)], axis=1)
        ga, gb = _gates((g0, g1, g2, g3), bg_ref, D)
        merged = (ga * ya + gb * yb).astype(BF16)
        x_mid = x_ref[...] + _dot(merged, wo_ref[...].reshape(D, D))
        xo_ref[...] = x_mid
        xn_ref[...] = _rms(x_mid, gn_ref[...])
        ya_ref[...] = ya.astype(BF16)
        yb_ref[...] = yb.astype(BF16)

    row = lambda w: pl.BlockSpec((T, w), lambda i: (i, 0))
    full = lambda a: pl.BlockSpec(a.shape, lambda i: (0,) * a.ndim)
    return _call(
        body, name=name, grid=(S // T,), parallel=(0,), plan=plan,
        in_specs=[row(D), row(P), pl.BlockSpec((HALO_POOL, P), lambda i: (jnp.maximum(i * hb - 1, 0), 0)), row(P)]
        + _gate_specs(T, D) + [full(pool_w), full(pool_scale), full(b_gate), full(w_pa4), full(w_pb4), full(w_o4),
                               full(g_next)],
        out_specs=[row(D), row(D), row(D), row(D), row(P)],
        out_shape=[jax.ShapeDtypeStruct((S, D), F32), jax.ShapeDtypeStruct((S, D), BF16),
                   jax.ShapeDtypeStruct((S, D), BF16), jax.ShapeDtypeStruct((S, D), BF16),
                   jax.ShapeDtypeStruct((S, P), BF16)],
        args=(x, z, z, of, z, z, z, z, pool_w, pool_scale, b_gate, w_pa4, w_pb4, w_o4, g_next))


def _mix_bwd(dxm, ya, yb, z, b_gate, pm, of, pool_w, pool_scale, w_pa4, w_pb4, w_o4, *, name):
    S, D = dxm.shape
    P = pm.shape[1]
    q4 = D // N_CHIPS
    T = _row_tile(S, 512)
    nt = S // T
    hb = T // HALO_POOL

    def body(dx_ref, ya_ref, yb_ref, g0, g1, g2, g3, bg_ref, pm_ref, of_ref, u_ref, halo_ref, pw_ref, ps_ref,
             wpa_ref, wpb_ref, wo_ref,
             dzg_ref, dof_ref, du_ref, dwo_ref, dwpa_ref, dwpb_ref, dbg_ref, dpw_ref, dps_ref,
             dwo16_ref, dwpa16_ref, dwpb16_ref, next_scr):
        i = pl.program_id(0)

        @pl.when(i == 0)
        def _():
            for ref in (dwo_ref, dwpa_ref, dwpb_ref, dbg_ref, dpw_ref, dps_ref, next_scr):
                ref[...] = jnp.zeros_like(ref)

        dxb = dx_ref[...].astype(BF16)
        ya = ya_ref[...].astype(F32)
        yb = yb_ref[...].astype(F32)
        ga, gb = _gates((g0, g1, g2, g3), bg_ref, D)
        merged = (ga * ya + gb * yb).astype(BF16)
        dwo_ref[...] += _dot_tn(merged, dxb).reshape(N_CHIPS, q4, D)
        dm = _dot_nt(dxb, wo_ref[...].reshape(D, D))
        dza = dm * ya * ga * (1.0 - ga)
        dzb = dm * yb * gb * (1.0 - gb)
        dzg_ref[:, :D] = dza.astype(BF16)
        dzg_ref[:, D:] = dzb.astype(BF16)
        dbg_ref[:, :D] += jnp.sum(dza, axis=0, keepdims=True)
        dbg_ref[:, D:] += jnp.sum(dzb, axis=0, keepdims=True)
        dya = (dm * ga).astype(BF16)
        dyb = (dm * gb).astype(BF16)
        pmv, ofv = pm_ref[...], of_ref[...]
        dpm = jnp.zeros((T, P), F32)
        dof = jnp.zeros((T, P), F32)
        for k in range(N_CHIPS):
            cols = slice(k * q4, (k + 1) * q4)
            dwpa_ref[k] += _dot_tn(pmv, dya[:, cols])
            dwpb_ref[k] += _dot_tn(ofv, dyb[:, cols])
            dpm = dpm + _dot_nt(dya[:, cols], wpa_ref[k])
            dof = dof + _dot_nt(dyb[:, cols], wpb_ref[k])
        dof_ref[...] = dof
        _pool_grad(u_ref[...], halo_ref[...], i == nt - 1, (nt - 1 - i) * T, dpm, next_scr[...], pw_ref, ps_ref,
                   du_ref, dpw_ref, dps_ref)
        next_scr[...] = dpm[:HALO_POOL, :]

        @pl.when(i == nt - 1)
        def _():
            dwo16_ref[...] = dwo_ref[...].astype(BF16)
            dwpa16_ref[...] = dwpa_ref[...].astype(BF16)
            dwpb16_ref[...] = dwpb_ref[...].astype(BF16)

    rev = lambda i: nt - 1 - i
    row = lambda w: pl.BlockSpec((T, w), lambda i: (rev(i), 0))
    full = lambda a: pl.BlockSpec(a.shape, lambda i: (0,) * a.ndim, pipeline_mode=pl.Buffered(1))
    like = lambda a, dt: jax.ShapeDtypeStruct(a.shape, dt)
    return _call(
        body, name=name, grid=(nt,),
        in_specs=[row(D), row(D), row(D)] + _gate_specs(T, D, rev)
        + [full(b_gate), row(P), row(P), row(P),
           pl.BlockSpec((HALO_POOL, P), lambda i: (jnp.maximum(rev(i) * hb - 1, 0), 0)),
           full(pool_w), full(pool_scale), full(w_pa4), full(w_pb4), full(w_o4)],
        out_specs=[row(2 * D), row(P), row(P), full(w_o4), full(w_pa4), full(w_pb4), full(b_gate),
                   full(pool_w), full(pool_scale), full(w_o4), full(w_pa4), full(w_pb4)],
        out_shape=[jax.ShapeDtypeStruct((S, 2 * D), BF16), jax.ShapeDtypeStruct((S, P), F32),
                   jax.ShapeDtypeStruct((S, P), BF16), like(w_o4, F32), like(w_pa4, F32), like(w_pb4, F32),
                   like(b_gate, F32), like(pool_w, F32), like(pool_scale, F32),
                   like(w_o4, BF16), like(w_pa4, BF16), like(w_pb4, BF16)],
        scratch=[pltpu.VMEM((HALO_POOL, P), F32)],
        args=(dxm, ya, yb, z, z, z, z, b_gate, pm, of, z, z, pool_w, pool_scale, w_pa4, w_pb4, w_o4))[0]


def _up_conv(xn, w_up4, conv_w, conv_b, *, name, plan=None):
    S, D = xn.shape
    f4 = w_up4.shape[2]
    nf = N_CHIPS // 2
    F = nf * f4
    T = _row_tile(S, 512)

    def body(xn_ref, wv_ref, wg_ref, cwv_ref, cwg_ref, cbv_ref, cbg_ref,
             hv_ref, hg_ref, val_ref, gate_ref, a_ref, pv_scr, pg_scr):
        i = pl.program_id(1)
        xv = xn_ref[...]

        def side(w_ref, cw_ref, cb_ref, h_ref, p_scr):
            h = _dot(xv, w_ref[...])
            h_ref[...] = h.astype(BF16)
            hp = jnp.concatenate([jnp.where(i == 0, 0.0, p_scr[...]), h], axis=0)
            p_scr[...] = h[-SUBLANES:, :]
            cw = cw_ref[...]
            return cw[0:1, :] * _shift_down(hp, 2) + cw[1:2, :] * _shift_down(hp, 1) + cw[2:3, :] * h + cb_ref[...]

        val = side(wv_ref, cwv_ref, cbv_ref, hv_ref, pv_scr)
        gate = side(wg_ref, cwg_ref, cbg_ref, hg_ref, pg_scr)
        val16, gate16 = val.astype(BF16), gate.astype(BF16)
        val_ref[...] = val16
        gate_ref[...] = gate16
        a_ref[...] = gate16 * _sigmoid(gate16) * val16

    out = pl.BlockSpec((T, f4), lambda f, i: (i, f))
    return _call(
        body, name=name, grid=(nf, S // T), plan=plan,
        in_specs=[pl.BlockSpec((T, D), lambda f, i: (i, 0)),
                  pl.BlockSpec((None, D, f4), lambda f, i: (f, 0, 0)),
                  pl.BlockSpec((None, D, f4), lambda f, i: (nf + f, 0, 0)),
                  pl.BlockSpec((3, f4), lambda f, i: (0, f)),
                  pl.BlockSpec((3, f4), lambda f, i: (0, nf + f)),
                  pl.BlockSpec((1, f4), lambda f, i: (0, f)),
                  pl.BlockSpec((1, f4), lambda f, i: (0, nf + f))],
        out_specs=[out] * 5,
        out_shape=[jax.ShapeDtypeStruct((S, F), BF16)] * 5,
        scratch=[pltpu.VMEM((SUBLANES, f4), F32), pltpu.VMEM((SUBLANES, f4), F32)],
        args=(xn, w_up4, w_up4, conv_w, conv_w, conv_b, conv_b))


def _down(a, w_down4, x, g_next, *, name, plan=None):
    S, F = a.shape
    D = x.shape[1]
    T = _row_tile(S, 1024)

    def body(a_ref, wd_ref, x_ref, gn_ref, o_ref, xn_ref):
        x_out = x_ref[...] + _dot(a_ref[...], wd_ref[...].reshape(F, D))
        o_ref[...] = x_out
        xn_ref[...] = _rms(x_out, gn_ref[...])

    row = lambda w: pl.BlockSpec((T, w), lambda i: (i, 0))
    return _call(
        body, name=name, grid=(S // T,), parallel=(0,), plan=plan,
        in_specs=[row(F), pl.BlockSpec(w_down4.shape, lambda i: (0, 0, 0), pipeline_mode=pl.Buffered(1)),
                  row(D), pl.BlockSpec((1, D), lambda i: (0, 0))],
        out_specs=[row(D), row(D)],
        out_shape=[jax.ShapeDtypeStruct((S, D), F32), jax.ShapeDtypeStruct((S, D), BF16)],
        args=(a, w_down4, x, g_next))


def _ffn_down_bwd(dxo, hv, hg, val16, gate16, a16, conv_w, w_down4, *, name, plan=None):
    S = hv.shape[0]
    _, f4, D = w_down4.shape
    F = N_CHIPS * f4
    T = _row_tile(S, 512)
    nf = 2
    tf = 2 * f4
    nt = S // T

    def body(dx_ref, hv_ref, hg_ref, val_ref, gate_ref, a_ref, cwv_ref, cwg_ref, wd_ref,
             dhv_ref, dhg_ref, dwd_ref, dwd16_ref, dcwv_ref, dcwg_ref, dcbv_ref, dcbg_ref, cv_scr, cg_scr):
        i = pl.program_id(1)

        @pl.when(i == 0)
        def _():
            cv_scr[...] = jnp.zeros_like(cv_scr)
            cg_scr[...] = jnp.zeros_like(cg_scr)
            dwd_ref[...] = jnp.zeros_like(dwd_ref)
            dcwv_ref[...] = jnp.zeros_like(dcwv_ref)
            dcwg_ref[...] = jnp.zeros_like(dcwg_ref)
            dcbv_ref[...] = jnp.zeros_like(dcbv_ref)
            dcbg_ref[...] = jnp.zeros_like(dcbg_ref)

        dxb = dx_ref[...].astype(BF16)
        val = val_ref[...].astype(F32)
        gate = gate_ref[...].astype(F32)
        sg = _sigmoid(gate)
        sil = gate * sg
        dwd_ref[...] += _dot_tn(a_ref[...], dxb).reshape(2, f4, D)
        da = _dot_nt(dxb, wd_ref[...].reshape(tf, D))

        def conv_bwd(dhc, h0, cw, c_scr, dh_ref, dcw_ref, dcb_ref):
            ext = jnp.concatenate([dhc, c_scr[...]], axis=0)
            n1 = _shift_up(ext, 1)
            n2 = _shift_up(ext, 2)
            dh_ref[...] = (cw[2:3, :] * dhc + cw[1:2, :] * n1 + cw[0:1, :] * n2).astype(BF16)
            c_scr[...] = dhc[:SUBLANES, :]
            dcw_ref[0:1, :] += jnp.sum(n2 * h0, axis=0, keepdims=True)
            dcw_ref[1:2, :] += jnp.sum(n1 * h0, axis=0, keepdims=True)
            dcw_ref[2:3, :] += jnp.sum(dhc * h0, axis=0, keepdims=True)
            dcb_ref[...] += jnp.sum(dhc, axis=0, keepdims=True)

        conv_bwd(da * sil, hv_ref[...].astype(F32), cwv_ref[...], cv_scr, dhv_ref, dcwv_ref, dcbv_ref)
        conv_bwd(da * val * _dsilu(gate, sg), hg_ref[...].astype(F32), cwg_ref[...], cg_scr, dhg_ref, dcwg_ref,
                 dcbg_ref)

        @pl.when(i == nt - 1)
        def _():
            dwd16_ref[...] = dwd_ref[...].astype(BF16)

    rev = lambda i: nt - 1 - i
    wd_spec = pl.BlockSpec((2, f4, D), lambda f, i: (f, 0, 0))
    return _call(
        body, name=name, grid=(nf, nt), plan=plan,
        in_specs=[pl.BlockSpec((T, D), lambda f, i: (rev(i), 0)),
                  pl.BlockSpec((T, tf), lambda f, i: (rev(i), f)),
                  pl.BlockSpec((T, tf), lambda f, i: (rev(i), f)),
                  pl.BlockSpec((T, tf), lambda f, i: (rev(i), f)),
                  pl.BlockSpec((T, tf), lambda f, i: (rev(i), f)),
                  pl.BlockSpec((T, tf), lambda f, i: (rev(i), f)),
                  pl.BlockSpec((3, tf), lambda f, i: (0, f)),
                  pl.BlockSpec((3, tf), lambda f, i: (0, nf + f)),
                  wd_spec],
        out_specs=[pl.BlockSpec((T, tf), lambda f, i: (rev(i), f)),
                   pl.BlockSpec((T, tf), lambda f, i: (rev(i), f)),
                   wd_spec, wd_spec,
                   pl.BlockSpec((3, tf), lambda f, i: (0, f)),
                   pl.BlockSpec((3, tf), lambda f, i: (0, f)),
                   pl.BlockSpec((1, tf), lambda f, i: (0, f)),
                   pl.BlockSpec((1, tf), lambda f, i: (0, f))],
        out_shape=[jax.ShapeDtypeStruct((S, F), BF16), jax.ShapeDtypeStruct((S, F), BF16),
                   jax.ShapeDtypeStruct((N_CHIPS, f4, D), F32), jax.ShapeDtypeStruct((N_CHIPS, f4, D), BF16),
                   jax.ShapeDtypeStruct((3, F), F32), jax.ShapeDtypeStruct((3, F), F32),
                   jax.ShapeDtypeStruct((1, F), F32), jax.ShapeDtypeStruct((1, F), F32)],
        scratch=[pltpu.VMEM((SUBLANES, tf), F32), pltpu.VMEM((SUBLANES, tf), F32)],
        args=(dxo, hv, hg, val16, gate16, a16, conv_w, conv_w, w_down4))


def _down_loss(a, w_down4, x, g, target, *, name):
    S, F = a.shape
    D = x.shape[1]
    T = _row_tile(S, 512)

    def body(a_ref, wd_ref, x_ref, g_ref, t_ref, loss_ref, dx_ref, dg_ref):
        @pl.when(pl.program_id(0) == 0)
        def _():
            loss_ref[...] = jnp.zeros_like(loss_ref)
            dg_ref[...] = jnp.zeros_like(dg_ref)

        xf = x_ref[...] + _dot(a_ref[...], wd_ref[...].reshape(F, D))
        r = lax.rsqrt(jnp.mean(xf * xf, axis=-1, keepdims=True) + EPS)
        xhat = xf * r
        err = xhat * g_ref[...] - t_ref[...]
        loss_ref[...] += jnp.sum(err * err, axis=0, keepdims=True) * (0.5 / D)
        dy = err * (1.0 / D)
        dxhat = dy * g_ref[...]
        dx_ref[...] = r * (dxhat - xhat * jnp.mean(dxhat * xhat, axis=-1, keepdims=True))
        dg_ref[...] += jnp.sum(dy * xhat, axis=0, keepdims=True)

    row = lambda w: pl.BlockSpec((T, w), lambda i: (i, 0))
    vec = pl.BlockSpec((1, D), lambda i: (0, 0))
    return _call(
        body, name=name, grid=(S // T,),
        in_specs=[row(F), pl.BlockSpec(w_down4.shape, lambda i: (0, 0, 0), pipeline_mode=pl.Buffered(1)),
                  row(D), vec, row(D)],
        out_specs=[vec, row(D), vec],
        out_shape=[jax.ShapeDtypeStruct((1, D), F32), jax.ShapeDtypeStruct((S, D), F32),
                   jax.ShapeDtypeStruct((1, D), F32)],
        args=(a, w_down4, x, g, target))[0]


BIG = ("w_in", "w_pa", "w_pb", "w_o", "w_up", "w_down")
SMALL = ("norm1_g", "b_gate", "pool_w", "pool_scale", "lb_logits", "hgrn_norm_g", "norm2_g", "conv_b", "final_g")
WEIGHTS = ("norm1_g", "w_in", "b_gate", "pool_w", "pool_scale", "lb_logits", "hgrn_norm_g", "w_pa", "w_pb", "w_o",
           "norm2_g", "w_up", "conv_w", "conv_b", "w_down", "final_g")


def _lower_bounds(lb_logits):
    soft = jax.nn.softmax(lb_logits.astype(F32), axis=0)
    cum = jnp.cumsum(soft, axis=0)
    return cum - cum[0:1]


def _step(x, target, sm, wts, shards=None):
    L = sm["norm1_g"].shape[0]
    wts = dict(wts)
    dist = shards is not None
    lbs, lb_vjp = jax.vjp(_lower_bounds, sm["lb_logits"])
    row = lambda a: a.reshape(1, -1)
    conv_w = sm.get("conv_w")

    def gather(names_layers, with_conv=False):
        items = [(shards[n], "rows", l) for n, l in names_layers]
        if with_conv:
            items.append((shards["conv_w"], "layer", None))
        return _GatherPlan(items)

    def landed(names_layers, outs):
        for key, arr in zip(names_layers, outs):
            wts[key] = arr

    own = {"in_proj": ("w_up",)}
    first = {"hgrn_fwd": ("w_pa", "w_pb", "w_o"), "mix_fwd": ("w_down",)}
    ahead = {"up": ("w_in", "w_pa", "w_pb", "w_o", "w_down")}
    conv_rider = "mix_fwd"

    def riders(l, kernel):
        if not dist:
            return [], None
        keys = [(n, l) for n in own.get(kernel, ())]
        keys += [(n, l) for n in first.get(kernel, ())] if l == 0 else []
        keys += [(n, l + 1) for n in ahead.get(kernel, ())] if l + 1 < L else []
        with_conv = l == 0 and kernel == conv_rider
        return keys, (gather(keys, with_conv) if keys or with_conv else None)

    keys = [("w_in", 0)] if dist else []
    (xn1,), got = _rmsnorm(x, row(sm["norm1_g"][0]), name="norm_in", plan=gather(keys) if keys else None)
    landed(keys, got)

    saved = []
    for l in range(L):
        keys, plan = riders(l, "in_proj")
        (z,), got = _matmul(xn1, wts[("w_in", l)], name=f"in_proj_{l}", plan=plan)
        landed(keys, got)
        keys, plan = riders(l, "hgrn_fwd")
        (o_raw, of, states), got = _hgrn_fwd(z, row(lbs[l]), row(sm["hgrn_norm_g"][l]), name=f"hgrn_fwd_{l}",
                                             plan=plan)
        landed(keys, got)
        keys, plan = riders(l, "mix_fwd")
        (x_mid, ya, yb, xn2, pm), got = _mix_fwd(
            x, of, z, sm["pool_w"][l], row(sm["pool_scale"][l]), row(sm["b_gate"][l]), wts[("w_pa", l)],
            wts[("w_pb", l)], wts[("w_o", l)], row(sm["norm2_g"][l]), name=f"mix_fwd_{l}", plan=plan)
        landed(keys, got)
        if dist and l == 0:
            full = got[-1]
            conv_w = jnp.concatenate([full[:, k] for k in range(N_CHIPS)], axis=2)
        keys, plan = riders(l, "up")
        (hv, hg, val16, gate16, a16), got = _up_conv(xn2, wts[("w_up", l)], conv_w[l], row(sm["conv_b"][l]),
                                                     name=f"up_{l}", plan=plan)
        landed(keys, got)
        saved.append(dict(x=x, xn1=xn1, z=z, pm=pm, o_raw=o_raw, of=of, states=states,
                          x_mid=x_mid, ya=ya, yb=yb, xn2=xn2, hv=hv, hg=hg, val16=val16, gate16=gate16, a16=a16))
        if l + 1 < L:
            keys, plan = riders(l, "down")
            (x, xn1), got = _down(a16, wts[("w_down", l)], x_mid, row(sm["norm1_g"][l + 1]), name=f"down_{l}",
                                  plan=plan)
            landed(keys, got)
        else:
            loss_cols, dx, d_final_g = _down_loss(a16, wts[("w_down", l)], x_mid, row(sm["final_g"]), target,
                                                  name="down_loss")

    small = {k: [None] * L for k in ("norm1_g", "b_gate", "pool_w", "pool_scale", "hgrn_norm_g", "norm2_g",
                                     "conv_w", "conv_b")}
    big32, big16, recv = {}, {}, {}
    dlbs = [None] * L
    pending = []

    def scatter():
        if not (dist and pending):
            return [], None
        keys = list(pending)
        del pending[:]
        return keys, _ScatterPlan([big16[k] for k in keys])

    def sent(keys, outs):
        for key, arr in zip(keys, outs):
            recv[key] = arr

    def made(name, l, g32, g16):
        big32[(name, l)], big16[(name, l)] = g32, g16
        pending.append((name, l))

    for l in reversed(range(L)):
        s = saved[l]
        keys, plan = scatter()
        (dhv, dhg, d_wd, d_wd16, dcwv, dcwg, dcbv, dcbg), got = _ffn_down_bwd(
            dx, s["hv"], s["hg"], s["val16"], s["gate16"], s["a16"], conv_w[l], wts[("w_down", l)], name=f"down_bwd_{l}",
            plan=plan)
        sent(keys, got)
        made("w_down", l, d_wd, d_wd16)
        small["conv_w"][l] = jnp.concatenate([dcwv, dcwg], axis=1)
        small["conv_b"][l] = jnp.concatenate([dcbv, dcbg], axis=1)[0]
        keys, plan = scatter()
        (d_wu, d_wu16), got = _wgrad(s["xn2"], [dhv, dhg], name=f"up_wgrad_{l}", rows=2048, plan=plan)
        sent(keys, got)
        made("w_up", l, d_wu, d_wu16)
        (dxm, dg2), _ = _dgrad_norm([dhv, dhg], wts[("w_up", l)], s["x_mid"], row(sm["norm2_g"][l]), dx,
                                    name=f"up_dgrad_{l}")
        small["norm2_g"][l] = dg2[0]

        dzg, dof, du, d_wo, d_wpa, d_wpb, dbg, dpw, dps, d_wo16, d_wpa16, d_wpb16 = _mix_bwd(
            dxm, s["ya"], s["yb"], s["z"], row(sm["b_gate"][l]), s["pm"], s["of"], sm["pool_w"][l],
            row(sm["pool_scale"][l]), wts[("w_pa", l)], wts[("w_pb", l)], wts[("w_o", l)], name=f"mix_bwd_{l}")
        small["pool_w"][l], small["pool_scale"][l] = dpw, dps[0]
        made("w_o", l, d_wo, d_wo16)
        made("w_pa", l, d_wpa, d_wpa16)
        made("w_pb", l, d_wpb, d_wpb16)
        small["b_gate"][l] = dbg[0]

        keys, plan = scatter()
        (dzq, dzf, dzi, dzo, dlb, dng), got = _hgrn_bwd(s["z"], row(lbs[l]), row(sm["hgrn_norm_g"][l]), s["o_raw"],
                                                      s["states"], dof, name=f"hgrn_bwd_{l}", plan=plan)
        sent(keys, got)
        dlbs[l] = dlb[0]
        small["hgrn_norm_g"][l] = jnp.sum(dng.reshape(-1, LANES), axis=0)

        dz = [du, dzq, dzf, dzi, dzo, dzg]
        (d_wi, d_wi16), _ = _wgrad(s["xn1"], dz, name=f"in_wgrad_{l}", rows=1024)
        made("w_in", l, d_wi, d_wi16)
        keys, plan = scatter()
        (dx, dg1), got = _dgrad_norm(dz, wts[("w_in", l)], s["x"], row(sm["norm1_g"][l]), dxm,
                                     name=f"in_dgrad_{l}", plan=plan)
        sent(keys, got)
        small["norm1_g"][l] = dg1[0]

    out = {k: jnp.stack(v) for k, v in small.items()}
    out["lb_logits"] = lb_vjp(jnp.stack(dlbs))[0]
    out["final_g"] = d_final_g[0]
    return loss_cols, dx, out, big32, recv


def _elementwise_rows(R, n, n_arrays):
    if 2 * n_arrays * R * n * 4 <= VMEM_LIMIT // 4 or R % 8:
        return R
    block = VMEM_LIMIT // 2 // (2 * n_arrays)
    want = 8
    while want * 2 * n * 4 <= block:
        want *= 2
    return _row_tile(R, want)


def _sum_layers(own, got, chip, *, name):
    L = len(own)
    _, r, n = own[0].shape
    T = _elementwise_rows(r, n, 6)
    nt = r // T

    def body(chip_ref, *refs):
        o_ref = refs[-1]
        l = pl.program_id(0)
        for k in range(L):
            @pl.when(l == k)
            def _():
                own_ref, got_ref = refs[2 * k], refs[2 * k + 1]
                acc = own_ref[...]
                for j in range(3):
                    acc = acc + got_ref[j].astype(F32)
                o_ref[...] = acc.astype(BF16)

    in_specs = []
    for k in range(L):
        hold = 0 if k else nt - 1
        in_specs.append(pl.BlockSpec((None, T, n), lambda l, i, c, k=k, hold=hold: (c[0], jnp.where(l == k, i, hold), 0)))
        in_specs.append(pl.BlockSpec((3, T, n), lambda l, i, c, k=k, hold=hold: (0, jnp.where(l == k, i, hold), 0)))
    grid_spec = pltpu.PrefetchScalarGridSpec(
        num_scalar_prefetch=1, grid=(L, nt), in_specs=in_specs,
        out_specs=pl.BlockSpec((None, T, n), lambda l, i, c: (l, i, 0)))
    args = [a for pair in zip(own, got) for a in pair]
    return pl.pallas_call(
        body, name=name, grid_spec=grid_spec, out_shape=jax.ShapeDtypeStruct((L, r, n), BF16),
        compiler_params=pltpu.CompilerParams(dimension_semantics=("arbitrary", "arbitrary"),
                                             vmem_limit_bytes=VMEM_LIMIT),
    )(chip, *args)


def _sum_stack(parts, *, name):
    K, R, n = parts.shape
    T = _elementwise_rows(R, n, K + 1)

    def body(p_ref, o_ref):
        acc = p_ref[0]
        for j in range(1, K):
            acc = acc + p_ref[j]
        o_ref[...] = acc

    return _call(
        body, name=name, grid=(R // T,), parallel=(0,),
        in_specs=[pl.BlockSpec((K, T, n), lambda i: (0, i, 0))],
        out_specs=[pl.BlockSpec((T, n), lambda i: (i, 0))],
        out_shape=[jax.ShapeDtypeStruct((R, n), F32)],
        args=(parts,))[0][0]


def _adamw(w, m, v, g_parts, *, name):
    R, n = w.shape
    n_g = len(g_parts)
    T = _elementwise_rows(R, n, 7 + n_g)

    def body(*refs):
        w_ref, m_ref, v_ref = refs[:3]
        g_refs = refs[3:3 + n_g]
        go_ref, d_ref, mo_ref, vo_ref = refs[3 + n_g:]
        g_ = g_refs[0][...].astype(F32)
        for r in g_refs[1:]:
            g_ = g_ + r[...].astype(F32)
        m_ = ADAM_B1 * m_ref[...] + (1.0 - ADAM_B1) * g_
        v_ = ADAM_B2 * v_ref[...] + (1.0 - ADAM_B2) * (g_ * g_)
        m_hat = m_ / (1.0 - ADAM_B1 ** ADAM_STEP)
        v_hat = v_ / (1.0 - ADAM_B2 ** ADAM_STEP)
        go_ref[...] = g_
        d_ref[...] = -ADAM_LR * (m_hat / (jnp.sqrt(v_hat) + ADAM_EPS) + ADAM_WD * w_ref[...])
        mo_ref[...] = m_
        vo_ref[...] = v_

    blk = pl.BlockSpec((T, n), lambda i: (i, 0))
    return _call(
        body, name=name, grid=(R // T,), parallel=(0,),
        in_specs=[blk] * (3 + n_g), out_specs=[blk] * 4,
        out_shape=[jax.ShapeDtypeStruct((R, n), F32)] * 4,
        args=(w, m, v, *g_parts))[0]


PACK_ALIGN = 8 * LANES


def _pack(pieces):
    flat = []
    for a in pieces:
        a = a.reshape(-1)
        pad = (-a.shape[0]) % PACK_ALIGN
        flat.append(jnp.pad(a, (0, pad)) if pad else a)
    return jnp.concatenate(flat).reshape(-1, LANES)


def _unpack(buf, shapes):
    flat = buf.reshape(-1)
    out, off = [], 0
    for shp in shapes:
        size = 1
        for s in shp:
            size *= s
        out.append(flat[off:off + size].reshape(shp))
        off += size + (-size) % PACK_ALIGN
    return out


def kernel(x, norm1_g, w_in, b_gate, pool_w, pool_scale, lb_logits, hgrn_norm_g, w_pa, w_pb, w_o, norm2_g, w_up, conv_w, conv_b, w_down, final_g, loss_target, m_norm1_g, m_w_in, m_b_gate, m_pool_w, m_pool_scale, m_lb_logits, m_hgrn_norm_g, m_w_pa, m_w_pb, m_w_o, m_norm2_g, m_w_up, m_conv_w, m_conv_b, m_w_down, m_final_g, v_norm1_g, v_w_in, v_b_gate, v_pool_w, v_pool_scale, v_lb_logits, v_hgrn_norm_g, v_w_pa, v_w_pb, v_w_o, v_norm2_g, v_w_up, v_conv_w, v_conv_b, v_w_down, v_final_g):
    env = dict(locals())
    w = {n: env[n] for n in WEIGHTS}
    m = {n: env["m_" + n] for n in WEIGHTS}
    v = {n: env["v_" + n] for n in WEIGHTS}
    my_chip = 2 * lax.axis_index("x") + lax.axis_index("y")
    L = w_in.shape[0]

    shards = {n: w[n].astype(BF16) for n in BIG}
    shards["conv_w"] = w["conv_w"]
    sm = {n: w[n] for n in SMALL}
    loss_cols, grad_x, g_small, big32, recv = _step(x[0], loss_target[0], sm, {}, shards)

    chip = my_chip.reshape(1).astype(jnp.int32)
    sums = [_sum_layers([big32[(n, l)] for l in range(L)], [recv[(n, l)] for l in range(L)], chip,
                        name="chip_sum_" + n) for n in BIG]
    small_names = list(SMALL)
    small_pieces = [g_small[n] for n in small_names] + [g_small["conv_w"], loss_cols]
    small_shapes = [a.shape for a in small_pieces]
    packed = _pack(small_pieces)
    Rs = packed.shape[0]
    swapped = _run_plan(_Together([_SiblingPlan(sums), _EveryonePlan(packed)]), name="tail_exchange")
    theirs, everyone = swapped[:-1], swapped[-1].reshape(8, Rs, LANES)
    g, delta, new_m, new_v = {}, {}, {}, {}
    for n, mine, other in zip(BIG, sums, theirs):
        shp = w[n].shape
        two_d = lambda a: a.reshape(-1, shp[-1])
        outs = _adamw(two_d(w[n]), two_d(m[n]), two_d(v[n]), [two_d(mine), two_d(other)], name="adamw_" + n)
        g[n], delta[n], new_m[n], new_v[n] = [a.reshape(shp) for a in outs]

    summed = _unpack(_sum_stack(everyone, name="small_sum"), small_shapes)
    loss = jnp.sum(summed[-1])
    cshard = w["conv_w"].shape[2]
    gs = dict(zip(small_names, summed[:len(small_names)]))
    g_cw = lax.dynamic_slice_in_dim(summed[-2], my_chip * cshard, cshard, axis=2)

    sm_out = _adamw(_pack([w[n] for n in small_names]), _pack([m[n] for n in small_names]),
                    _pack([v[n] for n in small_names]), [_pack([gs[n] for n in small_names])], name="adamw_small")
    shapes = [w[n].shape for n in small_names]
    for n, g_, d_, m_, v_ in zip(small_names, *[_unpack(a, shapes) for a in sm_out]):
        g[n], delta[n], new_m[n], new_v[n] = g_, d_, m_, v_
    shp = w["conv_w"].shape
    two_d = lambda a: a.reshape(-1, shp[-1])
    outs = _adamw(two_d(w["conv_w"]), two_d(m["conv_w"]), two_d(v["conv_w"]), [two_d(g_cw)], name="adamw_conv_w")
    g["conv_w"], delta["conv_w"], new_m["conv_w"], new_v["conv_w"] = [a.reshape(shp) for a in outs]

    return (loss, grad_x[None], *[g[n] for n in WEIGHTS], *[delta[n] for n in WEIGHTS],
            *[new_m[n] for n in WEIGHTS], *[new_v[n] for n in WEIGHTS])
```

```python
import jax
import jax.numpy as jnp
from jax import lax
from jax.experimental import pallas as pl
from jax.experimental.pallas import tpu as pltpu

F32 = jnp.float32
BF16 = jnp.bfloat16

EPS = 1e-6
CHUNK = 64
SUB = 32
LANES = 128
SUBLANES = 8
POOL_WINDOWS = (2, 4, 8, 16)
HALO_POOL = 16
EXP_CLAMP = 80.0

ADAM_LR = 0.001
ADAM_B1 = 0.9
ADAM_B2 = 0.999
ADAM_EPS = 1e-08
ADAM_WD = 0.01
ADAM_STEP = 10

VMEM_LIMIT = 56 * 1024 * 1024
MESH_ID = pl.DeviceIdType.MESH
N_CHIPS = 4
ANY = pl.BlockSpec(memory_space=pl.ANY)


def _dot(a, b):
    return jnp.dot(a, b, preferred_element_type=F32)


def _dot_nt(a, b):
    return lax.dot_general(a, b, (((1,), (1,)), ((), ())), preferred_element_type=F32)


def _dot_tn(a, b):
    return lax.dot_general(a, b, (((0,), (0,)), ((), ())), preferred_element_type=F32)


def _sigmoid(x):
    return jax.nn.sigmoid(x)


def _dsilu(x, s):
    return s * (1.0 + x * (1.0 - s))


def _row_tile(rows, want):
    t = min(rows, want)
    while rows % t:
        t //= 2
    return t


def _place():
    x, y, c = lax.axis_index("x"), lax.axis_index("y"), lax.axis_index("c")
    chips = [(1 - x, y), (x, 1 - y), (1 - x, 1 - y)]
    return x, y, c, chips


def _remote(src, dst, sems, k, to):
    return pltpu.make_async_remote_copy(src_ref=src, dst_ref=dst, send_sem=sems[0].at[k], recv_sem=sems[1].at[k],
                                        device_id=to, device_id_type=MESH_ID)


class _GatherPlan:
    def __init__(self, items):
        self.items = items
        self.inputs = [a for a, _, _ in items]
        self.out_shapes = []
        for a, kind, _ in items:
            shp = (N_CHIPS,) + a.shape[1:] if kind == "rows" else (a.shape[0], N_CHIPS) + a.shape[1:]
            self.out_shapes.append(jax.ShapeDtypeStruct(shp, a.dtype))
        n = len(items)
        self.scratch = [pltpu.SemaphoreType.DMA((6 * n,)), pltpu.SemaphoreType.DMA((6 * n,)),
                        pltpu.SemaphoreType.DMA((2 * n,))]

    def _views(self, i, src, dst):
        _, kind, l = self.items[i]
        if kind == "rows":
            half = src.shape[1] // 2
            part = lambda core: src.at[l, pl.ds(core * half, half), :]
            land = lambda chip, core: dst.at[chip, pl.ds(core * half, half), :]
        else:
            part = lambda core: src.at[core]
            land = lambda chip, core: dst.at[core, chip]
        return part, land

    def start(self, srcs, dsts, sems):
        x, y, c, chips = _place()
        me = 2 * x + y
        for i, (src, dst) in enumerate(zip(srcs, dsts)):
            part, land = self._views(i, src, dst)
            for core in range(2):
                pltpu.make_async_copy(part(core), land(me, core), sems[2].at[2 * i + core]).start()
            for j, (px, py) in enumerate(chips):
                _remote(part(c), land(me, c), sems, 6 * i + j, (px, py, c)).start()

    def finish(self, srcs, dsts, sems):
        x, y, c, chips = _place()
        me = 2 * x + y
        sibling = (x, y, 1 - c)
        for i, (src, dst) in enumerate(zip(srcs, dsts)):
            part, land = self._views(i, src, dst)
            for j, (px, py) in enumerate(chips):
                got = land(2 * px + py, c)
                _remote(got, got, sems, 6 * i + j, (px, py, c)).wait_recv()
                _remote(got, got, sems, 6 * i + 3 + j, sibling).start()
        for i, (src, dst) in enumerate(zip(srcs, dsts)):
            part, land = self._views(i, src, dst)
            for j, (px, py) in enumerate(chips):
                got = land(2 * px + py, 1 - c)
                _remote(got, got, sems, 6 * i + 3 + j, sibling).wait_recv()
            for j, (px, py) in enumerate(chips):
                _remote(part(c), land(me, c), sems, 6 * i + j, (px, py, c)).wait_send()
                mine = land(2 * px + py, c)
                _remote(mine, mine, sems, 6 * i + 3 + j, sibling).wait_send()
            for core in range(2):
                pltpu.make_async_copy(part(core), land(me, core), sems[2].at[2 * i + core]).wait()


class _ScatterPlan:
    def __init__(self, items):
        self.inputs = list(items)
        self.out_shapes = [jax.ShapeDtypeStruct((3,) + a.shape[1:], a.dtype) for a in items]
        n = len(items)
        self.scratch = [pltpu.SemaphoreType.DMA((3 * n,)), pltpu.SemaphoreType.DMA((3 * n,))]

    def _copies(self, srcs, dsts, sems):
        x, y, c, chips = _place()
        return [_remote(src.at[2 * px + py], dst.at[j], sems, 3 * i + j, (px, py, c))
                for i, (src, dst) in enumerate(zip(srcs, dsts)) for j, (px, py) in enumerate(chips)]

    def start(self, srcs, dsts, sems):
        for cp in self._copies(srcs, dsts, sems):
            cp.start()

    def finish(self, srcs, dsts, sems):
        copies = self._copies(srcs, dsts, sems)
        for cp in copies:
            cp.wait_recv()
        for cp in copies:
            cp.wait_send()


class _SiblingPlan:
    def __init__(self, items):
        self.inputs = list(items)
        self.out_shapes = [jax.ShapeDtypeStruct(a.shape, a.dtype) for a in items]
        n = len(items)
        self.scratch = [pltpu.SemaphoreType.DMA((n,)), pltpu.SemaphoreType.DMA((n,))]

    def _copies(self, srcs, dsts, sems):
        x, y, c, _ = _place()
        return [_remote(src, dst, sems, i, (x, y, 1 - c)) for i, (src, dst) in enumerate(zip(srcs, dsts))]

    def start(self, srcs, dsts, sems):
        for cp in self._copies(srcs, dsts, sems):
            cp.start()

    def finish(self, srcs, dsts, sems):
        copies = self._copies(srcs, dsts, sems)
        for cp in copies:
            cp.wait_recv()
        for cp in copies:
            cp.wait_send()


class _EveryonePlan:
    def __init__(self, block):
        self.inputs = [block]
        self.m = block.shape[0]
        self.out_shapes = [jax.ShapeDtypeStruct((8 * self.m,) + block.shape[1:], block.dtype)]
        self.scratch = [pltpu.SemaphoreType.DMA((7,)), pltpu.SemaphoreType.DMA((7,)), pltpu.SemaphoreType.DMA((1,))]

    def _rows(self, dst, px, py, pc):
        return dst.at[pl.ds((4 * px + 2 * py + pc) * self.m, self.m), :]

    def start(self, srcs, dsts, sems):
        x, y, c, chips = _place()
        src, dst = srcs[0], dsts[0]
        pltpu.make_async_copy(src, self._rows(dst, x, y, c), sems[2].at[0]).start()
        _remote(src, self._rows(dst, x, y, c), sems, 0, (x, y, 1 - c)).start()
        for j, (px, py) in enumerate(chips):
            _remote(src, self._rows(dst, x, y, c), sems, 1 + j, (px, py, c)).start()

    def finish(self, srcs, dsts, sems):
        x, y, c, chips = _place()
        src, dst = srcs[0], dsts[0]
        sibling = (x, y, 1 - c)
        for j, (px, py) in enumerate(chips):
            got = self._rows(dst, px, py, c)
            _remote(got, got, sems, 1 + j, (px, py, c)).wait_recv()
            _remote(got, got, sems, 4 + j, sibling).start()
        sib = self._rows(dst, x, y, 1 - c)
        _remote(sib, sib, sems, 0, sibling).wait_recv()
        for j, (px, py) in enumerate(chips):
            got = self._rows(dst, px, py, 1 - c)
            _remote(got, got, sems, 4 + j, sibling).wait_recv()
        mine = self._rows(dst, x, y, c)
        _remote(src, mine, sems, 0, sibling).wait_send()
        for j, (px, py) in enumerate(chips):
            _remote(src, mine, sems, 1 + j, (px, py, c)).wait_send()
            got = self._rows(dst, px, py, c)
            _remote(got, got, sems, 4 + j, sibling).wait_send()
        pltpu.make_async_copy(src, mine, sems[2].at[0]).wait()


class _NoPlan:
    inputs, out_shapes, scratch = [], [], []

    def start(self, *_):
        pass

    finish = start


def _call(body, *, name, grid, in_specs, out_specs, out_shape, args, scratch=(), parallel=(), plan=None,
          chip=None, aliases=None):
    n_in, n_out, n_scr = len(in_specs), len(out_shape), len(scratch)
    sem = tuple("parallel" if (a in parallel and plan is None) else "arbitrary" for a in range(len(grid)))
    params = pltpu.CompilerParams(dimension_semantics=sem, vmem_limit_bytes=VMEM_LIMIT)
    if plan is None and chip is None:
        outs = pl.pallas_call(body, name=name, grid=grid, in_specs=in_specs, out_specs=out_specs,
                              out_shape=out_shape, scratch_shapes=list(scratch), compiler_params=params)(*args)
        return list(outs), []
    plan = plan or _NoPlan()
    p_in, p_out, p_scr = len(plan.inputs), len(plan.out_shapes), len(plan.scratch)
    n_pre = 0 if chip is None else 1

    def wrapped(*refs):
        pre, refs = refs[:n_pre], refs[n_pre:]
        ins, refs = refs[:n_in], refs[n_in:]
        p_ins, refs = refs[:p_in], refs[p_in:]
        outs, refs = refs[:n_out], refs[n_out:]
        p_outs, refs = refs[:p_out], refs[p_out:]
        scr, p_sems = refs[:n_scr], refs[n_scr:]
        ids = [pl.program_id(a) for a in range(len(grid))]
        first = _all([i == 0 for i in ids])
        last = _all([i == n - 1 for i, n in zip(ids, grid)])

        @pl.when(first)
        def _():
            plan.start(p_ins, p_outs, p_sems)

        body(*pre, *ins, *outs, *scr)

        @pl.when(last)
        def _():
            plan.finish(p_ins, p_outs, p_sems)

    specs = dict(in_specs=list(in_specs) + [ANY] * p_in, out_specs=list(out_specs) + [ANY] * p_out,
                 scratch_shapes=list(scratch) + list(plan.scratch))
    if chip is None:
        how, pre_args = dict(grid=grid, **specs), ()
    else:
        how = dict(grid_spec=pltpu.PrefetchScalarGridSpec(num_scalar_prefetch=1, grid=grid, **specs))
        pre_args = (chip,)
    outs = pl.pallas_call(
        wrapped, name=name, out_shape=list(out_shape) + list(plan.out_shapes), compiler_params=params,
        input_output_aliases=aliases or {}, **how,
    )(*pre_args, *args, *plan.inputs)
    return list(outs[:n_out]), list(outs[n_out:])


def _all(conds):
    out = conds[0]
    for c in conds[1:]:
        out = out & c
    return out


class _Together:
    def __init__(self, plans):
        self.plans = plans
        self.inputs = [a for p in plans for a in p.inputs]
        self.out_shapes = [s for p in plans for s in p.out_shapes]
        self.scratch = [s for p in plans for s in p.scratch]

    def _split(self, refs, count):
        out, at = [], 0
        for p in self.plans:
            out.append(refs[at:at + count(p)])
            at += count(p)
        return out

    def _parts(self, srcs, dsts, sems):
        return zip(self.plans, self._split(srcs, lambda p: len(p.inputs)),
                   self._split(dsts, lambda p: len(p.out_shapes)), self._split(sems, lambda p: len(p.scratch)))

    def start(self, srcs, dsts, sems):
        for p, s, d, m in self._parts(srcs, dsts, sems):
            p.start(s, d, m)

    def finish(self, srcs, dsts, sems):
        for p, s, d, m in self._parts(srcs, dsts, sems):
            p.finish(s, d, m)


def _run_plan(plan, *, name):
    p_in, p_out = len(plan.inputs), len(plan.out_shapes)

    def body(*refs):
        srcs, dsts, sems = refs[:p_in], refs[p_in:p_in + p_out], refs[p_in + p_out:]
        plan.start(srcs, dsts, sems)
        plan.finish(srcs, dsts, sems)

    return list(pl.pallas_call(body, name=name, in_specs=[ANY] * p_in, out_specs=[ANY] * p_out,
                               out_shape=list(plan.out_shapes), scratch_shapes=list(plan.scratch))(*plan.inputs))


def _rms(xf, g):
    r = lax.rsqrt(jnp.mean(xf * xf, axis=-1, keepdims=True) + EPS)
    return (xf * r * g).astype(BF16)


def _rmsnorm(x, g, *, name, plan=None):
    S, D = x.shape
    tm = _row_tile(S, 1024)

    def body(x_ref, g_ref, xn_ref):
        xn_ref[...] = _rms(x_ref[...], g_ref[...])

    return _call(
        body, name=name, grid=(S // tm,), parallel=(0,), plan=plan,
        in_specs=[pl.BlockSpec((tm, D), lambda i: (i, 0)), pl.BlockSpec((1, D), lambda i: (0, 0))],
        out_specs=[pl.BlockSpec((tm, D), lambda i: (i, 0))],
        out_shape=[jax.ShapeDtypeStruct((S, D), BF16)],
        args=(x, g))


def _matmul(xn, w4, *, name, out_dtype=F32, plan=None):
    S, D = xn.shape
    n4 = w4.shape[2]
    tm = _row_tile(S, 2048)

    def body(xn_ref, w_ref, o_ref):
        o_ref[...] = _dot(xn_ref[...], w_ref[...]).astype(out_dtype)

    return _call(
        body, name=name, grid=(S // tm, N_CHIPS), parallel=(0,), plan=plan,
        in_specs=[pl.BlockSpec((tm, D), lambda i, j: (i, 0)),
                  pl.BlockSpec((None, D, n4), lambda i, j: (j, 0, 0))],
        out_specs=[pl.BlockSpec((tm, n4), lambda i, j: (i, j))],
        out_shape=[jax.ShapeDtypeStruct((S, N_CHIPS * n4), out_dtype)],
        args=(xn, w4))


def _norm_own_proj(x, g, w_own, chip, *, name, plan):
    S, D = x.shape
    n4 = w_own.shape[1]
    tm = _row_tile(S, 1024)

    def body(chip_ref, x_ref, g_ref, w_ref, xn_ref, z_ref):
        xn = _rms(x_ref[...], g_ref[...])
        xn_ref[...] = xn
        z_ref[...] = _dot(xn, w_ref[...])

    return _call(
        body, name=name, grid=(S // tm,), plan=plan, chip=chip,
        in_specs=[pl.BlockSpec((tm, D), lambda i, c: (i, 0)), pl.BlockSpec((1, D), lambda i, c: (0, 0)),
                  pl.BlockSpec((D, n4), lambda i, c: (0, 0))],
        out_specs=[pl.BlockSpec((tm, D), lambda i, c: (i, 0)), pl.BlockSpec((tm, n4), lambda i, c: (i, c[0]))],
        out_shape=[jax.ShapeDtypeStruct((S, D), BF16), jax.ShapeDtypeStruct((S, N_CHIPS * n4), F32)],
        args=(x, g, w_own))


def _matmul_rest(xn, w4, z, chip, *, name, plan=None):
    S, D = xn.shape
    n4 = w4.shape[2]
    tm = _row_tile(S, 2048)
    other = lambda j, c: (c[0] + 1 + j) % N_CHIPS

    def body(chip_ref, xn_ref, w_ref, z_ref, o_ref):
        o_ref[...] = _dot(xn_ref[...], w_ref[...])

    return _call(
        body, name=name, grid=(S // tm, N_CHIPS - 1), plan=plan, chip=chip, aliases={3: 0},
        in_specs=[pl.BlockSpec((tm, D), lambda i, j, c: (i, 0)),
                  pl.BlockSpec((None, D, n4), lambda i, j, c: (other(j, c), 0, 0)), ANY],
        out_specs=[pl.BlockSpec((tm, n4), lambda i, j, c: (i, other(j, c)))],
        out_shape=[jax.ShapeDtypeStruct(z.shape, z.dtype)],
        args=(xn, w4, z))


def _segments(widths, n4):
    per_chip = [[] for _ in range(N_CHIPS)]
    c0 = 0
    for p, w in enumerate(widths):
        a = c0
        while a < c0 + w:
            k = a // n4
            b = min(c0 + w, (k + 1) * n4)
            per_chip[k].append((p, (a - c0, b - c0), (a - k * n4, b - k * n4)))
            a = b
        c0 += w
    assert c0 == N_CHIPS * n4
    return per_chip


def _piece_specs(pieces, n4, tm):
    per_chip = _segments([p.shape[1] for p in pieces], n4)
    specs, local, start = [], [[] for _ in range(N_CHIPS)], 0
    for p, arr in enumerate(pieces):
        chips = [k for k in range(N_CHIPS) if any(seg[0] == p for seg in per_chip[k])]
        lo, hi = chips[0], chips[-1]
        tiled = arr.shape[1] % n4 == 0 and start % n4 == 0
        start += arr.shape[1]
        if tiled:
            imap = lambda k, i, lo=lo, hi=hi: (jnp.where((k >= lo) & (k <= hi), i, 0), jnp.clip(k - lo, 0, hi - lo))
            specs.append(pl.BlockSpec((tm, n4), imap))
        else:
            imap = lambda k, i, lo=lo, hi=hi: (jnp.where((k >= lo) & (k <= hi), i, 0), 0)
            specs.append(pl.BlockSpec((tm, arr.shape[1]), imap))
        for k in chips:
            for q, (pa, pb), cols in per_chip[k]:
                if q == p:
                    local[k].append((p, (0, n4) if tiled else (pa, pb), cols))
    return specs, local


def _dgrad_norm(dys, w4, x, g, dres, *, name, plan=None):
    S, D = x.shape
    n4 = w4.shape[2]
    tm = _row_tile(S, 512)
    per_chip = _segments([a.shape[1] for a in dys], n4)
    n_p = len(dys)

    def body(*refs):
        dy_refs = refs[:n_p]
        w_ref, x_ref, g_ref, dres_ref, dx_ref, dg_ref = refs[n_p:]

        @pl.when(pl.program_id(0) == 0)
        def _():
            dg_ref[...] = jnp.zeros_like(dg_ref)

        dxn = None
        for k in range(N_CHIPS):
            for p, (pa, pb), (ca, cb) in per_chip[k]:
                part = _dot_nt(dy_refs[p][:, pa:pb], w_ref[k, :, ca:cb])
                dxn = part if dxn is None else dxn + part
        xf = x_ref[...]
        r = lax.rsqrt(jnp.mean(xf * xf, axis=-1, keepdims=True) + EPS)
        xhat = xf * r
        dxhat = dxn * g_ref[...]
        dx_ref[...] = dres_ref[...] + r * (dxhat - xhat * jnp.mean(dxhat * xhat, axis=-1, keepdims=True))
        dg_ref[...] += jnp.sum(dxn * xhat, axis=0, keepdims=True)

    row = lambda w: pl.BlockSpec((tm, w), lambda i: (i, 0))
    return _call(
        body, name=name, grid=(S // tm,), plan=plan,
        in_specs=[row(a.shape[1]) for a in dys]
        + [pl.BlockSpec(w4.shape, lambda i: (0, 0, 0), pipeline_mode=pl.Buffered(1)),
           row(D), pl.BlockSpec((1, D), lambda i: (0, 0)), row(D)],
        out_specs=[row(D), pl.BlockSpec((1, D), lambda i: (0, 0))],
        out_shape=[jax.ShapeDtypeStruct((S, D), F32), jax.ShapeDtypeStruct((1, D), F32)],
        args=(*dys, w4, x, g, dres))


def _wgrad(a, dys, *, name, rows, plan=None):
    S, K = a.shape
    n4 = sum(p.shape[1] for p in dys) // N_CHIPS
    tm = _row_tile(S, rows)
    ns = S // tm
    specs, local = _piece_specs(dys, n4, tm)
    n_p = len(dys)

    def body(*refs):
        a_ref = refs[0]
        dy_refs = refs[1:1 + n_p]
        o_ref, o16_ref = refs[1 + n_p:]
        n, s = pl.program_id(0), pl.program_id(1)

        @pl.when(s == 0)
        def _():
            o_ref[...] = jnp.zeros_like(o_ref)

        for k in range(N_CHIPS):
            @pl.when(n == k)
            def _():
                av = a_ref[...]
                for p, (pa, pb), (ca, cb) in local[k]:
                    o_ref[:, ca:cb] += _dot_tn(av, dy_refs[p][:, pa:pb])

        @pl.when(s == ns - 1)
        def _():
            o16_ref[...] = o_ref[...].astype(BF16)

    out = pl.BlockSpec((None, K, n4), lambda n, s: (n, 0, 0))
    return _call(
        body, name=name, grid=(N_CHIPS, ns), parallel=(0,), plan=plan,
        in_specs=[pl.BlockSpec((tm, K), lambda n, s: (s, 0))] + specs,
        out_specs=[out, out],
        out_shape=[jax.ShapeDtypeStruct((N_CHIPS, K, n4), F32), jax.ShapeDtypeStruct((N_CHIPS, K, n4), BF16)],
        args=(a, *dys))


def _tiles(x):
    return x.reshape(x.shape[0] // SUBLANES, SUBLANES, x.shape[1])


def _shift_down(xp, s):
    n = xp.shape[0] - SUBLANES
    if s == SUBLANES:
        return xp[:n, :]
    t = _tiles(xp)
    rot = pltpu.roll(t, s, 1)
    sub = lax.broadcasted_iota(jnp.int32, t.shape, 1)[1:]
    return jnp.where(sub >= s, rot[1:], rot[:-1]).reshape(n, xp.shape[1])


def _shift_up(xn, s):
    n = xn.shape[0] - SUBLANES
    if s == SUBLANES:
        return xn[SUBLANES:, :]
    t = _tiles(xn)
    rot = pltpu.roll(t, SUBLANES - s, 1)
    sub = lax.broadcasted_iota(jnp.int32, t.shape, 1)[1:]
    return jnp.where(sub < SUBLANES - s, rot[:-1], rot[1:]).reshape(n, xn.shape[1])


def _pooled(u, halo, first_tile, row0):
    T = u.shape[0]
    halo = jnp.where(first_tile, 0.0, halo)
    pad = jnp.zeros((SUBLANES, u.shape[1]), F32)
    up = jnp.concatenate([pad, halo, u], axis=0)
    t1 = (row0 + lax.broadcasted_iota(jnp.int32, (T, 1), 0) + 1).astype(F32)
    outs = []
    for gi, w in enumerate(POOL_WINDOWS):
        s = up[:, gi * LANES:(gi + 1) * LANES]
        k = 1
        while k < w:
            if k < SUBLANES:
                s = jnp.concatenate([s[:SUBLANES, :], s[SUBLANES:, :] + _shift_down(s, k)], axis=0)
            else:
                s = s[SUBLANES:, :] + _shift_down(s, k)
            k *= 2
        s = s[-T:, :]
        inv = 1.0 / jnp.minimum(t1, float(w))
        outs.append(s * inv - u[:, gi * LANES:(gi + 1) * LANES])
    return outs


def _pool_mix(u, halo, first_tile, row0, pw_ref, ps_ref):
    pooled = _pooled(u, halo, first_tile, row0)
    outs = []
    for gi in range(len(POOL_WINDOWS)):
        mixed = _dot(pooled[gi].astype(BF16), pw_ref[gi].astype(BF16))
        outs.append((mixed * ps_ref[:, gi * LANES:(gi + 1) * LANES]).astype(BF16))
    return jnp.concatenate(outs, axis=1)


def _pool_grad(u, halo, first_tile, row0, d, dnext, pw_ref, ps_ref, du_ref, dpw_ref, dps_ref):
    T, P = d.shape
    pooled = _pooled(u, halo, first_tile, row0)
    pad = jnp.zeros((SUBLANES, P), F32)
    dext = jnp.concatenate([d, dnext, pad], axis=0)
    t1 = (row0 + lax.broadcasted_iota(jnp.int32, (T + HALO_POOL + SUBLANES, 1), 0) + 1).astype(F32)
    for gi, w in enumerate(POOL_WINDOWS):
        cols = slice(gi * LANES, (gi + 1) * LANES)
        pw = pw_ref[gi].astype(BF16)
        pg = pooled[gi].astype(BF16)
        mixed = _dot(pg, pw)
        dps_ref[:, cols] += jnp.sum(d[:, cols] * mixed, axis=0, keepdims=True)
        dmixed = (dext[:, cols] * ps_ref[:, cols]).astype(BF16)
        dpw_ref[gi] += _dot_tn(pg, dmixed[:T, :])
        dpooled = _dot_nt(dmixed, pw)
        e = dpooled * (1.0 / jnp.minimum(t1, float(w)))
        k = 1
        while k < w:
            if k < SUBLANES:
                e = jnp.concatenate([e[:-SUBLANES, :] + _shift_up(e, k), e[-SUBLANES:, :]], axis=0)
            else:
                e = e[:-SUBLANES, :] + _shift_up(e, k)
            k *= 2
        du_ref[:, cols] = (e[:T, :] - dpooled[:T, :]).astype(BF16)


def _cumsum_rows(x):
    n = x.shape[0]
    row = lax.broadcasted_iota(jnp.int32, x.shape, 0)
    s = 1
    while s < n:
        x = x + jnp.where(row >= s, pltpu.roll(x, s, 0), 0.0)
        s *= 2
    return x


def _rev_cumsum_rows(x):
    n = x.shape[0]
    row = lax.broadcasted_iota(jnp.int32, x.shape, 0)
    s = 1
    while s < n:
        x = x + jnp.where(row < n - s, pltpu.roll(x, n - s, 0), 0.0)
        s *= 2
    return x


def _chunk_prep(zq, zf, lb, b_ref):
    n_sub = CHUNK // SUB
    sq = _sigmoid(zq)
    q = zq * sq
    sf = _sigmoid(zf)
    f = lb + (1.0 - lb) * sf
    k = 1.0 - f
    b = _cumsum_rows(jnp.log(f))
    b_ref[...] = b
    shape = (SUB, b.shape[1])
    ends = [jnp.broadcast_to(b_ref[pl.ds(SUB * j + SUB - 1, 1), :], shape) for j in range(n_sub)]
    mids = [jnp.broadcast_to(b_ref[pl.ds(SUB * j + SUB // 2 - 1, 1), :], shape) for j in range(n_sub)]
    own = [b[SUB * j:SUB * (j + 1), :] for j in range(n_sub)]
    m0 = jnp.concatenate(mids, axis=0)
    e1 = jnp.concatenate(ends, axis=0)
    eq = [jnp.exp(jnp.minimum(b - m0, EXP_CLAMP))]
    for d in range(1, n_sub):
        rd = jnp.concatenate([own[j] if j < d else ends[j - d] for j in range(n_sub)], axis=0)
        eq.append(jnp.exp(b - rd))
    ek0 = jnp.exp(jnp.minimum(m0 - b, EXP_CLAMP))
    ek1 = jnp.exp(e1 - b)
    b_last = b_ref[pl.ds(CHUNK - 1, 1), :]
    return dict(q=q, k=k, f=f, sq=sq, sf=sf, b=b, eq=eq, ek0=ek0, ek1=ek1,
                eb=jnp.exp(b), ekl=jnp.exp(b_last - b), el=jnp.exp(b_last))


def _chunk_masks():
    ti = lax.broadcasted_iota(jnp.int32, (CHUNK, CHUNK), 0)
    si = lax.broadcasted_iota(jnp.int32, (CHUNK, CHUNK), 1)
    shift = SUB.bit_length() - 1
    dsub = jnp.right_shift(ti, shift) - jnp.right_shift(si, shift)
    masks = [(dsub == 0) & (si <= ti)]
    masks += [dsub == d for d in range(1, CHUNK // SUB)]
    return masks


def _chunk_attn(p, masks):
    qd = [(p["q"] * e).astype(BF16) for e in p["eq"]]
    k0 = (p["k"] * p["ek0"]).astype(BF16)
    k1 = (p["k"] * p["ek1"]).astype(BF16)
    a = jnp.where(masks[0], _dot_nt(qd[0], k0), 0.0)
    for d in range(1, len(masks)):
        a = jnp.where(masks[d], _dot_nt(qd[d], k1), a)
    return a, qd, k0, k1


def _hgrn_fwd(z, lb, norm_g, *, name, plan=None):
    S = z.shape[0]
    HW = lb.shape[1]
    NH = HW // LANES
    T = _row_tile(S, 512)
    nc = T // CHUNK

    def body(zq_ref, zf_ref, zi_ref, zo_ref, lb_ref, ng_ref, o_ref, of_ref, st_ref, s_scr, b_scr):
        @pl.when(pl.program_id(0) == 0)
        def _():
            s_scr[...] = jnp.zeros_like(s_scr)

        ng = ng_ref[...]
        masks = _chunk_masks()

        def chunk(c, carry):
            rows = pl.ds(pl.multiple_of(c * CHUNK, CHUNK), CHUNK)
            for h in range(NH):
                cols = slice(h * LANES, (h + 1) * LANES)
                p = _chunk_prep(zq_ref[rows, cols], zf_ref[rows, cols], lb_ref[:, cols], b_scr.at[h])
                v = zi_ref[rows, cols].astype(BF16)
                zo = zo_ref[rows, cols]
                st = s_scr[h]
                st_ref[c, h] = st
                a, _, _, _ = _chunk_attn(p, masks)
                o = _dot(a.astype(BF16), v) + _dot_nt((p["q"] * p["eb"]).astype(BF16), st.astype(BF16))
                s_scr[h] = st * p["el"] + _dot_tn(v, (p["k"] * p["ekl"]).astype(BF16))
                o_ref[rows, cols] = o
                r = lax.rsqrt(jnp.mean(o * o, axis=-1, keepdims=True) + EPS)
                of_ref[rows, cols] = (o * r * ng * (zo * _sigmoid(zo))).astype(BF16)
            return carry

        lax.fori_loop(0, nc, chunk, 0, unroll=8)

    part = lambda k: pl.BlockSpec((T, HW), lambda i, k=k: (i, k))
    return _call(
        body, name=name, grid=(S // T,), plan=plan,
        in_specs=[part(1), part(2), part(3), part(4),
                  pl.BlockSpec((1, HW), lambda i: (0, 0)), pl.BlockSpec((1, LANES), lambda i: (0, 0))],
        out_specs=[pl.BlockSpec((T, HW), lambda i: (i, 0)), pl.BlockSpec((T, HW), lambda i: (i, 0)),
                   pl.BlockSpec((nc, NH, LANES, LANES), lambda i: (i, 0, 0, 0))],
        out_shape=[jax.ShapeDtypeStruct((S, HW), F32), jax.ShapeDtypeStruct((S, HW), BF16),
                   jax.ShapeDtypeStruct((S // CHUNK, NH, LANES, LANES), F32)],
        scratch=[pltpu.VMEM((NH, LANES, LANES), F32), pltpu.VMEM((NH, CHUNK, LANES), F32)],
        args=(z, z, z, z, lb, norm_g))


def _hgrn_bwd(z, lb, norm_g, o_raw, states, dof, *, name, plan=None):
    S = z.shape[0]
    HW = lb.shape[1]
    NH = HW // LANES
    T = _row_tile(S, 512)
    nc = T // CHUNK
    nt = S // T

    def body(zq_ref, zf_ref, zi_ref, zo_ref, lb_ref, ng_ref, o_ref, st_ref, dof_ref,
             dzq_ref, dzf_ref, dzi_ref, dzo_ref, dlb_ref, dng_ref, ds_scr, b_scr):
        @pl.when(pl.program_id(0) == 0)
        def _():
            ds_scr[...] = jnp.zeros_like(ds_scr)
            dlb_ref[...] = jnp.zeros_like(dlb_ref)
            dng_ref[...] = jnp.zeros_like(dng_ref)

        ng = ng_ref[...]
        masks = _chunk_masks()
        last_row = lax.broadcasted_iota(jnp.int32, (CHUNK, 1), 0) == CHUNK - 1

        def chunk(cr, carry):
            c = nc - 1 - cr
            rows = pl.ds(pl.multiple_of(c * CHUNK, CHUNK), CHUNK)
            for h in range(NH):
                cols = slice(h * LANES, (h + 1) * LANES)
                lbv = lb_ref[:, cols]
                zq, zf, zo = zq_ref[rows, cols], zf_ref[rows, cols], zo_ref[rows, cols]
                o = o_ref[rows, cols]
                dof_c = dof_ref[rows, cols]
                st = st_ref[c, h]
                dst = ds_scr[h]

                so = _sigmoid(zo)
                r = lax.rsqrt(jnp.mean(o * o, axis=-1, keepdims=True) + EPS)
                ohat = o * r
                d_on = dof_c * (zo * so)
                dzo_ref[rows, cols] = (dof_c * ohat * ng * _dsilu(zo, so)).astype(BF16)
                dng_ref[:, cols] += jnp.sum(d_on * ohat, axis=0, keepdims=True)
                dohat = d_on * ng
                do = (r * (dohat - ohat * jnp.mean(dohat * ohat, axis=-1, keepdims=True))).astype(BF16)

                p = _chunk_prep(zq, zf, lbv, b_scr.at[h])
                q, k = p["q"], p["k"]
                v = zi_ref[rows, cols].astype(BF16)
                a, qd, k0, k1 = _chunk_attn(p, masks)
                ktl = (k * p["ekl"]).astype(BF16)
                dstb = dst.astype(BF16)

                da = _dot_nt(do, v)
                dzi_ref[rows, cols] = (_dot_tn(a.astype(BF16), do) + _dot_nt(ktl, dstb)).astype(BF16)

                da0 = jnp.where(masks[0], da, 0.0).astype(BF16)
                rq = _dot(da0, k0)
                rk0 = _dot_tn(da0, qd[0])
                dq = rq * p["eq"][0]
                db = qd[0].astype(F32) * rq - k0.astype(F32) * rk0
                rk1 = jnp.zeros_like(rk0)
                for d in range(1, len(masks)):
                    dad = jnp.where(masks[d], da, 0.0).astype(BF16)
                    rq = _dot(dad, k1)
                    dq = dq + rq * p["eq"][d]
                    db = db + qd[d].astype(F32) * rq
                    rk1 = rk1 + _dot_tn(dad, qd[d])
                dk = rk0 * p["ek0"] + rk1 * p["ek1"]
                db = db - k1.astype(F32) * rk1
                qe = (q * p["eb"]).astype(BF16)
                rq = _dot(do, st.astype(BF16))
                dq = dq + rq * p["eb"]
                db = db + qe.astype(F32) * rq
                rk = _dot(v, dstb)
                dk = dk + rk * p["ekl"]
                db = db - ktl.astype(F32) * rk

                st_new = st * p["el"] + _dot_tn(v, ktl)
                db = db + jnp.where(last_row, jnp.sum(dstb.astype(F32) * st_new, axis=0, keepdims=True), 0.0)
                dg = _rev_cumsum_rows(db)
                ds_scr[h] = dst * p["el"] + _dot_tn(do, qe)

                dzq_ref[rows, cols] = (dq * _dsilu(zq, p["sq"])).astype(BF16)
                df = dg / p["f"] - dk
                sf = p["sf"]
                dzf_ref[rows, cols] = (df * (1.0 - lbv) * sf * (1.0 - sf)).astype(BF16)
                dlb_ref[:, cols] += jnp.sum(df * (1.0 - sf), axis=0, keepdims=True)
            return carry

        lax.fori_loop(0, nc, chunk, 0, unroll=4)

    rev = lambda i: nt - 1 - i
    part = lambda k: pl.BlockSpec((T, HW), lambda i, k=k: (rev(i), k))
    blk = pl.BlockSpec((T, HW), lambda i: (rev(i), 0))
    vec = pl.BlockSpec((1, HW), lambda i: (0, 0))
    return _call(
        body, name=name, grid=(nt,), plan=plan,
        in_specs=[part(1), part(2), part(3), part(4), vec, pl.BlockSpec((1, LANES), lambda i: (0, 0)),
                  blk, pl.BlockSpec((nc, NH, LANES, LANES), lambda i: (rev(i), 0, 0, 0)), blk],
        out_specs=[blk, blk, blk, blk, vec, vec],
        out_shape=[jax.ShapeDtypeStruct((S, HW), BF16)] * 4 + [jax.ShapeDtypeStruct((1, HW), F32)] * 2,
        scratch=[pltpu.VMEM((NH, LANES, LANES), F32), pltpu.VMEM((NH, CHUNK, LANES), F32)],
        args=(z, z, z, z, lb, norm_g, o_raw, states, dof))


def _gate_specs(T, D, rows=lambda i: i):
    half = D // 2
    return [pl.BlockSpec((T, half), lambda i, k=k: (rows(i), 5 + k)) for k in range(4)]


def _gates(zg_refs, bg_ref, D):
    half = D // 2
    za = jnp.concatenate([zg_refs[0][...], zg_refs[1][...]], axis=1) + bg_ref[:, :D]
    zb = jnp.concatenate([zg_refs[2][...], zg_refs[3][...]], axis=1) + bg_ref[:, D:]
    return _sigmoid(za), _sigmoid(zb)


def _mix_fwd(x, of, z, pool_w, pool_scale, b_gate, w_pa4, w_pb4, w_o4, g_next, *, name, plan=None):
    S, D = x.shape
    P = of.shape[1]
    T = _row_tile(S, 512)
    hb = T // HALO_POOL

    def body(x_ref, u_ref, halo_ref, of_ref, g0, g1, g2, g3, pw_ref, ps_ref, bg_ref, wpa_ref, wpb_ref, wo_ref,
             gn_ref, xo_ref, ya_ref, yb_ref, xn_ref, pm_ref):
        i = pl.program_id(0)
        pmv = _pool_mix(u_ref[...], halo_ref[...], i == 0, i * T, pw_ref, ps_ref)
        pm_ref[...] = pmv
        ofv = of_ref[...]
        ya = jnp.concatenate([_dot(pmv, wpa_ref[k]) for k in range(N_CHIPS)], axis=1)
        yb = jnp.concatenate([_dot(ofv, wpb_ref[k]) for k in range(N_CHIPS)], axis=1)
        ga, gb = _gates((g0, g1, g2, g3), bg_ref, D)
        merged = (ga * ya + gb * yb).astype(BF16)
        x_mid = x_ref[...] + _dot(merged, wo_ref[...].reshape(D, D))
        xo_ref[...] = x_mid
        xn_ref[...] = _rms(x_mid, gn_ref[...])
        ya_ref[...] = ya.astype(BF16)
        yb_ref[...] = yb.astype(BF16)

    row = lambda w: pl.BlockSpec((T, w), lambda i: (i, 0))
    full = lambda a: pl.BlockSpec(a.shape, lambda i: (0,) * a.ndim)
    return _call(
        body, name=name, grid=(S // T,), parallel=(0,), plan=plan,
        in_specs=[row(D), row(P), pl.BlockSpec((HALO_POOL, P), lambda i: (jnp.maximum(i * hb - 1, 0), 0)), row(P)]
        + _gate_specs(T, D) + [full(pool_w), full(pool_scale), full(b_gate), full(w_pa4), full(w_pb4), full(w_o4),
                               full(g_next)],
        out_specs=[row(D), row(D), row(D), row(D), row(P)],
        out_shape=[jax.ShapeDtypeStruct((S, D), F32), jax.ShapeDtypeStruct((S, D), BF16),
                   jax.ShapeDtypeStruct((S, D), BF16), jax.ShapeDtypeStruct((S, D), BF16),
                   jax.ShapeDtypeStruct((S, P), BF16)],
        args=(x, z, z, of, z, z, z, z, pool_w, pool_scale, b_gate, w_pa4, w_pb4, w_o4, g_next))


def _mix_bwd(dxm, ya, yb, z, b_gate, pm, of, pool_w, pool_scale, w_pa4, w_pb4, w_o4, *, name):
    S, D = dxm.shape
    P = pm.shape[1]
    q4 = D // N_CHIPS
    T = _row_tile(S, 512)
    nt = S // T
    hb = T // HALO_POOL

    def body(dx_ref, ya_ref, yb_ref, g0, g1, g2, g3, bg_ref, pm_ref, of_ref, u_ref, halo_ref, pw_ref, ps_ref,
             wpa_ref, wpb_ref, wo_ref,
             dzg_ref, dof_ref, du_ref, dwo_ref, dwpa_ref, dwpb_ref, dbg_ref, dpw_ref, dps_ref,
             dwo16_ref, dwpa16_ref, dwpb16_ref, next_scr):
        i = pl.program_id(0)

        @pl.when(i == 0)
        def _():
            for ref in (dwo_ref, dwpa_ref, dwpb_ref, dbg_ref, dpw_ref, dps_ref, next_scr):
                ref[...] = jnp.zeros_like(ref)

        dxb = dx_ref[...].astype(BF16)
        ya = ya_ref[...].astype(F32)
        yb = yb_ref[...].astype(F32)
        ga, gb = _gates((g0, g1, g2, g3), bg_ref, D)
        merged = (ga * ya + gb * yb).astype(BF16)
        dwo_ref[...] += _dot_tn(merged, dxb).reshape(N_CHIPS, q4, D)
        dm = _dot_nt(dxb, wo_ref[...].reshape(D, D))
        dza = dm * ya * ga * (1.0 - ga)
        dzb = dm * yb * gb * (1.0 - gb)
        dzg_ref[:, :D] = dza.astype(BF16)
        dzg_ref[:, D:] = dzb.astype(BF16)
        dbg_ref[:, :D] += jnp.sum(dza, axis=0, keepdims=True)
        dbg_ref[:, D:] += jnp.sum(dzb, axis=0, keepdims=True)
        dya = (dm * ga).astype(BF16)
        dyb = (dm * gb).astype(BF16)
        pmv, ofv = pm_ref[...], of_ref[...]
        dpm = jnp.zeros((T, P), F32)
        dof = jnp.zeros((T, P), F32)
        for k in range(N_CHIPS):
            cols = slice(k * q4, (k + 1) * q4)
            dwpa_ref[k] += _dot_tn(pmv, dya[:, cols])
            dwpb_ref[k] += _dot_tn(ofv, dyb[:, cols])
            dpm = dpm + _dot_nt(dya[:, cols], wpa_ref[k])
            dof = dof + _dot_nt(dyb[:, cols], wpb_ref[k])
        dof_ref[...] = dof
        _pool_grad(u_ref[...], halo_ref[...], i == nt - 1, (nt - 1 - i) * T, dpm, next_scr[...], pw_ref, ps_ref,
                   du_ref, dpw_ref, dps_ref)
        next_scr[...] = dpm[:HALO_POOL, :]

        @pl.when(i == nt - 1)
        def _():
            dwo16_ref[...] = dwo_ref[...].astype(BF16)
            dwpa16_ref[...] = dwpa_ref[...].astype(BF16)
            dwpb16_ref[...] = dwpb_ref[...].astype(BF16)

    rev = lambda i: nt - 1 - i
    row = lambda w: pl.BlockSpec((T, w), lambda i: (rev(i), 0))
    full = lambda a: pl.BlockSpec(a.shape, lambda i: (0,) * a.ndim, pipeline_mode=pl.Buffered(1))
    like = lambda a, dt: jax.ShapeDtypeStruct(a.shape, dt)
    return _call(
        body, name=name, grid=(nt,),
        in_specs=[row(D), row(D), row(D)] + _gate_specs(T, D, rev)
        + [full(b_gate), row(P), row(P), row(P),
           pl.BlockSpec((HALO_POOL, P), lambda i: (jnp.maximum(rev(i) * hb - 1, 0), 0)),
           full(pool_w), full(pool_scale), full(w_pa4), full(w_pb4), full(w_o4)],
        out_specs=[row(2 * D), row(P), row(P), full(w_o4), full(w_pa4), full(w_pb4), full(b_gate),
                   full(pool_w), full(pool_scale), full(w_o4), full(w_pa4), full(w_pb4)],
        out_shape=[jax.ShapeDtypeStruct((S, 2 * D), BF16), jax.ShapeDtypeStruct((S, P), F32),
                   jax.ShapeDtypeStruct((S, P), BF16), like(w_o4, F32), like(w_pa4, F32), like(w_pb4, F32),
                   like(b_gate, F32), like(pool_w, F32), like(pool_scale, F32),
                   like(w_o4, BF16), like(w_pa4, BF16), like(w_pb4, BF16)],
        scratch=[pltpu.VMEM((HALO_POOL, P), F32)],
        args=(dxm, ya, yb, z, z, z, z, b_gate, pm, of, z, z, pool_w, pool_scale, w_pa4, w_pb4, w_o4))[0]


def _up_conv(xn, w_up4, conv_w, conv_b, *, name, plan=None):
    S, D = xn.shape
    f4 = w_up4.shape[2]
    nf = N_CHIPS // 2
    F = nf * f4
    T = _row_tile(S, 512)

    def body(xn_ref, wv_ref, wg_ref, cwv_ref, cwg_ref, cbv_ref, cbg_ref,
             hv_ref, hg_ref, val_ref, gate_ref, a_ref, pv_scr, pg_scr):
        i = pl.program_id(1)
        xv = xn_ref[...]

        def side(w_ref, cw_ref, cb_ref, h_ref, p_scr):
            h = _dot(xv, w_ref[...])
            h_ref[...] = h.astype(BF16)
            hp = jnp.concatenate([jnp.where(i == 0, 0.0, p_scr[...]), h], axis=0)
            p_scr[...] = h[-SUBLANES:, :]
            cw = cw_ref[...]
            return cw[0:1, :] * _shift_down(hp, 2) + cw[1:2, :] * _shift_down(hp, 1) + cw[2:3, :] * h + cb_ref[...]

        val = side(wv_ref, cwv_ref, cbv_ref, hv_ref, pv_scr)
        gate = side(wg_ref, cwg_ref, cbg_ref, hg_ref, pg_scr)
        val16, gate16 = val.astype(BF16), gate.astype(BF16)
        val_ref[...] = val16
        gate_ref[...] = gate16
        a_ref[...] = gate16 * _sigmoid(gate16) * val16

    out = pl.BlockSpec((T, f4), lambda f, i: (i, f))
    return _call(
        body, name=name, grid=(nf, S // T), plan=plan,
        in_specs=[pl.BlockSpec((T, D), lambda f, i: (i, 0)),
                  pl.BlockSpec((None, D, f4), lambda f, i: (f, 0, 0)),
                  pl.BlockSpec((None, D, f4), lambda f, i: (nf + f, 0, 0)),
                  pl.BlockSpec((3, f4), lambda f, i: (0, f)),
                  pl.BlockSpec((3, f4), lambda f, i: (0, nf + f)),
                  pl.BlockSpec((1, f4), lambda f, i: (0, f)),
                  pl.BlockSpec((1, f4), lambda f, i: (0, nf + f))],
        out_specs=[out] * 5,
        out_shape=[jax.ShapeDtypeStruct((S, F), BF16)] * 5,
        scratch=[pltpu.VMEM((SUBLANES, f4), F32), pltpu.VMEM((SUBLANES, f4), F32)],
        args=(xn, w_up4, w_up4, conv_w, conv_w, conv_b, conv_b))


def _down(a, w_down4, x, g_next, *, name, plan=None):
    S, F = a.shape
    D = x.shape[1]
    T = _row_tile(S, 1024)

    def body(a_ref, wd_ref, x_ref, gn_ref, o_ref, xn_ref):
        x_out = x_ref[...] + _dot(a_ref[...], wd_ref[...].reshape(F, D))
        o_ref[...] = x_out
        xn_ref[...] = _rms(x_out, gn_ref[...])

    row = lambda w: pl.BlockSpec((T, w), lambda i: (i, 0))
    return _call(
        body, name=name, grid=(S // T,), parallel=(0,), plan=plan,
        in_specs=[row(F), pl.BlockSpec(w_down4.shape, lambda i: (0, 0, 0), pipeline_mode=pl.Buffered(1)),
                  row(D), pl.BlockSpec((1, D), lambda i: (0, 0))],
        out_specs=[row(D), row(D)],
        out_shape=[jax.ShapeDtypeStruct((S, D), F32), jax.ShapeDtypeStruct((S, D), BF16)],
        args=(a, w_down4, x, g_next))


def _ffn_down_bwd(dxo, hv, hg, val16, gate16, a16, conv_w, w_down4, *, name, plan=None):
    S = hv.shape[0]
    _, f4, D = w_down4.shape
    F = N_CHIPS * f4
    T = _row_tile(S, 512)
    nf = 2
    tf = 2 * f4
    nt = S // T

    def body(dx_ref, hv_ref, hg_ref, val_ref, gate_ref, a_ref, cwv_ref, cwg_ref, wd_ref,
             dhv_ref, dhg_ref, dwd_ref, dwd16_ref, dcwv_ref, dcwg_ref, dcbv_ref, dcbg_ref, cv_scr, cg_scr):
        i = pl.program_id(1)

        @pl.when(i == 0)
        def _():
            cv_scr[...] = jnp.zeros_like(cv_scr)
            cg_scr[...] = jnp.zeros_like(cg_scr)
            dwd_ref[...] = jnp.zeros_like(dwd_ref)
            dcwv_ref[...] = jnp.zeros_like(dcwv_ref)
            dcwg_ref[...] = jnp.zeros_like(dcwg_ref)
            dcbv_ref[...] = jnp.zeros_like(dcbv_ref)
            dcbg_ref[...] = jnp.zeros_like(dcbg_ref)

        dxb = dx_ref[...].astype(BF16)
        val = val_ref[...].astype(F32)
        gate = gate_ref[...].astype(F32)
        sg = _sigmoid(gate)
        sil = gate * sg
        dwd_ref[...] += _dot_tn(a_ref[...], dxb).reshape(2, f4, D)
        da = _dot_nt(dxb, wd_ref[...].reshape(tf, D))

        def conv_bwd(dhc, h0, cw, c_scr, dh_ref, dcw_ref, dcb_ref):
            ext = jnp.concatenate([dhc, c_scr[...]], axis=0)
            n1 = _shift_up(ext, 1)
            n2 = _shift_up(ext, 2)
            dh_ref[...] = (cw[2:3, :] * dhc + cw[1:2, :] * n1 + cw[0:1, :] * n2).astype(BF16)
            c_scr[...] = dhc[:SUBLANES, :]
            dcw_ref[0:1, :] += jnp.sum(n2 * h0, axis=0, keepdims=True)
            dcw_ref[1:2, :] += jnp.sum(n1 * h0, axis=0, keepdims=True)
            dcw_ref[2:3, :] += jnp.sum(dhc * h0, axis=0, keepdims=True)
            dcb_ref[...] += jnp.sum(dhc, axis=0, keepdims=True)

        conv_bwd(da * sil, hv_ref[...].astype(F32), cwv_ref[...], cv_scr, dhv_ref, dcwv_ref, dcbv_ref)
        conv_bwd(da * val * _dsilu(gate, sg), hg_ref[...].astype(F32), cwg_ref[...], cg_scr, dhg_ref, dcwg_ref,
                 dcbg_ref)

        @pl.when(i == nt - 1)
        def _():
            dwd16_ref[...] = dwd_ref[...].astype(BF16)

    rev = lambda i: nt - 1 - i
    wd_spec = pl.BlockSpec((2, f4, D), lambda f, i: (f, 0, 0))
    return _call(
        body, name=name, grid=(nf, nt), plan=plan,
        in_specs=[pl.BlockSpec((T, D), lambda f, i: (rev(i), 0)),
                  pl.BlockSpec((T, tf), lambda f, i: (rev(i), f)),
                  pl.BlockSpec((T, tf), lambda f, i: (rev(i), f)),
                  pl.BlockSpec((T, tf), lambda f, i: (rev(i), f)),
                  pl.BlockSpec((T, tf), lambda f, i: (rev(i), f)),
                  pl.BlockSpec((T, tf), lambda f, i: (rev(i), f)),
                  pl.BlockSpec((3, tf), lambda f, i: (0, f)),
                  pl.BlockSpec((3, tf), lambda f, i: (0, nf + f)),
                  wd_spec],
        out_specs=[pl.BlockSpec((T, tf), lambda f, i: (rev(i), f)),
                   pl.BlockSpec((T, tf), lambda f, i: (rev(i), f)),
                   wd_spec, wd_spec,
                   pl.BlockSpec((3, tf), lambda f, i: (0, f)),
                   pl.BlockSpec((3, tf), lambda f, i: (0, f)),
                   pl.BlockSpec((1, tf), lambda f, i: (0, f)),
                   pl.BlockSpec((1, tf), lambda f, i: (0, f))],
        out_shape=[jax.ShapeDtypeStruct((S, F), BF16), jax.ShapeDtypeStruct((S, F), BF16),
                   jax.ShapeDtypeStruct((N_CHIPS, f4, D), F32), jax.ShapeDtypeStruct((N_CHIPS, f4, D), BF16),
                   jax.ShapeDtypeStruct((3, F), F32), jax.ShapeDtypeStruct((3, F), F32),
                   jax.ShapeDtypeStruct((1, F), F32), jax.ShapeDtypeStruct((1, F), F32)],
        scratch=[pltpu.VMEM((SUBLANES, tf), F32), pltpu.VMEM((SUBLANES, tf), F32)],
        args=(dxo, hv, hg, val16, gate16, a16, conv_w, conv_w, w_down4))


def _down_loss(a, w_down4, x, g, target, *, name):
    S, F = a.shape
    D = x.shape[1]
    T = _row_tile(S, 512)

    def body(a_ref, wd_ref, x_ref, g_ref, t_ref, loss_ref, dx_ref, dg_ref):
        @pl.when(pl.program_id(0) == 0)
        def _():
            loss_ref[...] = jnp.zeros_like(loss_ref)
            dg_ref[...] = jnp.zeros_like(dg_ref)

        xf = x_ref[...] + _dot(a_ref[...], wd_ref[...].reshape(F, D))
        r = lax.rsqrt(jnp.mean(xf * xf, axis=-1, keepdims=True) + EPS)
        xhat = xf * r
        err = xhat * g_ref[...] - t_ref[...]
        loss_ref[...] += jnp.sum(err * err, axis=0, keepdims=True) * (0.5 / D)
        dy = err * (1.0 / D)
        dxhat = dy * g_ref[...]
        dx_ref[...] = r * (dxhat - xhat * jnp.mean(dxhat * xhat, axis=-1, keepdims=True))
        dg_ref[...] += jnp.sum(dy * xhat, axis=0, keepdims=True)

    row = lambda w: pl.BlockSpec((T, w), lambda i: (i, 0))
    vec = pl.BlockSpec((1, D), lambda i: (0, 0))
    return _call(
        body, name=name, grid=(S // T,),
        in_specs=[row(F), pl.BlockSpec(w_down4.shape, lambda i: (0, 0, 0), pipeline_mode=pl.Buffered(1)),
                  row(D), vec, row(D)],
        out_specs=[vec, row(D), vec],
        out_shape=[jax.ShapeDtypeStruct((1, D), F32), jax.ShapeDtypeStruct((S, D), F32),
                   jax.ShapeDtypeStruct((1, D), F32)],
        args=(a, w_down4, x, g, target))[0]


BIG = ("w_in", "w_pa", "w_pb", "w_o", "w_up", "w_down")
SMALL = ("norm1_g", "b_gate", "pool_w", "pool_scale", "lb_logits", "hgrn_norm_g", "norm2_g", "conv_b", "final_g")
WEIGHTS = ("norm1_g", "w_in", "b_gate", "pool_w", "pool_scale", "lb_logits", "hgrn_norm_g", "w_pa", "w_pb", "w_o",
           "norm2_g", "w_up", "conv_w", "conv_b", "w_down", "final_g")


def _lower_bounds(lb_logits):
    soft = jax.nn.softmax(lb_logits.astype(F32), axis=0)
    cum = jnp.cumsum(soft, axis=0)
    return cum - cum[0:1]


def _step(x, target, sm, wts, shards=None):
    L = sm["norm1_g"].shape[0]
    wts = dict(wts)
    dist = shards is not None
    lbs, lb_vjp = jax.vjp(_lower_bounds, sm["lb_logits"])
    row = lambda a: a.reshape(1, -1)
    conv_w = sm.get("conv_w")

    def gather(names_layers, with_conv=False):
        items = [(shards[n], "rows", l) for n, l in names_layers]
        if with_conv:
            items.append((shards["conv_w"], "layer", None))
        return _GatherPlan(items)

    def landed(names_layers, outs):
        for key, arr in zip(names_layers, outs):
            wts[key] = arr

    own = {"in_proj": ("w_up",)}
    first = {"hgrn_fwd": ("w_pa", "w_pb", "w_o"), "mix_fwd": ("w_down",)}
    ahead = {"up": ("w_in", "w_pa", "w_pb", "w_o", "w_down")}
    conv_rider = "mix_fwd"

    def riders(l, kernel):
        if not dist:
            return [], None
        keys = [(n, l) for n in own.get(kernel, ())]
        keys += [(n, l) for n in first.get(kernel, ())] if l == 0 else []
        keys += [(n, l + 1) for n in ahead.get(kernel, ())] if l + 1 < L else []
        with_conv = l == 0 and kernel == conv_rider
        return keys, (gather(keys, with_conv) if keys or with_conv else None)

    if dist:
        chip = (2 * lax.axis_index("x") + lax.axis_index("y")).astype(jnp.int32).reshape(1)
        keys = [("w_in", 0)]
        (xn1, z_own), got = _norm_own_proj(x, row(sm["norm1_g"][0]), shards["w_in"][0], chip, name="norm_in",
                                           plan=gather(keys))
        landed(keys, got)
    else:
        (xn1,), _ = _rmsnorm(x, row(sm["norm1_g"][0]), name="norm_in")

    saved = []
    for l in range(L):
        keys, plan = riders(l, "in_proj")
        if dist and l == 0:
            (z,), got = _matmul_rest(xn1, wts[("w_in", l)], z_own, chip, name=f"in_proj_{l}", plan=plan)
        else:
            (z,), got = _matmul(xn1, wts[("w_in", l)], name=f"in_proj_{l}", plan=plan)
        landed(keys, got)
        keys, plan = riders(l, "hgrn_fwd")
        (o_raw, of, states), got = _hgrn_fwd(z, row(lbs[l]), row(sm["hgrn_norm_g"][l]), name=f"hgrn_fwd_{l}",
                                             plan=plan)
        landed(keys, got)
        keys, plan = riders(l, "mix_fwd")
        (x_mid, ya, yb, xn2, pm), got = _mix_fwd(
            x, of, z, sm["pool_w"][l], row(sm["pool_scale"][l]), row(sm["b_gate"][l]), wts[("w_pa", l)],
            wts[("w_pb", l)], wts[("w_o", l)], row(sm["norm2_g"][l]), name=f"mix_fwd_{l}", plan=plan)
        landed(keys, got)
        if dist and l == 0:
            full = got[-1]
            conv_w = jnp.concatenate([full[:, k] for k in range(N_CHIPS)], axis=2)
        keys, plan = riders(l, "up")
        (hv, hg, val16, gate16, a16), got = _up_conv(xn2, wts[("w_up", l)], conv_w[l], row(sm["conv_b"][l]),
                                                     name=f"up_{l}", plan=plan)
        landed(keys, got)
        saved.append(dict(x=x, xn1=xn1, z=z, pm=pm, o_raw=o_raw, of=of, states=states,
                          x_mid=x_mid, ya=ya, yb=yb, xn2=xn2, hv=hv, hg=hg, val16=val16, gate16=gate16, a16=a16))
        if l + 1 < L:
            keys, plan = riders(l, "down")
            (x, xn1), got = _down(a16, wts[("w_down", l)], x_mid, row(sm["norm1_g"][l + 1]), name=f"down_{l}",
                                  plan=plan)
            landed(keys, got)
        else:
            loss_cols, dx, d_final_g = _down_loss(a16, wts[("w_down", l)], x_mid, row(sm["final_g"]), target,
                                                  name="down_loss")

    small = {k: [None] * L for k in ("norm1_g", "b_gate", "pool_w", "pool_scale", "hgrn_norm_g", "norm2_g",
                                     "conv_w", "conv_b")}
    big32, big16, recv = {}, {}, {}
    dlbs = [None] * L
    pending = []

    def scatter():
        if not (dist and pending):
            return [], None
        keys = list(pending)
        del pending[:]
        return keys, _ScatterPlan([big16[k] for k in keys])

    def sent(keys, outs):
        for key, arr in zip(keys, outs):
            recv[key] = arr

    def made(name, l, g32, g16):
        big32[(name, l)], big16[(name, l)] = g32, g16
        pending.append((name, l))

    for l in reversed(range(L)):
        s = saved[l]
        keys, plan = scatter()
        (dhv, dhg, d_wd, d_wd16, dcwv, dcwg, dcbv, dcbg), got = _ffn_down_bwd(
            dx, s["hv"], s["hg"], s["val16"], s["gate16"], s["a16"], conv_w[l], wts[("w_down", l)], name=f"down_bwd_{l}",
            plan=plan)
        sent(keys, got)
        made("w_down", l, d_wd, d_wd16)
        small["conv_w"][l] = jnp.concatenate([dcwv, dcwg], axis=1)
        small["conv_b"][l] = jnp.concatenate([dcbv, dcbg], axis=1)[0]
        keys, plan = scatter()
        (d_wu, d_wu16), got = _wgrad(s["xn2"], [dhv, dhg], name=f"up_wgrad_{l}", rows=2048, plan=plan)
        sent(keys, got)
        made("w_up", l, d_wu, d_wu16)
        (dxm, dg2), _ = _dgrad_norm([dhv, dhg], wts[("w_up", l)], s["x_mid"], row(sm["norm2_g"][l]), dx,
                                    name=f"up_dgrad_{l}")
        small["norm2_g"][l] = dg2[0]

        dzg, dof, du, d_wo, d_wpa, d_wpb, dbg, dpw, dps, d_wo16, d_wpa16, d_wpb16 = _mix_bwd(
            dxm, s["ya"], s["yb"], s["z"], row(sm["b_gate"][l]), s["pm"], s["of"], sm["pool_w"][l],
            row(sm["pool_scale"][l]), wts[("w_pa", l)], wts[("w_pb", l)], wts[("w_o", l)], name=f"mix_bwd_{l}")
        small["pool_w"][l], small["pool_scale"][l] = dpw, dps[0]
        made("w_o", l, d_wo, d_wo16)
        made("w_pa", l, d_wpa, d_wpa16)
        made("w_pb", l, d_wpb, d_wpb16)
        small["b_gate"][l] = dbg[0]

        keys, plan = scatter()
        (dzq, dzf, dzi, dzo, dlb, dng), got = _hgrn_bwd(s["z"], row(lbs[l]), row(sm["hgrn_norm_g"][l]), s["o_raw"],
                                                      s["states"], dof, name=f"hgrn_bwd_{l}", plan=plan)
        sent(keys, got)
        dlbs[l] = dlb[0]
        small["hgrn_norm_g"][l] = jnp.sum(dng.reshape(-1, LANES), axis=0)

        dz = [du, dzq, dzf, dzi, dzo, dzg]
        (d_wi, d_wi16), _ = _wgrad(s["xn1"], dz, name=f"in_wgrad_{l}", rows=1024)
        made("w_in", l, d_wi, d_wi16)
        keys, plan = scatter()
        (dx, dg1), got = _dgrad_norm(dz, wts[("w_in", l)], s["x"], row(sm["norm1_g"][l]), dxm,
                                     name=f"in_dgrad_{l}", plan=plan)
        sent(keys, got)
        small["norm1_g"][l] = dg1[0]

    out = {k: jnp.stack(v) for k, v in small.items()}
    out["lb_logits"] = lb_vjp(jnp.stack(dlbs))[0]
    out["final_g"] = d_final_g[0]
    return loss_cols, dx, out, big32, recv


def _elementwise_rows(R, n, n_arrays):
    if 2 * n_arrays * R * n * 4 <= VMEM_LIMIT // 4 or R % 8:
        return R
    block = VMEM_LIMIT // 2 // (2 * n_arrays)
    want = 8
    while want * 2 * n * 4 <= block:
        want *= 2
    return _row_tile(R, want)


def _sum_layers(own, got, chip, *, name):
    L = len(own)
    _, r, n = own[0].shape
    T = _elementwise_rows(r, n, 6)
    nt = r // T

    def body(chip_ref, *refs):
        o_ref = refs[-1]
        l = pl.program_id(0)
        for k in range(L):
            @pl.when(l == k)
            def _():
                own_ref, got_ref = refs[2 * k], refs[2 * k + 1]
                acc = own_ref[...]
                for j in range(3):
                    acc = acc + got_ref[j].astype(F32)
                o_ref[...] = acc.astype(BF16)

    in_specs = []
    for k in range(L):
        hold = 0 if k else nt - 1
        in_specs.append(pl.BlockSpec((None, T, n), lambda l, i, c, k=k, hold=hold: (c[0], jnp.where(l == k, i, hold), 0)))
        in_specs.append(pl.BlockSpec((3, T, n), lambda l, i, c, k=k, hold=hold: (0, jnp.where(l == k, i, hold), 0)))
    grid_spec = pltpu.PrefetchScalarGridSpec(
        num_scalar_prefetch=1, grid=(L, nt), in_specs=in_specs,
        out_specs=pl.BlockSpec((None, T, n), lambda l, i, c: (l, i, 0)))
    args = [a for pair in zip(own, got) for a in pair]
    return pl.pallas_call(
        body, name=name, grid_spec=grid_spec, out_shape=jax.ShapeDtypeStruct((L, r, n), BF16),
        compiler_params=pltpu.CompilerParams(dimension_semantics=("arbitrary", "arbitrary"),
                                             vmem_limit_bytes=VMEM_LIMIT),
    )(chip, *args)


def _sum_stack(parts, *, name):
    K, R, n = parts.shape
    T = _elementwise_rows(R, n, K + 1)

    def body(p_ref, o_ref):
        acc = p_ref[0]
        for j in range(1, K):
            acc = acc + p_ref[j]
        o_ref[...] = acc

    return _call(
        body, name=name, grid=(R // T,), parallel=(0,),
        in_specs=[pl.BlockSpec((K, T, n), lambda i: (0, i, 0))],
        out_specs=[pl.BlockSpec((T, n), lambda i: (i, 0))],
        out_shape=[jax.ShapeDtypeStruct((R, n), F32)],
        args=(parts,))[0][0]


def _adamw(w, m, v, g_parts, *, name):
    R, n = w.shape
    n_g = len(g_parts)
    T = _elementwise_rows(R, n, 7 + n_g)

    def body(*refs):
        w_ref, m_ref, v_ref = refs[:3]
        g_refs = refs[3:3 + n_g]
        go_ref, d_ref, mo_ref, vo_ref = refs[3 + n_g:]
        g_ = g_refs[0][...].astype(F32)
        for r in g_refs[1:]:
            g_ = g_ + r[...].astype(F32)
        m_ = ADAM_B1 * m_ref[...] + (1.0 - ADAM_B1) * g_
        v_ = ADAM_B2 * v_ref[...] + (1.0 - ADAM_B2) * (g_ * g_)
        m_hat = m_ / (1.0 - ADAM_B1 ** ADAM_STEP)
        v_hat = v_ / (1.0 - ADAM_B2 ** ADAM_STEP)
        go_ref[...] = g_
        d_ref[...] = -ADAM_LR * (m_hat / (jnp.sqrt(v_hat) + ADAM_EPS) + ADAM_WD * w_ref[...])
        mo_ref[...] = m_
        vo_ref[...] = v_

    blk = pl.BlockSpec((T, n), lambda i: (i, 0))
    return _call(
        body, name=name, grid=(R // T,), parallel=(0,),
        in_specs=[blk] * (3 + n_g), out_specs=[blk] * 4,
        out_shape=[jax.ShapeDtypeStruct((R, n), F32)] * 4,
        args=(w, m, v, *g_parts))[0]


PACK_ALIGN = 8 * LANES


def _pack(pieces):
    flat = []
    for a in pieces:
        a = a.reshape(-1)
        pad = (-a.shape[0]) % PACK_ALIGN
        flat.append(jnp.pad(a, (0, pad)) if pad else a)
    return jnp.concatenate(flat).reshape(-1, LANES)


def _unpack(buf, shapes):
    flat = buf.reshape(-1)
    out, off = [], 0
    for shp in shapes:
        size = 1
        for s in shp:
            size *= s
        out.append(flat[off:off + size].reshape(shp))
        off += size + (-size) % PACK_ALIGN
    return out


def kernel(x, norm1_g, w_in, b_gate, pool_w, pool_scale, lb_logits, hgrn_norm_g, w_pa, w_pb, w_o, norm2_g, w_up, conv_w, conv_b, w_down, final_g, loss_target, m_norm1_g, m_w_in, m_b_gate, m_pool_w, m_pool_scale, m_lb_logits, m_hgrn_norm_g, m_w_pa, m_w_pb, m_w_o, m_norm2_g, m_w_up, m_conv_w, m_conv_b, m_w_down, m_final_g, v_norm1_g, v_w_in, v_b_gate, v_pool_w, v_pool_scale, v_lb_logits, v_hgrn_norm_g, v_w_pa, v_w_pb, v_w_o, v_norm2_g, v_w_up, v_conv_w, v_conv_b, v_w_down, v_final_g):
    env = dict(locals())
    w = {n: env[n] for n in WEIGHTS}
    m = {n: env["m_" + n] for n in WEIGHTS}
    v = {n: env["v_" + n] for n in WEIGHTS}
    my_chip = 2 * lax.axis_index("x") + lax.axis_index("y")
    L = w_in.shape[0]

    shards = {n: w[n].astype(BF16) for n in BIG}
    shards["conv_w"] = w["conv_w"]
    sm = {n: w[n] for n in SMALL}
    loss_cols, grad_x, g_small, big32, recv = _step(x[0], loss_target[0], sm, {}, shards)

    chip = my_chip.reshape(1).astype(jnp.int32)
    sums = [_sum_layers([big32[(n, l)] for l in range(L)], [recv[(n, l)] for l in range(L)], chip,
                        name="chip_sum_" + n) for n in BIG]
    small_names = list(SMALL)
    small_pieces = [g_small[n] for n in small_names] + [g_small["conv_w"], loss_cols]
    small_shapes = [a.shape for a in small_pieces]
    packed = _pack(small_pieces)
    Rs = packed.shape[0]
    swapped = _run_plan(_Together([_SiblingPlan(sums), _EveryonePlan(packed)]), name="tail_exchange")
    theirs, everyone = swapped[:-1], swapped[-1].reshape(8, Rs, LANES)
    g, delta, new_m, new_v = {}, {}, {}, {}
    for n, mine, other in zip(BIG, sums, theirs):
        shp = w[n].shape
        two_d = lambda a: a.reshape(-1, shp[-1])
        outs = _adamw(two_d(w[n]), two_d(m[n]), two_d(v[n]), [two_d(mine), two_d(other)], name="adamw_" + n)
        g[n], delta[n], new_m[n], new_v[n] = [a.reshape(shp) for a in outs]

    summed = _unpack(_sum_stack(everyone, name="small_sum"), small_shapes)
    loss = jnp.sum(summed[-1])
    cshard = w["conv_w"].shape[2]
    gs = dict(zip(small_names, summed[:len(small_names)]))
    g_cw = lax.dynamic_slice_in_dim(summed[-2], my_chip * cshard, cshard, axis=2)

    sm_out = _adamw(_pack([w[n] for n in small_names]), _pack([m[n] for n in small_names]),
                    _pack([v[n] for n in small_names]), [_pack([gs[n] for n in small_names])], name="adamw_small")
    shapes = [w[n].shape for n in small_names]
    for n, g_, d_, m_, v_ in zip(small_names, *[_unpack(a, shapes) for a in sm_out]):
        g[n], delta[n], new_m[n], new_v[n] = g_, d_, m_, v_
    shp = w["conv_w"].shape
    two_d = lambda a: a.reshape(-1, shp[-1])
    outs = _adamw(two_d(w["conv_w"]), two_d(m["conv_w"]), two_d(v["conv_w"]), [two_d(g_cw)], name="adamw_conv_w")
    g["conv_w"], delta["conv_w"], new_m["conv_w"], new_v["conv_w"] = [a.reshape(shp) for a in outs]

    return (loss, grad_x[None], *[g[n] for n in WEIGHTS], *[delta[n] for n in WEIGHTS],
            *[new_m[n] for n in WEIGHTS], *[new_v[n] for n in WEIGHTS])
```
